```python
import math
import jax, jax.numpy as jnp
from jax import lax
import numpy as np

D_MODEL = 1024
BATCH = 2
SEQ = 8192
DEPTH = 2

HEAD_DIM = 64
SWEEP_BLOCK = 128
RMS_EPS = 1e-6
NEG_INF = -1e30

MLA_HEADS = 8
MLA_Q_RANK = 256
MLA_KV_RANK = 128
MLA_NOPE = 64
MLA_ROPE = 32
MLA_V = 64
ROPE_THETA = 10000.0

SWA_HEADS = 8
SWA_KV_HEADS = 2
SWA_WINDOW = 128

MOBA_HEADS = 8
MOBA_BLOCK = 256
MOBA_TOPK = 3

SB_HEADS = 8

REL_BUCKETS = 32
REL_MAX_DIST = 2048
REL_HEADS = 8

N_GROUPS = 4
EXPERTS_PER_GROUP = 8
N_EXPERTS = N_GROUPS * EXPERTS_PER_GROUP
EXPERT_TOPK = 2
D_EXPERT = 256
MOE_CHUNK = 128

N_EVEN = (DEPTH + 1) // 2
N_ODD = DEPTH // 2

SWA_Q_W = SWA_HEADS * HEAD_DIM
SWA_KV_W = SWA_KV_HEADS * HEAD_DIM
IN_EVEN = MLA_Q_RANK + MLA_KV_RANK + MLA_ROPE + SWA_Q_W + 2 * SWA_KV_W
SPLIT_EVEN = (MLA_Q_RANK, MLA_Q_RANK + MLA_KV_RANK, MLA_Q_RANK + MLA_KV_RANK + MLA_ROPE,
              MLA_Q_RANK + MLA_KV_RANK + MLA_ROPE + SWA_Q_W,
              MLA_Q_RANK + MLA_KV_RANK + MLA_ROPE + SWA_Q_W + SWA_KV_W)
MIX_EVEN = MLA_HEADS * MLA_V + SWA_HEADS * HEAD_DIM
IN_ODD = 3 * (MOBA_HEADS + SB_HEADS) * HEAD_DIM
MIX_ODD = (MOBA_HEADS + SB_HEADS) * HEAD_DIM

kernel_name = "hybrid_mla_swa_moba_stickbreak_hmoe"


def rms_norm(x, g):
    xf = x.astype(jnp.float32)
    y = xf * lax.rsqrt(jnp.mean(xf * xf, axis=-1, keepdims=True) + RMS_EPS)
    return (y * g.astype(jnp.float32)).astype(x.dtype)


def rope(x):
    T, R = x.shape[1], x.shape[-1]
    freqs = ROPE_THETA ** (-jnp.arange(0, R, 2, dtype=jnp.float32) / R)
    ang = jnp.arange(T, dtype=jnp.float32)[:, None] * freqs[None, :]
    shape = (T,) + (1,) * (x.ndim - 3) + (R // 2,)
    cos, sin = jnp.cos(ang).reshape(shape), jnp.sin(ang).reshape(shape)
    xf = x.astype(jnp.float32)
    x1, x2 = xf[..., : R // 2], xf[..., R // 2:]
    return jnp.concatenate([x1 * cos - x2 * sin, x2 * cos + x1 * sin], axis=-1).astype(x.dtype)


def t5_bucket(dist):
    n = jnp.maximum(dist, 0)
    max_exact = REL_BUCKETS // 2
    nf = jnp.maximum(n, 1).astype(jnp.float32)
    large = max_exact + (jnp.log(nf / max_exact) / math.log(REL_MAX_DIST / max_exact)
                         * (REL_BUCKETS - max_exact)).astype(jnp.int32)
    large = jnp.minimum(large, REL_BUCKETS - 1)
    return jnp.where(n < max_exact, n, large)


def rel_bias(dist, table):
    return jnp.moveaxis(table[t5_bucket(dist)].astype(jnp.float32), -1, 0)


def mla_attention(q_nope, q_rope, k_nope, k_rope, v):
    B_, T, H, _ = q_nope.shape
    scale = (MLA_NOPE + MLA_ROPE) ** -0.5
    kpos = jnp.arange(T)

    def block(i):
        start = i * SWEEP_BLOCK
        qn = lax.dynamic_slice_in_dim(q_nope, start, SWEEP_BLOCK, axis=1)
        qr = lax.dynamic_slice_in_dim(q_rope, start, SWEEP_BLOCK, axis=1)
        s = (jnp.einsum('bqhd,bkhd->bhqk', qn, k_nope)
             + jnp.einsum('bqhr,bkr->bhqk', qr, k_rope)).astype(jnp.float32) * scale
        qpos = start + jnp.arange(SWEEP_BLOCK)
        s = jnp.where(kpos[None, :] <= qpos[:, None], s, NEG_INF)
        p = jax.nn.softmax(s, axis=-1).astype(v.dtype)
        return jnp.einsum('bhqk,bkhd->bqhd', p, v)

    out = lax.map(block, jnp.arange(T // SWEEP_BLOCK))
    return out.transpose(1, 0, 2, 3, 4).reshape(B_, T, H, -1)


def swa_attention(q, k, v, sinks, rel_table):
    B_, T, Hq, D = q.shape
    Hkv = k.shape[2]
    G = Hq // Hkv
    nb = T // SWEEP_BLOCK
    Q = SWEEP_BLOCK
    qb = q.reshape(B_, nb, Q, Hkv, G, D)
    pad = ((0, 0), (Q, 0), (0, 0), (0, 0))
    kb = jnp.pad(k, pad).reshape(B_, nb + 1, Q, Hkv, D)
    vb = jnp.pad(v, pad).reshape(B_, nb + 1, Q, Hkv, D)
    kw = jnp.concatenate([kb[:, :-1], kb[:, 1:]], axis=2)
    vw = jnp.concatenate([vb[:, :-1], vb[:, 1:]], axis=2)
    s = jnp.einsum('bnqhgd,bnkhd->bhgnqk', qb, kw).astype(jnp.float32) * D ** -0.5
    qpos = jnp.arange(nb)[:, None] * Q + jnp.arange(Q)[None, :]
    kpos = (jnp.arange(nb)[:, None] - 1) * Q + jnp.arange(2 * Q)[None, :]
    dist = qpos[:, :, None] - kpos[:, None, :]
    valid = (dist >= 0) & (dist < SWA_WINDOW) & (kpos[:, None, :] >= 0)
    s = s + rel_bias(dist, rel_table).reshape(Hkv, G, nb, Q, 2 * Q)
    s = jnp.where(valid, s, NEG_INF)
    sink = jnp.broadcast_to(sinks.astype(jnp.float32).reshape(1, Hkv, G, 1, 1, 1), s.shape[:-1] + (1,))
    p = jax.nn.softmax(jnp.concatenate([s, sink], axis=-1), axis=-1)[..., :-1]
    out = jnp.einsum('bhgnqk,bnkhd->bnqhgd', p.astype(v.dtype), vw)
    return out.reshape(B_, T, Hq, D)


def moba_attention(q, k, v, rel_table):
    B_, T, H, D = q.shape
    L = MOBA_BLOCK
    nkb = -(-T // L)
    Tp = nkb * L
    topk = min(MOBA_TOPK, nkb)
    scale = D ** -0.5
    qh = q.transpose(0, 2, 1, 3)
    kp = jnp.pad(k.transpose(0, 2, 1, 3), ((0, 0), (0, 0), (0, Tp - T), (0, 0)))
    vp = jnp.pad(v.transpose(0, 2, 1, 3), ((0, 0), (0, 0), (0, Tp - T), (0, 0)))
    kblk = kp.reshape(B_, H, nkb, L, D)
    vblk = vp.reshape(B_, H, nkb, L, D)
    kmean = jnp.mean(kblk.astype(jnp.float32), axis=3)
    table_h = rel_table.T
    bi = jnp.arange(B_)[:, None, None, None]
    hi = jnp.arange(H)[None, :, None, None]
    Q = SWEEP_BLOCK

    def block(i):
        start = i * Q
        own = start // L
        qc = lax.dynamic_slice_in_dim(qh, start, Q, axis=2)
        qpos = start + jnp.arange(Q)
        gsc = jnp.einsum('bhqd,bhnd->bhqn', qc.astype(jnp.float32), kmean)
        gsc = jnp.where(jnp.arange(nkb) < own, gsc, NEG_INF)
        _, sel = lax.top_k(gsc, topk)
        sel_valid = sel < own
        ks = kblk[bi, hi, sel]
        vs = vblk[bi, hi, sel]
        s_sel = jnp.einsum('bhqd,bhqkld->bhqkl', qc, ks).astype(jnp.float32) * scale
        kpos_sel = sel[..., None] * L + jnp.arange(L)
        dist_sel = qpos[:, None, None] - kpos_sel
        s_sel = s_sel + table_h[hi[..., None], t5_bucket(dist_sel)].astype(jnp.float32)
        s_sel = jnp.where(sel_valid[..., None], s_sel, NEG_INF).reshape(B_, H, Q, topk * L)
        kown = lax.dynamic_slice_in_dim(kp, own * L, L, axis=2)
        vown = lax.dynamic_slice_in_dim(vp, own * L, L, axis=2)
        s_own = jnp.einsum('bhqd,bhld->bhql', qc, kown).astype(jnp.float32) * scale
        dist_own = qpos[:, None] - (own * L + jnp.arange(L))[None, :]
        s_own = s_own + rel_bias(dist_own, rel_table)[None]
        s_own = jnp.where(dist_own >= 0, s_own, NEG_INF)
        p = jax.nn.softmax(jnp.concatenate([s_sel, s_own], axis=-1), axis=-1)
        p_sel = p[..., : topk * L].reshape(B_, H, Q, topk, L).astype(v.dtype)
        p_own = p[..., topk * L:].astype(v.dtype)
        return (jnp.einsum('bhqkl,bhqkld->bhqd', p_sel, vs)
                + jnp.einsum('bhql,bhld->bhqd', p_own, vown))

    out = lax.map(block, jnp.arange(T // Q))
    return out.transpose(1, 0, 3, 2, 4).reshape(B_, T, H, D)


def stick_breaking_attention(q, k, v):
    B_, T, H, D = q.shape
    scale = D ** -0.5
    kpos = jnp.arange(T)

    def block(i):
        start = i * SWEEP_BLOCK
        qc = lax.dynamic_slice_in_dim(q, start, SWEEP_BLOCK, axis=1)
        z = jnp.einsum('bqhd,bkhd->bhqk', qc, k).astype(jnp.float32) * scale
        qpos = start + jnp.arange(SWEEP_BLOCK)
        past = kpos[None, :] < qpos[:, None]
        log_1m = jnp.where(past, jax.nn.log_sigmoid(-z), 0.0)
        after = lax.cumsum(log_1m, axis=3, reverse=True) - log_1m
        w = jnp.where(past, jnp.exp(jax.nn.log_sigmoid(z) + after), 0.0)
        return jnp.einsum('bhqk,bkhd->bqhd', w.astype(v.dtype), v)

    out = lax.map(block, jnp.arange(T // SWEEP_BLOCK))
    return out.transpose(1, 0, 2, 3, 4).reshape(B_, T, H, D)


def even_mixer(u, w_in, g_q, g_kv, w_qb, w_kvb, sinks, w_out, rel_table):
    B_, T, _ = u.shape
    cq, ckv, kr, qs, ks, vs = jnp.split(u @ w_in, SPLIT_EVEN, axis=-1)
    q = (rms_norm(cq, g_q) @ w_qb).reshape(B_, T, MLA_HEADS, MLA_NOPE + MLA_ROPE)
    kv = (rms_norm(ckv, g_kv) @ w_kvb).reshape(B_, T, MLA_HEADS, MLA_NOPE + MLA_V)
    o_a = mla_attention(q[..., :MLA_NOPE], rope(q[..., MLA_NOPE:]),
                        kv[..., :MLA_NOPE], rope(kr), kv[..., MLA_NOPE:]).reshape(B_, T, -1)
    o_b = swa_attention(qs.reshape(B_, T, SWA_HEADS, HEAD_DIM),
                        ks.reshape(B_, T, SWA_KV_HEADS, HEAD_DIM),
                        vs.reshape(B_, T, SWA_KV_HEADS, HEAD_DIM), sinks, rel_table).reshape(B_, T, -1)
    return jnp.concatenate([o_a, o_b], axis=-1) @ w_out


def odd_mixer(u, w_in, w_out, rel_table):
    B_, T, _ = u.shape
    parts = [p.reshape(B_, T, -1, HEAD_DIM) for p in jnp.split(u @ w_in, 6, axis=-1)]
    qc, kc, vc, qd, kd, vd = parts
    o_c = moba_attention(qc, kc, vc, rel_table).reshape(B_, T, -1)
    o_d = stick_breaking_attention(qd, kd, vd).reshape(B_, T, -1)
    return jnp.concatenate([o_c, o_d], axis=-1) @ w_out


def hier_moe(u, w_group, b_group, w_router, b_router, w_gate, w_up, w_down):
    B_, T, D = u.shape
    uf = u.reshape(-1, D)
    N = uf.shape[0]
    glog = (uf @ w_group).astype(jnp.float32) + b_group.astype(jnp.float32)
    gprob = jax.nn.softmax(glog, axis=-1)
    g_sel = jnp.argmax(glog, axis=-1)
    g_gate = jnp.take_along_axis(gprob, g_sel[:, None], axis=-1)
    elog = ((uf @ w_router).astype(jnp.float32) + b_router.astype(jnp.float32)).reshape(N, N_GROUPS, EXPERTS_PER_GROUP)
    elog_g = jnp.take_along_axis(elog, g_sel[:, None, None], axis=1)[:, 0]
    top_p, top_i = lax.top_k(jax.nn.softmax(elog_g, axis=-1), EXPERT_TOPK)
    top_p = top_p / jnp.sum(top_p, axis=-1, keepdims=True)
    expert_id = g_sel[:, None] * EXPERTS_PER_GROUP + top_i
    combine = jnp.sum(jax.nn.one_hot(expert_id, N_EXPERTS, dtype=jnp.float32)
                      * (g_gate * top_p)[..., None], axis=1)

    def chunk(args):
        xc, cc = args
        h = jax.nn.silu(jnp.einsum('nd,edf->nef', xc, w_gate)) * jnp.einsum('nd,edf->nef', xc, w_up)
        h = h * cc[..., None].astype(h.dtype)
        return jnp.einsum('nef,efd->nd', h, w_down)

    y = lax.map(chunk, (uf.reshape(-1, MOE_CHUNK, D), combine.reshape(-1, MOE_CHUNK, N_EXPERTS)))
    return y.reshape(B_, T, D)


def setup_inputs(seed: int = 0) -> dict:
    key = jax.random.key(seed)
    k = jax.random.split(key, 21)
    f32 = jnp.float32
    nrm = lambda kk, shape, fan_in: jax.random.normal(kk, shape, f32) * fan_in ** -0.5
    gain = lambda kk, shape: 1.0 + 0.05 * jax.random.normal(kk, shape, f32)
    return {
        "x": jax.random.normal(k[0], (BATCH, SEQ, D_MODEL), f32),
        "norm_mix": gain(k[1], (DEPTH, D_MODEL)),
        "norm_ffn": gain(k[2], (DEPTH, D_MODEL)),
        "norm_final": gain(k[3], (D_MODEL,)),
        "rel_bias_table": 0.3 * jax.random.normal(k[4], (REL_BUCKETS, REL_HEADS), f32),
        "w_in_even": nrm(k[5], (N_EVEN, D_MODEL, IN_EVEN), D_MODEL),
        "g_mla_q": gain(k[6], (N_EVEN, MLA_Q_RANK)),
        "g_mla_kv": gain(k[7], (N_EVEN, MLA_KV_RANK)),
        "w_mla_qb": nrm(k[8], (N_EVEN, MLA_Q_RANK, MLA_HEADS * (MLA_NOPE + MLA_ROPE)), MLA_Q_RANK),
        "w_mla_kvb": nrm(k[9], (N_EVEN, MLA_KV_RANK, MLA_HEADS * (MLA_NOPE + MLA_V)), MLA_KV_RANK),
        "swa_sinks": 0.5 * jax.random.normal(k[10], (N_EVEN, SWA_HEADS), f32),
        "w_out_even": nrm(k[11], (N_EVEN, MIX_EVEN, D_MODEL), MIX_EVEN),
        "w_in_odd": nrm(k[12], (N_ODD, D_MODEL, IN_ODD), D_MODEL),
        "w_out_odd": nrm(k[13], (N_ODD, MIX_ODD, D_MODEL), MIX_ODD),
        "moe_w_group": nrm(k[14], (DEPTH, D_MODEL, N_GROUPS), D_MODEL),
        "moe_b_group": 0.01 * jax.random.normal(k[15], (DEPTH, N_GROUPS), f32),
        "moe_w_router": nrm(k[16], (DEPTH, D_MODEL, N_EXPERTS), D_MODEL),
        "moe_b_router": 0.01 * jax.random.normal(k[17], (DEPTH, N_EXPERTS), f32),
        "moe_w_gate": nrm(k[18], (DEPTH, N_EXPERTS, D_MODEL, D_EXPERT), D_MODEL),
        "moe_w_up": nrm(k[19], (DEPTH, N_EXPERTS, D_MODEL, D_EXPERT), D_MODEL),
        "moe_w_down": nrm(k[20], (DEPTH, N_EXPERTS, D_EXPERT, D_MODEL), D_EXPERT),
    }


def reference(x, norm_mix, norm_ffn, norm_final, rel_bias_table, w_in_even, g_mla_q, g_mla_kv,
              w_mla_qb, w_mla_kvb, swa_sinks, w_out_even, w_in_odd, w_out_odd, moe_w_group,
              moe_b_group, moe_w_router, moe_b_router, moe_w_gate, moe_w_up, moe_w_down):
    h = x
    for layer in range(DEPTH):
        j = layer // 2
        u = rms_norm(h, norm_mix[layer])
        if layer % 2 == 0:
            h = h + even_mixer(u, w_in_even[j], g_mla_q[j], g_mla_kv[j], w_mla_qb[j], w_mla_kvb[j],
                               swa_sinks[j], w_out_even[j], rel_bias_table)
        else:
            h = h + odd_mixer(u, w_in_odd[j], w_out_odd[j], rel_bias_table)
        u = rms_norm(h, norm_ffn[layer])
        h = h + hier_moe(u, moe_w_group[layer], moe_b_group[layer], moe_w_router[layer], moe_b_router[layer],
                         moe_w_gate[layer], moe_w_up[layer], moe_w_down[layer])
    return rms_norm(h, norm_final)
```

```python
import functools
import math

import numpy as np
import jax
import jax.numpy as jnp
from jax import lax
from jax.experimental import pallas as pl
from jax.experimental.pallas import tpu as pltpu

F32 = jnp.float32
BF16 = jnp.bfloat16
NEG = -1e30
EPS = 1e-6

D = 1024
HD = 64
LANES = 128
MLA_HEADS, MLA_QR, MLA_KVR, MLA_NOPE, MLA_ROPE, MLA_V = 8, 256, 128, 64, 32, 64
ROPE_THETA = 10000.0
SWA_HEADS, SWA_KV, SWA_WIN = 8, 2, 128
MOBA_L, MOBA_TOPK = 256, 3
REL_BUCKETS, REL_MAX = 32, 2048
N_GROUPS, EPG, N_EXP, D_EXP = 4, 8, 32, 256
BIAS_CONST_FROM = 1792
BIAS_ROWS = BIAS_CONST_FROM + MOBA_L
TM = 512
TM_E = 256
VMEM_LIMIT = 56 * 1024 * 1024


def _cparams(sem):
    return pltpu.CompilerParams(dimension_semantics=sem, vmem_limit_bytes=VMEM_LIMIT)


def _dot(a, b):
    return jnp.dot(a, b, preferred_element_type=F32)


def _dot_nt(a, b):
    return lax.dot_general(a, b, (((1,), (1,)), ((), ())), preferred_element_type=F32)


def _rms(x, g):
    return x * lax.rsqrt(jnp.mean(x * x, axis=-1, keepdims=True) + EPS) * g


def _t5_bucket_np(dist):
    n = np.maximum(dist, 0)
    max_exact = REL_BUCKETS // 2
    nf = np.maximum(n, 1).astype(np.float32)
    large = max_exact + (np.log(nf / np.float32(max_exact)) / np.float32(math.log(REL_MAX / max_exact))
                         * np.float32(REL_BUCKETS - max_exact)).astype(np.int32)
    large = np.minimum(large, REL_BUCKETS - 1)
    return np.where(n < max_exact, n, large).astype(np.int32)


def _bias_kernel(table_ref, bucket_ref, out_ref):
    b = bucket_ref[0]
    for h in range(8):
        acc = jnp.zeros(b.shape, F32)
        for k in range(REL_BUCKETS):
            acc = jnp.where(b == k, table_ref[k, h], acc)
        out_ref[h, 0] = acc


def _bias_table(rel_table):
    nt = BIAS_ROWS // 128
    r = np.arange(BIAS_ROWS)[:, None] - np.arange(MOBA_L)[None, :]
    buckets = jnp.asarray(_t5_bucket_np(r).reshape(nt, 128, MOBA_L))
    out = pl.pallas_call(
        _bias_kernel,
        grid=(nt,),
        in_specs=[pl.BlockSpec(memory_space=pltpu.SMEM),
                  pl.BlockSpec((1, 128, MOBA_L), lambda m: (m, 0, 0))],
        out_specs=pl.BlockSpec((8, 1, 128, MOBA_L), lambda m: (0, m, 0, 0)),
        out_shape=jax.ShapeDtypeStruct((8, nt, 128, MOBA_L), F32),
        compiler_params=_cparams(("arbitrary",)),
        name="bias_table",
    )(rel_table, buckets)
    return out.reshape(8, BIAS_ROWS, MOBA_L)


def _proj_even_kernel(h_ref, g_ref, win_ref, gq_ref, gkv_ref, wqa_ref, wqb_ref, wk_ref, wv_ref,
                      cq_ref, sq_ref, ck_ref, sk_ref,
                      q_ref, k_ref, v_ref, qs_ref, ks_ref, vs_ref, vsw_ref):
    u = _rms(h_ref[...], g_ref[...]).astype(BF16)
    nq = _rms(_dot(u, win_ref[:, 0:256]), gq_ref[...]).astype(BF16)
    qa = _dot(nq, wqa_ref[...])
    qb = _dot(nq, wqb_ref[...])
    cq, sq = cq_ref[...], sq_ref[...]
    for hh in range(MLA_HEADS):
        sl = slice(128 * hh, 128 * (hh + 1))
        q_ref[:, sl] = (qa[:, sl] * cq + qb[:, sl] * sq).astype(BF16)
    nkv = _rms(_dot(u, win_ref[:, 256:384]), gkv_ref[...]).astype(BF16)
    kn = _dot(nkv, wk_ref[...])
    v_ref[...] = _dot(nkv, wv_ref[...]).astype(BF16)
    kr = _dot(u, win_ref[:, 1792:1920]) * ck_ref[...] + _dot(u, win_ref[:, 1920:2048]) * sk_ref[...]
    for hh in range(MLA_HEADS):
        sl = slice(128 * hh, 128 * (hh + 1))
        k_ref[:, sl] = (kn[:, sl] + kr).astype(BF16)
    qs_ref[...] = _dot(u, win_ref[:, 384:1408]).astype(BF16)
    ks_ref[...] = _dot(u, win_ref[:, 1408:1536]).astype(BF16)
    vs_ref[...] = _dot(u, win_ref[:, 1536:1664]).astype(BF16)
    vsw_ref[...] = _dot(u, win_ref[:, 1664:1792]).astype(BF16)


def _cols(w, idx, scale=None):
    idx = np.asarray(idx)
    out = jnp.take(w, jnp.asarray(np.maximum(idx, 0)), axis=1) * jnp.asarray((idx >= 0).astype(np.float32))
    if scale is not None:
        out = out * scale
    return out


def _even_weights(w_in, w_qb, w_kvb):
    o_cq, o_ckv, o_kr = 0, MLA_QR, MLA_QR + MLA_KVR
    o_qs = o_kr + MLA_ROPE
    o_ks = o_qs + SWA_HEADS * HD
    o_vs = o_ks + SWA_KV * HD
    idx = list(range(o_cq, o_cq + MLA_QR)) + list(range(o_ckv, o_ckv + MLA_KVR))
    scale = [1.0] * len(idx)
    for hq in range(SWA_HEADS):
        c = hq // (SWA_HEADS // SWA_KV)
        blk = [-1] * 128
        blk[64 * c:64 * c + 64] = range(o_qs + 64 * hq, o_qs + 64 * hq + 64)
        idx += blk
        scale += [HD ** -0.5] * 128
    idx += list(range(o_ks, o_ks + 128))
    idx += list(range(o_vs, o_vs + 128))
    idx += list(range(o_vs + 64, o_vs + 128)) + list(range(o_vs, o_vs + 64))
    scale += [1.0] * 384
    half = MLA_ROPE // 2
    idx += [-1] * 64 + list(range(o_kr, o_kr + MLA_ROPE)) + [-1] * 32
    idx += [-1] * 64 + list(range(o_kr + half, o_kr + MLA_ROPE)) + list(range(o_kr, o_kr + half)) + [-1] * 32
    scale += [1.0] * 256
    win = _cols(w_in, idx, jnp.asarray(np.asarray(scale, np.float32))).astype(BF16)
    qw = MLA_NOPE + MLA_ROPE
    ia, ib, ik, iv = [], [], [], []
    for h in range(MLA_HEADS):
        ia += list(range(qw * h, qw * h + qw)) + [-1] * 32
        ib += [-1] * 64 + list(range(qw * h + MLA_NOPE + half, qw * h + qw)) \
            + list(range(qw * h + MLA_NOPE, qw * h + MLA_NOPE + half)) + [-1] * 32
        ik += list(range(128 * h, 128 * h + MLA_NOPE)) + [-1] * 64
        iv += list(range(128 * h + MLA_NOPE, 128 * h + 128))
    return (win, _cols(w_qb, ia).astype(BF16), _cols(w_qb, ib).astype(BF16),
            _cols(w_kvb, ik).astype(BF16), _cols(w_kvb, iv).astype(BF16))


def _rope_tables(T):
    half = MLA_ROPE // 2
    freqs = ROPE_THETA ** (-jnp.arange(0, MLA_ROPE, 2, dtype=F32) / MLA_ROPE)
    ang = jnp.arange(T, dtype=F32)[:, None] * freqs[None, :]
    cos, sin = jnp.cos(ang), jnp.sin(ang)
    z = lambda n: jnp.zeros((T, n), F32)
    scale = (MLA_NOPE + MLA_ROPE) ** -0.5
    ck = jnp.concatenate([z(64), cos, cos, z(32)], axis=1)
    sk = jnp.concatenate([z(64), -sin, sin, z(32)], axis=1)
    cq = jnp.concatenate([jnp.ones((T, 64), F32), cos, cos, z(32)], axis=1) * scale
    sq = sk * scale
    return cq, sq, ck, sk


def _proj_even(h, g, win, gq, gkv, wqa, wqb, wk, wv, tables, T):
    N = h.shape[0]
    nt = T // TM
    full = lambda a: pl.BlockSpec(a.shape, lambda i: (0,) * a.ndim)
    tab = pl.BlockSpec((TM, 128), lambda i: (i % nt, 0))
    row = lambda w: pl.BlockSpec((TM, w), lambda i: (i, 0))
    widths = (1024, 1024, 512, 1024, 128, 128, 128)
    return pl.pallas_call(
        _proj_even_kernel,
        grid=(N // TM,),
        in_specs=[row(D), full(g), full(win), full(gq), full(gkv), full(wqa), full(wqb), full(wk), full(wv),
                  tab, tab, tab, tab],
        out_specs=[row(w) for w in widths],
        out_shape=[jax.ShapeDtypeStruct((N, w), BF16) for w in widths],
        compiler_params=_cparams(("arbitrary",)),
        name="proj_even",
    )(h, g, win, gq, gkv, wqa, wqb, wk, wv, *tables)


def _mla_kernel(q_ref, k_ref, v_ref, o_ref, *, tq):
    i = pl.program_id(2)
    qs = [q_ref[:, 128 * j:128 * (j + 1)] for j in range(2)]

    def step(kb, carry, diag):
        start = pl.multiple_of(kb * tq, tq)
        v = v_ref[pl.ds(start, tq), :]
        out = []
        for j in range(2):
            m, l, acc = carry[j]
            k = k_ref[pl.ds(start, tq), 128 * j:128 * (j + 1)]
            s = _dot_nt(qs[j], k)
            if diag:
                row = lax.broadcasted_iota(jnp.int32, s.shape, 0)
                col = lax.broadcasted_iota(jnp.int32, s.shape, 1)
                s = jnp.where(col <= row, s, NEG)
            m_new = jnp.maximum(m, jnp.max(s, axis=-1, keepdims=True))
            alpha = jnp.exp(m - m_new)
            p = jnp.exp(s - m_new)
            l = alpha * l + jnp.sum(p, axis=-1, keepdims=True)
            acc = alpha * acc + _dot(p.astype(BF16), v)
            out.append((m_new, l, acc))
        return tuple(out)

    init = tuple((jnp.full((tq, 1), NEG, F32), jnp.zeros((tq, 1), F32), jnp.zeros((tq, 128), F32))
                 for _ in range(2))
    carry = step(i, init, True)
    carry = lax.fori_loop(0, i, lambda kb, c: step(kb, c, False), carry)
    lane = lax.broadcasted_iota(jnp.int32, (tq, 128), 1)
    o0 = carry[0][2] / carry[0][1]
    o1 = carry[1][2] / carry[1][1]
    o_ref[...] = jnp.where(lane < HD, o0, o1).astype(BF16)


def _mla_attention(q, k, v, B, T, tq=512):
    nq = T // tq
    return pl.pallas_call(
        functools.partial(_mla_kernel, tq=tq),
        grid=(B, MLA_HEADS // 2, nq),
        in_specs=[pl.BlockSpec((tq, 256), lambda b, p, i: (b * nq + i, p)),
                  pl.BlockSpec((T, 256), lambda b, p, i: (b, p)),
                  pl.BlockSpec((T, 128), lambda b, p, i: (b, p))],
        out_specs=pl.BlockSpec((tq, 128), lambda b, p, i: (b * nq + i, p)),
        out_shape=jax.ShapeDtypeStruct((B * T, MLA_HEADS * MLA_V), BF16),
        compiler_params=_cparams(("arbitrary", "arbitrary", "arbitrary")),
        name="mla_attention",
    )(q, k, v)


def _swa_kernel(sink_ref, q_ref, k_ref, v_ref, vsw_ref, tb_ref, o_ref, *, tq):
    i = pl.program_id(1)
    G = SWA_HEADS // SWA_KV
    qi = lax.broadcasted_iota(jnp.int32, (128, 256), 0)
    kj = lax.broadcasted_iota(jnp.int32, (128, 256), 1)
    lane = lax.broadcasted_iota(jnp.int32, (128, 128), 1)
    for r in range(tq // 128):
        qstart = i * tq + 128 * r
        kstart = pl.multiple_of(jnp.maximum(qstart - 128, 0), 128)
        off = pl.multiple_of(qstart - kstart, 128)
        dist = off + qi - kj
        valid = (dist >= 0) & (dist < SWA_WIN)
        kw = k_ref[pl.ds(kstart, 256), :]
        vw = v_ref[pl.ds(kstart, 256), :]
        vsw = vsw_ref[pl.ds(kstart, 256), :]
        halves = [None] * SWA_HEADS
        for c in range(SWA_KV):
            qg = jnp.concatenate([q_ref[128 * r:128 * (r + 1), 128 * (G * c + g):128 * (G * c + g + 1)]
                                  for g in range(G)], axis=0)
            s_all = _dot_nt(qg, kw)
            for g in range(G):
                hq = G * c + g
                s = s_all[128 * g:128 * (g + 1)] + tb_ref[hq, pl.ds(off, 128), :]
                s = jnp.where(valid, s, NEG)
                sink = sink_ref[hq]
                m = jnp.maximum(jnp.max(s, axis=-1, keepdims=True), sink)
                p = jnp.exp(s - m)
                denom = jnp.sum(p, axis=-1, keepdims=True) + jnp.exp(sink - m)
                p = (p / denom).astype(BF16)
                halves[hq] = _dot(p, vw if c == hq % 2 else vsw)
        for mp in range(SWA_HEADS // 2):
            o_ref[128 * r:128 * (r + 1), 128 * mp:128 * (mp + 1)] = jnp.where(
                lane < HD, halves[2 * mp], halves[2 * mp + 1]).astype(BF16)


def _swa_attention(q8, k, v, vsw, sinks, tb, B, T, tq=512):
    nq = T // tq
    kv = pl.BlockSpec((T, 128), lambda b, i: (b, 0))
    return pl.pallas_call(
        functools.partial(_swa_kernel, tq=tq),
        grid=(B, nq),
        in_specs=[pl.BlockSpec(memory_space=pltpu.SMEM),
                  pl.BlockSpec((tq, 1024), lambda b, i: (b * nq + i, 0)),
                  kv, kv, kv,
                  pl.BlockSpec((8, 256, MOBA_L), lambda b, i: (0, 0, 0))],
        out_specs=pl.BlockSpec((tq, 512), lambda b, i: (b * nq + i, 0)),
        out_shape=jax.ShapeDtypeStruct((B * T, SWA_HEADS * HD), BF16),
        compiler_params=_cparams(("arbitrary", "arbitrary")),
        name="swa_attention",
    )(sinks, q8, k, v, vsw, tb)


def _outproj_router_kernel(h_ref, oa_ref, ob_ref, wo_ref, g_ref, wr_ref, br_ref,
                           hn_ref, u_ref, ri_ref, rg_ref, cnt_ref, base_ref):
    i = pl.program_id(0)

    @pl.when(i == 0)
    def _():
        base_ref[...] = jnp.zeros_like(base_ref)

    hn = h_ref[...] + _dot(oa_ref[...], wo_ref[0:512, :]) + _dot(ob_ref[...], wo_ref[512:1024, :])
    hn_ref[...] = hn
    u = _rms(hn, g_ref[...])
    u_ref[...] = u.astype(BF16)
    logits = jnp.dot(u, wr_ref[...], preferred_element_type=F32, precision=lax.Precision.HIGHEST) + br_ref[...]
    tm = logits.shape[0]
    lane = lax.broadcasted_iota(jnp.int32, (tm, 128), 1).astype(F32)
    big = 1e6
    isg = (lane >= N_EXP) & (lane < N_EXP + N_GROUPS)
    gl = jnp.where(isg, logits, NEG)
    gmax = jnp.max(gl, axis=-1, keepdims=True)
    gsel = jnp.min(jnp.where(gl == gmax, lane, big), axis=-1, keepdims=True) - N_EXP
    g_gate = 1.0 / jnp.sum(jnp.where(isg, jnp.exp(gl - gmax), 0.0), axis=-1, keepdims=True)
    ine = (lane >= gsel * EPG) & (lane < gsel * EPG + EPG)
    el = jnp.where(ine, logits, NEG)
    m1 = jnp.max(el, axis=-1, keepdims=True)
    e1 = jnp.min(jnp.where(el == m1, lane, big), axis=-1, keepdims=True)
    el2 = jnp.where(lane == e1, NEG, el)
    m2 = jnp.max(el2, axis=-1, keepdims=True)
    e2 = jnp.min(jnp.where(el2 == m2, lane, big), axis=-1, keepdims=True)
    r = jnp.exp(m2 - m1)
    gate1 = g_gate / (1.0 + r)
    gate2 = g_gate * r / (1.0 + r)
    oh1 = (lane == e1).astype(F32)
    oh2 = (lane == e2).astype(F32)
    oh = oh1 + oh2
    rt = lax.broadcasted_iota(jnp.int32, (tm, tm), 0)
    ct = lax.broadcasted_iota(jnp.int32, (tm, tm), 1)
    lower = jnp.where(ct < rt, 1.0, 0.0).astype(BF16)
    prefix = _dot(lower, oh.astype(BF16)) + base_ref[...]
    rank1 = jnp.sum(oh1 * prefix, axis=-1, keepdims=True)
    rank2 = jnp.sum(oh2 * prefix, axis=-1, keepdims=True)
    base_ref[...] = base_ref[...] + jnp.sum(oh, axis=0, keepdims=True)
    cnt_ref[...] = base_ref[...]
    ri = jnp.where(lane == 0, e1, jnp.where(lane == 1, e2, 0.0))
    ri = jnp.where(lane == 2, rank1, jnp.where(lane == 3, rank2, ri))
    ri_ref[...] = ri.astype(jnp.int32)
    rg_ref[...] = jnp.where(lane == 0, gate1, jnp.where(lane == 1, gate2, 0.0))


def _outproj_router(h, oa, ob, wo, g, wr, br):
    N = h.shape[0]
    full = lambda a: pl.BlockSpec(a.shape, lambda i: (0,) * a.ndim)
    row = lambda w: pl.BlockSpec((TM, w), lambda i: (i, 0))
    return pl.pallas_call(
        _outproj_router_kernel,
        grid=(N // TM,),
        in_specs=[row(D), row(512), row(512), full(wo), full(g), full(wr), full(br)],
        out_specs=[row(D), row(D), row(128), row(128), pl.BlockSpec((1, 128), lambda i: (0, 0))],
        out_shape=[jax.ShapeDtypeStruct((N, D), F32), jax.ShapeDtypeStruct((N, D), BF16),
                   jax.ShapeDtypeStruct((N, 128), jnp.int32), jax.ShapeDtypeStruct((N, 128), F32),
                   jax.ShapeDtypeStruct((1, 128), F32)],
        scratch_shapes=[pltpu.VMEM((1, 128), F32)],
        compiler_params=_cparams(("arbitrary",)),
        name="outproj_router",
    )(h, oa, ob, wo, g, wr, br)


def _moe_kernel(te_ref, nu_ref, x_ref, wgu_ref, wd_ref, o_ref):
    i = pl.program_id(0)

    @pl.when(i < nu_ref[0])
    def _():
        hgu = _dot(x_ref[...], wgu_ref[...])
        hg, hu = hgu[:, :D_EXP], hgu[:, D_EXP:]
        a = hg * (1.0 / (1.0 + jnp.exp(-hg))) * hu
        o_ref[...] = _dot(a.astype(BF16), wd_ref[...]).astype(BF16)

    @pl.when(i >= nu_ref[0])
    def _():
        o_ref[...] = jnp.zeros_like(o_ref)


def _moe_experts(xs, tile_expert, num_used, wgu, wd):
    P = xs.shape[0]
    grid_spec = pltpu.PrefetchScalarGridSpec(
        num_scalar_prefetch=2,
        grid=(P // TM_E,),
        in_specs=[pl.BlockSpec((TM_E, D), lambda i, te, nu: (i, 0)),
                  pl.BlockSpec((None, D, 2 * D_EXP), lambda i, te, nu: (te[i], 0, 0)),
                  pl.BlockSpec((None, D_EXP, D), lambda i, te, nu: (te[i], 0, 0))],
        out_specs=pl.BlockSpec((TM_E, D), lambda i, te, nu: (i, 0)),
    )
    return pl.pallas_call(
        _moe_kernel,
        grid_spec=grid_spec,
        out_shape=jax.ShapeDtypeStruct((P, D), BF16),
        compiler_params=_cparams(("arbitrary",)),
        name="moe_experts",
    )(tile_expert, num_used, xs, wgu, wd)


def _dispatch(u, ri, cnt):
    N = u.shape[0]
    counts = cnt[0, :N_EXP].astype(jnp.int32)
    padded = ((counts + TM_E - 1) // TM_E) * TM_E
    ends = jnp.cumsum(padded)
    offs = ends - padded
    e1, e2, r1, r2 = ri[:, 0], ri[:, 1], ri[:, 2], ri[:, 3]
    d1 = offs[e1] + r1
    d2 = offs[e2] + r2
    P = 2 * N + N_EXP * TM_E
    tok = jnp.arange(N, dtype=jnp.int32)
    row_token = jnp.zeros((P,), jnp.int32).at[d1].set(tok).at[d2].set(tok)
    tile_start = jnp.arange(P // TM_E, dtype=jnp.int32) * TM_E
    tile_expert = jnp.minimum(jnp.searchsorted(ends, tile_start, side="right"), N_EXP - 1).astype(jnp.int32)
    num_used = (ends[-1:] // TM_E).astype(jnp.int32)
    xs = jnp.take(u, row_token, axis=0)
    return xs, tile_expert, num_used, d1, d2


def _combine(h_ref, y1_ref, y2_ref, rg_ref):
    rg = rg_ref[...]
    return h_ref[...] + rg[:, 0:1] * y1_ref[...].astype(F32) + rg[:, 1:2] * y2_ref[...].astype(F32)


def _combine_proj_odd_kernel(h_ref, y1_ref, y2_ref, rg_ref, g_ref, win_ref, hn_ref, *out_refs):
    hn = _combine(h_ref, y1_ref, y2_ref, rg_ref)
    hn_ref[...] = hn
    u = _rms(hn, g_ref[...]).astype(BF16)
    for n, o_ref in enumerate(out_refs):
        o_ref[...] = _dot(u, win_ref[:, 512 * n:512 * (n + 1)]).astype(BF16)


def _combine_proj_odd(h, y1, y2, rg, g, win):
    N = h.shape[0]
    full = lambda a: pl.BlockSpec(a.shape, lambda i: (0,) * a.ndim)
    row = lambda w: pl.BlockSpec((TM, w), lambda i: (i, 0))
    return pl.pallas_call(
        _combine_proj_odd_kernel,
        grid=(N // TM,),
        in_specs=[row(D), row(D), row(D), row(128), full(g), full(win)],
        out_specs=[row(D)] + [row(512)] * 6,
        out_shape=[jax.ShapeDtypeStruct((N, D), F32)] + [jax.ShapeDtypeStruct((N, 512), BF16)] * 6,
        compiler_params=_cparams(("arbitrary",)),
        name="combine_proj_odd",
    )(h, y1, y2, rg, g, win)


def _combine_final_kernel(h_ref, y1_ref, y2_ref, rg_ref, g_ref, o_ref):
    o_ref[...] = _rms(_combine(h_ref, y1_ref, y2_ref, rg_ref), g_ref[...])


def _combine_final(h, y1, y2, rg, g):
    N = h.shape[0]
    row = lambda w: pl.BlockSpec((TM, w), lambda i: (i, 0))
    return pl.pallas_call(
        _combine_final_kernel,
        grid=(N // TM,),
        in_specs=[row(D), row(D), row(D), row(128), pl.BlockSpec((1, D), lambda i: (0, 0))],
        out_specs=row(D),
        out_shape=jax.ShapeDtypeStruct((N, D), F32),
        compiler_params=_cparams(("arbitrary",)),
        name="combine_final",
    )(h, y1, y2, rg, g)


def _moba_kernel(q_ref, k_ref, v_ref, tb_ref, o_ref, kmean_ref, sel_ref, *, nkb):
    i = pl.program_id(2)
    L = MOBA_L

    @pl.when(i == 0)
    def _():
        kmean_ref[...] = jnp.zeros_like(kmean_ref)
        for n in range(nkb):
            kmean_ref[n:n + 1, :] = jnp.sum(k_ref[n * L:(n + 1) * L, :].astype(F32), axis=0, keepdims=True) * (1.0 / L)

    lane = lax.broadcasted_iota(jnp.int32, (L, 128), 1)
    q_all = q_ref[...]
    qs = [jnp.where(lane < HD, q_all, 0).astype(BF16), jnp.where(lane >= HD, q_all, 0).astype(BF16)]
    lane_f = lane.astype(F32)
    for j in range(2):
        gsc = lax.dot_general(qs[j].astype(F32), kmean_ref[...], (((1,), (1,)), ((), ())),
                              preferred_element_type=F32, precision=lax.Precision.HIGHEST)
        g = jnp.where(lane < i, gsc, NEG)
        sel = jnp.zeros((L, 128), F32)
        for _ in range(MOBA_TOPK):
            mx = jnp.max(g, axis=-1, keepdims=True)
            idx = jnp.min(jnp.where(g == mx, lane_f, 1e6), axis=-1, keepdims=True)
            pick = lane_f == idx
            sel = jnp.where(pick & (mx > 0.5 * NEG), 1.0, sel)
            g = jnp.where(pick, 2.0 * NEG, g)
        sel_ref[j] = sel

    row = lax.broadcasted_iota(jnp.int32, (L, L), 0)
    col = lax.broadcasted_iota(jnp.int32, (L, L), 1)

    def step(n, carry, diag):
        start = pl.multiple_of(n * L, L)
        k = k_ref[pl.ds(start, L), :]
        v = v_ref[pl.ds(start, L), :]
        boff = pl.multiple_of(jnp.minimum((i - n) * L, BIAS_CONST_FROM), 128)
        out = []
        for j in range(2):
            m, l, acc = carry[j]
            s = _dot_nt(qs[j], k) + tb_ref[j, pl.ds(boff, L), :]
            if diag:
                s = jnp.where(col <= row, s, NEG)
            else:
                a = jnp.sum(jnp.where(lane == n, sel_ref[j], 0.0), axis=-1, keepdims=True)
                s = jnp.where(a > 0.5, s, NEG)
            m_new = jnp.maximum(m, jnp.max(s, axis=-1, keepdims=True))
            alpha = jnp.exp(m - m_new)
            p = jnp.exp(s - m_new)
            l = alpha * l + jnp.sum(p, axis=-1, keepdims=True)
            acc = alpha * acc + _dot(p.astype(BF16), v)
            out.append((m_new, l, acc))
        return tuple(out)

    init = tuple((jnp.full((L, 1), NEG, F32), jnp.zeros((L, 1), F32), jnp.zeros((L, 128), F32))
                 for _ in range(2))
    carry = step(i, init, True)
    carry = lax.fori_loop(0, i, lambda n, c: step(n, c, False), carry)
    o0 = carry[0][2] / carry[0][1]
    o1 = carry[1][2] / carry[1][1]
    o_ref[...] = jnp.where(lane < HD, o0, o1).astype(BF16)


def _moba_attention(q, k, v, tb, B, T):
    L = MOBA_L
    nq = T // L
    kv = pl.BlockSpec((T, 128), lambda b, p, i: (b, p))
    return pl.pallas_call(
        functools.partial(_moba_kernel, nkb=nq),
        grid=(B, 4, nq),
        in_specs=[pl.BlockSpec((L, 128), lambda b, p, i: (b * nq + i, p)), kv, kv,
                  pl.BlockSpec((2, BIAS_ROWS, L), lambda b, p, i: (p, 0, 0))],
        out_specs=pl.BlockSpec((L, 128), lambda b, p, i: (b * nq + i, p)),
        out_shape=jax.ShapeDtypeStruct((B * T, 512), BF16),
        scratch_shapes=[pltpu.VMEM((128, 128), F32), pltpu.VMEM((2, L, 128), F32)],
        compiler_params=_cparams(("arbitrary", "arbitrary", "arbitrary")),
        name="moba_attention",
    )(q, k, v, tb)


def _sb_kernel(q_ref, k_ref, v_ref, o_ref, *, tq):
    i = pl.program_id(2)
    lane = lax.broadcasted_iota(jnp.int32, (tq, 128), 1)
    q_all = q_ref[...]
    qs = [jnp.where(lane < HD, q_all, 0).astype(BF16), jnp.where(lane >= HD, q_all, 0).astype(BF16)]
    row = lax.broadcasted_iota(jnp.int32, (tq, tq), 0)
    col = lax.broadcasted_iota(jnp.int32, (tq, tq), 1)
    past = col < row
    after_mat = jnp.where(row > col, 1.0, 0.0).astype(BF16)

    def step(kb, carry, diag):
        start = pl.multiple_of(kb * tq, tq)
        k = k_ref[pl.ds(start, tq), :]
        v = v_ref[pl.ds(start, tq), :]
        out = []
        for j in range(2):
            c, acc = carry[j]
            z = _dot_nt(qs[j], k)
            sp = jnp.log(1.0 + jnp.exp(-jnp.abs(z)))
            log1m = -(jnp.maximum(z, 0.0) + sp)
            logsig = jnp.minimum(z, 0.0) - sp
            if diag:
                log1m = jnp.where(past, log1m, 0.0)
            hi = log1m.astype(BF16)
            lo = (log1m - hi.astype(F32)).astype(BF16)
            after = c + _dot(hi, after_mat) + _dot(lo, after_mat)
            w = jnp.exp(logsig + after)
            if diag:
                w = jnp.where(past, w, 0.0)
            acc = acc + _dot(w.astype(BF16), v)
            c = c + jnp.sum(log1m, axis=-1, keepdims=True)
            out.append((c, acc))
        return tuple(out)

    init = tuple((jnp.zeros((tq, 1), F32), jnp.zeros((tq, 128), F32)) for _ in range(2))
    carry = step(i, init, True)
    carry = lax.fori_loop(0, i, lambda t, c: step(i - 1 - t, c, False), carry)
    o_ref[...] = jnp.where(lane < HD, carry[0][1], carry[1][1]).astype(BF16)


def _sb_attention(q, k, v, B, T, tq=256):
    nq = T // tq
    kv = pl.BlockSpec((T, 128), lambda b, p, i: (b, p))
    return pl.pallas_call(
        functools.partial(_sb_kernel, tq=tq),
        grid=(B, 4, nq),
        in_specs=[pl.BlockSpec((tq, 128), lambda b, p, i: (b * nq + i, p)), kv, kv],
        out_specs=pl.BlockSpec((tq, 128), lambda b, p, i: (b * nq + i, p)),
        out_shape=jax.ShapeDtypeStruct((B * T, 512), BF16),
        compiler_params=_cparams(("arbitrary", "arbitrary", "arbitrary")),
        name="sb_attention",
    )(q, k, v)


def _router_weights(w_group, b_group, w_router, b_router):
    pad = 128 - N_EXP - N_GROUPS
    wr = jnp.concatenate([w_router, w_group, jnp.zeros((D, pad), F32)], axis=1)
    br = jnp.concatenate([b_router, b_group, jnp.zeros((pad,), F32)])[None, :]
    return wr, br


def _ffn(h, oa, ob, wo, g, w_group, b_group, w_router, b_router, w_gate, w_up, w_down):
    wr, br = _router_weights(w_group, b_group, w_router, b_router)
    hn, u, ri, rg, cnt = _outproj_router(h, oa, ob, wo.astype(BF16), g[None, :], wr, br)
    xs, tile_expert, num_used, d1, d2 = _dispatch(u, ri, cnt)
    wgu = jnp.concatenate([w_gate, w_up], axis=-1).astype(BF16)
    ys = _moe_experts(xs, tile_expert, num_used, wgu, w_down.astype(BF16))
    return hn, jnp.take(ys, d1, axis=0), jnp.take(ys, d2, axis=0), rg


def kernel(x, norm_mix, norm_ffn, norm_final, rel_bias_table, w_in_even, g_mla_q, g_mla_kv, w_mla_qb, w_mla_kvb, swa_sinks, w_out_even, w_in_odd, w_out_odd, moe_w_group, moe_b_group, moe_w_router, moe_b_router, moe_w_gate, moe_w_up, moe_w_down):
    B, T, _ = x.shape
    h = x.reshape(B * T, D)
    tb = _bias_table(rel_bias_table)

    win, wqa, wqb, wk, wv = _even_weights(w_in_even[0], w_mla_qb[0], w_mla_kvb[0])
    q, k, v, qs8, ks, vs, vsw = _proj_even(h, norm_mix[0][None, :], win, g_mla_q[0][None, :], g_mla_kv[0][None, :],
                                           wqa, wqb, wk, wv, _rope_tables(T), T)
    oa = _mla_attention(q, k, v, B, T)
    ob = _swa_attention(qs8, ks, vs, vsw, swa_sinks[0], tb, B, T)
    h, y1, y2, rg = _ffn(h, oa, ob, w_out_even[0], norm_ffn[0], moe_w_group[0], moe_b_group[0], moe_w_router[0],
                         moe_b_router[0], moe_w_gate[0], moe_w_up[0], moe_w_down[0])

    scale = np.ones((1, 3072), np.float32)
    scale[:, 0:512] = HD ** -0.5
    scale[:, 1536:2048] = HD ** -0.5
    wodd = (w_in_odd[0] * jnp.asarray(scale)).astype(BF16)
    h, qc, kc, vc, qd, kd, vd = _combine_proj_odd(h, y1, y2, rg, norm_mix[1][None, :], wodd)
    oc = _moba_attention(qc, kc, vc, tb, B, T)
    od = _sb_attention(qd, kd, vd, B, T)
    h, y1, y2, rg = _ffn(h, oc, od, w_out_odd[0], norm_ffn[1], moe_w_group[1], moe_b_group[1], moe_w_router[1],
                         moe_b_router[1], moe_w_gate[1], moe_w_up[1], moe_w_down[1])
    out = _combine_final(h, y1, y2, rg, norm_final[None, :])
    return out.reshape(B, T, D)
```

```python
import functools
import math

import numpy as np
import jax
import jax.numpy as jnp
from jax import lax
from jax.experimental import pallas as pl
from jax.experimental.pallas import tpu as pltpu

F32 = jnp.float32
BF16 = jnp.bfloat16
NEG = -1e30
EPS = 1e-6

D = 1024
HD = 64
LANES = 128
MLA_HEADS, MLA_QR, MLA_KVR, MLA_NOPE, MLA_ROPE, MLA_V = 8, 256, 128, 64, 32, 64
ROPE_THETA = 10000.0
SWA_HEADS, SWA_KV, SWA_WIN = 8, 2, 128
MOBA_L, MOBA_TOPK = 256, 3
REL_BUCKETS, REL_MAX = 32, 2048
N_GROUPS, EPG, N_EXP, D_EXP = 4, 8, 32, 256
BIAS_CONST_FROM = 1792
MOBA_TQ = 2 * MOBA_L
BIAS_PAD = MOBA_L
BIAS_ROWS = BIAS_PAD + BIAS_CONST_FROM + MOBA_TQ
SB_DONE = -110.0
TM = 512
TM_E = 256
VMEM_LIMIT = 56 * 1024 * 1024


def _cparams(sem):
    return pltpu.CompilerParams(dimension_semantics=sem, vmem_limit_bytes=VMEM_LIMIT)


def _dot(a, b):
    return jnp.dot(a, b, preferred_element_type=F32)


def _dot_nt(a, b):
    return lax.dot_general(a, b, (((1,), (1,)), ((), ())), preferred_element_type=F32)


def _rms(x, g):
    return x * lax.rsqrt(jnp.mean(x * x, axis=-1, keepdims=True) + EPS) * g


def _flash_update(s, v, m, l, acc):
    m_new = jnp.maximum(m, jnp.max(s, axis=-1, keepdims=True))
    alpha = jnp.exp(m - m_new)
    p = jnp.exp(s - m_new)
    l = alpha * l + jnp.sum(p, axis=-1, keepdims=True)
    acc = alpha * acc + _dot(p.astype(BF16), v)
    return m_new, l, acc


def _t5_bucket_np(dist):
    n = np.maximum(dist, 0)
    max_exact = REL_BUCKETS // 2
    nf = np.maximum(n, 1).astype(np.float32)
    large = max_exact + (np.log(nf / np.float32(max_exact)) / np.float32(math.log(REL_MAX / max_exact))
                         * np.float32(REL_BUCKETS - max_exact)).astype(np.int32)
    large = np.minimum(large, REL_BUCKETS - 1)
    return np.where(n < max_exact, n, large).astype(np.int32)


def _bias_kernel(table_ref, bucket_ref, out_ref):
    b = bucket_ref[0]
    for h in range(8):
        acc = jnp.zeros(b.shape, F32)
        for k in range(REL_BUCKETS):
            acc = jnp.where(b == k, table_ref[k, h], acc)
        out_ref[h, 0] = acc


def _bias_table(rel_table):
    nt = BIAS_ROWS // 128
    r = np.arange(BIAS_ROWS)[:, None] - BIAS_PAD - np.arange(MOBA_L)[None, :]
    buckets = jnp.asarray(_t5_bucket_np(r).reshape(nt, 128, MOBA_L))
    out = pl.pallas_call(
        _bias_kernel,
        grid=(nt,),
        in_specs=[pl.BlockSpec(memory_space=pltpu.SMEM),
                  pl.BlockSpec((1, 128, MOBA_L), lambda m: (m, 0, 0))],
        out_specs=pl.BlockSpec((8, 1, 128, MOBA_L), lambda m: (0, m, 0, 0)),
        out_shape=jax.ShapeDtypeStruct((8, nt, 128, MOBA_L), F32),
        compiler_params=_cparams(("arbitrary",)),
        name="bias_table",
    )(rel_table, buckets)
    return out.reshape(8, BIAS_ROWS, MOBA_L)


def _proj_even_kernel(h_ref, g_ref, win_ref, gq_ref, gkv_ref, wqa_ref, wqb_ref, wk_ref, wv_ref,
                      cq_ref, sq_ref, ck_ref, sk_ref,
                      q_ref, k_ref, v_ref, qs_ref, ks_ref, vs_ref, vsw_ref):
    u = _rms(h_ref[...], g_ref[...]).astype(BF16)
    nq = _rms(_dot(u, win_ref[:, 0:256]), gq_ref[...]).astype(BF16)
    qa = _dot(nq, wqa_ref[...])
    qb = _dot(nq, wqb_ref[...])
    cq, sq = cq_ref[...], sq_ref[...]
    for hh in range(MLA_HEADS):
        sl = slice(128 * hh, 128 * (hh + 1))
        q_ref[:, sl] = (qa[:, sl] * cq + qb[:, sl] * sq).astype(BF16)
    nkv = _rms(_dot(u, win_ref[:, 256:384]), gkv_ref[...]).astype(BF16)
    kn = _dot(nkv, wk_ref[...])
    v_ref[...] = _dot(nkv, wv_ref[...]).astype(BF16)
    kr = _dot(u, win_ref[:, 1792:1920]) * ck_ref[...] + _dot(u, win_ref[:, 1920:2048]) * sk_ref[...]
    for hh in range(MLA_HEADS):
        sl = slice(128 * hh, 128 * (hh + 1))
        k_ref[:, sl] = (kn[:, sl] + kr).astype(BF16)
    qs_ref[...] = _dot(u, win_ref[:, 384:1408]).astype(BF16)
    ks_ref[...] = _dot(u, win_ref[:, 1408:1536]).astype(BF16)
    vs_ref[...] = _dot(u, win_ref[:, 1536:1664]).astype(BF16)
    vsw_ref[...] = _dot(u, win_ref[:, 1664:1792]).astype(BF16)


def _cols(w, idx, scale=None):
    idx = np.asarray(idx)
    out = jnp.take(w, jnp.asarray(np.maximum(idx, 0)), axis=1) * jnp.asarray((idx >= 0).astype(np.float32))
    if scale is not None:
        out = out * scale
    return out


def _even_weights(w_in, w_qb, w_kvb):
    o_cq, o_ckv, o_kr = 0, MLA_QR, MLA_QR + MLA_KVR
    o_qs = o_kr + MLA_ROPE
    o_ks = o_qs + SWA_HEADS * HD
    o_vs = o_ks + SWA_KV * HD
    idx = list(range(o_cq, o_cq + MLA_QR)) + list(range(o_ckv, o_ckv + MLA_KVR))
    scale = [1.0] * len(idx)
    for hq in range(SWA_HEADS):
        c = hq // (SWA_HEADS // SWA_KV)
        blk = [-1] * 128
        blk[64 * c:64 * c + 64] = range(o_qs + 64 * hq, o_qs + 64 * hq + 64)
        idx += blk
        scale += [HD ** -0.5] * 128
    idx += list(range(o_ks, o_ks + 128))
    idx += list(range(o_vs, o_vs + 128))
    idx += list(range(o_vs + 64, o_vs + 128)) + list(range(o_vs, o_vs + 64))
    scale += [1.0] * 384
    half = MLA_ROPE // 2
    idx += [-1] * 64 + list(range(o_kr, o_kr + MLA_ROPE)) + [-1] * 32
    idx += [-1] * 64 + list(range(o_kr + half, o_kr + MLA_ROPE)) + list(range(o_kr, o_kr + half)) + [-1] * 32
    scale += [1.0] * 256
    win = _cols(w_in, idx, jnp.asarray(np.asarray(scale, np.float32))).astype(BF16)
    qw = MLA_NOPE + MLA_ROPE
    ia, ib, ik, iv = [], [], [], []
    for h in range(MLA_HEADS):
        ia += list(range(qw * h, qw * h + qw)) + [-1] * 32
        ib += [-1] * 64 + list(range(qw * h + MLA_NOPE + half, qw * h + qw)) \
            + list(range(qw * h + MLA_NOPE, qw * h + MLA_NOPE + half)) + [-1] * 32
        ik += list(range(128 * h, 128 * h + MLA_NOPE)) + [-1] * 64
        iv += list(range(128 * h + MLA_NOPE, 128 * h + 128))
    return (win, _cols(w_qb, ia).astype(BF16), _cols(w_qb, ib).astype(BF16),
            _cols(w_kvb, ik).astype(BF16), _cols(w_kvb, iv).astype(BF16))


def _rope_tables(T):
    half = MLA_ROPE // 2
    freqs = ROPE_THETA ** (-jnp.arange(0, MLA_ROPE, 2, dtype=F32) / MLA_ROPE)
    ang = jnp.arange(T, dtype=F32)[:, None] * freqs[None, :]
    cos, sin = jnp.cos(ang), jnp.sin(ang)
    z = lambda n: jnp.zeros((T, n), F32)
    scale = (MLA_NOPE + MLA_ROPE) ** -0.5
    ck = jnp.concatenate([z(64), cos, cos, z(32)], axis=1)
    sk = jnp.concatenate([z(64), -sin, sin, z(32)], axis=1)
    cq = jnp.concatenate([jnp.ones((T, 64), F32), cos, cos, z(32)], axis=1) * scale
    sq = sk * scale
    return cq, sq, ck, sk


def _proj_even(h, g, win, gq, gkv, wqa, wqb, wk, wv, tables, T):
    N = h.shape[0]
    nt = T // TM
    full = lambda a: pl.BlockSpec(a.shape, lambda i: (0,) * a.ndim)
    tab = pl.BlockSpec((TM, 128), lambda i: (i % nt, 0))
    row = lambda w: pl.BlockSpec((TM, w), lambda i: (i, 0))
    widths = (1024, 1024, 512, 1024, 128, 128, 128)
    return pl.pallas_call(
        _proj_even_kernel,
        grid=(N // TM,),
        in_specs=[row(D), full(g), full(win), full(gq), full(gkv), full(wqa), full(wqb), full(wk), full(wv),
                  tab, tab, tab, tab],
        out_specs=[row(w) for w in widths],
        out_shape=[jax.ShapeDtypeStruct((N, w), BF16) for w in widths],
        compiler_params=_cparams(("arbitrary",)),
        name="proj_even",
    )(h, g, win, gq, gkv, wqa, wqb, wk, wv, *tables)


def _mla_kernel(q_ref, k_ref, v_ref, o_ref, *, tq):
    i = pl.program_id(2)
    qs = [q_ref[:, 128 * j:128 * (j + 1)] for j in range(2)]

    def step(kb, carry, diag):
        start = pl.multiple_of(kb * tq, tq)
        v = v_ref[pl.ds(start, tq), :]
        out = []
        for j in range(2):
            k = k_ref[pl.ds(start, tq), 128 * j:128 * (j + 1)]
            s = _dot_nt(qs[j], k)
            if diag:
                row = lax.broadcasted_iota(jnp.int32, s.shape, 0)
                col = lax.broadcasted_iota(jnp.int32, s.shape, 1)
                s = jnp.where(col <= row, s, NEG)
            out.append(_flash_update(s, v, *carry[j]))
        return tuple(out)

    init = tuple((jnp.full((tq, 1), NEG, F32), jnp.zeros((tq, 1), F32), jnp.zeros((tq, 128), F32))
                 for _ in range(2))
    carry = step(i, init, True)
    carry = lax.fori_loop(0, i, lambda kb, c: step(kb, c, False), carry)
    lane = lax.broadcasted_iota(jnp.int32, (tq, 128), 1)
    o0 = carry[0][2] / carry[0][1]
    o1 = carry[1][2] / carry[1][1]
    o_ref[...] = jnp.where(lane < HD, o0, o1).astype(BF16)


def _mla_attention(q, k, v, B, T, tq=512):
    nq = T // tq
    return pl.pallas_call(
        functools.partial(_mla_kernel, tq=tq),
        grid=(B, MLA_HEADS // 2, nq),
        in_specs=[pl.BlockSpec((tq, 256), lambda b, p, i: (b * nq + i, p)),
                  pl.BlockSpec((T, 256), lambda b, p, i: (b, p)),
                  pl.BlockSpec((T, 128), lambda b, p, i: (b, p))],
        out_specs=pl.BlockSpec((tq, 128), lambda b, p, i: (b * nq + i, p)),
        out_shape=jax.ShapeDtypeStruct((B * T, MLA_HEADS * MLA_V), BF16),
        compiler_params=_cparams(("arbitrary", "arbitrary", "arbitrary")),
        name="mla_attention",
    )(q, k, v)


def _swa_kernel(sink_ref, q_ref, k_ref, v_ref, vsw_ref, tb_ref, o_ref, *, tq):
    i = pl.program_id(1)
    G = SWA_HEADS // SWA_KV
    qi = lax.broadcasted_iota(jnp.int32, (128, 256), 0)
    kj = lax.broadcasted_iota(jnp.int32, (128, 256), 1)
    lane = lax.broadcasted_iota(jnp.int32, (128, 128), 1)
    for r in range(tq // 128):
        qstart = i * tq + 128 * r
        kstart = pl.multiple_of(jnp.maximum(qstart - 128, 0), 128)
        off = pl.multiple_of(qstart - kstart, 128)
        dist = off + qi - kj
        valid = (dist >= 0) & (dist < SWA_WIN)
        kw = k_ref[pl.ds(kstart, 256), :]
        vw = v_ref[pl.ds(kstart, 256), :]
        vsw = vsw_ref[pl.ds(kstart, 256), :]
        halves = [None] * SWA_HEADS
        for c in range(SWA_KV):
            qg = jnp.concatenate([q_ref[128 * r:128 * (r + 1), 128 * (G * c + g):128 * (G * c + g + 1)]
                                  for g in range(G)], axis=0)
            s_all = _dot_nt(qg, kw)
            for g in range(G):
                hq = G * c + g
                s = s_all[128 * g:128 * (g + 1)] + tb_ref[hq, pl.ds(off + BIAS_PAD, 128), :]
                s = jnp.where(valid, s, NEG)
                sink = sink_ref[hq]
                m = jnp.maximum(jnp.max(s, axis=-1, keepdims=True), sink)
                p = jnp.exp(s - m)
                denom = jnp.sum(p, axis=-1, keepdims=True) + jnp.exp(sink - m)
                p = (p / denom).astype(BF16)
                halves[hq] = _dot(p, vw if c == hq % 2 else vsw)
        for mp in range(SWA_HEADS // 2):
            o_ref[128 * r:128 * (r + 1), 128 * mp:128 * (mp + 1)] = jnp.where(
                lane < HD, halves[2 * mp], halves[2 * mp + 1]).astype(BF16)


def _swa_attention(q8, k, v, vsw, sinks, tb, B, T, tq=512):
    nq = T // tq
    kv = pl.BlockSpec((T, 128), lambda b, i: (b, 0))
    return pl.pallas_call(
        functools.partial(_swa_kernel, tq=tq),
        grid=(B, nq),
        in_specs=[pl.BlockSpec(memory_space=pltpu.SMEM),
                  pl.BlockSpec((tq, 1024), lambda b, i: (b * nq + i, 0)),
                  kv, kv, kv,
                  pl.BlockSpec((8, BIAS_PAD + 256, MOBA_L), lambda b, i: (0, 0, 0))],
        out_specs=pl.BlockSpec((tq, 512), lambda b, i: (b * nq + i, 0)),
        out_shape=jax.ShapeDtypeStruct((B * T, SWA_HEADS * HD), BF16),
        compiler_params=_cparams(("arbitrary", "arbitrary")),
        name="swa_attention",
    )(sinks, q8, k, v, vsw, tb)


def _outproj_router_kernel(h_ref, oa_ref, ob_ref, wo_ref, g_ref, wr_ref, br_ref,
                           hn_ref, u_ref, ri_ref, rg_ref, cnt_ref, base_ref):
    i = pl.program_id(0)

    @pl.when(i == 0)
    def _():
        base_ref[...] = jnp.zeros_like(base_ref)

    hn = h_ref[...] + _dot(oa_ref[...], wo_ref[0:512, :]) + _dot(ob_ref[...], wo_ref[512:1024, :])
    hn_ref[...] = hn
    u = _rms(hn, g_ref[...])
    u_ref[...] = u.astype(BF16)
    logits = jnp.dot(u, wr_ref[...], preferred_element_type=F32, precision=lax.Precision.HIGHEST) + br_ref[...]
    tm = logits.shape[0]
    lane = lax.broadcasted_iota(jnp.int32, (tm, 128), 1).astype(F32)
    big = 1e6
    isg = (lane >= N_EXP) & (lane < N_EXP + N_GROUPS)
    gl = jnp.where(isg, logits, NEG)
    gmax = jnp.max(gl, axis=-1, keepdims=True)
    gsel = jnp.min(jnp.where(gl == gmax, lane, big), axis=-1, keepdims=True) - N_EXP
    g_gate = 1.0 / jnp.sum(jnp.where(isg, jnp.exp(gl - gmax), 0.0), axis=-1, keepdims=True)
    ine = (lane >= gsel * EPG) & (lane < gsel * EPG + EPG)
    el = jnp.where(ine, logits, NEG)
    m1 = jnp.max(el, axis=-1, keepdims=True)
    e1 = jnp.min(jnp.where(el == m1, lane, big), axis=-1, keepdims=True)
    el2 = jnp.where(lane == e1, NEG, el)
    m2 = jnp.max(el2, axis=-1, keepdims=True)
    e2 = jnp.min(jnp.where(el2 == m2, lane, big), axis=-1, keepdims=True)
    r = jnp.exp(m2 - m1)
    gate1 = g_gate / (1.0 + r)
    gate2 = g_gate * r / (1.0 + r)
    oh1 = (lane == e1).astype(F32)
    oh2 = (lane == e2).astype(F32)
    oh = oh1 + oh2
    rt = lax.broadcasted_iota(jnp.int32, (tm, tm), 0)
    ct = lax.broadcasted_iota(jnp.int32, (tm, tm), 1)
    lower = jnp.where(ct < rt, 1.0, 0.0).astype(BF16)
    prefix = _dot(lower, oh.astype(BF16)) + base_ref[...]
    rank1 = jnp.sum(oh1 * prefix, axis=-1, keepdims=True)
    rank2 = jnp.sum(oh2 * prefix, axis=-1, keepdims=True)
    base_ref[...] = base_ref[...] + jnp.sum(oh, axis=0, keepdims=True)
    cnt_ref[...] = base_ref[...]
    ri = jnp.where(lane == 0, e1, jnp.where(lane == 1, e2, 0.0))
    ri = jnp.where(lane == 2, rank1, jnp.where(lane == 3, rank2, ri))
    ri_ref[...] = ri.astype(jnp.int32)
    rg_ref[...] = jnp.where(lane == 0, gate1, jnp.where(lane == 1, gate2, 0.0))


def _outproj_router(h, oa, ob, wo, g, wr, br):
    N = h.shape[0]
    full = lambda a: pl.BlockSpec(a.shape, lambda i: (0,) * a.ndim)
    row = lambda w: pl.BlockSpec((TM, w), lambda i: (i, 0))
    return pl.pallas_call(
        _outproj_router_kernel,
        grid=(N // TM,),
        in_specs=[row(D), row(512), row(512), full(wo), full(g), full(wr), full(br)],
        out_specs=[row(D), row(D), row(128), row(128), pl.BlockSpec((1, 128), lambda i: (0, 0))],
        out_shape=[jax.ShapeDtypeStruct((N, D), F32), jax.ShapeDtypeStruct((N, D), BF16),
                   jax.ShapeDtypeStruct((N, 128), jnp.int32), jax.ShapeDtypeStruct((N, 128), F32),
                   jax.ShapeDtypeStruct((1, 128), F32)],
        scratch_shapes=[pltpu.VMEM((1, 128), F32)],
        compiler_params=_cparams(("arbitrary",)),
        name="outproj_router",
    )(h, oa, ob, wo, g, wr, br)


def _moe_kernel(te_ref, nu_ref, x_ref, wgu_ref, wd_ref, o_ref):
    i = pl.program_id(0)

    @pl.when(i < nu_ref[0])
    def _():
        hgu = _dot(x_ref[...], wgu_ref[...])
        hg, hu = hgu[:, :D_EXP], hgu[:, D_EXP:]
        a = hg * (1.0 / (1.0 + jnp.exp(-hg))) * hu
        o_ref[...] = _dot(a.astype(BF16), wd_ref[...]).astype(BF16)

    @pl.when(i >= nu_ref[0])
    def _():
        o_ref[...] = jnp.zeros_like(o_ref)


def _moe_experts(xs, tile_expert, num_used, wgu, wd):
    P = xs.shape[0]
    grid_spec = pltpu.PrefetchScalarGridSpec(
        num_scalar_prefetch=2,
        grid=(P // TM_E,),
        in_specs=[pl.BlockSpec((TM_E, D), lambda i, te, nu: (i, 0)),
                  pl.BlockSpec((None, D, 2 * D_EXP), lambda i, te, nu: (te[i], 0, 0)),
                  pl.BlockSpec((None, D_EXP, D), lambda i, te, nu: (te[i], 0, 0))],
        out_specs=pl.BlockSpec((TM_E, D), lambda i, te, nu: (i, 0)),
    )
    return pl.pallas_call(
        _moe_kernel,
        grid_spec=grid_spec,
        out_shape=jax.ShapeDtypeStruct((P, D), BF16),
        compiler_params=_cparams(("arbitrary",)),
        name="moe_experts",
    )(tile_expert, num_used, xs, wgu, wd)


def _dispatch(u, ri, cnt):
    N = u.shape[0]
    counts = cnt[0, :N_EXP].astype(jnp.int32)
    padded = ((counts + TM_E - 1) // TM_E) * TM_E
    ends = jnp.cumsum(padded)
    offs = ends - padded
    e1, e2, r1, r2 = ri[:, 0], ri[:, 1], ri[:, 2], ri[:, 3]
    d1 = offs[e1] + r1
    d2 = offs[e2] + r2
    P = 2 * N + N_EXP * TM_E
    tok = jnp.arange(N, dtype=jnp.int32)
    row_token = jnp.zeros((P,), jnp.int32).at[d1].set(tok).at[d2].set(tok)
    tile_start = jnp.arange(P // TM_E, dtype=jnp.int32) * TM_E
    tile_expert = jnp.minimum(jnp.searchsorted(ends, tile_start, side="right"), N_EXP - 1).astype(jnp.int32)
    num_used = (ends[-1:] // TM_E).astype(jnp.int32)
    xs = jnp.take(u, row_token, axis=0)
    return xs, tile_expert, num_used, d1, d2


def _combine(h_ref, y1_ref, y2_ref, rg_ref):
    rg = rg_ref[...]
    return h_ref[...] + rg[:, 0:1] * y1_ref[...].astype(F32) + rg[:, 1:2] * y2_ref[...].astype(F32)


ODD_WIDTHS = (1024, 1024, 512, 512, 512, 512)


def _combine_proj_odd_kernel(h_ref, y1_ref, y2_ref, rg_ref, g_ref, win_ref, ind_ref, hn_ref, *out_refs):
    hn = _combine(h_ref, y1_ref, y2_ref, rg_ref)
    hn_ref[...] = hn
    u = _rms(hn, g_ref[...]).astype(BF16)
    c0 = 0
    for n, o_ref in enumerate(out_refs):
        w = ODD_WIDTHS[n]
        if n == 1:
            for hh in range(8):
                o_ref[:, 128 * hh:128 * (hh + 1)] = (
                    _dot(u, win_ref[:, c0 + 128 * hh:c0 + 128 * (hh + 1)]) + ind_ref[...]).astype(BF16)
        else:
            for c in range(0, w, 512):
                o_ref[:, c:c + 512] = _dot(u, win_ref[:, c0 + c:c0 + c + 512]).astype(BF16)
        c0 += w


def _odd_weights(w_in):
    idx, scale = [], []
    for base, s in ((0, HD ** -0.5), (512, 1.0)):
        for hh in range(8):
            idx += list(range(base + 64 * hh, base + 64 * hh + 64)) + [-1] * 64
            scale += [s] * 128
    idx += list(range(1024, 3072))
    scale += [1.0] * 512 + [HD ** -0.5] * 512 + [1.0] * 1024
    return _cols(w_in, idx, jnp.asarray(np.asarray(scale, np.float32))).astype(BF16)


def _block_indicator(T):
    assert T // MOBA_L <= 64
    ind = np.zeros((T, 128), np.float32)
    ind[np.arange(T), 64 + np.arange(T) // MOBA_L] = 1.0
    return jnp.asarray(ind)


def _combine_proj_odd(h, y1, y2, rg, g, win, ind, T):
    N = h.shape[0]
    nt = T // TM
    full = lambda a: pl.BlockSpec(a.shape, lambda i: (0,) * a.ndim)
    row = lambda w: pl.BlockSpec((TM, w), lambda i: (i, 0))
    return pl.pallas_call(
        _combine_proj_odd_kernel,
        grid=(N // TM,),
        in_specs=[row(D), row(D), row(D), row(128), full(g), full(win),
                  pl.BlockSpec((TM, 128), lambda i: (i % nt, 0))],
        out_specs=[row(D)] + [row(w) for w in ODD_WIDTHS],
        out_shape=[jax.ShapeDtypeStruct((N, D), F32)] + [jax.ShapeDtypeStruct((N, w), BF16) for w in ODD_WIDTHS],
        compiler_params=_cparams(("arbitrary",)),
        name="combine_proj_odd",
    )(h, y1, y2, rg, g, win, ind)


def _combine_final_kernel(h_ref, y1_ref, y2_ref, rg_ref, g_ref, o_ref):
    o_ref[...] = _rms(_combine(h_ref, y1_ref, y2_ref, rg_ref), g_ref[...])


def _combine_final(h, y1, y2, rg, g):
    N = h.shape[0]
    row = lambda w: pl.BlockSpec((TM, w), lambda i: (i, 0))
    return pl.pallas_call(
        _combine_final_kernel,
        grid=(N // TM,),
        in_specs=[row(D), row(D), row(D), row(128), pl.BlockSpec((1, D), lambda i: (0, 0))],
        out_specs=row(D),
        out_shape=jax.ShapeDtypeStruct((N, D), F32),
        compiler_params=_cparams(("arbitrary",)),
        name="combine_final",
    )(h, y1, y2, rg, g)


def _moba_kernel(q_ref, k_ref, v_ref, tb_ref, o_ref, kmean_ref, qa_ref, *, nkb):
    i = pl.program_id(2)
    L, tq = MOBA_L, MOBA_TQ

    @pl.when(i == 0)
    def _():
        kmean_ref[...] = jnp.zeros_like(kmean_ref)
        for n in range(nkb):
            kmean_ref[64 + n:65 + n, :] = jnp.sum(k_ref[n * L:(n + 1) * L, :].astype(F32), axis=0,
                                                  keepdims=True) * (1.0 / L)

    lane = lax.broadcasted_iota(jnp.int32, (tq, 128), 1)
    rowi = lax.broadcasted_iota(jnp.int32, (tq, 128), 0)
    own = 2 * i + jnp.where(rowi >= L, 1, 0)
    blk = lane - 64
    lane_f = lane.astype(F32)
    for j in range(2):
        qj = q_ref[:, 128 * j:128 * (j + 1)]
        gsc = lax.dot_general(qj.astype(F32), kmean_ref[:, 128 * j:128 * (j + 1)], (((1,), (1,)), ((), ())),
                              preferred_element_type=F32, precision=lax.Precision.HIGHEST)
        g = jnp.where((blk >= 0) & (blk < own), gsc, NEG)
        allowed = blk == own
        for _ in range(MOBA_TOPK):
            mx = jnp.max(g, axis=-1, keepdims=True)
            idx = jnp.min(jnp.where(g == mx, lane_f, 1e6), axis=-1, keepdims=True)
            pick = lane_f == idx
            allowed = allowed | (pick & (mx > 0.5 * NEG))
            g = jnp.where(pick, 2.0 * NEG, g)
        qa_ref[j] = jnp.where(lane < HD, qj, jnp.where(allowed, 0.0, NEG).astype(BF16))

    row = lax.broadcasted_iota(jnp.int32, (tq, tq), 0)
    col = lax.broadcasted_iota(jnp.int32, (tq, tq), 1)

    def step(n, carry, diag):
        start = pl.multiple_of(n * tq, tq)
        v = v_ref[pl.ds(start, tq), :]
        off = (i - n) * tq
        b0 = pl.multiple_of(jnp.minimum(off, BIAS_CONST_FROM) + BIAS_PAD, L)
        b1 = pl.multiple_of(jnp.minimum(off - L, BIAS_CONST_FROM) + BIAS_PAD, L)
        out = []
        for j in range(2):
            k = k_ref[pl.ds(start, tq), 128 * j:128 * (j + 1)]
            bias = jnp.concatenate([tb_ref[j, pl.ds(b0, tq), :], tb_ref[j, pl.ds(b1, tq), :]], axis=1)
            s = _dot_nt(qa_ref[j], k) + bias
            if diag:
                s = jnp.where(col <= row, s, NEG)
            out.append(_flash_update(s, v, *carry[j]))
        return tuple(out)

    init = tuple((jnp.full((tq, 1), NEG, F32), jnp.zeros((tq, 1), F32), jnp.zeros((tq, 128), F32))
                 for _ in range(2))
    carry = step(i, init, True)
    carry = lax.fori_loop(0, i, lambda n, c: step(n, c, False), carry)
    o0 = carry[0][2] / carry[0][1]
    o1 = carry[1][2] / carry[1][1]
    o_ref[...] = jnp.where(lane < HD, o0, o1).astype(BF16)


def _moba_attention(q, k, v, tb, B, T):
    tq = MOBA_TQ
    nq = T // tq
    return pl.pallas_call(
        functools.partial(_moba_kernel, nkb=T // MOBA_L),
        grid=(B, 4, nq),
        in_specs=[pl.BlockSpec((tq, 256), lambda b, p, i: (b * nq + i, p)),
                  pl.BlockSpec((T, 256), lambda b, p, i: (b, p)),
                  pl.BlockSpec((T, 128), lambda b, p, i: (b, p)),
                  pl.BlockSpec((2, BIAS_ROWS, MOBA_L), lambda b, p, i: (p, 0, 0))],
        out_specs=pl.BlockSpec((tq, 128), lambda b, p, i: (b * nq + i, p)),
        out_shape=jax.ShapeDtypeStruct((B * T, 512), BF16),
        scratch_shapes=[pltpu.VMEM((128, 256), F32), pltpu.VMEM((2, tq, 128), BF16)],
        compiler_params=_cparams(("arbitrary", "arbitrary", "arbitrary")),
        name="moba_attention",
    )(q, k, v, tb)


def _sb_kernel(q_ref, k_ref, v_ref, o_ref, *, tq):
    i = pl.program_id(2)
    lane = lax.broadcasted_iota(jnp.int32, (tq, 128), 1)
    q_all = q_ref[...]
    qs = [jnp.where(lane < HD, q_all, 0).astype(BF16), jnp.where(lane >= HD, q_all, 0).astype(BF16)]
    row = lax.broadcasted_iota(jnp.int32, (tq, tq), 0)
    col = lax.broadcasted_iota(jnp.int32, (tq, tq), 1)
    past = col < row
    after_mat = jnp.where(row > col, 1.0, 0.0).astype(BF16)

    def step(kb, carry, diag):
        start = pl.multiple_of(kb * tq, tq)
        k = k_ref[pl.ds(start, tq), :]
        v = v_ref[pl.ds(start, tq), :]
        out = []
        for j in range(2):
            c, acc = carry[j]
            z = _dot_nt(qs[j], k)
            sp = jnp.log(1.0 + jnp.exp(-jnp.abs(z)))
            log1m = -(jnp.maximum(z, 0.0) + sp)
            logsig = jnp.minimum(z, 0.0) - sp
            if diag:
                log1m = jnp.where(past, log1m, 0.0)
            hi = log1m.astype(BF16)
            lo = (log1m - hi.astype(F32)).astype(BF16)
            after = c + _dot(hi, after_mat) + _dot(lo, after_mat)
            w = jnp.exp(logsig + after)
            if diag:
                w = jnp.where(past, w, 0.0)
            acc = acc + _dot(w.astype(BF16), v)
            c = c + jnp.sum(log1m, axis=-1, keepdims=True)
            out.append((c, acc))
        return tuple(out)

    def live(carry):
        return (jnp.maximum(jnp.max(carry[0][0]), jnp.max(carry[1][0])) > SB_DONE).astype(jnp.int32)

    def body(state):
        t, _, carry = state
        carry = step(i - 1 - t, carry, False)
        return t + 1, live(carry), carry

    init = tuple((jnp.zeros((tq, 1), F32), jnp.zeros((tq, 128), F32)) for _ in range(2))
    carry = step(i, init, True)
    _, _, carry = lax.while_loop(lambda st: (st[0] < i) & (st[1] > 0), body, (jnp.int32(0), live(carry), carry))
    o_ref[...] = jnp.where(lane < HD, carry[0][1], carry[1][1]).astype(BF16)


def _sb_attention(q, k, v, B, T, tq=256):
    nq = T // tq
    kv = pl.BlockSpec((T, 128), lambda b, p, i: (b, p))
    return pl.pallas_call(
        functools.partial(_sb_kernel, tq=tq),
        grid=(B, 4, nq),
        in_specs=[pl.BlockSpec((tq, 128), lambda b, p, i: (b * nq + i, p)), kv, kv],
        out_specs=pl.BlockSpec((tq, 128), lambda b, p, i: (b * nq + i, p)),
        out_shape=jax.ShapeDtypeStruct((B * T, 512), BF16),
        compiler_params=_cparams(("arbitrary", "arbitrary", "arbitrary")),
        name="sb_attention",
    )(q, k, v)


def _router_weights(w_group, b_group, w_router, b_router):
    pad = 128 - N_EXP - N_GROUPS
    wr = jnp.concatenate([w_router, w_group, jnp.zeros((D, pad), F32)], axis=1)
    br = jnp.concatenate([b_router, b_group, jnp.zeros((pad,), F32)])[None, :]
    return wr, br


def _ffn(h, oa, ob, wo, g, w_group, b_group, w_router, b_router, w_gate, w_up, w_down):
    wr, br = _router_weights(w_group, b_group, w_router, b_router)
    hn, u, ri, rg, cnt = _outproj_router(h, oa, ob, wo.astype(BF16), g[None, :], wr, br)
    xs, tile_expert, num_used, d1, d2 = _dispatch(u, ri, cnt)
    wgu = jnp.concatenate([w_gate, w_up], axis=-1).astype(BF16)
    ys = _moe_experts(xs, tile_expert, num_used, wgu, w_down.astype(BF16))
    return hn, jnp.take(ys, d1, axis=0), jnp.take(ys, d2, axis=0), rg


def kernel(x, norm_mix, norm_ffn, norm_final, rel_bias_table, w_in_even, g_mla_q, g_mla_kv, w_mla_qb, w_mla_kvb, swa_sinks, w_out_even, w_in_odd, w_out_odd, moe_w_group, moe_b_group, moe_w_router, moe_b_router, moe_w_gate, moe_w_up, moe_w_down):
    B, T, _ = x.shape
    h = x.reshape(B * T, D)
    tb = _bias_table(rel_bias_table)

    win, wqa, wqb, wk, wv = _even_weights(w_in_even[0], w_mla_qb[0], w_mla_kvb[0])
    q, k, v, qs8, ks, vs, vsw = _proj_even(h, norm_mix[0][None, :], win, g_mla_q[0][None, :], g_mla_kv[0][None, :],
                                           wqa, wqb, wk, wv, _rope_tables(T), T)
    oa = _mla_attention(q, k, v, B, T)
    ob = _swa_attention(qs8, ks, vs, vsw, swa_sinks[0], tb, B, T)
    h, y1, y2, rg = _ffn(h, oa, ob, w_out_even[0], norm_ffn[0], moe_w_group[0], moe_b_group[0], moe_w_router[0],
                         moe_b_router[0], moe_w_gate[0], moe_w_up[0], moe_w_down[0])

    h, qc, kc, vc, qd, kd, vd = _combine_proj_odd(h, y1, y2, rg, norm_mix[1][None, :], _odd_weights(w_in_odd[0]),
                                                  _block_indicator(T), T)
    oc = _moba_attention(qc, kc, vc, tb, B, T)
    od = _sb_attention(qd, kd, vd, B, T)
    h, y1, y2, rg = _ffn(h, oc, od, w_out_odd[0], norm_ffn[1], moe_w_group[1], moe_b_group[1], moe_w_router[1],
                         moe_b_router[1], moe_w_gate[1], moe_w_up[1], moe_w_down[1])
    out = _combine_final(h, y1, y2, rg, norm_final[None, :])
    return out.reshape(B, T, D)
```

```python
import functools
import math

import numpy as np
import jax
import jax.numpy as jnp
from jax import lax
from jax.experimental import pallas as pl
from jax.experimental.pallas import tpu as pltpu
from jax.experimental.pallas import tpu_sc as plsc

F32 = jnp.float32
BF16 = jnp.bfloat16
NEG = -1e30
EPS = 1e-6

D = 1024
HD = 64
LANES = 128
MLA_HEADS, MLA_QR, MLA_KVR, MLA_NOPE, MLA_ROPE, MLA_V = 8, 256, 128, 64, 32, 64
ROPE_THETA = 10000.0
SWA_HEADS, SWA_KV, SWA_WIN = 8, 2, 128
MOBA_L, MOBA_TOPK = 256, 3
REL_BUCKETS, REL_MAX = 32, 2048
N_GROUPS, EPG, N_EXP, D_EXP = 4, 8, 32, 256
BIAS_CONST_FROM = 1792
MOBA_TQ = 2 * MOBA_L
BIAS_PAD = MOBA_L
BIAS_ROWS = BIAS_PAD + BIAS_CONST_FROM + MOBA_TQ
SB_DONE = -110.0
TM = 512
TM_E = 256
VMEM_LIMIT = 56 * 1024 * 1024
SC_CORES, SC_SUBCORES = 2, 16
SC_CHUNK = 128
HALF = D // 2


def _pack_halves(x):
    lo = pltpu.bitcast(x[:, :HALF].astype(BF16).astype(F32), jnp.uint32) >> 16
    hi = pltpu.bitcast(x[:, HALF:].astype(BF16).astype(F32), jnp.uint32) & jnp.uint32(0xFFFF0000)
    return pltpu.bitcast(lo | hi, jnp.int32)


def _unpack_halves(w):
    w = pltpu.bitcast(w, jnp.uint32)
    return pltpu.bitcast(w << 16, F32), pltpu.bitcast(w & jnp.uint32(0xFFFF0000), F32)


def _cparams(sem):
    return pltpu.CompilerParams(dimension_semantics=sem, vmem_limit_bytes=VMEM_LIMIT)


def _dot(a, b):
    return jnp.dot(a, b, preferred_element_type=F32)


def _dot_nt(a, b):
    return lax.dot_general(a, b, (((1,), (1,)), ((), ())), preferred_element_type=F32)


def _rms(x, g):
    return x * lax.rsqrt(jnp.mean(x * x, axis=-1, keepdims=True) + EPS) * g


def _flash_update(s, v, m, l, acc):
    m_new = jnp.maximum(m, jnp.max(s, axis=-1, keepdims=True))
    alpha = jnp.exp(m - m_new)
    p = jnp.exp(s - m_new)
    l = alpha * l + jnp.sum(p, axis=-1, keepdims=True)
    acc = alpha * acc + _dot(p.astype(BF16), v)
    return m_new, l, acc


def _t5_bucket_np(dist):
    n = np.maximum(dist, 0)
    max_exact = REL_BUCKETS // 2
    nf = np.maximum(n, 1).astype(np.float32)
    large = max_exact + (np.log(nf / np.float32(max_exact)) / np.float32(math.log(REL_MAX / max_exact))
                         * np.float32(REL_BUCKETS - max_exact)).astype(np.int32)
    large = np.minimum(large, REL_BUCKETS - 1)
    return np.where(n < max_exact, n, large).astype(np.int32)


def _bias_kernel(table_ref, bucket_ref, out_ref):
    b = bucket_ref[0]
    for h in range(8):
        acc = jnp.zeros(b.shape, F32)
        for k in range(REL_BUCKETS):
            acc = jnp.where(b == k, table_ref[k, h], acc)
        out_ref[h, 0] = acc


def _bias_table(rel_table):
    nt = BIAS_ROWS // 128
    r = np.arange(BIAS_ROWS)[:, None] - BIAS_PAD - np.arange(MOBA_L)[None, :]
    buckets = jnp.asarray(_t5_bucket_np(r).reshape(nt, 128, MOBA_L))
    out = pl.pallas_call(
        _bias_kernel,
        grid=(nt,),
        in_specs=[pl.BlockSpec(memory_space=pltpu.SMEM),
                  pl.BlockSpec((1, 128, MOBA_L), lambda m: (m, 0, 0))],
        out_specs=pl.BlockSpec((8, 1, 128, MOBA_L), lambda m: (0, m, 0, 0)),
        out_shape=jax.ShapeDtypeStruct((8, nt, 128, MOBA_L), F32),
        compiler_params=_cparams(("arbitrary",)),
        name="bias_table",
    )(rel_table, buckets)
    return out.reshape(8, BIAS_ROWS, MOBA_L)


def _proj_even_kernel(h_ref, g_ref, win_ref, gq_ref, gkv_ref, wqa_ref, wqb_ref, wk_ref, wv_ref,
                      cq_ref, sq_ref, ck_ref, sk_ref,
                      q_ref, k_ref, v_ref, qs_ref, ks_ref, vs_ref, vsw_ref):
    u = _rms(h_ref[...], g_ref[...]).astype(BF16)
    nq = _rms(_dot(u, win_ref[:, 0:256]), gq_ref[...]).astype(BF16)
    qa = _dot(nq, wqa_ref[...])
    qb = _dot(nq, wqb_ref[...])
    cq, sq = cq_ref[...], sq_ref[...]
    for hh in range(MLA_HEADS):
        sl = slice(128 * hh, 128 * (hh + 1))
        q_ref[:, sl] = (qa[:, sl] * cq + qb[:, sl] * sq).astype(BF16)
    nkv = _rms(_dot(u, win_ref[:, 256:384]), gkv_ref[...]).astype(BF16)
    kn = _dot(nkv, wk_ref[...])
    v_ref[...] = _dot(nkv, wv_ref[...]).astype(BF16)
    kr = _dot(u, win_ref[:, 1792:1920]) * ck_ref[...] + _dot(u, win_ref[:, 1920:2048]) * sk_ref[...]
    for hh in range(MLA_HEADS):
        sl = slice(128 * hh, 128 * (hh + 1))
        k_ref[:, sl] = (kn[:, sl] + kr).astype(BF16)
    qs_ref[...] = _dot(u, win_ref[:, 384:1408]).astype(BF16)
    ks_ref[...] = _dot(u, win_ref[:, 1408:1536]).astype(BF16)
    vs_ref[...] = _dot(u, win_ref[:, 1536:1664]).astype(BF16)
    vsw_ref[...] = _dot(u, win_ref[:, 1664:1792]).astype(BF16)


def _cols(w, idx, scale=None):
    idx = np.asarray(idx)
    out = jnp.take(w, jnp.asarray(np.maximum(idx, 0)), axis=1) * jnp.asarray((idx >= 0).astype(np.float32))
    if scale is not None:
        out = out * scale
    return out


def _even_weights(w_in, w_qb, w_kvb):
    o_cq, o_ckv, o_kr = 0, MLA_QR, MLA_QR + MLA_KVR
    o_qs = o_kr + MLA_ROPE
    o_ks = o_qs + SWA_HEADS * HD
    o_vs = o_ks + SWA_KV * HD
    idx = list(range(o_cq, o_cq + MLA_QR)) + list(range(o_ckv, o_ckv + MLA_KVR))
    scale = [1.0] * len(idx)
    for hq in range(SWA_HEADS):
        c = hq // (SWA_HEADS // SWA_KV)
        blk = [-1] * 128
        blk[64 * c:64 * c + 64] = range(o_qs + 64 * hq, o_qs + 64 * hq + 64)
        idx += blk
        scale += [HD ** -0.5] * 128
    idx += list(range(o_ks, o_ks + 128))
    idx += list(range(o_vs, o_vs + 128))
    idx += list(range(o_vs + 64, o_vs + 128)) + list(range(o_vs, o_vs + 64))
    scale += [1.0] * 384
    half = MLA_ROPE // 2
    idx += [-1] * 64 + list(range(o_kr, o_kr + MLA_ROPE)) + [-1] * 32
    idx += [-1] * 64 + list(range(o_kr + half, o_kr + MLA_ROPE)) + list(range(o_kr, o_kr + half)) + [-1] * 32
    scale += [1.0] * 256
    win = _cols(w_in, idx, jnp.asarray(np.asarray(scale, np.float32))).astype(BF16)
    qw = MLA_NOPE + MLA_ROPE
    ia, ib, ik, iv = [], [], [], []
    for h in range(MLA_HEADS):
        ia += list(range(qw * h, qw * h + qw)) + [-1] * 32
        ib += [-1] * 64 + list(range(qw * h + MLA_NOPE + half, qw * h + qw)) \
            + list(range(qw * h + MLA_NOPE, qw * h + MLA_NOPE + half)) + [-1] * 32
        ik += list(range(128 * h, 128 * h + MLA_NOPE)) + [-1] * 64
        iv += list(range(128 * h + MLA_NOPE, 128 * h + 128))
    return (win, _cols(w_qb, ia).astype(BF16), _cols(w_qb, ib).astype(BF16),
            _cols(w_kvb, ik).astype(BF16), _cols(w_kvb, iv).astype(BF16))


def _rope_tables(T):
    half = MLA_ROPE // 2
    freqs = ROPE_THETA ** (-jnp.arange(0, MLA_ROPE, 2, dtype=F32) / MLA_ROPE)
    ang = jnp.arange(T, dtype=F32)[:, None] * freqs[None, :]
    cos, sin = jnp.cos(ang), jnp.sin(ang)
    z = lambda n: jnp.zeros((T, n), F32)
    scale = (MLA_NOPE + MLA_ROPE) ** -0.5
    ck = jnp.concatenate([z(64), cos, cos, z(32)], axis=1)
    sk = jnp.concatenate([z(64), -sin, sin, z(32)], axis=1)
    cq = jnp.concatenate([jnp.ones((T, 64), F32), cos, cos, z(32)], axis=1) * scale
    sq = sk * scale
    return cq, sq, ck, sk


def _proj_even(h, g, win, gq, gkv, wqa, wqb, wk, wv, tables, T):
    N = h.shape[0]
    nt = T // TM
    full = lambda a: pl.BlockSpec(a.shape, lambda i: (0,) * a.ndim)
    tab = pl.BlockSpec((TM, 128), lambda i: (i % nt, 0))
    row = lambda w: pl.BlockSpec((TM, w), lambda i: (i, 0))
    widths = (1024, 1024, 512, 1024, 128, 128, 128)
    return pl.pallas_call(
        _proj_even_kernel,
        grid=(N // TM,),
        in_specs=[row(D), full(g), full(win), full(gq), full(gkv), full(wqa), full(wqb), full(wk), full(wv),
                  tab, tab, tab, tab],
        out_specs=[row(w) for w in widths],
        out_shape=[jax.ShapeDtypeStruct((N, w), BF16) for w in widths],
        compiler_params=_cparams(("arbitrary",)),
        name="proj_even",
    )(h, g, win, gq, gkv, wqa, wqb, wk, wv, *tables)


def _mla_kernel(q_ref, k_ref, v_ref, o_ref, *, tq):
    i = pl.program_id(2)
    qs = [q_ref[:, 128 * j:128 * (j + 1)] for j in range(2)]

    def step(kb, carry, diag):
        start = pl.multiple_of(kb * tq, tq)
        v = v_ref[pl.ds(start, tq), :]
        out = []
        for j in range(2):
            k = k_ref[pl.ds(start, tq), 128 * j:128 * (j + 1)]
            s = _dot_nt(qs[j], k)
            if diag:
                row = lax.broadcasted_iota(jnp.int32, s.shape, 0)
                col = lax.broadcasted_iota(jnp.int32, s.shape, 1)
                s = jnp.where(col <= row, s, NEG)
            out.append(_flash_update(s, v, *carry[j]))
        return tuple(out)

    init = tuple((jnp.full((tq, 1), NEG, F32), jnp.zeros((tq, 1), F32), jnp.zeros((tq, 128), F32))
                 for _ in range(2))
    carry = step(i, init, True)
    carry = lax.fori_loop(0, i, lambda kb, c: step(kb, c, False), carry)
    lane = lax.broadcasted_iota(jnp.int32, (tq, 128), 1)
    o0 = carry[0][2] / carry[0][1]
    o1 = carry[1][2] / carry[1][1]
    o_ref[...] = jnp.where(lane < HD, o0, o1).astype(BF16)


def _mla_attention(q, k, v, B, T, tq=512):
    nq = T // tq
    return pl.pallas_call(
        functools.partial(_mla_kernel, tq=tq),
        grid=(B, MLA_HEADS // 2, nq),
        in_specs=[pl.BlockSpec((tq, 256), lambda b, p, i: (b * nq + i, p)),
                  pl.BlockSpec((T, 256), lambda b, p, i: (b, p)),
                  pl.BlockSpec((T, 128), lambda b, p, i: (b, p))],
        out_specs=pl.BlockSpec((tq, 128), lambda b, p, i: (b * nq + i, p)),
        out_shape=jax.ShapeDtypeStruct((B * T, MLA_HEADS * MLA_V), BF16),
        compiler_params=_cparams(("arbitrary", "arbitrary", "arbitrary")),
        name="mla_attention",
    )(q, k, v)


def _swa_kernel(sink_ref, q_ref, k_ref, v_ref, vsw_ref, tb_ref, o_ref, *, tq):
    i = pl.program_id(1)
    G = SWA_HEADS // SWA_KV
    qi = lax.broadcasted_iota(jnp.int32, (128, 256), 0)
    kj = lax.broadcasted_iota(jnp.int32, (128, 256), 1)
    lane = lax.broadcasted_iota(jnp.int32, (128, 128), 1)
    for r in range(tq // 128):
        qstart = i * tq + 128 * r
        kstart = pl.multiple_of(jnp.maximum(qstart - 128, 0), 128)
        off = pl.multiple_of(qstart - kstart, 128)
        dist = off + qi - kj
        valid = (dist >= 0) & (dist < SWA_WIN)
        kw = k_ref[pl.ds(kstart, 256), :]
        vw = v_ref[pl.ds(kstart, 256), :]
        vsw = vsw_ref[pl.ds(kstart, 256), :]
        halves = [None] * SWA_HEADS
        for c in range(SWA_KV):
            qg = jnp.concatenate([q_ref[128 * r:128 * (r + 1), 128 * (G * c + g):128 * (G * c + g + 1)]
                                  for g in range(G)], axis=0)
            s_all = _dot_nt(qg, kw)
            for g in range(G):
                hq = G * c + g
                s = s_all[128 * g:128 * (g + 1)] + tb_ref[hq, pl.ds(off + BIAS_PAD, 128), :]
                s = jnp.where(valid, s, NEG)
                sink = sink_ref[hq]
                m = jnp.maximum(jnp.max(s, axis=-1, keepdims=True), sink)
                p = jnp.exp(s - m)
                denom = jnp.sum(p, axis=-1, keepdims=True) + jnp.exp(sink - m)
                p = (p / denom).astype(BF16)
                halves[hq] = _dot(p, vw if c == hq % 2 else vsw)
        for mp in range(SWA_HEADS // 2):
            o_ref[128 * r:128 * (r + 1), 128 * mp:128 * (mp + 1)] = jnp.where(
                lane < HD, halves[2 * mp], halves[2 * mp + 1]).astype(BF16)


def _swa_attention(q8, k, v, vsw, sinks, tb, B, T, tq=512):
    nq = T // tq
    kv = pl.BlockSpec((T, 128), lambda b, i: (b, 0))
    return pl.pallas_call(
        functools.partial(_swa_kernel, tq=tq),
        grid=(B, nq),
        in_specs=[pl.BlockSpec(memory_space=pltpu.SMEM),
                  pl.BlockSpec((tq, 1024), lambda b, i: (b * nq + i, 0)),
                  kv, kv, kv,
                  pl.BlockSpec((8, BIAS_PAD + 256, MOBA_L), lambda b, i: (0, 0, 0))],
        out_specs=pl.BlockSpec((tq, 512), lambda b, i: (b * nq + i, 0)),
        out_shape=jax.ShapeDtypeStruct((B * T, SWA_HEADS * HD), BF16),
        compiler_params=_cparams(("arbitrary", "arbitrary")),
        name="swa_attention",
    )(sinks, q8, k, v, vsw, tb)


def _outproj_router_kernel(h_ref, oa_ref, ob_ref, wo_ref, g_ref, wr_ref, br_ref,
                           hn_ref, u_ref, ri_ref, rg_ref, cnt_ref, base_ref):
    i = pl.program_id(0)

    @pl.when(i == 0)
    def _():
        base_ref[...] = jnp.zeros_like(base_ref)

    hn = h_ref[...] + _dot(oa_ref[...], wo_ref[0:512, :]) + _dot(ob_ref[...], wo_ref[512:1024, :])
    hn_ref[...] = hn
    u = _rms(hn, g_ref[...])
    u_ref[...] = _pack_halves(u)
    logits = jnp.dot(u, wr_ref[...], preferred_element_type=F32, precision=lax.Precision.HIGHEST) + br_ref[...]
    tm = logits.shape[0]
    lane = lax.broadcasted_iota(jnp.int32, (tm, 128), 1).astype(F32)
    big = 1e6
    isg = (lane >= N_EXP) & (lane < N_EXP + N_GROUPS)
    gl = jnp.where(isg, logits, NEG)
    gmax = jnp.max(gl, axis=-1, keepdims=True)
    gsel = jnp.min(jnp.where(gl == gmax, lane, big), axis=-1, keepdims=True) - N_EXP
    g_gate = 1.0 / jnp.sum(jnp.where(isg, jnp.exp(gl - gmax), 0.0), axis=-1, keepdims=True)
    ine = (lane >= gsel * EPG) & (lane < gsel * EPG + EPG)
    el = jnp.where(ine, logits, NEG)
    m1 = jnp.max(el, axis=-1, keepdims=True)
    e1 = jnp.min(jnp.where(el == m1, lane, big), axis=-1, keepdims=True)
    el2 = jnp.where(lane == e1, NEG, el)
    m2 = jnp.max(el2, axis=-1, keepdims=True)
    e2 = jnp.min(jnp.where(el2 == m2, lane, big), axis=-1, keepdims=True)
    r = jnp.exp(m2 - m1)
    gate1 = g_gate / (1.0 + r)
    gate2 = g_gate * r / (1.0 + r)
    oh1 = (lane == e1).astype(F32)
    oh2 = (lane == e2).astype(F32)
    oh = oh1 + oh2
    rt = lax.broadcasted_iota(jnp.int32, (tm, tm), 0)
    ct = lax.broadcasted_iota(jnp.int32, (tm, tm), 1)
    lower = jnp.where(ct < rt, 1.0, 0.0).astype(BF16)
    prefix = _dot(lower, oh.astype(BF16)) + base_ref[...]
    rank1 = jnp.sum(oh1 * prefix, axis=-1, keepdims=True)
    rank2 = jnp.sum(oh2 * prefix, axis=-1, keepdims=True)
    base_ref[...] = base_ref[...] + jnp.sum(oh, axis=0, keepdims=True)
    cnt_ref[...] = base_ref[...]
    ri = jnp.where(lane == 0, e1, jnp.where(lane == 1, e2, 0.0))
    ri = jnp.where(lane == 2, rank1, jnp.where(lane == 3, rank2, ri))
    ri_ref[...] = ri.astype(jnp.int32)
    rg_ref[...] = jnp.where(lane == 0, gate1, jnp.where(lane == 1, gate2, 0.0))


def _outproj_router(h, oa, ob, wo, g, wr, br):
    N = h.shape[0]
    full = lambda a: pl.BlockSpec(a.shape, lambda i: (0,) * a.ndim)
    row = lambda w: pl.BlockSpec((TM, w), lambda i: (i, 0))
    return pl.pallas_call(
        _outproj_router_kernel,
        grid=(N // TM,),
        in_specs=[row(D), row(512), row(512), full(wo), full(g), full(wr), full(br)],
        out_specs=[row(D), row(HALF), row(128), row(128), pl.BlockSpec((1, 128), lambda i: (0, 0))],
        out_shape=[jax.ShapeDtypeStruct((N, D), F32), jax.ShapeDtypeStruct((N, HALF), jnp.int32),
                   jax.ShapeDtypeStruct((N, 128), jnp.int32), jax.ShapeDtypeStruct((N, 128), F32),
                   jax.ShapeDtypeStruct((1, 128), F32)],
        scratch_shapes=[pltpu.VMEM((1, 128), F32)],
        compiler_params=_cparams(("arbitrary",)),
        name="outproj_router",
    )(h, oa, ob, wo, g, wr, br)


def _moe_kernel(te_ref, tv_ref, x_ref, wgu_ref, wd_ref, o_ref):
    i = pl.program_id(0)
    valid = tv_ref[i]

    @pl.when(valid > 0)
    def _():
        rows = lax.broadcasted_iota(jnp.int32, x_ref.shape, 0)
        lo, hi = _unpack_halves(jnp.where(rows < valid, x_ref[...], 0))
        hgu = _dot(lo.astype(BF16), wgu_ref[:HALF, :]) + _dot(hi.astype(BF16), wgu_ref[HALF:, :])
        hg, hu = hgu[:, :D_EXP], hgu[:, D_EXP:]
        a = hg * (1.0 / (1.0 + jnp.exp(-hg))) * hu
        o_ref[...] = _pack_halves(_dot(a.astype(BF16), wd_ref[...]))

    @pl.when(valid <= 0)
    def _():
        o_ref[...] = jnp.zeros_like(o_ref)


def _moe_experts(xs, tile_expert, tile_valid, wgu, wd):
    P = xs.shape[0]
    grid_spec = pltpu.PrefetchScalarGridSpec(
        num_scalar_prefetch=2,
        grid=(P // TM_E,),
        in_specs=[pl.BlockSpec((TM_E, HALF), lambda i, te, tv: (i, 0)),
                  pl.BlockSpec((None, D, 2 * D_EXP), lambda i, te, tv: (te[i], 0, 0)),
                  pl.BlockSpec((None, D_EXP, D), lambda i, te, tv: (te[i], 0, 0))],
        out_specs=pl.BlockSpec((TM_E, HALF), lambda i, te, tv: (i, 0)),
    )
    return pl.pallas_call(
        _moe_kernel,
        grid_spec=grid_spec,
        out_shape=jax.ShapeDtypeStruct((P, HALF), jnp.int32),
        compiler_params=_cparams(("arbitrary",)),
        name="moe_experts",
    )(tile_expert, tile_valid, xs, wgu, wd)


def _sc_mesh():
    return plsc.VectorSubcoreMesh(core_axis_name="c", subcore_axis_name="s",
                                  num_cores=SC_CORES, num_subcores=SC_SUBCORES)


def _sc_scatter_rows(src, d1, d2, P):
    N, W = src.shape
    per_w = N // (SC_CORES * SC_SUBCORES)

    @functools.partial(pl.kernel, mesh=_sc_mesh(), out_type=jax.ShapeDtypeStruct((P, W), src.dtype),
                       scratch_types=[pltpu.VMEM((SC_CHUNK,), jnp.int32), pltpu.VMEM((SC_CHUNK, W), src.dtype)],
                       name="sc_scatter_rows")
    def k(src_hbm, d1_hbm, d2_hbm, out_hbm, idx_v, rows_v):
        wid = lax.axis_index("s") * SC_CORES + lax.axis_index("c")

        @pl.loop(0, per_w // SC_CHUNK)
        def _(c):
            off = wid * per_w + c * SC_CHUNK
            pltpu.sync_copy(src_hbm.at[pl.ds(off, SC_CHUNK)], rows_v)
            pltpu.sync_copy(d1_hbm.at[pl.ds(off, SC_CHUNK)], idx_v)
            pltpu.sync_copy(rows_v, out_hbm.at[idx_v])
            pltpu.sync_copy(d2_hbm.at[pl.ds(off, SC_CHUNK)], idx_v)
            pltpu.sync_copy(rows_v, out_hbm.at[idx_v])

    return k(src, d1, d2)


def _sc_gather_rows(table, idx):
    B, W = idx.shape[0], table.shape[1]
    per_w = B // (SC_CORES * SC_SUBCORES)

    @functools.partial(pl.kernel, mesh=_sc_mesh(), out_type=jax.ShapeDtypeStruct((B, W), table.dtype),
                       scratch_types=[pltpu.VMEM((SC_CHUNK,), jnp.int32), pltpu.VMEM((SC_CHUNK, W), table.dtype)],
                       name="sc_gather_rows")
    def k(table_hbm, idx_hbm, out_hbm, idx_v, rows_v):
        wid = lax.axis_index("s") * SC_CORES + lax.axis_index("c")

        @pl.loop(0, per_w // SC_CHUNK)
        def _(c):
            off = wid * per_w + c * SC_CHUNK
            pltpu.sync_copy(idx_hbm.at[pl.ds(off, SC_CHUNK)], idx_v)
            pltpu.sync_copy(table_hbm.at[idx_v], rows_v)
            pltpu.sync_copy(rows_v, out_hbm.at[pl.ds(off, SC_CHUNK)])

    return k(table, idx)


def _dispatch_plan(ri, cnt, N):
    counts = cnt[0, :N_EXP].astype(jnp.int32)
    padded = ((counts + TM_E - 1) // TM_E) * TM_E
    ends = jnp.cumsum(padded)
    offs = ends - padded
    take = lambda t, i: t.at[i].get(mode="promise_in_bounds")
    d1 = take(offs, ri[:, 0]) + ri[:, 2]
    d2 = take(offs, ri[:, 1]) + ri[:, 3]
    P = 2 * N + N_EXP * TM_E
    tile_start = jnp.arange(P // TM_E, dtype=jnp.int32) * TM_E
    tile_expert = jnp.minimum(jnp.sum((ends[None, :] <= tile_start[:, None]).astype(jnp.int32), axis=1), N_EXP - 1)
    tile_valid = jnp.clip(take(offs + counts, tile_expert) - tile_start, 0, TM_E)
    return d1, d2, P, tile_expert, tile_valid


def _combine(h_ref, y1_ref, y2_ref, rg_ref):
    rg = rg_ref[...]
    lo1, hi1 = _unpack_halves(y1_ref[...])
    lo2, hi2 = _unpack_halves(y2_ref[...])
    g1, g2 = rg[:, 0:1], rg[:, 1:2]
    return h_ref[...] + jnp.concatenate([g1 * lo1 + g2 * lo2, g1 * hi1 + g2 * hi2], axis=1)


ODD_WIDTHS = (1024, 1024, 512, 512, 512, 512)


def _combine_proj_odd_kernel(h_ref, y1_ref, y2_ref, rg_ref, g_ref, win_ref, ind_ref, hn_ref, *out_refs):
    hn = _combine(h_ref, y1_ref, y2_ref, rg_ref)
    hn_ref[...] = hn
    u = _rms(hn, g_ref[...]).astype(BF16)
    c0 = 0
    for n, o_ref in enumerate(out_refs):
        w = ODD_WIDTHS[n]
        if n == 1:
            for hh in range(8):
                o_ref[:, 128 * hh:128 * (hh + 1)] = (
                    _dot(u, win_ref[:, c0 + 128 * hh:c0 + 128 * (hh + 1)]) + ind_ref[...]).astype(BF16)
        else:
            for c in range(0, w, 512):
                o_ref[:, c:c + 512] = _dot(u, win_ref[:, c0 + c:c0 + c + 512]).astype(BF16)
        c0 += w


def _odd_weights(w_in):
    idx, scale = [], []
    for base, s in ((0, HD ** -0.5), (512, 1.0)):
        for hh in range(8):
            idx += list(range(base + 64 * hh, base + 64 * hh + 64)) + [-1] * 64
            scale += [s] * 128
    idx += list(range(1024, 3072))
    scale += [1.0] * 512 + [HD ** -0.5] * 512 + [1.0] * 1024
    return _cols(w_in, idx, jnp.asarray(np.asarray(scale, np.float32))).astype(BF16)


def _block_indicator(T):
    assert T // MOBA_L <= 64
    ind = np.zeros((T, 128), np.float32)
    ind[np.arange(T), 64 + np.arange(T) // MOBA_L] = 1.0
    return jnp.asarray(ind)


def _y_specs(N):
    return [pl.BlockSpec((TM, HALF), lambda i: (i, 0)), pl.BlockSpec((TM, HALF), lambda i: (i + N // TM, 0))]


def _combine_proj_odd(h, y12, rg, g, win, ind, T):
    N = h.shape[0]
    nt = T // TM
    full = lambda a: pl.BlockSpec(a.shape, lambda i: (0,) * a.ndim)
    row = lambda w: pl.BlockSpec((TM, w), lambda i: (i, 0))
    return pl.pallas_call(
        _combine_proj_odd_kernel,
        grid=(N // TM,),
        in_specs=[row(D)] + _y_specs(N) + [row(128), full(g), full(win),
                  pl.BlockSpec((TM, 128), lambda i: (i % nt, 0))],
        out_specs=[row(D)] + [row(w) for w in ODD_WIDTHS],
        out_shape=[jax.ShapeDtypeStruct((N, D), F32)] + [jax.ShapeDtypeStruct((N, w), BF16) for w in ODD_WIDTHS],
        compiler_params=_cparams(("arbitrary",)),
        name="combine_proj_odd",
    )(h, y12, y12, rg, g, win, ind)


def _combine_final_kernel(h_ref, y1_ref, y2_ref, rg_ref, g_ref, o_ref):
    o_ref[...] = _rms(_combine(h_ref, y1_ref, y2_ref, rg_ref), g_ref[...])


def _combine_final(h, y12, rg, g):
    N = h.shape[0]
    row = lambda w: pl.BlockSpec((TM, w), lambda i: (i, 0))
    return pl.pallas_call(
        _combine_final_kernel,
        grid=(N // TM,),
        in_specs=[row(D)] + _y_specs(N) + [row(128), pl.BlockSpec((1, D), lambda i: (0, 0))],
        out_specs=row(D),
        out_shape=jax.ShapeDtypeStruct((N, D), F32),
        compiler_params=_cparams(("arbitrary",)),
        name="combine_final",
    )(h, y12, y12, rg, g)


def _moba_kernel(q_ref, k_ref, v_ref, tb_ref, o_ref, kmean_ref, qa_ref, *, nkb):
    i = pl.program_id(2)
    L, tq = MOBA_L, MOBA_TQ

    @pl.when(i == 0)
    def _():
        kmean_ref[...] = jnp.zeros_like(kmean_ref)
        for n in range(nkb):
            kmean_ref[64 + n:65 + n, :] = jnp.sum(k_ref[n * L:(n + 1) * L, :].astype(F32), axis=0,
                                                  keepdims=True) * (1.0 / L)

    lane = lax.broadcasted_iota(jnp.int32, (tq, 128), 1)
    rowi = lax.broadcasted_iota(jnp.int32, (tq, 128), 0)
    own = 2 * i + jnp.where(rowi >= L, 1, 0)
    blk = lane - 64
    lane_f = lane.astype(F32)
    for j in range(2):
        qj = q_ref[:, 128 * j:128 * (j + 1)]
        gsc = lax.dot_general(qj.astype(F32), kmean_ref[:, 128 * j:128 * (j + 1)], (((1,), (1,)), ((), ())),
                              preferred_element_type=F32, precision=lax.Precision.HIGHEST)
        g = jnp.where((blk >= 0) & (blk < own), gsc, NEG)
        allowed = blk == own
        for _ in range(MOBA_TOPK):
            mx = jnp.max(g, axis=-1, keepdims=True)
            idx = jnp.min(jnp.where(g == mx, lane_f, 1e6), axis=-1, keepdims=True)
            pick = lane_f == idx
            allowed = allowed | (pick & (mx > 0.5 * NEG))
            g = jnp.where(pick, 2.0 * NEG, g)
        qa_ref[j] = jnp.where(lane < HD, qj, jnp.where(allowed, 0.0, NEG).astype(BF16))

    row = lax.broadcasted_iota(jnp.int32, (tq, tq), 0)
    col = lax.broadcasted_iota(jnp.int32, (tq, tq), 1)

    def step(n, carry, diag):
        start = pl.multiple_of(n * tq, tq)
        v = v_ref[pl.ds(start, tq), :]
        off = (i - n) * tq
        b0 = pl.multiple_of(jnp.minimum(off, BIAS_CONST_FROM) + BIAS_PAD, L)
        b1 = pl.multiple_of(jnp.minimum(off - L, BIAS_CONST_FROM) + BIAS_PAD, L)
        out = []
        for j in range(2):
            k = k_ref[pl.ds(start, tq), 128 * j:128 * (j + 1)]
            bias = jnp.concatenate([tb_ref[j, pl.ds(b0, tq), :], tb_ref[j, pl.ds(b1, tq), :]], axis=1)
            s = _dot_nt(qa_ref[j], k) + bias
            if diag:
                s = jnp.where(col <= row, s, NEG)
            out.append(_flash_update(s, v, *carry[j]))
        return tuple(out)

    init = tuple((jnp.full((tq, 1), NEG, F32), jnp.zeros((tq, 1), F32), jnp.zeros((tq, 128), F32))
                 for _ in range(2))
    carry = step(i, init, True)
    carry = lax.fori_loop(0, i, lambda n, c: step(n, c, False), carry)
    o0 = carry[0][2] / carry[0][1]
    o1 = carry[1][2] / carry[1][1]
    o_ref[...] = jnp.where(lane < HD, o0, o1).astype(BF16)


def _moba_attention(q, k, v, tb, B, T):
    tq = MOBA_TQ
    nq = T // tq
    return pl.pallas_call(
        functools.partial(_moba_kernel, nkb=T // MOBA_L),
        grid=(B, 4, nq),
        in_specs=[pl.BlockSpec((tq, 256), lambda b, p, i: (b * nq + i, p)),
                  pl.BlockSpec((T, 256), lambda b, p, i: (b, p)),
                  pl.BlockSpec((T, 128), lambda b, p, i: (b, p)),
                  pl.BlockSpec((2, BIAS_ROWS, MOBA_L), lambda b, p, i: (p, 0, 0))],
        out_specs=pl.BlockSpec((tq, 128), lambda b, p, i: (b * nq + i, p)),
        out_shape=jax.ShapeDtypeStruct((B * T, 512), BF16),
        scratch_shapes=[pltpu.VMEM((128, 256), F32), pltpu.VMEM((2, tq, 128), BF16)],
        compiler_params=_cparams(("arbitrary", "arbitrary", "arbitrary")),
        name="moba_attention",
    )(q, k, v, tb)


def _sb_kernel(q_ref, k_ref, v_ref, o_ref, *, tq):
    i = pl.program_id(2)
    lane = lax.broadcasted_iota(jnp.int32, (tq, 128), 1)
    q_all = q_ref[...]
    qs = [jnp.where(lane < HD, q_all, 0).astype(BF16), jnp.where(lane >= HD, q_all, 0).astype(BF16)]
    row = lax.broadcasted_iota(jnp.int32, (tq, tq), 0)
    col = lax.broadcasted_iota(jnp.int32, (tq, tq), 1)
    past = col < row
    after_mat = jnp.where(row > col, 1.0, 0.0).astype(BF16)

    def step(kb, carry, diag):
        start = pl.multiple_of(kb * tq, tq)
        k = k_ref[pl.ds(start, tq), :]
        v = v_ref[pl.ds(start, tq), :]
        out = []
        for j in range(2):
            c, acc = carry[j]
            z = _dot_nt(qs[j], k)
            sp = jnp.log(1.0 + jnp.exp(-jnp.abs(z)))
            log1m = -(jnp.maximum(z, 0.0) + sp)
            logsig = jnp.minimum(z, 0.0) - sp
            if diag:
                log1m = jnp.where(past, log1m, 0.0)
            hi = log1m.astype(BF16)
            lo = (log1m - hi.astype(F32)).astype(BF16)
            after = c + _dot(hi, after_mat) + _dot(lo, after_mat)
            w = jnp.exp(logsig + after)
            if diag:
                w = jnp.where(past, w, 0.0)
            acc = acc + _dot(w.astype(BF16), v)
            c = c + jnp.sum(log1m, axis=-1, keepdims=True)
            out.append((c, acc))
        return tuple(out)

    def live(carry):
        return (jnp.maximum(jnp.max(carry[0][0]), jnp.max(carry[1][0])) > SB_DONE).astype(jnp.int32)

    def body(state):
        t, _, carry = state
        carry = step(i - 1 - t, carry, False)
        return t + 1, live(carry), carry

    init = tuple((jnp.zeros((tq, 1), F32), jnp.zeros((tq, 128), F32)) for _ in range(2))
    carry = step(i, init, True)
    _, _, carry = lax.while_loop(lambda st: (st[0] < i) & (st[1] > 0), body, (jnp.int32(0), live(carry), carry))
    o_ref[...] = jnp.where(lane < HD, carry[0][1], carry[1][1]).astype(BF16)


def _sb_attention(q, k, v, B, T, tq=256):
    nq = T // tq
    kv = pl.BlockSpec((T, 128), lambda b, p, i: (b, p))
    return pl.pallas_call(
        functools.partial(_sb_kernel, tq=tq),
        grid=(B, 4, nq),
        in_specs=[pl.BlockSpec((tq, 128), lambda b, p, i: (b * nq + i, p)), kv, kv],
        out_specs=pl.BlockSpec((tq, 128), lambda b, p, i: (b * nq + i, p)),
        out_shape=jax.ShapeDtypeStruct((B * T, 512), BF16),
        compiler_params=_cparams(("arbitrary", "arbitrary", "arbitrary")),
        name="sb_attention",
    )(q, k, v)


def _router_weights(w_group, b_group, w_router, b_router):
    pad = 128 - N_EXP - N_GROUPS
    wr = jnp.concatenate([w_router, w_group, jnp.zeros((D, pad), F32)], axis=1)
    br = jnp.concatenate([b_router, b_group, jnp.zeros((pad,), F32)])[None, :]
    return wr, br


def _ffn(h, oa, ob, wo, g, w_group, b_group, w_router, b_router, w_gate, w_up, w_down):
    wr, br = _router_weights(w_group, b_group, w_router, b_router)
    hn, u, ri, rg, cnt = _outproj_router(h, oa, ob, wo.astype(BF16), g[None, :], wr, br)
    d1, d2, P, tile_expert, tile_valid = _dispatch_plan(ri, cnt, h.shape[0])
    xs = _sc_scatter_rows(u, d1, d2, P)
    wgu = jnp.concatenate([w_gate, w_up], axis=-1).astype(BF16)
    ys = _moe_experts(xs, tile_expert, tile_valid, wgu, w_down.astype(BF16))
    return hn, _sc_gather_rows(ys, jnp.concatenate([d1, d2])), rg


def kernel(x, norm_mix, norm_ffn, norm_final, rel_bias_table, w_in_even, g_mla_q, g_mla_kv, w_mla_qb, w_mla_kvb, swa_sinks, w_out_even, w_in_odd, w_out_odd, moe_w_group, moe_b_group, moe_w_router, moe_b_router, moe_w_gate, moe_w_up, moe_w_down):
    B, T, _ = x.shape
    h = x.reshape(B * T, D)
    tb = _bias_table(rel_bias_table)

    win, wqa, wqb, wk, wv = _even_weights(w_in_even[0], w_mla_qb[0], w_mla_kvb[0])
    q, k, v, qs8, ks, vs, vsw = _proj_even(h, norm_mix[0][None, :], win, g_mla_q[0][None, :], g_mla_kv[0][None, :],
                                           wqa, wqb, wk, wv, _rope_tables(T), T)
    oa = _mla_attention(q, k, v, B, T)
    ob = _swa_attention(qs8, ks, vs, vsw, swa_sinks[0], tb, B, T)
    h, y12, rg = _ffn(h, oa, ob, w_out_even[0], norm_ffn[0], moe_w_group[0], moe_b_group[0], moe_w_router[0],
                         moe_b_router[0], moe_w_gate[0], moe_w_up[0], moe_w_down[0])

    h, qc, kc, vc, qd, kd, vd = _combine_proj_odd(h, y12, rg, norm_mix[1][None, :], _odd_weights(w_in_odd[0]),
                                                  _block_indicator(T), T)
    oc = _moba_attention(qc, kc, vc, tb, B, T)
    od = _sb_attention(qd, kd, vd, B, T)
    h, y12, rg = _ffn(h, oc, od, w_out_odd[0], norm_ffn[1], moe_w_group[1], moe_b_group[1], moe_w_router[1],
                         moe_b_router[1], moe_w_gate[1], moe_w_up[1], moe_w_down[1])
    out = _combine_final(h, y12, rg, norm_final[None, :])
    return out.reshape(B, T, D)
```

```python
import functools
import math

import numpy as np
import jax
import jax.numpy as jnp
from jax import lax
from jax.experimental import pallas as pl
from jax.experimental.pallas import tpu as pltpu
from jax.experimental.pallas import tpu_sc as plsc

F32 = jnp.float32
BF16 = jnp.bfloat16
NEG = -1e30
EPS = 1e-6

D = 1024
HD = 64
LANES = 128
MLA_HEADS, MLA_QR, MLA_KVR, MLA_NOPE, MLA_ROPE, MLA_V = 8, 256, 128, 64, 32, 64
ROPE_THETA = 10000.0
SWA_HEADS, SWA_KV, SWA_WIN = 8, 2, 128
MOBA_L, MOBA_TOPK = 256, 3
REL_BUCKETS, REL_MAX = 32, 2048
N_GROUPS, EPG, N_EXP, D_EXP = 4, 8, 32, 256
BIAS_CONST_FROM = 1792
MOBA_TQ = 2 * MOBA_L
BIAS_PAD = MOBA_L
BIAS_ROWS = BIAS_PAD + BIAS_CONST_FROM + MOBA_TQ
SB_DONE = -110.0
TM = 512
TM_E = 256
VMEM_LIMIT = 56 * 1024 * 1024
SC_CORES, SC_SUBCORES = 2, 16
SC_CHUNK = 128
HALF = D // 2
LOG2E = math.log2(math.e)


def _pack_halves(x):
    lo = pltpu.bitcast(x[:, :HALF].astype(BF16).astype(F32), jnp.uint32) >> 16
    hi = pltpu.bitcast(x[:, HALF:].astype(BF16).astype(F32), jnp.uint32) & jnp.uint32(0xFFFF0000)
    return pltpu.bitcast(lo | hi, jnp.int32)


def _unpack_halves(w):
    w = pltpu.bitcast(w, jnp.uint32)
    return pltpu.bitcast(w << 16, F32), pltpu.bitcast(w & jnp.uint32(0xFFFF0000), F32)


def _cparams(sem):
    return pltpu.CompilerParams(dimension_semantics=sem, vmem_limit_bytes=VMEM_LIMIT)


def _dot(a, b):
    return jnp.dot(a, b, preferred_element_type=F32)


def _dot_nt(a, b):
    return lax.dot_general(a, b, (((1,), (1,)), ((), ())), preferred_element_type=F32)


def _rms(x, g):
    return x * lax.rsqrt(jnp.mean(x * x, axis=-1, keepdims=True) + EPS) * g


def _with_ones_rows(vt):
    rows = lax.broadcasted_iota(jnp.int32, vt.shape, 0)
    return jnp.where((rows & HD) != 0, 1.0, vt)


def _flash_update_t(st, vt, m, acc):
    m_new = jnp.maximum(m, jnp.max(st, axis=0, keepdims=True))
    alpha = jnp.exp2(m - m_new)
    pt = jnp.exp2(st - m_new).astype(BF16)
    return m_new, alpha * acc + _dot(vt, pt)


def _flash_update(s, v, m, l, acc):
    m_new = jnp.maximum(m, jnp.max(s, axis=-1, keepdims=True))
    alpha = jnp.exp(m - m_new)
    p = jnp.exp(s - m_new)
    l = alpha * l + jnp.sum(p, axis=-1, keepdims=True)
    acc = alpha * acc + _dot(p.astype(BF16), v)
    return m_new, l, acc


def _t5_bucket_np(dist):
    n = np.maximum(dist, 0)
    max_exact = REL_BUCKETS // 2
    nf = np.maximum(n, 1).astype(np.float32)
    large = max_exact + (np.log(nf / np.float32(max_exact)) / np.float32(math.log(REL_MAX / max_exact))
                         * np.float32(REL_BUCKETS - max_exact)).astype(np.int32)
    large = np.minimum(large, REL_BUCKETS - 1)
    return np.where(n < max_exact, n, large).astype(np.int32)


def _bias_kernel(table_ref, bucket_ref, out_ref, *, scale):
    b = bucket_ref[...]
    for h in range(8):
        acc = jnp.zeros(b.shape, F32)
        for k in range(REL_BUCKETS):
            acc = jnp.where(b == k, table_ref[k, h] * scale, acc)
        out_ref[h] = acc


def _bias_lookup(rel_table, dist, scale, name):
    R, C = dist.shape
    return pl.pallas_call(
        functools.partial(_bias_kernel, scale=scale),
        grid=(C // 256,),
        in_specs=[pl.BlockSpec(memory_space=pltpu.SMEM),
                  pl.BlockSpec((R, 256), lambda m: (0, m))],
        out_specs=pl.BlockSpec((8, R, 256), lambda m: (0, 0, m)),
        out_shape=jax.ShapeDtypeStruct((8, R, C), F32),
        compiler_params=_cparams(("arbitrary",)),
        name=name,
    )(rel_table, jnp.asarray(_t5_bucket_np(dist)))


def _bias_tables(rel_table):
    swa = _bias_lookup(rel_table, np.arange(256)[:, None] - np.arange(256)[None, :], 1.0, "bias_swa")
    moba = _bias_lookup(rel_table, np.arange(BIAS_ROWS)[None, :] - BIAS_PAD - np.arange(MOBA_L)[:, None], LOG2E,
                        "bias_moba")
    return swa, moba


def _proj_even_kernel(h_ref, g_ref, win_ref, gq_ref, gkv_ref, wqa_ref, wqb_ref, wk_ref, wv_ref,
                      cq_ref, sq_ref, ck_ref, sk_ref,
                      q_ref, k_ref, v_ref, qs_ref, ks_ref, vs_ref, vsw_ref):
    u = _rms(h_ref[...], g_ref[...]).astype(BF16)
    nq = _rms(_dot(u, win_ref[:, 0:256]), gq_ref[...]).astype(BF16)
    qa = _dot(nq, wqa_ref[...])
    qb = _dot(nq, wqb_ref[...])
    cq, sq = cq_ref[...], sq_ref[...]
    for hh in range(MLA_HEADS):
        sl = slice(128 * hh, 128 * (hh + 1))
        q_ref[:, sl] = (qa[:, sl] * cq + qb[:, sl] * sq).astype(BF16)
    nkv = _rms(_dot(u, win_ref[:, 256:384]), gkv_ref[...]).astype(BF16)
    kn = _dot(nkv, wk_ref[...])
    v_ref[...] = _with_ones_rows(_dot_nt(wv_ref[...], nkv)).astype(BF16)
    kr = _dot(u, win_ref[:, 1792:1920]) * ck_ref[...] + _dot(u, win_ref[:, 1920:2048]) * sk_ref[...]
    for hh in range(MLA_HEADS):
        sl = slice(128 * hh, 128 * (hh + 1))
        k_ref[:, sl] = (kn[:, sl] + kr).astype(BF16)
    qs_ref[...] = _dot(u, win_ref[:, 384:1408]).astype(BF16)
    ks_ref[...] = _dot(u, win_ref[:, 1408:1536]).astype(BF16)
    vs_ref[...] = _dot(u, win_ref[:, 1536:1664]).astype(BF16)
    vsw_ref[...] = _dot(u, win_ref[:, 1664:1792]).astype(BF16)


def _cols(w, idx, scale=None):
    idx = np.asarray(idx)
    out = jnp.take(w, jnp.asarray(np.maximum(idx, 0)), axis=1) * jnp.asarray((idx >= 0).astype(np.float32))
    if scale is not None:
        out = out * scale
    return out


def _even_weights(w_in, w_qb, w_kvb):
    o_cq, o_ckv, o_kr = 0, MLA_QR, MLA_QR + MLA_KVR
    o_qs = o_kr + MLA_ROPE
    o_ks = o_qs + SWA_HEADS * HD
    o_vs = o_ks + SWA_KV * HD
    idx = list(range(o_cq, o_cq + MLA_QR)) + list(range(o_ckv, o_ckv + MLA_KVR))
    scale = [1.0] * len(idx)
    for hq in range(SWA_HEADS):
        c = hq // (SWA_HEADS // SWA_KV)
        blk = [-1] * 128
        blk[64 * c:64 * c + 64] = range(o_qs + 64 * hq, o_qs + 64 * hq + 64)
        idx += blk
        scale += [HD ** -0.5] * 128
    idx += list(range(o_ks, o_ks + 128))
    idx += list(range(o_vs, o_vs + 128))
    idx += list(range(o_vs + 64, o_vs + 128)) + list(range(o_vs, o_vs + 64))
    scale += [1.0] * 384
    half = MLA_ROPE // 2
    idx += [-1] * 64 + list(range(o_kr, o_kr + MLA_ROPE)) + [-1] * 32
    idx += [-1] * 64 + list(range(o_kr + half, o_kr + MLA_ROPE)) + list(range(o_kr, o_kr + half)) + [-1] * 32
    scale += [1.0] * 256
    win = _cols(w_in, idx, jnp.asarray(np.asarray(scale, np.float32))).astype(BF16)
    qw = MLA_NOPE + MLA_ROPE
    ia, ib, ik, iv = [], [], [], []
    for h in range(MLA_HEADS):
        ia += list(range(qw * h, qw * h + qw)) + [-1] * 32
        ib += [-1] * 64 + list(range(qw * h + MLA_NOPE + half, qw * h + qw)) \
            + list(range(qw * h + MLA_NOPE, qw * h + MLA_NOPE + half)) + [-1] * 32
        ik += list(range(128 * h, 128 * h + MLA_NOPE)) + [-1] * 64
        iv += list(range(128 * h + MLA_NOPE, 128 * h + 128)) + [-1] * 64
    return (win, _cols(w_qb, ia).astype(BF16), _cols(w_qb, ib).astype(BF16),
            _cols(w_kvb, ik).astype(BF16), _cols(w_kvb, iv).T.astype(BF16))


def _rope_tables(T):
    half = MLA_ROPE // 2
    freqs = ROPE_THETA ** (-jnp.arange(0, MLA_ROPE, 2, dtype=F32) / MLA_ROPE)
    ang = jnp.arange(T, dtype=F32)[:, None] * freqs[None, :]
    cos, sin = jnp.cos(ang), jnp.sin(ang)
    z = lambda n: jnp.zeros((T, n), F32)
    scale = (MLA_NOPE + MLA_ROPE) ** -0.5 * LOG2E
    ck = jnp.concatenate([z(64), cos, cos, z(32)], axis=1)
    sk = jnp.concatenate([z(64), -sin, sin, z(32)], axis=1)
    cq = jnp.concatenate([jnp.ones((T, 64), F32), cos, cos, z(32)], axis=1) * scale
    sq = sk * scale
    return cq, sq, ck, sk


def _proj_even(h, g, win, gq, gkv, wqa, wqb, wk, wv, tables, T):
    N = h.shape[0]
    nt = T // TM
    full = lambda a: pl.BlockSpec(a.shape, lambda i: (0,) * a.ndim)
    tab = pl.BlockSpec((TM, 128), lambda i: (i % nt, 0))
    row = lambda w: pl.BlockSpec((TM, w), lambda i: (i, 0))
    widths = (1024, 1024, None, 1024, 128, 128, 128)
    vt_spec = pl.BlockSpec((None, 1024, TM), lambda i: (i // nt, 0, i % nt))
    vt_shape = jax.ShapeDtypeStruct((N // T, 1024, T), BF16)
    return pl.pallas_call(
        _proj_even_kernel,
        grid=(N // TM,),
        in_specs=[row(D), full(g), full(win), full(gq), full(gkv), full(wqa), full(wqb), full(wk), full(wv),
                  tab, tab, tab, tab],
        out_specs=[vt_spec if w is None else row(w) for w in widths],
        out_shape=[vt_shape if w is None else jax.ShapeDtypeStruct((N, w), BF16) for w in widths],
        compiler_params=_cparams(("arbitrary",)),
        name="proj_even",
    )(h, g, win, gq, gkv, wqa, wqb, wk, wv, *tables)


def _finish_t(accs):
    ot = jnp.concatenate([acc[:HD] / acc[HD:HD + 1] for acc in accs], axis=0)
    return ot.T.astype(BF16)


def _causal_attention_t(i, q_of, k_ref, vt_ref, o_ref, s_ref, m_ref, acc_ref, *, tq, tk, bias_of=None):
    n_full = i * (tq // tk)
    key = lax.broadcasted_iota(jnp.int32, (tk, tq), 0)
    qry = i * tq + lax.broadcasted_iota(jnp.int32, (tk, tq), 1)

    def scores_to(slot, start, boundary):
        start = pl.multiple_of(start, tk)
        for j in range(2):
            st = _dot_nt(k_ref[pl.ds(start, tk), 128 * j:128 * (j + 1)], q_of(j))
            if bias_of is not None:
                st = st + bias_of(start, j)
            if boundary:
                st = jnp.where(start + key <= qry, st, NEG)
            s_ref[slot, j] = st

    def update_from(slot, start):
        start = pl.multiple_of(start, tk)
        for j in range(2):
            m_new, acc = _flash_update_t(s_ref[slot, j], vt_ref[128 * j:128 * (j + 1), pl.ds(start, tk)],
                                         m_ref[j], acc_ref[j])
            m_ref[j] = m_new
            acc_ref[j] = acc

    m_ref[...] = jnp.full(m_ref.shape, NEG, F32)
    acc_ref[...] = jnp.zeros(acc_ref.shape, F32)
    for d in range(tq // tk - 1):
        scores_to(0, (n_full + d) * tk, True)
        update_from(0, (n_full + d) * tk)
    first = (n_full + tq // tk - 1) * tk
    scores_to(0, first, True)

    def block_start(k):
        return jnp.where(k == 0, first, (k - 1) * tk)

    def pair(p, _):
        scores_to(1, block_start(2 * p + 1), False)
        update_from(0, block_start(2 * p))
        scores_to(0, block_start(2 * p + 2), False)
        update_from(1, block_start(2 * p + 1))
        return 0

    lax.fori_loop(0, n_full // 2, pair, 0)

    @pl.when(n_full % 2 == 1)
    def _():
        scores_to(1, block_start(n_full), False)
        update_from(0, block_start(n_full - 1))
        update_from(1, block_start(n_full))

    @pl.when(n_full % 2 == 0)
    def _():
        update_from(0, block_start(n_full))

    o_ref[...] = _finish_t([acc_ref[j] for j in range(2)])


def _attention_t_scratch(tq, tk):
    return [pltpu.VMEM((2, 2, tk, tq), F32), pltpu.VMEM((2, 1, tq), F32), pltpu.VMEM((2, 128, tq), F32)]


def _mla_kernel(q_ref, k_ref, vt_ref, o_ref, s_ref, m_ref, acc_ref, *, tq, tk):
    _causal_attention_t(pl.program_id(2), lambda j: q_ref[:, 128 * j:128 * (j + 1)], k_ref, vt_ref, o_ref,
                        s_ref, m_ref, acc_ref, tq=tq, tk=tk)


def _mla_attention(q, k, vt, B, T, tq=512, tk=512):
    nq = T // tq
    return pl.pallas_call(
        functools.partial(_mla_kernel, tq=tq, tk=tk),
        grid=(B, MLA_HEADS // 2, nq),
        in_specs=[pl.BlockSpec((tq, 256), lambda b, p, i: (b * nq + i, p)),
                  pl.BlockSpec((T, 256), lambda b, p, i: (b, p)),
                  pl.BlockSpec((None, 256, T), lambda b, p, i: (b, p, 0))],
        out_specs=pl.BlockSpec((tq, 128), lambda b, p, i: (b * nq + i, p)),
        out_shape=jax.ShapeDtypeStruct((B * T, MLA_HEADS * MLA_V), BF16),
        scratch_shapes=_attention_t_scratch(tq, tk),
        compiler_params=_cparams(("arbitrary", "arbitrary", "arbitrary")),
        name="mla_attention",
    )(q, k, vt)


def _swa_kernel(sink_ref, q_ref, k_ref, v_ref, vsw_ref, tb_ref, o_ref, *, tq):
    i = pl.program_id(1)
    G = SWA_HEADS // SWA_KV
    qi = lax.broadcasted_iota(jnp.int32, (128, 256), 0)
    kj = lax.broadcasted_iota(jnp.int32, (128, 256), 1)
    lane = lax.broadcasted_iota(jnp.int32, (128, 128), 1)
    for r in range(tq // 128):
        qstart = i * tq + 128 * r
        kstart = pl.multiple_of(jnp.maximum(qstart - 128, 0), 128)
        off = pl.multiple_of(qstart - kstart, 128)
        dist = off + qi - kj
        valid = (dist >= 0) & (dist < SWA_WIN)
        kw = k_ref[pl.ds(kstart, 256), :]
        vw = v_ref[pl.ds(kstart, 256), :]
        vsw = vsw_ref[pl.ds(kstart, 256), :]
        halves = [None] * SWA_HEADS
        for c in range(SWA_KV):
            qg = jnp.concatenate([q_ref[128 * r:128 * (r + 1), 128 * (G * c + g):128 * (G * c + g + 1)]
                                  for g in range(G)], axis=0)
            s_all = _dot_nt(qg, kw)
            for g in range(G):
                hq = G * c + g
                s = s_all[128 * g:128 * (g + 1)] + tb_ref[hq, pl.ds(off, 128), :]
                s = jnp.where(valid, s, NEG)
                sink = sink_ref[hq]
                m = jnp.maximum(jnp.max(s, axis=-1, keepdims=True), sink)
                p = jnp.exp(s - m)
                denom = jnp.sum(p, axis=-1, keepdims=True) + jnp.exp(sink - m)
                p = (p / denom).astype(BF16)
                halves[hq] = _dot(p, vw if c == hq % 2 else vsw)
        for mp in range(SWA_HEADS // 2):
            o_ref[128 * r:128 * (r + 1), 128 * mp:128 * (mp + 1)] = jnp.where(
                lane < HD, halves[2 * mp], halves[2 * mp + 1]).astype(BF16)


def _swa_attention(q8, k, v, vsw, sinks, tb, B, T, tq=512):
    nq = T // tq
    kv = pl.BlockSpec((T, 128), lambda b, i: (b, 0))
    return pl.pallas_call(
        functools.partial(_swa_kernel, tq=tq),
        grid=(B, nq),
        in_specs=[pl.BlockSpec(memory_space=pltpu.SMEM),
                  pl.BlockSpec((tq, 1024), lambda b, i: (b * nq + i, 0)),
                  kv, kv, kv,
                  pl.BlockSpec((8, 256, 256), lambda b, i: (0, 0, 0))],
        out_specs=pl.BlockSpec((tq, 512), lambda b, i: (b * nq + i, 0)),
        out_shape=jax.ShapeDtypeStruct((B * T, SWA_HEADS * HD), BF16),
        compiler_params=_cparams(("arbitrary", "arbitrary")),
        name="swa_attention",
    )(sinks, q8, k, v, vsw, tb)


def _outproj_router_kernel(h_ref, oa_ref, ob_ref, wo_ref, g_ref, wr_ref, br_ref,
                           hn_ref, u_ref, ri_ref, rg_ref, cnt_ref, base_ref):
    i = pl.program_id(0)

    @pl.when(i == 0)
    def _():
        base_ref[...] = jnp.zeros_like(base_ref)

    hn = h_ref[...] + _dot(oa_ref[...], wo_ref[0:512, :]) + _dot(ob_ref[...], wo_ref[512:1024, :])
    hn_ref[...] = hn
    u = _rms(hn, g_ref[...])
    u_ref[...] = _pack_halves(u)
    logits = jnp.dot(u, wr_ref[...], preferred_element_type=F32, precision=lax.Precision.HIGHEST) + br_ref[...]
    tm = logits.shape[0]
    lane = lax.broadcasted_iota(jnp.int32, (tm, 128), 1).astype(F32)
    big = 1e6
    isg = (lane >= N_EXP) & (lane < N_EXP + N_GROUPS)
    gl = jnp.where(isg, logits, NEG)
    gmax = jnp.max(gl, axis=-1, keepdims=True)
    gsel = jnp.min(jnp.where(gl == gmax, lane, big), axis=-1, keepdims=True) - N_EXP
    g_gate = 1.0 / jnp.sum(jnp.where(isg, jnp.exp(gl - gmax), 0.0), axis=-1, keepdims=True)
    ine = (lane >= gsel * EPG) & (lane < gsel * EPG + EPG)
    el = jnp.where(ine, logits, NEG)
    m1 = jnp.max(el, axis=-1, keepdims=True)
    e1 = jnp.min(jnp.where(el == m1, lane, big), axis=-1, keepdims=True)
    el2 = jnp.where(lane == e1, NEG, el)
    m2 = jnp.max(el2, axis=-1, keepdims=True)
    e2 = jnp.min(jnp.where(el2 == m2, lane, big), axis=-1, keepdims=True)
    r = jnp.exp(m2 - m1)
    gate1 = g_gate / (1.0 + r)
    gate2 = g_gate * r / (1.0 + r)
    oh1 = (lane == e1).astype(F32)
    oh2 = (lane == e2).astype(F32)
    oh = oh1 + oh2
    rt = lax.broadcasted_iota(jnp.int32, (tm, tm), 0)
    ct = lax.broadcasted_iota(jnp.int32, (tm, tm), 1)
    lower = jnp.where(ct < rt, 1.0, 0.0).astype(BF16)
    prefix = _dot(lower, oh.astype(BF16)) + base_ref[...]
    rank1 = jnp.sum(oh1 * prefix, axis=-1, keepdims=True)
    rank2 = jnp.sum(oh2 * prefix, axis=-1, keepdims=True)
    base_ref[...] = base_ref[...] + jnp.sum(oh, axis=0, keepdims=True)
    cnt_ref[...] = base_ref[...]
    ri = jnp.where(lane == 0, e1, jnp.where(lane == 1, e2, 0.0))
    ri = jnp.where(lane == 2, rank1, jnp.where(lane == 3, rank2, ri))
    ri_ref[...] = ri.astype(jnp.int32)
    rg_ref[...] = jnp.where(lane == 0, gate1, jnp.where(lane == 1, gate2, 0.0))


def _outproj_router(h, oa, ob, wo, g, wr, br):
    N = h.shape[0]
    full = lambda a: pl.BlockSpec(a.shape, lambda i: (0,) * a.ndim)
    row = lambda w: pl.BlockSpec((TM, w), lambda i: (i, 0))
    return pl.pallas_call(
        _outproj_router_kernel,
        grid=(N // TM,),
        in_specs=[row(D), row(512), row(512), full(wo), full(g), full(wr), full(br)],
        out_specs=[row(D), row(HALF), row(128), row(128), pl.BlockSpec((1, 128), lambda i: (0, 0))],
        out_shape=[jax.ShapeDtypeStruct((N, D), F32), jax.ShapeDtypeStruct((N, HALF), jnp.int32),
                   jax.ShapeDtypeStruct((N, 128), jnp.int32), jax.ShapeDtypeStruct((N, 128), F32),
                   jax.ShapeDtypeStruct((1, 128), F32)],
        scratch_shapes=[pltpu.VMEM((1, 128), F32)],
        compiler_params=_cparams(("arbitrary",)),
        name="outproj_router",
    )(h, oa, ob, wo, g, wr, br)


def _moe_kernel(te_ref, tv_ref, x_ref, wgu_ref, wd_ref, o_ref):
    i = pl.program_id(0)
    valid = tv_ref[i]

    @pl.when(valid > 0)
    def _():
        rows = lax.broadcasted_iota(jnp.int32, x_ref.shape, 0)
        lo, hi = _unpack_halves(jnp.where(rows < valid, x_ref[...], 0))
        hgu = _dot(lo.astype(BF16), wgu_ref[:HALF, :]) + _dot(hi.astype(BF16), wgu_ref[HALF:, :])
        hg, hu = hgu[:, :D_EXP], hgu[:, D_EXP:]
        a = hg * (1.0 / (1.0 + jnp.exp(-hg))) * hu
        o_ref[...] = _pack_halves(_dot(a.astype(BF16), wd_ref[...]))

    @pl.when(valid <= 0)
    def _():
        o_ref[...] = jnp.zeros_like(o_ref)


def _moe_experts(xs, tile_expert, tile_valid, wgu, wd):
    P = xs.shape[0]
    grid_spec = pltpu.PrefetchScalarGridSpec(
        num_scalar_prefetch=2,
        grid=(P // TM_E,),
        in_specs=[pl.BlockSpec((TM_E, HALF), lambda i, te, tv: (i, 0)),
                  pl.BlockSpec((None, D, 2 * D_EXP), lambda i, te, tv: (te[i], 0, 0)),
                  pl.BlockSpec((None, D_EXP, D), lambda i, te, tv: (te[i], 0, 0))],
        out_specs=pl.BlockSpec((TM_E, HALF), lambda i, te, tv: (i, 0)),
    )
    return pl.pallas_call(
        _moe_kernel,
        grid_spec=grid_spec,
        out_shape=jax.ShapeDtypeStruct((P, HALF), jnp.int32),
        compiler_params=_cparams(("arbitrary",)),
        name="moe_experts",
    )(tile_expert, tile_valid, xs, wgu, wd)


def _sc_mesh():
    return plsc.VectorSubcoreMesh(core_axis_name="c", subcore_axis_name="s",
                                  num_cores=SC_CORES, num_subcores=SC_SUBCORES)


def _sc_scatter_rows(src, d1, d2, P):
    N, W = src.shape
    per_w = N // (SC_CORES * SC_SUBCORES)

    @functools.partial(pl.kernel, mesh=_sc_mesh(), out_type=jax.ShapeDtypeStruct((P, W), src.dtype),
                       scratch_types=[pltpu.VMEM((SC_CHUNK,), jnp.int32), pltpu.VMEM((SC_CHUNK, W), src.dtype)],
                       name="sc_scatter_rows")
    def k(src_hbm, d1_hbm, d2_hbm, out_hbm, idx_v, rows_v):
        wid = lax.axis_index("s") * SC_CORES + lax.axis_index("c")

        @pl.loop(0, per_w // SC_CHUNK)
        def _(c):
            off = wid * per_w + c * SC_CHUNK
            pltpu.sync_copy(src_hbm.at[pl.ds(off, SC_CHUNK)], rows_v)
            pltpu.sync_copy(d1_hbm.at[pl.ds(off, SC_CHUNK)], idx_v)
            pltpu.sync_copy(rows_v, out_hbm.at[idx_v])
            pltpu.sync_copy(d2_hbm.at[pl.ds(off, SC_CHUNK)], idx_v)
            pltpu.sync_copy(rows_v, out_hbm.at[idx_v])

    return k(src, d1, d2)


def _sc_gather_rows(table, idx):
    B, W = idx.shape[0], table.shape[1]
    per_w = B // (SC_CORES * SC_SUBCORES)

    @functools.partial(pl.kernel, mesh=_sc_mesh(), out_type=jax.ShapeDtypeStruct((B, W), table.dtype),
                       scratch_types=[pltpu.VMEM((SC_CHUNK,), jnp.int32), pltpu.VMEM((SC_CHUNK, W), table.dtype)],
                       name="sc_gather_rows")
    def k(table_hbm, idx_hbm, out_hbm, idx_v, rows_v):
        wid = lax.axis_index("s") * SC_CORES + lax.axis_index("c")

        @pl.loop(0, per_w // SC_CHUNK)
        def _(c):
            off = wid * per_w + c * SC_CHUNK
            pltpu.sync_copy(idx_hbm.at[pl.ds(off, SC_CHUNK)], idx_v)
            pltpu.sync_copy(table_hbm.at[idx_v], rows_v)
            pltpu.sync_copy(rows_v, out_hbm.at[pl.ds(off, SC_CHUNK)])

    return k(table, idx)


def _dispatch_plan(ri, cnt, N):
    counts = cnt[0, :N_EXP].astype(jnp.int32)
    padded = ((counts + TM_E - 1) // TM_E) * TM_E
    ends = jnp.cumsum(padded)
    offs = ends - padded
    take = lambda t, i: t.at[i].get(mode="promise_in_bounds")
    d1 = take(offs, ri[:, 0]) + ri[:, 2]
    d2 = take(offs, ri[:, 1]) + ri[:, 3]
    P = 2 * N + N_EXP * TM_E
    tile_start = jnp.arange(P // TM_E, dtype=jnp.int32) * TM_E
    tile_expert = jnp.minimum(jnp.sum((ends[None, :] <= tile_start[:, None]).astype(jnp.int32), axis=1), N_EXP - 1)
    tile_valid = jnp.clip(take(offs + counts, tile_expert) - tile_start, 0, TM_E)
    return d1, d2, P, tile_expert, tile_valid


def _combine(h_ref, y1_ref, y2_ref, rg_ref):
    rg = rg_ref[...]
    lo1, hi1 = _unpack_halves(y1_ref[...])
    lo2, hi2 = _unpack_halves(y2_ref[...])
    g1, g2 = rg[:, 0:1], rg[:, 1:2]
    return h_ref[...] + jnp.concatenate([g1 * lo1 + g2 * lo2, g1 * hi1 + g2 * hi2], axis=1)


ODD_WIDTHS = (1024, 1024, 512, 512, 512)


def _combine_proj_odd_kernel(h_ref, y1_ref, y2_ref, rg_ref, g_ref, win_ref, wvt_ref, ind_ref, hn_ref, vt_ref,
                             *out_refs):
    hn = _combine(h_ref, y1_ref, y2_ref, rg_ref)
    hn_ref[...] = hn
    u = _rms(hn, g_ref[...]).astype(BF16)
    vt_ref[...] = _with_ones_rows(_dot_nt(wvt_ref[...], u)).astype(BF16)
    c0 = 0
    for n, o_ref in enumerate(out_refs):
        w = ODD_WIDTHS[n]
        if n == 1:
            for hh in range(8):
                o_ref[:, 128 * hh:128 * (hh + 1)] = (
                    _dot(u, win_ref[:, c0 + 128 * hh:c0 + 128 * (hh + 1)]) + ind_ref[...]).astype(BF16)
        else:
            for c in range(0, w, 512):
                o_ref[:, c:c + 512] = _dot(u, win_ref[:, c0 + c:c0 + c + 512]).astype(BF16)
        c0 += w


def _odd_weights(w_in):
    idx, scale = [], []
    for base, s in ((0, HD ** -0.5 * LOG2E), (512, 1.0)):
        for hh in range(8):
            idx += list(range(base + 64 * hh, base + 64 * hh + 64)) + [-1] * 64
            scale += [s] * 128
    idx += list(range(1536, 3072))
    scale += [HD ** -0.5] * 512 + [1.0] * 1024
    win = _cols(w_in, idx, jnp.asarray(np.asarray(scale, np.float32))).astype(BF16)
    iv = []
    for hh in range(8):
        iv += list(range(1024 + 64 * hh, 1024 + 64 * hh + 64)) + [-1] * 64
    return win, _cols(w_in, iv).T.astype(BF16)


def _block_indicator(T):
    assert T // MOBA_L <= 64
    ind = np.zeros((T, 128), np.float32)
    ind[np.arange(T), 64 + np.arange(T) // MOBA_L] = 1.0
    return jnp.asarray(ind)


def _y_specs(N):
    return [pl.BlockSpec((TM, HALF), lambda i: (i, 0)), pl.BlockSpec((TM, HALF), lambda i: (i + N // TM, 0))]


def _combine_proj_odd(h, y12, rg, g, win, wvt, ind, T):
    N = h.shape[0]
    nt = T // TM
    full = lambda a: pl.BlockSpec(a.shape, lambda i: (0,) * a.ndim)
    row = lambda w: pl.BlockSpec((TM, w), lambda i: (i, 0))
    return pl.pallas_call(
        _combine_proj_odd_kernel,
        grid=(N // TM,),
        in_specs=[row(D)] + _y_specs(N) + [row(128), full(g), full(win), full(wvt),
                  pl.BlockSpec((TM, 128), lambda i: (i % nt, 0))],
        out_specs=[row(D), pl.BlockSpec((None, 1024, TM), lambda i: (i // nt, 0, i % nt))]
        + [row(w) for w in ODD_WIDTHS],
        out_shape=[jax.ShapeDtypeStruct((N, D), F32), jax.ShapeDtypeStruct((N // T, 1024, T), BF16)]
        + [jax.ShapeDtypeStruct((N, w), BF16) for w in ODD_WIDTHS],
        compiler_params=_cparams(("arbitrary",)),
        name="combine_proj_odd",
    )(h, y12, y12, rg, g, win, wvt, ind)


def _combine_final_kernel(h_ref, y1_ref, y2_ref, rg_ref, g_ref, o_ref):
    o_ref[...] = _rms(_combine(h_ref, y1_ref, y2_ref, rg_ref), g_ref[...])


def _combine_final(h, y12, rg, g):
    N = h.shape[0]
    row = lambda w: pl.BlockSpec((TM, w), lambda i: (i, 0))
    return pl.pallas_call(
        _combine_final_kernel,
        grid=(N // TM,),
        in_specs=[row(D)] + _y_specs(N) + [row(128), pl.BlockSpec((1, D), lambda i: (0, 0))],
        out_specs=row(D),
        out_shape=jax.ShapeDtypeStruct((N, D), F32),
        compiler_params=_cparams(("arbitrary",)),
        name="combine_final",
    )(h, y12, y12, rg, g)


def _moba_kernel(q_ref, k_ref, vt_ref, tb_ref, o_ref, kmean_ref, qa_ref, s_ref, m_ref, acc_ref, *, nkb):
    i = pl.program_id(2)
    L, tq = MOBA_L, MOBA_TQ

    @pl.when(i == 0)
    def _():
        kmean_ref[...] = jnp.zeros_like(kmean_ref)
        for n in range(nkb):
            kmean_ref[64 + n:65 + n, :] = jnp.sum(k_ref[n * L:(n + 1) * L, :].astype(F32), axis=0,
                                                  keepdims=True) * (1.0 / L)

    lane = lax.broadcasted_iota(jnp.int32, (tq, 128), 1)
    rowi = lax.broadcasted_iota(jnp.int32, (tq, 128), 0)
    own = 2 * i + jnp.where(rowi >= L, 1, 0)
    blk = lane - 64
    lane_f = lane.astype(F32)
    for j in range(2):
        qj = q_ref[:, 128 * j:128 * (j + 1)]
        gsc = lax.dot_general(qj.astype(F32), kmean_ref[:, 128 * j:128 * (j + 1)], (((1,), (1,)), ((), ())),
                              preferred_element_type=F32, precision=lax.Precision.HIGHEST)
        g = jnp.where((blk >= 0) & (blk < own), gsc, NEG)
        allowed = blk == own
        for _ in range(MOBA_TOPK):
            mx = jnp.max(g, axis=-1, keepdims=True)
            idx = jnp.min(jnp.where(g == mx, lane_f, 1e6), axis=-1, keepdims=True)
            pick = lane_f == idx
            allowed = allowed | (pick & (mx > 0.5 * NEG))
            g = jnp.where(pick, 2.0 * NEG, g)
        qa_ref[j] = jnp.where(lane < HD, qj, jnp.where(allowed, 0.0, NEG).astype(BF16))

    def bias_of(start, j):
        parts = []
        for c in range(tq // L):
            off = jnp.minimum(i * tq - start - c * L, BIAS_CONST_FROM) + BIAS_PAD
            parts.append(tb_ref[j, :, pl.ds(pl.multiple_of(off, L), tq)])
        return jnp.concatenate(parts, axis=0)

    _causal_attention_t(i, lambda j: qa_ref[j], k_ref, vt_ref, o_ref, s_ref, m_ref, acc_ref, tq=tq, tk=tq,
                        bias_of=bias_of)


def _moba_attention(q, k, vt, tb, B, T):
    tq = MOBA_TQ
    nq = T // tq
    return pl.pallas_call(
        functools.partial(_moba_kernel, nkb=T // MOBA_L),
        grid=(B, 4, nq),
        in_specs=[pl.BlockSpec((tq, 256), lambda b, p, i: (b * nq + i, p)),
                  pl.BlockSpec((T, 256), lambda b, p, i: (b, p)),
                  pl.BlockSpec((None, 256, T), lambda b, p, i: (b, p, 0)),
                  pl.BlockSpec((2, MOBA_L, BIAS_ROWS), lambda b, p, i: (p, 0, 0))],
        out_specs=pl.BlockSpec((tq, 128), lambda b, p, i: (b * nq + i, p)),
        out_shape=jax.ShapeDtypeStruct((B * T, 512), BF16),
        scratch_shapes=[pltpu.VMEM((128, 256), F32), pltpu.VMEM((2, tq, 128), BF16)]
        + _attention_t_scratch(tq, tq),
        compiler_params=_cparams(("arbitrary", "arbitrary", "arbitrary")),
        name="moba_attention",
    )(q, k, vt, tb)


def _sb_kernel(q_ref, k_ref, v_ref, o_ref, *, tq):
    i = pl.program_id(2)
    lane = lax.broadcasted_iota(jnp.int32, (tq, 128), 1)
    q_all = q_ref[...]
    qs = [jnp.where(lane < HD, q_all, 0).astype(BF16), jnp.where(lane >= HD, q_all, 0).astype(BF16)]
    row = lax.broadcasted_iota(jnp.int32, (tq, tq), 0)
    col = lax.broadcasted_iota(jnp.int32, (tq, tq), 1)
    past = col < row
    after_mat = jnp.where(row > col, 1.0, 0.0).astype(BF16)

    def step(kb, carry, diag):
        start = pl.multiple_of(kb * tq, tq)
        k = k_ref[pl.ds(start, tq), :]
        v = v_ref[pl.ds(start, tq), :]
        out = []
        for j in range(2):
            c, acc = carry[j]
            z = _dot_nt(qs[j], k)
            sp = jnp.log(1.0 + jnp.exp(-jnp.abs(z)))
            log1m = -(jnp.maximum(z, 0.0) + sp)
            logsig = jnp.minimum(z, 0.0) - sp
            if diag:
                log1m = jnp.where(past, log1m, 0.0)
            hi = log1m.astype(BF16)
            lo = (log1m - hi.astype(F32)).astype(BF16)
            after = c + _dot(hi, after_mat) + _dot(lo, after_mat)
            w = jnp.exp(logsig + after)
            if diag:
                w = jnp.where(past, w, 0.0)
            acc = acc + _dot(w.astype(BF16), v)
            c = c + jnp.sum(log1m, axis=-1, keepdims=True)
            out.append((c, acc))
        return tuple(out)

    def live(carry):
        return (jnp.maximum(jnp.max(carry[0][0]), jnp.max(carry[1][0])) > SB_DONE).astype(jnp.int32)

    def body(state):
        t, _, carry = state
        carry = step(i - 1 - t, carry, False)
        return t + 1, live(carry), carry

    init = tuple((jnp.zeros((tq, 1), F32), jnp.zeros((tq, 128), F32)) for _ in range(2))
    carry = step(i, init, True)
    _, _, carry = lax.while_loop(lambda st: (st[0] < i) & (st[1] > 0), body, (jnp.int32(0), live(carry), carry))
    o_ref[...] = jnp.where(lane < HD, carry[0][1], carry[1][1]).astype(BF16)


def _sb_attention(q, k, v, B, T, tq=256):
    nq = T // tq
    kv = pl.BlockSpec((T, 128), lambda b, p, i: (b, p))
    return pl.pallas_call(
        functools.partial(_sb_kernel, tq=tq),
        grid=(B, 4, nq),
        in_specs=[pl.BlockSpec((tq, 128), lambda b, p, i: (b * nq + i, p)), kv, kv],
        out_specs=pl.BlockSpec((tq, 128), lambda b, p, i: (b * nq + i, p)),
        out_shape=jax.ShapeDtypeStruct((B * T, 512), BF16),
        compiler_params=_cparams(("arbitrary", "arbitrary", "arbitrary")),
        name="sb_attention",
    )(q, k, v)


def _router_weights(w_group, b_group, w_router, b_router):
    pad = 128 - N_EXP - N_GROUPS
    wr = jnp.concatenate([w_router, w_group, jnp.zeros((D, pad), F32)], axis=1)
    br = jnp.concatenate([b_router, b_group, jnp.zeros((pad,), F32)])[None, :]
    return wr, br


def _ffn(h, oa, ob, wo, g, w_group, b_group, w_router, b_router, w_gate, w_up, w_down):
    wr, br = _router_weights(w_group, b_group, w_router, b_router)
    hn, u, ri, rg, cnt = _outproj_router(h, oa, ob, wo.astype(BF16), g[None, :], wr, br)
    d1, d2, P, tile_expert, tile_valid = _dispatch_plan(ri, cnt, h.shape[0])
    xs = _sc_scatter_rows(u, d1, d2, P)
    wgu = jnp.concatenate([w_gate, w_up], axis=-1).astype(BF16)
    ys = _moe_experts(xs, tile_expert, tile_valid, wgu, w_down.astype(BF16))
    return hn, _sc_gather_rows(ys, jnp.concatenate([d1, d2])), rg


def kernel(x, norm_mix, norm_ffn, norm_final, rel_bias_table, w_in_even, g_mla_q, g_mla_kv, w_mla_qb, w_mla_kvb, swa_sinks, w_out_even, w_in_odd, w_out_odd, moe_w_group, moe_b_group, moe_w_router, moe_b_router, moe_w_gate, moe_w_up, moe_w_down):
    B, T, _ = x.shape
    h = x.reshape(B * T, D)
    tb_swa, tb_moba = _bias_tables(rel_bias_table)

    win, wqa, wqb, wk, wv = _even_weights(w_in_even[0], w_mla_qb[0], w_mla_kvb[0])
    q, k, v, qs8, ks, vs, vsw = _proj_even(h, norm_mix[0][None, :], win, g_mla_q[0][None, :], g_mla_kv[0][None, :],
                                           wqa, wqb, wk, wv, _rope_tables(T), T)
    oa = _mla_attention(q, k, v, B, T)
    ob = _swa_attention(qs8, ks, vs, vsw, swa_sinks[0], tb_swa, B, T)
    h, y12, rg = _ffn(h, oa, ob, w_out_even[0], norm_ffn[0], moe_w_group[0], moe_b_group[0], moe_w_router[0],
                         moe_b_router[0], moe_w_gate[0], moe_w_up[0], moe_w_down[0])

    h, vct, qc, kc, qd, kd, vd = _combine_proj_odd(h, y12, rg, norm_mix[1][None, :], *_odd_weights(w_in_odd[0]),
                                                   _block_indicator(T), T)
    oc = _moba_attention(qc, kc, vct, tb_moba, B, T)
    od = _sb_attention(qd, kd, vd, B, T)
    h, y12, rg = _ffn(h, oc, od, w_out_odd[0], norm_ffn[1], moe_w_group[1], moe_b_group[1], moe_w_router[1],
                         moe_b_router[1], moe_w_gate[1], moe_w_up[1], moe_w_down[1])
    out = _combine_final(h, y12, rg, norm_final[None, :])
    return out.reshape(B, T, D)
```

```python
import functools
import math

import numpy as np
import jax
import jax.numpy as jnp
from jax import lax
from jax.experimental import pallas as pl
from jax.experimental.pallas import tpu as pltpu
from jax.experimental.pallas import tpu_sc as plsc

F32 = jnp.float32
BF16 = jnp.bfloat16
NEG = -1e30
EPS = 1e-6

D = 1024
HD = 64
LANES = 128
MLA_HEADS, MLA_QR, MLA_KVR, MLA_NOPE, MLA_ROPE, MLA_V = 8, 256, 128, 64, 32, 64
ROPE_THETA = 10000.0
SWA_HEADS, SWA_KV, SWA_WIN = 8, 2, 128
MOBA_L, MOBA_TOPK = 256, 3
REL_BUCKETS, REL_MAX = 32, 2048
N_GROUPS, EPG, N_EXP, D_EXP = 4, 8, 32, 256
BIAS_CONST_FROM = 1792
MOBA_TQ = 2 * MOBA_L
BIAS_PAD = MOBA_L
BIAS_ROWS = BIAS_PAD + BIAS_CONST_FROM + MOBA_TQ
SB_DONE = -110.0
TM = 512
TM_E = 256
VMEM_LIMIT = 56 * 1024 * 1024
SC_CORES, SC_SUBCORES = 2, 16
SC_CHUNK = 128
HALF = D // 2
LOG2E = math.log2(math.e)


def _pack_halves(x):
    lo = pltpu.bitcast(x[:, :HALF].astype(BF16).astype(F32), jnp.uint32) >> 16
    hi = pltpu.bitcast(x[:, HALF:].astype(BF16).astype(F32), jnp.uint32) & jnp.uint32(0xFFFF0000)
    return pltpu.bitcast(lo | hi, jnp.int32)


def _unpack_halves(w):
    w = pltpu.bitcast(w, jnp.uint32)
    return pltpu.bitcast(w << 16, F32), pltpu.bitcast(w & jnp.uint32(0xFFFF0000), F32)


def _cparams(sem):
    return pltpu.CompilerParams(dimension_semantics=sem, vmem_limit_bytes=VMEM_LIMIT)


def _dot(a, b):
    return jnp.dot(a, b, preferred_element_type=F32)


def _dot_nt(a, b):
    return lax.dot_general(a, b, (((1,), (1,)), ((), ())), preferred_element_type=F32)


def _rms(x, g):
    return x * lax.rsqrt(jnp.mean(x * x, axis=-1, keepdims=True) + EPS) * g


def _with_ones_rows(vt):
    rows = lax.broadcasted_iota(jnp.int32, vt.shape, 0)
    return jnp.where((rows & HD) != 0, 1.0, vt)


def _flash_update_t(st, vt, m, acc):
    m_new = jnp.maximum(m, jnp.max(st, axis=0, keepdims=True))
    alpha = jnp.exp2(m - m_new)
    pt = jnp.exp2(st - m_new).astype(BF16)
    return m_new, alpha * acc + _dot(vt, pt)


def _flash_update(s, v, m, l, acc):
    m_new = jnp.maximum(m, jnp.max(s, axis=-1, keepdims=True))
    alpha = jnp.exp(m - m_new)
    p = jnp.exp(s - m_new)
    l = alpha * l + jnp.sum(p, axis=-1, keepdims=True)
    acc = alpha * acc + _dot(p.astype(BF16), v)
    return m_new, l, acc


def _t5_bucket_np(dist):
    n = np.maximum(dist, 0)
    max_exact = REL_BUCKETS // 2
    nf = np.maximum(n, 1).astype(np.float32)
    large = max_exact + (np.log(nf / np.float32(max_exact)) / np.float32(math.log(REL_MAX / max_exact))
                         * np.float32(REL_BUCKETS - max_exact)).astype(np.int32)
    large = np.minimum(large, REL_BUCKETS - 1)
    return np.where(n < max_exact, n, large).astype(np.int32)


def _bias_kernel(table_ref, bucket_ref, out_ref, *, scale):
    b = bucket_ref[...]
    for h in range(8):
        acc = jnp.zeros(b.shape, F32)
        for k in range(REL_BUCKETS):
            acc = jnp.where(b == k, table_ref[k, h] * scale, acc)
        out_ref[h] = acc


def _bias_lookup(rel_table, dist, scale, name):
    R, C = dist.shape
    return pl.pallas_call(
        functools.partial(_bias_kernel, scale=scale),
        grid=(C // 256,),
        in_specs=[pl.BlockSpec(memory_space=pltpu.SMEM),
                  pl.BlockSpec((R, 256), lambda m: (0, m))],
        out_specs=pl.BlockSpec((8, R, 256), lambda m: (0, 0, m)),
        out_shape=jax.ShapeDtypeStruct((8, R, C), F32),
        compiler_params=_cparams(("arbitrary",)),
        name=name,
    )(rel_table, jnp.asarray(_t5_bucket_np(dist)))


def _bias_tables(rel_table):
    swa = _bias_lookup(rel_table, np.arange(256)[:, None] - np.arange(256)[None, :], 1.0, "bias_swa")
    moba = _bias_lookup(rel_table, np.arange(BIAS_ROWS)[None, :] - BIAS_PAD - np.arange(MOBA_L)[:, None], LOG2E,
                        "bias_moba")
    return swa, moba


def _proj_even_kernel(h_ref, g_ref, win_ref, gq_ref, gkv_ref, wqa_ref, wqb_ref, wk_ref, wv_ref,
                      cq_ref, sq_ref, ck_ref, sk_ref,
                      q_ref, k_ref, v_ref, qs_ref, ks_ref, vs_ref, vsw_ref):
    u = _rms(h_ref[...], g_ref[...]).astype(BF16)
    nq = _rms(_dot(u, win_ref[:, 0:256]), gq_ref[...]).astype(BF16)
    qa = _dot(nq, wqa_ref[...])
    qb = _dot(nq, wqb_ref[...])
    cq, sq = cq_ref[...], sq_ref[...]
    for hh in range(MLA_HEADS):
        sl = slice(128 * hh, 128 * (hh + 1))
        q_ref[:, sl] = (qa[:, sl] * cq + qb[:, sl] * sq).astype(BF16)
    nkv = _rms(_dot(u, win_ref[:, 256:384]), gkv_ref[...]).astype(BF16)
    kn = _dot(nkv, wk_ref[...])
    v_ref[...] = _with_ones_rows(_dot_nt(wv_ref[...], nkv)).astype(BF16)
    kr = _dot(u, win_ref[:, 1792:1920]) * ck_ref[...] + _dot(u, win_ref[:, 1920:2048]) * sk_ref[...]
    for hh in range(MLA_HEADS):
        sl = slice(128 * hh, 128 * (hh + 1))
        k_ref[:, sl] = (kn[:, sl] + kr).astype(BF16)
    qs_ref[...] = _dot(u, win_ref[:, 384:1408]).astype(BF16)
    ks_ref[...] = _dot(u, win_ref[:, 1408:1536]).astype(BF16)
    vs_ref[...] = _dot(u, win_ref[:, 1536:1664]).astype(BF16)
    vsw_ref[...] = _dot(u, win_ref[:, 1664:1792]).astype(BF16)


def _cols(w, idx, scale=None):
    idx = np.asarray(idx)
    cuts = [0] + [n for n in range(1, len(idx)) if (idx[n] < 0) != (idx[n - 1] < 0)
                  or (idx[n] >= 0 and idx[n] != idx[n - 1] + 1)] + [len(idx)]
    parts = [jnp.zeros((w.shape[0], b - a), w.dtype) if idx[a] < 0 else w[:, idx[a]:idx[a] + b - a]
             for a, b in zip(cuts[:-1], cuts[1:])]
    out = jnp.concatenate(parts, axis=1)
    if scale is not None:
        out = out * scale
    return out


def _even_weights(w_in, w_qb, w_kvb):
    o_cq, o_ckv, o_kr = 0, MLA_QR, MLA_QR + MLA_KVR
    o_qs = o_kr + MLA_ROPE
    o_ks = o_qs + SWA_HEADS * HD
    o_vs = o_ks + SWA_KV * HD
    idx = list(range(o_cq, o_cq + MLA_QR)) + list(range(o_ckv, o_ckv + MLA_KVR))
    scale = [1.0] * len(idx)
    for hq in range(SWA_HEADS):
        c = hq // (SWA_HEADS // SWA_KV)
        blk = [-1] * 128
        blk[64 * c:64 * c + 64] = range(o_qs + 64 * hq, o_qs + 64 * hq + 64)
        idx += blk
        scale += [HD ** -0.5] * 128
    idx += list(range(o_ks, o_ks + 128))
    idx += list(range(o_vs, o_vs + 128))
    idx += list(range(o_vs + 64, o_vs + 128)) + list(range(o_vs, o_vs + 64))
    scale += [1.0] * 384
    half = MLA_ROPE // 2
    idx += [-1] * 64 + list(range(o_kr, o_kr + MLA_ROPE)) + [-1] * 32
    idx += [-1] * 64 + list(range(o_kr + half, o_kr + MLA_ROPE)) + list(range(o_kr, o_kr + half)) + [-1] * 32
    scale += [1.0] * 256
    win = _cols(w_in, idx, jnp.asarray(np.asarray(scale, np.float32))).astype(BF16)
    qw = MLA_NOPE + MLA_ROPE
    ia, ib, ik, iv = [], [], [], []
    for h in range(MLA_HEADS):
        ia += list(range(qw * h, qw * h + qw)) + [-1] * 32
        ib += [-1] * 64 + list(range(qw * h + MLA_NOPE + half, qw * h + qw)) \
            + list(range(qw * h + MLA_NOPE, qw * h + MLA_NOPE + half)) + [-1] * 32
        ik += list(range(128 * h, 128 * h + MLA_NOPE)) + [-1] * 64
        iv += list(range(128 * h + MLA_NOPE, 128 * h + 128)) + [-1] * 64
    return (win, _cols(w_qb, ia).astype(BF16), _cols(w_qb, ib).astype(BF16),
            _cols(w_kvb, ik).astype(BF16), _cols(w_kvb, iv).T.astype(BF16))


def _rope_tables(T):
    half = MLA_ROPE // 2
    freqs = ROPE_THETA ** (-jnp.arange(0, MLA_ROPE, 2, dtype=F32) / MLA_ROPE)
    ang = jnp.arange(T, dtype=F32)[:, None] * freqs[None, :]
    cos, sin = jnp.cos(ang), jnp.sin(ang)
    z = lambda n: jnp.zeros((T, n), F32)
    scale = (MLA_NOPE + MLA_ROPE) ** -0.5 * LOG2E
    ck = jnp.concatenate([z(64), cos, cos, z(32)], axis=1)
    sk = jnp.concatenate([z(64), -sin, sin, z(32)], axis=1)
    cq = jnp.concatenate([jnp.ones((T, 64), F32), cos, cos, z(32)], axis=1) * scale
    sq = sk * scale
    return cq, sq, ck, sk


def _proj_even(h, g, win, gq, gkv, wqa, wqb, wk, wv, tables, T):
    N = h.shape[0]
    nt = T // TM
    full = lambda a: pl.BlockSpec(a.shape, lambda i: (0,) * a.ndim)
    tab = pl.BlockSpec((TM, 128), lambda i: (i % nt, 0))
    row = lambda w: pl.BlockSpec((TM, w), lambda i: (i, 0))
    widths = (1024, 1024, None, 1024, 128, 128, 128)
    vt_spec = pl.BlockSpec((None, 1024, TM), lambda i: (i // nt, 0, i % nt))
    vt_shape = jax.ShapeDtypeStruct((N // T, 1024, T), BF16)
    return pl.pallas_call(
        _proj_even_kernel,
        grid=(N // TM,),
        in_specs=[row(D), full(g), full(win), full(gq), full(gkv), full(wqa), full(wqb), full(wk), full(wv),
                  tab, tab, tab, tab],
        out_specs=[vt_spec if w is None else row(w) for w in widths],
        out_shape=[vt_shape if w is None else jax.ShapeDtypeStruct((N, w), BF16) for w in widths],
        compiler_params=_cparams(("arbitrary",)),
        name="proj_even",
    )(h, g, win, gq, gkv, wqa, wqb, wk, wv, *tables)


def _finish_t(accs):
    ot = jnp.concatenate([acc[:HD] / acc[HD:HD + 1] for acc in accs], axis=0)
    return ot.T.astype(BF16)


def _causal_attention_t(i, q_of, k_ref, vt_ref, o_ref, s_ref, m_ref, acc_ref, *, tq, tk, bias_of=None):
    n_full = i * (tq // tk)
    key = lax.broadcasted_iota(jnp.int32, (tk, tq), 0)
    qry = i * tq + lax.broadcasted_iota(jnp.int32, (tk, tq), 1)

    def scores_to(slot, start, boundary):
        start = pl.multiple_of(start, tk)
        for j in range(2):
            st = _dot_nt(k_ref[pl.ds(start, tk), 128 * j:128 * (j + 1)], q_of(j))
            if bias_of is not None:
                st = st + bias_of(start, j)
            if boundary:
                st = jnp.where(start + key <= qry, st, NEG)
            s_ref[slot, j] = st

    def update_from(slot, start):
        start = pl.multiple_of(start, tk)
        for j in range(2):
            m_new, acc = _flash_update_t(s_ref[slot, j], vt_ref[128 * j:128 * (j + 1), pl.ds(start, tk)],
                                         m_ref[j], acc_ref[j])
            m_ref[j] = m_new
            acc_ref[j] = acc

    m_ref[...] = jnp.full(m_ref.shape, NEG, F32)
    acc_ref[...] = jnp.zeros(acc_ref.shape, F32)
    for d in range(tq // tk - 1):
        scores_to(0, (n_full + d) * tk, True)
        update_from(0, (n_full + d) * tk)
    first = (n_full + tq // tk - 1) * tk
    scores_to(0, first, True)

    def block_start(k):
        return jnp.where(k == 0, first, (k - 1) * tk)

    def pair(p, _):
        scores_to(1, block_start(2 * p + 1), False)
        update_from(0, block_start(2 * p))
        scores_to(0, block_start(2 * p + 2), False)
        update_from(1, block_start(2 * p + 1))
        return 0

    lax.fori_loop(0, n_full // 2, pair, 0)

    @pl.when(n_full % 2 == 1)
    def _():
        scores_to(1, block_start(n_full), False)
        update_from(0, block_start(n_full - 1))
        update_from(1, block_start(n_full))

    @pl.when(n_full % 2 == 0)
    def _():
        update_from(0, block_start(n_full))

    o_ref[...] = _finish_t([acc_ref[j] for j in range(2)])


def _attention_t_scratch(tq, tk):
    return [pltpu.VMEM((2, 2, tk, tq), F32), pltpu.VMEM((2, 1, tq), F32), pltpu.VMEM((2, 128, tq), F32)]


def _mla_kernel(q_ref, k_ref, vt_ref, o_ref, s_ref, m_ref, acc_ref, *, tq, tk):
    _causal_attention_t(pl.program_id(2), lambda j: q_ref[:, 128 * j:128 * (j + 1)], k_ref, vt_ref, o_ref,
                        s_ref, m_ref, acc_ref, tq=tq, tk=tk)


def _mla_attention(q, k, vt, B, T, tq=512, tk=512):
    nq = T // tq
    return pl.pallas_call(
        functools.partial(_mla_kernel, tq=tq, tk=tk),
        grid=(B, MLA_HEADS // 2, nq),
        in_specs=[pl.BlockSpec((tq, 256), lambda b, p, i: (b * nq + i, p)),
                  pl.BlockSpec((T, 256), lambda b, p, i: (b, p)),
                  pl.BlockSpec((None, 256, T), lambda b, p, i: (b, p, 0))],
        out_specs=pl.BlockSpec((tq, 128), lambda b, p, i: (b * nq + i, p)),
        out_shape=jax.ShapeDtypeStruct((B * T, MLA_HEADS * MLA_V), BF16),
        scratch_shapes=_attention_t_scratch(tq, tk),
        compiler_params=_cparams(("arbitrary", "arbitrary", "arbitrary")),
        name="mla_attention",
    )(q, k, vt)


def _swa_kernel(sink_ref, q_ref, k_ref, v_ref, vsw_ref, tb_ref, o_ref, *, tq):
    i = pl.program_id(1)
    G = SWA_HEADS // SWA_KV
    qi = lax.broadcasted_iota(jnp.int32, (128, 256), 0)
    kj = lax.broadcasted_iota(jnp.int32, (128, 256), 1)
    lane = lax.broadcasted_iota(jnp.int32, (128, 128), 1)
    for r in range(tq // 128):
        qstart = i * tq + 128 * r
        kstart = pl.multiple_of(jnp.maximum(qstart - 128, 0), 128)
        off = pl.multiple_of(qstart - kstart, 128)
        dist = off + qi - kj
        valid = (dist >= 0) & (dist < SWA_WIN)
        kw = k_ref[pl.ds(kstart, 256), :]
        vw = v_ref[pl.ds(kstart, 256), :]
        vsw = vsw_ref[pl.ds(kstart, 256), :]
        halves = [None] * SWA_HEADS
        for c in range(SWA_KV):
            qg = jnp.concatenate([q_ref[128 * r:128 * (r + 1), 128 * (G * c + g):128 * (G * c + g + 1)]
                                  for g in range(G)], axis=0)
            s_all = _dot_nt(qg, kw)
            for g in range(G):
                hq = G * c + g
                s = s_all[128 * g:128 * (g + 1)] + tb_ref[hq, pl.ds(off, 128), :]
                s = jnp.where(valid, s, NEG)
                sink = sink_ref[hq]
                m = jnp.maximum(jnp.max(s, axis=-1, keepdims=True), sink)
                p = jnp.exp(s - m)
                denom = jnp.sum(p, axis=-1, keepdims=True) + jnp.exp(sink - m)
                p = (p / denom).astype(BF16)
                halves[hq] = _dot(p, vw if c == hq % 2 else vsw)
        for mp in range(SWA_HEADS // 2):
            o_ref[128 * r:128 * (r + 1), 128 * mp:128 * (mp + 1)] = jnp.where(
                lane < HD, halves[2 * mp], halves[2 * mp + 1]).astype(BF16)


def _swa_attention(q8, k, v, vsw, sinks, tb, B, T, tq=512):
    nq = T // tq
    kv = pl.BlockSpec((T, 128), lambda b, i: (b, 0))
    return pl.pallas_call(
        functools.partial(_swa_kernel, tq=tq),
        grid=(B, nq),
        in_specs=[pl.BlockSpec(memory_space=pltpu.SMEM),
                  pl.BlockSpec((tq, 1024), lambda b, i: (b * nq + i, 0)),
                  kv, kv, kv,
                  pl.BlockSpec((8, 256, 256), lambda b, i: (0, 0, 0))],
        out_specs=pl.BlockSpec((tq, 512), lambda b, i: (b * nq + i, 0)),
        out_shape=jax.ShapeDtypeStruct((B * T, SWA_HEADS * HD), BF16),
        compiler_params=_cparams(("arbitrary", "arbitrary")),
        name="swa_attention",
    )(sinks, q8, k, v, vsw, tb)


def _outproj_router_kernel(h_ref, oa_ref, ob_ref, wo_ref, g_ref, wr_ref, br_ref,
                           hn_ref, u_ref, ri_ref, rg_ref, cnt_ref, base_ref):
    i = pl.program_id(0)

    @pl.when(i == 0)
    def _():
        base_ref[...] = jnp.zeros_like(base_ref)

    hn = h_ref[...] + _dot(oa_ref[...], wo_ref[0:512, :]) + _dot(ob_ref[...], wo_ref[512:1024, :])
    hn_ref[...] = hn
    u = _rms(hn, g_ref[...])
    u_ref[...] = _pack_halves(u)
    u_hi = u.astype(BF16)
    u_lo = (u - u_hi.astype(F32)).astype(BF16)
    both = _dot_nt(wr_ref[...], u_hi)
    logits = both[:128] + both[128:] + _dot_nt(wr_ref[0:128, :], u_lo) + br_ref[...]
    tm = logits.shape[1]
    sub = lax.broadcasted_iota(jnp.int32, (8, tm), 0).astype(F32)
    big = 1e6
    isg = sub < N_GROUPS
    gl = jnp.where(isg, logits[N_EXP:N_EXP + 8], NEG)
    gmax = jnp.max(gl, axis=0, keepdims=True)
    gsel = jnp.min(jnp.where(gl == gmax, sub, big), axis=0, keepdims=True)
    g_gate = 1.0 / jnp.sum(jnp.where(isg, jnp.exp(gl - gmax), 0.0), axis=0, keepdims=True)
    el = jnp.zeros((EPG, tm), F32)
    for grp in range(N_GROUPS):
        el = jnp.where(gsel == grp, logits[EPG * grp:EPG * (grp + 1)], el)
    m1 = jnp.max(el, axis=0, keepdims=True)
    i1 = jnp.min(jnp.where(el == m1, sub, big), axis=0, keepdims=True)
    el2 = jnp.where(sub == i1, NEG, el)
    m2 = jnp.max(el2, axis=0, keepdims=True)
    i2 = jnp.min(jnp.where(el2 == m2, sub, big), axis=0, keepdims=True)
    e1 = gsel * EPG + i1
    e2 = gsel * EPG + i2
    r = jnp.exp(m2 - m1)
    gate1 = g_gate / (1.0 + r)
    gate2 = g_gate * r / (1.0 + r)
    rows = lax.broadcasted_iota(jnp.int32, (128, tm), 0).astype(F32)
    oh1 = (rows == e1).astype(F32)
    oh2 = (rows == e2).astype(F32)
    oh = oh1 + oh2
    t_row = lax.broadcasted_iota(jnp.int32, (tm, tm), 0)
    t_col = lax.broadcasted_iota(jnp.int32, (tm, tm), 1)
    earlier = jnp.where(t_row < t_col, 1.0, 0.0).astype(BF16)
    prefix = _dot(oh.astype(BF16), earlier) + base_ref[:, 0:1]
    rank1 = jnp.sum(oh1 * prefix, axis=0, keepdims=True)
    rank2 = jnp.sum(oh2 * prefix, axis=0, keepdims=True)
    base_ref[...] = base_ref[...] + jnp.sum(oh, axis=1, keepdims=True)
    cnt_ref[...] = base_ref[...]
    ri_ref[...] = jnp.where(sub == 0, e1, jnp.where(sub == 1, e2, jnp.where(sub == 2, rank1,
                            jnp.where(sub == 3, rank2, 0.0))))
    rg_ref[...] = jnp.where(rows == 0, gate1, jnp.where(rows == 1, gate2, 0.0)).T


def _outproj_router(h, oa, ob, wo, g, wr, br):
    N = h.shape[0]
    full = lambda a: pl.BlockSpec(a.shape, lambda i: (0,) * a.ndim)
    row = lambda w: pl.BlockSpec((TM, w), lambda i: (i, 0))
    return pl.pallas_call(
        _outproj_router_kernel,
        grid=(N // TM,),
        in_specs=[row(D), row(512), row(512), full(wo), full(g), full(wr), full(br)],
        out_specs=[row(D), row(HALF), pl.BlockSpec((8, TM), lambda i: (0, i)), row(128),
                   pl.BlockSpec((128, 128), lambda i: (0, 0))],
        out_shape=[jax.ShapeDtypeStruct((N, D), F32), jax.ShapeDtypeStruct((N, HALF), jnp.int32),
                   jax.ShapeDtypeStruct((8, N), F32), jax.ShapeDtypeStruct((N, 128), F32),
                   jax.ShapeDtypeStruct((128, 128), F32)],
        scratch_shapes=[pltpu.VMEM((128, 128), F32)],
        compiler_params=_cparams(("arbitrary",)),
        name="outproj_router",
    )(h, oa, ob, wo, g, wr, br)


def _moe_kernel(te_ref, tv_ref, x_ref, wg_ref, wu_ref, wd_ref, o_ref, wgu_s, wd_s):
    i = pl.program_id(0)
    valid = tv_ref[i]

    @pl.when((i == 0) | (te_ref[i] != te_ref[jnp.maximum(i - 1, 0)]))
    def _():
        wgu_s[:, :D_EXP] = wg_ref[...].astype(BF16)
        wgu_s[:, D_EXP:] = wu_ref[...].astype(BF16)
        wd_s[...] = wd_ref[...].astype(BF16)

    @pl.when(valid > 0)
    def _():
        rows = lax.broadcasted_iota(jnp.int32, x_ref.shape, 0)
        lo, hi = _unpack_halves(jnp.where(rows < valid, x_ref[...], 0))
        hgu = _dot(lo.astype(BF16), wgu_s[:HALF, :]) + _dot(hi.astype(BF16), wgu_s[HALF:, :])
        hg, hu = hgu[:, :D_EXP], hgu[:, D_EXP:]
        a = hg * (1.0 / (1.0 + jnp.exp(-hg))) * hu
        o_ref[...] = _pack_halves(_dot(a.astype(BF16), wd_s[...]))

    @pl.when(valid <= 0)
    def _():
        o_ref[...] = jnp.zeros_like(o_ref)


def _moe_experts(xs, tile_expert, tile_valid, w_gate, w_up, w_down):
    P = xs.shape[0]
    grid_spec = pltpu.PrefetchScalarGridSpec(
        num_scalar_prefetch=2,
        grid=(P // TM_E,),
        in_specs=[pl.BlockSpec((TM_E, HALF), lambda i, te, tv: (i, 0)),
                  pl.BlockSpec((None, D, D_EXP), lambda i, te, tv: (te[i], 0, 0)),
                  pl.BlockSpec((None, D, D_EXP), lambda i, te, tv: (te[i], 0, 0)),
                  pl.BlockSpec((None, D_EXP, D), lambda i, te, tv: (te[i], 0, 0))],
        out_specs=pl.BlockSpec((TM_E, HALF), lambda i, te, tv: (i, 0)),
        scratch_shapes=[pltpu.VMEM((D, 2 * D_EXP), BF16), pltpu.VMEM((D_EXP, D), BF16)],
    )
    return pl.pallas_call(
        _moe_kernel,
        grid_spec=grid_spec,
        out_shape=jax.ShapeDtypeStruct((P, HALF), jnp.int32),
        compiler_params=_cparams(("arbitrary",)),
        name="moe_experts",
    )(tile_expert, tile_valid, xs, w_gate, w_up, w_down)


def _sc_mesh():
    return plsc.VectorSubcoreMesh(core_axis_name="c", subcore_axis_name="s",
                                  num_cores=SC_CORES, num_subcores=SC_SUBCORES)


def _sc_scatter_rows(src, d1, d2, P):
    N, W = src.shape
    per_w = N // (SC_CORES * SC_SUBCORES)

    @functools.partial(pl.kernel, mesh=_sc_mesh(), out_type=jax.ShapeDtypeStruct((P, W), src.dtype),
                       scratch_types=[pltpu.VMEM((SC_CHUNK,), jnp.int32), pltpu.VMEM((SC_CHUNK, W), src.dtype)],
                       name="sc_scatter_rows")
    def k(src_hbm, d1_hbm, d2_hbm, out_hbm, idx_v, rows_v):
        wid = lax.axis_index("s") * SC_CORES + lax.axis_index("c")

        @pl.loop(0, per_w // SC_CHUNK)
        def _(c):
            off = wid * per_w + c * SC_CHUNK
            pltpu.sync_copy(src_hbm.at[pl.ds(off, SC_CHUNK)], rows_v)
            pltpu.sync_copy(d1_hbm.at[pl.ds(off, SC_CHUNK)], idx_v)
            pltpu.sync_copy(rows_v, out_hbm.at[idx_v])
            pltpu.sync_copy(d2_hbm.at[pl.ds(off, SC_CHUNK)], idx_v)
            pltpu.sync_copy(rows_v, out_hbm.at[idx_v])

    return k(src, d1, d2)


def _sc_gather_rows(table, idx):
    B, W = idx.shape[0], table.shape[1]
    per_w = B // (SC_CORES * SC_SUBCORES)

    @functools.partial(pl.kernel, mesh=_sc_mesh(), out_type=jax.ShapeDtypeStruct((B, W), table.dtype),
                       scratch_types=[pltpu.VMEM((SC_CHUNK,), jnp.int32), pltpu.VMEM((SC_CHUNK, W), table.dtype)],
                       name="sc_gather_rows")
    def k(table_hbm, idx_hbm, out_hbm, idx_v, rows_v):
        wid = lax.axis_index("s") * SC_CORES + lax.axis_index("c")

        @pl.loop(0, per_w // SC_CHUNK)
        def _(c):
            off = wid * per_w + c * SC_CHUNK
            pltpu.sync_copy(idx_hbm.at[pl.ds(off, SC_CHUNK)], idx_v)
            pltpu.sync_copy(table_hbm.at[idx_v], rows_v)
            pltpu.sync_copy(rows_v, out_hbm.at[pl.ds(off, SC_CHUNK)])

    return k(table, idx)


def _dest_kernel(offs_ref, ri_ref, d_ref):
    ri = ri_ref[...]
    base = jnp.zeros(ri.shape, F32)
    for e in range(N_EXP):
        base = jnp.where(ri == e, offs_ref[e].astype(F32), base)
    d_ref[...] = (base + pltpu.roll(ri, 6, 0)).astype(jnp.int32)


def _dispatch_plan(ri, cnt, N):
    counts = cnt[:N_EXP, 0].astype(jnp.int32)
    padded = ((counts + TM_E - 1) // TM_E) * TM_E
    ends = jnp.cumsum(padded)
    offs = ends - padded
    take = lambda t, i: t.at[i].get(mode="promise_in_bounds")
    dest = pl.pallas_call(
        _dest_kernel,
        in_specs=[pl.BlockSpec(memory_space=pltpu.SMEM), pl.BlockSpec((8, N), lambda: (0, 0))],
        out_specs=pl.BlockSpec((8, N), lambda: (0, 0)),
        out_shape=jax.ShapeDtypeStruct((8, N), jnp.int32),
        name="dispatch_rows",
    )(offs, ri)
    d1, d2 = dest[0], dest[1]
    P = 2 * N + N_EXP * TM_E
    tile_start = jnp.arange(P // TM_E, dtype=jnp.int32) * TM_E
    tile_expert = jnp.minimum(jnp.sum((ends[None, :] <= tile_start[:, None]).astype(jnp.int32), axis=1), N_EXP - 1)
    tile_valid = jnp.clip(take(offs + counts, tile_expert) - tile_start, 0, TM_E)
    return d1, d2, P, tile_expert, tile_valid


def _combine(h_ref, y1_ref, y2_ref, rg_ref):
    rg = rg_ref[...]
    lo1, hi1 = _unpack_halves(y1_ref[...])
    lo2, hi2 = _unpack_halves(y2_ref[...])
    g1, g2 = rg[:, 0:1], rg[:, 1:2]
    return h_ref[...] + jnp.concatenate([g1 * lo1 + g2 * lo2, g1 * hi1 + g2 * hi2], axis=1)


ODD_WIDTHS = (1024, 1024, 512, 512, 512)


def _combine_proj_odd_kernel(h_ref, y1_ref, y2_ref, rg_ref, g_ref, win_ref, wvt_ref, ind_ref, hn_ref, vt_ref,
                             *out_refs):
    hn = _combine(h_ref, y1_ref, y2_ref, rg_ref)
    hn_ref[...] = hn
    u = _rms(hn, g_ref[...]).astype(BF16)
    vt_ref[...] = _with_ones_rows(_dot_nt(wvt_ref[...], u)).astype(BF16)
    c0 = 0
    for n, o_ref in enumerate(out_refs):
        w = ODD_WIDTHS[n]
        if n == 1:
            for hh in range(8):
                o_ref[:, 128 * hh:128 * (hh + 1)] = (
                    _dot(u, win_ref[:, c0 + 128 * hh:c0 + 128 * (hh + 1)]) + ind_ref[...]).astype(BF16)
        else:
            for c in range(0, w, 512):
                o_ref[:, c:c + 512] = _dot(u, win_ref[:, c0 + c:c0 + c + 512]).astype(BF16)
        c0 += w


def _odd_weights(w_in):
    idx, scale = [], []
    for base, s in ((0, HD ** -0.5 * LOG2E), (512, 1.0)):
        for hh in range(8):
            idx += list(range(base + 64 * hh, base + 64 * hh + 64)) + [-1] * 64
            scale += [s] * 128
    idx += list(range(1536, 3072))
    scale += [HD ** -0.5] * 512 + [1.0] * 1024
    win = _cols(w_in, idx, jnp.asarray(np.asarray(scale, np.float32))).astype(BF16)
    iv = []
    for hh in range(8):
        iv += list(range(1024 + 64 * hh, 1024 + 64 * hh + 64)) + [-1] * 64
    return win, _cols(w_in, iv).T.astype(BF16)


def _block_indicator(T):
    assert T // MOBA_L <= 64
    ind = np.zeros((T, 128), np.float32)
    ind[np.arange(T), 64 + np.arange(T) // MOBA_L] = 1.0
    return jnp.asarray(ind)


def _y_specs(N):
    return [pl.BlockSpec((TM, HALF), lambda i: (i, 0)), pl.BlockSpec((TM, HALF), lambda i: (i + N // TM, 0))]


def _combine_proj_odd(h, y12, rg, g, win, wvt, ind, T):
    N = h.shape[0]
    nt = T // TM
    full = lambda a: pl.BlockSpec(a.shape, lambda i: (0,) * a.ndim)
    row = lambda w: pl.BlockSpec((TM, w), lambda i: (i, 0))
    return pl.pallas_call(
        _combine_proj_odd_kernel,
        grid=(N // TM,),
        in_specs=[row(D)] + _y_specs(N) + [row(128), full(g), full(win), full(wvt),
                  pl.BlockSpec((TM, 128), lambda i: (i % nt, 0))],
        out_specs=[row(D), pl.BlockSpec((None, 1024, TM), lambda i: (i // nt, 0, i % nt))]
        + [row(w) for w in ODD_WIDTHS],
        out_shape=[jax.ShapeDtypeStruct((N, D), F32), jax.ShapeDtypeStruct((N // T, 1024, T), BF16)]
        + [jax.ShapeDtypeStruct((N, w), BF16) for w in ODD_WIDTHS],
        compiler_params=_cparams(("arbitrary",)),
        name="combine_proj_odd",
    )(h, y12, y12, rg, g, win, wvt, ind)


def _combine_final_kernel(h_ref, y1_ref, y2_ref, rg_ref, g_ref, o_ref):
    o_ref[...] = _rms(_combine(h_ref, y1_ref, y2_ref, rg_ref), g_ref[...])


def _combine_final(h, y12, rg, g):
    N = h.shape[0]
    row = lambda w: pl.BlockSpec((TM, w), lambda i: (i, 0))
    return pl.pallas_call(
        _combine_final_kernel,
        grid=(N // TM,),
        in_specs=[row(D)] + _y_specs(N) + [row(128), pl.BlockSpec((1, D), lambda i: (0, 0))],
        out_specs=row(D),
        out_shape=jax.ShapeDtypeStruct((N, D), F32),
        compiler_params=_cparams(("arbitrary",)),
        name="combine_final",
    )(h, y12, y12, rg, g)


def _moba_kernel(q_ref, k_ref, vt_ref, tb_ref, o_ref, kmean_ref, qa_ref, s_ref, m_ref, acc_ref, *, nkb):
    i = pl.program_id(2)
    L, tq = MOBA_L, MOBA_TQ

    @pl.when(i == 0)
    def _():
        kmean_ref[...] = jnp.zeros_like(kmean_ref)
        for n in range(nkb):
            kmean_ref[64 + n:65 + n, :] = jnp.sum(k_ref[n * L:(n + 1) * L, :].astype(F32), axis=0,
                                                  keepdims=True) * (1.0 / L)

    lane = lax.broadcasted_iota(jnp.int32, (tq, 128), 1)
    rowi = lax.broadcasted_iota(jnp.int32, (tq, 128), 0)
    own = 2 * i + jnp.where(rowi >= L, 1, 0)
    blk = lane - 64
    lane_f = lane.astype(F32)
    for j in range(2):
        qj = q_ref[:, 128 * j:128 * (j + 1)]
        gsc = lax.dot_general(qj.astype(F32), kmean_ref[:, 128 * j:128 * (j + 1)], (((1,), (1,)), ((), ())),
                              preferred_element_type=F32, precision=lax.Precision.HIGHEST)
        g = jnp.where((blk >= 0) & (blk < own), gsc, NEG)
        allowed = blk == own
        for _ in range(MOBA_TOPK):
            mx = jnp.max(g, axis=-1, keepdims=True)
            idx = jnp.min(jnp.where(g == mx, lane_f, 1e6), axis=-1, keepdims=True)
            pick = lane_f == idx
            allowed = allowed | (pick & (mx > 0.5 * NEG))
            g = jnp.where(pick, 2.0 * NEG, g)
        qa_ref[j] = jnp.where(lane < HD, qj, jnp.where(allowed, 0.0, NEG).astype(BF16))

    def bias_of(start, j):
        parts = []
        for c in range(tq // L):
            off = jnp.minimum(i * tq - start - c * L, BIAS_CONST_FROM) + BIAS_PAD
            parts.append(tb_ref[j, :, pl.ds(pl.multiple_of(off, L), tq)])
        return jnp.concatenate(parts, axis=0)

    _causal_attention_t(i, lambda j: qa_ref[j], k_ref, vt_ref, o_ref, s_ref, m_ref, acc_ref, tq=tq, tk=tq,
                        bias_of=bias_of)


def _moba_attention(q, k, vt, tb, B, T):
    tq = MOBA_TQ
    nq = T // tq
    return pl.pallas_call(
        functools.partial(_moba_kernel, nkb=T // MOBA_L),
        grid=(B, 4, nq),
        in_specs=[pl.BlockSpec((tq, 256), lambda b, p, i: (b * nq + i, p)),
                  pl.BlockSpec((T, 256), lambda b, p, i: (b, p)),
                  pl.BlockSpec((None, 256, T), lambda b, p, i: (b, p, 0)),
                  pl.BlockSpec((2, MOBA_L, BIAS_ROWS), lambda b, p, i: (p, 0, 0))],
        out_specs=pl.BlockSpec((tq, 128), lambda b, p, i: (b * nq + i, p)),
        out_shape=jax.ShapeDtypeStruct((B * T, 512), BF16),
        scratch_shapes=[pltpu.VMEM((128, 256), F32), pltpu.VMEM((2, tq, 128), BF16)]
        + _attention_t_scratch(tq, tq),
        compiler_params=_cparams(("arbitrary", "arbitrary", "arbitrary")),
        name="moba_attention",
    )(q, k, vt, tb)


def _sb_kernel(q_ref, k_ref, v_ref, o_ref, *, tq):
    i = pl.program_id(2)
    lane = lax.broadcasted_iota(jnp.int32, (tq, 128), 1)
    q_all = q_ref[...]
    qs = [jnp.where(lane < HD, q_all, 0).astype(BF16), jnp.where(lane >= HD, q_all, 0).astype(BF16)]
    row = lax.broadcasted_iota(jnp.int32, (tq, tq), 0)
    col = lax.broadcasted_iota(jnp.int32, (tq, tq), 1)
    past = col < row
    after_mat = jnp.where(row > col, 1.0, 0.0).astype(BF16)

    def step(kb, carry, diag):
        start = pl.multiple_of(kb * tq, tq)
        k = k_ref[pl.ds(start, tq), :]
        v = v_ref[pl.ds(start, tq), :]
        out = []
        for j in range(2):
            c, acc = carry[j]
            z = _dot_nt(qs[j], k)
            sp = jnp.log(1.0 + jnp.exp(-jnp.abs(z)))
            log1m = -(jnp.maximum(z, 0.0) + sp)
            logsig = jnp.minimum(z, 0.0) - sp
            if diag:
                log1m = jnp.where(past, log1m, 0.0)
            hi = log1m.astype(BF16)
            lo = (log1m - hi.astype(F32)).astype(BF16)
            after = c + _dot(hi, after_mat) + _dot(lo, after_mat)
            w = jnp.exp(logsig + after)
            if diag:
                w = jnp.where(past, w, 0.0)
            acc = acc + _dot(w.astype(BF16), v)
            c = c + jnp.sum(log1m, axis=-1, keepdims=True)
            out.append((c, acc))
        return tuple(out)

    def live(carry):
        return (jnp.maximum(jnp.max(carry[0][0]), jnp.max(carry[1][0])) > SB_DONE).astype(jnp.int32)

    def body(state):
        t, _, carry = state
        carry = step(i - 1 - t, carry, False)
        return t + 1, live(carry), carry

    init = tuple((jnp.zeros((tq, 1), F32), jnp.zeros((tq, 128), F32)) for _ in range(2))
    carry = step(i, init, True)
    _, _, carry = lax.while_loop(lambda st: (st[0] < i) & (st[1] > 0), body, (jnp.int32(0), live(carry), carry))
    o_ref[...] = jnp.where(lane < HD, carry[0][1], carry[1][1]).astype(BF16)


def _sb_attention(q, k, v, B, T, tq=256):
    nq = T // tq
    kv = pl.BlockSpec((T, 128), lambda b, p, i: (b, p))
    return pl.pallas_call(
        functools.partial(_sb_kernel, tq=tq),
        grid=(B, 4, nq),
        in_specs=[pl.BlockSpec((tq, 128), lambda b, p, i: (b * nq + i, p)), kv, kv],
        out_specs=pl.BlockSpec((tq, 128), lambda b, p, i: (b * nq + i, p)),
        out_shape=jax.ShapeDtypeStruct((B * T, 512), BF16),
        compiler_params=_cparams(("arbitrary", "arbitrary", "arbitrary")),
        name="sb_attention",
    )(q, k, v)


def _router_weights(w_group, b_group, w_router, b_router):
    pad = 128 - N_EXP - N_GROUPS
    wr = jnp.concatenate([w_router, w_group, jnp.zeros((D, pad), F32)], axis=1).T
    hi = wr.astype(BF16)
    lo = (wr - hi.astype(F32)).astype(BF16)
    br = jnp.concatenate([b_router, b_group, jnp.zeros((pad,), F32)])[:, None]
    return jnp.concatenate([hi, lo], axis=0), br


def _ffn(h, oa, ob, wo, g, w_group, b_group, w_router, b_router, w_gate, w_up, w_down):
    wr, br = _router_weights(w_group, b_group, w_router, b_router)
    hn, u, ri, rg, cnt = _outproj_router(h, oa, ob, wo.astype(BF16), g[None, :], wr, br)
    d1, d2, P, tile_expert, tile_valid = _dispatch_plan(ri, cnt, h.shape[0])
    xs = _sc_scatter_rows(u, d1, d2, P)
    ys = _moe_experts(xs, tile_expert, tile_valid, w_gate, w_up, w_down)
    return hn, _sc_gather_rows(ys, jnp.concatenate([d1, d2])), rg


def kernel(x, norm_mix, norm_ffn, norm_final, rel_bias_table, w_in_even, g_mla_q, g_mla_kv, w_mla_qb, w_mla_kvb, swa_sinks, w_out_even, w_in_odd, w_out_odd, moe_w_group, moe_b_group, moe_w_router, moe_b_router, moe_w_gate, moe_w_up, moe_w_down):
    B, T, _ = x.shape
    h = x.reshape(B * T, D)
    tb_swa, tb_moba = _bias_tables(rel_bias_table)

    win, wqa, wqb, wk, wv = _even_weights(w_in_even[0], w_mla_qb[0], w_mla_kvb[0])
    q, k, v, qs8, ks, vs, vsw = _proj_even(h, norm_mix[0][None, :], win, g_mla_q[0][None, :], g_mla_kv[0][None, :],
                                           wqa, wqb, wk, wv, _rope_tables(T), T)
    oa = _mla_attention(q, k, v, B, T)
    ob = _swa_attention(qs8, ks, vs, vsw, swa_sinks[0], tb_swa, B, T)
    h, y12, rg = _ffn(h, oa, ob, w_out_even[0], norm_ffn[0], moe_w_group[0], moe_b_group[0], moe_w_router[0],
                         moe_b_router[0], moe_w_gate[0], moe_w_up[0], moe_w_down[0])

    h, vct, qc, kc, qd, kd, vd = _combine_proj_odd(h, y12, rg, norm_mix[1][None, :], *_odd_weights(w_in_odd[0]),
                                                   _block_indicator(T), T)
    oc = _moba_attention(qc, kc, vct, tb_moba, B, T)
    od = _sb_attention(qd, kd, vd, B, T)
    h, y12, rg = _ffn(h, oc, od, w_out_odd[0], norm_ffn[1], moe_w_group[1], moe_b_group[1], moe_w_router[1],
                         moe_b_router[1], moe_w_gate[1], moe_w_up[1], moe_w_down[1])
    out = _combine_final(h, y12, rg, norm_final[None, :])
    return out.reshape(B, T, D)
```

```python
import functools
import math

import numpy as np
import jax
import jax.numpy as jnp
from jax import lax
from jax.experimental import pallas as pl
from jax.experimental.pallas import tpu as pltpu
from jax.experimental.pallas import tpu_sc as plsc

F32 = jnp.float32
BF16 = jnp.bfloat16
NEG = -1e30
EPS = 1e-6

D = 1024
HD = 64
LANES = 128
MLA_HEADS, MLA_QR, MLA_KVR, MLA_NOPE, MLA_ROPE, MLA_V = 8, 256, 128, 64, 32, 64
ROPE_THETA = 10000.0
SWA_HEADS, SWA_KV, SWA_WIN = 8, 2, 128
MOBA_L, MOBA_TOPK = 256, 3
REL_BUCKETS, REL_MAX = 32, 2048
N_GROUPS, EPG, N_EXP, D_EXP = 4, 8, 32, 256
BIAS_CONST_FROM = 1792
MOBA_TQ = 2 * MOBA_L
BIAS_PAD = MOBA_L
BIAS_ROWS = BIAS_PAD + BIAS_CONST_FROM + MOBA_TQ
SB_DONE = -160.0
TM = 512
TM_E = 256
VMEM_LIMIT = 56 * 1024 * 1024
SC_CORES, SC_SUBCORES = 2, 16
SC_CHUNK = 128
HALF = D // 2
LOG2E = math.log2(math.e)


def _pack_halves(x):
    lo = pltpu.bitcast(x[:, :HALF].astype(BF16).astype(F32), jnp.uint32) >> 16
    hi = pltpu.bitcast(x[:, HALF:].astype(BF16).astype(F32), jnp.uint32) & jnp.uint32(0xFFFF0000)
    return pltpu.bitcast(lo | hi, jnp.int32)


def _unpack_halves(w):
    w = pltpu.bitcast(w, jnp.uint32)
    return pltpu.bitcast(w << 16, F32), pltpu.bitcast(w & jnp.uint32(0xFFFF0000), F32)


def _cparams(sem):
    return pltpu.CompilerParams(dimension_semantics=sem, vmem_limit_bytes=VMEM_LIMIT)


def _dot(a, b):
    return jnp.dot(a, b, preferred_element_type=F32)


def _dot_nt(a, b):
    return lax.dot_general(a, b, (((1,), (1,)), ((), ())), preferred_element_type=F32)


def _rms(x, g):
    return x * lax.rsqrt(jnp.mean(x * x, axis=-1, keepdims=True) + EPS) * g


def _with_ones_rows(vt):
    rows = lax.broadcasted_iota(jnp.int32, vt.shape, 0)
    return jnp.where((rows & HD) != 0, 1.0, vt)


def _flash_update_t(st, vt, m, acc):
    m_new = jnp.maximum(m, jnp.max(st, axis=0, keepdims=True))
    alpha = jnp.exp2(m - m_new)
    pt = jnp.exp2(st - m_new).astype(BF16)
    return m_new, alpha * acc + _dot(vt, pt)


def _flash_update(s, v, m, l, acc):
    m_new = jnp.maximum(m, jnp.max(s, axis=-1, keepdims=True))
    alpha = jnp.exp(m - m_new)
    p = jnp.exp(s - m_new)
    l = alpha * l + jnp.sum(p, axis=-1, keepdims=True)
    acc = alpha * acc + _dot(p.astype(BF16), v)
    return m_new, l, acc


def _t5_bucket_np(dist):
    n = np.maximum(dist, 0)
    max_exact = REL_BUCKETS // 2
    nf = np.maximum(n, 1).astype(np.float32)
    large = max_exact + (np.log(nf / np.float32(max_exact)) / np.float32(math.log(REL_MAX / max_exact))
                         * np.float32(REL_BUCKETS - max_exact)).astype(np.int32)
    large = np.minimum(large, REL_BUCKETS - 1)
    return np.where(n < max_exact, n, large).astype(np.int32)


def _bias_kernel(table_ref, bucket_ref, out_ref, *, scale):
    b = bucket_ref[...]
    for h in range(8):
        acc = jnp.zeros(b.shape, F32)
        for k in range(REL_BUCKETS):
            acc = jnp.where(b == k, table_ref[k, h] * scale, acc)
        out_ref[h] = acc


def _bias_lookup(rel_table, dist, scale, name):
    R, C = dist.shape
    return pl.pallas_call(
        functools.partial(_bias_kernel, scale=scale),
        grid=(C // 256,),
        in_specs=[pl.BlockSpec(memory_space=pltpu.SMEM),
                  pl.BlockSpec((R, 256), lambda m: (0, m))],
        out_specs=pl.BlockSpec((8, R, 256), lambda m: (0, 0, m)),
        out_shape=jax.ShapeDtypeStruct((8, R, C), F32),
        compiler_params=_cparams(("arbitrary",)),
        name=name,
    )(rel_table, jnp.asarray(_t5_bucket_np(dist)))


def _bias_tables(rel_table):
    swa = _bias_lookup(rel_table, np.arange(256)[:, None] - np.arange(256)[None, :], 1.0, "bias_swa")
    moba = _bias_lookup(rel_table, np.arange(BIAS_ROWS)[None, :] - BIAS_PAD - np.arange(MOBA_L)[:, None], LOG2E,
                        "bias_moba")
    return swa, moba


def _proj_even_kernel(h_ref, g_ref, win_ref, gq_ref, gkv_ref, wqa_ref, wqb_ref, wk_ref, wv_ref,
                      cq_ref, sq_ref, ck_ref, sk_ref,
                      q_ref, k_ref, v_ref, qs_ref, ks_ref, vs_ref, vsw_ref):
    u = _rms(h_ref[...], g_ref[...]).astype(BF16)
    nq = _rms(_dot(u, win_ref[:, 0:256]), gq_ref[...]).astype(BF16)
    qa = _dot(nq, wqa_ref[...])
    qb = _dot(nq, wqb_ref[...])
    cq, sq = cq_ref[...], sq_ref[...]
    for hh in range(MLA_HEADS):
        sl = slice(128 * hh, 128 * (hh + 1))
        q_ref[:, sl] = (qa[:, sl] * cq + qb[:, sl] * sq).astype(BF16)
    nkv = _rms(_dot(u, win_ref[:, 256:384]), gkv_ref[...]).astype(BF16)
    kn = _dot(nkv, wk_ref[...])
    v_ref[...] = _with_ones_rows(_dot_nt(wv_ref[...], nkv)).astype(BF16)
    kr = _dot(u, win_ref[:, 1792:1920]) * ck_ref[...] + _dot(u, win_ref[:, 1920:2048]) * sk_ref[...]
    for hh in range(MLA_HEADS):
        sl = slice(128 * hh, 128 * (hh + 1))
        k_ref[:, sl] = (kn[:, sl] + kr).astype(BF16)
    qs_ref[...] = _dot(u, win_ref[:, 384:1408]).astype(BF16)
    ks_ref[...] = _dot(u, win_ref[:, 1408:1536]).astype(BF16)
    vs_ref[...] = _dot(u, win_ref[:, 1536:1664]).astype(BF16)
    vsw_ref[...] = _dot(u, win_ref[:, 1664:1792]).astype(BF16)


def _cols(w, idx, scale=None):
    idx = np.asarray(idx)
    cuts = [0] + [n for n in range(1, len(idx)) if (idx[n] < 0) != (idx[n - 1] < 0)
                  or (idx[n] >= 0 and idx[n] != idx[n - 1] + 1)] + [len(idx)]
    parts = [jnp.zeros((w.shape[0], b - a), w.dtype) if idx[a] < 0 else w[:, idx[a]:idx[a] + b - a]
             for a, b in zip(cuts[:-1], cuts[1:])]
    out = jnp.concatenate(parts, axis=1)
    if scale is not None:
        out = out * scale
    return out


def _even_weights(w_in, w_qb, w_kvb):
    o_cq, o_ckv, o_kr = 0, MLA_QR, MLA_QR + MLA_KVR
    o_qs = o_kr + MLA_ROPE
    o_ks = o_qs + SWA_HEADS * HD
    o_vs = o_ks + SWA_KV * HD
    idx = list(range(o_cq, o_cq + MLA_QR)) + list(range(o_ckv, o_ckv + MLA_KVR))
    scale = [1.0] * len(idx)
    for hq in range(SWA_HEADS):
        c = hq // (SWA_HEADS // SWA_KV)
        blk = [-1] * 128
        blk[64 * c:64 * c + 64] = range(o_qs + 64 * hq, o_qs + 64 * hq + 64)
        idx += blk
        scale += [HD ** -0.5] * 128
    idx += list(range(o_ks, o_ks + 128))
    idx += list(range(o_vs, o_vs + 128))
    idx += list(range(o_vs + 64, o_vs + 128)) + list(range(o_vs, o_vs + 64))
    scale += [1.0] * 384
    half = MLA_ROPE // 2
    idx += [-1] * 64 + list(range(o_kr, o_kr + MLA_ROPE)) + [-1] * 32
    idx += [-1] * 64 + list(range(o_kr + half, o_kr + MLA_ROPE)) + list(range(o_kr, o_kr + half)) + [-1] * 32
    scale += [1.0] * 256
    win = _cols(w_in, idx, jnp.asarray(np.asarray(scale, np.float32))).astype(BF16)
    qw = MLA_NOPE + MLA_ROPE
    ia, ib, ik, iv = [], [], [], []
    for h in range(MLA_HEADS):
        ia += list(range(qw * h, qw * h + qw)) + [-1] * 32
        ib += [-1] * 64 + list(range(qw * h + MLA_NOPE + half, qw * h + qw)) \
            + list(range(qw * h + MLA_NOPE, qw * h + MLA_NOPE + half)) + [-1] * 32
        ik += list(range(128 * h, 128 * h + MLA_NOPE)) + [-1] * 64
        iv += list(range(128 * h + MLA_NOPE, 128 * h + 128)) + [-1] * 64
    return (win, _cols(w_qb, ia).astype(BF16), _cols(w_qb, ib).astype(BF16),
            _cols(w_kvb, ik).astype(BF16), _cols(w_kvb, iv).T.astype(BF16))


def _rope_tables(T):
    half = MLA_ROPE // 2
    freqs = ROPE_THETA ** (-jnp.arange(0, MLA_ROPE, 2, dtype=F32) / MLA_ROPE)
    ang = jnp.arange(T, dtype=F32)[:, None] * freqs[None, :]
    cos, sin = jnp.cos(ang), jnp.sin(ang)
    z = lambda n: jnp.zeros((T, n), F32)
    scale = (MLA_NOPE + MLA_ROPE) ** -0.5 * LOG2E
    ck = jnp.concatenate([z(64), cos, cos, z(32)], axis=1)
    sk = jnp.concatenate([z(64), -sin, sin, z(32)], axis=1)
    cq = jnp.concatenate([jnp.ones((T, 64), F32), cos, cos, z(32)], axis=1) * scale
    sq = sk * scale
    return cq, sq, ck, sk


def _proj_even(h, g, win, gq, gkv, wqa, wqb, wk, wv, tables, T):
    N = h.shape[0]
    nt = T // TM
    full = lambda a: pl.BlockSpec(a.shape, lambda i: (0,) * a.ndim)
    tab = pl.BlockSpec((TM, 128), lambda i: (i % nt, 0))
    row = lambda w: pl.BlockSpec((TM, w), lambda i: (i, 0))
    widths = (1024, 1024, None, 1024, 128, 128, 128)
    vt_spec = pl.BlockSpec((None, 1024, TM), lambda i: (i // nt, 0, i % nt))
    vt_shape = jax.ShapeDtypeStruct((N // T, 1024, T), BF16)
    return pl.pallas_call(
        _proj_even_kernel,
        grid=(N // TM,),
        in_specs=[row(D), full(g), full(win), full(gq), full(gkv), full(wqa), full(wqb), full(wk), full(wv),
                  tab, tab, tab, tab],
        out_specs=[vt_spec if w is None else row(w) for w in widths],
        out_shape=[vt_shape if w is None else jax.ShapeDtypeStruct((N, w), BF16) for w in widths],
        compiler_params=_cparams(("arbitrary",)),
        name="proj_even",
    )(h, g, win, gq, gkv, wqa, wqb, wk, wv, *tables)


def _finish_t(accs):
    ot = jnp.concatenate([acc[:HD] / acc[HD:HD + 1] for acc in accs], axis=0)
    return ot.T.astype(BF16)


def _causal_attention_t(i, q_of, k_ref, vt_ref, o_ref, s_ref, m_ref, acc_ref, *, tq, tk, bias_of=None):
    n_full = i * (tq // tk)
    key = lax.broadcasted_iota(jnp.int32, (tk, tq), 0)
    qry = i * tq + lax.broadcasted_iota(jnp.int32, (tk, tq), 1)

    def scores_to(slot, start, boundary):
        start = pl.multiple_of(start, tk)
        for j in range(2):
            st = _dot_nt(k_ref[pl.ds(start, tk), 128 * j:128 * (j + 1)], q_of(j))
            if bias_of is not None:
                st = st + bias_of(start, j)
            if boundary:
                st = jnp.where(start + key <= qry, st, NEG)
            s_ref[slot, j] = st

    def update_from(slot, start):
        start = pl.multiple_of(start, tk)
        for j in range(2):
            m_new, acc = _flash_update_t(s_ref[slot, j], vt_ref[128 * j:128 * (j + 1), pl.ds(start, tk)],
                                         m_ref[j], acc_ref[j])
            m_ref[j] = m_new
            acc_ref[j] = acc

    m_ref[...] = jnp.full(m_ref.shape, NEG, F32)
    acc_ref[...] = jnp.zeros(acc_ref.shape, F32)
    for d in range(tq // tk - 1):
        scores_to(0, (n_full + d) * tk, True)
        update_from(0, (n_full + d) * tk)
    first = (n_full + tq // tk - 1) * tk
    scores_to(0, first, True)

    def block_start(k):
        return jnp.where(k == 0, first, (k - 1) * tk)

    def pair(p, _):
        scores_to(1, block_start(2 * p + 1), False)
        update_from(0, block_start(2 * p))
        scores_to(0, block_start(2 * p + 2), False)
        update_from(1, block_start(2 * p + 1))
        return 0

    lax.fori_loop(0, n_full // 2, pair, 0)

    @pl.when(n_full % 2 == 1)
    def _():
        scores_to(1, block_start(n_full), False)
        update_from(0, block_start(n_full - 1))
        update_from(1, block_start(n_full))

    @pl.when(n_full % 2 == 0)
    def _():
        update_from(0, block_start(n_full))

    o_ref[...] = _finish_t([acc_ref[j] for j in range(2)])


def _attention_t_scratch(tq, tk):
    return [pltpu.VMEM((2, 2, tk, tq), F32), pltpu.VMEM((2, 1, tq), F32), pltpu.VMEM((2, 128, tq), F32)]


def _mla_kernel(q_ref, k_ref, vt_ref, o_ref, s_ref, m_ref, acc_ref, *, tq, tk):
    _causal_attention_t(pl.program_id(2), lambda j: q_ref[:, 128 * j:128 * (j + 1)], k_ref, vt_ref, o_ref,
                        s_ref, m_ref, acc_ref, tq=tq, tk=tk)


def _mla_attention(q, k, vt, B, T, tq=512, tk=512):
    nq = T // tq
    return pl.pallas_call(
        functools.partial(_mla_kernel, tq=tq, tk=tk),
        grid=(B, MLA_HEADS // 2, nq),
        in_specs=[pl.BlockSpec((tq, 256), lambda b, p, i: (b * nq + i, p)),
                  pl.BlockSpec((T, 256), lambda b, p, i: (b, p)),
                  pl.BlockSpec((None, 256, T), lambda b, p, i: (b, p, 0))],
        out_specs=pl.BlockSpec((tq, 128), lambda b, p, i: (b * nq + i, p)),
        out_shape=jax.ShapeDtypeStruct((B * T, MLA_HEADS * MLA_V), BF16),
        scratch_shapes=_attention_t_scratch(tq, tk),
        compiler_params=_cparams(("arbitrary", "arbitrary", "arbitrary")),
        name="mla_attention",
    )(q, k, vt)


def _swa_kernel(sink_ref, q_ref, k_ref, v_ref, vsw_ref, tb_ref, o_ref, *, tq):
    i = pl.program_id(1)
    G = SWA_HEADS // SWA_KV
    qi = lax.broadcasted_iota(jnp.int32, (128, 256), 0)
    kj = lax.broadcasted_iota(jnp.int32, (128, 256), 1)
    lane = lax.broadcasted_iota(jnp.int32, (128, 128), 1)
    for r in range(tq // 128):
        qstart = i * tq + 128 * r
        kstart = pl.multiple_of(jnp.maximum(qstart - 128, 0), 128)
        off = pl.multiple_of(qstart - kstart, 128)
        dist = off + qi - kj
        valid = (dist >= 0) & (dist < SWA_WIN)
        kw = k_ref[pl.ds(kstart, 256), :]
        vw = v_ref[pl.ds(kstart, 256), :]
        vsw = vsw_ref[pl.ds(kstart, 256), :]
        halves = [None] * SWA_HEADS
        for c in range(SWA_KV):
            qg = jnp.concatenate([q_ref[128 * r:128 * (r + 1), 128 * (G * c + g):128 * (G * c + g + 1)]
                                  for g in range(G)], axis=0)
            s_all = _dot_nt(qg, kw)
            for g in range(G):
                hq = G * c + g
                s = s_all[128 * g:128 * (g + 1)] + tb_ref[hq, pl.ds(off, 128), :]
                s = jnp.where(valid, s, NEG)
                sink = sink_ref[hq]
                m = jnp.maximum(jnp.max(s, axis=-1, keepdims=True), sink)
                p = jnp.exp(s - m)
                denom = jnp.sum(p, axis=-1, keepdims=True) + jnp.exp(sink - m)
                p = (p / denom).astype(BF16)
                halves[hq] = _dot(p, vw if c == hq % 2 else vsw)
        for mp in range(SWA_HEADS // 2):
            o_ref[128 * r:128 * (r + 1), 128 * mp:128 * (mp + 1)] = jnp.where(
                lane < HD, halves[2 * mp], halves[2 * mp + 1]).astype(BF16)


def _swa_attention(q8, k, v, vsw, sinks, tb, B, T, tq=512):
    nq = T // tq
    kv = pl.BlockSpec((T, 128), lambda b, i: (b, 0))
    return pl.pallas_call(
        functools.partial(_swa_kernel, tq=tq),
        grid=(B, nq),
        in_specs=[pl.BlockSpec(memory_space=pltpu.SMEM),
                  pl.BlockSpec((tq, 1024), lambda b, i: (b * nq + i, 0)),
                  kv, kv, kv,
                  pl.BlockSpec((8, 256, 256), lambda b, i: (0, 0, 0))],
        out_specs=pl.BlockSpec((tq, 512), lambda b, i: (b * nq + i, 0)),
        out_shape=jax.ShapeDtypeStruct((B * T, SWA_HEADS * HD), BF16),
        compiler_params=_cparams(("arbitrary", "arbitrary")),
        name="swa_attention",
    )(sinks, q8, k, v, vsw, tb)


def _outproj_router_kernel(h_ref, oa_ref, ob_ref, wo_ref, g_ref, wr_ref, br_ref,
                           hn_ref, u_ref, ri_ref, rg_ref, cnt_ref, base_ref):
    i = pl.program_id(0)

    @pl.when(i == 0)
    def _():
        base_ref[...] = jnp.zeros_like(base_ref)

    hn = h_ref[...] + _dot(oa_ref[...], wo_ref[0:512, :]) + _dot(ob_ref[...], wo_ref[512:1024, :])
    hn_ref[...] = hn
    u = _rms(hn, g_ref[...])
    u_ref[...] = _pack_halves(u)
    u_hi = u.astype(BF16)
    u_lo = (u - u_hi.astype(F32)).astype(BF16)
    both = _dot_nt(wr_ref[...], u_hi)
    logits = both[:128] + both[128:] + _dot_nt(wr_ref[0:128, :], u_lo) + br_ref[...]
    tm = logits.shape[1]
    sub = lax.broadcasted_iota(jnp.int32, (8, tm), 0).astype(F32)
    big = 1e6
    isg = sub < N_GROUPS
    gl = jnp.where(isg, logits[N_EXP:N_EXP + 8], NEG)
    gmax = jnp.max(gl, axis=0, keepdims=True)
    gsel = jnp.min(jnp.where(gl == gmax, sub, big), axis=0, keepdims=True)
    g_gate = 1.0 / jnp.sum(jnp.where(isg, jnp.exp(gl - gmax), 0.0), axis=0, keepdims=True)
    el = jnp.zeros((EPG, tm), F32)
    for grp in range(N_GROUPS):
        el = jnp.where(gsel == grp, logits[EPG * grp:EPG * (grp + 1)], el)
    m1 = jnp.max(el, axis=0, keepdims=True)
    i1 = jnp.min(jnp.where(el == m1, sub, big), axis=0, keepdims=True)
    el2 = jnp.where(sub == i1, NEG, el)
    m2 = jnp.max(el2, axis=0, keepdims=True)
    i2 = jnp.min(jnp.where(el2 == m2, sub, big), axis=0, keepdims=True)
    e1 = gsel * EPG + i1
    e2 = gsel * EPG + i2
    r = jnp.exp(m2 - m1)
    gate1 = g_gate / (1.0 + r)
    gate2 = g_gate * r / (1.0 + r)
    rows = lax.broadcasted_iota(jnp.int32, (128, tm), 0).astype(F32)
    oh1 = (rows == e1).astype(F32)
    oh2 = (rows == e2).astype(F32)
    oh = oh1 + oh2
    t_row = lax.broadcasted_iota(jnp.int32, (tm, tm), 0)
    t_col = lax.broadcasted_iota(jnp.int32, (tm, tm), 1)
    earlier = jnp.where(t_row < t_col, 1.0, 0.0).astype(BF16)
    prefix = _dot(oh.astype(BF16), earlier) + base_ref[:, 0:1]
    rank1 = jnp.sum(oh1 * prefix, axis=0, keepdims=True)
    rank2 = jnp.sum(oh2 * prefix, axis=0, keepdims=True)
    base_ref[...] = base_ref[...] + jnp.sum(oh, axis=1, keepdims=True)
    cnt_ref[...] = base_ref[...]
    ri_ref[...] = jnp.where(sub == 0, e1, jnp.where(sub == 1, e2, jnp.where(sub == 2, rank1,
                            jnp.where(sub == 3, rank2, 0.0))))
    rg_ref[...] = jnp.where(rows == 0, gate1, jnp.where(rows == 1, gate2, 0.0)).T


def _outproj_router(h, oa, ob, wo, g, wr, br):
    N = h.shape[0]
    full = lambda a: pl.BlockSpec(a.shape, lambda i: (0,) * a.ndim)
    row = lambda w: pl.BlockSpec((TM, w), lambda i: (i, 0))
    return pl.pallas_call(
        _outproj_router_kernel,
        grid=(N // TM,),
        in_specs=[row(D), row(512), row(512), full(wo), full(g), full(wr), full(br)],
        out_specs=[row(D), row(HALF), pl.BlockSpec((8, TM), lambda i: (0, i)), row(128),
                   pl.BlockSpec((128, 128), lambda i: (0, 0))],
        out_shape=[jax.ShapeDtypeStruct((N, D), F32), jax.ShapeDtypeStruct((N, HALF), jnp.int32),
                   jax.ShapeDtypeStruct((8, N), F32), jax.ShapeDtypeStruct((N, 128), F32),
                   jax.ShapeDtypeStruct((128, 128), F32)],
        scratch_shapes=[pltpu.VMEM((128, 128), F32)],
        compiler_params=_cparams(("arbitrary",)),
        name="outproj_router",
    )(h, oa, ob, wo, g, wr, br)


def _moe_kernel(te_ref, tv_ref, x_ref, wg_ref, wu_ref, wd_ref, o_ref, wgu_s, wd_s):
    i = pl.program_id(0)
    valid = tv_ref[i]

    @pl.when((i == 0) | (te_ref[i] != te_ref[jnp.maximum(i - 1, 0)]))
    def _():
        wgu_s[:, :D_EXP] = wg_ref[...].astype(BF16)
        wgu_s[:, D_EXP:] = wu_ref[...].astype(BF16)
        wd_s[...] = wd_ref[...].astype(BF16)

    @pl.when(valid > 0)
    def _():
        rows = lax.broadcasted_iota(jnp.int32, x_ref.shape, 0)
        lo, hi = _unpack_halves(jnp.where(rows < valid, x_ref[...], 0))
        hgu = _dot(lo.astype(BF16), wgu_s[:HALF, :]) + _dot(hi.astype(BF16), wgu_s[HALF:, :])
        hg, hu = hgu[:, :D_EXP], hgu[:, D_EXP:]
        a = hg * (1.0 / (1.0 + jnp.exp(-hg))) * hu
        o_ref[...] = _pack_halves(_dot(a.astype(BF16), wd_s[...]))

    @pl.when(valid <= 0)
    def _():
        o_ref[...] = jnp.zeros_like(o_ref)


def _moe_experts(xs, tile_expert, tile_valid, w_gate, w_up, w_down, layer):
    P = xs.shape[0]
    grid_spec = pltpu.PrefetchScalarGridSpec(
        num_scalar_prefetch=2,
        grid=(P // TM_E,),
        in_specs=[pl.BlockSpec((TM_E, HALF), lambda i, te, tv: (i, 0)),
                  pl.BlockSpec((None, None, D, D_EXP), lambda i, te, tv: (layer, te[i], 0, 0)),
                  pl.BlockSpec((None, None, D, D_EXP), lambda i, te, tv: (layer, te[i], 0, 0)),
                  pl.BlockSpec((None, None, D_EXP, D), lambda i, te, tv: (layer, te[i], 0, 0))],
        out_specs=pl.BlockSpec((TM_E, HALF), lambda i, te, tv: (i, 0)),
        scratch_shapes=[pltpu.VMEM((D, 2 * D_EXP), BF16), pltpu.VMEM((D_EXP, D), BF16)],
    )
    return pl.pallas_call(
        _moe_kernel,
        grid_spec=grid_spec,
        out_shape=jax.ShapeDtypeStruct((P, HALF), jnp.int32),
        compiler_params=_cparams(("arbitrary",)),
        name="moe_experts",
    )(tile_expert, tile_valid, xs, w_gate, w_up, w_down)


def _sc_mesh():
    return plsc.VectorSubcoreMesh(core_axis_name="c", subcore_axis_name="s",
                                  num_cores=SC_CORES, num_subcores=SC_SUBCORES)


def _sc_scatter_rows(src, d1, d2, P):
    N, W = src.shape
    per_w = N // (SC_CORES * SC_SUBCORES)

    @functools.partial(pl.kernel, mesh=_sc_mesh(), out_type=jax.ShapeDtypeStruct((P, W), src.dtype),
                       scratch_types=[pltpu.VMEM((SC_CHUNK,), jnp.int32), pltpu.VMEM((SC_CHUNK, W), src.dtype)],
                       name="sc_scatter_rows")
    def k(src_hbm, d1_hbm, d2_hbm, out_hbm, idx_v, rows_v):
        wid = lax.axis_index("s") * SC_CORES + lax.axis_index("c")

        @pl.loop(0, per_w // SC_CHUNK)
        def _(c):
            off = wid * per_w + c * SC_CHUNK
            pltpu.sync_copy(src_hbm.at[pl.ds(off, SC_CHUNK)], rows_v)
            pltpu.sync_copy(d1_hbm.at[pl.ds(off, SC_CHUNK)], idx_v)
            pltpu.sync_copy(rows_v, out_hbm.at[idx_v])
            pltpu.sync_copy(d2_hbm.at[pl.ds(off, SC_CHUNK)], idx_v)
            pltpu.sync_copy(rows_v, out_hbm.at[idx_v])

    return k(src, d1, d2)


def _sc_gather_rows(table, idx):
    B, W = idx.shape[0], table.shape[1]
    per_w = B // (SC_CORES * SC_SUBCORES)

    @functools.partial(pl.kernel, mesh=_sc_mesh(), out_type=jax.ShapeDtypeStruct((B, W), table.dtype),
                       scratch_types=[pltpu.VMEM((SC_CHUNK,), jnp.int32), pltpu.VMEM((SC_CHUNK, W), table.dtype)],
                       name="sc_gather_rows")
    def k(table_hbm, idx_hbm, out_hbm, idx_v, rows_v):
        wid = lax.axis_index("s") * SC_CORES + lax.axis_index("c")

        @pl.loop(0, per_w // SC_CHUNK)
        def _(c):
            off = wid * per_w + c * SC_CHUNK
            pltpu.sync_copy(idx_hbm.at[pl.ds(off, SC_CHUNK)], idx_v)
            pltpu.sync_copy(table_hbm.at[idx_v], rows_v)
            pltpu.sync_copy(rows_v, out_hbm.at[pl.ds(off, SC_CHUNK)])

    return k(table, idx)


def _dest_kernel(offs_ref, ri_ref, d_ref):
    ri = ri_ref[...]
    base = jnp.zeros(ri.shape, F32)
    for e in range(N_EXP):
        base = jnp.where(ri == e, offs_ref[e].astype(F32), base)
    d_ref[...] = (base + pltpu.roll(ri, 6, 0)).astype(jnp.int32)


def _dispatch_plan(ri, cnt, N):
    counts = cnt[:N_EXP, 0].astype(jnp.int32)
    padded = ((counts + TM_E - 1) // TM_E) * TM_E
    ends = jnp.cumsum(padded)
    offs = ends - padded
    take = lambda t, i: t.at[i].get(mode="promise_in_bounds")
    dest = pl.pallas_call(
        _dest_kernel,
        in_specs=[pl.BlockSpec(memory_space=pltpu.SMEM), pl.BlockSpec((8, N), lambda: (0, 0))],
        out_specs=pl.BlockSpec((8, N), lambda: (0, 0)),
        out_shape=jax.ShapeDtypeStruct((8, N), jnp.int32),
        name="dispatch_rows",
    )(offs, ri)
    d1, d2 = dest[0], dest[1]
    P = 2 * N + N_EXP * TM_E
    tile_start = jnp.arange(P // TM_E, dtype=jnp.int32) * TM_E
    tile_expert = jnp.minimum(jnp.sum((ends[None, :] <= tile_start[:, None]).astype(jnp.int32), axis=1), N_EXP - 1)
    tile_valid = jnp.clip(take(offs + counts, tile_expert) - tile_start, 0, TM_E)
    return d1, d2, P, tile_expert, tile_valid


def _combine(h_ref, y1_ref, y2_ref, rg_ref):
    rg = rg_ref[...]
    lo1, hi1 = _unpack_halves(y1_ref[...])
    lo2, hi2 = _unpack_halves(y2_ref[...])
    g1, g2 = rg[:, 0:1], rg[:, 1:2]
    return h_ref[...] + jnp.concatenate([g1 * lo1 + g2 * lo2, g1 * hi1 + g2 * hi2], axis=1)


ODD_WIDTHS = (1024, 1024, 512, 512)


def _combine_proj_odd_kernel(h_ref, y1_ref, y2_ref, rg_ref, g_ref, win_ref, wvt_ref, ind_ref, hn_ref, vt_ref,
                             vdt_ref, *out_refs):
    hn = _combine(h_ref, y1_ref, y2_ref, rg_ref)
    hn_ref[...] = hn
    u = _rms(hn, g_ref[...]).astype(BF16)
    vt_ref[...] = _with_ones_rows(_dot_nt(wvt_ref[0:1024, :], u)).astype(BF16)
    vdt_ref[...] = _dot_nt(wvt_ref[1024:1536, :], u).astype(BF16)
    c0 = 0
    for n, o_ref in enumerate(out_refs):
        w = ODD_WIDTHS[n]
        if n == 1:
            for hh in range(8):
                o_ref[:, 128 * hh:128 * (hh + 1)] = (
                    _dot(u, win_ref[:, c0 + 128 * hh:c0 + 128 * (hh + 1)]) + ind_ref[...]).astype(BF16)
        else:
            for c in range(0, w, 512):
                o_ref[:, c:c + 512] = _dot(u, win_ref[:, c0 + c:c0 + c + 512]).astype(BF16)
        c0 += w


def _odd_weights(w_in):
    idx, scale = [], []
    for base, s in ((0, HD ** -0.5 * LOG2E), (512, 1.0)):
        for hh in range(8):
            idx += list(range(base + 64 * hh, base + 64 * hh + 64)) + [-1] * 64
            scale += [s] * 128
    idx += list(range(1536, 2560))
    scale += [HD ** -0.5 * LOG2E] * 512 + [1.0] * 512
    win = _cols(w_in, idx, jnp.asarray(np.asarray(scale, np.float32))).astype(BF16)
    iv = []
    for hh in range(8):
        iv += list(range(1024 + 64 * hh, 1024 + 64 * hh + 64)) + [-1] * 64
    iv += list(range(2560, 3072))
    return win, _cols(w_in, iv).T.astype(BF16)


def _block_indicator(T):
    assert T // MOBA_L <= 64
    ind = np.zeros((T, 128), np.float32)
    ind[np.arange(T), 64 + np.arange(T) // MOBA_L] = 1.0
    return jnp.asarray(ind)


def _y_specs(N):
    return [pl.BlockSpec((TM, HALF), lambda i: (i, 0)), pl.BlockSpec((TM, HALF), lambda i: (i + N // TM, 0))]


def _combine_proj_odd(h, y12, rg, g, win, wvt, ind, T):
    N = h.shape[0]
    nt = T // TM
    full = lambda a: pl.BlockSpec(a.shape, lambda i: (0,) * a.ndim)
    row = lambda w: pl.BlockSpec((TM, w), lambda i: (i, 0))
    return pl.pallas_call(
        _combine_proj_odd_kernel,
        grid=(N // TM,),
        in_specs=[row(D)] + _y_specs(N) + [row(128), full(g), full(win), full(wvt),
                  pl.BlockSpec((TM, 128), lambda i: (i % nt, 0))],
        out_specs=[row(D), pl.BlockSpec((None, 1024, TM), lambda i: (i // nt, 0, i % nt)),
                   pl.BlockSpec((None, 512, TM), lambda i: (i // nt, 0, i % nt))]
        + [row(w) for w in ODD_WIDTHS],
        out_shape=[jax.ShapeDtypeStruct((N, D), F32), jax.ShapeDtypeStruct((N // T, 1024, T), BF16),
                   jax.ShapeDtypeStruct((N // T, 512, T), BF16)]
        + [jax.ShapeDtypeStruct((N, w), BF16) for w in ODD_WIDTHS],
        compiler_params=_cparams(("arbitrary",)),
        name="combine_proj_odd",
    )(h, y12, y12, rg, g, win, wvt, ind)


def _combine_final_kernel(h_ref, y1_ref, y2_ref, rg_ref, g_ref, o_ref):
    o_ref[...] = _rms(_combine(h_ref, y1_ref, y2_ref, rg_ref), g_ref[...])


def _combine_final(h, y12, rg, g):
    N = h.shape[0]
    row = lambda w: pl.BlockSpec((TM, w), lambda i: (i, 0))
    return pl.pallas_call(
        _combine_final_kernel,
        grid=(N // TM,),
        in_specs=[row(D)] + _y_specs(N) + [row(128), pl.BlockSpec((1, D), lambda i: (0, 0))],
        out_specs=row(D),
        out_shape=jax.ShapeDtypeStruct((N, D), F32),
        compiler_params=_cparams(("arbitrary",)),
        name="combine_final",
    )(h, y12, y12, rg, g)


def _moba_kernel(q_ref, k_ref, vt_ref, tb_ref, o_ref, kmean_ref, qa_ref, s_ref, m_ref, acc_ref, *, nkb):
    i = pl.program_id(2)
    L, tq = MOBA_L, MOBA_TQ

    @pl.when(i == 0)
    def _():
        kmean_ref[...] = jnp.zeros_like(kmean_ref)
        for n in range(nkb):
            kmean_ref[64 + n:65 + n, :] = jnp.sum(k_ref[n * L:(n + 1) * L, :].astype(F32), axis=0,
                                                  keepdims=True) * (1.0 / L)

    lane = lax.broadcasted_iota(jnp.int32, (tq, 128), 1)
    rowi = lax.broadcasted_iota(jnp.int32, (tq, 128), 0)
    own = 2 * i + jnp.where(rowi >= L, 1, 0)
    blk = lane - 64
    lane_f = lane.astype(F32)
    for j in range(2):
        qj = q_ref[:, 128 * j:128 * (j + 1)]
        gsc = lax.dot_general(qj.astype(F32), kmean_ref[:, 128 * j:128 * (j + 1)], (((1,), (1,)), ((), ())),
                              preferred_element_type=F32, precision=lax.Precision.HIGHEST)
        g = jnp.where((blk >= 0) & (blk < own), gsc, NEG)
        allowed = blk == own
        for _ in range(MOBA_TOPK):
            mx = jnp.max(g, axis=-1, keepdims=True)
            idx = jnp.min(jnp.where(g == mx, lane_f, 1e6), axis=-1, keepdims=True)
            pick = lane_f == idx
            allowed = allowed | (pick & (mx > 0.5 * NEG))
            g = jnp.where(pick, 2.0 * NEG, g)
        qa_ref[j] = jnp.where(lane < HD, qj, jnp.where(allowed, 0.0, NEG).astype(BF16))

    def bias_of(start, j):
        parts = []
        for c in range(tq // L):
            off = jnp.minimum(i * tq - start - c * L, BIAS_CONST_FROM) + BIAS_PAD
            parts.append(tb_ref[j, :, pl.ds(pl.multiple_of(off, L), tq)])
        return jnp.concatenate(parts, axis=0)

    _causal_attention_t(i, lambda j: qa_ref[j], k_ref, vt_ref, o_ref, s_ref, m_ref, acc_ref, tq=tq, tk=tq,
                        bias_of=bias_of)


def _moba_attention(q, k, vt, tb, B, T):
    tq = MOBA_TQ
    nq = T // tq
    return pl.pallas_call(
        functools.partial(_moba_kernel, nkb=T // MOBA_L),
        grid=(B, 4, nq),
        in_specs=[pl.BlockSpec((tq, 256), lambda b, p, i: (b * nq + i, p)),
                  pl.BlockSpec((T, 256), lambda b, p, i: (b, p)),
                  pl.BlockSpec((None, 256, T), lambda b, p, i: (b, p, 0)),
                  pl.BlockSpec((2, MOBA_L, BIAS_ROWS), lambda b, p, i: (p, 0, 0))],
        out_specs=pl.BlockSpec((tq, 128), lambda b, p, i: (b * nq + i, p)),
        out_shape=jax.ShapeDtypeStruct((B * T, 512), BF16),
        scratch_shapes=[pltpu.VMEM((128, 256), F32), pltpu.VMEM((2, tq, 128), BF16)]
        + _attention_t_scratch(tq, tq),
        compiler_params=_cparams(("arbitrary", "arbitrary", "arbitrary")),
        name="moba_attention",
    )(q, k, vt, tb)


def _sb_kernel(q_ref, k_ref, vt_ref, o_ref, *, tq, pairs):
    i = pl.program_id(2)
    lane = lax.broadcasted_iota(jnp.int32, (tq, 128), 1)
    qs = []
    for p in range(pairs):
        q_pair = q_ref[:, 128 * p:128 * (p + 1)]
        qs += [jnp.where(lane < HD, q_pair, 0).astype(BF16), jnp.where(lane >= HD, q_pair, 0).astype(BF16)]
    key = lax.broadcasted_iota(jnp.int32, (tq, tq), 0)
    qry = lax.broadcasted_iota(jnp.int32, (tq, tq), 1)
    past = key < qry
    suffix = jnp.where(qry >= key, -1.0, 0.0).astype(BF16)
    heads = range(2 * pairs)

    def step(kb, carry, boundary):
        start = pl.multiple_of(kb * tq, tq)
        zs = [_dot_nt(k_ref[pl.ds(start, tq), 128 * (h // 2):128 * (h // 2 + 1)], qs[h]) for h in heads]
        parts = []
        for z in zs:
            soft = jnp.maximum(z, 0.0) + jnp.log2(1.0 + jnp.exp2(-jnp.abs(z)))
            if boundary:
                soft = jnp.where(past, soft, 0.0)
            hi = soft.astype(BF16)
            parts.append((hi, (soft - hi.astype(F32)).astype(BF16)))
        sums = [_dot(suffix, hi) + _dot(suffix, lo) for hi, lo in parts]
        ws = []
        for h in heads:
            w = jnp.exp2(zs[h] + (carry[h][0] + sums[h]))
            ws.append((jnp.where(past, w, 0.0) if boundary else w).astype(BF16))
        return tuple((carry[h][0] + sums[h][0:1], carry[h][1]
                      + _dot(vt_ref[128 * (h // 2):128 * (h // 2 + 1), pl.ds(start, tq)], ws[h])) for h in heads)

    def live(carry):
        top = carry[0][0]
        for c, _ in carry[1:]:
            top = jnp.maximum(top, c)
        return (jnp.max(top) > SB_DONE).astype(jnp.int32)

    def body(state):
        t, _, carry = state
        carry = step(i - 1 - t, carry, False)
        return t + 1, live(carry), carry

    init = tuple((jnp.zeros((1, tq), F32), jnp.zeros((128, tq), F32)) for _ in range(2 * pairs))
    carry = step(i, init, True)
    _, _, carry = lax.while_loop(lambda st: (st[0] < i) & (st[1] > 0), body, (jnp.int32(0), live(carry), carry))
    rows = lax.broadcasted_iota(jnp.int32, (128, tq), 0)
    for p in range(pairs):
        ot = jnp.where(rows < HD, carry[2 * p][1], carry[2 * p + 1][1])
        o_ref[:, 128 * p:128 * (p + 1)] = ot.T.astype(BF16)


def _sb_attention(q, k, vt, B, T, tq=256, pairs=4):
    nq = T // tq
    w = 128 * pairs
    return pl.pallas_call(
        functools.partial(_sb_kernel, tq=tq, pairs=pairs),
        grid=(B, 4 // pairs, nq),
        in_specs=[pl.BlockSpec((tq, w), lambda b, p, i: (b * nq + i, p)),
                  pl.BlockSpec((T, w), lambda b, p, i: (b, p)),
                  pl.BlockSpec((None, w, T), lambda b, p, i: (b, p, 0))],
        out_specs=pl.BlockSpec((tq, w), lambda b, p, i: (b * nq + i, p)),
        out_shape=jax.ShapeDtypeStruct((B * T, 512), BF16),
        compiler_params=_cparams(("arbitrary", "arbitrary", "arbitrary")),
        name="sb_attention",
    )(q, k, vt)


def _router_weights(w_group, b_group, w_router, b_router):
    pad = 128 - N_EXP - N_GROUPS
    wr = jnp.concatenate([w_router, w_group, jnp.zeros((D, pad), F32)], axis=1).T
    hi = wr.astype(BF16)
    lo = (wr - hi.astype(F32)).astype(BF16)
    br = jnp.concatenate([b_router, b_group, jnp.zeros((pad,), F32)])[:, None]
    return jnp.concatenate([hi, lo], axis=0), br


def _ffn(h, oa, ob, wo, g, w_group, b_group, w_router, b_router, w_gate, w_up, w_down, layer):
    wr, br = _router_weights(w_group, b_group, w_router, b_router)
    hn, u, ri, rg, cnt = _outproj_router(h, oa, ob, wo.astype(BF16), g[None, :], wr, br)
    d1, d2, P, tile_expert, tile_valid = _dispatch_plan(ri, cnt, h.shape[0])
    xs = _sc_scatter_rows(u, d1, d2, P)
    ys = _moe_experts(xs, tile_expert, tile_valid, w_gate, w_up, w_down, layer)
    return hn, _sc_gather_rows(ys, jnp.concatenate([d1, d2])), rg


def kernel(x, norm_mix, norm_ffn, norm_final, rel_bias_table, w_in_even, g_mla_q, g_mla_kv, w_mla_qb, w_mla_kvb, swa_sinks, w_out_even, w_in_odd, w_out_odd, moe_w_group, moe_b_group, moe_w_router, moe_b_router, moe_w_gate, moe_w_up, moe_w_down):
    B, T, _ = x.shape
    h = x.reshape(B * T, D)
    tb_swa, tb_moba = _bias_tables(rel_bias_table)

    win, wqa, wqb, wk, wv = _even_weights(w_in_even[0], w_mla_qb[0], w_mla_kvb[0])
    q, k, v, qs8, ks, vs, vsw = _proj_even(h, norm_mix[0][None, :], win, g_mla_q[0][None, :], g_mla_kv[0][None, :],
                                           wqa, wqb, wk, wv, _rope_tables(T), T)
    oa = _mla_attention(q, k, v, B, T)
    ob = _swa_attention(qs8, ks, vs, vsw, swa_sinks[0], tb_swa, B, T)
    h, y12, rg = _ffn(h, oa, ob, w_out_even[0], norm_ffn[0], moe_w_group[0], moe_b_group[0], moe_w_router[0],
                      moe_b_router[0], moe_w_gate, moe_w_up, moe_w_down, 0)

    h, vct, vdt, qc, kc, qd, kd = _combine_proj_odd(h, y12, rg, norm_mix[1][None, :], *_odd_weights(w_in_odd[0]),
                                                   _block_indicator(T), T)
    oc = _moba_attention(qc, kc, vct, tb_moba, B, T)
    od = _sb_attention(qd, kd, vdt, B, T)
    h, y12, rg = _ffn(h, oc, od, w_out_odd[0], norm_ffn[1], moe_w_group[1], moe_b_group[1], moe_w_router[1],
                      moe_b_router[1], moe_w_gate, moe_w_up, moe_w_down, 1)
    out = _combine_final(h, y12, rg, norm_final[None, :])
    return out.reshape(B, T, D)
```

```python
import functools
import math

import numpy as np
import jax
import jax.numpy as jnp
from jax import lax
from jax.experimental import pallas as pl
from jax.experimental.pallas import tpu as pltpu
from jax.experimental.pallas import tpu_sc as plsc

F32 = jnp.float32
BF16 = jnp.bfloat16
NEG = -1e30
EPS = 1e-6

D = 1024
HD = 64
LANES = 128
MLA_HEADS, MLA_QR, MLA_KVR, MLA_NOPE, MLA_ROPE, MLA_V = 8, 256, 128, 64, 32, 64
ROPE_THETA = 10000.0
SWA_HEADS, SWA_KV, SWA_WIN = 8, 2, 128
MOBA_L, MOBA_TOPK = 256, 3
REL_BUCKETS, REL_MAX = 32, 2048
N_GROUPS, EPG, N_EXP, D_EXP = 4, 8, 32, 256
BIAS_CONST_FROM = 1792
MOBA_TQ = 2 * MOBA_L
BIAS_PAD = MOBA_L
BIAS_ROWS = BIAS_PAD + BIAS_CONST_FROM + MOBA_TQ
SB_DONE = -160.0
TM = 512
TM_E = 512
VMEM_LIMIT = 56 * 1024 * 1024
SC_CORES, SC_SUBCORES = 2, 16
SC_CHUNK = 128
HALF = D // 2
LOG2E = math.log2(math.e)


def _pack_halves(x):
    lo = pltpu.bitcast(x[:, :HALF].astype(BF16).astype(F32), jnp.uint32) >> 16
    hi = pltpu.bitcast(x[:, HALF:].astype(BF16).astype(F32), jnp.uint32) & jnp.uint32(0xFFFF0000)
    return pltpu.bitcast(lo | hi, jnp.int32)


def _unpack_halves(w):
    w = pltpu.bitcast(w, jnp.uint32)
    return pltpu.bitcast(w << 16, F32), pltpu.bitcast(w & jnp.uint32(0xFFFF0000), F32)


def _cparams(sem):
    return pltpu.CompilerParams(dimension_semantics=sem, vmem_limit_bytes=VMEM_LIMIT)


def _dot(a, b):
    return jnp.dot(a, b, preferred_element_type=F32)


def _dot_nt(a, b):
    return lax.dot_general(a, b, (((1,), (1,)), ((), ())), preferred_element_type=F32)


def _rms(x, g):
    return x * lax.rsqrt(jnp.mean(x * x, axis=-1, keepdims=True) + EPS) * g


def _values_t(vt):
    ones = jnp.ones((HD, vt.shape[1]), vt.dtype)
    return jnp.concatenate([blk for h in range(8) for blk in (vt[HD * h:HD * (h + 1)], ones)], axis=0)


def _spread_heads(x, lane_of):
    lane = lax.broadcasted_iota(jnp.int32, (x.shape[0], 128), 1)
    out = []
    for h in range(8):
        pair = x[:, 128 * (h // 2):128 * (h // 2 + 1)]
        if HD * (h % 2) != lane_of(h):
            pair = pltpu.roll(pair, HD, 1)
        out.append(jnp.where((lane >= lane_of(h)) & (lane < lane_of(h) + HD), pair, 0.0))
    return out


def _flash_update_t(st, vt, m, acc):
    m_new = jnp.maximum(m, jnp.max(st, axis=0, keepdims=True))
    alpha = jnp.exp2(m - m_new)
    pt = jnp.exp2(st - m_new).astype(BF16)
    return m_new, alpha * acc + _dot(vt, pt)


def _flash_update(s, v, m, l, acc):
    m_new = jnp.maximum(m, jnp.max(s, axis=-1, keepdims=True))
    alpha = jnp.exp(m - m_new)
    p = jnp.exp(s - m_new)
    l = alpha * l + jnp.sum(p, axis=-1, keepdims=True)
    acc = alpha * acc + _dot(p.astype(BF16), v)
    return m_new, l, acc


def _t5_bucket_np(dist):
    n = np.maximum(dist, 0)
    max_exact = REL_BUCKETS // 2
    nf = np.maximum(n, 1).astype(np.float32)
    large = max_exact + (np.log(nf / np.float32(max_exact)) / np.float32(math.log(REL_MAX / max_exact))
                         * np.float32(REL_BUCKETS - max_exact)).astype(np.int32)
    large = np.minimum(large, REL_BUCKETS - 1)
    return np.where(n < max_exact, n, large).astype(np.int32)


def _bias_kernel(table_ref, bucket_ref, out_ref, *, scale):
    b = bucket_ref[...]
    for h in range(8):
        acc = jnp.zeros(b.shape, F32)
        for k in range(REL_BUCKETS):
            acc = jnp.where(b == k, table_ref[k, h] * scale, acc)
        out_ref[h] = acc


def _bias_lookup(rel_table, dist, scale, name):
    R, C = dist.shape
    return pl.pallas_call(
        functools.partial(_bias_kernel, scale=scale),
        grid=(C // 256,),
        in_specs=[pl.BlockSpec(memory_space=pltpu.SMEM),
                  pl.BlockSpec((R, 256), lambda m: (0, m))],
        out_specs=pl.BlockSpec((8, R, 256), lambda m: (0, 0, m)),
        out_shape=jax.ShapeDtypeStruct((8, R, C), F32),
        compiler_params=_cparams(("arbitrary",)),
        name=name,
    )(rel_table, jnp.asarray(_t5_bucket_np(dist)))


def _bias_tables(rel_table):
    swa = _bias_lookup(rel_table, np.arange(256)[:, None] - np.arange(256)[None, :], 1.0, "bias_swa")
    moba = _bias_lookup(rel_table, np.arange(BIAS_ROWS)[None, :] - BIAS_PAD - np.arange(MOBA_L)[:, None], LOG2E,
                        "bias_moba")
    return swa, moba


def _proj_even_kernel(h_ref, g_ref, win_ref, gq_ref, gkv_ref, wqa_ref, wqb_ref, wk_ref, wv_ref,
                      cq_ref, sq_ref, ck_ref, sk_ref,
                      q_ref, k_ref, v_ref, qs_ref, ks_ref, vs_ref, vsw_ref):
    u = _rms(h_ref[...], g_ref[...]).astype(BF16)
    nq = _rms(_dot(u, win_ref[:, 0:256]), gq_ref[...]).astype(BF16)
    qa = _dot(nq, wqa_ref[...])
    qb = _dot(nq, wqb_ref[...])
    cq, sq = cq_ref[...], sq_ref[...]
    for hh in range(MLA_HEADS):
        sl = slice(128 * hh, 128 * (hh + 1))
        q_ref[:, sl] = (qa[:, sl] * cq + qb[:, sl] * sq).astype(BF16)
    nkv = _rms(_dot(u, win_ref[:, 256:384]), gkv_ref[...]).astype(BF16)
    kn = _dot(nkv, wk_ref[...])
    v_ref[...] = _values_t(_dot_nt(wv_ref[...], nkv)).astype(BF16)
    kr = _dot(u, win_ref[:, 1280:1408]) * ck_ref[...] + _dot(u, win_ref[:, 1408:1536]) * sk_ref[...]
    for hh in range(MLA_HEADS):
        sl = slice(128 * hh, 128 * (hh + 1))
        k_ref[:, sl] = (kn[:, sl] + kr).astype(BF16)
    qs = _spread_heads(_dot(u, win_ref[:, 384:896]), lambda hq: HD * (hq // (SWA_HEADS // SWA_KV)))
    for hq in range(SWA_HEADS):
        qs_ref[:, 128 * hq:128 * (hq + 1)] = qs[hq].astype(BF16)
    ks_ref[...] = _dot(u, win_ref[:, 896:1024]).astype(BF16)
    vs_ref[...] = _dot(u, win_ref[:, 1024:1152]).astype(BF16)
    vsw_ref[...] = _dot(u, win_ref[:, 1152:1280]).astype(BF16)


def _cols(w, idx, scale=None):
    idx = np.asarray(idx)
    cuts = [0] + [n for n in range(1, len(idx)) if (idx[n] < 0) != (idx[n - 1] < 0)
                  or (idx[n] >= 0 and idx[n] != idx[n - 1] + 1)] + [len(idx)]
    parts = [jnp.zeros((w.shape[0], b - a), w.dtype) if idx[a] < 0 else w[:, idx[a]:idx[a] + b - a]
             for a, b in zip(cuts[:-1], cuts[1:])]
    out = jnp.concatenate(parts, axis=1)
    if scale is not None:
        out = out * scale
    return out


def _even_weights(w_in, w_qb, w_kvb):
    o_cq, o_ckv, o_kr = 0, MLA_QR, MLA_QR + MLA_KVR
    o_qs = o_kr + MLA_ROPE
    o_ks = o_qs + SWA_HEADS * HD
    o_vs = o_ks + SWA_KV * HD
    idx = list(range(o_cq, o_cq + MLA_QR)) + list(range(o_ckv, o_ckv + MLA_KVR))
    scale = [1.0] * len(idx)
    idx += list(range(o_qs, o_qs + SWA_HEADS * HD))
    scale += [HD ** -0.5] * (SWA_HEADS * HD)
    idx += list(range(o_ks, o_ks + 128))
    idx += list(range(o_vs, o_vs + 128))
    idx += list(range(o_vs + 64, o_vs + 128)) + list(range(o_vs, o_vs + 64))
    scale += [1.0] * 384
    half = MLA_ROPE // 2
    idx += [-1] * 64 + list(range(o_kr, o_kr + MLA_ROPE)) + [-1] * 32
    idx += [-1] * 64 + list(range(o_kr + half, o_kr + MLA_ROPE)) + list(range(o_kr, o_kr + half)) + [-1] * 32
    scale += [1.0] * 256
    win = _cols(w_in, idx, jnp.asarray(np.asarray(scale, np.float32))).astype(BF16)
    qw = MLA_NOPE + MLA_ROPE
    ia, ib, ik, iv = [], [], [], []
    for h in range(MLA_HEADS):
        ia += list(range(qw * h, qw * h + qw)) + [-1] * 32
        ib += [-1] * 64 + list(range(qw * h + MLA_NOPE + half, qw * h + qw)) \
            + list(range(qw * h + MLA_NOPE, qw * h + MLA_NOPE + half)) + [-1] * 32
        ik += list(range(128 * h, 128 * h + MLA_NOPE)) + [-1] * 64
        iv += list(range(128 * h + MLA_NOPE, 128 * h + 128))
    return (win, _cols(w_qb, ia).astype(BF16), _cols(w_qb, ib).astype(BF16),
            _cols(w_kvb, ik).astype(BF16), _cols(w_kvb, iv).T.astype(BF16))


def _rope_tables(T):
    half = MLA_ROPE // 2
    freqs = ROPE_THETA ** (-jnp.arange(0, MLA_ROPE, 2, dtype=F32) / MLA_ROPE)
    ang = jnp.arange(T, dtype=F32)[:, None] * freqs[None, :]
    cos, sin = jnp.cos(ang), jnp.sin(ang)
    z = lambda n: jnp.zeros((T, n), F32)
    scale = (MLA_NOPE + MLA_ROPE) ** -0.5 * LOG2E
    ck = jnp.concatenate([z(64), cos, cos, z(32)], axis=1)
    sk = jnp.concatenate([z(64), -sin, sin, z(32)], axis=1)
    cq = jnp.concatenate([jnp.ones((T, 64), F32), cos, cos, z(32)], axis=1) * scale
    sq = sk * scale
    return cq, sq, ck, sk


def _proj_even(h, g, win, gq, gkv, wqa, wqb, wk, wv, tables, T):
    N = h.shape[0]
    nt = T // TM
    full = lambda a: pl.BlockSpec(a.shape, lambda i: (0,) * a.ndim)
    tab = pl.BlockSpec((TM, 128), lambda i: (i % nt, 0))
    row = lambda w: pl.BlockSpec((TM, w), lambda i: (i, 0))
    widths = (1024, 1024, None, 1024, 128, 128, 128)
    vt_spec = pl.BlockSpec((None, 1024, TM), lambda i: (i // nt, 0, i % nt))
    vt_shape = jax.ShapeDtypeStruct((N // T, 1024, T), BF16)
    return pl.pallas_call(
        _proj_even_kernel,
        grid=(N // TM,),
        in_specs=[row(D), full(g), full(win), full(gq), full(gkv), full(wqa), full(wqb), full(wk), full(wv),
                  tab, tab, tab, tab],
        out_specs=[vt_spec if w is None else row(w) for w in widths],
        out_shape=[vt_shape if w is None else jax.ShapeDtypeStruct((N, w), BF16) for w in widths],
        compiler_params=_cparams(("arbitrary",)),
        name="proj_even",
    )(h, g, win, gq, gkv, wqa, wqb, wk, wv, *tables)


def _finish_t(accs):
    ot = jnp.concatenate([acc[:HD] / acc[HD:HD + 1] for acc in accs], axis=0)
    return ot.T.astype(BF16)


def _causal_attention_t(i, q_of, k_ref, vt_ref, o_ref, s_ref, m_ref, acc_ref, *, tq, tk, bias_of=None):
    n_full = i * (tq // tk)
    key = lax.broadcasted_iota(jnp.int32, (tk, tq), 0)
    qry = i * tq + lax.broadcasted_iota(jnp.int32, (tk, tq), 1)

    def scores_to(slot, start, boundary):
        start = pl.multiple_of(start, tk)
        for j in range(2):
            st = _dot_nt(k_ref[pl.ds(start, tk), 128 * j:128 * (j + 1)], q_of(j))
            if bias_of is not None:
                st = st + bias_of(start, j)
            if boundary:
                st = jnp.where(start + key <= qry, st, NEG)
            s_ref[slot, j] = st

    def update_from(slot, start):
        start = pl.multiple_of(start, tk)
        m_new = [jnp.maximum(m_ref[j], jnp.max(s_ref[slot, j], axis=0, keepdims=True)) for j in range(2)]
        pts = [jnp.exp2(s_ref[slot, j] - m_new[j]).astype(BF16) for j in range(2)]
        for j in range(2):
            acc_ref[j] = (jnp.exp2(m_ref[j] - m_new[j]) * acc_ref[j]
                          + _dot(vt_ref[128 * j:128 * (j + 1), pl.ds(start, tk)], pts[j]))
            m_ref[j] = m_new[j]

    m_ref[...] = jnp.full(m_ref.shape, NEG, F32)
    acc_ref[...] = jnp.zeros(acc_ref.shape, F32)
    for d in range(tq // tk - 1):
        scores_to(0, (n_full + d) * tk, True)
        update_from(0, (n_full + d) * tk)
    first = (n_full + tq // tk - 1) * tk
    scores_to(0, first, True)

    def block_start(k):
        return jnp.where(k == 0, first, (k - 1) * tk)

    def pair(p, _):
        scores_to(1, block_start(2 * p + 1), False)
        update_from(0, block_start(2 * p))
        scores_to(0, block_start(2 * p + 2), False)
        update_from(1, block_start(2 * p + 1))
        return 0

    lax.fori_loop(0, n_full // 2, pair, 0)

    @pl.when(n_full % 2 == 1)
    def _():
        scores_to(1, block_start(n_full), False)
        update_from(0, block_start(n_full - 1))
        update_from(1, block_start(n_full))

    @pl.when(n_full % 2 == 0)
    def _():
        update_from(0, block_start(n_full))

    o_ref[...] = _finish_t([acc_ref[j] for j in range(2)])


def _attention_t_scratch(tq, tk):
    return [pltpu.VMEM((2, 2, tk, tq), F32), pltpu.VMEM((2, 1, tq), F32), pltpu.VMEM((2, 128, tq), F32)]


def _mla_kernel(q_ref, k_ref, vt_ref, o_ref, s_ref, m_ref, acc_ref, *, tq, tk):
    _causal_attention_t(pl.program_id(2), lambda j: q_ref[:, 128 * j:128 * (j + 1)], k_ref, vt_ref, o_ref,
                        s_ref, m_ref, acc_ref, tq=tq, tk=tk)


def _mla_attention(q, k, vt, B, T, tq=512, tk=512):
    nq = T // tq
    return pl.pallas_call(
        functools.partial(_mla_kernel, tq=tq, tk=tk),
        grid=(B, MLA_HEADS // 2, nq),
        in_specs=[pl.BlockSpec((tq, 256), lambda b, p, i: (b * nq + i, p)),
                  pl.BlockSpec((T, 256), lambda b, p, i: (b, p)),
                  pl.BlockSpec((None, 256, T), lambda b, p, i: (b, p, 0))],
        out_specs=pl.BlockSpec((tq, 128), lambda b, p, i: (b * nq + i, p)),
        out_shape=jax.ShapeDtypeStruct((B * T, MLA_HEADS * MLA_V), BF16),
        scratch_shapes=_attention_t_scratch(tq, tk),
        compiler_params=_cparams(("arbitrary", "arbitrary", "arbitrary")),
        name="mla_attention",
    )(q, k, vt)


def _swa_kernel(sink_ref, q_ref, k_ref, v_ref, vsw_ref, tb_ref, o_ref, *, tq):
    i = pl.program_id(1)
    G = SWA_HEADS // SWA_KV
    qi = lax.broadcasted_iota(jnp.int32, (128, 256), 0)
    kj = lax.broadcasted_iota(jnp.int32, (128, 256), 1)
    lane = lax.broadcasted_iota(jnp.int32, (128, 128), 1)
    for r in range(tq // 128):
        qstart = i * tq + 128 * r
        kstart = pl.multiple_of(jnp.maximum(qstart - 128, 0), 128)
        off = pl.multiple_of(qstart - kstart, 128)
        dist = off + qi - kj
        valid = (dist >= 0) & (dist < SWA_WIN)
        kw = k_ref[pl.ds(kstart, 256), :]
        vw = v_ref[pl.ds(kstart, 256), :]
        vsw = vsw_ref[pl.ds(kstart, 256), :]
        halves = [None] * SWA_HEADS
        for c in range(SWA_KV):
            qg = jnp.concatenate([q_ref[128 * r:128 * (r + 1), 128 * (G * c + g):128 * (G * c + g + 1)]
                                  for g in range(G)], axis=0)
            s_all = _dot_nt(qg, kw)
            for g in range(G):
                hq = G * c + g
                s = s_all[128 * g:128 * (g + 1)] + tb_ref[hq, pl.ds(off, 128), :]
                s = jnp.where(valid, s, NEG)
                sink = sink_ref[hq]
                m = jnp.maximum(jnp.max(s, axis=-1, keepdims=True), sink)
                p = jnp.exp(s - m)
                denom = jnp.sum(p, axis=-1, keepdims=True) + jnp.exp(sink - m)
                p = (p / denom).astype(BF16)
                halves[hq] = _dot(p, vw if c == hq % 2 else vsw)
        for mp in range(SWA_HEADS // 2):
            o_ref[128 * r:128 * (r + 1), 128 * mp:128 * (mp + 1)] = jnp.where(
                lane < HD, halves[2 * mp], halves[2 * mp + 1]).astype(BF16)


def _swa_attention(q8, k, v, vsw, sinks, tb, B, T, tq=512):
    nq = T // tq
    kv = pl.BlockSpec((T, 128), lambda b, i: (b, 0))
    return pl.pallas_call(
        functools.partial(_swa_kernel, tq=tq),
        grid=(B, nq),
        in_specs=[pl.BlockSpec(memory_space=pltpu.SMEM),
                  pl.BlockSpec((tq, 1024), lambda b, i: (b * nq + i, 0)),
                  kv, kv, kv,
                  pl.BlockSpec((8, 256, 256), lambda b, i: (0, 0, 0))],
        out_specs=pl.BlockSpec((tq, 512), lambda b, i: (b * nq + i, 0)),
        out_shape=jax.ShapeDtypeStruct((B * T, SWA_HEADS * HD), BF16),
        compiler_params=_cparams(("arbitrary", "arbitrary")),
        name="swa_attention",
    )(sinks, q8, k, v, vsw, tb)


def _outproj_router_kernel(h_ref, oa_ref, ob_ref, wo_ref, g_ref, wr_ref, br_ref,
                           hn_ref, u_ref, ri_ref, rg_ref, cnt_ref, base_ref):
    i = pl.program_id(0)

    @pl.when(i == 0)
    def _():
        base_ref[...] = jnp.zeros_like(base_ref)

    hn = h_ref[...] + _dot(oa_ref[...], wo_ref[0:512, :]) + _dot(ob_ref[...], wo_ref[512:1024, :])
    hn_ref[...] = hn
    u = _rms(hn, g_ref[...])
    u_ref[...] = _pack_halves(u)
    u_hi = u.astype(BF16)
    u_lo = (u - u_hi.astype(F32)).astype(BF16)
    both = _dot_nt(wr_ref[...], u_hi)
    logits = both[:128] + both[128:] + _dot_nt(wr_ref[0:128, :], u_lo) + br_ref[...]
    tm = logits.shape[1]
    sub = lax.broadcasted_iota(jnp.int32, (8, tm), 0).astype(F32)
    big = 1e6
    isg = sub < N_GROUPS
    gl = jnp.where(isg, logits[N_EXP:N_EXP + 8], NEG)
    gmax = jnp.max(gl, axis=0, keepdims=True)
    gsel = jnp.min(jnp.where(gl == gmax, sub, big), axis=0, keepdims=True)
    g_gate = 1.0 / jnp.sum(jnp.where(isg, jnp.exp(gl - gmax), 0.0), axis=0, keepdims=True)
    el = jnp.zeros((EPG, tm), F32)
    for grp in range(N_GROUPS):
        el = jnp.where(gsel == grp, logits[EPG * grp:EPG * (grp + 1)], el)
    m1 = jnp.max(el, axis=0, keepdims=True)
    i1 = jnp.min(jnp.where(el == m1, sub, big), axis=0, keepdims=True)
    el2 = jnp.where(sub == i1, NEG, el)
    m2 = jnp.max(el2, axis=0, keepdims=True)
    i2 = jnp.min(jnp.where(el2 == m2, sub, big), axis=0, keepdims=True)
    e1 = gsel * EPG + i1
    e2 = gsel * EPG + i2
    r = jnp.exp(m2 - m1)
    gate1 = g_gate / (1.0 + r)
    gate2 = g_gate * r / (1.0 + r)
    rows = lax.broadcasted_iota(jnp.int32, (128, tm), 0).astype(F32)
    oh1 = (rows == e1).astype(F32)
    oh2 = (rows == e2).astype(F32)
    oh = oh1 + oh2
    t_row = lax.broadcasted_iota(jnp.int32, (tm, tm), 0)
    t_col = lax.broadcasted_iota(jnp.int32, (tm, tm), 1)
    earlier = jnp.where(t_row < t_col, 1.0, 0.0).astype(BF16)
    prefix = _dot(oh.astype(BF16), earlier) + base_ref[:, 0:1]
    rank1 = jnp.sum(oh1 * prefix, axis=0, keepdims=True)
    rank2 = jnp.sum(oh2 * prefix, axis=0, keepdims=True)
    base_ref[...] = base_ref[...] + jnp.sum(oh, axis=1, keepdims=True)
    cnt_ref[...] = base_ref[...]
    ri_ref[...] = jnp.where(sub == 0, e1, jnp.where(sub == 1, e2, jnp.where(sub == 2, rank1,
                            jnp.where(sub == 3, rank2, 0.0))))
    rg_ref[...] = jnp.where(rows == 0, gate1, jnp.where(rows == 1, gate2, 0.0)).T


def _outproj_router(h, oa, ob, wo, g, wr, br):
    N = h.shape[0]
    full = lambda a: pl.BlockSpec(a.shape, lambda i: (0,) * a.ndim)
    row = lambda w: pl.BlockSpec((TM, w), lambda i: (i, 0))
    return pl.pallas_call(
        _outproj_router_kernel,
        grid=(N // TM,),
        in_specs=[row(D), row(512), row(512), full(wo), full(g), full(wr), full(br)],
        out_specs=[row(D), row(HALF), pl.BlockSpec((8, TM), lambda i: (0, i)), row(128),
                   pl.BlockSpec((128, 128), lambda i: (0, 0))],
        out_shape=[jax.ShapeDtypeStruct((N, D), F32), jax.ShapeDtypeStruct((N, HALF), jnp.int32),
                   jax.ShapeDtypeStruct((8, N), F32), jax.ShapeDtypeStruct((N, 128), F32),
                   jax.ShapeDtypeStruct((128, 128), F32)],
        scratch_shapes=[pltpu.VMEM((128, 128), F32)],
        compiler_params=_cparams(("arbitrary",)),
        name="outproj_router",
    )(h, oa, ob, wo, g, wr, br)


def _moe_kernel(te_ref, tv_ref, x_ref, wg_ref, wu_ref, wd_ref, o_ref, wgu_s, wd_s):
    i = pl.program_id(0)
    valid = tv_ref[i]

    @pl.when((i == 0) | (te_ref[i] != te_ref[jnp.maximum(i - 1, 0)]))
    def _():
        wgu_s[:, :D_EXP] = wg_ref[...].astype(BF16)
        wgu_s[:, D_EXP:] = wu_ref[...].astype(BF16)
        wd_s[...] = wd_ref[...].astype(BF16)

    @pl.when(valid > 0)
    def _():
        rows = lax.broadcasted_iota(jnp.int32, x_ref.shape, 0)
        lo, hi = _unpack_halves(jnp.where(rows < valid, x_ref[...], 0))
        hgu = _dot(lo.astype(BF16), wgu_s[:HALF, :]) + _dot(hi.astype(BF16), wgu_s[HALF:, :])
        hg, hu = hgu[:, :D_EXP], hgu[:, D_EXP:]
        a = hg * (1.0 / (1.0 + jnp.exp(-hg))) * hu
        o_ref[...] = _pack_halves(_dot(a.astype(BF16), wd_s[...]))

    @pl.when(valid <= 0)
    def _():
        o_ref[...] = jnp.zeros_like(o_ref)


def _moe_experts(xs, tile_expert, tile_valid, w_gate, w_up, w_down, layer):
    P = xs.shape[0]
    grid_spec = pltpu.PrefetchScalarGridSpec(
        num_scalar_prefetch=2,
        grid=(P // TM_E,),
        in_specs=[pl.BlockSpec((TM_E, HALF), lambda i, te, tv: (i, 0)),
                  pl.BlockSpec((None, None, D, D_EXP), lambda i, te, tv: (layer, te[i], 0, 0)),
                  pl.BlockSpec((None, None, D, D_EXP), lambda i, te, tv: (layer, te[i], 0, 0)),
                  pl.BlockSpec((None, None, D_EXP, D), lambda i, te, tv: (layer, te[i], 0, 0))],
        out_specs=pl.BlockSpec((TM_E, HALF), lambda i, te, tv: (i, 0)),
        scratch_shapes=[pltpu.VMEM((D, 2 * D_EXP), BF16), pltpu.VMEM((D_EXP, D), BF16)],
    )
    return pl.pallas_call(
        _moe_kernel,
        grid_spec=grid_spec,
        out_shape=jax.ShapeDtypeStruct((P, HALF), jnp.int32),
        compiler_params=_cparams(("arbitrary",)),
        name="moe_experts",
    )(tile_expert, tile_valid, xs, w_gate, w_up, w_down)


def _sc_mesh():
    return plsc.VectorSubcoreMesh(core_axis_name="c", subcore_axis_name="s",
                                  num_cores=SC_CORES, num_subcores=SC_SUBCORES)


def _sc_scatter_rows(src, d1, d2, P):
    N, W = src.shape
    per_w = N // (SC_CORES * SC_SUBCORES)

    @functools.partial(pl.kernel, mesh=_sc_mesh(), out_type=jax.ShapeDtypeStruct((P, W), src.dtype),
                       scratch_types=[pltpu.VMEM((SC_CHUNK,), jnp.int32), pltpu.VMEM((SC_CHUNK, W), src.dtype)],
                       name="sc_scatter_rows")
    def k(src_hbm, d1_hbm, d2_hbm, out_hbm, idx_v, rows_v):
        wid = lax.axis_index("s") * SC_CORES + lax.axis_index("c")

        @pl.loop(0, per_w // SC_CHUNK)
        def _(c):
            off = wid * per_w + c * SC_CHUNK
            pltpu.sync_copy(src_hbm.at[pl.ds(off, SC_CHUNK)], rows_v)
            pltpu.sync_copy(d1_hbm.at[pl.ds(off, SC_CHUNK)], idx_v)
            pltpu.sync_copy(rows_v, out_hbm.at[idx_v])
            pltpu.sync_copy(d2_hbm.at[pl.ds(off, SC_CHUNK)], idx_v)
            pltpu.sync_copy(rows_v, out_hbm.at[idx_v])

    return k(src, d1, d2)


def _sc_gather_rows(table, idx):
    B, W = idx.shape[0], table.shape[1]
    per_w = B // (SC_CORES * SC_SUBCORES)

    @functools.partial(pl.kernel, mesh=_sc_mesh(), out_type=jax.ShapeDtypeStruct((B, W), table.dtype),
                       scratch_types=[pltpu.VMEM((SC_CHUNK,), jnp.int32), pltpu.VMEM((SC_CHUNK, W), table.dtype)],
                       name="sc_gather_rows")
    def k(table_hbm, idx_hbm, out_hbm, idx_v, rows_v):
        wid = lax.axis_index("s") * SC_CORES + lax.axis_index("c")

        @pl.loop(0, per_w // SC_CHUNK)
        def _(c):
            off = wid * per_w + c * SC_CHUNK
            pltpu.sync_copy(idx_hbm.at[pl.ds(off, SC_CHUNK)], idx_v)
            pltpu.sync_copy(table_hbm.at[idx_v], rows_v)
            pltpu.sync_copy(rows_v, out_hbm.at[pl.ds(off, SC_CHUNK)])

    return k(table, idx)


def _dest_kernel(offs_ref, ri_ref, d_ref):
    ri = ri_ref[...]
    base = jnp.zeros(ri.shape, F32)
    for e in range(N_EXP):
        base = jnp.where(ri == e, offs_ref[e].astype(F32), base)
    d_ref[...] = (base + pltpu.roll(ri, 6, 0)).astype(jnp.int32)


def _dispatch_plan(ri, cnt, N):
    counts = cnt[:N_EXP, 0].astype(jnp.int32)
    padded = ((counts + TM_E - 1) // TM_E) * TM_E
    ends = jnp.cumsum(padded)
    offs = ends - padded
    take = lambda t, i: t.at[i].get(mode="promise_in_bounds")
    dest = pl.pallas_call(
        _dest_kernel,
        in_specs=[pl.BlockSpec(memory_space=pltpu.SMEM), pl.BlockSpec((8, N), lambda: (0, 0))],
        out_specs=pl.BlockSpec((8, N), lambda: (0, 0)),
        out_shape=jax.ShapeDtypeStruct((8, N), jnp.int32),
        name="dispatch_rows",
    )(offs, ri)
    d1, d2 = dest[0], dest[1]
    P = 2 * N + N_EXP * TM_E
    tile_start = jnp.arange(P // TM_E, dtype=jnp.int32) * TM_E
    tile_expert = jnp.minimum(jnp.sum((ends[None, :] <= tile_start[:, None]).astype(jnp.int32), axis=1), N_EXP - 1)
    tile_valid = jnp.clip(take(offs + counts, tile_expert) - tile_start, 0, TM_E)
    return d1, d2, P, tile_expert, tile_valid


def _combine(h_ref, y1_ref, y2_ref, rg_ref):
    rg = rg_ref[...]
    lo1, hi1 = _unpack_halves(y1_ref[...])
    lo2, hi2 = _unpack_halves(y2_ref[...])
    g1, g2 = rg[:, 0:1], rg[:, 1:2]
    return h_ref[...] + jnp.concatenate([g1 * lo1 + g2 * lo2, g1 * hi1 + g2 * hi2], axis=1)


ODD_WIDTHS = (1024, 1024, 512, 512)


def _combine_proj_odd_kernel(h_ref, y1_ref, y2_ref, rg_ref, g_ref, win_ref, wvt_ref, ind_ref, hn_ref, vt_ref,
                             vdt_ref, *out_refs):
    hn = _combine(h_ref, y1_ref, y2_ref, rg_ref)
    hn_ref[...] = hn
    u = _rms(hn, g_ref[...]).astype(BF16)
    vt_ref[...] = _values_t(_dot_nt(wvt_ref[0:512, :], u)).astype(BF16)
    vdt_ref[...] = _dot_nt(wvt_ref[512:1024, :], u).astype(BF16)
    qc_ref, kc_ref, qd_ref, kd_ref = out_refs
    qc = _spread_heads(_dot(u, win_ref[:, 0:512]), lambda h: 0)
    kc = _spread_heads(_dot(u, win_ref[:, 512:1024]), lambda h: 0)
    for hh in range(8):
        qc_ref[:, 128 * hh:128 * (hh + 1)] = qc[hh].astype(BF16)
        kc_ref[:, 128 * hh:128 * (hh + 1)] = (kc[hh] + ind_ref[...]).astype(BF16)
    qd_ref[...] = _dot(u, win_ref[:, 1024:1536]).astype(BF16)
    kd_ref[...] = _dot(u, win_ref[:, 1536:2048]).astype(BF16)


def _odd_weights(w_in):
    s = HD ** -0.5 * LOG2E
    scale = np.asarray([s] * 512 + [1.0] * 512 + [s] * 512 + [1.0] * 512, np.float32)
    win = jnp.concatenate([w_in[:, 0:1024], w_in[:, 1536:2560]], axis=1) * jnp.asarray(scale)
    wvt = jnp.concatenate([w_in[:, 1024:1536], w_in[:, 2560:3072]], axis=1).T
    return win.astype(BF16), wvt.astype(BF16)


def _block_indicator(T):
    assert T // MOBA_L <= 64
    ind = np.zeros((T, 128), np.float32)
    ind[np.arange(T), 64 + np.arange(T) // MOBA_L] = 1.0
    return jnp.asarray(ind)


def _y_specs(N):
    return [pl.BlockSpec((TM, HALF), lambda i: (i, 0)), pl.BlockSpec((TM, HALF), lambda i: (i + N // TM, 0))]


def _combine_proj_odd(h, y12, rg, g, win, wvt, ind, T):
    N = h.shape[0]
    nt = T // TM
    full = lambda a: pl.BlockSpec(a.shape, lambda i: (0,) * a.ndim)
    row = lambda w: pl.BlockSpec((TM, w), lambda i: (i, 0))
    return pl.pallas_call(
        _combine_proj_odd_kernel,
        grid=(N // TM,),
        in_specs=[row(D)] + _y_specs(N) + [row(128), full(g), full(win), full(wvt),
                  pl.BlockSpec((TM, 128), lambda i: (i % nt, 0))],
        out_specs=[row(D), pl.BlockSpec((None, 1024, TM), lambda i: (i // nt, 0, i % nt)),
                   pl.BlockSpec((None, 512, TM), lambda i: (i // nt, 0, i % nt))]
        + [row(w) for w in ODD_WIDTHS],
        out_shape=[jax.ShapeDtypeStruct((N, D), F32), jax.ShapeDtypeStruct((N // T, 1024, T), BF16),
                   jax.ShapeDtypeStruct((N // T, 512, T), BF16)]
        + [jax.ShapeDtypeStruct((N, w), BF16) for w in ODD_WIDTHS],
        compiler_params=_cparams(("arbitrary",)),
        name="combine_proj_odd",
    )(h, y12, y12, rg, g, win, wvt, ind)


def _combine_final_kernel(h_ref, y1_ref, y2_ref, rg_ref, g_ref, o_ref):
    o_ref[...] = _rms(_combine(h_ref, y1_ref, y2_ref, rg_ref), g_ref[...])


def _combine_final(h, y12, rg, g):
    N = h.shape[0]
    row = lambda w: pl.BlockSpec((TM, w), lambda i: (i, 0))
    return pl.pallas_call(
        _combine_final_kernel,
        grid=(N // TM,),
        in_specs=[row(D)] + _y_specs(N) + [row(128), pl.BlockSpec((1, D), lambda i: (0, 0))],
        out_specs=row(D),
        out_shape=jax.ShapeDtypeStruct((N, D), F32),
        compiler_params=_cparams(("arbitrary",)),
        name="combine_final",
    )(h, y12, y12, rg, g)


def _moba_kernel(q_ref, k_ref, vt_ref, tb_ref, o_ref, kmean_ref, qa_ref, s_ref, m_ref, acc_ref, *, nkb):
    i = pl.program_id(2)
    L, tq = MOBA_L, MOBA_TQ

    @pl.when(i == 0)
    def _():
        kmean_ref[...] = jnp.zeros_like(kmean_ref)
        for n in range(nkb):
            kmean_ref[64 + n:65 + n, :] = jnp.sum(k_ref[n * L:(n + 1) * L, :].astype(F32), axis=0,
                                                  keepdims=True) * (1.0 / L)

    lane = lax.broadcasted_iota(jnp.int32, (tq, 128), 1)
    rows = lax.broadcasted_iota(jnp.int32, (128, tq), 0)
    own = 2 * i + jnp.where(lax.broadcasted_iota(jnp.int32, (128, tq), 1) >= L, 1, 0)
    blk = rows - 64
    rows_f = rows.astype(F32)
    for j in range(2):
        qj = q_ref[:, 128 * j:128 * (j + 1)]
        gsc = lax.dot_general(kmean_ref[:, 128 * j:128 * (j + 1)], qj.astype(F32), (((1,), (1,)), ((), ())),
                              preferred_element_type=F32, precision=lax.Precision.HIGHEST)
        g = jnp.where((blk >= 0) & (blk < own), gsc, NEG)
        allowed = blk == own
        for _ in range(MOBA_TOPK):
            mx = jnp.max(g, axis=0, keepdims=True)
            idx = jnp.min(jnp.where(g == mx, rows_f, 1e6), axis=0, keepdims=True)
            pick = rows_f == idx
            allowed = allowed | (pick & (mx > 0.5 * NEG))
            g = jnp.where(pick, 2.0 * NEG, g)
        qa_ref[j] = jnp.where(lane < HD, qj, jnp.where(allowed, 0.0, NEG).T.astype(BF16))

    def bias_of(start, j):
        parts = []
        for c in range(tq // L):
            off = jnp.minimum(i * tq - start - c * L, BIAS_CONST_FROM) + BIAS_PAD
            parts.append(tb_ref[j, :, pl.ds(pl.multiple_of(off, L), tq)])
        return jnp.concatenate(parts, axis=0)

    _causal_attention_t(i, lambda j: qa_ref[j], k_ref, vt_ref, o_ref, s_ref, m_ref, acc_ref, tq=tq, tk=tq,
                        bias_of=bias_of)


def _moba_attention(q, k, vt, tb, B, T):
    tq = MOBA_TQ
    nq = T // tq
    return pl.pallas_call(
        functools.partial(_moba_kernel, nkb=T // MOBA_L),
        grid=(B, 4, nq),
        in_specs=[pl.BlockSpec((tq, 256), lambda b, p, i: (b * nq + i, p)),
                  pl.BlockSpec((T, 256), lambda b, p, i: (b, p)),
                  pl.BlockSpec((None, 256, T), lambda b, p, i: (b, p, 0)),
                  pl.BlockSpec((2, MOBA_L, BIAS_ROWS), lambda b, p, i: (p, 0, 0))],
        out_specs=pl.BlockSpec((tq, 128), lambda b, p, i: (b * nq + i, p)),
        out_shape=jax.ShapeDtypeStruct((B * T, 512), BF16),
        scratch_shapes=[pltpu.VMEM((128, 256), F32), pltpu.VMEM((2, tq, 128), BF16)]
        + _attention_t_scratch(tq, tq),
        compiler_params=_cparams(("arbitrary", "arbitrary", "arbitrary")),
        name="moba_attention",
    )(q, k, vt, tb)


def _sb_kernel(q_ref, k_ref, vt_ref, o_ref, *, tq, pairs):
    i = pl.program_id(2)
    lane = lax.broadcasted_iota(jnp.int32, (tq, 128), 1)
    qs = []
    for p in range(pairs):
        q_pair = q_ref[:, 128 * p:128 * (p + 1)]
        qs += [jnp.where(lane < HD, q_pair, 0).astype(BF16), jnp.where(lane >= HD, q_pair, 0).astype(BF16)]
    key = lax.broadcasted_iota(jnp.int32, (tq, tq), 0)
    qry = lax.broadcasted_iota(jnp.int32, (tq, tq), 1)
    past = key < qry
    suffix = jnp.where(qry >= key, -1.0, 0.0).astype(BF16)
    heads = range(2 * pairs)

    def step(kb, carry, boundary):
        start = pl.multiple_of(kb * tq, tq)
        zs = [_dot_nt(k_ref[pl.ds(start, tq), 128 * (h // 2):128 * (h // 2 + 1)], qs[h]) for h in heads]
        parts = []
        for z in zs:
            soft = jnp.maximum(z, 0.0) + jnp.log2(1.0 + jnp.exp2(-jnp.abs(z)))
            if boundary:
                soft = jnp.where(past, soft, 0.0)
            hi = soft.astype(BF16)
            parts.append((hi, (soft - hi.astype(F32)).astype(BF16)))
        sums = [_dot(suffix, hi) + _dot(suffix, lo) for hi, lo in parts]
        ws = []
        for h in heads:
            w = jnp.exp2(zs[h] + (carry[h][0] + sums[h]))
            ws.append((jnp.where(past, w, 0.0) if boundary else w).astype(BF16))
        return tuple((carry[h][0] + sums[h][0:1], carry[h][1]
                      + _dot(vt_ref[128 * (h // 2):128 * (h // 2 + 1), pl.ds(start, tq)], ws[h])) for h in heads)

    def live(carry):
        top = carry[0][0]
        for c, _ in carry[1:]:
            top = jnp.maximum(top, c)
        return (jnp.max(top) > SB_DONE).astype(jnp.int32)

    def body(state):
        t, _, carry = state
        carry = step(i - 1 - t, carry, False)
        return t + 1, live(carry), carry

    init = tuple((jnp.zeros((1, tq), F32), jnp.zeros((128, tq), F32)) for _ in range(2 * pairs))
    carry = step(i, init, True)
    _, _, carry = lax.while_loop(lambda st: (st[0] < i) & (st[1] > 0), body, (jnp.int32(0), live(carry), carry))
    rows = lax.broadcasted_iota(jnp.int32, (128, tq), 0)
    for p in range(pairs):
        ot = jnp.where(rows < HD, carry[2 * p][1], carry[2 * p + 1][1])
        o_ref[:, 128 * p:128 * (p + 1)] = ot.T.astype(BF16)


def _sb_attention(q, k, vt, B, T, tq=256, pairs=4):
    nq = T // tq
    w = 128 * pairs
    return pl.pallas_call(
        functools.partial(_sb_kernel, tq=tq, pairs=pairs),
        grid=(B, 4 // pairs, nq),
        in_specs=[pl.BlockSpec((tq, w), lambda b, p, i: (b * nq + i, p)),
                  pl.BlockSpec((T, w), lambda b, p, i: (b, p)),
                  pl.BlockSpec((None, w, T), lambda b, p, i: (b, p, 0))],
        out_specs=pl.BlockSpec((tq, w), lambda b, p, i: (b * nq + i, p)),
        out_shape=jax.ShapeDtypeStruct((B * T, 512), BF16),
        compiler_params=_cparams(("arbitrary", "arbitrary", "arbitrary")),
        name="sb_attention",
    )(q, k, vt)


def _router_weights(w_group, b_group, w_router, b_router):
    pad = 128 - N_EXP - N_GROUPS
    wr = jnp.concatenate([w_router, w_group, jnp.zeros((D, pad), F32)], axis=1).T
    hi = wr.astype(BF16)
    lo = (wr - hi.astype(F32)).astype(BF16)
    br = jnp.concatenate([b_router, b_group, jnp.zeros((pad,), F32)])[:, None]
    return jnp.concatenate([hi, lo], axis=0), br


def _ffn(h, oa, ob, wo, g, w_group, b_group, w_router, b_router, w_gate, w_up, w_down, layer):
    wr, br = _router_weights(w_group, b_group, w_router, b_router)
    hn, u, ri, rg, cnt = _outproj_router(h, oa, ob, wo.astype(BF16), g[None, :], wr, br)
    d1, d2, P, tile_expert, tile_valid = _dispatch_plan(ri, cnt, h.shape[0])
    xs = _sc_scatter_rows(u, d1, d2, P)
    ys = _moe_experts(xs, tile_expert, tile_valid, w_gate, w_up, w_down, layer)
    return hn, _sc_gather_rows(ys, jnp.concatenate([d1, d2])), rg


def kernel(x, norm_mix, norm_ffn, norm_final, rel_bias_table, w_in_even, g_mla_q, g_mla_kv, w_mla_qb, w_mla_kvb, swa_sinks, w_out_even, w_in_odd, w_out_odd, moe_w_group, moe_b_group, moe_w_router, moe_b_router, moe_w_gate, moe_w_up, moe_w_down):
    B, T, _ = x.shape
    h = x.reshape(B * T, D)
    tb_swa, tb_moba = _bias_tables(rel_bias_table)

    win, wqa, wqb, wk, wv = _even_weights(w_in_even[0], w_mla_qb[0], w_mla_kvb[0])
    q, k, v, qs8, ks, vs, vsw = _proj_even(h, norm_mix[0][None, :], win, g_mla_q[0][None, :], g_mla_kv[0][None, :],
                                           wqa, wqb, wk, wv, _rope_tables(T), T)
    oa = _mla_attention(q, k, v, B, T)
    ob = _swa_attention(qs8, ks, vs, vsw, swa_sinks[0], tb_swa, B, T)
    h, y12, rg = _ffn(h, oa, ob, w_out_even[0], norm_ffn[0], moe_w_group[0], moe_b_group[0], moe_w_router[0],
                      moe_b_router[0], moe_w_gate, moe_w_up, moe_w_down, 0)

    h, vct, vdt, qc, kc, qd, kd = _combine_proj_odd(h, y12, rg, norm_mix[1][None, :], *_odd_weights(w_in_odd[0]),
                                                   _block_indicator(T), T)
    oc = _moba_attention(qc, kc, vct, tb_moba, B, T)
    od = _sb_attention(qd, kd, vdt, B, T)
    h, y12, rg = _ffn(h, oc, od, w_out_odd[0], norm_ffn[1], moe_w_group[1], moe_b_group[1], moe_w_router[1],
                      moe_b_router[1], moe_w_gate, moe_w_up, moe_w_down, 1)
    out = _combine_final(h, y12, rg, norm_final[None, :])
    return out.reshape(B, T, D)
```

```python
import functools
import math

import numpy as np
import jax
import jax.numpy as jnp
from jax import lax
from jax.experimental import pallas as pl
from jax.experimental.pallas import tpu as pltpu
from jax.experimental.pallas import tpu_sc as plsc

F32 = jnp.float32
BF16 = jnp.bfloat16
NEG = -1e30
EPS = 1e-6

D = 1024
HD = 64
LANES = 128
MLA_HEADS, MLA_QR, MLA_KVR, MLA_NOPE, MLA_ROPE, MLA_V = 8, 256, 128, 64, 32, 64
ROPE_THETA = 10000.0
SWA_HEADS, SWA_KV, SWA_WIN = 8, 2, 128
MOBA_L, MOBA_TOPK = 256, 3
REL_BUCKETS, REL_MAX = 32, 2048
N_GROUPS, EPG, N_EXP, D_EXP = 4, 8, 32, 256
BIAS_CONST_FROM = 1792
MOBA_TQ = 2 * MOBA_L
BIAS_PAD = MOBA_L
BIAS_ROWS = BIAS_PAD + BIAS_CONST_FROM + MOBA_TQ
SB_DONE = -160.0
TM = 512
TM_E = 512
VMEM_LIMIT = 56 * 1024 * 1024
SC_CORES, SC_SUBCORES = 2, 16
SC_CHUNK = 128
HALF = D // 2
VT_ROWS = HD + 16
LOG2E = math.log2(math.e)


def _pack_halves(x):
    lo = pltpu.bitcast(x[:, :HALF].astype(BF16).astype(F32), jnp.uint32) >> 16
    hi = pltpu.bitcast(x[:, HALF:].astype(BF16).astype(F32), jnp.uint32) & jnp.uint32(0xFFFF0000)
    return pltpu.bitcast(lo | hi, jnp.int32)


def _unpack_halves(w):
    w = pltpu.bitcast(w, jnp.uint32)
    return pltpu.bitcast(w << 16, F32), pltpu.bitcast(w & jnp.uint32(0xFFFF0000), F32)


def _cparams(sem):
    return pltpu.CompilerParams(dimension_semantics=sem, vmem_limit_bytes=VMEM_LIMIT)


def _dot(a, b):
    return jnp.dot(a, b, preferred_element_type=F32)


def _dot_nt(a, b):
    return lax.dot_general(a, b, (((1,), (1,)), ((), ())), preferred_element_type=F32)


def _rms(x, g):
    return x * lax.rsqrt(jnp.mean(x * x, axis=-1, keepdims=True) + EPS) * g


def _values_t(vt):
    ones = jnp.ones((VT_ROWS - HD, vt.shape[1]), vt.dtype)
    return jnp.concatenate([blk for h in range(8) for blk in (vt[HD * h:HD * (h + 1)], ones)], axis=0)


def _spread_heads(x, lane_of):
    lane = lax.broadcasted_iota(jnp.int32, (x.shape[0], 128), 1)
    out = []
    for h in range(8):
        pair = x[:, 128 * (h // 2):128 * (h // 2 + 1)]
        if HD * (h % 2) != lane_of(h):
            pair = pltpu.roll(pair, HD, 1)
        out.append(jnp.where((lane >= lane_of(h)) & (lane < lane_of(h) + HD), pair, 0.0))
    return out


def _flash_update_t(st, vt, m, acc):
    m_new = jnp.maximum(m, jnp.max(st, axis=0, keepdims=True))
    alpha = jnp.exp2(m - m_new)
    pt = jnp.exp2(st - m_new).astype(BF16)
    return m_new, alpha * acc + _dot(vt, pt)


def _flash_update(s, v, m, l, acc):
    m_new = jnp.maximum(m, jnp.max(s, axis=-1, keepdims=True))
    alpha = jnp.exp(m - m_new)
    p = jnp.exp(s - m_new)
    l = alpha * l + jnp.sum(p, axis=-1, keepdims=True)
    acc = alpha * acc + _dot(p.astype(BF16), v)
    return m_new, l, acc


def _t5_bucket_np(dist):
    n = np.maximum(dist, 0)
    max_exact = REL_BUCKETS // 2
    nf = np.maximum(n, 1).astype(np.float32)
    large = max_exact + (np.log(nf / np.float32(max_exact)) / np.float32(math.log(REL_MAX / max_exact))
                         * np.float32(REL_BUCKETS - max_exact)).astype(np.int32)
    large = np.minimum(large, REL_BUCKETS - 1)
    return np.where(n < max_exact, n, large).astype(np.int32)


def _bias_kernel(table_ref, bucket_ref, out_ref, *, scale):
    b = bucket_ref[...]
    for h in range(8):
        acc = jnp.zeros(b.shape, F32)
        for k in range(REL_BUCKETS):
            acc = jnp.where(b == k, table_ref[k, h] * scale, acc)
        out_ref[h] = acc


def _bias_lookup(rel_table, dist, scale, name):
    R, C = dist.shape
    return pl.pallas_call(
        functools.partial(_bias_kernel, scale=scale),
        grid=(C // 256,),
        in_specs=[pl.BlockSpec(memory_space=pltpu.SMEM),
                  pl.BlockSpec((R, 256), lambda m: (0, m))],
        out_specs=pl.BlockSpec((8, R, 256), lambda m: (0, 0, m)),
        out_shape=jax.ShapeDtypeStruct((8, R, C), F32),
        compiler_params=_cparams(("arbitrary",)),
        name=name,
    )(rel_table, jnp.asarray(_t5_bucket_np(dist)))


def _bias_tables(rel_table):
    swa = _bias_lookup(rel_table, np.arange(256)[:, None] - np.arange(256)[None, :], 1.0, "bias_swa")
    moba = _bias_lookup(rel_table, np.arange(BIAS_ROWS)[None, :] - BIAS_PAD - np.arange(MOBA_L)[:, None], LOG2E,
                        "bias_moba")
    return swa, moba


def _proj_even_kernel(h_ref, g_ref, win_ref, gq_ref, gkv_ref, wqa_ref, wqb_ref, wk_ref, wv_ref,
                      cq_ref, sq_ref, ck_ref, sk_ref,
                      q_ref, k_ref, v_ref, qs_ref, ks_ref, vs_ref, vsw_ref):
    u = _rms(h_ref[...], g_ref[...]).astype(BF16)
    nq = _rms(_dot(u, win_ref[:, 0:256]), gq_ref[...]).astype(BF16)
    qa = _dot(nq, wqa_ref[...])
    qb = _dot(nq, wqb_ref[...])
    cq, sq = cq_ref[...], sq_ref[...]
    for hh in range(MLA_HEADS):
        sl = slice(128 * hh, 128 * (hh + 1))
        q_ref[:, sl] = (qa[:, sl] * cq + qb[:, sl] * sq).astype(BF16)
    nkv = _rms(_dot(u, win_ref[:, 256:384]), gkv_ref[...]).astype(BF16)
    kn = _dot(nkv, wk_ref[...])
    v_ref[...] = _values_t(_dot_nt(wv_ref[...], nkv)).astype(BF16)
    kr = _dot(u, win_ref[:, 1280:1408]) * ck_ref[...] + _dot(u, win_ref[:, 1408:1536]) * sk_ref[...]
    for hh in range(MLA_HEADS):
        sl = slice(128 * hh, 128 * (hh + 1))
        k_ref[:, sl] = (kn[:, sl] + kr).astype(BF16)
    qs = _spread_heads(_dot(u, win_ref[:, 384:896]), lambda hq: HD * (hq // (SWA_HEADS // SWA_KV)))
    for hq in range(SWA_HEADS):
        qs_ref[:, 128 * hq:128 * (hq + 1)] = qs[hq].astype(BF16)
    ks_ref[...] = _dot(u, win_ref[:, 896:1024]).astype(BF16)
    vs_ref[...] = _dot(u, win_ref[:, 1024:1152]).astype(BF16)
    vsw_ref[...] = _dot(u, win_ref[:, 1152:1280]).astype(BF16)


def _cols(w, idx, scale=None):
    idx = np.asarray(idx)
    cuts = [0] + [n for n in range(1, len(idx)) if (idx[n] < 0) != (idx[n - 1] < 0)
                  or (idx[n] >= 0 and idx[n] != idx[n - 1] + 1)] + [len(idx)]
    parts = [jnp.zeros((w.shape[0], b - a), w.dtype) if idx[a] < 0 else w[:, idx[a]:idx[a] + b - a]
             for a, b in zip(cuts[:-1], cuts[1:])]
    out = jnp.concatenate(parts, axis=1)
    if scale is not None:
        out = out * scale
    return out


def _even_weights(w_in, w_qb, w_kvb):
    o_cq, o_ckv, o_kr = 0, MLA_QR, MLA_QR + MLA_KVR
    o_qs = o_kr + MLA_ROPE
    o_ks = o_qs + SWA_HEADS * HD
    o_vs = o_ks + SWA_KV * HD
    idx = list(range(o_cq, o_cq + MLA_QR)) + list(range(o_ckv, o_ckv + MLA_KVR))
    scale = [1.0] * len(idx)
    idx += list(range(o_qs, o_qs + SWA_HEADS * HD))
    scale += [HD ** -0.5] * (SWA_HEADS * HD)
    idx += list(range(o_ks, o_ks + 128))
    idx += list(range(o_vs, o_vs + 128))
    idx += list(range(o_vs + 64, o_vs + 128)) + list(range(o_vs, o_vs + 64))
    scale += [1.0] * 384
    half = MLA_ROPE // 2
    idx += [-1] * 64 + list(range(o_kr, o_kr + MLA_ROPE)) + [-1] * 32
    idx += [-1] * 64 + list(range(o_kr + half, o_kr + MLA_ROPE)) + list(range(o_kr, o_kr + half)) + [-1] * 32
    scale += [1.0] * 256
    win = _cols(w_in, idx, jnp.asarray(np.asarray(scale, np.float32))).astype(BF16)
    qw = MLA_NOPE + MLA_ROPE
    ia, ib, ik, iv = [], [], [], []
    for h in range(MLA_HEADS):
        ia += list(range(qw * h, qw * h + qw)) + [-1] * 32
        ib += [-1] * 64 + list(range(qw * h + MLA_NOPE + half, qw * h + qw)) \
            + list(range(qw * h + MLA_NOPE, qw * h + MLA_NOPE + half)) + [-1] * 32
        ik += list(range(128 * h, 128 * h + MLA_NOPE)) + [-1] * 64
        iv += list(range(128 * h + MLA_NOPE, 128 * h + 128))
    return (win, _cols(w_qb, ia).astype(BF16), _cols(w_qb, ib).astype(BF16),
            _cols(w_kvb, ik).astype(BF16), _cols(w_kvb, iv).T.astype(BF16))


def _rope_tables(T):
    half = MLA_ROPE // 2
    freqs = ROPE_THETA ** (-jnp.arange(0, MLA_ROPE, 2, dtype=F32) / MLA_ROPE)
    ang = jnp.arange(T, dtype=F32)[:, None] * freqs[None, :]
    cos, sin = jnp.cos(ang), jnp.sin(ang)
    z = lambda n: jnp.zeros((T, n), F32)
    scale = (MLA_NOPE + MLA_ROPE) ** -0.5 * LOG2E
    ck = jnp.concatenate([z(64), cos, cos, z(32)], axis=1)
    sk = jnp.concatenate([z(64), -sin, sin, z(32)], axis=1)
    cq = jnp.concatenate([jnp.ones((T, 64), F32), cos, cos, z(32)], axis=1) * scale
    sq = sk * scale
    return cq, sq, ck, sk


def _proj_even(h, g, win, gq, gkv, wqa, wqb, wk, wv, tables, T):
    N = h.shape[0]
    nt = T // TM
    full = lambda a: pl.BlockSpec(a.shape, lambda i: (0,) * a.ndim)
    tab = pl.BlockSpec((TM, 128), lambda i: (i % nt, 0))
    row = lambda w: pl.BlockSpec((TM, w), lambda i: (i, 0))
    widths = (1024, 1024, None, 1024, 128, 128, 128)
    vt_spec = pl.BlockSpec((None, 8 * VT_ROWS, TM), lambda i: (i // nt, 0, i % nt))
    vt_shape = jax.ShapeDtypeStruct((N // T, 8 * VT_ROWS, T), BF16)
    return pl.pallas_call(
        _proj_even_kernel,
        grid=(N // TM,),
        in_specs=[row(D), full(g), full(win), full(gq), full(gkv), full(wqa), full(wqb), full(wk), full(wv),
                  tab, tab, tab, tab],
        out_specs=[vt_spec if w is None else row(w) for w in widths],
        out_shape=[vt_shape if w is None else jax.ShapeDtypeStruct((N, w), BF16) for w in widths],
        compiler_params=_cparams(("arbitrary",)),
        name="proj_even",
    )(h, g, win, gq, gkv, wqa, wqb, wk, wv, *tables)


def _finish_t(accs):
    ot = jnp.concatenate([acc[:HD] / acc[HD:HD + 1] for acc in accs], axis=0)
    return ot.T.astype(BF16)


def _causal_attention_t(i, q_of, k_ref, vt_ref, o_ref, s_ref, m_ref, acc_ref, *, tq, tk, bias_of=None):
    n_full = i * (tq // tk)
    key = lax.broadcasted_iota(jnp.int32, (tk, tq), 0)
    qry = i * tq + lax.broadcasted_iota(jnp.int32, (tk, tq), 1)

    def scores_to(slot, start, boundary):
        start = pl.multiple_of(start, tk)
        for j in range(2):
            st = _dot_nt(k_ref[pl.ds(start, tk), 128 * j:128 * (j + 1)], q_of(j))
            if bias_of is not None:
                st = st + bias_of(start, j)
            if boundary:
                st = jnp.where(start + key <= qry, st, NEG)
            s_ref[slot, j] = st

    def update_from(slot, start):
        start = pl.multiple_of(start, tk)
        m_new = [jnp.maximum(m_ref[j], jnp.max(s_ref[slot, j], axis=0, keepdims=True)) for j in range(2)]
        pts = [jnp.exp2(s_ref[slot, j] - m_new[j]).astype(BF16) for j in range(2)]
        for j in range(2):
            acc_ref[j] = (jnp.exp2(m_ref[j] - m_new[j]) * acc_ref[j]
                          + _dot(vt_ref[VT_ROWS * j:VT_ROWS * (j + 1), pl.ds(start, tk)], pts[j]))
            m_ref[j] = m_new[j]

    m_ref[...] = jnp.full(m_ref.shape, NEG, F32)
    acc_ref[...] = jnp.zeros(acc_ref.shape, F32)
    for d in range(tq // tk - 1):
        scores_to(0, (n_full + d) * tk, True)
        update_from(0, (n_full + d) * tk)
    first = (n_full + tq // tk - 1) * tk
    scores_to(0, first, True)

    def block_start(k):
        return jnp.where(k == 0, first, (k - 1) * tk)

    def pair(k):
        scores_to(1, block_start(k + 1), False)
        update_from(0, block_start(k))
        scores_to(0, block_start(k + 2), False)
        update_from(1, block_start(k + 1))

    def quad(p, _):
        pair(4 * p)
        pair(4 * p + 2)
        return 0

    lax.fori_loop(0, n_full // 4, quad, 0)

    @pl.when(n_full % 4 >= 2)
    def _():
        pair((n_full // 4) * 4)

    @pl.when(n_full % 2 == 1)
    def _():
        scores_to(1, block_start(n_full), False)
        update_from(0, block_start(n_full - 1))
        update_from(1, block_start(n_full))

    @pl.when(n_full % 2 == 0)
    def _():
        update_from(0, block_start(n_full))

    o_ref[...] = _finish_t([acc_ref[j] for j in range(2)])


def _attention_t_scratch(tq, tk):
    return [pltpu.VMEM((2, 2, tk, tq), F32), pltpu.VMEM((2, 1, tq), F32), pltpu.VMEM((2, VT_ROWS, tq), F32)]


def _mla_kernel(q_ref, k_ref, vt_ref, o_ref, s_ref, m_ref, acc_ref, *, tq, tk):
    _causal_attention_t(pl.program_id(2), lambda j: q_ref[:, 128 * j:128 * (j + 1)], k_ref, vt_ref, o_ref,
                        s_ref, m_ref, acc_ref, tq=tq, tk=tk)


def _mla_attention(q, k, vt, B, T, tq=512, tk=512):
    nq = T // tq
    return pl.pallas_call(
        functools.partial(_mla_kernel, tq=tq, tk=tk),
        grid=(B, MLA_HEADS // 2, nq),
        in_specs=[pl.BlockSpec((tq, 256), lambda b, p, i: (b * nq + i, p)),
                  pl.BlockSpec((T, 256), lambda b, p, i: (b, p)),
                  pl.BlockSpec((None, 2 * VT_ROWS, T), lambda b, p, i: (b, p, 0))],
        out_specs=pl.BlockSpec((tq, 128), lambda b, p, i: (b * nq + i, p)),
        out_shape=jax.ShapeDtypeStruct((B * T, MLA_HEADS * MLA_V), BF16),
        scratch_shapes=_attention_t_scratch(tq, tk),
        compiler_params=_cparams(("arbitrary", "arbitrary", "arbitrary")),
        name="mla_attention",
    )(q, k, vt)


def _swa_kernel(sink_ref, q_ref, k_ref, v_ref, vsw_ref, tb_ref, o_ref, *, tq):
    i = pl.program_id(1)
    G = SWA_HEADS // SWA_KV
    qi = lax.broadcasted_iota(jnp.int32, (128, 256), 0)
    kj = lax.broadcasted_iota(jnp.int32, (128, 256), 1)
    lane = lax.broadcasted_iota(jnp.int32, (128, 128), 1)
    for r in range(tq // 128):
        qstart = i * tq + 128 * r
        kstart = pl.multiple_of(jnp.maximum(qstart - 128, 0), 128)
        off = pl.multiple_of(qstart - kstart, 128)
        dist = off + qi - kj
        valid = (dist >= 0) & (dist < SWA_WIN)
        kw = k_ref[pl.ds(kstart, 256), :]
        vw = v_ref[pl.ds(kstart, 256), :]
        vsw = vsw_ref[pl.ds(kstart, 256), :]
        halves = [None] * SWA_HEADS
        for c in range(SWA_KV):
            qg = jnp.concatenate([q_ref[128 * r:128 * (r + 1), 128 * (G * c + g):128 * (G * c + g + 1)]
                                  for g in range(G)], axis=0)
            s_all = _dot_nt(qg, kw)
            for g in range(G):
                hq = G * c + g
                s = s_all[128 * g:128 * (g + 1)] + tb_ref[hq, pl.ds(off, 128), :]
                s = jnp.where(valid, s, NEG)
                sink = sink_ref[hq]
                m = jnp.maximum(jnp.max(s, axis=-1, keepdims=True), sink)
                p = jnp.exp(s - m)
                denom = jnp.sum(p, axis=-1, keepdims=True) + jnp.exp(sink - m)
                p = (p / denom).astype(BF16)
                halves[hq] = _dot(p, vw if c == hq % 2 else vsw)
        for mp in range(SWA_HEADS // 2):
            o_ref[128 * r:128 * (r + 1), 128 * mp:128 * (mp + 1)] = jnp.where(
                lane < HD, halves[2 * mp], halves[2 * mp + 1]).astype(BF16)


def _swa_attention(q8, k, v, vsw, sinks, tb, B, T, tq=512):
    nq = T // tq
    kv = pl.BlockSpec((T, 128), lambda b, i: (b, 0))
    return pl.pallas_call(
        functools.partial(_swa_kernel, tq=tq),
        grid=(B, nq),
        in_specs=[pl.BlockSpec(memory_space=pltpu.SMEM),
                  pl.BlockSpec((tq, 1024), lambda b, i: (b * nq + i, 0)),
                  kv, kv, kv,
                  pl.BlockSpec((8, 256, 256), lambda b, i: (0, 0, 0))],
        out_specs=pl.BlockSpec((tq, 512), lambda b, i: (b * nq + i, 0)),
        out_shape=jax.ShapeDtypeStruct((B * T, SWA_HEADS * HD), BF16),
        compiler_params=_cparams(("arbitrary", "arbitrary")),
        name="swa_attention",
    )(sinks, q8, k, v, vsw, tb)


def _outproj_router_kernel(h_ref, oa_ref, ob_ref, wo_ref, g_ref, wr_ref, br_ref,
                           hn_ref, u_ref, ri_ref, rg_ref, cnt_ref, base_ref):
    i = pl.program_id(0)

    @pl.when(i == 0)
    def _():
        base_ref[...] = jnp.zeros_like(base_ref)

    hn = h_ref[...] + _dot(oa_ref[...], wo_ref[0:512, :]) + _dot(ob_ref[...], wo_ref[512:1024, :])
    hn_ref[...] = hn
    u = _rms(hn, g_ref[...])
    u_ref[...] = _pack_halves(u)
    u_hi = u.astype(BF16)
    u_lo = (u - u_hi.astype(F32)).astype(BF16)
    both = _dot_nt(wr_ref[...], u_hi)
    logits = both[:128] + both[128:] + _dot_nt(wr_ref[0:128, :], u_lo) + br_ref[...]
    tm = logits.shape[1]
    sub = lax.broadcasted_iota(jnp.int32, (8, tm), 0).astype(F32)
    big = 1e6
    isg = sub < N_GROUPS
    gl = jnp.where(isg, logits[N_EXP:N_EXP + 8], NEG)
    gmax = jnp.max(gl, axis=0, keepdims=True)
    gsel = jnp.min(jnp.where(gl == gmax, sub, big), axis=0, keepdims=True)
    g_gate = 1.0 / jnp.sum(jnp.where(isg, jnp.exp(gl - gmax), 0.0), axis=0, keepdims=True)
    el = jnp.zeros((EPG, tm), F32)
    for grp in range(N_GROUPS):
        el = jnp.where(gsel == grp, logits[EPG * grp:EPG * (grp + 1)], el)
    m1 = jnp.max(el, axis=0, keepdims=True)
    i1 = jnp.min(jnp.where(el == m1, sub, big), axis=0, keepdims=True)
    el2 = jnp.where(sub == i1, NEG, el)
    m2 = jnp.max(el2, axis=0, keepdims=True)
    i2 = jnp.min(jnp.where(el2 == m2, sub, big), axis=0, keepdims=True)
    e1 = gsel * EPG + i1
    e2 = gsel * EPG + i2
    r = jnp.exp(m2 - m1)
    gate1 = g_gate / (1.0 + r)
    gate2 = g_gate * r / (1.0 + r)
    rows = lax.broadcasted_iota(jnp.int32, (128, tm), 0).astype(F32)
    oh1 = (rows == e1).astype(F32)
    oh2 = (rows == e2).astype(F32)
    oh = oh1 + oh2
    t_row = lax.broadcasted_iota(jnp.int32, (tm, tm), 0)
    t_col = lax.broadcasted_iota(jnp.int32, (tm, tm), 1)
    earlier = jnp.where(t_row < t_col, 1.0, 0.0).astype(BF16)
    prefix = _dot(oh.astype(BF16), earlier) + base_ref[:, 0:1]
    rank1 = jnp.sum(oh1 * prefix, axis=0, keepdims=True)
    rank2 = jnp.sum(oh2 * prefix, axis=0, keepdims=True)
    base_ref[...] = base_ref[...] + jnp.sum(oh, axis=1, keepdims=True)
    cnt_ref[...] = base_ref[...]
    ri_ref[...] = jnp.where(sub == 0, e1, jnp.where(sub == 1, e2, jnp.where(sub == 2, rank1,
                            jnp.where(sub == 3, rank2, 0.0))))
    rg_ref[...] = jnp.where(rows == 0, gate1, jnp.where(rows == 1, gate2, 0.0)).T


def _outproj_router(h, oa, ob, wo, g, wr, br):
    N = h.shape[0]
    full = lambda a: pl.BlockSpec(a.shape, lambda i: (0,) * a.ndim)
    row = lambda w: pl.BlockSpec((TM, w), lambda i: (i, 0))
    return pl.pallas_call(
        _outproj_router_kernel,
        grid=(N // TM,),
        in_specs=[row(D), row(512), row(512), full(wo), full(g), full(wr), full(br)],
        out_specs=[row(D), row(HALF), pl.BlockSpec((8, TM), lambda i: (0, i)), row(128),
                   pl.BlockSpec((128, 128), lambda i: (0, 0))],
        out_shape=[jax.ShapeDtypeStruct((N, D), F32), jax.ShapeDtypeStruct((N, HALF), jnp.int32),
                   jax.ShapeDtypeStruct((8, N), F32), jax.ShapeDtypeStruct((N, 128), F32),
                   jax.ShapeDtypeStruct((128, 128), F32)],
        scratch_shapes=[pltpu.VMEM((128, 128), F32)],
        compiler_params=_cparams(("arbitrary",)),
        name="outproj_router",
    )(h, oa, ob, wo, g, wr, br)


def _moe_kernel(te_ref, tv_ref, x_ref, wg_ref, wu_ref, wd_ref, o_ref, wgu_s, wd_s):
    i = pl.program_id(0)
    valid = tv_ref[i]

    @pl.when((i == 0) | (te_ref[i] != te_ref[jnp.maximum(i - 1, 0)]))
    def _():
        wgu_s[:, :D_EXP] = wg_ref[...].astype(BF16)
        wgu_s[:, D_EXP:] = wu_ref[...].astype(BF16)
        wd_s[...] = wd_ref[...].astype(BF16)

    @pl.when(valid > 0)
    def _():
        rows = lax.broadcasted_iota(jnp.int32, x_ref.shape, 0)
        lo, hi = _unpack_halves(jnp.where(rows < valid, x_ref[...], 0))
        hgu = _dot(lo.astype(BF16), wgu_s[:HALF, :]) + _dot(hi.astype(BF16), wgu_s[HALF:, :])
        hg, hu = hgu[:, :D_EXP], hgu[:, D_EXP:]
        a = hg * (1.0 / (1.0 + jnp.exp(-hg))) * hu
        o_ref[...] = _pack_halves(_dot(a.astype(BF16), wd_s[...]))

    @pl.when(valid <= 0)
    def _():
        o_ref[...] = jnp.zeros_like(o_ref)


def _moe_experts(xs, tile_expert, tile_valid, w_gate, w_up, w_down, layer):
    P = xs.shape[0]
    grid_spec = pltpu.PrefetchScalarGridSpec(
        num_scalar_prefetch=2,
        grid=(P // TM_E,),
        in_specs=[pl.BlockSpec((TM_E, HALF), lambda i, te, tv: (i, 0)),
                  pl.BlockSpec((None, None, D, D_EXP), lambda i, te, tv: (layer, te[i], 0, 0)),
                  pl.BlockSpec((None, None, D, D_EXP), lambda i, te, tv: (layer, te[i], 0, 0)),
                  pl.BlockSpec((None, None, D_EXP, D), lambda i, te, tv: (layer, te[i], 0, 0))],
        out_specs=pl.BlockSpec((TM_E, HALF), lambda i, te, tv: (i, 0)),
        scratch_shapes=[pltpu.VMEM((D, 2 * D_EXP), BF16), pltpu.VMEM((D_EXP, D), BF16)],
    )
    return pl.pallas_call(
        _moe_kernel,
        grid_spec=grid_spec,
        out_shape=jax.ShapeDtypeStruct((P, HALF), jnp.int32),
        compiler_params=_cparams(("arbitrary",)),
        name="moe_experts",
    )(tile_expert, tile_valid, xs, w_gate, w_up, w_down)


def _sc_mesh():
    return plsc.VectorSubcoreMesh(core_axis_name="c", subcore_axis_name="s",
                                  num_cores=SC_CORES, num_subcores=SC_SUBCORES)


def _sc_scatter_rows(src, d1, d2, P):
    N, W = src.shape
    per_w = N // (SC_CORES * SC_SUBCORES)

    @functools.partial(pl.kernel, mesh=_sc_mesh(), out_type=jax.ShapeDtypeStruct((P, W), src.dtype),
                       scratch_types=[pltpu.VMEM((SC_CHUNK,), jnp.int32), pltpu.VMEM((SC_CHUNK, W), src.dtype)],
                       name="sc_scatter_rows")
    def k(src_hbm, d1_hbm, d2_hbm, out_hbm, idx_v, rows_v):
        wid = lax.axis_index("s") * SC_CORES + lax.axis_index("c")

        @pl.loop(0, per_w // SC_CHUNK)
        def _(c):
            off = wid * per_w + c * SC_CHUNK
            pltpu.sync_copy(src_hbm.at[pl.ds(off, SC_CHUNK)], rows_v)
            pltpu.sync_copy(d1_hbm.at[pl.ds(off, SC_CHUNK)], idx_v)
            pltpu.sync_copy(rows_v, out_hbm.at[idx_v])
            pltpu.sync_copy(d2_hbm.at[pl.ds(off, SC_CHUNK)], idx_v)
            pltpu.sync_copy(rows_v, out_hbm.at[idx_v])

    return k(src, d1, d2)


def _sc_gather_rows(table, idx):
    B, W = idx.shape[0], table.shape[1]
    per_w = B // (SC_CORES * SC_SUBCORES)

    @functools.partial(pl.kernel, mesh=_sc_mesh(), out_type=jax.ShapeDtypeStruct((B, W), table.dtype),
                       scratch_types=[pltpu.VMEM((SC_CHUNK,), jnp.int32), pltpu.VMEM((SC_CHUNK, W), table.dtype)],
                       name="sc_gather_rows")
    def k(table_hbm, idx_hbm, out_hbm, idx_v, rows_v):
        wid = lax.axis_index("s") * SC_CORES + lax.axis_index("c")

        @pl.loop(0, per_w // SC_CHUNK)
        def _(c):
            off = wid * per_w + c * SC_CHUNK
            pltpu.sync_copy(idx_hbm.at[pl.ds(off, SC_CHUNK)], idx_v)
            pltpu.sync_copy(table_hbm.at[idx_v], rows_v)
            pltpu.sync_copy(rows_v, out_hbm.at[pl.ds(off, SC_CHUNK)])

    return k(table, idx)


def _dest_kernel(offs_ref, ri_ref, d_ref):
    ri = ri_ref[...]
    base = jnp.zeros(ri.shape, F32)
    for e in range(N_EXP):
        base = jnp.where(ri == e, offs_ref[e].astype(F32), base)
    d_ref[...] = (base + pltpu.roll(ri, 6, 0)).astype(jnp.int32)


def _dispatch_plan(ri, cnt, N):
    counts = cnt[:N_EXP, 0].astype(jnp.int32)
    padded = ((counts + TM_E - 1) // TM_E) * TM_E
    ends = jnp.cumsum(padded)
    offs = ends - padded
    take = lambda t, i: t.at[i].get(mode="promise_in_bounds")
    dest = pl.pallas_call(
        _dest_kernel,
        in_specs=[pl.BlockSpec(memory_space=pltpu.SMEM), pl.BlockSpec((8, N), lambda: (0, 0))],
        out_specs=pl.BlockSpec((8, N), lambda: (0, 0)),
        out_shape=jax.ShapeDtypeStruct((8, N), jnp.int32),
        name="dispatch_rows",
    )(offs, ri)
    d1, d2 = dest[0], dest[1]
    P = 2 * N + N_EXP * TM_E
    tile_start = jnp.arange(P // TM_E, dtype=jnp.int32) * TM_E
    tile_expert = jnp.minimum(jnp.sum((ends[None, :] <= tile_start[:, None]).astype(jnp.int32), axis=1), N_EXP - 1)
    tile_valid = jnp.clip(take(offs + counts, tile_expert) - tile_start, 0, TM_E)
    return d1, d2, P, tile_expert, tile_valid


def _combine(h_ref, y1_ref, y2_ref, rg_ref):
    rg = rg_ref[...]
    lo1, hi1 = _unpack_halves(y1_ref[...])
    lo2, hi2 = _unpack_halves(y2_ref[...])
    g1, g2 = rg[:, 0:1], rg[:, 1:2]
    return h_ref[...] + jnp.concatenate([g1 * lo1 + g2 * lo2, g1 * hi1 + g2 * hi2], axis=1)


ODD_WIDTHS = (1024, 1024, 512, 512)


def _combine_proj_odd_kernel(h_ref, y1_ref, y2_ref, rg_ref, g_ref, win_ref, wvt_ref, ind_ref, hn_ref, vt_ref,
                             vdt_ref, *out_refs):
    hn = _combine(h_ref, y1_ref, y2_ref, rg_ref)
    hn_ref[...] = hn
    u = _rms(hn, g_ref[...]).astype(BF16)
    vt_ref[...] = _values_t(_dot_nt(wvt_ref[0:512, :], u)).astype(BF16)
    vdt_ref[...] = _dot_nt(wvt_ref[512:1024, :], u).astype(BF16)
    qc_ref, kc_ref, qd_ref, kd_ref = out_refs
    qc = _spread_heads(_dot(u, win_ref[:, 0:512]), lambda h: 0)
    kc = _spread_heads(_dot(u, win_ref[:, 512:1024]), lambda h: 0)
    for hh in range(8):
        qc_ref[:, 128 * hh:128 * (hh + 1)] = qc[hh].astype(BF16)
        kc_ref[:, 128 * hh:128 * (hh + 1)] = (kc[hh] + ind_ref[...]).astype(BF16)
    qd_ref[...] = _dot(u, win_ref[:, 1024:1536]).astype(BF16)
    kd_ref[...] = _dot(u, win_ref[:, 1536:2048]).astype(BF16)


def _odd_weights(w_in):
    s = HD ** -0.5 * LOG2E
    scale = np.asarray([s] * 512 + [1.0] * 512 + [s] * 512 + [1.0] * 512, np.float32)
    win = jnp.concatenate([w_in[:, 0:1024], w_in[:, 1536:2560]], axis=1) * jnp.asarray(scale)
    wvt = jnp.concatenate([w_in[:, 1024:1536], w_in[:, 2560:3072]], axis=1).T
    return win.astype(BF16), wvt.astype(BF16)


def _block_indicator(T):
    assert T // MOBA_L <= 64
    ind = np.zeros((T, 128), np.float32)
    ind[np.arange(T), 64 + np.arange(T) // MOBA_L] = 1.0
    return jnp.asarray(ind)


def _y_specs(N):
    return [pl.BlockSpec((TM, HALF), lambda i: (i, 0)), pl.BlockSpec((TM, HALF), lambda i: (i + N // TM, 0))]


def _combine_proj_odd(h, y12, rg, g, win, wvt, ind, T):
    N = h.shape[0]
    nt = T // TM
    full = lambda a: pl.BlockSpec(a.shape, lambda i: (0,) * a.ndim)
    row = lambda w: pl.BlockSpec((TM, w), lambda i: (i, 0))
    return pl.pallas_call(
        _combine_proj_odd_kernel,
        grid=(N // TM,),
        in_specs=[row(D)] + _y_specs(N) + [row(128), full(g), full(win), full(wvt),
                  pl.BlockSpec((TM, 128), lambda i: (i % nt, 0))],
        out_specs=[row(D), pl.BlockSpec((None, 8 * VT_ROWS, TM), lambda i: (i // nt, 0, i % nt)),
                   pl.BlockSpec((None, 512, TM), lambda i: (i // nt, 0, i % nt))]
        + [row(w) for w in ODD_WIDTHS],
        out_shape=[jax.ShapeDtypeStruct((N, D), F32), jax.ShapeDtypeStruct((N // T, 8 * VT_ROWS, T), BF16),
                   jax.ShapeDtypeStruct((N // T, 512, T), BF16)]
        + [jax.ShapeDtypeStruct((N, w), BF16) for w in ODD_WIDTHS],
        compiler_params=_cparams(("arbitrary",)),
        name="combine_proj_odd",
    )(h, y12, y12, rg, g, win, wvt, ind)


def _combine_final_kernel(h_ref, y1_ref, y2_ref, rg_ref, g_ref, o_ref):
    o_ref[...] = _rms(_combine(h_ref, y1_ref, y2_ref, rg_ref), g_ref[...])


def _combine_final(h, y12, rg, g):
    N = h.shape[0]
    row = lambda w: pl.BlockSpec((TM, w), lambda i: (i, 0))
    return pl.pallas_call(
        _combine_final_kernel,
        grid=(N // TM,),
        in_specs=[row(D)] + _y_specs(N) + [row(128), pl.BlockSpec((1, D), lambda i: (0, 0))],
        out_specs=row(D),
        out_shape=jax.ShapeDtypeStruct((N, D), F32),
        compiler_params=_cparams(("arbitrary",)),
        name="combine_final",
    )(h, y12, y12, rg, g)


def _moba_kernel(q_ref, k_ref, vt_ref, tb_ref, o_ref, kmean_ref, qa_ref, s_ref, m_ref, acc_ref, *, nkb):
    i = pl.program_id(2)
    L, tq = MOBA_L, MOBA_TQ

    @pl.when(i == 0)
    def _():
        kmean_ref[...] = jnp.zeros_like(kmean_ref)
        for n in range(nkb):
            kmean_ref[64 + n:65 + n, :] = jnp.sum(k_ref[n * L:(n + 1) * L, :].astype(F32), axis=0,
                                                  keepdims=True) * (1.0 / L)

    lane = lax.broadcasted_iota(jnp.int32, (tq, 128), 1)
    rows = lax.broadcasted_iota(jnp.int32, (128, tq), 0)
    own = 2 * i + jnp.where(lax.broadcasted_iota(jnp.int32, (128, tq), 1) >= L, 1, 0)
    blk = rows - 64
    rows_f = rows.astype(F32)
    for j in range(2):
        qj = q_ref[:, 128 * j:128 * (j + 1)]
        gsc = lax.dot_general(kmean_ref[:, 128 * j:128 * (j + 1)], qj.astype(F32), (((1,), (1,)), ((), ())),
                              preferred_element_type=F32, precision=lax.Precision.HIGHEST)
        g = jnp.where((blk >= 0) & (blk < own), gsc, NEG)
        allowed = blk == own
        for _ in range(MOBA_TOPK):
            mx = jnp.max(g, axis=0, keepdims=True)
            idx = jnp.min(jnp.where(g == mx, rows_f, 1e6), axis=0, keepdims=True)
            pick = rows_f == idx
            allowed = allowed | (pick & (mx > 0.5 * NEG))
            g = jnp.where(pick, 2.0 * NEG, g)
        qa_ref[j] = jnp.where(lane < HD, qj, jnp.where(allowed, 0.0, NEG).T.astype(BF16))

    def bias_of(start, j):
        parts = []
        for c in range(tq // L):
            off = jnp.minimum(i * tq - start - c * L, BIAS_CONST_FROM) + BIAS_PAD
            parts.append(tb_ref[j, :, pl.ds(pl.multiple_of(off, L), tq)])
        return jnp.concatenate(parts, axis=0)

    _causal_attention_t(i, lambda j: qa_ref[j], k_ref, vt_ref, o_ref, s_ref, m_ref, acc_ref, tq=tq, tk=tq,
                        bias_of=bias_of)


def _moba_attention(q, k, vt, tb, B, T):
    tq = MOBA_TQ
    nq = T // tq
    return pl.pallas_call(
        functools.partial(_moba_kernel, nkb=T // MOBA_L),
        grid=(B, 4, nq),
        in_specs=[pl.BlockSpec((tq, 256), lambda b, p, i: (b * nq + i, p)),
                  pl.BlockSpec((T, 256), lambda b, p, i: (b, p)),
                  pl.BlockSpec((None, 2 * VT_ROWS, T), lambda b, p, i: (b, p, 0)),
                  pl.BlockSpec((2, MOBA_L, BIAS_ROWS), lambda b, p, i: (p, 0, 0))],
        out_specs=pl.BlockSpec((tq, 128), lambda b, p, i: (b * nq + i, p)),
        out_shape=jax.ShapeDtypeStruct((B * T, 512), BF16),
        scratch_shapes=[pltpu.VMEM((128, 256), F32), pltpu.VMEM((2, tq, 128), BF16)]
        + _attention_t_scratch(tq, tq),
        compiler_params=_cparams(("arbitrary", "arbitrary", "arbitrary")),
        name="moba_attention",
    )(q, k, vt, tb)


def _sb_kernel(q_ref, k_ref, vt_ref, o_ref, *, tq, pairs):
    i = pl.program_id(2)
    lane = lax.broadcasted_iota(jnp.int32, (tq, 128), 1)
    qs = []
    for p in range(pairs):
        q_pair = q_ref[:, 128 * p:128 * (p + 1)]
        qs += [jnp.where(lane < HD, q_pair, 0).astype(BF16), jnp.where(lane >= HD, q_pair, 0).astype(BF16)]
    key = lax.broadcasted_iota(jnp.int32, (tq, tq), 0)
    qry = lax.broadcasted_iota(jnp.int32, (tq, tq), 1)
    past = key < qry
    suffix = jnp.where(qry >= key, -1.0, 0.0).astype(BF16)
    heads = range(2 * pairs)

    def step(kb, carry, boundary):
        start = pl.multiple_of(kb * tq, tq)
        zs = [_dot_nt(k_ref[pl.ds(start, tq), 128 * (h // 2):128 * (h // 2 + 1)], qs[h]) for h in heads]
        parts = []
        for z in zs:
            soft = jnp.maximum(z, 0.0) + jnp.log2(1.0 + jnp.exp2(-jnp.abs(z)))
            if boundary:
                soft = jnp.where(past, soft, 0.0)
            hi = soft.astype(BF16)
            parts.append((hi, (soft - hi.astype(F32)).astype(BF16)))
        sums = [_dot(suffix, hi) + _dot(suffix, lo) for hi, lo in parts]
        ws = []
        for h in heads:
            w = jnp.exp2(zs[h] + (carry[h][0] + sums[h]))
            ws.append((jnp.where(past, w, 0.0) if boundary else w).astype(BF16))
        return tuple((carry[h][0] + sums[h][0:1], carry[h][1]
                      + _dot(vt_ref[128 * (h // 2):128 * (h // 2 + 1), pl.ds(start, tq)], ws[h])) for h in heads)

    def live(carry):
        top = carry[0][0]
        for c, _ in carry[1:]:
            top = jnp.maximum(top, c)
        return (jnp.max(top) > SB_DONE).astype(jnp.int32)

    def body(state):
        t, _, carry = state
        carry = step(i - 1 - t, carry, False)
        return t + 1, live(carry), carry

    init = tuple((jnp.zeros((1, tq), F32), jnp.zeros((128, tq), F32)) for _ in range(2 * pairs))
    carry = step(i, init, True)
    _, _, carry = lax.while_loop(lambda st: (st[0] < i) & (st[1] > 0), body, (jnp.int32(0), live(carry), carry))
    rows = lax.broadcasted_iota(jnp.int32, (128, tq), 0)
    for p in range(pairs):
        ot = jnp.where(rows < HD, carry[2 * p][1], carry[2 * p + 1][1])
        o_ref[:, 128 * p:128 * (p + 1)] = ot.T.astype(BF16)


def _sb_attention(q, k, vt, B, T, tq=256, pairs=4):
    nq = T // tq
    w = 128 * pairs
    return pl.pallas_call(
        functools.partial(_sb_kernel, tq=tq, pairs=pairs),
        grid=(B, 4 // pairs, nq),
        in_specs=[pl.BlockSpec((tq, w), lambda b, p, i: (b * nq + i, p)),
                  pl.BlockSpec((T, w), lambda b, p, i: (b, p)),
                  pl.BlockSpec((None, w, T), lambda b, p, i: (b, p, 0))],
        out_specs=pl.BlockSpec((tq, w), lambda b, p, i: (b * nq + i, p)),
        out_shape=jax.ShapeDtypeStruct((B * T, 512), BF16),
        compiler_params=_cparams(("arbitrary", "arbitrary", "arbitrary")),
        name="sb_attention",
    )(q, k, vt)


def _router_weights(w_group, b_group, w_router, b_router):
    pad = 128 - N_EXP - N_GROUPS
    wr = jnp.concatenate([w_router, w_group, jnp.zeros((D, pad), F32)], axis=1).T
    hi = wr.astype(BF16)
    lo = (wr - hi.astype(F32)).astype(BF16)
    br = jnp.concatenate([b_router, b_group, jnp.zeros((pad,), F32)])[:, None]
    return jnp.concatenate([hi, lo], axis=0), br


def _ffn(h, oa, ob, wo, g, w_group, b_group, w_router, b_router, w_gate, w_up, w_down, layer):
    wr, br = _router_weights(w_group, b_group, w_router, b_router)
    hn, u, ri, rg, cnt = _outproj_router(h, oa, ob, wo.astype(BF16), g[None, :], wr, br)
    d1, d2, P, tile_expert, tile_valid = _dispatch_plan(ri, cnt, h.shape[0])
    xs = _sc_scatter_rows(u, d1, d2, P)
    ys = _moe_experts(xs, tile_expert, tile_valid, w_gate, w_up, w_down, layer)
    return hn, _sc_gather_rows(ys, jnp.concatenate([d1, d2])), rg


def kernel(x, norm_mix, norm_ffn, norm_final, rel_bias_table, w_in_even, g_mla_q, g_mla_kv, w_mla_qb, w_mla_kvb, swa_sinks, w_out_even, w_in_odd, w_out_odd, moe_w_group, moe_b_group, moe_w_router, moe_b_router, moe_w_gate, moe_w_up, moe_w_down):
    B, T, _ = x.shape
    h = x.reshape(B * T, D)
    tb_swa, tb_moba = _bias_tables(rel_bias_table)

    win, wqa, wqb, wk, wv = _even_weights(w_in_even[0], w_mla_qb[0], w_mla_kvb[0])
    q, k, v, qs8, ks, vs, vsw = _proj_even(h, norm_mix[0][None, :], win, g_mla_q[0][None, :], g_mla_kv[0][None, :],
                                           wqa, wqb, wk, wv, _rope_tables(T), T)
    oa = _mla_attention(q, k, v, B, T)
    ob = _swa_attention(qs8, ks, vs, vsw, swa_sinks[0], tb_swa, B, T)
    h, y12, rg = _ffn(h, oa, ob, w_out_even[0], norm_ffn[0], moe_w_group[0], moe_b_group[0], moe_w_router[0],
                      moe_b_router[0], moe_w_gate, moe_w_up, moe_w_down, 0)

    h, vct, vdt, qc, kc, qd, kd = _combine_proj_odd(h, y12, rg, norm_mix[1][None, :], *_odd_weights(w_in_odd[0]),
                                                   _block_indicator(T), T)
    oc = _moba_attention(qc, kc, vct, tb_moba, B, T)
    od = _sb_attention(qd, kd, vdt, B, T)
    h, y12, rg = _ffn(h, oc, od, w_out_odd[0], norm_ffn[1], moe_w_group[1], moe_b_group[1], moe_w_router[1],
                      moe_b_router[1], moe_w_gate, moe_w_up, moe_w_down, 1)
    out = _combine_final(h, y12, rg, norm_final[None, :])
    return out.reshape(B, T, D)
```

```python
import functools
import math

import numpy as np
import jax
import jax.numpy as jnp
from jax import lax
from jax.experimental import pallas as pl
from jax.experimental.pallas import tpu as pltpu
from jax.experimental.pallas import tpu_sc as plsc

F32 = jnp.float32
BF16 = jnp.bfloat16
NEG = -1e30
EPS = 1e-6

D = 1024
HD = 64
LANES = 128
MLA_HEADS, MLA_QR, MLA_KVR, MLA_NOPE, MLA_ROPE, MLA_V = 8, 256, 128, 64, 32, 64
ROPE_THETA = 10000.0
SWA_HEADS, SWA_KV, SWA_WIN = 8, 2, 128
MOBA_L, MOBA_TOPK = 256, 3
REL_BUCKETS, REL_MAX = 32, 2048
N_GROUPS, EPG, N_EXP, D_EXP = 4, 8, 32, 256
BIAS_CONST_FROM = 1792
MOBA_TQ = 2 * MOBA_L
BIAS_PAD = MOBA_L
BIAS_ROWS = BIAS_PAD + BIAS_CONST_FROM + MOBA_TQ
SB_DONE = -160.0
TM = 512
TM_E = 512
VMEM_LIMIT = 56 * 1024 * 1024
SC_CORES, SC_SUBCORES = 2, 16
SC_CHUNK = 128
HALF = D // 2
VT_ROWS = HD + 16
LOG2E = math.log2(math.e)


def _pack_halves(x):
    lo = pltpu.bitcast(x[:, :HALF].astype(BF16).astype(F32), jnp.uint32) >> 16
    hi = pltpu.bitcast(x[:, HALF:].astype(BF16).astype(F32), jnp.uint32) & jnp.uint32(0xFFFF0000)
    return pltpu.bitcast(lo | hi, jnp.int32)


def _unpack_halves(w):
    w = pltpu.bitcast(w, jnp.uint32)
    return pltpu.bitcast(w << 16, F32), pltpu.bitcast(w & jnp.uint32(0xFFFF0000), F32)


def _cparams(sem):
    return pltpu.CompilerParams(dimension_semantics=sem, vmem_limit_bytes=VMEM_LIMIT)


def _dot(a, b):
    return jnp.dot(a, b, preferred_element_type=F32)


def _dot_nt(a, b):
    return lax.dot_general(a, b, (((1,), (1,)), ((), ())), preferred_element_type=F32)


def _rms(x, g):
    return x * lax.rsqrt(jnp.mean(x * x, axis=-1, keepdims=True) + EPS) * g


def _values_t(vt):
    ones = jnp.ones((VT_ROWS - HD, vt.shape[1]), vt.dtype)
    return jnp.concatenate([blk for h in range(8) for blk in (vt[HD * h:HD * (h + 1)], ones)], axis=0)


def _spread_heads(x, lane_of):
    lane = lax.broadcasted_iota(jnp.int32, (x.shape[0], 128), 1)
    out = []
    for h in range(8):
        pair = x[:, 128 * (h // 2):128 * (h // 2 + 1)]
        if HD * (h % 2) != lane_of(h):
            pair = pltpu.roll(pair, HD, 1)
        out.append(jnp.where((lane >= lane_of(h)) & (lane < lane_of(h) + HD), pair, 0.0))
    return out


def _flash_update_t(st, vt, m, acc):
    m_new = jnp.maximum(m, jnp.max(st, axis=0, keepdims=True))
    alpha = jnp.exp2(m - m_new)
    pt = jnp.exp2(st - m_new).astype(BF16)
    return m_new, alpha * acc + _dot(vt, pt)


def _flash_update(s, v, m, l, acc):
    m_new = jnp.maximum(m, jnp.max(s, axis=-1, keepdims=True))
    alpha = jnp.exp(m - m_new)
    p = jnp.exp(s - m_new)
    l = alpha * l + jnp.sum(p, axis=-1, keepdims=True)
    acc = alpha * acc + _dot(p.astype(BF16), v)
    return m_new, l, acc


def _t5_bucket_np(dist):
    n = np.maximum(dist, 0)
    max_exact = REL_BUCKETS // 2
    nf = np.maximum(n, 1).astype(np.float32)
    large = max_exact + (np.log(nf / np.float32(max_exact)) / np.float32(math.log(REL_MAX / max_exact))
                         * np.float32(REL_BUCKETS - max_exact)).astype(np.int32)
    large = np.minimum(large, REL_BUCKETS - 1)
    return np.where(n < max_exact, n, large).astype(np.int32)


def _bias_kernel(table_ref, bucket_ref, out_ref, *, scale):
    b = bucket_ref[...]
    for h in range(8):
        acc = jnp.zeros(b.shape, F32)
        for k in range(REL_BUCKETS):
            acc = jnp.where(b == k, table_ref[k, h] * scale, acc)
        out_ref[h] = acc


def _bias_lookup(rel_table, dist, scale, name):
    R, C = dist.shape
    return pl.pallas_call(
        functools.partial(_bias_kernel, scale=scale),
        grid=(C // 256,),
        in_specs=[pl.BlockSpec(memory_space=pltpu.SMEM),
                  pl.BlockSpec((R, 256), lambda m: (0, m))],
        out_specs=pl.BlockSpec((8, R, 256), lambda m: (0, 0, m)),
        out_shape=jax.ShapeDtypeStruct((8, R, C), F32),
        compiler_params=_cparams(("arbitrary",)),
        name=name,
    )(rel_table, jnp.asarray(_t5_bucket_np(dist)))


def _bias_tables(rel_table):
    swa = _bias_lookup(rel_table, np.arange(256)[:, None] - np.arange(256)[None, :], 1.0, "bias_swa")
    moba = _bias_lookup(rel_table, np.arange(BIAS_ROWS)[None, :] - BIAS_PAD - np.arange(MOBA_L)[:, None], LOG2E,
                        "bias_moba")
    return swa, moba


def _proj_even_kernel(h_ref, g_ref, win_ref, gq_ref, gkv_ref, wqa_ref, wqb_ref, wk_ref, wv_ref,
                      cq_ref, sq_ref, ck_ref, sk_ref,
                      q_ref, k_ref, v_ref, qs_ref, ks_ref, vs_ref, vsw_ref):
    u = _rms(h_ref[...], g_ref[...]).astype(BF16)
    nq = _rms(_dot(u, win_ref[:, 0:256]), gq_ref[...]).astype(BF16)
    qa = _dot(nq, wqa_ref[...])
    qb = _dot(nq, wqb_ref[...])
    cq, sq = cq_ref[...], sq_ref[...]
    for hh in range(MLA_HEADS):
        sl = slice(128 * hh, 128 * (hh + 1))
        q_ref[:, sl] = (qa[:, sl] * cq + qb[:, sl] * sq).astype(BF16)
    nkv = _rms(_dot(u, win_ref[:, 256:384]), gkv_ref[...]).astype(BF16)
    kn = _dot(nkv, wk_ref[...])
    v_ref[...] = _values_t(_dot_nt(wv_ref[...], nkv)).astype(BF16)
    kr = _dot(u, win_ref[:, 1280:1408]) * ck_ref[...] + _dot(u, win_ref[:, 1408:1536]) * sk_ref[...]
    for hh in range(MLA_HEADS):
        sl = slice(128 * hh, 128 * (hh + 1))
        k_ref[:, sl] = (kn[:, sl] + kr).astype(BF16)
    qs = _spread_heads(_dot(u, win_ref[:, 384:896]), lambda hq: HD * (hq // (SWA_HEADS // SWA_KV)))
    for hq in range(SWA_HEADS):
        qs_ref[:, 128 * hq:128 * (hq + 1)] = qs[hq].astype(BF16)
    ks_ref[...] = _dot(u, win_ref[:, 896:1024]).astype(BF16)
    vs_ref[...] = _dot(u, win_ref[:, 1024:1152]).astype(BF16)
    vsw_ref[...] = _dot(u, win_ref[:, 1152:1280]).astype(BF16)


def _cols(w, idx, scale=None):
    idx = np.asarray(idx)
    cuts = [0] + [n for n in range(1, len(idx)) if (idx[n] < 0) != (idx[n - 1] < 0)
                  or (idx[n] >= 0 and idx[n] != idx[n - 1] + 1)] + [len(idx)]
    parts = [jnp.zeros((w.shape[0], b - a), w.dtype) if idx[a] < 0 else w[:, idx[a]:idx[a] + b - a]
             for a, b in zip(cuts[:-1], cuts[1:])]
    out = jnp.concatenate(parts, axis=1)
    if scale is not None:
        out = out * scale
    return out


def _even_weights(w_in, w_qb, w_kvb):
    o_cq, o_ckv, o_kr = 0, MLA_QR, MLA_QR + MLA_KVR
    o_qs = o_kr + MLA_ROPE
    o_ks = o_qs + SWA_HEADS * HD
    o_vs = o_ks + SWA_KV * HD
    idx = list(range(o_cq, o_cq + MLA_QR)) + list(range(o_ckv, o_ckv + MLA_KVR))
    scale = [1.0] * len(idx)
    idx += list(range(o_qs, o_qs + SWA_HEADS * HD))
    scale += [HD ** -0.5] * (SWA_HEADS * HD)
    idx += list(range(o_ks, o_ks + 128))
    idx += list(range(o_vs, o_vs + 128))
    idx += list(range(o_vs + 64, o_vs + 128)) + list(range(o_vs, o_vs + 64))
    scale += [1.0] * 384
    half = MLA_ROPE // 2
    idx += [-1] * 64 + list(range(o_kr, o_kr + MLA_ROPE)) + [-1] * 32
    idx += [-1] * 64 + list(range(o_kr + half, o_kr + MLA_ROPE)) + list(range(o_kr, o_kr + half)) + [-1] * 32
    scale += [1.0] * 256
    win = _cols(w_in, idx, jnp.asarray(np.asarray(scale, np.float32))).astype(BF16)
    qw = MLA_NOPE + MLA_ROPE
    ia, ib, ik, iv = [], [], [], []
    for h in range(MLA_HEADS):
        ia += list(range(qw * h, qw * h + qw)) + [-1] * 32
        ib += [-1] * 64 + list(range(qw * h + MLA_NOPE + half, qw * h + qw)) \
            + list(range(qw * h + MLA_NOPE, qw * h + MLA_NOPE + half)) + [-1] * 32
        ik += list(range(128 * h, 128 * h + MLA_NOPE)) + [-1] * 64
        iv += list(range(128 * h + MLA_NOPE, 128 * h + 128))
    return (win, _cols(w_qb, ia).astype(BF16), _cols(w_qb, ib).astype(BF16),
            _cols(w_kvb, ik).astype(BF16), _cols(w_kvb, iv).T.astype(BF16))


def _rope_tables(T):
    half = MLA_ROPE // 2
    freqs = ROPE_THETA ** (-jnp.arange(0, MLA_ROPE, 2, dtype=F32) / MLA_ROPE)
    ang = jnp.arange(T, dtype=F32)[:, None] * freqs[None, :]
    cos, sin = jnp.cos(ang), jnp.sin(ang)
    z = lambda n: jnp.zeros((T, n), F32)
    scale = (MLA_NOPE + MLA_ROPE) ** -0.5 * LOG2E
    ck = jnp.concatenate([z(64), cos, cos, z(32)], axis=1)
    sk = jnp.concatenate([z(64), -sin, sin, z(32)], axis=1)
    cq = jnp.concatenate([jnp.ones((T, 64), F32), cos, cos, z(32)], axis=1) * scale
    sq = sk * scale
    return cq, sq, ck, sk


def _proj_even(h, g, win, gq, gkv, wqa, wqb, wk, wv, tables, T):
    N = h.shape[0]
    nt = T // TM
    full = lambda a: pl.BlockSpec(a.shape, lambda i: (0,) * a.ndim)
    tab = pl.BlockSpec((TM, 128), lambda i: (i % nt, 0))
    row = lambda w: pl.BlockSpec((TM, w), lambda i: (i, 0))
    widths = (1024, 1024, None, 1024, 128, 128, 128)
    vt_spec = pl.BlockSpec((None, 8 * VT_ROWS, TM), lambda i: (i // nt, 0, i % nt))
    vt_shape = jax.ShapeDtypeStruct((N // T, 8 * VT_ROWS, T), BF16)
    return pl.pallas_call(
        _proj_even_kernel,
        grid=(N // TM,),
        in_specs=[row(D), full(g), full(win), full(gq), full(gkv), full(wqa), full(wqb), full(wk), full(wv),
                  tab, tab, tab, tab],
        out_specs=[vt_spec if w is None else row(w) for w in widths],
        out_shape=[vt_shape if w is None else jax.ShapeDtypeStruct((N, w), BF16) for w in widths],
        compiler_params=_cparams(("arbitrary",)),
        name="proj_even",
    )(h, g, win, gq, gkv, wqa, wqb, wk, wv, *tables)


def _finish_t(accs):
    ot = jnp.concatenate([acc[:HD] / acc[HD:HD + 1] for acc in accs], axis=0)
    return ot.T.astype(BF16)


def _causal_attention_t(i, q_of, k_ref, vt_ref, o_ref, s_ref, m_ref, acc_ref, *, tq, tk, bias_of=None):
    n_full = i * (tq // tk)
    key = lax.broadcasted_iota(jnp.int32, (tk, tq), 0)
    qry = i * tq + lax.broadcasted_iota(jnp.int32, (tk, tq), 1)

    def scores_to(slot, start, boundary):
        start = pl.multiple_of(start, tk)
        for j in range(2):
            st = _dot_nt(k_ref[pl.ds(start, tk), 128 * j:128 * (j + 1)], q_of(j))
            if bias_of is not None:
                st = st + bias_of(start, j)
            if boundary:
                st = jnp.where(start + key <= qry, st, NEG)
            s_ref[slot, j] = st

    def update_from(slot, start):
        start = pl.multiple_of(start, tk)
        m_new = [jnp.maximum(m_ref[j], jnp.max(s_ref[slot, j], axis=0, keepdims=True)) for j in range(2)]
        pts = [jnp.exp2(s_ref[slot, j] - m_new[j]).astype(BF16) for j in range(2)]
        for j in range(2):
            acc_ref[j] = (jnp.exp2(m_ref[j] - m_new[j]) * acc_ref[j]
                          + _dot(vt_ref[VT_ROWS * j:VT_ROWS * (j + 1), pl.ds(start, tk)], pts[j]))
            m_ref[j] = m_new[j]

    m_ref[...] = jnp.full(m_ref.shape, NEG, F32)
    acc_ref[...] = jnp.zeros(acc_ref.shape, F32)
    for d in range(tq // tk - 1):
        scores_to(0, (n_full + d) * tk, True)
        update_from(0, (n_full + d) * tk)
    first = (n_full + tq // tk - 1) * tk
    scores_to(0, first, True)

    def block_start(k):
        return jnp.where(k == 0, first, (k - 1) * tk)

    def pair(k):
        scores_to(1, block_start(k + 1), False)
        update_from(0, block_start(k))
        scores_to(0, block_start(k + 2), False)
        update_from(1, block_start(k + 1))

    def quad(p, _):
        pair(4 * p)
        pair(4 * p + 2)
        return 0

    lax.fori_loop(0, n_full // 4, quad, 0)

    @pl.when(n_full % 4 >= 2)
    def _():
        pair((n_full // 4) * 4)

    @pl.when(n_full % 2 == 1)
    def _():
        scores_to(1, block_start(n_full), False)
        update_from(0, block_start(n_full - 1))
        update_from(1, block_start(n_full))

    @pl.when(n_full % 2 == 0)
    def _():
        update_from(0, block_start(n_full))

    o_ref[...] = _finish_t([acc_ref[j] for j in range(2)])


def _attention_t_scratch(tq, tk):
    return [pltpu.VMEM((2, 2, tk, tq), F32), pltpu.VMEM((2, 1, tq), F32), pltpu.VMEM((2, VT_ROWS, tq), F32)]


def _mla_kernel(q_ref, k_ref, vt_ref, o_ref, s_ref, m_ref, acc_ref, *, tq, tk):
    _causal_attention_t(pl.program_id(2), lambda j: q_ref[:, 128 * j:128 * (j + 1)], k_ref, vt_ref, o_ref,
                        s_ref, m_ref, acc_ref, tq=tq, tk=tk)


def _mla_attention(q, k, vt, B, T, tq=512, tk=512):
    nq = T // tq
    return pl.pallas_call(
        functools.partial(_mla_kernel, tq=tq, tk=tk),
        grid=(B, MLA_HEADS // 2, nq),
        in_specs=[pl.BlockSpec((tq, 256), lambda b, p, i: (b * nq + i, p)),
                  pl.BlockSpec((T, 256), lambda b, p, i: (b, p)),
                  pl.BlockSpec((None, 2 * VT_ROWS, T), lambda b, p, i: (b, p, 0))],
        out_specs=pl.BlockSpec((tq, 128), lambda b, p, i: (b * nq + i, p)),
        out_shape=jax.ShapeDtypeStruct((B * T, MLA_HEADS * MLA_V), BF16),
        scratch_shapes=_attention_t_scratch(tq, tk),
        compiler_params=_cparams(("arbitrary", "arbitrary", "arbitrary")),
        name="mla_attention",
    )(q, k, vt)


def _swa_kernel(sink_ref, q_ref, k_ref, v_ref, vsw_ref, tb_ref, o_ref, *, tq):
    i = pl.program_id(1)
    G = SWA_HEADS // SWA_KV
    qi = lax.broadcasted_iota(jnp.int32, (128, 256), 0)
    kj = lax.broadcasted_iota(jnp.int32, (128, 256), 1)
    lane = lax.broadcasted_iota(jnp.int32, (128, 128), 1)
    for r in range(tq // 128):
        qstart = i * tq + 128 * r
        kstart = pl.multiple_of(jnp.maximum(qstart - 128, 0), 128)
        off = pl.multiple_of(qstart - kstart, 128)
        dist = off + qi - kj
        valid = (dist >= 0) & (dist < SWA_WIN)
        kw = k_ref[pl.ds(kstart, 256), :]
        vw = v_ref[pl.ds(kstart, 256), :]
        vsw = vsw_ref[pl.ds(kstart, 256), :]
        halves = [None] * SWA_HEADS
        for c in range(SWA_KV):
            qg = jnp.concatenate([q_ref[128 * r:128 * (r + 1), 128 * (G * c + g):128 * (G * c + g + 1)]
                                  for g in range(G)], axis=0)
            s_all = _dot_nt(qg, kw)
            for g in range(G):
                hq = G * c + g
                s = s_all[128 * g:128 * (g + 1)] + tb_ref[hq, pl.ds(off, 128), :]
                s = jnp.where(valid, s, NEG)
                sink = sink_ref[hq]
                m = jnp.maximum(jnp.max(s, axis=-1, keepdims=True), sink)
                p = jnp.exp(s - m)
                denom = jnp.sum(p, axis=-1, keepdims=True) + jnp.exp(sink - m)
                p = (p / denom).astype(BF16)
                halves[hq] = _dot(p, vw if c == hq % 2 else vsw)
        for mp in range(SWA_HEADS // 2):
            o_ref[128 * r:128 * (r + 1), 128 * mp:128 * (mp + 1)] = jnp.where(
                lane < HD, halves[2 * mp], halves[2 * mp + 1]).astype(BF16)


def _swa_attention(q8, k, v, vsw, sinks, tb, B, T, tq=512):
    nq = T // tq
    kv = pl.BlockSpec((T, 128), lambda b, i: (b, 0))
    return pl.pallas_call(
        functools.partial(_swa_kernel, tq=tq),
        grid=(B, nq),
        in_specs=[pl.BlockSpec(memory_space=pltpu.SMEM),
                  pl.BlockSpec((tq, 1024), lambda b, i: (b * nq + i, 0)),
                  kv, kv, kv,
                  pl.BlockSpec((8, 256, 256), lambda b, i: (0, 0, 0))],
        out_specs=pl.BlockSpec((tq, 512), lambda b, i: (b * nq + i, 0)),
        out_shape=jax.ShapeDtypeStruct((B * T, SWA_HEADS * HD), BF16),
        compiler_params=_cparams(("arbitrary", "arbitrary")),
        name="swa_attention",
    )(sinks, q8, k, v, vsw, tb)


def _outproj_router_kernel(h_ref, oa_ref, ob_ref, wo_ref, g_ref, wr_ref, br_ref,
                           hn_ref, u_ref, ri_ref, rg_ref, cnt_ref, base_ref):
    i = pl.program_id(0)

    @pl.when(i == 0)
    def _():
        base_ref[...] = jnp.zeros_like(base_ref)

    hn = h_ref[...] + _dot(oa_ref[...], wo_ref[0:512, :]) + _dot(ob_ref[...], wo_ref[512:1024, :])
    hn_ref[...] = hn
    u = _rms(hn, g_ref[...])
    u_ref[...] = _pack_halves(u)
    u_hi = u.astype(BF16)
    u_lo = (u - u_hi.astype(F32)).astype(BF16)
    both = _dot_nt(wr_ref[...], u_hi)
    logits = both[:128] + both[128:] + _dot_nt(wr_ref[0:128, :], u_lo) + br_ref[...]
    tm = logits.shape[1]
    sub = lax.broadcasted_iota(jnp.int32, (8, tm), 0).astype(F32)
    big = 1e6
    isg = sub < N_GROUPS
    gl = jnp.where(isg, logits[N_EXP:N_EXP + 8], NEG)
    gmax = jnp.max(gl, axis=0, keepdims=True)
    gsel = jnp.min(jnp.where(gl == gmax, sub, big), axis=0, keepdims=True)
    g_gate = 1.0 / jnp.sum(jnp.where(isg, jnp.exp(gl - gmax), 0.0), axis=0, keepdims=True)
    el = jnp.zeros((EPG, tm), F32)
    for grp in range(N_GROUPS):
        el = jnp.where(gsel == grp, logits[EPG * grp:EPG * (grp + 1)], el)
    m1 = jnp.max(el, axis=0, keepdims=True)
    i1 = jnp.min(jnp.where(el == m1, sub, big), axis=0, keepdims=True)
    el2 = jnp.where(sub == i1, NEG, el)
    m2 = jnp.max(el2, axis=0, keepdims=True)
    i2 = jnp.min(jnp.where(el2 == m2, sub, big), axis=0, keepdims=True)
    e1 = gsel * EPG + i1
    e2 = gsel * EPG + i2
    r = jnp.exp(m2 - m1)
    gate1 = g_gate / (1.0 + r)
    gate2 = g_gate * r / (1.0 + r)
    rows = lax.broadcasted_iota(jnp.int32, (128, tm), 0).astype(F32)
    oh1 = (rows == e1).astype(F32)
    oh2 = (rows == e2).astype(F32)
    oh = oh1 + oh2
    t_row = lax.broadcasted_iota(jnp.int32, (tm, tm), 0)
    t_col = lax.broadcasted_iota(jnp.int32, (tm, tm), 1)
    earlier = jnp.where(t_row < t_col, 1.0, 0.0).astype(BF16)
    prefix = _dot(oh.astype(BF16), earlier) + base_ref[:, 0:1]
    rank1 = jnp.sum(oh1 * prefix, axis=0, keepdims=True)
    rank2 = jnp.sum(oh2 * prefix, axis=0, keepdims=True)
    base_ref[...] = base_ref[...] + jnp.sum(oh, axis=1, keepdims=True)
    cnt_ref[...] = base_ref[...]
    ri_ref[...] = jnp.where(sub == 0, e1, jnp.where(sub == 1, e2, jnp.where(sub == 2, rank1,
                            jnp.where(sub == 3, rank2, 0.0))))
    rg_ref[...] = jnp.where(rows == 0, gate1, jnp.where(rows == 1, gate2, 0.0)).T


def _outproj_router(h, oa, ob, wo, g, wr, br):
    N = h.shape[0]
    full = lambda a: pl.BlockSpec(a.shape, lambda i: (0,) * a.ndim)
    row = lambda w: pl.BlockSpec((TM, w), lambda i: (i, 0))
    return pl.pallas_call(
        _outproj_router_kernel,
        grid=(N // TM,),
        in_specs=[row(D), row(512), row(512), full(wo), full(g), full(wr), full(br)],
        out_specs=[row(D), row(HALF), pl.BlockSpec((8, TM), lambda i: (0, i)), row(128),
                   pl.BlockSpec((128, 128), lambda i: (0, 0))],
        out_shape=[jax.ShapeDtypeStruct((N, D), F32), jax.ShapeDtypeStruct((N, HALF), jnp.int32),
                   jax.ShapeDtypeStruct((8, N), F32), jax.ShapeDtypeStruct((N, 128), F32),
                   jax.ShapeDtypeStruct((128, 128), F32)],
        scratch_shapes=[pltpu.VMEM((128, 128), F32)],
        compiler_params=_cparams(("arbitrary",)),
        name="outproj_router",
    )(h, oa, ob, wo, g, wr, br)


def _moe_kernel(te_ref, tv_ref, x_ref, wg_ref, wu_ref, wd_ref, o_ref, wgu_s, wd_s):
    i = pl.program_id(0)
    valid = tv_ref[i]

    @pl.when((i == 0) | (te_ref[i] != te_ref[jnp.maximum(i - 1, 0)]))
    def _():
        wgu_s[:, :D_EXP] = wg_ref[...].astype(BF16)
        wgu_s[:, D_EXP:] = wu_ref[...].astype(BF16)
        wd_s[...] = wd_ref[...].astype(BF16)

    @pl.when(valid > 0)
    def _():
        rows = lax.broadcasted_iota(jnp.int32, x_ref.shape, 0)
        lo, hi = _unpack_halves(jnp.where(rows < valid, x_ref[...], 0))
        hgu = _dot(lo.astype(BF16), wgu_s[:HALF, :]) + _dot(hi.astype(BF16), wgu_s[HALF:, :])
        hg, hu = hgu[:, :D_EXP], hgu[:, D_EXP:]
        a = hg * (1.0 / (1.0 + jnp.exp(-hg))) * hu
        o_ref[...] = _pack_halves(_dot(a.astype(BF16), wd_s[...]))

    @pl.when(valid <= 0)
    def _():
        o_ref[...] = jnp.zeros_like(o_ref)


def _moe_experts(xs, tile_expert, tile_valid, w_gate, w_up, w_down, layer):
    P = xs.shape[0]
    grid_spec = pltpu.PrefetchScalarGridSpec(
        num_scalar_prefetch=2,
        grid=(P // TM_E,),
        in_specs=[pl.BlockSpec((TM_E, HALF), lambda i, te, tv: (i, 0)),
                  pl.BlockSpec((None, None, D, D_EXP), lambda i, te, tv: (layer, te[i], 0, 0)),
                  pl.BlockSpec((None, None, D, D_EXP), lambda i, te, tv: (layer, te[i], 0, 0)),
                  pl.BlockSpec((None, None, D_EXP, D), lambda i, te, tv: (layer, te[i], 0, 0))],
        out_specs=pl.BlockSpec((TM_E, HALF), lambda i, te, tv: (i, 0)),
        scratch_shapes=[pltpu.VMEM((D, 2 * D_EXP), BF16), pltpu.VMEM((D_EXP, D), BF16)],
    )
    return pl.pallas_call(
        _moe_kernel,
        grid_spec=grid_spec,
        out_shape=jax.ShapeDtypeStruct((P, HALF), jnp.int32),
        compiler_params=_cparams(("arbitrary",)),
        name="moe_experts",
    )(tile_expert, tile_valid, xs, w_gate, w_up, w_down)


def _sc_mesh():
    return plsc.VectorSubcoreMesh(core_axis_name="c", subcore_axis_name="s",
                                  num_cores=SC_CORES, num_subcores=SC_SUBCORES)


def _sc_scatter_rows(src, d1, d2, P):
    N, W = src.shape
    per_w = N // (SC_CORES * SC_SUBCORES)

    @functools.partial(pl.kernel, mesh=_sc_mesh(), out_type=jax.ShapeDtypeStruct((P, W), src.dtype),
                       scratch_types=[pltpu.VMEM((SC_CHUNK,), jnp.int32), pltpu.VMEM((SC_CHUNK,), jnp.int32),
                                      pltpu.VMEM((SC_CHUNK, W), src.dtype),
                                      pltpu.SemaphoreType.DMA, pltpu.SemaphoreType.DMA],
                       name="sc_scatter_rows")
    def k(src_hbm, d1_hbm, d2_hbm, out_hbm, idx1_v, idx2_v, rows_v, sem1, sem2):
        wid = lax.axis_index("s") * SC_CORES + lax.axis_index("c")

        @pl.loop(0, per_w // SC_CHUNK)
        def _(c):
            off = wid * per_w + c * SC_CHUNK
            pltpu.sync_copy(src_hbm.at[pl.ds(off, SC_CHUNK)], rows_v)
            pltpu.sync_copy(d1_hbm.at[pl.ds(off, SC_CHUNK)], idx1_v)
            pltpu.sync_copy(d2_hbm.at[pl.ds(off, SC_CHUNK)], idx2_v)
            first = pltpu.async_copy(rows_v, out_hbm.at[idx1_v], sem1)
            second = pltpu.async_copy(rows_v, out_hbm.at[idx2_v], sem2)
            first.wait()
            second.wait()

    return k(src, d1, d2)


def _sc_gather_rows(table, idx):
    B, W = idx.shape[0], table.shape[1]
    per_w = B // (SC_CORES * SC_SUBCORES)

    ch = SC_CHUNK // 2
    n = per_w // ch
    assert n % 2 == 0

    @functools.partial(pl.kernel, mesh=_sc_mesh(), out_type=jax.ShapeDtypeStruct((B, W), table.dtype),
                       scratch_types=[pltpu.VMEM((ch,), jnp.int32), pltpu.VMEM((ch,), jnp.int32),
                                      pltpu.VMEM((ch, W), table.dtype), pltpu.VMEM((ch, W), table.dtype),
                                      pltpu.SemaphoreType.DMA, pltpu.SemaphoreType.DMA],
                       name="sc_gather_rows")
    def k(table_hbm, idx_hbm, out_hbm, idx_a, idx_b, rows_a, rows_b, sem_a, sem_b):
        base = (lax.axis_index("s") * SC_CORES + lax.axis_index("c")) * per_w

        def fire(c, idx_v, rows_v, sem):
            pltpu.sync_copy(idx_hbm.at[pl.ds(base + c * ch, ch)], idx_v)
            pltpu.async_copy(table_hbm.at[idx_v], rows_v, sem)

        def drain(c, idx_v, rows_v, sem):
            pltpu.make_async_copy(table_hbm.at[idx_v], rows_v, sem).wait()
            pltpu.sync_copy(rows_v, out_hbm.at[pl.ds(base + c * ch, ch)])

        fire(0, idx_a, rows_a, sem_a)

        @pl.loop(0, n, step=2)
        def _(c):
            fire(c + 1, idx_b, rows_b, sem_b)
            drain(c, idx_a, rows_a, sem_a)

            @pl.when(c + 2 < n)
            def _():
                fire(c + 2, idx_a, rows_a, sem_a)

            drain(c + 1, idx_b, rows_b, sem_b)

    return k(table, idx)


def _dest_kernel(offs_ref, ri_ref, d_ref):
    ri = ri_ref[...]
    base = jnp.zeros(ri.shape, F32)
    for e in range(N_EXP):
        base = jnp.where(ri == e, offs_ref[e].astype(F32), base)
    d_ref[...] = (base + pltpu.roll(ri, 6, 0)).astype(jnp.int32)


def _dispatch_plan(ri, cnt, N):
    counts = cnt[:N_EXP, 0].astype(jnp.int32)
    padded = ((counts + TM_E - 1) // TM_E) * TM_E
    ends = jnp.cumsum(padded)
    offs = ends - padded
    take = lambda t, i: t.at[i].get(mode="promise_in_bounds")
    dest = pl.pallas_call(
        _dest_kernel,
        in_specs=[pl.BlockSpec(memory_space=pltpu.SMEM), pl.BlockSpec((8, N), lambda: (0, 0))],
        out_specs=pl.BlockSpec((8, N), lambda: (0, 0)),
        out_shape=jax.ShapeDtypeStruct((8, N), jnp.int32),
        name="dispatch_rows",
    )(offs, ri)
    d1, d2 = dest[0], dest[1]
    P = 2 * N + N_EXP * TM_E
    tile_start = jnp.arange(P // TM_E, dtype=jnp.int32) * TM_E
    tile_expert = jnp.minimum(jnp.sum((ends[None, :] <= tile_start[:, None]).astype(jnp.int32), axis=1), N_EXP - 1)
    tile_valid = jnp.clip(take(offs + counts, tile_expert) - tile_start, 0, TM_E)
    return d1, d2, P, tile_expert, tile_valid


def _combine(h_ref, y1_ref, y2_ref, rg_ref):
    rg = rg_ref[...]
    lo1, hi1 = _unpack_halves(y1_ref[...])
    lo2, hi2 = _unpack_halves(y2_ref[...])
    g1, g2 = rg[:, 0:1], rg[:, 1:2]
    return h_ref[...] + jnp.concatenate([g1 * lo1 + g2 * lo2, g1 * hi1 + g2 * hi2], axis=1)


ODD_WIDTHS = (1024, 1024, 512, 512)


def _combine_proj_odd_kernel(h_ref, y1_ref, y2_ref, rg_ref, g_ref, win_ref, wvt_ref, ind_ref, hn_ref, vt_ref,
                             vdt_ref, *out_refs):
    hn = _combine(h_ref, y1_ref, y2_ref, rg_ref)
    hn_ref[...] = hn
    u = _rms(hn, g_ref[...]).astype(BF16)
    vt_ref[...] = _values_t(_dot_nt(wvt_ref[0:512, :], u)).astype(BF16)
    vdt_ref[...] = _dot_nt(wvt_ref[512:1024, :], u).astype(BF16)
    qc_ref, kc_ref, qd_ref, kd_ref = out_refs
    qc = _spread_heads(_dot(u, win_ref[:, 0:512]), lambda h: 0)
    kc = _spread_heads(_dot(u, win_ref[:, 512:1024]), lambda h: 0)
    for hh in range(8):
        qc_ref[:, 128 * hh:128 * (hh + 1)] = qc[hh].astype(BF16)
        kc_ref[:, 128 * hh:128 * (hh + 1)] = (kc[hh] + ind_ref[...]).astype(BF16)
    qd_ref[...] = _dot(u, win_ref[:, 1024:1536]).astype(BF16)
    kd_ref[...] = _dot(u, win_ref[:, 1536:2048]).astype(BF16)


def _odd_weights(w_in):
    s = HD ** -0.5 * LOG2E
    scale = np.asarray([s] * 512 + [1.0] * 512 + [s] * 512 + [1.0] * 512, np.float32)
    win = jnp.concatenate([w_in[:, 0:1024], w_in[:, 1536:2560]], axis=1) * jnp.asarray(scale)
    wvt = jnp.concatenate([w_in[:, 1024:1536], w_in[:, 2560:3072]], axis=1).T
    return win.astype(BF16), wvt.astype(BF16)


def _block_indicator(T):
    assert T // MOBA_L <= 64
    ind = np.zeros((T, 128), np.float32)
    ind[np.arange(T), 64 + np.arange(T) // MOBA_L] = 1.0
    return jnp.asarray(ind)


def _y_specs(N):
    return [pl.BlockSpec((TM, HALF), lambda i: (i, 0)), pl.BlockSpec((TM, HALF), lambda i: (i + N // TM, 0))]


def _combine_proj_odd(h, y12, rg, g, win, wvt, ind, T):
    N = h.shape[0]
    nt = T // TM
    full = lambda a: pl.BlockSpec(a.shape, lambda i: (0,) * a.ndim)
    row = lambda w: pl.BlockSpec((TM, w), lambda i: (i, 0))
    return pl.pallas_call(
        _combine_proj_odd_kernel,
        grid=(N // TM,),
        in_specs=[row(D)] + _y_specs(N) + [row(128), full(g), full(win), full(wvt),
                  pl.BlockSpec((TM, 128), lambda i: (i % nt, 0))],
        out_specs=[row(D), pl.BlockSpec((None, 8 * VT_ROWS, TM), lambda i: (i // nt, 0, i % nt)),
                   pl.BlockSpec((None, 512, TM), lambda i: (i // nt, 0, i % nt))]
        + [row(w) for w in ODD_WIDTHS],
        out_shape=[jax.ShapeDtypeStruct((N, D), F32), jax.ShapeDtypeStruct((N // T, 8 * VT_ROWS, T), BF16),
                   jax.ShapeDtypeStruct((N // T, 512, T), BF16)]
        + [jax.ShapeDtypeStruct((N, w), BF16) for w in ODD_WIDTHS],
        compiler_params=_cparams(("arbitrary",)),
        name="combine_proj_odd",
    )(h, y12, y12, rg, g, win, wvt, ind)


def _combine_final_kernel(h_ref, y1_ref, y2_ref, rg_ref, g_ref, o_ref):
    o_ref[...] = _rms(_combine(h_ref, y1_ref, y2_ref, rg_ref), g_ref[...])


def _combine_final(h, y12, rg, g):
    N = h.shape[0]
    row = lambda w: pl.BlockSpec((TM, w), lambda i: (i, 0))
    return pl.pallas_call(
        _combine_final_kernel,
        grid=(N // TM,),
        in_specs=[row(D)] + _y_specs(N) + [row(128), pl.BlockSpec((1, D), lambda i: (0, 0))],
        out_specs=row(D),
        out_shape=jax.ShapeDtypeStruct((N, D), F32),
        compiler_params=_cparams(("arbitrary",)),
        name="combine_final",
    )(h, y12, y12, rg, g)


def _moba_kernel(q_ref, k_ref, vt_ref, tb_ref, o_ref, kmean_ref, qa_ref, s_ref, m_ref, acc_ref, *, nkb):
    i = pl.program_id(2)
    L, tq = MOBA_L, MOBA_TQ

    @pl.when(i == 0)
    def _():
        kmean_ref[...] = jnp.zeros_like(kmean_ref)
        for n in range(nkb):
            kmean_ref[64 + n:65 + n, :] = jnp.sum(k_ref[n * L:(n + 1) * L, :].astype(F32), axis=0,
                                                  keepdims=True) * (1.0 / L)

    lane = lax.broadcasted_iota(jnp.int32, (tq, 128), 1)
    rows = lax.broadcasted_iota(jnp.int32, (128, tq), 0)
    own = 2 * i + jnp.where(lax.broadcasted_iota(jnp.int32, (128, tq), 1) >= L, 1, 0)
    blk = rows - 64
    rows_f = rows.astype(F32)
    for j in range(2):
        qj = q_ref[:, 128 * j:128 * (j + 1)]
        gsc = lax.dot_general(kmean_ref[:, 128 * j:128 * (j + 1)], qj.astype(F32), (((1,), (1,)), ((), ())),
                              preferred_element_type=F32, precision=lax.Precision.HIGHEST)
        g = jnp.where((blk >= 0) & (blk < own), gsc, NEG)
        allowed = blk == own
        for _ in range(MOBA_TOPK):
            mx = jnp.max(g, axis=0, keepdims=True)
            idx = jnp.min(jnp.where(g == mx, rows_f, 1e6), axis=0, keepdims=True)
            pick = rows_f == idx
            allowed = allowed | (pick & (mx > 0.5 * NEG))
            g = jnp.where(pick, 2.0 * NEG, g)
        qa_ref[j] = jnp.where(lane < HD, qj, jnp.where(allowed, 0.0, NEG).T.astype(BF16))

    def bias_of(start, j):
        parts = []
        for c in range(tq // L):
            off = jnp.minimum(i * tq - start - c * L, BIAS_CONST_FROM) + BIAS_PAD
            parts.append(tb_ref[j, :, pl.ds(pl.multiple_of(off, L), tq)])
        return jnp.concatenate(parts, axis=0)

    _causal_attention_t(i, lambda j: qa_ref[j], k_ref, vt_ref, o_ref, s_ref, m_ref, acc_ref, tq=tq, tk=tq,
                        bias_of=bias_of)


def _moba_attention(q, k, vt, tb, B, T):
    tq = MOBA_TQ
    nq = T // tq
    return pl.pallas_call(
        functools.partial(_moba_kernel, nkb=T // MOBA_L),
        grid=(B, 4, nq),
        in_specs=[pl.BlockSpec((tq, 256), lambda b, p, i: (b * nq + i, p)),
                  pl.BlockSpec((T, 256), lambda b, p, i: (b, p)),
                  pl.BlockSpec((None, 2 * VT_ROWS, T), lambda b, p, i: (b, p, 0)),
                  pl.BlockSpec((2, MOBA_L, BIAS_ROWS), lambda b, p, i: (p, 0, 0))],
        out_specs=pl.BlockSpec((tq, 128), lambda b, p, i: (b * nq + i, p)),
        out_shape=jax.ShapeDtypeStruct((B * T, 512), BF16),
        scratch_shapes=[pltpu.VMEM((128, 256), F32), pltpu.VMEM((2, tq, 128), BF16)]
        + _attention_t_scratch(tq, tq),
        compiler_params=_cparams(("arbitrary", "arbitrary", "arbitrary")),
        name="moba_attention",
    )(q, k, vt, tb)


def _sb_kernel(q_ref, k_ref, vt_ref, o_ref, *, tq, pairs):
    i = pl.program_id(2)
    lane = lax.broadcasted_iota(jnp.int32, (tq, 128), 1)
    qs = []
    for p in range(pairs):
        q_pair = q_ref[:, 128 * p:128 * (p + 1)]
        qs += [jnp.where(lane < HD, q_pair, 0).astype(BF16), jnp.where(lane >= HD, q_pair, 0).astype(BF16)]
    key = lax.broadcasted_iota(jnp.int32, (tq, tq), 0)
    qry = lax.broadcasted_iota(jnp.int32, (tq, tq), 1)
    past = key < qry
    suffix = jnp.where(qry >= key, -1.0, 0.0).astype(BF16)
    heads = range(2 * pairs)

    def step(kb, carry, boundary):
        start = pl.multiple_of(kb * tq, tq)
        zs = [_dot_nt(k_ref[pl.ds(start, tq), 128 * (h // 2):128 * (h // 2 + 1)], qs[h]) for h in heads]
        parts = []
        for z in zs:
            soft = jnp.maximum(z, 0.0) + jnp.log2(1.0 + jnp.exp2(-jnp.abs(z)))
            if boundary:
                soft = jnp.where(past, soft, 0.0)
            hi = soft.astype(BF16)
            parts.append((hi, (soft - hi.astype(F32)).astype(BF16)))
        sums = [_dot(suffix, hi) + _dot(suffix, lo) for hi, lo in parts]
        ws = []
        for h in heads:
            w = jnp.exp2(zs[h] + (carry[h][0] + sums[h]))
            ws.append((jnp.where(past, w, 0.0) if boundary else w).astype(BF16))
        return tuple((carry[h][0] + sums[h][0:1], carry[h][1]
                      + _dot(vt_ref[128 * (h // 2):128 * (h // 2 + 1), pl.ds(start, tq)], ws[h])) for h in heads)

    def live(carry):
        top = carry[0][0]
        for c, _ in carry[1:]:
            top = jnp.maximum(top, c)
        return (jnp.max(top) > SB_DONE).astype(jnp.int32)

    def body(state):
        t, _, carry = state
        carry = step(i - 1 - t, carry, False)
        return t + 1, live(carry), carry

    init = tuple((jnp.zeros((1, tq), F32), jnp.zeros((128, tq), F32)) for _ in range(2 * pairs))
    carry = step(i, init, True)
    _, _, carry = lax.while_loop(lambda st: (st[0] < i) & (st[1] > 0), body, (jnp.int32(0), live(carry), carry))
    rows = lax.broadcasted_iota(jnp.int32, (128, tq), 0)
    for p in range(pairs):
        ot = jnp.where(rows < HD, carry[2 * p][1], carry[2 * p + 1][1])
        o_ref[:, 128 * p:128 * (p + 1)] = ot.T.astype(BF16)


def _sb_attention(q, k, vt, B, T, tq=256, pairs=4):
    nq = T // tq
    w = 128 * pairs
    return pl.pallas_call(
        functools.partial(_sb_kernel, tq=tq, pairs=pairs),
        grid=(B, 4 // pairs, nq),
        in_specs=[pl.BlockSpec((tq, w), lambda b, p, i: (b * nq + i, p)),
                  pl.BlockSpec((T, w), lambda b, p, i: (b, p)),
                  pl.BlockSpec((None, w, T), lambda b, p, i: (b, p, 0))],
        out_specs=pl.BlockSpec((tq, w), lambda b, p, i: (b * nq + i, p)),
        out_shape=jax.ShapeDtypeStruct((B * T, 512), BF16),
        compiler_params=_cparams(("arbitrary", "arbitrary", "arbitrary")),
        name="sb_attention",
    )(q, k, vt)


def _router_weights(w_group, b_group, w_router, b_router):
    pad = 128 - N_EXP - N_GROUPS
    wr = jnp.concatenate([w_router, w_group, jnp.zeros((D, pad), F32)], axis=1).T
    hi = wr.astype(BF16)
    lo = (wr - hi.astype(F32)).astype(BF16)
    br = jnp.concatenate([b_router, b_group, jnp.zeros((pad,), F32)])[:, None]
    return jnp.concatenate([hi, lo], axis=0), br


def _ffn(h, oa, ob, wo, g, w_group, b_group, w_router, b_router, w_gate, w_up, w_down, layer):
    wr, br = _router_weights(w_group, b_group, w_router, b_router)
    hn, u, ri, rg, cnt = _outproj_router(h, oa, ob, wo.astype(BF16), g[None, :], wr, br)
    d1, d2, P, tile_expert, tile_valid = _dispatch_plan(ri, cnt, h.shape[0])
    xs = _sc_scatter_rows(u, d1, d2, P)
    ys = _moe_experts(xs, tile_expert, tile_valid, w_gate, w_up, w_down, layer)
    return hn, _sc_gather_rows(ys, jnp.concatenate([d1, d2])), rg


def kernel(x, norm_mix, norm_ffn, norm_final, rel_bias_table, w_in_even, g_mla_q, g_mla_kv, w_mla_qb, w_mla_kvb, swa_sinks, w_out_even, w_in_odd, w_out_odd, moe_w_group, moe_b_group, moe_w_router, moe_b_router, moe_w_gate, moe_w_up, moe_w_down):
    B, T, _ = x.shape
    h = x.reshape(B * T, D)
    tb_swa, tb_moba = _bias_tables(rel_bias_table)

    win, wqa, wqb, wk, wv = _even_weights(w_in_even[0], w_mla_qb[0], w_mla_kvb[0])
    q, k, v, qs8, ks, vs, vsw = _proj_even(h, norm_mix[0][None, :], win, g_mla_q[0][None, :], g_mla_kv[0][None, :],
                                           wqa, wqb, wk, wv, _rope_tables(T), T)
    oa = _mla_attention(q, k, v, B, T)
    ob = _swa_attention(qs8, ks, vs, vsw, swa_sinks[0], tb_swa, B, T)
    h, y12, rg = _ffn(h, oa, ob, w_out_even[0], norm_ffn[0], moe_w_group[0], moe_b_group[0], moe_w_router[0],
                      moe_b_router[0], moe_w_gate, moe_w_up, moe_w_down, 0)

    h, vct, vdt, qc, kc, qd, kd = _combine_proj_odd(h, y12, rg, norm_mix[1][None, :], *_odd_weights(w_in_odd[0]),
                                                   _block_indicator(T), T)
    oc = _moba_attention(qc, kc, vct, tb_moba, B, T)
    od = _sb_attention(qd, kd, vdt, B, T)
    h, y12, rg = _ffn(h, oc, od, w_out_odd[0], norm_ffn[1], moe_w_group[1], moe_b_group[1], moe_w_router[1],
                      moe_b_router[1], moe_w_gate, moe_w_up, moe_w_down, 1)
    out = _combine_final(h, y12, rg, norm_final[None, :])
    return out.reshape(B, T, D)
```

```python
import functools
import math

import numpy as np
import jax
import jax.numpy as jnp
from jax import lax
from jax.experimental import pallas as pl
from jax.experimental.pallas import tpu as pltpu
from jax.experimental.pallas import tpu_sc as plsc

F32 = jnp.float32
BF16 = jnp.bfloat16
NEG = -1e30
EPS = 1e-6

D = 1024
HD = 64
LANES = 128
MLA_HEADS, MLA_QR, MLA_KVR, MLA_NOPE, MLA_ROPE, MLA_V = 8, 256, 128, 64, 32, 64
ROPE_THETA = 10000.0
SWA_HEADS, SWA_KV, SWA_WIN = 8, 2, 128
MOBA_L, MOBA_TOPK = 256, 3
REL_BUCKETS, REL_MAX = 32, 2048
N_GROUPS, EPG, N_EXP, D_EXP = 4, 8, 32, 256
BIAS_CONST_FROM = 1792
MOBA_TQ = 2 * MOBA_L
BIAS_PAD = MOBA_L
BIAS_ROWS = BIAS_PAD + BIAS_CONST_FROM + MOBA_TQ
SB_DONE = -160.0
TM = 512
TM_E = 512
VMEM_LIMIT = 56 * 1024 * 1024
SC_CORES, SC_SUBCORES = 2, 16
SC_CHUNK = 128
HALF = D // 2
VT_ROWS = HD + 16
LOG2E = math.log2(math.e)


def _pack_halves(x):
    lo = pltpu.bitcast(x[:, :HALF].astype(BF16).astype(F32), jnp.uint32) >> 16
    hi = pltpu.bitcast(x[:, HALF:].astype(BF16).astype(F32), jnp.uint32) & jnp.uint32(0xFFFF0000)
    return pltpu.bitcast(lo | hi, jnp.int32)


def _unpack_halves(w):
    w = pltpu.bitcast(w, jnp.uint32)
    return pltpu.bitcast(w << 16, F32), pltpu.bitcast(w & jnp.uint32(0xFFFF0000), F32)


def _cparams(sem):
    return pltpu.CompilerParams(dimension_semantics=sem, vmem_limit_bytes=VMEM_LIMIT)


def _dot(a, b):
    return jnp.dot(a, b, preferred_element_type=F32)


def _dot_nt(a, b):
    return lax.dot_general(a, b, (((1,), (1,)), ((), ())), preferred_element_type=F32)


def _rms(x, g):
    return x * lax.rsqrt(jnp.mean(x * x, axis=-1, keepdims=True) + EPS) * g


def _values_t(vt):
    ones = jnp.ones((VT_ROWS - HD, vt.shape[1]), vt.dtype)
    return jnp.concatenate([blk for h in range(8) for blk in (vt[HD * h:HD * (h + 1)], ones)], axis=0)


def _spread_heads(x, lane_of):
    lane = lax.broadcasted_iota(jnp.int32, (x.shape[0], 128), 1)
    out = []
    for h in range(8):
        pair = x[:, 128 * (h // 2):128 * (h // 2 + 1)]
        if HD * (h % 2) != lane_of(h):
            pair = pltpu.roll(pair, HD, 1)
        out.append(jnp.where((lane >= lane_of(h)) & (lane < lane_of(h) + HD), pair, 0.0))
    return out


def _flash_update_t(st, vt, m, acc):
    m_new = jnp.maximum(m, jnp.max(st, axis=0, keepdims=True))
    alpha = jnp.exp2(m - m_new)
    pt = jnp.exp2(st - m_new).astype(BF16)
    return m_new, alpha * acc + _dot(vt, pt)


def _flash_update(s, v, m, l, acc):
    m_new = jnp.maximum(m, jnp.max(s, axis=-1, keepdims=True))
    alpha = jnp.exp(m - m_new)
    p = jnp.exp(s - m_new)
    l = alpha * l + jnp.sum(p, axis=-1, keepdims=True)
    acc = alpha * acc + _dot(p.astype(BF16), v)
    return m_new, l, acc


def _t5_bucket_np(dist):
    n = np.maximum(dist, 0)
    max_exact = REL_BUCKETS // 2
    nf = np.maximum(n, 1).astype(np.float32)
    large = max_exact + (np.log(nf / np.float32(max_exact)) / np.float32(math.log(REL_MAX / max_exact))
                         * np.float32(REL_BUCKETS - max_exact)).astype(np.int32)
    large = np.minimum(large, REL_BUCKETS - 1)
    return np.where(n < max_exact, n, large).astype(np.int32)


def _bias_kernel(table_ref, bucket_ref, out_ref, *, scale):
    b = bucket_ref[...]
    for h in range(8):
        acc = jnp.zeros(b.shape, F32)
        for k in range(REL_BUCKETS):
            acc = jnp.where(b == k, table_ref[k, h] * scale, acc)
        out_ref[h] = acc


def _bias_lookup(rel_table, dist, scale, name):
    R, C = dist.shape
    return pl.pallas_call(
        functools.partial(_bias_kernel, scale=scale),
        grid=(C // 256,),
        in_specs=[pl.BlockSpec(memory_space=pltpu.SMEM),
                  pl.BlockSpec((R, 256), lambda m: (0, m))],
        out_specs=pl.BlockSpec((8, R, 256), lambda m: (0, 0, m)),
        out_shape=jax.ShapeDtypeStruct((8, R, C), F32),
        compiler_params=_cparams(("arbitrary",)),
        name=name,
    )(rel_table, jnp.asarray(_t5_bucket_np(dist)))


def _bias_tables(rel_table):
    swa = _bias_lookup(rel_table, np.arange(256)[:, None] - np.arange(256)[None, :], 1.0, "bias_swa")
    moba = _bias_lookup(rel_table, np.arange(BIAS_ROWS)[None, :] - BIAS_PAD - np.arange(MOBA_L)[:, None], LOG2E,
                        "bias_moba")
    return swa, moba


def _proj_even_kernel(h_ref, g_ref, win_ref, gq_ref, gkv_ref, wqa_ref, wqb_ref, wk_ref, wv_ref,
                      cq_ref, sq_ref, ck_ref, sk_ref,
                      q_ref, k_ref, v_ref, qs_ref, ks_ref, vs_ref, vsw_ref):
    u = _rms(h_ref[...], g_ref[...]).astype(BF16)
    nq = _rms(_dot(u, win_ref[:, 0:256]), gq_ref[...]).astype(BF16)
    qa = _dot(nq, wqa_ref[...])
    qb = _dot(nq, wqb_ref[...])
    cq, sq = cq_ref[...], sq_ref[...]
    for hh in range(MLA_HEADS):
        sl = slice(128 * hh, 128 * (hh + 1))
        q_ref[:, sl] = (qa[:, sl] * cq + qb[:, sl] * sq).astype(BF16)
    nkv = _rms(_dot(u, win_ref[:, 256:384]), gkv_ref[...]).astype(BF16)
    kn = _dot(nkv, wk_ref[...])
    v_ref[...] = _values_t(_dot_nt(wv_ref[...], nkv)).astype(BF16)
    rest = _dot(u, win_ref[:, 896:1536])
    kr = rest[:, 384:512] * ck_ref[...] + rest[:, 512:640] * sk_ref[...]
    for hh in range(MLA_HEADS):
        sl = slice(128 * hh, 128 * (hh + 1))
        k_ref[:, sl] = (kn[:, sl] + kr).astype(BF16)
    qs = _spread_heads(_dot(u, win_ref[:, 384:896]), lambda hq: HD * (hq // (SWA_HEADS // SWA_KV)))
    for hq in range(SWA_HEADS):
        qs_ref[:, 128 * hq:128 * (hq + 1)] = qs[hq].astype(BF16)
    ks_ref[...] = rest[:, 0:128].astype(BF16)
    vs_ref[...] = rest[:, 128:256].astype(BF16)
    vsw_ref[...] = rest[:, 256:384].astype(BF16)


def _cols(w, idx, scale=None):
    idx = np.asarray(idx)
    cuts = [0] + [n for n in range(1, len(idx)) if (idx[n] < 0) != (idx[n - 1] < 0)
                  or (idx[n] >= 0 and idx[n] != idx[n - 1] + 1)] + [len(idx)]
    parts = [jnp.zeros((w.shape[0], b - a), w.dtype) if idx[a] < 0 else w[:, idx[a]:idx[a] + b - a]
             for a, b in zip(cuts[:-1], cuts[1:])]
    out = jnp.concatenate(parts, axis=1)
    if scale is not None:
        out = out * scale
    return out


def _even_weights(w_in, w_qb, w_kvb):
    o_cq, o_ckv, o_kr = 0, MLA_QR, MLA_QR + MLA_KVR
    o_qs = o_kr + MLA_ROPE
    o_ks = o_qs + SWA_HEADS * HD
    o_vs = o_ks + SWA_KV * HD
    idx = list(range(o_cq, o_cq + MLA_QR)) + list(range(o_ckv, o_ckv + MLA_KVR))
    scale = [1.0] * len(idx)
    idx += list(range(o_qs, o_qs + SWA_HEADS * HD))
    scale += [HD ** -0.5] * (SWA_HEADS * HD)
    idx += list(range(o_ks, o_ks + 128))
    idx += list(range(o_vs, o_vs + 128))
    idx += list(range(o_vs + 64, o_vs + 128)) + list(range(o_vs, o_vs + 64))
    scale += [1.0] * 384
    half = MLA_ROPE // 2
    idx += [-1] * 64 + list(range(o_kr, o_kr + MLA_ROPE)) + [-1] * 32
    idx += [-1] * 64 + list(range(o_kr + half, o_kr + MLA_ROPE)) + list(range(o_kr, o_kr + half)) + [-1] * 32
    scale += [1.0] * 256
    win = _cols(w_in, idx, jnp.asarray(np.asarray(scale, np.float32))).astype(BF16)
    qw = MLA_NOPE + MLA_ROPE
    ia, ib, ik, iv = [], [], [], []
    for h in range(MLA_HEADS):
        ia += list(range(qw * h, qw * h + qw)) + [-1] * 32
        ib += [-1] * 64 + list(range(qw * h + MLA_NOPE + half, qw * h + qw)) \
            + list(range(qw * h + MLA_NOPE, qw * h + MLA_NOPE + half)) + [-1] * 32
        ik += list(range(128 * h, 128 * h + MLA_NOPE)) + [-1] * 64
        iv += list(range(128 * h + MLA_NOPE, 128 * h + 128))
    return (win, _cols(w_qb, ia).astype(BF16), _cols(w_qb, ib).astype(BF16),
            _cols(w_kvb, ik).astype(BF16), _cols(w_kvb, iv).T.astype(BF16))


def _rope_tables(T):
    half = MLA_ROPE // 2
    freqs = ROPE_THETA ** (-jnp.arange(0, MLA_ROPE, 2, dtype=F32) / MLA_ROPE)
    ang = jnp.arange(T, dtype=F32)[:, None] * freqs[None, :]
    cos, sin = jnp.cos(ang), jnp.sin(ang)
    z = lambda n: jnp.zeros((T, n), F32)
    scale = (MLA_NOPE + MLA_ROPE) ** -0.5 * LOG2E
    ck = jnp.concatenate([z(64), cos, cos, z(32)], axis=1)
    sk = jnp.concatenate([z(64), -sin, sin, z(32)], axis=1)
    cq = jnp.concatenate([jnp.ones((T, 64), F32), cos, cos, z(32)], axis=1) * scale
    sq = sk * scale
    return cq, sq, ck, sk


def _proj_even(h, g, win, gq, gkv, wqa, wqb, wk, wv, tables, T):
    N = h.shape[0]
    nt = T // TM
    full = lambda a: pl.BlockSpec(a.shape, lambda i: (0,) * a.ndim)
    tab = pl.BlockSpec((TM, 128), lambda i: (i % nt, 0))
    row = lambda w: pl.BlockSpec((TM, w), lambda i: (i, 0))
    widths = (1024, 1024, None, 1024, 128, 128, 128)
    vt_spec = pl.BlockSpec((None, 8 * VT_ROWS, TM), lambda i: (i // nt, 0, i % nt))
    vt_shape = jax.ShapeDtypeStruct((N // T, 8 * VT_ROWS, T), BF16)
    return pl.pallas_call(
        _proj_even_kernel,
        grid=(N // TM,),
        in_specs=[row(D), full(g), full(win), full(gq), full(gkv), full(wqa), full(wqb), full(wk), full(wv),
                  tab, tab, tab, tab],
        out_specs=[vt_spec if w is None else row(w) for w in widths],
        out_shape=[vt_shape if w is None else jax.ShapeDtypeStruct((N, w), BF16) for w in widths],
        compiler_params=_cparams(("arbitrary",)),
        name="proj_even",
    )(h, g, win, gq, gkv, wqa, wqb, wk, wv, *tables)


def _finish_t(accs):
    ot = jnp.concatenate([acc[:HD] / acc[HD:HD + 1] for acc in accs], axis=0)
    return ot.T.astype(BF16)


def _causal_attention_t(i, q_of, k_ref, vt_ref, o_ref, s_ref, smax_ref, m_ref, acc_ref, *, tq, tk, bias_of=None):
    n_full = i * (tq // tk)
    key = lax.broadcasted_iota(jnp.int32, (tk, tq), 0)
    qry = i * tq + lax.broadcasted_iota(jnp.int32, (tk, tq), 1)

    def scores_to(slot, start, boundary):
        start = pl.multiple_of(start, tk)
        for j in range(2):
            st = _dot_nt(k_ref[pl.ds(start, tk), 128 * j:128 * (j + 1)], q_of(j))
            if bias_of is not None:
                st = st + bias_of(start, j)
            if boundary:
                st = jnp.where(start + key <= qry, st, NEG)
            s_ref[slot, j] = st
            smax_ref[slot, j] = jnp.max(st, axis=0, keepdims=True)

    def update_from(slot, start):
        start = pl.multiple_of(start, tk)
        m_new = [jnp.maximum(m_ref[j], smax_ref[slot, j]) for j in range(2)]
        pts = [jnp.exp2(s_ref[slot, j] - m_new[j]).astype(BF16) for j in range(2)]
        for j in range(2):
            acc_ref[j] = (jnp.exp2(m_ref[j] - m_new[j]) * acc_ref[j]
                          + _dot(vt_ref[VT_ROWS * j:VT_ROWS * (j + 1), pl.ds(start, tk)], pts[j]))
            m_ref[j] = m_new[j]

    m_ref[...] = jnp.full(m_ref.shape, NEG, F32)
    acc_ref[...] = jnp.zeros(acc_ref.shape, F32)
    for d in range(tq // tk - 1):
        scores_to(0, (n_full + d) * tk, True)
        update_from(0, (n_full + d) * tk)
    first = (n_full + tq // tk - 1) * tk
    scores_to(0, first, True)

    def block_start(k):
        return jnp.where(k == 0, first, (k - 1) * tk)

    def pair(k):
        scores_to(1, block_start(k + 1), False)
        update_from(0, block_start(k))
        scores_to(0, block_start(k + 2), False)
        update_from(1, block_start(k + 1))

    def quad(p, _):
        pair(4 * p)
        pair(4 * p + 2)
        return 0

    lax.fori_loop(0, n_full // 4, quad, 0)

    @pl.when(n_full % 4 >= 2)
    def _():
        pair((n_full // 4) * 4)

    @pl.when(n_full % 2 == 1)
    def _():
        scores_to(1, block_start(n_full), False)
        update_from(0, block_start(n_full - 1))
        update_from(1, block_start(n_full))

    @pl.when(n_full % 2 == 0)
    def _():
        update_from(0, block_start(n_full))

    o_ref[...] = _finish_t([acc_ref[j] for j in range(2)])


def _attention_t_scratch(tq, tk):
    return [pltpu.VMEM((2, 2, tk, tq), F32), pltpu.VMEM((2, 2, 1, tq), F32), pltpu.VMEM((2, 1, tq), F32),
            pltpu.VMEM((2, VT_ROWS, tq), F32)]


def _mla_kernel(q_ref, k_ref, vt_ref, o_ref, s_ref, smax_ref, m_ref, acc_ref, *, tq, tk):
    _causal_attention_t(pl.program_id(2), lambda j: q_ref[:, 128 * j:128 * (j + 1)], k_ref, vt_ref, o_ref,
                        s_ref, smax_ref, m_ref, acc_ref, tq=tq, tk=tk)


def _mla_attention(q, k, vt, B, T, tq=512, tk=512):
    nq = T // tq
    return pl.pallas_call(
        functools.partial(_mla_kernel, tq=tq, tk=tk),
        grid=(B, MLA_HEADS // 2, nq),
        in_specs=[pl.BlockSpec((tq, 256), lambda b, p, i: (b * nq + i, p)),
                  pl.BlockSpec((T, 256), lambda b, p, i: (b, p)),
                  pl.BlockSpec((None, 2 * VT_ROWS, T), lambda b, p, i: (b, p, 0))],
        out_specs=pl.BlockSpec((tq, 128), lambda b, p, i: (b * nq + i, p)),
        out_shape=jax.ShapeDtypeStruct((B * T, MLA_HEADS * MLA_V), BF16),
        scratch_shapes=_attention_t_scratch(tq, tk),
        compiler_params=_cparams(("arbitrary", "arbitrary", "arbitrary")),
        name="mla_attention",
    )(q, k, vt)


def _swa_kernel(sink_ref, q_ref, k_ref, v_ref, vsw_ref, tb_ref, o_ref, *, tq):
    i = pl.program_id(1)
    G = SWA_HEADS // SWA_KV
    qi = lax.broadcasted_iota(jnp.int32, (128, 256), 0)
    kj = lax.broadcasted_iota(jnp.int32, (128, 256), 1)
    lane = lax.broadcasted_iota(jnp.int32, (128, 128), 1)
    for r in range(tq // 128):
        qstart = i * tq + 128 * r
        kstart = pl.multiple_of(jnp.maximum(qstart - 128, 0), 128)
        off = pl.multiple_of(qstart - kstart, 128)
        dist = off + qi - kj
        valid = (dist >= 0) & (dist < SWA_WIN)
        kw = k_ref[pl.ds(kstart, 256), :]
        vw = v_ref[pl.ds(kstart, 256), :]
        vsw = vsw_ref[pl.ds(kstart, 256), :]
        halves = [None] * SWA_HEADS
        for c in range(SWA_KV):
            qg = jnp.concatenate([q_ref[128 * r:128 * (r + 1), 128 * (G * c + g):128 * (G * c + g + 1)]
                                  for g in range(G)], axis=0)
            s_all = _dot_nt(qg, kw)
            for g in range(G):
                hq = G * c + g
                s = s_all[128 * g:128 * (g + 1)] + tb_ref[hq, pl.ds(off, 128), :]
                s = jnp.where(valid, s, NEG)
                sink = sink_ref[hq]
                m = jnp.maximum(jnp.max(s, axis=-1, keepdims=True), sink)
                p = jnp.exp(s - m)
                denom = jnp.sum(p, axis=-1, keepdims=True) + jnp.exp(sink - m)
                p = (p / denom).astype(BF16)
                halves[hq] = _dot(p, vw if c == hq % 2 else vsw)
        for mp in range(SWA_HEADS // 2):
            o_ref[128 * r:128 * (r + 1), 128 * mp:128 * (mp + 1)] = jnp.where(
                lane < HD, halves[2 * mp], halves[2 * mp + 1]).astype(BF16)


def _swa_attention(q8, k, v, vsw, sinks, tb, B, T, tq=512):
    nq = T // tq
    kv = pl.BlockSpec((T, 128), lambda b, i: (b, 0))
    return pl.pallas_call(
        functools.partial(_swa_kernel, tq=tq),
        grid=(B, nq),
        in_specs=[pl.BlockSpec(memory_space=pltpu.SMEM),
                  pl.BlockSpec((tq, 1024), lambda b, i: (b * nq + i, 0)),
                  kv, kv, kv,
                  pl.BlockSpec((8, 256, 256), lambda b, i: (0, 0, 0))],
        out_specs=pl.BlockSpec((tq, 512), lambda b, i: (b * nq + i, 0)),
        out_shape=jax.ShapeDtypeStruct((B * T, SWA_HEADS * HD), BF16),
        compiler_params=_cparams(("arbitrary", "arbitrary")),
        name="swa_attention",
    )(sinks, q8, k, v, vsw, tb)


def _outproj_router_kernel(h_ref, oa_ref, ob_ref, wo_ref, g_ref, wr_ref, br_ref,
                           hn_ref, u_ref, ri_ref, rg_ref, cnt_ref, base_ref):
    i = pl.program_id(0)

    @pl.when(i == 0)
    def _():
        base_ref[...] = jnp.zeros_like(base_ref)

    hn = h_ref[...] + _dot(oa_ref[...], wo_ref[0:512, :]) + _dot(ob_ref[...], wo_ref[512:1024, :])
    hn_ref[...] = hn
    u = _rms(hn, g_ref[...])
    u_ref[...] = _pack_halves(u)
    u_hi = u.astype(BF16)
    u_lo = (u - u_hi.astype(F32)).astype(BF16)
    both = _dot_nt(wr_ref[...], u_hi)
    logits = both[:128] + both[128:] + _dot_nt(wr_ref[0:128, :], u_lo) + br_ref[...]
    tm = logits.shape[1]
    sub = lax.broadcasted_iota(jnp.int32, (8, tm), 0).astype(F32)
    big = 1e6
    isg = sub < N_GROUPS
    gl = jnp.where(isg, logits[N_EXP:N_EXP + 8], NEG)
    gmax = jnp.max(gl, axis=0, keepdims=True)
    gsel = jnp.min(jnp.where(gl == gmax, sub, big), axis=0, keepdims=True)
    g_gate = 1.0 / jnp.sum(jnp.where(isg, jnp.exp(gl - gmax), 0.0), axis=0, keepdims=True)
    el = jnp.zeros((EPG, tm), F32)
    for grp in range(N_GROUPS):
        el = jnp.where(gsel == grp, logits[EPG * grp:EPG * (grp + 1)], el)
    m1 = jnp.max(el, axis=0, keepdims=True)
    i1 = jnp.min(jnp.where(el == m1, sub, big), axis=0, keepdims=True)
    el2 = jnp.where(sub == i1, NEG, el)
    m2 = jnp.max(el2, axis=0, keepdims=True)
    i2 = jnp.min(jnp.where(el2 == m2, sub, big), axis=0, keepdims=True)
    e1 = gsel * EPG + i1
    e2 = gsel * EPG + i2
    r = jnp.exp(m2 - m1)
    gate1 = g_gate / (1.0 + r)
    gate2 = g_gate * r / (1.0 + r)
    rows = lax.broadcasted_iota(jnp.int32, (128, tm), 0).astype(F32)
    oh1 = (rows == e1).astype(F32)
    oh2 = (rows == e2).astype(F32)
    oh = oh1 + oh2
    t_row = lax.broadcasted_iota(jnp.int32, (tm, tm), 0)
    t_col = lax.broadcasted_iota(jnp.int32, (tm, tm), 1)
    earlier = jnp.where(t_row < t_col, 1.0, 0.0).astype(BF16)
    prefix = _dot(oh.astype(BF16), earlier) + base_ref[:, 0:1]
    rank1 = jnp.sum(oh1 * prefix, axis=0, keepdims=True)
    rank2 = jnp.sum(oh2 * prefix, axis=0, keepdims=True)
    base_ref[...] = base_ref[...] + jnp.sum(oh, axis=1, keepdims=True)
    cnt_ref[...] = base_ref[...]
    ri_ref[...] = jnp.where(sub == 0, e1, jnp.where(sub == 1, e2, jnp.where(sub == 2, rank1,
                            jnp.where(sub == 3, rank2, 0.0))))
    rg_ref[...] = jnp.where(rows == 0, gate1, jnp.where(rows == 1, gate2, 0.0)).T


def _outproj_router(h, oa, ob, wo, g, wr, br):
    N = h.shape[0]
    full = lambda a: pl.BlockSpec(a.shape, lambda i: (0,) * a.ndim)
    row = lambda w: pl.BlockSpec((TM, w), lambda i: (i, 0))
    return pl.pallas_call(
        _outproj_router_kernel,
        grid=(N // TM,),
        in_specs=[row(D), row(512), row(512), full(wo), full(g), full(wr), full(br)],
        out_specs=[row(D), row(HALF), pl.BlockSpec((8, TM), lambda i: (0, i)), row(128),
                   pl.BlockSpec((128, 128), lambda i: (0, 0))],
        out_shape=[jax.ShapeDtypeStruct((N, D), F32), jax.ShapeDtypeStruct((N, HALF), jnp.int32),
                   jax.ShapeDtypeStruct((8, N), F32), jax.ShapeDtypeStruct((N, 128), F32),
                   jax.ShapeDtypeStruct((128, 128), F32)],
        scratch_shapes=[pltpu.VMEM((128, 128), F32)],
        compiler_params=_cparams(("arbitrary",)),
        name="outproj_router",
    )(h, oa, ob, wo, g, wr, br)


def _moe_kernel(te_ref, tv_ref, x_ref, wg_ref, wu_ref, wd_ref, o_ref, wgu_s, wd_s):
    i = pl.program_id(0)
    valid = tv_ref[i]

    @pl.when((i == 0) | (te_ref[i] != te_ref[jnp.maximum(i - 1, 0)]))
    def _():
        wgu_s[:, :D_EXP] = wg_ref[...].astype(BF16)
        wgu_s[:, D_EXP:] = wu_ref[...].astype(BF16)
        wd_s[...] = wd_ref[...].astype(BF16)

    @pl.when(valid > 0)
    def _():
        rows = lax.broadcasted_iota(jnp.int32, x_ref.shape, 0)
        lo, hi = _unpack_halves(jnp.where(rows < valid, x_ref[...], 0))
        hgu = _dot(lo.astype(BF16), wgu_s[:HALF, :]) + _dot(hi.astype(BF16), wgu_s[HALF:, :])
        hg, hu = hgu[:, :D_EXP], hgu[:, D_EXP:]
        a = hg * (1.0 / (1.0 + jnp.exp(-hg))) * hu
        o_ref[...] = _pack_halves(_dot(a.astype(BF16), wd_s[...]))

    @pl.when(valid <= 0)
    def _():
        o_ref[...] = jnp.zeros_like(o_ref)


def _moe_experts(xs, tile_expert, tile_valid, w_gate, w_up, w_down, layer):
    P = xs.shape[0]
    grid_spec = pltpu.PrefetchScalarGridSpec(
        num_scalar_prefetch=2,
        grid=(P // TM_E,),
        in_specs=[pl.BlockSpec((TM_E, HALF), lambda i, te, tv: (i, 0)),
                  pl.BlockSpec((None, None, D, D_EXP), lambda i, te, tv: (layer, te[i], 0, 0)),
                  pl.BlockSpec((None, None, D, D_EXP), lambda i, te, tv: (layer, te[i], 0, 0)),
                  pl.BlockSpec((None, None, D_EXP, D), lambda i, te, tv: (layer, te[i], 0, 0))],
        out_specs=pl.BlockSpec((TM_E, HALF), lambda i, te, tv: (i, 0)),
        scratch_shapes=[pltpu.VMEM((D, 2 * D_EXP), BF16), pltpu.VMEM((D_EXP, D), BF16)],
    )
    return pl.pallas_call(
        _moe_kernel,
        grid_spec=grid_spec,
        out_shape=jax.ShapeDtypeStruct((P, HALF), jnp.int32),
        compiler_params=_cparams(("arbitrary",)),
        name="moe_experts",
    )(tile_expert, tile_valid, xs, w_gate, w_up, w_down)


def _sc_mesh():
    return plsc.VectorSubcoreMesh(core_axis_name="c", subcore_axis_name="s",
                                  num_cores=SC_CORES, num_subcores=SC_SUBCORES)


def _sc_scatter_rows(src, d1, d2, P):
    N, W = src.shape
    per_w = N // (SC_CORES * SC_SUBCORES)

    @functools.partial(pl.kernel, mesh=_sc_mesh(), out_type=jax.ShapeDtypeStruct((P, W), src.dtype),
                       scratch_types=[pltpu.VMEM((SC_CHUNK,), jnp.int32), pltpu.VMEM((SC_CHUNK, W), src.dtype)],
                       name="sc_scatter_rows")
    def k(src_hbm, d1_hbm, d2_hbm, out_hbm, idx_v, rows_v):
        wid = lax.axis_index("s") * SC_CORES + lax.axis_index("c")

        @pl.loop(0, per_w // SC_CHUNK)
        def _(c):
            off = wid * per_w + c * SC_CHUNK
            pltpu.sync_copy(src_hbm.at[pl.ds(off, SC_CHUNK)], rows_v)
            pltpu.sync_copy(d1_hbm.at[pl.ds(off, SC_CHUNK)], idx_v)
            pltpu.sync_copy(rows_v, out_hbm.at[idx_v])
            pltpu.sync_copy(d2_hbm.at[pl.ds(off, SC_CHUNK)], idx_v)
            pltpu.sync_copy(rows_v, out_hbm.at[idx_v])

    return k(src, d1, d2)


def _sc_gather_rows(table, idx):
    B, W = idx.shape[0], table.shape[1]
    per_w = B // (SC_CORES * SC_SUBCORES)

    @functools.partial(pl.kernel, mesh=_sc_mesh(), out_type=jax.ShapeDtypeStruct((B, W), table.dtype),
                       scratch_types=[pltpu.VMEM((SC_CHUNK,), jnp.int32), pltpu.VMEM((SC_CHUNK, W), table.dtype)],
                       name="sc_gather_rows")
    def k(table_hbm, idx_hbm, out_hbm, idx_v, rows_v):
        wid = lax.axis_index("s") * SC_CORES + lax.axis_index("c")

        @pl.loop(0, per_w // SC_CHUNK)
        def _(c):
            off = wid * per_w + c * SC_CHUNK
            pltpu.sync_copy(idx_hbm.at[pl.ds(off, SC_CHUNK)], idx_v)
            pltpu.sync_copy(table_hbm.at[idx_v], rows_v)
            pltpu.sync_copy(rows_v, out_hbm.at[pl.ds(off, SC_CHUNK)])

    return k(table, idx)


def _dest_kernel(offs_ref, ri_ref, d_ref):
    ri = ri_ref[...]
    base = jnp.zeros(ri.shape, F32)
    for e in range(N_EXP):
        base = jnp.where(ri == e, offs_ref[e].astype(F32), base)
    d_ref[...] = (base + pltpu.roll(ri, 6, 0)).astype(jnp.int32)


def _dispatch_plan(ri, cnt, N):
    counts = cnt[:N_EXP, 0].astype(jnp.int32)
    padded = ((counts + TM_E - 1) // TM_E) * TM_E
    ends = jnp.cumsum(padded)
    offs = ends - padded
    take = lambda t, i: t.at[i].get(mode="promise_in_bounds")
    dest = pl.pallas_call(
        _dest_kernel,
        in_specs=[pl.BlockSpec(memory_space=pltpu.SMEM), pl.BlockSpec((8, N), lambda: (0, 0))],
        out_specs=pl.BlockSpec((8, N), lambda: (0, 0)),
        out_shape=jax.ShapeDtypeStruct((8, N), jnp.int32),
        name="dispatch_rows",
    )(offs, ri)
    d1, d2 = dest[0], dest[1]
    P = 2 * N + N_EXP * TM_E
    tile_start = jnp.arange(P // TM_E, dtype=jnp.int32) * TM_E
    tile_expert = jnp.minimum(jnp.sum((ends[None, :] <= tile_start[:, None]).astype(jnp.int32), axis=1), N_EXP - 1)
    tile_valid = jnp.clip(take(offs + counts, tile_expert) - tile_start, 0, TM_E)
    return d1, d2, P, tile_expert, tile_valid


def _combine(h_ref, y1_ref, y2_ref, rg_ref):
    rg = rg_ref[...]
    lo1, hi1 = _unpack_halves(y1_ref[...])
    lo2, hi2 = _unpack_halves(y2_ref[...])
    g1, g2 = rg[:, 0:1], rg[:, 1:2]
    return h_ref[...] + jnp.concatenate([g1 * lo1 + g2 * lo2, g1 * hi1 + g2 * hi2], axis=1)


ODD_WIDTHS = (1024, 1024, 512, 512)


def _combine_proj_odd_kernel(h_ref, y1_ref, y2_ref, rg_ref, g_ref, win_ref, wvt_ref, ind_ref, hn_ref, vt_ref,
                             vdt_ref, *out_refs):
    hn = _combine(h_ref, y1_ref, y2_ref, rg_ref)
    hn_ref[...] = hn
    u = _rms(hn, g_ref[...]).astype(BF16)
    vt_ref[...] = _values_t(_dot_nt(wvt_ref[0:512, :], u)).astype(BF16)
    vdt_ref[...] = _dot_nt(wvt_ref[512:1024, :], u).astype(BF16)
    qc_ref, kc_ref, qd_ref, kd_ref = out_refs
    qc = _spread_heads(_dot(u, win_ref[:, 0:512]), lambda h: 0)
    kc = _spread_heads(_dot(u, win_ref[:, 512:1024]), lambda h: 0)
    for hh in range(8):
        qc_ref[:, 128 * hh:128 * (hh + 1)] = qc[hh].astype(BF16)
        kc_ref[:, 128 * hh:128 * (hh + 1)] = (kc[hh] + ind_ref[...]).astype(BF16)
    qd_ref[...] = _dot(u, win_ref[:, 1024:1536]).astype(BF16)
    kd_ref[...] = _dot(u, win_ref[:, 1536:2048]).astype(BF16)


def _odd_weights(w_in):
    s = HD ** -0.5 * LOG2E
    scale = np.asarray([s] * 512 + [1.0] * 512 + [s] * 512 + [1.0] * 512, np.float32)
    win = jnp.concatenate([w_in[:, 0:1024], w_in[:, 1536:2560]], axis=1) * jnp.asarray(scale)
    wvt = jnp.concatenate([w_in[:, 1024:1536], w_in[:, 2560:3072]], axis=1).T
    return win.astype(BF16), wvt.astype(BF16)


def _block_indicator(T):
    assert T // MOBA_L <= 64
    ind = np.zeros((T, 128), np.float32)
    ind[np.arange(T), 64 + np.arange(T) // MOBA_L] = 1.0
    return jnp.asarray(ind)


def _y_specs(N):
    return [pl.BlockSpec((TM, HALF), lambda i: (i, 0)), pl.BlockSpec((TM, HALF), lambda i: (i + N // TM, 0))]


def _combine_proj_odd(h, y12, rg, g, win, wvt, ind, T):
    N = h.shape[0]
    nt = T // TM
    full = lambda a: pl.BlockSpec(a.shape, lambda i: (0,) * a.ndim)
    row = lambda w: pl.BlockSpec((TM, w), lambda i: (i, 0))
    return pl.pallas_call(
        _combine_proj_odd_kernel,
        grid=(N // TM,),
        in_specs=[row(D)] + _y_specs(N) + [row(128), full(g), full(win), full(wvt),
                  pl.BlockSpec((TM, 128), lambda i: (i % nt, 0))],
        out_specs=[row(D), pl.BlockSpec((None, 8 * VT_ROWS, TM), lambda i: (i // nt, 0, i % nt)),
                   pl.BlockSpec((None, 512, TM), lambda i: (i // nt, 0, i % nt))]
        + [row(w) for w in ODD_WIDTHS],
        out_shape=[jax.ShapeDtypeStruct((N, D), F32), jax.ShapeDtypeStruct((N // T, 8 * VT_ROWS, T), BF16),
                   jax.ShapeDtypeStruct((N // T, 512, T), BF16)]
        + [jax.ShapeDtypeStruct((N, w), BF16) for w in ODD_WIDTHS],
        compiler_params=_cparams(("arbitrary",)),
        name="combine_proj_odd",
    )(h, y12, y12, rg, g, win, wvt, ind)


def _combine_final_kernel(h_ref, y1_ref, y2_ref, rg_ref, g_ref, o_ref):
    o_ref[...] = _rms(_combine(h_ref, y1_ref, y2_ref, rg_ref), g_ref[...])


def _combine_final(h, y12, rg, g):
    N = h.shape[0]
    row = lambda w: pl.BlockSpec((TM, w), lambda i: (i, 0))
    return pl.pallas_call(
        _combine_final_kernel,
        grid=(N // TM,),
        in_specs=[row(D)] + _y_specs(N) + [row(128), pl.BlockSpec((1, D), lambda i: (0, 0))],
        out_specs=row(D),
        out_shape=jax.ShapeDtypeStruct((N, D), F32),
        compiler_params=_cparams(("arbitrary",)),
        name="combine_final",
    )(h, y12, y12, rg, g)


def _moba_kernel(q_ref, k_ref, vt_ref, tb_ref, o_ref, kmean_ref, qa_ref, s_ref, smax_ref, m_ref, acc_ref, *, nkb):
    i = pl.program_id(2)
    L, tq = MOBA_L, MOBA_TQ

    @pl.when(i == 0)
    def _():
        kmean_ref[...] = jnp.zeros_like(kmean_ref)
        for n in range(nkb):
            kmean_ref[64 + n:65 + n, :] = jnp.sum(k_ref[n * L:(n + 1) * L, :].astype(F32), axis=0,
                                                  keepdims=True) * (1.0 / L)

    lane = lax.broadcasted_iota(jnp.int32, (tq, 128), 1)
    rows = lax.broadcasted_iota(jnp.int32, (128, tq), 0)
    own = 2 * i + jnp.where(lax.broadcasted_iota(jnp.int32, (128, tq), 1) >= L, 1, 0)
    blk = rows - 64
    rows_f = rows.astype(F32)
    for j in range(2):
        qj = q_ref[:, 128 * j:128 * (j + 1)]
        gsc = lax.dot_general(kmean_ref[:, 128 * j:128 * (j + 1)], qj.astype(F32), (((1,), (1,)), ((), ())),
                              preferred_element_type=F32, precision=lax.Precision.HIGHEST)
        g = jnp.where((blk >= 0) & (blk < own), gsc, NEG)
        allowed = blk == own
        for _ in range(MOBA_TOPK):
            mx = jnp.max(g, axis=0, keepdims=True)
            idx = jnp.min(jnp.where(g == mx, rows_f, 1e6), axis=0, keepdims=True)
            pick = rows_f == idx
            allowed = allowed | (pick & (mx > 0.5 * NEG))
            g = jnp.where(pick, 2.0 * NEG, g)
        qa_ref[j] = jnp.where(lane < HD, qj, jnp.where(allowed, 0.0, NEG).T.astype(BF16))

    def bias_of(start, j):
        parts = []
        for c in range(tq // L):
            off = jnp.minimum(i * tq - start - c * L, BIAS_CONST_FROM) + BIAS_PAD
            parts.append(tb_ref[j, :, pl.ds(pl.multiple_of(off, L), tq)])
        return jnp.concatenate(parts, axis=0)

    _causal_attention_t(i, lambda j: qa_ref[j], k_ref, vt_ref, o_ref, s_ref, smax_ref, m_ref, acc_ref, tq=tq, tk=tq,
                        bias_of=bias_of)


def _moba_attention(q, k, vt, tb, B, T):
    tq = MOBA_TQ
    nq = T // tq
    return pl.pallas_call(
        functools.partial(_moba_kernel, nkb=T // MOBA_L),
        grid=(B, 4, nq),
        in_specs=[pl.BlockSpec((tq, 256), lambda b, p, i: (b * nq + i, p)),
                  pl.BlockSpec((T, 256), lambda b, p, i: (b, p)),
                  pl.BlockSpec((None, 2 * VT_ROWS, T), lambda b, p, i: (b, p, 0)),
                  pl.BlockSpec((2, MOBA_L, BIAS_ROWS), lambda b, p, i: (p, 0, 0))],
        out_specs=pl.BlockSpec((tq, 128), lambda b, p, i: (b * nq + i, p)),
        out_shape=jax.ShapeDtypeStruct((B * T, 512), BF16),
        scratch_shapes=[pltpu.VMEM((128, 256), F32), pltpu.VMEM((2, tq, 128), BF16)]
        + _attention_t_scratch(tq, tq),
        compiler_params=_cparams(("arbitrary", "arbitrary", "arbitrary")),
        name="moba_attention",
    )(q, k, vt, tb)


def _sb_kernel(q_ref, k_ref, vt_ref, o_ref, *, tq, pairs):
    i = pl.program_id(2)
    lane = lax.broadcasted_iota(jnp.int32, (tq, 128), 1)
    qs = []
    for p in range(pairs):
        q_pair = q_ref[:, 128 * p:128 * (p + 1)]
        qs += [jnp.where(lane < HD, q_pair, 0).astype(BF16), jnp.where(lane >= HD, q_pair, 0).astype(BF16)]
    key = lax.broadcasted_iota(jnp.int32, (tq, tq), 0)
    qry = lax.broadcasted_iota(jnp.int32, (tq, tq), 1)
    past = key < qry
    suffix = jnp.where(qry >= key, -1.0, 0.0).astype(BF16)
    heads = range(2 * pairs)

    def step(kb, carry, boundary):
        start = pl.multiple_of(kb * tq, tq)
        zs = [_dot_nt(k_ref[pl.ds(start, tq), 128 * (h // 2):128 * (h // 2 + 1)], qs[h]) for h in heads]
        parts = []
        for z in zs:
            soft = jnp.maximum(z, 0.0) + jnp.log2(1.0 + jnp.exp2(-jnp.abs(z)))
            if boundary:
                soft = jnp.where(past, soft, 0.0)
            hi = soft.astype(BF16)
            parts.append((hi, (soft - hi.astype(F32)).astype(BF16)))
        sums = [_dot(suffix, hi) + _dot(suffix, lo) for hi, lo in parts]
        ws = []
        for h in heads:
            w = jnp.exp2(zs[h] + (carry[h][0] + sums[h]))
            ws.append((jnp.where(past, w, 0.0) if boundary else w).astype(BF16))
        return tuple((carry[h][0] + sums[h][0:1], carry[h][1]
                      + _dot(vt_ref[128 * (h // 2):128 * (h // 2 + 1), pl.ds(start, tq)], ws[h])) for h in heads)

    def live(carry):
        top = carry[0][0]
        for c, _ in carry[1:]:
            top = jnp.maximum(top, c)
        return (jnp.max(top) > SB_DONE).astype(jnp.int32)

    def body(state):
        t, _, carry = state
        carry = step(i - 1 - t, carry, False)
        return t + 1, live(carry), carry

    init = tuple((jnp.zeros((1, tq), F32), jnp.zeros((128, tq), F32)) for _ in range(2 * pairs))
    carry = step(i, init, True)
    _, _, carry = lax.while_loop(lambda st: (st[0] < i) & (st[1] > 0), body, (jnp.int32(0), live(carry), carry))
    rows = lax.broadcasted_iota(jnp.int32, (128, tq), 0)
    for p in range(pairs):
        ot = jnp.where(rows < HD, carry[2 * p][1], carry[2 * p + 1][1])
        o_ref[:, 128 * p:128 * (p + 1)] = ot.T.astype(BF16)


def _sb_attention(q, k, vt, B, T, tq=256, pairs=4):
    nq = T // tq
    w = 128 * pairs
    return pl.pallas_call(
        functools.partial(_sb_kernel, tq=tq, pairs=pairs),
        grid=(B, 4 // pairs, nq),
        in_specs=[pl.BlockSpec((tq, w), lambda b, p, i: (b * nq + i, p)),
                  pl.BlockSpec((T, w), lambda b, p, i: (b, p)),
                  pl.BlockSpec((None, w, T), lambda b, p, i: (b, p, 0))],
        out_specs=pl.BlockSpec((tq, w), lambda b, p, i: (b * nq + i, p)),
        out_shape=jax.ShapeDtypeStruct((B * T, 512), BF16),
        compiler_params=_cparams(("arbitrary", "arbitrary", "arbitrary")),
        name="sb_attention",
    )(q, k, vt)


def _router_weights(w_group, b_group, w_router, b_router):
    pad = 128 - N_EXP - N_GROUPS
    wr = jnp.concatenate([w_router, w_group, jnp.zeros((D, pad), F32)], axis=1).T
    hi = wr.astype(BF16)
    lo = (wr - hi.astype(F32)).astype(BF16)
    br = jnp.concatenate([b_router, b_group, jnp.zeros((pad,), F32)])[:, None]
    return jnp.concatenate([hi, lo], axis=0), br


def _ffn(h, oa, ob, wo, g, w_group, b_group, w_router, b_router, w_gate, w_up, w_down, layer):
    wr, br = _router_weights(w_group, b_group, w_router, b_router)
    hn, u, ri, rg, cnt = _outproj_router(h, oa, ob, wo.astype(BF16), g[None, :], wr, br)
    d1, d2, P, tile_expert, tile_valid = _dispatch_plan(ri, cnt, h.shape[0])
    xs = _sc_scatter_rows(u, d1, d2, P)
    ys = _moe_experts(xs, tile_expert, tile_valid, w_gate, w_up, w_down, layer)
    return hn, _sc_gather_rows(ys, jnp.concatenate([d1, d2])), rg


def kernel(x, norm_mix, norm_ffn, norm_final, rel_bias_table, w_in_even, g_mla_q, g_mla_kv, w_mla_qb, w_mla_kvb, swa_sinks, w_out_even, w_in_odd, w_out_odd, moe_w_group, moe_b_group, moe_w_router, moe_b_router, moe_w_gate, moe_w_up, moe_w_down):
    B, T, _ = x.shape
    h = x.reshape(B * T, D)
    tb_swa, tb_moba = _bias_tables(rel_bias_table)

    win, wqa, wqb, wk, wv = _even_weights(w_in_even[0], w_mla_qb[0], w_mla_kvb[0])
    q, k, v, qs8, ks, vs, vsw = _proj_even(h, norm_mix[0][None, :], win, g_mla_q[0][None, :], g_mla_kv[0][None, :],
                                           wqa, wqb, wk, wv, _rope_tables(T), T)
    oa = _mla_attention(q, k, v, B, T)
    ob = _swa_attention(qs8, ks, vs, vsw, swa_sinks[0], tb_swa, B, T)
    h, y12, rg = _ffn(h, oa, ob, w_out_even[0], norm_ffn[0], moe_w_group[0], moe_b_group[0], moe_w_router[0],
                      moe_b_router[0], moe_w_gate, moe_w_up, moe_w_down, 0)

    h, vct, vdt, qc, kc, qd, kd = _combine_proj_odd(h, y12, rg, norm_mix[1][None, :], *_odd_weights(w_in_odd[0]),
                                                   _block_indicator(T), T)
    oc = _moba_attention(qc, kc, vct, tb_moba, B, T)
    od = _sb_attention(qd, kd, vdt, B, T)
    h, y12, rg = _ffn(h, oc, od, w_out_odd[0], norm_ffn[1], moe_w_group[1], moe_b_group[1], moe_w_router[1],
                      moe_b_router[1], moe_w_gate, moe_w_up, moe_w_down, 1)
    out = _combine_final(h, y12, rg, norm_final[None, :])
    return out.reshape(B, T, D)
```

```python
import functools
import math

import numpy as np
import jax
import jax.numpy as jnp
from jax import lax
from jax.experimental import pallas as pl
from jax.experimental.pallas import tpu as pltpu
from jax.experimental.pallas import tpu_sc as plsc

F32 = jnp.float32
BF16 = jnp.bfloat16
NEG = -1e30
EPS = 1e-6

D = 1024
HD = 64
LANES = 128
MLA_HEADS, MLA_QR, MLA_KVR, MLA_NOPE, MLA_ROPE, MLA_V = 8, 256, 128, 64, 32, 64
ROPE_THETA = 10000.0
SWA_HEADS, SWA_KV, SWA_WIN = 8, 2, 128
MOBA_L, MOBA_TOPK = 256, 3
REL_BUCKETS, REL_MAX = 32, 2048
N_GROUPS, EPG, N_EXP, D_EXP = 4, 8, 32, 256
BIAS_CONST_FROM = 1792
MOBA_TQ = 2 * MOBA_L
BIAS_PAD = MOBA_L
BIAS_ROWS = BIAS_PAD + BIAS_CONST_FROM + MOBA_TQ
SB_DONE = -160.0
TM = 512
TM_E = 512
VMEM_LIMIT = 56 * 1024 * 1024
SC_CORES, SC_SUBCORES = 2, 16
SC_CHUNK = 128
HALF = D // 2
VT_ROWS = HD + 16
LOG2E = math.log2(math.e)


def _pack_halves(x):
    lo = pltpu.bitcast(x[:, :HALF].astype(BF16).astype(F32), jnp.uint32) >> 16
    hi = pltpu.bitcast(x[:, HALF:].astype(BF16).astype(F32), jnp.uint32) & jnp.uint32(0xFFFF0000)
    return pltpu.bitcast(lo | hi, jnp.int32)


def _unpack_halves(w):
    w = pltpu.bitcast(w, jnp.uint32)
    return pltpu.bitcast(w << 16, F32), pltpu.bitcast(w & jnp.uint32(0xFFFF0000), F32)


def _cparams(sem):
    return pltpu.CompilerParams(dimension_semantics=sem, vmem_limit_bytes=VMEM_LIMIT)


def _dot(a, b):
    return jnp.dot(a, b, preferred_element_type=F32)


def _dot_nt(a, b):
    return lax.dot_general(a, b, (((1,), (1,)), ((), ())), preferred_element_type=F32)


def _rms(x, g):
    return x * lax.rsqrt(jnp.mean(x * x, axis=-1, keepdims=True) + EPS) * g


def _values_t(vt):
    ones = jnp.ones((VT_ROWS - HD, vt.shape[1]), vt.dtype)
    return jnp.concatenate([blk for h in range(8) for blk in (vt[HD * h:HD * (h + 1)], ones)], axis=0)


def _spread_heads(x, lane_of):
    lane = lax.broadcasted_iota(jnp.int32, (x.shape[0], 128), 1)
    out = []
    for h in range(8):
        pair = x[:, 128 * (h // 2):128 * (h // 2 + 1)]
        if HD * (h % 2) != lane_of(h):
            pair = pltpu.roll(pair, HD, 1)
        out.append(jnp.where((lane >= lane_of(h)) & (lane < lane_of(h) + HD), pair, 0.0))
    return out


def _flash_update_t(st, vt, m, acc):
    m_new = jnp.maximum(m, jnp.max(st, axis=0, keepdims=True))
    alpha = jnp.exp2(m - m_new)
    pt = jnp.exp2(st - m_new).astype(BF16)
    return m_new, alpha * acc + _dot(vt, pt)


def _flash_update(s, v, m, l, acc):
    m_new = jnp.maximum(m, jnp.max(s, axis=-1, keepdims=True))
    alpha = jnp.exp(m - m_new)
    p = jnp.exp(s - m_new)
    l = alpha * l + jnp.sum(p, axis=-1, keepdims=True)
    acc = alpha * acc + _dot(p.astype(BF16), v)
    return m_new, l, acc


def _t5_bucket_np(dist):
    n = np.maximum(dist, 0)
    max_exact = REL_BUCKETS // 2
    nf = np.maximum(n, 1).astype(np.float32)
    large = max_exact + (np.log(nf / np.float32(max_exact)) / np.float32(math.log(REL_MAX / max_exact))
                         * np.float32(REL_BUCKETS - max_exact)).astype(np.int32)
    large = np.minimum(large, REL_BUCKETS - 1)
    return np.where(n < max_exact, n, large).astype(np.int32)


def _bias_kernel(table_ref, bucket_ref, out_ref, *, scale, ranges):
    b = bucket_ref[...]
    for m, (lo, hi) in enumerate(ranges):
        @pl.when(pl.program_id(0) == m)
        def _():
            for h in range(8):
                acc = jnp.full(b.shape, table_ref[lo, h] * scale, F32)
                for k in range(lo + 1, hi + 1):
                    acc = jnp.where(b == k, table_ref[k, h] * scale, acc)
                out_ref[h] = acc


def _bias_lookup(rel_table, dist, scale, name):
    R, C = dist.shape
    buckets = _t5_bucket_np(dist)
    ranges = tuple((int(buckets[:, c:c + 256].min()), int(buckets[:, c:c + 256].max())) for c in range(0, C, 256))
    return pl.pallas_call(
        functools.partial(_bias_kernel, scale=scale, ranges=ranges),
        grid=(C // 256,),
        in_specs=[pl.BlockSpec(memory_space=pltpu.SMEM),
                  pl.BlockSpec((R, 256), lambda m: (0, m))],
        out_specs=pl.BlockSpec((8, R, 256), lambda m: (0, 0, m)),
        out_shape=jax.ShapeDtypeStruct((8, R, C), F32),
        compiler_params=_cparams(("arbitrary",)),
        name=name,
    )(rel_table, jnp.asarray(buckets))


def _bias_tables(rel_table):
    swa = _bias_lookup(rel_table, np.arange(256)[:, None] - np.arange(256)[None, :], 1.0, "bias_swa")
    moba = _bias_lookup(rel_table, np.arange(BIAS_ROWS)[None, :] - BIAS_PAD - np.arange(MOBA_L)[:, None], LOG2E,
                        "bias_moba")
    return swa, moba


def _proj_even_kernel(h_ref, g_ref, win_ref, gq_ref, gkv_ref, wqa_ref, wqb_ref, wk_ref, wv_ref,
                      cq_ref, sq_ref, ck_ref, sk_ref,
                      q_ref, k_ref, v_ref, qs_ref, ks_ref, vs_ref, vsw_ref):
    u = _rms(h_ref[...], g_ref[...]).astype(BF16)
    nq = _rms(_dot(u, win_ref[:, 0:256]), gq_ref[...]).astype(BF16)
    qa = _dot(nq, wqa_ref[...])
    qb = _dot(nq, wqb_ref[...])
    cq, sq = cq_ref[...], sq_ref[...]
    for hh in range(MLA_HEADS):
        sl = slice(128 * hh, 128 * (hh + 1))
        q_ref[:, sl] = (qa[:, sl] * cq + qb[:, sl] * sq).astype(BF16)
    nkv = _rms(_dot(u, win_ref[:, 256:384]), gkv_ref[...]).astype(BF16)
    kn = _dot(nkv, wk_ref[...])
    v_ref[...] = _values_t(_dot_nt(wv_ref[...], nkv)).astype(BF16)
    rest = _dot(u, win_ref[:, 896:1536])
    kr = rest[:, 384:512] * ck_ref[...] + rest[:, 512:640] * sk_ref[...]
    for hh in range(MLA_HEADS):
        sl = slice(128 * hh, 128 * (hh + 1))
        k_ref[:, sl] = (kn[:, sl] + kr).astype(BF16)
    qs = _spread_heads(_dot(u, win_ref[:, 384:896]), lambda hq: HD * (hq // (SWA_HEADS // SWA_KV)))
    for hq in range(SWA_HEADS):
        qs_ref[:, 128 * hq:128 * (hq + 1)] = qs[hq].astype(BF16)
    ks_ref[...] = rest[:, 0:128].astype(BF16)
    vs_ref[...] = rest[:, 128:256].astype(BF16)
    vsw_ref[...] = rest[:, 256:384].astype(BF16)


def _cols(w, idx, scale=None):
    idx = np.asarray(idx)
    cuts = [0] + [n for n in range(1, len(idx)) if (idx[n] < 0) != (idx[n - 1] < 0)
                  or (idx[n] >= 0 and idx[n] != idx[n - 1] + 1)] + [len(idx)]
    parts = [jnp.zeros((w.shape[0], b - a), w.dtype) if idx[a] < 0 else w[:, idx[a]:idx[a] + b - a]
             for a, b in zip(cuts[:-1], cuts[1:])]
    out = jnp.concatenate(parts, axis=1)
    if scale is not None:
        out = out * scale
    return out


def _even_weights(w_in, w_qb, w_kvb):
    o_cq, o_ckv, o_kr = 0, MLA_QR, MLA_QR + MLA_KVR
    o_qs = o_kr + MLA_ROPE
    o_ks = o_qs + SWA_HEADS * HD
    o_vs = o_ks + SWA_KV * HD
    idx = list(range(o_cq, o_cq + MLA_QR)) + list(range(o_ckv, o_ckv + MLA_KVR))
    scale = [1.0] * len(idx)
    idx += list(range(o_qs, o_qs + SWA_HEADS * HD))
    scale += [HD ** -0.5] * (SWA_HEADS * HD)
    idx += list(range(o_ks, o_ks + 128))
    idx += list(range(o_vs, o_vs + 128))
    idx += list(range(o_vs + 64, o_vs + 128)) + list(range(o_vs, o_vs + 64))
    scale += [1.0] * 384
    half = MLA_ROPE // 2
    idx += [-1] * 64 + list(range(o_kr, o_kr + MLA_ROPE)) + [-1] * 32
    idx += [-1] * 64 + list(range(o_kr + half, o_kr + MLA_ROPE)) + list(range(o_kr, o_kr + half)) + [-1] * 32
    scale += [1.0] * 256
    win = _cols(w_in, idx, jnp.asarray(np.asarray(scale, np.float32))).astype(BF16)
    qw = MLA_NOPE + MLA_ROPE
    ia, ib, ik, iv = [], [], [], []
    for h in range(MLA_HEADS):
        ia += list(range(qw * h, qw * h + qw)) + [-1] * 32
        ib += [-1] * 64 + list(range(qw * h + MLA_NOPE + half, qw * h + qw)) \
            + list(range(qw * h + MLA_NOPE, qw * h + MLA_NOPE + half)) + [-1] * 32
        ik += list(range(128 * h, 128 * h + MLA_NOPE)) + [-1] * 64
        iv += list(range(128 * h + MLA_NOPE, 128 * h + 128))
    return (win, _cols(w_qb, ia).astype(BF16), _cols(w_qb, ib).astype(BF16),
            _cols(w_kvb, ik).astype(BF16), _cols(w_kvb, iv).T.astype(BF16))


def _rope_tables(T):
    f32 = np.float32
    freqs = f32(ROPE_THETA) ** (-np.arange(0, MLA_ROPE, 2, dtype=f32) / f32(MLA_ROPE))
    ang = np.arange(T, dtype=f32)[:, None] * freqs[None, :]
    cos, sin = np.cos(ang), np.sin(ang)
    z = lambda n: np.zeros((T, n), f32)
    scale = f32((MLA_NOPE + MLA_ROPE) ** -0.5 * LOG2E)
    ck = np.concatenate([z(64), cos, cos, z(32)], axis=1)
    sk = np.concatenate([z(64), -sin, sin, z(32)], axis=1)
    cq = np.concatenate([np.ones((T, 64), f32), cos, cos, z(32)], axis=1) * scale
    sq = sk * scale
    return tuple(jnp.asarray(t, F32) for t in (cq, sq, ck, sk))


def _proj_even(h, g, win, gq, gkv, wqa, wqb, wk, wv, tables, T):
    N = h.shape[0]
    nt = T // TM
    full = lambda a: pl.BlockSpec(a.shape, lambda i: (0,) * a.ndim)
    tab = pl.BlockSpec((TM, 128), lambda i: (i % nt, 0))
    row = lambda w: pl.BlockSpec((TM, w), lambda i: (i, 0))
    widths = (1024, 1024, None, 1024, 128, 128, 128)
    vt_spec = pl.BlockSpec((None, 8 * VT_ROWS, TM), lambda i: (i // nt, 0, i % nt))
    vt_shape = jax.ShapeDtypeStruct((N // T, 8 * VT_ROWS, T), BF16)
    return pl.pallas_call(
        _proj_even_kernel,
        grid=(N // TM,),
        in_specs=[row(D), full(g), full(win), full(gq), full(gkv), full(wqa), full(wqb), full(wk), full(wv),
                  tab, tab, tab, tab],
        out_specs=[vt_spec if w is None else row(w) for w in widths],
        out_shape=[vt_shape if w is None else jax.ShapeDtypeStruct((N, w), BF16) for w in widths],
        compiler_params=_cparams(("arbitrary",)),
        name="proj_even",
    )(h, g, win, gq, gkv, wqa, wqb, wk, wv, *tables)


def _finish_t(accs):
    ot = jnp.concatenate([acc[:HD] / acc[HD:HD + 1] for acc in accs], axis=0)
    return ot.T.astype(BF16)


def _causal_attention_t(i, q_of, k_ref, vt_ref, o_ref, s_ref, smax_ref, m_ref, acc_ref, *, tq, tk, bias_of=None):
    n_full = i * (tq // tk)
    key = lax.broadcasted_iota(jnp.int32, (tk, tq), 0)
    qry = i * tq + lax.broadcasted_iota(jnp.int32, (tk, tq), 1)

    def scores_to(slot, start, boundary):
        start = pl.multiple_of(start, tk)
        for j in range(2):
            st = _dot_nt(k_ref[pl.ds(start, tk), 128 * j:128 * (j + 1)], q_of(j))
            if bias_of is not None:
                st = st + bias_of(start, j)
            if boundary:
                st = jnp.where(start + key <= qry, st, NEG)
            s_ref[slot, j] = st
            smax_ref[slot, j] = jnp.max(st, axis=0, keepdims=True)

    def update_from(slot, start):
        start = pl.multiple_of(start, tk)
        m_new = [jnp.maximum(m_ref[j], smax_ref[slot, j]) for j in range(2)]
        pts = [jnp.exp2(s_ref[slot, j] - m_new[j]).astype(BF16) for j in range(2)]
        for j in range(2):
            acc_ref[j] = (jnp.exp2(m_ref[j] - m_new[j]) * acc_ref[j]
                          + _dot(vt_ref[VT_ROWS * j:VT_ROWS * (j + 1), pl.ds(start, tk)], pts[j]))
            m_ref[j] = m_new[j]

    m_ref[...] = jnp.full(m_ref.shape, NEG, F32)
    acc_ref[...] = jnp.zeros(acc_ref.shape, F32)
    for d in range(tq // tk - 1):
        scores_to(0, (n_full + d) * tk, True)
        update_from(0, (n_full + d) * tk)
    first = (n_full + tq // tk - 1) * tk
    scores_to(0, first, True)

    def block_start(k):
        return jnp.where(k == 0, first, (k - 1) * tk)

    def pair(k):
        scores_to(1, block_start(k + 1), False)
        update_from(0, block_start(k))
        scores_to(0, block_start(k + 2), False)
        update_from(1, block_start(k + 1))

    def quad(p, _):
        pair(4 * p)
        pair(4 * p + 2)
        return 0

    lax.fori_loop(0, n_full // 4, quad, 0)

    @pl.when(n_full % 4 >= 2)
    def _():
        pair((n_full // 4) * 4)

    @pl.when(n_full % 2 == 1)
    def _():
        scores_to(1, block_start(n_full), False)
        update_from(0, block_start(n_full - 1))
        update_from(1, block_start(n_full))

    @pl.when(n_full % 2 == 0)
    def _():
        update_from(0, block_start(n_full))

    o_ref[...] = _finish_t([acc_ref[j] for j in range(2)])


def _attention_t_scratch(tq, tk):
    return [pltpu.VMEM((2, 2, tk, tq), F32), pltpu.VMEM((2, 2, 1, tq), F32), pltpu.VMEM((2, 1, tq), F32),
            pltpu.VMEM((2, VT_ROWS, tq), F32)]


def _mla_kernel(q_ref, k_ref, vt_ref, o_ref, s_ref, smax_ref, m_ref, acc_ref, *, tq, tk):
    _causal_attention_t(pl.program_id(2), lambda j: q_ref[:, 128 * j:128 * (j + 1)], k_ref, vt_ref, o_ref,
                        s_ref, smax_ref, m_ref, acc_ref, tq=tq, tk=tk)


def _mla_attention(q, k, vt, B, T, tq=512, tk=512):
    nq = T // tq
    return pl.pallas_call(
        functools.partial(_mla_kernel, tq=tq, tk=tk),
        grid=(B, MLA_HEADS // 2, nq),
        in_specs=[pl.BlockSpec((tq, 256), lambda b, p, i: (b * nq + i, p)),
                  pl.BlockSpec((T, 256), lambda b, p, i: (b, p)),
                  pl.BlockSpec((None, 2 * VT_ROWS, T), lambda b, p, i: (b, p, 0))],
        out_specs=pl.BlockSpec((tq, 128), lambda b, p, i: (b * nq + i, p)),
        out_shape=jax.ShapeDtypeStruct((B * T, MLA_HEADS * MLA_V), BF16),
        scratch_shapes=_attention_t_scratch(tq, tk),
        compiler_params=_cparams(("arbitrary", "arbitrary", "arbitrary")),
        name="mla_attention",
    )(q, k, vt)


def _swa_kernel(sink_ref, q_ref, k_ref, v_ref, vsw_ref, tb_ref, o_ref, *, tq):
    i = pl.program_id(1)
    G = SWA_HEADS // SWA_KV
    qi = lax.broadcasted_iota(jnp.int32, (128, 256), 0)
    kj = lax.broadcasted_iota(jnp.int32, (128, 256), 1)
    lane = lax.broadcasted_iota(jnp.int32, (128, 128), 1)
    for r in range(tq // 128):
        qstart = i * tq + 128 * r
        kstart = pl.multiple_of(jnp.maximum(qstart - 128, 0), 128)
        off = pl.multiple_of(qstart - kstart, 128)
        dist = off + qi - kj
        valid = (dist >= 0) & (dist < SWA_WIN)
        kw = k_ref[pl.ds(kstart, 256), :]
        vw = v_ref[pl.ds(kstart, 256), :]
        vsw = vsw_ref[pl.ds(kstart, 256), :]
        halves = [None] * SWA_HEADS
        for c in range(SWA_KV):
            qg = jnp.concatenate([q_ref[128 * r:128 * (r + 1), 128 * (G * c + g):128 * (G * c + g + 1)]
                                  for g in range(G)], axis=0)
            s_all = _dot_nt(qg, kw)
            for g in range(G):
                hq = G * c + g
                s = s_all[128 * g:128 * (g + 1)] + tb_ref[hq, pl.ds(off, 128), :]
                s = jnp.where(valid, s, NEG)
                sink = sink_ref[hq]
                m = jnp.maximum(jnp.max(s, axis=-1, keepdims=True), sink)
                p = jnp.exp(s - m)
                denom = jnp.sum(p, axis=-1, keepdims=True) + jnp.exp(sink - m)
                p = (p / denom).astype(BF16)
                halves[hq] = _dot(p, vw if c == hq % 2 else vsw)
        for mp in range(SWA_HEADS // 2):
            o_ref[128 * r:128 * (r + 1), 128 * mp:128 * (mp + 1)] = jnp.where(
                lane < HD, halves[2 * mp], halves[2 * mp + 1]).astype(BF16)


def _swa_attention(q8, k, v, vsw, sinks, tb, B, T, tq=512):
    nq = T // tq
    kv = pl.BlockSpec((T, 128), lambda b, i: (b, 0))
    return pl.pallas_call(
        functools.partial(_swa_kernel, tq=tq),
        grid=(B, nq),
        in_specs=[pl.BlockSpec(memory_space=pltpu.SMEM),
                  pl.BlockSpec((tq, 1024), lambda b, i: (b * nq + i, 0)),
                  kv, kv, kv,
                  pl.BlockSpec((8, 256, 256), lambda b, i: (0, 0, 0))],
        out_specs=pl.BlockSpec((tq, 512), lambda b, i: (b * nq + i, 0)),
        out_shape=jax.ShapeDtypeStruct((B * T, SWA_HEADS * HD), BF16),
        compiler_params=_cparams(("arbitrary", "arbitrary")),
        name="swa_attention",
    )(sinks, q8, k, v, vsw, tb)


def _outproj_router_kernel(h_ref, oa_ref, ob_ref, wo_ref, g_ref, wr_ref, br_ref,
                           hn_ref, u_ref, ri_ref, rg_ref, cnt_ref, base_ref):
    i = pl.program_id(0)

    @pl.when(i == 0)
    def _():
        base_ref[...] = jnp.zeros_like(base_ref)

    hn = h_ref[...] + _dot(oa_ref[...], wo_ref[0:512, :]) + _dot(ob_ref[...], wo_ref[512:1024, :])
    hn_ref[...] = hn
    u = _rms(hn, g_ref[...])
    u_ref[...] = _pack_halves(u)
    u_hi = u.astype(BF16)
    u_lo = (u - u_hi.astype(F32)).astype(BF16)
    both = _dot_nt(wr_ref[...], u_hi)
    logits = both[:128] + both[128:] + _dot_nt(wr_ref[0:128, :], u_lo) + br_ref[...]
    tm = logits.shape[1]
    sub = lax.broadcasted_iota(jnp.int32, (8, tm), 0).astype(F32)
    big = 1e6
    isg = sub < N_GROUPS
    gl = jnp.where(isg, logits[N_EXP:N_EXP + 8], NEG)
    gmax = jnp.max(gl, axis=0, keepdims=True)
    gsel = jnp.min(jnp.where(gl == gmax, sub, big), axis=0, keepdims=True)
    g_gate = 1.0 / jnp.sum(jnp.where(isg, jnp.exp(gl - gmax), 0.0), axis=0, keepdims=True)
    el = jnp.zeros((EPG, tm), F32)
    for grp in range(N_GROUPS):
        el = jnp.where(gsel == grp, logits[EPG * grp:EPG * (grp + 1)], el)
    m1 = jnp.max(el, axis=0, keepdims=True)
    i1 = jnp.min(jnp.where(el == m1, sub, big), axis=0, keepdims=True)
    el2 = jnp.where(sub == i1, NEG, el)
    m2 = jnp.max(el2, axis=0, keepdims=True)
    i2 = jnp.min(jnp.where(el2 == m2, sub, big), axis=0, keepdims=True)
    e1 = gsel * EPG + i1
    e2 = gsel * EPG + i2
    r = jnp.exp(m2 - m1)
    gate1 = g_gate / (1.0 + r)
    gate2 = g_gate * r / (1.0 + r)
    rows = lax.broadcasted_iota(jnp.int32, (128, tm), 0).astype(F32)
    oh1 = (rows == e1).astype(F32)
    oh2 = (rows == e2).astype(F32)
    oh = oh1 + oh2
    t_row = lax.broadcasted_iota(jnp.int32, (tm, tm), 0)
    t_col = lax.broadcasted_iota(jnp.int32, (tm, tm), 1)
    earlier = jnp.where(t_row < t_col, 1.0, 0.0).astype(BF16)
    prefix = _dot(oh.astype(BF16), earlier) + base_ref[:, 0:1]
    rank1 = jnp.sum(oh1 * prefix, axis=0, keepdims=True)
    rank2 = jnp.sum(oh2 * prefix, axis=0, keepdims=True)
    base_ref[...] = base_ref[...] + jnp.sum(oh, axis=1, keepdims=True)
    cnt_ref[...] = base_ref[...]
    ri_ref[...] = jnp.where(sub == 0, e1, jnp.where(sub == 1, e2, jnp.where(sub == 2, rank1,
                            jnp.where(sub == 3, rank2, 0.0))))
    rg_ref[...] = jnp.where(rows == 0, gate1, jnp.where(rows == 1, gate2, 0.0)).T


def _outproj_router(h, oa, ob, wo, g, wr, br):
    N = h.shape[0]
    full = lambda a: pl.BlockSpec(a.shape, lambda i: (0,) * a.ndim)
    row = lambda w: pl.BlockSpec((TM, w), lambda i: (i, 0))
    return pl.pallas_call(
        _outproj_router_kernel,
        grid=(N // TM,),
        in_specs=[row(D), row(512), row(512), full(wo), full(g), full(wr), full(br)],
        out_specs=[row(D), row(HALF), pl.BlockSpec((8, TM), lambda i: (0, i)), row(128),
                   pl.BlockSpec((128, 128), lambda i: (0, 0))],
        out_shape=[jax.ShapeDtypeStruct((N, D), F32), jax.ShapeDtypeStruct((N, HALF), jnp.int32),
                   jax.ShapeDtypeStruct((8, N), F32), jax.ShapeDtypeStruct((N, 128), F32),
                   jax.ShapeDtypeStruct((128, 128), F32)],
        scratch_shapes=[pltpu.VMEM((128, 128), F32)],
        compiler_params=_cparams(("arbitrary",)),
        name="outproj_router",
    )(h, oa, ob, wo, g, wr, br)


def _moe_kernel(te_ref, tv_ref, x_ref, wg_ref, wu_ref, wd_ref, o_ref, wgu_s, wd_s):
    i = pl.program_id(0)
    valid = tv_ref[i]

    @pl.when((i == 0) | (te_ref[i] != te_ref[jnp.maximum(i - 1, 0)]))
    def _():
        wgu_s[:, :D_EXP] = wg_ref[...].astype(BF16)
        wgu_s[:, D_EXP:] = wu_ref[...].astype(BF16)
        wd_s[...] = wd_ref[...].astype(BF16)

    @pl.when(valid > 0)
    def _():
        rows = lax.broadcasted_iota(jnp.int32, x_ref.shape, 0)
        lo, hi = _unpack_halves(jnp.where(rows < valid, x_ref[...], 0))
        hgu = _dot(lo.astype(BF16), wgu_s[:HALF, :]) + _dot(hi.astype(BF16), wgu_s[HALF:, :])
        hg, hu = hgu[:, :D_EXP], hgu[:, D_EXP:]
        a = hg * (1.0 / (1.0 + jnp.exp(-hg))) * hu
        o_ref[...] = _pack_halves(_dot(a.astype(BF16), wd_s[...]))

    @pl.when(valid <= 0)
    def _():
        o_ref[...] = jnp.zeros_like(o_ref)


def _moe_experts(xs, tile_expert, tile_valid, w_gate, w_up, w_down, layer):
    P = xs.shape[0]
    grid_spec = pltpu.PrefetchScalarGridSpec(
        num_scalar_prefetch=2,
        grid=(P // TM_E,),
        in_specs=[pl.BlockSpec((TM_E, HALF), lambda i, te, tv: (i, 0)),
                  pl.BlockSpec((None, None, D, D_EXP), lambda i, te, tv: (layer, te[i], 0, 0)),
                  pl.BlockSpec((None, None, D, D_EXP), lambda i, te, tv: (layer, te[i], 0, 0)),
                  pl.BlockSpec((None, None, D_EXP, D), lambda i, te, tv: (layer, te[i], 0, 0))],
        out_specs=pl.BlockSpec((TM_E, HALF), lambda i, te, tv: (i, 0)),
        scratch_shapes=[pltpu.VMEM((D, 2 * D_EXP), BF16), pltpu.VMEM((D_EXP, D), BF16)],
    )
    return pl.pallas_call(
        _moe_kernel,
        grid_spec=grid_spec,
        out_shape=jax.ShapeDtypeStruct((P, HALF), jnp.int32),
        compiler_params=_cparams(("arbitrary",)),
        name="moe_experts",
    )(tile_expert, tile_valid, xs, w_gate, w_up, w_down)


def _sc_mesh():
    return plsc.VectorSubcoreMesh(core_axis_name="c", subcore_axis_name="s",
                                  num_cores=SC_CORES, num_subcores=SC_SUBCORES)


def _sc_scatter_rows(src, d1, d2, P):
    N, W = src.shape
    per_w = N // (SC_CORES * SC_SUBCORES)

    @functools.partial(pl.kernel, mesh=_sc_mesh(), out_type=jax.ShapeDtypeStruct((P, W), src.dtype),
                       scratch_types=[pltpu.VMEM((SC_CHUNK,), jnp.int32), pltpu.VMEM((SC_CHUNK, W), src.dtype)],
                       name="sc_scatter_rows")
    def k(src_hbm, d1_hbm, d2_hbm, out_hbm, idx_v, rows_v):
        wid = lax.axis_index("s") * SC_CORES + lax.axis_index("c")

        @pl.loop(0, per_w // SC_CHUNK)
        def _(c):
            off = wid * per_w + c * SC_CHUNK
            pltpu.sync_copy(src_hbm.at[pl.ds(off, SC_CHUNK)], rows_v)
            pltpu.sync_copy(d1_hbm.at[pl.ds(off, SC_CHUNK)], idx_v)
            pltpu.sync_copy(rows_v, out_hbm.at[idx_v])
            pltpu.sync_copy(d2_hbm.at[pl.ds(off, SC_CHUNK)], idx_v)
            pltpu.sync_copy(rows_v, out_hbm.at[idx_v])

    return k(src, d1, d2)


def _sc_gather_rows(table, idx):
    B, W = idx.shape[0], table.shape[1]
    per_w = B // (SC_CORES * SC_SUBCORES)

    @functools.partial(pl.kernel, mesh=_sc_mesh(), out_type=jax.ShapeDtypeStruct((B, W), table.dtype),
                       scratch_types=[pltpu.VMEM((SC_CHUNK,), jnp.int32), pltpu.VMEM((SC_CHUNK, W), table.dtype)],
                       name="sc_gather_rows")
    def k(table_hbm, idx_hbm, out_hbm, idx_v, rows_v):
        wid = lax.axis_index("s") * SC_CORES + lax.axis_index("c")

        @pl.loop(0, per_w // SC_CHUNK)
        def _(c):
            off = wid * per_w + c * SC_CHUNK
            pltpu.sync_copy(idx_hbm.at[pl.ds(off, SC_CHUNK)], idx_v)
            pltpu.sync_copy(table_hbm.at[idx_v], rows_v)
            pltpu.sync_copy(rows_v, out_hbm.at[pl.ds(off, SC_CHUNK)])

    return k(table, idx)


def _dest_kernel(offs_ref, ri_ref, d_ref):
    ri = ri_ref[...]
    base = jnp.zeros(ri.shape, F32)
    for e in range(N_EXP):
        base = jnp.where(ri == e, offs_ref[e].astype(F32), base)
    d_ref[...] = (base + pltpu.roll(ri, 6, 0)).astype(jnp.int32)


def _dispatch_plan(ri, cnt, N):
    counts = cnt[:N_EXP, 0].astype(jnp.int32)
    padded = ((counts + TM_E - 1) // TM_E) * TM_E
    ends = jnp.cumsum(padded)
    offs = ends - padded
    take = lambda t, i: t.at[i].get(mode="promise_in_bounds")
    dest = pl.pallas_call(
        _dest_kernel,
        in_specs=[pl.BlockSpec(memory_space=pltpu.SMEM), pl.BlockSpec((8, N), lambda: (0, 0))],
        out_specs=pl.BlockSpec((8, N), lambda: (0, 0)),
        out_shape=jax.ShapeDtypeStruct((8, N), jnp.int32),
        name="dispatch_rows",
    )(offs, ri)
    d1, d2 = dest[0], dest[1]
    P = 2 * N + N_EXP * TM_E
    tile_start = jnp.arange(P // TM_E, dtype=jnp.int32) * TM_E
    tile_expert = jnp.minimum(jnp.sum((ends[None, :] <= tile_start[:, None]).astype(jnp.int32), axis=1), N_EXP - 1)
    tile_valid = jnp.clip(take(offs + counts, tile_expert) - tile_start, 0, TM_E)
    return d1, d2, P, tile_expert, tile_valid


def _combine(h_ref, y1_ref, y2_ref, rg_ref):
    rg = rg_ref[...]
    lo1, hi1 = _unpack_halves(y1_ref[...])
    lo2, hi2 = _unpack_halves(y2_ref[...])
    g1, g2 = rg[:, 0:1], rg[:, 1:2]
    return h_ref[...] + jnp.concatenate([g1 * lo1 + g2 * lo2, g1 * hi1 + g2 * hi2], axis=1)


ODD_WIDTHS = (1024, 1024, 512, 512)


def _combine_proj_odd_kernel(h_ref, y1_ref, y2_ref, rg_ref, g_ref, win_ref, wvt_ref, ind_ref, hn_ref, vt_ref,
                             vdt_ref, *out_refs):
    hn = _combine(h_ref, y1_ref, y2_ref, rg_ref)
    hn_ref[...] = hn
    u = _rms(hn, g_ref[...]).astype(BF16)
    vt_ref[...] = _values_t(_dot_nt(wvt_ref[0:512, :], u)).astype(BF16)
    vdt_ref[...] = _dot_nt(wvt_ref[512:1024, :], u).astype(BF16)
    qc_ref, kc_ref, qd_ref, kd_ref = out_refs
    qc = _spread_heads(_dot(u, win_ref[:, 0:512]), lambda h: 0)
    kc = _spread_heads(_dot(u, win_ref[:, 512:1024]), lambda h: 0)
    for hh in range(8):
        qc_ref[:, 128 * hh:128 * (hh + 1)] = qc[hh].astype(BF16)
        kc_ref[:, 128 * hh:128 * (hh + 1)] = (kc[hh] + ind_ref[...]).astype(BF16)
    qd_ref[...] = _dot(u, win_ref[:, 1024:1536]).astype(BF16)
    kd_ref[...] = _dot(u, win_ref[:, 1536:2048]).astype(BF16)


def _odd_weights(w_in):
    s = HD ** -0.5 * LOG2E
    scale = np.asarray([s] * 512 + [1.0] * 512 + [s] * 512 + [1.0] * 512, np.float32)
    win = jnp.concatenate([w_in[:, 0:1024], w_in[:, 1536:2560]], axis=1) * jnp.asarray(scale)
    wvt = jnp.concatenate([w_in[:, 1024:1536], w_in[:, 2560:3072]], axis=1).T
    return win.astype(BF16), wvt.astype(BF16)


def _block_indicator(T):
    assert T // MOBA_L <= 64
    ind = np.zeros((T, 128), np.float32)
    ind[np.arange(T), 64 + np.arange(T) // MOBA_L] = 1.0
    return jnp.asarray(ind)


def _y_specs(N):
    return [pl.BlockSpec((TM, HALF), lambda i: (i, 0)), pl.BlockSpec((TM, HALF), lambda i: (i + N // TM, 0))]


def _combine_proj_odd(h, y12, rg, g, win, wvt, ind, T):
    N = h.shape[0]
    nt = T // TM
    full = lambda a: pl.BlockSpec(a.shape, lambda i: (0,) * a.ndim)
    row = lambda w: pl.BlockSpec((TM, w), lambda i: (i, 0))
    return pl.pallas_call(
        _combine_proj_odd_kernel,
        grid=(N // TM,),
        in_specs=[row(D)] + _y_specs(N) + [row(128), full(g), full(win), full(wvt),
                  pl.BlockSpec((TM, 128), lambda i: (i % nt, 0))],
        out_specs=[row(D), pl.BlockSpec((None, 8 * VT_ROWS, TM), lambda i: (i // nt, 0, i % nt)),
                   pl.BlockSpec((None, 512, TM), lambda i: (i // nt, 0, i % nt))]
        + [row(w) for w in ODD_WIDTHS],
        out_shape=[jax.ShapeDtypeStruct((N, D), F32), jax.ShapeDtypeStruct((N // T, 8 * VT_ROWS, T), BF16),
                   jax.ShapeDtypeStruct((N // T, 512, T), BF16)]
        + [jax.ShapeDtypeStruct((N, w), BF16) for w in ODD_WIDTHS],
        compiler_params=_cparams(("arbitrary",)),
        name="combine_proj_odd",
    )(h, y12, y12, rg, g, win, wvt, ind)


def _combine_final_kernel(h_ref, y1_ref, y2_ref, rg_ref, g_ref, o_ref):
    o_ref[...] = _rms(_combine(h_ref, y1_ref, y2_ref, rg_ref), g_ref[...])


def _combine_final(h, y12, rg, g):
    N = h.shape[0]
    row = lambda w: pl.BlockSpec((TM, w), lambda i: (i, 0))
    return pl.pallas_call(
        _combine_final_kernel,
        grid=(N // TM,),
        in_specs=[row(D)] + _y_specs(N) + [row(128), pl.BlockSpec((1, D), lambda i: (0, 0))],
        out_specs=row(D),
        out_shape=jax.ShapeDtypeStruct((N, D), F32),
        compiler_params=_cparams(("arbitrary",)),
        name="combine_final",
    )(h, y12, y12, rg, g)


def _moba_kernel(q_ref, k_ref, vt_ref, tb_ref, o_ref, kmean_ref, qa_ref, s_ref, smax_ref, m_ref, acc_ref, *, nkb):
    i = pl.program_id(2)
    L, tq = MOBA_L, MOBA_TQ

    @pl.when(i == 0)
    def _():
        kmean_ref[...] = jnp.zeros_like(kmean_ref)
        for n in range(nkb):
            kmean_ref[64 + n:65 + n, :] = jnp.sum(k_ref[n * L:(n + 1) * L, :].astype(F32), axis=0,
                                                  keepdims=True) * (1.0 / L)

    lane = lax.broadcasted_iota(jnp.int32, (tq, 128), 1)
    rows = lax.broadcasted_iota(jnp.int32, (128, tq), 0)
    own = 2 * i + jnp.where(lax.broadcasted_iota(jnp.int32, (128, tq), 1) >= L, 1, 0)
    blk = rows - 64
    rows_f = rows.astype(F32)
    for j in range(2):
        qj = q_ref[:, 128 * j:128 * (j + 1)]
        gsc = lax.dot_general(kmean_ref[:, 128 * j:128 * (j + 1)], qj.astype(F32), (((1,), (1,)), ((), ())),
                              preferred_element_type=F32, precision=lax.Precision.HIGHEST)
        g = jnp.where((blk >= 0) & (blk < own), gsc, NEG)
        allowed = blk == own
        for _ in range(MOBA_TOPK):
            mx = jnp.max(g, axis=0, keepdims=True)
            idx = jnp.min(jnp.where(g == mx, rows_f, 1e6), axis=0, keepdims=True)
            pick = rows_f == idx
            allowed = allowed | (pick & (mx > 0.5 * NEG))
            g = jnp.where(pick, 2.0 * NEG, g)
        qa_ref[j] = jnp.where(lane < HD, qj, jnp.where(allowed, 0.0, NEG).T.astype(BF16))

    def bias_of(start, j):
        parts = []
        for c in range(tq // L):
            off = jnp.minimum(i * tq - start - c * L, BIAS_CONST_FROM) + BIAS_PAD
            parts.append(tb_ref[j, :, pl.ds(pl.multiple_of(off, L), tq)])
        return jnp.concatenate(parts, axis=0)

    _causal_attention_t(i, lambda j: qa_ref[j], k_ref, vt_ref, o_ref, s_ref, smax_ref, m_ref, acc_ref, tq=tq, tk=tq,
                        bias_of=bias_of)


def _moba_attention(q, k, vt, tb, B, T):
    tq = MOBA_TQ
    nq = T // tq
    return pl.pallas_call(
        functools.partial(_moba_kernel, nkb=T // MOBA_L),
        grid=(B, 4, nq),
        in_specs=[pl.BlockSpec((tq, 256), lambda b, p, i: (b * nq + i, p)),
                  pl.BlockSpec((T, 256), lambda b, p, i: (b, p)),
                  pl.BlockSpec((None, 2 * VT_ROWS, T), lambda b, p, i: (b, p, 0)),
                  pl.BlockSpec((2, MOBA_L, BIAS_ROWS), lambda b, p, i: (p, 0, 0))],
        out_specs=pl.BlockSpec((tq, 128), lambda b, p, i: (b * nq + i, p)),
        out_shape=jax.ShapeDtypeStruct((B * T, 512), BF16),
        scratch_shapes=[pltpu.VMEM((128, 256), F32), pltpu.VMEM((2, tq, 128), BF16)]
        + _attention_t_scratch(tq, tq),
        compiler_params=_cparams(("arbitrary", "arbitrary", "arbitrary")),
        name="moba_attention",
    )(q, k, vt, tb)


def _sb_kernel(q_ref, k_ref, vt_ref, o_ref, *, tq, pairs):
    i = pl.program_id(2)
    lane = lax.broadcasted_iota(jnp.int32, (tq, 128), 1)
    qs = []
    for p in range(pairs):
        q_pair = q_ref[:, 128 * p:128 * (p + 1)]
        qs += [jnp.where(lane < HD, q_pair, 0).astype(BF16), jnp.where(lane >= HD, q_pair, 0).astype(BF16)]
    key = lax.broadcasted_iota(jnp.int32, (tq, tq), 0)
    qry = lax.broadcasted_iota(jnp.int32, (tq, tq), 1)
    past = key < qry
    suffix = jnp.where(qry >= key, -1.0, 0.0).astype(BF16)
    heads = range(2 * pairs)

    def step(kb, carry, boundary):
        start = pl.multiple_of(kb * tq, tq)
        zs = [_dot_nt(k_ref[pl.ds(start, tq), 128 * (h // 2):128 * (h // 2 + 1)], qs[h]) for h in heads]
        parts = []
        for z in zs:
            soft = jnp.maximum(z, 0.0) + jnp.log2(1.0 + jnp.exp2(-jnp.abs(z)))
            if boundary:
                soft = jnp.where(past, soft, 0.0)
            hi = soft.astype(BF16)
            parts.append((hi, (soft - hi.astype(F32)).astype(BF16)))
        sums = [_dot(suffix, hi) + _dot(suffix, lo) for hi, lo in parts]
        ws = []
        for h in heads:
            w = jnp.exp2(zs[h] + (carry[h][0] + sums[h]))
            ws.append((jnp.where(past, w, 0.0) if boundary else w).astype(BF16))
        return tuple((carry[h][0] + sums[h][0:1], carry[h][1]
                      + _dot(vt_ref[128 * (h // 2):128 * (h // 2 + 1), pl.ds(start, tq)], ws[h])) for h in heads)

    def live(carry):
        top = carry[0][0]
        for c, _ in carry[1:]:
            top = jnp.maximum(top, c)
        return (jnp.max(top) > SB_DONE).astype(jnp.int32)

    def body(state):
        t, _, carry = state
        carry = step(i - 1 - t, carry, False)
        return t + 1, live(carry), carry

    init = tuple((jnp.zeros((1, tq), F32), jnp.zeros((128, tq), F32)) for _ in range(2 * pairs))
    carry = step(i, init, True)
    _, _, carry = lax.while_loop(lambda st: (st[0] < i) & (st[1] > 0), body, (jnp.int32(0), live(carry), carry))
    rows = lax.broadcasted_iota(jnp.int32, (128, tq), 0)
    for p in range(pairs):
        ot = jnp.where(rows < HD, carry[2 * p][1], carry[2 * p + 1][1])
        o_ref[:, 128 * p:128 * (p + 1)] = ot.T.astype(BF16)


def _sb_attention(q, k, vt, B, T, tq=256, pairs=4):
    nq = T // tq
    w = 128 * pairs
    return pl.pallas_call(
        functools.partial(_sb_kernel, tq=tq, pairs=pairs),
        grid=(B, 4 // pairs, nq),
        in_specs=[pl.BlockSpec((tq, w), lambda b, p, i: (b * nq + i, p)),
                  pl.BlockSpec((T, w), lambda b, p, i: (b, p)),
                  pl.BlockSpec((None, w, T), lambda b, p, i: (b, p, 0))],
        out_specs=pl.BlockSpec((tq, w), lambda b, p, i: (b * nq + i, p)),
        out_shape=jax.ShapeDtypeStruct((B * T, 512), BF16),
        compiler_params=_cparams(("arbitrary", "arbitrary", "arbitrary")),
        name="sb_attention",
    )(q, k, vt)


def _router_weights(w_group, b_group, w_router, b_router):
    pad = 128 - N_EXP - N_GROUPS
    wr = jnp.concatenate([w_router, w_group, jnp.zeros((D, pad), F32)], axis=1).T
    hi = wr.astype(BF16)
    lo = (wr - hi.astype(F32)).astype(BF16)
    br = jnp.concatenate([b_router, b_group, jnp.zeros((pad,), F32)])[:, None]
    return jnp.concatenate([hi, lo], axis=0), br


def _ffn(h, oa, ob, wo, g, w_group, b_group, w_router, b_router, w_gate, w_up, w_down, layer):
    wr, br = _router_weights(w_group, b_group, w_router, b_router)
    hn, u, ri, rg, cnt = _outproj_router(h, oa, ob, wo.astype(BF16), g[None, :], wr, br)
    d1, d2, P, tile_expert, tile_valid = _dispatch_plan(ri, cnt, h.shape[0])
    xs = _sc_scatter_rows(u, d1, d2, P)
    ys = _moe_experts(xs, tile_expert, tile_valid, w_gate, w_up, w_down, layer)
    return hn, _sc_gather_rows(ys, jnp.concatenate([d1, d2])), rg


def kernel(x, norm_mix, norm_ffn, norm_final, rel_bias_table, w_in_even, g_mla_q, g_mla_kv, w_mla_qb, w_mla_kvb, swa_sinks, w_out_even, w_in_odd, w_out_odd, moe_w_group, moe_b_group, moe_w_router, moe_b_router, moe_w_gate, moe_w_up, moe_w_down):
    B, T, _ = x.shape
    h = x.reshape(B * T, D)
    tb_swa, tb_moba = _bias_tables(rel_bias_table)

    win, wqa, wqb, wk, wv = _even_weights(w_in_even[0], w_mla_qb[0], w_mla_kvb[0])
    q, k, v, qs8, ks, vs, vsw = _proj_even(h, norm_mix[0][None, :], win, g_mla_q[0][None, :], g_mla_kv[0][None, :],
                                           wqa, wqb, wk, wv, _rope_tables(T), T)
    oa = _mla_attention(q, k, v, B, T)
    ob = _swa_attention(qs8, ks, vs, vsw, swa_sinks[0], tb_swa, B, T)
    h, y12, rg = _ffn(h, oa, ob, w_out_even[0], norm_ffn[0], moe_w_group[0], moe_b_group[0], moe_w_router[0],
                      moe_b_router[0], moe_w_gate, moe_w_up, moe_w_down, 0)

    h, vct, vdt, qc, kc, qd, kd = _combine_proj_odd(h, y12, rg, norm_mix[1][None, :], *_odd_weights(w_in_odd[0]),
                                                   _block_indicator(T), T)
    oc = _moba_attention(qc, kc, vct, tb_moba, B, T)
    od = _sb_attention(qd, kd, vdt, B, T)
    h, y12, rg = _ffn(h, oc, od, w_out_odd[0], norm_ffn[1], moe_w_group[1], moe_b_group[1], moe_w_router[1],
                      moe_b_router[1], moe_w_gate, moe_w_up, moe_w_down, 1)
    out = _combine_final(h, y12, rg, norm_final[None, :])
    return out.reshape(B, T, D)
```

```python
import functools
import math

import numpy as np
import jax
import jax.numpy as jnp
from jax import lax
from jax.experimental import pallas as pl
from jax.experimental.pallas import tpu as pltpu
from jax.experimental.pallas import tpu_sc as plsc

F32 = jnp.float32
BF16 = jnp.bfloat16
NEG = -1e30
EPS = 1e-6

D = 1024
HD = 64
LANES = 128
MLA_HEADS, MLA_QR, MLA_KVR, MLA_NOPE, MLA_ROPE, MLA_V = 8, 256, 128, 64, 32, 64
ROPE_THETA = 10000.0
SWA_HEADS, SWA_KV, SWA_WIN = 8, 2, 128
MOBA_L, MOBA_TOPK = 256, 3
REL_BUCKETS, REL_MAX = 32, 2048
N_GROUPS, EPG, N_EXP, D_EXP = 4, 8, 32, 256
BIAS_CONST_FROM = 1792
MOBA_TQ = 2 * MOBA_L
BIAS_PAD = MOBA_L
BIAS_ROWS = BIAS_PAD + BIAS_CONST_FROM + MOBA_TQ
SB_DONE = -160.0
TM = 512
TM_E = 512
VMEM_LIMIT = 56 * 1024 * 1024
SC_CORES, SC_SUBCORES = 2, 16
SC_CHUNK = 128
HALF = D // 2
VT_ROWS = HD + 16
LOG2E = math.log2(math.e)


def _pack_halves(x):
    lo = pltpu.bitcast(x[:, :HALF].astype(BF16).astype(F32), jnp.uint32) >> 16
    hi = pltpu.bitcast(x[:, HALF:].astype(BF16).astype(F32), jnp.uint32) & jnp.uint32(0xFFFF0000)
    return pltpu.bitcast(lo | hi, jnp.int32)


def _unpack_halves(w):
    w = pltpu.bitcast(w, jnp.uint32)
    return pltpu.bitcast(w << 16, F32), pltpu.bitcast(w & jnp.uint32(0xFFFF0000), F32)


def _cparams(sem):
    return pltpu.CompilerParams(dimension_semantics=sem, vmem_limit_bytes=VMEM_LIMIT)


def _dot(a, b):
    return jnp.dot(a, b, preferred_element_type=F32)


def _dot_nt(a, b):
    return lax.dot_general(a, b, (((1,), (1,)), ((), ())), preferred_element_type=F32)


def _rms(x, g):
    return x * lax.rsqrt(jnp.mean(x * x, axis=-1, keepdims=True) + EPS) * g


def _values_t(vt):
    ones = jnp.ones((VT_ROWS - HD, vt.shape[1]), vt.dtype)
    return jnp.concatenate([blk for h in range(8) for blk in (vt[HD * h:HD * (h + 1)], ones)], axis=0)


def _spread_heads(x, lane_of):
    lane = lax.broadcasted_iota(jnp.int32, (x.shape[0], 128), 1)
    out = []
    for h in range(8):
        pair = x[:, 128 * (h // 2):128 * (h // 2 + 1)]
        if HD * (h % 2) != lane_of(h):
            pair = pltpu.roll(pair, HD, 1)
        out.append(jnp.where((lane >= lane_of(h)) & (lane < lane_of(h) + HD), pair, 0.0))
    return out


def _flash_update_t(st, vt, m, acc):
    m_new = jnp.maximum(m, jnp.max(st, axis=0, keepdims=True))
    alpha = jnp.exp2(m - m_new)
    pt = jnp.exp2(st - m_new).astype(BF16)
    return m_new, alpha * acc + _dot(vt, pt)


def _flash_update(s, v, m, l, acc):
    m_new = jnp.maximum(m, jnp.max(s, axis=-1, keepdims=True))
    alpha = jnp.exp(m - m_new)
    p = jnp.exp(s - m_new)
    l = alpha * l + jnp.sum(p, axis=-1, keepdims=True)
    acc = alpha * acc + _dot(p.astype(BF16), v)
    return m_new, l, acc


def _t5_bucket_np(dist):
    n = np.maximum(dist, 0)
    max_exact = REL_BUCKETS // 2
    nf = np.maximum(n, 1).astype(np.float32)
    large = max_exact + (np.log(nf / np.float32(max_exact)) / np.float32(math.log(REL_MAX / max_exact))
                         * np.float32(REL_BUCKETS - max_exact)).astype(np.int32)
    large = np.minimum(large, REL_BUCKETS - 1)
    return np.where(n < max_exact, n, large).astype(np.int32)


def _bias_kernel(table_ref, bucket_ref, out_ref, *, scale, ranges):
    b = bucket_ref[...]
    for m, (lo, hi) in enumerate(ranges):
        @pl.when(pl.program_id(0) == m)
        def _():
            for h in range(8):
                acc = jnp.full(b.shape, table_ref[lo, h] * scale, F32)
                for k in range(lo + 1, hi + 1):
                    acc = jnp.where(b == k, table_ref[k, h] * scale, acc)
                out_ref[h] = acc


def _bias_lookup(rel_table, dist, scale, name):
    R, C = dist.shape
    buckets = _t5_bucket_np(dist)
    ranges = tuple((int(buckets[:, c:c + 256].min()), int(buckets[:, c:c + 256].max())) for c in range(0, C, 256))
    return pl.pallas_call(
        functools.partial(_bias_kernel, scale=scale, ranges=ranges),
        grid=(C // 256,),
        in_specs=[pl.BlockSpec(memory_space=pltpu.SMEM),
                  pl.BlockSpec((R, 256), lambda m: (0, m))],
        out_specs=pl.BlockSpec((8, R, 256), lambda m: (0, 0, m)),
        out_shape=jax.ShapeDtypeStruct((8, R, C), F32),
        compiler_params=_cparams(("arbitrary",)),
        name=name,
    )(rel_table, jnp.asarray(buckets))


def _bias_tables(rel_table):
    swa = _bias_lookup(rel_table, np.arange(256)[:, None] - np.arange(256)[None, :], 1.0, "bias_swa")
    moba = _bias_lookup(rel_table, np.arange(BIAS_ROWS)[None, :] - BIAS_PAD - np.arange(MOBA_L)[:, None], LOG2E,
                        "bias_moba")
    return swa, moba


def _proj_even_kernel(h_ref, g_ref, win_ref, gq_ref, gkv_ref, wqa_ref, wqb_ref, wk_ref, wv_ref,
                      cq_ref, sq_ref, ck_ref, sk_ref,
                      q_ref, k_ref, v_ref, qs_ref, ks_ref, vs_ref, vsw_ref):
    u = _rms(h_ref[...], g_ref[...]).astype(BF16)
    nq = _rms(_dot(u, win_ref[:, 0:256]), gq_ref[...]).astype(BF16)
    qa = _dot(nq, wqa_ref[...])
    qb = _dot(nq, wqb_ref[...])
    cq, sq = cq_ref[...], sq_ref[...]
    for hh in range(MLA_HEADS):
        sl = slice(128 * hh, 128 * (hh + 1))
        q_ref[:, sl] = (qa[:, sl] * cq + qb[:, sl] * sq).astype(BF16)
    nkv = _rms(_dot(u, win_ref[:, 256:384]), gkv_ref[...]).astype(BF16)
    kn = _dot(nkv, wk_ref[...])
    v_ref[...] = _values_t(_dot_nt(wv_ref[...], nkv)).astype(BF16)
    rest = _dot(u, win_ref[:, 896:1536])
    kr = rest[:, 384:512] * ck_ref[...] + rest[:, 512:640] * sk_ref[...]
    for hh in range(MLA_HEADS):
        sl = slice(128 * hh, 128 * (hh + 1))
        k_ref[:, sl] = (kn[:, sl] + kr).astype(BF16)
    qs = _spread_heads(_dot(u, win_ref[:, 384:896]), lambda hq: HD * (hq // (SWA_HEADS // SWA_KV)))
    for hq in range(SWA_HEADS):
        qs_ref[:, 128 * hq:128 * (hq + 1)] = qs[hq].astype(BF16)
    ks_ref[...] = rest[:, 0:128].astype(BF16)
    vs_ref[...] = rest[:, 128:256].astype(BF16)
    vsw_ref[...] = rest[:, 256:384].astype(BF16)


def _cols(w, idx, scale=None):
    idx = np.asarray(idx)
    cuts = [0] + [n for n in range(1, len(idx)) if (idx[n] < 0) != (idx[n - 1] < 0)
                  or (idx[n] >= 0 and idx[n] != idx[n - 1] + 1)] + [len(idx)]
    parts = [jnp.zeros((w.shape[0], b - a), w.dtype) if idx[a] < 0 else w[:, idx[a]:idx[a] + b - a]
             for a, b in zip(cuts[:-1], cuts[1:])]
    out = jnp.concatenate(parts, axis=1)
    if scale is not None:
        out = out * scale
    return out


def _even_weights(w_in, w_qb, w_kvb):
    o_cq, o_ckv, o_kr = 0, MLA_QR, MLA_QR + MLA_KVR
    o_qs = o_kr + MLA_ROPE
    o_ks = o_qs + SWA_HEADS * HD
    o_vs = o_ks + SWA_KV * HD
    idx = list(range(o_cq, o_cq + MLA_QR)) + list(range(o_ckv, o_ckv + MLA_KVR))
    scale = [1.0] * len(idx)
    idx += list(range(o_qs, o_qs + SWA_HEADS * HD))
    scale += [HD ** -0.5] * (SWA_HEADS * HD)
    idx += list(range(o_ks, o_ks + 128))
    idx += list(range(o_vs, o_vs + 128))
    idx += list(range(o_vs + 64, o_vs + 128)) + list(range(o_vs, o_vs + 64))
    scale += [1.0] * 384
    half = MLA_ROPE // 2
    idx += [-1] * 64 + list(range(o_kr, o_kr + MLA_ROPE)) + [-1] * 32
    idx += [-1] * 64 + list(range(o_kr + half, o_kr + MLA_ROPE)) + list(range(o_kr, o_kr + half)) + [-1] * 32
    scale += [1.0] * 256
    win = _cols(w_in, idx, jnp.asarray(np.asarray(scale, np.float32))).astype(BF16)
    qw = MLA_NOPE + MLA_ROPE
    ia, ib, ik, iv = [], [], [], []
    for h in range(MLA_HEADS):
        ia += list(range(qw * h, qw * h + qw)) + [-1] * 32
        ib += [-1] * 64 + list(range(qw * h + MLA_NOPE + half, qw * h + qw)) \
            + list(range(qw * h + MLA_NOPE, qw * h + MLA_NOPE + half)) + [-1] * 32
        ik += list(range(128 * h, 128 * h + MLA_NOPE)) + [-1] * 64
        iv += list(range(128 * h + MLA_NOPE, 128 * h + 128))
    return (win, _cols(w_qb, ia).astype(BF16), _cols(w_qb, ib).astype(BF16),
            _cols(w_kvb, ik).astype(BF16), _cols(w_kvb, iv).T.astype(BF16))


def _rope_tables(T):
    f32 = np.float32
    freqs = f32(ROPE_THETA) ** (-np.arange(0, MLA_ROPE, 2, dtype=f32) / f32(MLA_ROPE))
    ang = np.arange(T, dtype=f32)[:, None] * freqs[None, :]
    cos, sin = np.cos(ang), np.sin(ang)
    z = lambda n: np.zeros((T, n), f32)
    scale = f32((MLA_NOPE + MLA_ROPE) ** -0.5 * LOG2E)
    ck = np.concatenate([z(64), cos, cos, z(32)], axis=1)
    sk = np.concatenate([z(64), -sin, sin, z(32)], axis=1)
    cq = np.concatenate([np.ones((T, 64), f32), cos, cos, z(32)], axis=1) * scale
    sq = sk * scale
    return tuple(jnp.asarray(t, F32) for t in (cq, sq, ck, sk))


def _proj_even(h, g, win, gq, gkv, wqa, wqb, wk, wv, tables, T):
    N = h.shape[0]
    nt = T // TM
    full = lambda a: pl.BlockSpec(a.shape, lambda i: (0,) * a.ndim)
    tab = pl.BlockSpec((TM, 128), lambda i: (i % nt, 0))
    row = lambda w: pl.BlockSpec((TM, w), lambda i: (i, 0))
    widths = (1024, 1024, None, 1024, 128, 128, 128)
    vt_spec = pl.BlockSpec((None, 8 * VT_ROWS, TM), lambda i: (i // nt, 0, i % nt))
    vt_shape = jax.ShapeDtypeStruct((N // T, 8 * VT_ROWS, T), BF16)
    return pl.pallas_call(
        _proj_even_kernel,
        grid=(N // TM,),
        in_specs=[row(D), full(g), full(win), full(gq), full(gkv), full(wqa), full(wqb), full(wk), full(wv),
                  tab, tab, tab, tab],
        out_specs=[vt_spec if w is None else row(w) for w in widths],
        out_shape=[vt_shape if w is None else jax.ShapeDtypeStruct((N, w), BF16) for w in widths],
        compiler_params=_cparams(("arbitrary",)),
        name="proj_even",
    )(h, g, win, gq, gkv, wqa, wqb, wk, wv, *tables)


def _finish_t(accs):
    ot = jnp.concatenate([acc[:HD] / acc[HD:HD + 1] for acc in accs], axis=0)
    return ot.T.astype(BF16)


def _causal_attention_pair_t(ia, nq, q_of, k_ref, vt_ref, o_ref, s_ref, smax_ref, m_ref, acc_ref, *, tq,
                             bias_of=None):
    first_query = (ia * tq, (nq - 1 - ia) * tq)
    n_items = nq + 1
    assert n_items % 2 == 1
    key = lax.broadcasted_iota(jnp.int32, (tq, tq), 0)
    qry = lax.broadcasted_iota(jnp.int32, (tq, tq), 1)

    def item(p):
        if isinstance(p, int) and p < 2:
            return p, first_query[p], first_query[p]
        t = jnp.where(p - 2 >= ia, 1, 0)
        return t, jnp.where(t == 1, first_query[1], first_query[0]), (p - 2 - t * ia) * tq

    def scores_to(slot, p):
        t, q0, start = item(p)
        start = pl.multiple_of(start, tq)
        for j in range(2):
            st = _dot_nt(k_ref[pl.ds(start, tq), 128 * j:128 * (j + 1)], q_of(t, q0, j))
            if bias_of is not None:
                st = st + bias_of(start, q0, j)
            if isinstance(p, int) and p < 2:
                st = jnp.where(key <= qry, st, NEG)
            s_ref[slot, j] = st
            smax_ref[slot, j] = jnp.max(st, axis=0, keepdims=True)

    def update_from(slot, p):
        t, _, start = item(p)
        start = pl.multiple_of(start, tq)
        m_new = [jnp.maximum(m_ref[t, j], smax_ref[slot, j]) for j in range(2)]
        pts = [jnp.exp2(s_ref[slot, j] - m_new[j]).astype(BF16) for j in range(2)]
        for j in range(2):
            acc_ref[t, j] = (jnp.exp2(m_ref[t, j] - m_new[j]) * acc_ref[t, j]
                             + _dot(vt_ref[VT_ROWS * j:VT_ROWS * (j + 1), pl.ds(start, tq)], pts[j]))
            m_ref[t, j] = m_new[j]

    def pair(k):
        scores_to(1, k + 1)
        update_from(0, k)
        scores_to(0, k + 2)
        update_from(1, k + 1)

    def quad(c, _):
        pair(2 + 4 * c)
        pair(4 + 4 * c)
        return 0

    m_ref[...] = jnp.full(m_ref.shape, NEG, F32)
    acc_ref[...] = jnp.zeros(acc_ref.shape, F32)
    scores_to(0, 0)
    pair(0)
    later_pairs = (n_items - 1) // 2 - 1
    lax.fori_loop(0, later_pairs // 2, quad, 0)
    if later_pairs % 2:
        pair(n_items - 3)
    update_from(0, n_items - 1)
    for t in range(2):
        o_ref[pl.ds(pl.multiple_of(first_query[t], tq), tq), :] = _finish_t([acc_ref[t, j] for j in range(2)])


def _attention_t_scratch(tq):
    return [pltpu.VMEM((2, 2, tq, tq), F32), pltpu.VMEM((2, 2, 1, tq), F32), pltpu.VMEM((2, 2, 1, tq), F32),
            pltpu.VMEM((2, 2, VT_ROWS, tq), F32)]


def _mla_kernel(q_ref, k_ref, vt_ref, o_ref, s_ref, smax_ref, m_ref, acc_ref, *, tq, nq):
    def q_of(t, q0, j):
        return q_ref[pl.ds(pl.multiple_of(q0, tq), tq), 128 * j:128 * (j + 1)]

    _causal_attention_pair_t(pl.program_id(2), nq, q_of, k_ref, vt_ref, o_ref, s_ref, smax_ref, m_ref, acc_ref,
                             tq=tq)


def _mla_attention(q, k, vt, B, T, tq=512):
    nq = T // tq
    return pl.pallas_call(
        functools.partial(_mla_kernel, tq=tq, nq=nq),
        grid=(B, MLA_HEADS // 2, nq // 2),
        in_specs=[pl.BlockSpec((T, 256), lambda b, p, i: (b, p)),
                  pl.BlockSpec((T, 256), lambda b, p, i: (b, p)),
                  pl.BlockSpec((None, 2 * VT_ROWS, T), lambda b, p, i: (b, p, 0))],
        out_specs=pl.BlockSpec((T, 128), lambda b, p, i: (b, p)),
        out_shape=jax.ShapeDtypeStruct((B * T, MLA_HEADS * MLA_V), BF16),
        scratch_shapes=_attention_t_scratch(tq),
        compiler_params=_cparams(("arbitrary", "arbitrary", "arbitrary")),
        name="mla_attention",
    )(q, k, vt)


def _swa_kernel(sink_ref, q_ref, k_ref, v_ref, vsw_ref, tb_ref, o_ref, *, tq):
    i = pl.program_id(1)
    G = SWA_HEADS // SWA_KV
    qi = lax.broadcasted_iota(jnp.int32, (128, 256), 0)
    kj = lax.broadcasted_iota(jnp.int32, (128, 256), 1)
    lane = lax.broadcasted_iota(jnp.int32, (128, 128), 1)
    for r in range(tq // 128):
        qstart = i * tq + 128 * r
        kstart = pl.multiple_of(jnp.maximum(qstart - 128, 0), 128)
        off = pl.multiple_of(qstart - kstart, 128)
        dist = off + qi - kj
        valid = (dist >= 0) & (dist < SWA_WIN)
        kw = k_ref[pl.ds(kstart, 256), :]
        vw = v_ref[pl.ds(kstart, 256), :]
        vsw = vsw_ref[pl.ds(kstart, 256), :]
        halves = [None] * SWA_HEADS
        for c in range(SWA_KV):
            qg = jnp.concatenate([q_ref[128 * r:128 * (r + 1), 128 * (G * c + g):128 * (G * c + g + 1)]
                                  for g in range(G)], axis=0)
            s_all = _dot_nt(qg, kw)
            for g in range(G):
                hq = G * c + g
                s = s_all[128 * g:128 * (g + 1)] + tb_ref[hq, pl.ds(off, 128), :]
                s = jnp.where(valid, s, NEG)
                sink = sink_ref[hq]
                m = jnp.maximum(jnp.max(s, axis=-1, keepdims=True), sink)
                p = jnp.exp(s - m)
                denom = jnp.sum(p, axis=-1, keepdims=True) + jnp.exp(sink - m)
                p = (p / denom).astype(BF16)
                halves[hq] = _dot(p, vw if c == hq % 2 else vsw)
        for mp in range(SWA_HEADS // 2):
            o_ref[128 * r:128 * (r + 1), 128 * mp:128 * (mp + 1)] = jnp.where(
                lane < HD, halves[2 * mp], halves[2 * mp + 1]).astype(BF16)


def _swa_attention(q8, k, v, vsw, sinks, tb, B, T, tq=512):
    nq = T // tq
    kv = pl.BlockSpec((T, 128), lambda b, i: (b, 0))
    return pl.pallas_call(
        functools.partial(_swa_kernel, tq=tq),
        grid=(B, nq),
        in_specs=[pl.BlockSpec(memory_space=pltpu.SMEM),
                  pl.BlockSpec((tq, 1024), lambda b, i: (b * nq + i, 0)),
                  kv, kv, kv,
                  pl.BlockSpec((8, 256, 256), lambda b, i: (0, 0, 0))],
        out_specs=pl.BlockSpec((tq, 512), lambda b, i: (b * nq + i, 0)),
        out_shape=jax.ShapeDtypeStruct((B * T, SWA_HEADS * HD), BF16),
        compiler_params=_cparams(("arbitrary", "arbitrary")),
        name="swa_attention",
    )(sinks, q8, k, v, vsw, tb)


def _outproj_router_kernel(h_ref, oa_ref, ob_ref, wo_ref, g_ref, wr_ref, br_ref,
                           hn_ref, u_ref, ri_ref, rg_ref, cnt_ref, base_ref):
    i = pl.program_id(0)

    @pl.when(i == 0)
    def _():
        base_ref[...] = jnp.zeros_like(base_ref)

    hn = h_ref[...] + _dot(oa_ref[...], wo_ref[0:512, :]) + _dot(ob_ref[...], wo_ref[512:1024, :])
    hn_ref[...] = hn
    u = _rms(hn, g_ref[...])
    u_ref[...] = _pack_halves(u)
    u_hi = u.astype(BF16)
    u_lo = (u - u_hi.astype(F32)).astype(BF16)
    both = _dot_nt(wr_ref[...], u_hi)
    logits = both[:128] + both[128:] + _dot_nt(wr_ref[0:128, :], u_lo) + br_ref[...]
    tm = logits.shape[1]
    sub = lax.broadcasted_iota(jnp.int32, (8, tm), 0).astype(F32)
    big = 1e6
    isg = sub < N_GROUPS
    gl = jnp.where(isg, logits[N_EXP:N_EXP + 8], NEG)
    gmax = jnp.max(gl, axis=0, keepdims=True)
    gsel = jnp.min(jnp.where(gl == gmax, sub, big), axis=0, keepdims=True)
    g_gate = 1.0 / jnp.sum(jnp.where(isg, jnp.exp(gl - gmax), 0.0), axis=0, keepdims=True)
    el = jnp.zeros((EPG, tm), F32)
    for grp in range(N_GROUPS):
        el = jnp.where(gsel == grp, logits[EPG * grp:EPG * (grp + 1)], el)
    m1 = jnp.max(el, axis=0, keepdims=True)
    i1 = jnp.min(jnp.where(el == m1, sub, big), axis=0, keepdims=True)
    el2 = jnp.where(sub == i1, NEG, el)
    m2 = jnp.max(el2, axis=0, keepdims=True)
    i2 = jnp.min(jnp.where(el2 == m2, sub, big), axis=0, keepdims=True)
    e1 = gsel * EPG + i1
    e2 = gsel * EPG + i2
    r = jnp.exp(m2 - m1)
    gate1 = g_gate / (1.0 + r)
    gate2 = g_gate * r / (1.0 + r)
    rows = lax.broadcasted_iota(jnp.int32, (128, tm), 0).astype(F32)
    oh1 = (rows == e1).astype(F32)
    oh2 = (rows == e2).astype(F32)
    oh = oh1 + oh2
    t_row = lax.broadcasted_iota(jnp.int32, (tm, tm), 0)
    t_col = lax.broadcasted_iota(jnp.int32, (tm, tm), 1)
    earlier = jnp.where(t_row < t_col, 1.0, 0.0).astype(BF16)
    prefix = _dot(oh.astype(BF16), earlier) + base_ref[:, 0:1]
    rank1 = jnp.sum(oh1 * prefix, axis=0, keepdims=True)
    rank2 = jnp.sum(oh2 * prefix, axis=0, keepdims=True)
    base_ref[...] = base_ref[...] + jnp.sum(oh, axis=1, keepdims=True)
    cnt_ref[...] = base_ref[...]
    ri_ref[...] = jnp.where(sub == 0, e1, jnp.where(sub == 1, e2, jnp.where(sub == 2, rank1,
                            jnp.where(sub == 3, rank2, 0.0))))
    rg_ref[...] = jnp.where(rows == 0, gate1, jnp.where(rows == 1, gate2, 0.0)).T


def _outproj_router(h, oa, ob, wo, g, wr, br):
    N = h.shape[0]
    full = lambda a: pl.BlockSpec(a.shape, lambda i: (0,) * a.ndim)
    row = lambda w: pl.BlockSpec((TM, w), lambda i: (i, 0))
    return pl.pallas_call(
        _outproj_router_kernel,
        grid=(N // TM,),
        in_specs=[row(D), row(512), row(512), full(wo), full(g), full(wr), full(br)],
        out_specs=[row(D), row(HALF), pl.BlockSpec((8, TM), lambda i: (0, i)), row(128),
                   pl.BlockSpec((128, 128), lambda i: (0, 0))],
        out_shape=[jax.ShapeDtypeStruct((N, D), F32), jax.ShapeDtypeStruct((N, HALF), jnp.int32),
                   jax.ShapeDtypeStruct((8, N), F32), jax.ShapeDtypeStruct((N, 128), F32),
                   jax.ShapeDtypeStruct((128, 128), F32)],
        scratch_shapes=[pltpu.VMEM((128, 128), F32)],
        compiler_params=_cparams(("arbitrary",)),
        name="outproj_router",
    )(h, oa, ob, wo, g, wr, br)


def _moe_kernel(te_ref, tv_ref, x_ref, wg_ref, wu_ref, wd_ref, o_ref, wgu_s, wd_s):
    i = pl.program_id(0)
    valid = tv_ref[i]

    @pl.when((i == 0) | (te_ref[i] != te_ref[jnp.maximum(i - 1, 0)]))
    def _():
        wgu_s[:, :D_EXP] = wg_ref[...].astype(BF16)
        wgu_s[:, D_EXP:] = wu_ref[...].astype(BF16)
        wd_s[...] = wd_ref[...].astype(BF16)

    @pl.when(valid > 0)
    def _():
        rows = lax.broadcasted_iota(jnp.int32, x_ref.shape, 0)
        lo, hi = _unpack_halves(jnp.where(rows < valid, x_ref[...], 0))
        hgu = _dot(lo.astype(BF16), wgu_s[:HALF, :]) + _dot(hi.astype(BF16), wgu_s[HALF:, :])
        hg, hu = hgu[:, :D_EXP], hgu[:, D_EXP:]
        a = hg * (1.0 / (1.0 + jnp.exp(-hg))) * hu
        o_ref[...] = _pack_halves(_dot(a.astype(BF16), wd_s[...]))

    @pl.when(valid <= 0)
    def _():
        o_ref[...] = jnp.zeros_like(o_ref)


def _moe_experts(xs, tile_expert, tile_valid, w_gate, w_up, w_down, layer):
    P = xs.shape[0]
    grid_spec = pltpu.PrefetchScalarGridSpec(
        num_scalar_prefetch=2,
        grid=(P // TM_E,),
        in_specs=[pl.BlockSpec((TM_E, HALF), lambda i, te, tv: (i, 0)),
                  pl.BlockSpec((None, None, D, D_EXP), lambda i, te, tv: (layer, te[i], 0, 0)),
                  pl.BlockSpec((None, None, D, D_EXP), lambda i, te, tv: (layer, te[i], 0, 0)),
                  pl.BlockSpec((None, None, D_EXP, D), lambda i, te, tv: (layer, te[i], 0, 0))],
        out_specs=pl.BlockSpec((TM_E, HALF), lambda i, te, tv: (i, 0)),
        scratch_shapes=[pltpu.VMEM((D, 2 * D_EXP), BF16), pltpu.VMEM((D_EXP, D), BF16)],
    )
    return pl.pallas_call(
        _moe_kernel,
        grid_spec=grid_spec,
        out_shape=jax.ShapeDtypeStruct((P, HALF), jnp.int32),
        compiler_params=_cparams(("arbitrary",)),
        name="moe_experts",
    )(tile_expert, tile_valid, xs, w_gate, w_up, w_down)


def _sc_mesh():
    return plsc.VectorSubcoreMesh(core_axis_name="c", subcore_axis_name="s",
                                  num_cores=SC_CORES, num_subcores=SC_SUBCORES)


def _sc_scatter_rows(src, d1, d2, P):
    N, W = src.shape
    per_w = N // (SC_CORES * SC_SUBCORES)

    @functools.partial(pl.kernel, mesh=_sc_mesh(), out_type=jax.ShapeDtypeStruct((P, W), src.dtype),
                       scratch_types=[pltpu.VMEM((SC_CHUNK,), jnp.int32), pltpu.VMEM((SC_CHUNK, W), src.dtype)],
                       name="sc_scatter_rows")
    def k(src_hbm, d1_hbm, d2_hbm, out_hbm, idx_v, rows_v):
        wid = lax.axis_index("s") * SC_CORES + lax.axis_index("c")

        @pl.loop(0, per_w // SC_CHUNK)
        def _(c):
            off = wid * per_w + c * SC_CHUNK
            pltpu.sync_copy(src_hbm.at[pl.ds(off, SC_CHUNK)], rows_v)
            pltpu.sync_copy(d1_hbm.at[pl.ds(off, SC_CHUNK)], idx_v)
            pltpu.sync_copy(rows_v, out_hbm.at[idx_v])
            pltpu.sync_copy(d2_hbm.at[pl.ds(off, SC_CHUNK)], idx_v)
            pltpu.sync_copy(rows_v, out_hbm.at[idx_v])

    return k(src, d1, d2)


def _sc_gather_rows(table, idx):
    B, W = idx.shape[0], table.shape[1]
    per_w = B // (SC_CORES * SC_SUBCORES)

    @functools.partial(pl.kernel, mesh=_sc_mesh(), out_type=jax.ShapeDtypeStruct((B, W), table.dtype),
                       scratch_types=[pltpu.VMEM((SC_CHUNK,), jnp.int32), pltpu.VMEM((SC_CHUNK, W), table.dtype)],
                       name="sc_gather_rows")
    def k(table_hbm, idx_hbm, out_hbm, idx_v, rows_v):
        wid = lax.axis_index("s") * SC_CORES + lax.axis_index("c")

        @pl.loop(0, per_w // SC_CHUNK)
        def _(c):
            off = wid * per_w + c * SC_CHUNK
            pltpu.sync_copy(idx_hbm.at[pl.ds(off, SC_CHUNK)], idx_v)
            pltpu.sync_copy(table_hbm.at[idx_v], rows_v)
            pltpu.sync_copy(rows_v, out_hbm.at[pl.ds(off, SC_CHUNK)])

    return k(table, idx)


def _dest_kernel(offs_ref, ri_ref, d_ref):
    ri = ri_ref[...]
    base = jnp.zeros(ri.shape, F32)
    for e in range(N_EXP):
        base = jnp.where(ri == e, offs_ref[e].astype(F32), base)
    d_ref[...] = (base + pltpu.roll(ri, 6, 0)).astype(jnp.int32)


def _dispatch_plan(ri, cnt, N):
    counts = cnt[:N_EXP, 0].astype(jnp.int32)
    padded = ((counts + TM_E - 1) // TM_E) * TM_E
    ends = jnp.cumsum(padded)
    offs = ends - padded
    take = lambda t, i: t.at[i].get(mode="promise_in_bounds")
    dest = pl.pallas_call(
        _dest_kernel,
        in_specs=[pl.BlockSpec(memory_space=pltpu.SMEM), pl.BlockSpec((8, N), lambda: (0, 0))],
        out_specs=pl.BlockSpec((8, N), lambda: (0, 0)),
        out_shape=jax.ShapeDtypeStruct((8, N), jnp.int32),
        name="dispatch_rows",
    )(offs, ri)
    d1, d2 = dest[0], dest[1]
    P = 2 * N + N_EXP * TM_E
    tile_start = jnp.arange(P // TM_E, dtype=jnp.int32) * TM_E
    tile_expert = jnp.minimum(jnp.sum((ends[None, :] <= tile_start[:, None]).astype(jnp.int32), axis=1), N_EXP - 1)
    tile_valid = jnp.clip(take(offs + counts, tile_expert) - tile_start, 0, TM_E)
    return d1, d2, P, tile_expert, tile_valid


def _combine(h_ref, y1_ref, y2_ref, rg_ref):
    rg = rg_ref[...]
    lo1, hi1 = _unpack_halves(y1_ref[...])
    lo2, hi2 = _unpack_halves(y2_ref[...])
    g1, g2 = rg[:, 0:1], rg[:, 1:2]
    return h_ref[...] + jnp.concatenate([g1 * lo1 + g2 * lo2, g1 * hi1 + g2 * hi2], axis=1)


ODD_WIDTHS = (1024, 1024, 512, 512)


def _combine_proj_odd_kernel(h_ref, y1_ref, y2_ref, rg_ref, g_ref, win_ref, wvt_ref, ind_ref, hn_ref, vt_ref,
                             vdt_ref, *out_refs):
    hn = _combine(h_ref, y1_ref, y2_ref, rg_ref)
    hn_ref[...] = hn
    u = _rms(hn, g_ref[...]).astype(BF16)
    vt_ref[...] = _values_t(_dot_nt(wvt_ref[0:512, :], u)).astype(BF16)
    vdt_ref[...] = _dot_nt(wvt_ref[512:1024, :], u).astype(BF16)
    qc_ref, kc_ref, qd_ref, kd_ref = out_refs
    qc = _spread_heads(_dot(u, win_ref[:, 0:512]), lambda h: 0)
    kc = _spread_heads(_dot(u, win_ref[:, 512:1024]), lambda h: 0)
    for hh in range(8):
        qc_ref[:, 128 * hh:128 * (hh + 1)] = qc[hh].astype(BF16)
        kc_ref[:, 128 * hh:128 * (hh + 1)] = (kc[hh] + ind_ref[...]).astype(BF16)
    qd_ref[...] = _dot(u, win_ref[:, 1024:1536]).astype(BF16)
    kd_ref[...] = _dot(u, win_ref[:, 1536:2048]).astype(BF16)


def _odd_weights(w_in):
    s = HD ** -0.5 * LOG2E
    scale = np.asarray([s] * 512 + [1.0] * 512 + [s] * 512 + [1.0] * 512, np.float32)
    win = jnp.concatenate([w_in[:, 0:1024], w_in[:, 1536:2560]], axis=1) * jnp.asarray(scale)
    wvt = jnp.concatenate([w_in[:, 1024:1536], w_in[:, 2560:3072]], axis=1).T
    return win.astype(BF16), wvt.astype(BF16)


def _block_indicator(T):
    assert T // MOBA_L <= 64
    ind = np.zeros((T, 128), np.float32)
    ind[np.arange(T), 64 + np.arange(T) // MOBA_L] = 1.0
    return jnp.asarray(ind)


def _y_specs(N):
    return [pl.BlockSpec((TM, HALF), lambda i: (i, 0)), pl.BlockSpec((TM, HALF), lambda i: (i + N // TM, 0))]


def _combine_proj_odd(h, y12, rg, g, win, wvt, ind, T):
    N = h.shape[0]
    nt = T // TM
    full = lambda a: pl.BlockSpec(a.shape, lambda i: (0,) * a.ndim)
    row = lambda w: pl.BlockSpec((TM, w), lambda i: (i, 0))
    return pl.pallas_call(
        _combine_proj_odd_kernel,
        grid=(N // TM,),
        in_specs=[row(D)] + _y_specs(N) + [row(128), full(g), full(win), full(wvt),
                  pl.BlockSpec((TM, 128), lambda i: (i % nt, 0))],
        out_specs=[row(D), pl.BlockSpec((None, 8 * VT_ROWS, TM), lambda i: (i // nt, 0, i % nt)),
                   pl.BlockSpec((None, 512, TM), lambda i: (i // nt, 0, i % nt))]
        + [row(w) for w in ODD_WIDTHS],
        out_shape=[jax.ShapeDtypeStruct((N, D), F32), jax.ShapeDtypeStruct((N // T, 8 * VT_ROWS, T), BF16),
                   jax.ShapeDtypeStruct((N // T, 512, T), BF16)]
        + [jax.ShapeDtypeStruct((N, w), BF16) for w in ODD_WIDTHS],
        compiler_params=_cparams(("arbitrary",)),
        name="combine_proj_odd",
    )(h, y12, y12, rg, g, win, wvt, ind)


def _combine_final_kernel(h_ref, y1_ref, y2_ref, rg_ref, g_ref, o_ref):
    o_ref[...] = _rms(_combine(h_ref, y1_ref, y2_ref, rg_ref), g_ref[...])


def _combine_final(h, y12, rg, g):
    N = h.shape[0]
    row = lambda w: pl.BlockSpec((TM, w), lambda i: (i, 0))
    return pl.pallas_call(
        _combine_final_kernel,
        grid=(N // TM,),
        in_specs=[row(D)] + _y_specs(N) + [row(128), pl.BlockSpec((1, D), lambda i: (0, 0))],
        out_specs=row(D),
        out_shape=jax.ShapeDtypeStruct((N, D), F32),
        compiler_params=_cparams(("arbitrary",)),
        name="combine_final",
    )(h, y12, y12, rg, g)


def _moba_kernel(q_ref, k_ref, vt_ref, tb_ref, o_ref, kmean_ref, qa_ref, s_ref, smax_ref, m_ref, acc_ref, *, nkb,
                 nq):
    ia = pl.program_id(2)
    L, tq = MOBA_L, MOBA_TQ

    @pl.when(ia == 0)
    def _():
        kmean_ref[...] = jnp.zeros_like(kmean_ref)
        for n in range(nkb):
            kmean_ref[64 + n:65 + n, :] = jnp.sum(k_ref[n * L:(n + 1) * L, :].astype(F32), axis=0,
                                                  keepdims=True) * (1.0 / L)

    lane = lax.broadcasted_iota(jnp.int32, (tq, 128), 1)
    rows = lax.broadcasted_iota(jnp.int32, (128, tq), 0)
    half = jnp.where(lax.broadcasted_iota(jnp.int32, (128, tq), 1) >= L, 1, 0)
    blk = rows - 64
    rows_f = rows.astype(F32)
    for t, tile in enumerate((ia, nq - 1 - ia)):
        own = 2 * tile + half
        for j in range(2):
            qj = q_ref[pl.ds(pl.multiple_of(tile * tq, tq), tq), 128 * j:128 * (j + 1)]
            gsc = lax.dot_general(kmean_ref[:, 128 * j:128 * (j + 1)], qj.astype(F32), (((1,), (1,)), ((), ())),
                                  preferred_element_type=F32, precision=lax.Precision.HIGHEST)
            g = jnp.where((blk >= 0) & (blk < own), gsc, NEG)
            allowed = blk == own
            for _ in range(MOBA_TOPK):
                mx = jnp.max(g, axis=0, keepdims=True)
                idx = jnp.min(jnp.where(g == mx, rows_f, 1e6), axis=0, keepdims=True)
                pick = rows_f == idx
                allowed = allowed | (pick & (mx > 0.5 * NEG))
                g = jnp.where(pick, 2.0 * NEG, g)
            qa_ref[t, j] = jnp.where(lane < HD, qj, jnp.where(allowed, 0.0, NEG).T.astype(BF16))

    def bias_of(start, q0, j):
        parts = []
        for c in range(tq // L):
            off = jnp.minimum(q0 - start - c * L, BIAS_CONST_FROM) + BIAS_PAD
            parts.append(tb_ref[j, :, pl.ds(pl.multiple_of(off, L), tq)])
        return jnp.concatenate(parts, axis=0)

    _causal_attention_pair_t(ia, nq, lambda t, q0, j: qa_ref[t, j], k_ref, vt_ref, o_ref, s_ref, smax_ref, m_ref,
                             acc_ref, tq=tq, bias_of=bias_of)


def _moba_attention(q, k, vt, tb, B, T):
    tq = MOBA_TQ
    nq = T // tq
    return pl.pallas_call(
        functools.partial(_moba_kernel, nkb=T // MOBA_L, nq=nq),
        grid=(B, 4, nq // 2),
        in_specs=[pl.BlockSpec((T, 256), lambda b, p, i: (b, p)),
                  pl.BlockSpec((T, 256), lambda b, p, i: (b, p)),
                  pl.BlockSpec((None, 2 * VT_ROWS, T), lambda b, p, i: (b, p, 0)),
                  pl.BlockSpec((2, MOBA_L, BIAS_ROWS), lambda b, p, i: (p, 0, 0))],
        out_specs=pl.BlockSpec((T, 128), lambda b, p, i: (b, p)),
        out_shape=jax.ShapeDtypeStruct((B * T, 512), BF16),
        scratch_shapes=[pltpu.VMEM((128, 256), F32), pltpu.VMEM((2, 2, tq, 128), BF16)]
        + _attention_t_scratch(tq),
        compiler_params=_cparams(("arbitrary", "arbitrary", "arbitrary")),
        name="moba_attention",
    )(q, k, vt, tb)


def _sb_kernel(q_ref, k_ref, vt_ref, o_ref, *, tq, pairs):
    i = pl.program_id(2)
    lane = lax.broadcasted_iota(jnp.int32, (tq, 128), 1)
    qs = []
    for p in range(pairs):
        q_pair = q_ref[:, 128 * p:128 * (p + 1)]
        qs += [jnp.where(lane < HD, q_pair, 0).astype(BF16), jnp.where(lane >= HD, q_pair, 0).astype(BF16)]
    key = lax.broadcasted_iota(jnp.int32, (tq, tq), 0)
    qry = lax.broadcasted_iota(jnp.int32, (tq, tq), 1)
    past = key < qry
    suffix = jnp.where(qry >= key, -1.0, 0.0).astype(BF16)
    heads = range(2 * pairs)

    def step(kb, carry, boundary):
        start = pl.multiple_of(kb * tq, tq)
        zs = [_dot_nt(k_ref[pl.ds(start, tq), 128 * (h // 2):128 * (h // 2 + 1)], qs[h]) for h in heads]
        parts = []
        for z in zs:
            soft = jnp.maximum(z, 0.0) + jnp.log2(1.0 + jnp.exp2(-jnp.abs(z)))
            if boundary:
                soft = jnp.where(past, soft, 0.0)
            hi = soft.astype(BF16)
            parts.append((hi, (soft - hi.astype(F32)).astype(BF16)))
        sums = [_dot(suffix, hi) + _dot(suffix, lo) for hi, lo in parts]
        ws = []
        for h in heads:
            w = jnp.exp2(zs[h] + (carry[h][0] + sums[h]))
            ws.append((jnp.where(past, w, 0.0) if boundary else w).astype(BF16))
        return tuple((carry[h][0] + sums[h][0:1], carry[h][1]
                      + _dot(vt_ref[128 * (h // 2):128 * (h // 2 + 1), pl.ds(start, tq)], ws[h])) for h in heads)

    def live(carry):
        top = carry[0][0]
        for c, _ in carry[1:]:
            top = jnp.maximum(top, c)
        return (jnp.max(top) > SB_DONE).astype(jnp.int32)

    def body(state):
        t, _, carry = state
        carry = step(i - 1 - t, carry, False)
        return t + 1, live(carry), carry

    init = tuple((jnp.zeros((1, tq), F32), jnp.zeros((128, tq), F32)) for _ in range(2 * pairs))
    carry = step(i, init, True)
    _, _, carry = lax.while_loop(lambda st: (st[0] < i) & (st[1] > 0), body, (jnp.int32(0), live(carry), carry))
    rows = lax.broadcasted_iota(jnp.int32, (128, tq), 0)
    for p in range(pairs):
        ot = jnp.where(rows < HD, carry[2 * p][1], carry[2 * p + 1][1])
        o_ref[:, 128 * p:128 * (p + 1)] = ot.T.astype(BF16)


def _sb_attention(q, k, vt, B, T, tq=256, pairs=4):
    nq = T // tq
    w = 128 * pairs
    return pl.pallas_call(
        functools.partial(_sb_kernel, tq=tq, pairs=pairs),
        grid=(B, 4 // pairs, nq),
        in_specs=[pl.BlockSpec((tq, w), lambda b, p, i: (b * nq + i, p)),
                  pl.BlockSpec((T, w), lambda b, p, i: (b, p)),
                  pl.BlockSpec((None, w, T), lambda b, p, i: (b, p, 0))],
        out_specs=pl.BlockSpec((tq, w), lambda b, p, i: (b * nq + i, p)),
        out_shape=jax.ShapeDtypeStruct((B * T, 512), BF16),
        compiler_params=_cparams(("arbitrary", "arbitrary", "arbitrary")),
        name="sb_attention",
    )(q, k, vt)


def _router_weights(w_group, b_group, w_router, b_router):
    pad = 128 - N_EXP - N_GROUPS
    wr = jnp.concatenate([w_router, w_group, jnp.zeros((D, pad), F32)], axis=1).T
    hi = wr.astype(BF16)
    lo = (wr - hi.astype(F32)).astype(BF16)
    br = jnp.concatenate([b_router, b_group, jnp.zeros((pad,), F32)])[:, None]
    return jnp.concatenate([hi, lo], axis=0), br


def _ffn(h, oa, ob, wo, g, w_group, b_group, w_router, b_router, w_gate, w_up, w_down, layer):
    wr, br = _router_weights(w_group, b_group, w_router, b_router)
    hn, u, ri, rg, cnt = _outproj_router(h, oa, ob, wo.astype(BF16), g[None, :], wr, br)
    d1, d2, P, tile_expert, tile_valid = _dispatch_plan(ri, cnt, h.shape[0])
    xs = _sc_scatter_rows(u, d1, d2, P)
    ys = _moe_experts(xs, tile_expert, tile_valid, w_gate, w_up, w_down, layer)
    return hn, _sc_gather_rows(ys, jnp.concatenate([d1, d2])), rg


def kernel(x, norm_mix, norm_ffn, norm_final, rel_bias_table, w_in_even, g_mla_q, g_mla_kv, w_mla_qb, w_mla_kvb, swa_sinks, w_out_even, w_in_odd, w_out_odd, moe_w_group, moe_b_group, moe_w_router, moe_b_router, moe_w_gate, moe_w_up, moe_w_down):
    B, T, _ = x.shape
    h = x.reshape(B * T, D)
    tb_swa, tb_moba = _bias_tables(rel_bias_table)

    win, wqa, wqb, wk, wv = _even_weights(w_in_even[0], w_mla_qb[0], w_mla_kvb[0])
    q, k, v, qs8, ks, vs, vsw = _proj_even(h, norm_mix[0][None, :], win, g_mla_q[0][None, :], g_mla_kv[0][None, :],
                                           wqa, wqb, wk, wv, _rope_tables(T), T)
    oa = _mla_attention(q, k, v, B, T)
    ob = _swa_attention(qs8, ks, vs, vsw, swa_sinks[0], tb_swa, B, T)
    h, y12, rg = _ffn(h, oa, ob, w_out_even[0], norm_ffn[0], moe_w_group[0], moe_b_group[0], moe_w_router[0],
                      moe_b_router[0], moe_w_gate, moe_w_up, moe_w_down, 0)

    h, vct, vdt, qc, kc, qd, kd = _combine_proj_odd(h, y12, rg, norm_mix[1][None, :], *_odd_weights(w_in_odd[0]),
                                                   _block_indicator(T), T)
    oc = _moba_attention(qc, kc, vct, tb_moba, B, T)
    od = _sb_attention(qd, kd, vdt, B, T)
    h, y12, rg = _ffn(h, oc, od, w_out_odd[0], norm_ffn[1], moe_w_group[1], moe_b_group[1], moe_w_router[1],
                      moe_b_router[1], moe_w_gate, moe_w_up, moe_w_down, 1)
    out = _combine_final(h, y12, rg, norm_final[None, :])
    return out.reshape(B, T, D)
```

```python
import functools
import math

import numpy as np
import jax
import jax.numpy as jnp
from jax import lax
from jax.experimental import pallas as pl
from jax.experimental.pallas import tpu as pltpu
from jax.experimental.pallas import tpu_sc as plsc

F32 = jnp.float32
BF16 = jnp.bfloat16
NEG = -1e30
EPS = 1e-6

D = 1024
HD = 64
LANES = 128
MLA_HEADS, MLA_QR, MLA_KVR, MLA_NOPE, MLA_ROPE, MLA_V = 8, 256, 128, 64, 32, 64
ROPE_THETA = 10000.0
SWA_HEADS, SWA_KV, SWA_WIN = 8, 2, 128
MOBA_L, MOBA_TOPK = 256, 3
REL_BUCKETS, REL_MAX = 32, 2048
N_GROUPS, EPG, N_EXP, D_EXP = 4, 8, 32, 256
BIAS_CONST_FROM = 1792
MOBA_TQ = 2 * MOBA_L
BIAS_PAD = MOBA_L
BIAS_ROWS = BIAS_PAD + BIAS_CONST_FROM + MOBA_TQ
SB_DONE = -160.0
TM = 512
TM_E = 512
VMEM_LIMIT = 56 * 1024 * 1024
SC_CORES, SC_SUBCORES = 2, 16
SC_CHUNK = 128
HALF = D // 2
VT_ROWS = HD + 16
LOG2E = math.log2(math.e)


def _pack_halves(x):
    lo = pltpu.bitcast(x[:, :HALF].astype(BF16).astype(F32), jnp.uint32) >> 16
    hi = pltpu.bitcast(x[:, HALF:].astype(BF16).astype(F32), jnp.uint32) & jnp.uint32(0xFFFF0000)
    return pltpu.bitcast(lo | hi, jnp.int32)


def _unpack_halves(w):
    w = pltpu.bitcast(w, jnp.uint32)
    return pltpu.bitcast(w << 16, F32), pltpu.bitcast(w & jnp.uint32(0xFFFF0000), F32)


def _cparams(sem):
    return pltpu.CompilerParams(dimension_semantics=sem, vmem_limit_bytes=VMEM_LIMIT)


def _dot(a, b):
    return jnp.dot(a, b, preferred_element_type=F32)


def _dot_nt(a, b):
    return lax.dot_general(a, b, (((1,), (1,)), ((), ())), preferred_element_type=F32)


def _rms(x, g):
    return x * lax.rsqrt(jnp.mean(x * x, axis=-1, keepdims=True) + EPS) * g


def _values_t(vt):
    ones = jnp.ones((VT_ROWS - HD, vt.shape[1]), vt.dtype)
    return jnp.concatenate([blk for h in range(8) for blk in (vt[HD * h:HD * (h + 1)], ones)], axis=0)


def _spread_heads(x, lane_of):
    lane = lax.broadcasted_iota(jnp.int32, (x.shape[0], 128), 1)
    out = []
    for h in range(8):
        pair = x[:, 128 * (h // 2):128 * (h // 2 + 1)]
        if HD * (h % 2) != lane_of(h):
            pair = pltpu.roll(pair, HD, 1)
        out.append(jnp.where((lane >= lane_of(h)) & (lane < lane_of(h) + HD), pair, 0.0))
    return out


def _flash_update_t(st, vt, m, acc):
    m_new = jnp.maximum(m, jnp.max(st, axis=0, keepdims=True))
    alpha = jnp.exp2(m - m_new)
    pt = jnp.exp2(st - m_new).astype(BF16)
    return m_new, alpha * acc + _dot(vt, pt)


def _flash_update(s, v, m, l, acc):
    m_new = jnp.maximum(m, jnp.max(s, axis=-1, keepdims=True))
    alpha = jnp.exp(m - m_new)
    p = jnp.exp(s - m_new)
    l = alpha * l + jnp.sum(p, axis=-1, keepdims=True)
    acc = alpha * acc + _dot(p.astype(BF16), v)
    return m_new, l, acc


def _t5_bucket_np(dist):
    n = np.maximum(dist, 0)
    max_exact = REL_BUCKETS // 2
    nf = np.maximum(n, 1).astype(np.float32)
    large = max_exact + (np.log(nf / np.float32(max_exact)) / np.float32(math.log(REL_MAX / max_exact))
                         * np.float32(REL_BUCKETS - max_exact)).astype(np.int32)
    large = np.minimum(large, REL_BUCKETS - 1)
    return np.where(n < max_exact, n, large).astype(np.int32)


def _bias_kernel(table_ref, bucket_ref, out_ref, *, scale, ranges):
    b = bucket_ref[...]
    for m, (lo, hi) in enumerate(ranges):
        @pl.when(pl.program_id(0) == m)
        def _():
            for h in range(8):
                acc = jnp.full(b.shape, table_ref[lo, h] * scale, F32)
                for k in range(lo + 1, hi + 1):
                    acc = jnp.where(b == k, table_ref[k, h] * scale, acc)
                out_ref[h] = acc


def _bias_lookup(rel_table, dist, scale, name):
    R, C = dist.shape
    buckets = _t5_bucket_np(dist)
    ranges = tuple((int(buckets[:, c:c + 256].min()), int(buckets[:, c:c + 256].max())) for c in range(0, C, 256))
    return pl.pallas_call(
        functools.partial(_bias_kernel, scale=scale, ranges=ranges),
        grid=(C // 256,),
        in_specs=[pl.BlockSpec(memory_space=pltpu.SMEM),
                  pl.BlockSpec((R, 256), lambda m: (0, m))],
        out_specs=pl.BlockSpec((8, R, 256), lambda m: (0, 0, m)),
        out_shape=jax.ShapeDtypeStruct((8, R, C), F32),
        compiler_params=_cparams(("arbitrary",)),
        name=name,
    )(rel_table, jnp.asarray(buckets))


def _bias_tables(rel_table):
    swa = _bias_lookup(rel_table, np.arange(256)[:, None] - np.arange(256)[None, :], 1.0, "bias_swa")
    moba = _bias_lookup(rel_table, np.arange(BIAS_ROWS)[None, :] - BIAS_PAD - np.arange(MOBA_L)[:, None], LOG2E,
                        "bias_moba")
    return swa, moba


def _proj_even_kernel(h_ref, g_ref, win_ref, gq_ref, gkv_ref, wqa_ref, wqb_ref, wk_ref, wv_ref,
                      cq_ref, sq_ref, ck_ref, sk_ref,
                      q_ref, k_ref, v_ref, qs_ref, ks_ref, vs_ref, vsw_ref):
    u = _rms(h_ref[...], g_ref[...]).astype(BF16)
    nq = _rms(_dot(u, win_ref[:, 0:256]), gq_ref[...]).astype(BF16)
    qa = _dot(nq, wqa_ref[...])
    qb = _dot(nq, wqb_ref[...])
    cq, sq = cq_ref[...], sq_ref[...]
    for hh in range(MLA_HEADS):
        sl = slice(128 * hh, 128 * (hh + 1))
        q_ref[:, sl] = (qa[:, sl] * cq + qb[:, sl] * sq).astype(BF16)
    nkv = _rms(_dot(u, win_ref[:, 256:384]), gkv_ref[...]).astype(BF16)
    kn = _dot(nkv, wk_ref[...])
    v_ref[...] = _values_t(_dot_nt(wv_ref[...], nkv)).astype(BF16)
    rest = _dot(u, win_ref[:, 896:1536])
    kr = rest[:, 384:512] * ck_ref[...] + rest[:, 512:640] * sk_ref[...]
    for hh in range(MLA_HEADS):
        sl = slice(128 * hh, 128 * (hh + 1))
        k_ref[:, sl] = (kn[:, sl] + kr).astype(BF16)
    qs = _spread_heads(_dot(u, win_ref[:, 384:896]), lambda hq: HD * (hq // (SWA_HEADS // SWA_KV)))
    for hq in range(SWA_HEADS):
        qs_ref[:, 128 * hq:128 * (hq + 1)] = qs[hq].astype(BF16)
    ks_ref[...] = rest[:, 0:128].astype(BF16)
    vs_ref[...] = rest[:, 128:256].astype(BF16)
    vsw_ref[...] = rest[:, 256:384].astype(BF16)


def _cols(w, idx, scale=None):
    idx = np.asarray(idx)
    cuts = [0] + [n for n in range(1, len(idx)) if (idx[n] < 0) != (idx[n - 1] < 0)
                  or (idx[n] >= 0 and idx[n] != idx[n - 1] + 1)] + [len(idx)]
    parts = [jnp.zeros((w.shape[0], b - a), w.dtype) if idx[a] < 0 else w[:, idx[a]:idx[a] + b - a]
             for a, b in zip(cuts[:-1], cuts[1:])]
    out = jnp.concatenate(parts, axis=1)
    if scale is not None:
        out = out * scale
    return out


def _even_weights(w_in, w_qb, w_kvb):
    o_cq, o_ckv, o_kr = 0, MLA_QR, MLA_QR + MLA_KVR
    o_qs = o_kr + MLA_ROPE
    o_ks = o_qs + SWA_HEADS * HD
    o_vs = o_ks + SWA_KV * HD
    idx = list(range(o_cq, o_cq + MLA_QR)) + list(range(o_ckv, o_ckv + MLA_KVR))
    scale = [1.0] * len(idx)
    idx += list(range(o_qs, o_qs + SWA_HEADS * HD))
    scale += [HD ** -0.5] * (SWA_HEADS * HD)
    idx += list(range(o_ks, o_ks + 128))
    idx += list(range(o_vs, o_vs + 128))
    idx += list(range(o_vs + 64, o_vs + 128)) + list(range(o_vs, o_vs + 64))
    scale += [1.0] * 384
    half = MLA_ROPE // 2
    idx += [-1] * 64 + list(range(o_kr, o_kr + MLA_ROPE)) + [-1] * 32
    idx += [-1] * 64 + list(range(o_kr + half, o_kr + MLA_ROPE)) + list(range(o_kr, o_kr + half)) + [-1] * 32
    scale += [1.0] * 256
    win = _cols(w_in, idx, jnp.asarray(np.asarray(scale, np.float32))).astype(BF16)
    qw = MLA_NOPE + MLA_ROPE
    ia, ib, ik, iv = [], [], [], []
    for h in range(MLA_HEADS):
        ia += list(range(qw * h, qw * h + qw)) + [-1] * 32
        ib += [-1] * 64 + list(range(qw * h + MLA_NOPE + half, qw * h + qw)) \
            + list(range(qw * h + MLA_NOPE, qw * h + MLA_NOPE + half)) + [-1] * 32
        ik += list(range(128 * h, 128 * h + MLA_NOPE)) + [-1] * 64
        iv += list(range(128 * h + MLA_NOPE, 128 * h + 128))
    return (win, _cols(w_qb, ia).astype(BF16), _cols(w_qb, ib).astype(BF16),
            _cols(w_kvb, ik).astype(BF16), _cols(w_kvb, iv).T.astype(BF16))


def _rope_tables(T):
    f32 = np.float32
    freqs = f32(ROPE_THETA) ** (-np.arange(0, MLA_ROPE, 2, dtype=f32) / f32(MLA_ROPE))
    ang = np.arange(T, dtype=f32)[:, None] * freqs[None, :]
    cos, sin = np.cos(ang), np.sin(ang)
    z = lambda n: np.zeros((T, n), f32)
    scale = f32((MLA_NOPE + MLA_ROPE) ** -0.5 * LOG2E)
    ck = np.concatenate([z(64), cos, cos, z(32)], axis=1)
    sk = np.concatenate([z(64), -sin, sin, z(32)], axis=1)
    cq = np.concatenate([np.ones((T, 64), f32), cos, cos, z(32)], axis=1) * scale
    sq = sk * scale
    return tuple(jnp.asarray(t, F32) for t in (cq, sq, ck, sk))


def _proj_even(h, g, win, gq, gkv, wqa, wqb, wk, wv, tables, T):
    N = h.shape[0]
    nt = T // TM
    full = lambda a: pl.BlockSpec(a.shape, lambda i: (0,) * a.ndim)
    tab = pl.BlockSpec((TM, 128), lambda i: (i % nt, 0))
    row = lambda w: pl.BlockSpec((TM, w), lambda i: (i, 0))
    widths = (1024, 1024, None, 1024, 128, 128, 128)
    vt_spec = pl.BlockSpec((None, 8 * VT_ROWS, TM), lambda i: (i // nt, 0, i % nt))
    vt_shape = jax.ShapeDtypeStruct((N // T, 8 * VT_ROWS, T), BF16)
    return pl.pallas_call(
        _proj_even_kernel,
        grid=(N // TM,),
        in_specs=[row(D), full(g), full(win), full(gq), full(gkv), full(wqa), full(wqb), full(wk), full(wv),
                  tab, tab, tab, tab],
        out_specs=[vt_spec if w is None else row(w) for w in widths],
        out_shape=[vt_shape if w is None else jax.ShapeDtypeStruct((N, w), BF16) for w in widths],
        compiler_params=_cparams(("arbitrary",)),
        name="proj_even",
    )(h, g, win, gq, gkv, wqa, wqb, wk, wv, *tables)


def _finish_t(accs):
    ot = jnp.concatenate([acc[:HD] / acc[HD:HD + 1] for acc in accs], axis=0)
    return ot.T.astype(BF16)


def _causal_attention_pair_t(ia, nq, q_of, k_ref, vt_ref, o_ref, s_ref, smax_ref, m_ref, acc_ref, *, tq,
                             bias_of=None):
    first_query = (ia * tq, (nq - 1 - ia) * tq)
    n_items = nq + 1
    assert n_items % 2 == 1
    key = lax.broadcasted_iota(jnp.int32, (tq, tq), 0)
    qry = lax.broadcasted_iota(jnp.int32, (tq, tq), 1)

    def item(p):
        if isinstance(p, int) and p < 2:
            return p, first_query[p], first_query[p]
        t = jnp.where(p - 2 >= ia, 1, 0)
        return t, jnp.where(t == 1, first_query[1], first_query[0]), (p - 2 - t * ia) * tq

    def scores_to(slot, p):
        t, q0, start = item(p)
        start = pl.multiple_of(start, tq)
        for j in range(2):
            st = _dot_nt(k_ref[pl.ds(start, tq), 128 * j:128 * (j + 1)], q_of(t, q0, j))
            if bias_of is not None:
                st = st + bias_of(start, q0, j)
            if isinstance(p, int) and p < 2:
                st = jnp.where(key <= qry, st, NEG)
            s_ref[slot, j] = st
            smax_ref[slot, j] = jnp.max(st, axis=0, keepdims=True)

    def update_from(slot, p):
        t, _, start = item(p)
        start = pl.multiple_of(start, tq)
        m_new = [jnp.maximum(m_ref[t, j], smax_ref[slot, j]) for j in range(2)]
        pts = [jnp.exp2(s_ref[slot, j] - m_new[j]).astype(BF16) for j in range(2)]
        for j in range(2):
            acc_ref[t, j] = (jnp.exp2(m_ref[t, j] - m_new[j]) * acc_ref[t, j]
                             + _dot(vt_ref[VT_ROWS * j:VT_ROWS * (j + 1), pl.ds(start, tq)], pts[j]))
            m_ref[t, j] = m_new[j]

    def pair(k):
        scores_to(1, k + 1)
        update_from(0, k)
        scores_to(0, k + 2)
        update_from(1, k + 1)

    def quad(c, _):
        pair(2 + 4 * c)
        pair(4 + 4 * c)
        return 0

    m_ref[...] = jnp.full(m_ref.shape, NEG, F32)
    acc_ref[...] = jnp.zeros(acc_ref.shape, F32)
    scores_to(0, 0)
    pair(0)
    later_pairs = (n_items - 1) // 2 - 1
    lax.fori_loop(0, later_pairs // 2, quad, 0)
    if later_pairs % 2:
        pair(n_items - 3)
    update_from(0, n_items - 1)
    for t in range(2):
        o_ref[pl.ds(pl.multiple_of(first_query[t], tq), tq), :] = _finish_t([acc_ref[t, j] for j in range(2)])


def _attention_t_scratch(tq):
    return [pltpu.VMEM((2, 2, tq, tq), F32), pltpu.VMEM((2, 2, 1, tq), F32), pltpu.VMEM((2, 2, 1, tq), F32),
            pltpu.VMEM((2, 2, VT_ROWS, tq), F32)]


def _mla_kernel(q_ref, k_ref, vt_ref, o_ref, s_ref, smax_ref, m_ref, acc_ref, *, tq, nq):
    def q_of(t, q0, j):
        return q_ref[pl.ds(pl.multiple_of(q0, tq), tq), 128 * j:128 * (j + 1)]

    _causal_attention_pair_t(pl.program_id(2), nq, q_of, k_ref, vt_ref, o_ref, s_ref, smax_ref, m_ref, acc_ref,
                             tq=tq)


def _mla_attention(q, k, vt, B, T, tq=512):
    nq = T // tq
    return pl.pallas_call(
        functools.partial(_mla_kernel, tq=tq, nq=nq),
        grid=(B, MLA_HEADS // 2, nq // 2),
        in_specs=[pl.BlockSpec((T, 256), lambda b, p, i: (b, p)),
                  pl.BlockSpec((T, 256), lambda b, p, i: (b, p)),
                  pl.BlockSpec((None, 2 * VT_ROWS, T), lambda b, p, i: (b, p, 0))],
        out_specs=pl.BlockSpec((T, 128), lambda b, p, i: (b, p)),
        out_shape=jax.ShapeDtypeStruct((B * T, MLA_HEADS * MLA_V), BF16),
        scratch_shapes=_attention_t_scratch(tq),
        compiler_params=_cparams(("arbitrary", "arbitrary", "arbitrary")),
        name="mla_attention",
    )(q, k, vt)


def _swa_kernel(sink_ref, q_ref, k_ref, v_ref, vsw_ref, tb_ref, o_ref, *, tq):
    i = pl.program_id(1)
    G = SWA_HEADS // SWA_KV
    qi = lax.broadcasted_iota(jnp.int32, (128, 256), 0)
    kj = lax.broadcasted_iota(jnp.int32, (128, 256), 1)
    lane = lax.broadcasted_iota(jnp.int32, (128, 128), 1)
    for r in range(tq // 128):
        qstart = i * tq + 128 * r
        kstart = pl.multiple_of(jnp.maximum(qstart - 128, 0), 128)
        off = pl.multiple_of(qstart - kstart, 128)
        dist = off + qi - kj
        valid = (dist >= 0) & (dist < SWA_WIN)
        kw = k_ref[pl.ds(kstart, 256), :]
        vw = v_ref[pl.ds(kstart, 256), :]
        vsw = vsw_ref[pl.ds(kstart, 256), :]
        halves = [None] * SWA_HEADS
        for c in range(SWA_KV):
            qg = jnp.concatenate([q_ref[128 * r:128 * (r + 1), 128 * (G * c + g):128 * (G * c + g + 1)]
                                  for g in range(G)], axis=0)
            s_all = _dot_nt(qg, kw)
            for g in range(G):
                hq = G * c + g
                s = s_all[128 * g:128 * (g + 1)] + tb_ref[hq, pl.ds(off, 128), :]
                s = jnp.where(valid, s, NEG)
                sink = sink_ref[hq]
                m = jnp.maximum(jnp.max(s, axis=-1, keepdims=True), sink)
                p = jnp.exp(s - m)
                denom = jnp.sum(p, axis=-1, keepdims=True) + jnp.exp(sink - m)
                p = (p / denom).astype(BF16)
                halves[hq] = _dot(p, vw if c == hq % 2 else vsw)
        for mp in range(SWA_HEADS // 2):
            o_ref[128 * r:128 * (r + 1), 128 * mp:128 * (mp + 1)] = jnp.where(
                lane < HD, halves[2 * mp], halves[2 * mp + 1]).astype(BF16)


def _swa_attention(q8, k, v, vsw, sinks, tb, B, T, tq=512):
    nq = T // tq
    kv = pl.BlockSpec((T, 128), lambda b, i: (b, 0))
    return pl.pallas_call(
        functools.partial(_swa_kernel, tq=tq),
        grid=(B, nq),
        in_specs=[pl.BlockSpec(memory_space=pltpu.SMEM),
                  pl.BlockSpec((tq, 1024), lambda b, i: (b * nq + i, 0)),
                  kv, kv, kv,
                  pl.BlockSpec((8, 256, 256), lambda b, i: (0, 0, 0))],
        out_specs=pl.BlockSpec((tq, 512), lambda b, i: (b * nq + i, 0)),
        out_shape=jax.ShapeDtypeStruct((B * T, SWA_HEADS * HD), BF16),
        compiler_params=_cparams(("arbitrary", "arbitrary")),
        name="swa_attention",
    )(sinks, q8, k, v, vsw, tb)


def _outproj_router_kernel(h_ref, oa_ref, ob_ref, wo_ref, g_ref, wr_ref, br_ref,
                           hn_ref, u_ref, ri_ref, rg_ref, cnt_ref, base_ref):
    i = pl.program_id(0)

    @pl.when(i == 0)
    def _():
        base_ref[...] = jnp.zeros_like(base_ref)

    hn = h_ref[...] + _dot(oa_ref[...], wo_ref[0:512, :]) + _dot(ob_ref[...], wo_ref[512:1024, :])
    hn_ref[...] = hn
    u = _rms(hn, g_ref[...])
    u_ref[...] = _pack_halves(u)
    u_hi = u.astype(BF16)
    u_lo = (u - u_hi.astype(F32)).astype(BF16)
    both = _dot_nt(wr_ref[...], u_hi)
    logits = both[:128] + both[128:] + _dot_nt(wr_ref[0:128, :], u_lo) + br_ref[...]
    tm = logits.shape[1]
    sub = lax.broadcasted_iota(jnp.int32, (8, tm), 0).astype(F32)
    big = 1e6
    isg = sub < N_GROUPS
    gl = jnp.where(isg, logits[N_EXP:N_EXP + 8], NEG)
    gmax = jnp.max(gl, axis=0, keepdims=True)
    gsel = jnp.min(jnp.where(gl == gmax, sub, big), axis=0, keepdims=True)
    g_gate = 1.0 / jnp.sum(jnp.where(isg, jnp.exp(gl - gmax), 0.0), axis=0, keepdims=True)
    el = jnp.zeros((EPG, tm), F32)
    for grp in range(N_GROUPS):
        el = jnp.where(gsel == grp, logits[EPG * grp:EPG * (grp + 1)], el)
    m1 = jnp.max(el, axis=0, keepdims=True)
    i1 = jnp.min(jnp.where(el == m1, sub, big), axis=0, keepdims=True)
    el2 = jnp.where(sub == i1, NEG, el)
    m2 = jnp.max(el2, axis=0, keepdims=True)
    i2 = jnp.min(jnp.where(el2 == m2, sub, big), axis=0, keepdims=True)
    e1 = gsel * EPG + i1
    e2 = gsel * EPG + i2
    r = jnp.exp(m2 - m1)
    gate1 = g_gate / (1.0 + r)
    gate2 = g_gate * r / (1.0 + r)
    rows = lax.broadcasted_iota(jnp.int32, (128, tm), 0).astype(F32)
    oh1 = (rows == e1).astype(F32)
    oh2 = (rows == e2).astype(F32)
    oh = oh1 + oh2
    t_row = lax.broadcasted_iota(jnp.int32, (tm, tm), 0)
    t_col = lax.broadcasted_iota(jnp.int32, (tm, tm), 1)
    earlier = jnp.where(t_row < t_col, 1.0, 0.0).astype(BF16)
    prefix = _dot(oh.astype(BF16), earlier) + base_ref[:, 0:1]
    rank1 = jnp.sum(oh1 * prefix, axis=0, keepdims=True)
    rank2 = jnp.sum(oh2 * prefix, axis=0, keepdims=True)
    base_ref[...] = base_ref[...] + jnp.sum(oh, axis=1, keepdims=True)
    cnt_ref[...] = base_ref[...]
    ri_ref[...] = jnp.where(sub == 0, e1, jnp.where(sub == 1, e2, jnp.where(sub == 2, rank1,
                            jnp.where(sub == 3, rank2, 0.0))))
    rg_ref[...] = jnp.where(rows == 0, gate1, jnp.where(rows == 1, gate2, 0.0)).T


def _outproj_router(h, oa, ob, wo, g, wr, br):
    N = h.shape[0]
    full = lambda a: pl.BlockSpec(a.shape, lambda i: (0,) * a.ndim)
    row = lambda w: pl.BlockSpec((TM, w), lambda i: (i, 0))
    return pl.pallas_call(
        _outproj_router_kernel,
        grid=(N // TM,),
        in_specs=[row(D), row(512), row(512), full(wo), full(g), full(wr), full(br)],
        out_specs=[row(D), row(HALF), pl.BlockSpec((8, TM), lambda i: (0, i)), row(128),
                   pl.BlockSpec((128, 128), lambda i: (0, 0))],
        out_shape=[jax.ShapeDtypeStruct((N, D), F32), jax.ShapeDtypeStruct((N, HALF), jnp.int32),
                   jax.ShapeDtypeStruct((8, N), F32), jax.ShapeDtypeStruct((N, 128), F32),
                   jax.ShapeDtypeStruct((128, 128), F32)],
        scratch_shapes=[pltpu.VMEM((128, 128), F32)],
        compiler_params=_cparams(("arbitrary",)),
        name="outproj_router",
    )(h, oa, ob, wo, g, wr, br)


def _moe_kernel(te_ref, tv_ref, x_ref, wg_ref, wu_ref, wd_ref, o_ref, wgu_s, wd_s):
    i = pl.program_id(0)
    valid = tv_ref[i]

    @pl.when((i == 0) | (te_ref[i] != te_ref[jnp.maximum(i - 1, 0)]))
    def _():
        wgu_s[:, :D_EXP] = wg_ref[...].astype(BF16)
        wgu_s[:, D_EXP:] = wu_ref[...].astype(BF16)
        wd_s[...] = wd_ref[...].astype(BF16)

    @pl.when(valid > 0)
    def _():
        rows = lax.broadcasted_iota(jnp.int32, x_ref.shape, 0)
        lo, hi = _unpack_halves(jnp.where(rows < valid, x_ref[...], 0))
        hgu = _dot(lo.astype(BF16), wgu_s[:HALF, :]) + _dot(hi.astype(BF16), wgu_s[HALF:, :])
        hg, hu = hgu[:, :D_EXP], hgu[:, D_EXP:]
        a = hg * (1.0 / (1.0 + jnp.exp(-hg))) * hu
        o_ref[...] = _pack_halves(_dot(a.astype(BF16), wd_s[...]))

    @pl.when(valid <= 0)
    def _():
        o_ref[...] = jnp.zeros_like(o_ref)


def _moe_experts(xs, tile_expert, tile_valid, w_gate, w_up, w_down, layer):
    P = xs.shape[0]
    grid_spec = pltpu.PrefetchScalarGridSpec(
        num_scalar_prefetch=2,
        grid=(P // TM_E,),
        in_specs=[pl.BlockSpec((TM_E, HALF), lambda i, te, tv: (i, 0)),
                  pl.BlockSpec((None, None, D, D_EXP), lambda i, te, tv: (layer, te[i], 0, 0)),
                  pl.BlockSpec((None, None, D, D_EXP), lambda i, te, tv: (layer, te[i], 0, 0)),
                  pl.BlockSpec((None, None, D_EXP, D), lambda i, te, tv: (layer, te[i], 0, 0))],
        out_specs=pl.BlockSpec((TM_E, HALF), lambda i, te, tv: (i, 0)),
        scratch_shapes=[pltpu.VMEM((D, 2 * D_EXP), BF16), pltpu.VMEM((D_EXP, D), BF16)],
    )
    return pl.pallas_call(
        _moe_kernel,
        grid_spec=grid_spec,
        out_shape=jax.ShapeDtypeStruct((P, HALF), jnp.int32),
        compiler_params=_cparams(("arbitrary",)),
        name="moe_experts",
    )(tile_expert, tile_valid, xs, w_gate, w_up, w_down)


def _sc_mesh():
    return plsc.VectorSubcoreMesh(core_axis_name="c", subcore_axis_name="s",
                                  num_cores=SC_CORES, num_subcores=SC_SUBCORES)


def _sc_scatter_rows(src, d1, d2, P):
    N, W = src.shape
    per_w = N // (SC_CORES * SC_SUBCORES)

    @functools.partial(pl.kernel, mesh=_sc_mesh(), out_type=jax.ShapeDtypeStruct((P, W), src.dtype),
                       scratch_types=[pltpu.VMEM((SC_CHUNK,), jnp.int32), pltpu.VMEM((SC_CHUNK, W), src.dtype)],
                       name="sc_scatter_rows")
    def k(src_hbm, d1_hbm, d2_hbm, out_hbm, idx_v, rows_v):
        wid = lax.axis_index("s") * SC_CORES + lax.axis_index("c")

        @pl.loop(0, per_w // SC_CHUNK)
        def _(c):
            off = wid * per_w + c * SC_CHUNK
            pltpu.sync_copy(src_hbm.at[pl.ds(off, SC_CHUNK)], rows_v)
            pltpu.sync_copy(d1_hbm.at[pl.ds(off, SC_CHUNK)], idx_v)
            pltpu.sync_copy(rows_v, out_hbm.at[idx_v])
            pltpu.sync_copy(d2_hbm.at[pl.ds(off, SC_CHUNK)], idx_v)
            pltpu.sync_copy(rows_v, out_hbm.at[idx_v])

    return k(src, d1, d2)


def _sc_gather_rows(table, idx):
    B, W = idx.shape[0], table.shape[1]
    per_w = B // (SC_CORES * SC_SUBCORES)

    @functools.partial(pl.kernel, mesh=_sc_mesh(), out_type=jax.ShapeDtypeStruct((B, W), table.dtype),
                       scratch_types=[pltpu.VMEM((SC_CHUNK,), jnp.int32), pltpu.VMEM((SC_CHUNK, W), table.dtype)],
                       name="sc_gather_rows")
    def k(table_hbm, idx_hbm, out_hbm, idx_v, rows_v):
        wid = lax.axis_index("s") * SC_CORES + lax.axis_index("c")

        @pl.loop(0, per_w // SC_CHUNK)
        def _(c):
            off = wid * per_w + c * SC_CHUNK
            pltpu.sync_copy(idx_hbm.at[pl.ds(off, SC_CHUNK)], idx_v)
            pltpu.sync_copy(table_hbm.at[idx_v], rows_v)
            pltpu.sync_copy(rows_v, out_hbm.at[pl.ds(off, SC_CHUNK)])

    return k(table, idx)


def _dest_kernel(offs_ref, ri_ref, d_ref):
    ri = ri_ref[...]
    base = jnp.zeros(ri.shape, F32)
    for e in range(N_EXP):
        base = jnp.where(ri == e, offs_ref[e].astype(F32), base)
    d_ref[...] = (base + pltpu.roll(ri, 6, 0)).astype(jnp.int32)


def _dispatch_plan(ri, cnt, N):
    counts = cnt[:N_EXP, 0].astype(jnp.int32)
    padded = ((counts + TM_E - 1) // TM_E) * TM_E
    ends = jnp.cumsum(padded)
    offs = ends - padded
    take = lambda t, i: t.at[i].get(mode="promise_in_bounds")
    dest = pl.pallas_call(
        _dest_kernel,
        in_specs=[pl.BlockSpec(memory_space=pltpu.SMEM), pl.BlockSpec((8, N), lambda: (0, 0))],
        out_specs=pl.BlockSpec((8, N), lambda: (0, 0)),
        out_shape=jax.ShapeDtypeStruct((8, N), jnp.int32),
        name="dispatch_rows",
    )(offs, ri)
    d1, d2 = dest[0], dest[1]
    P = 2 * N + N_EXP * TM_E
    tile_start = jnp.arange(P // TM_E, dtype=jnp.int32) * TM_E
    tile_expert = jnp.minimum(jnp.sum((ends[None, :] <= tile_start[:, None]).astype(jnp.int32), axis=1), N_EXP - 1)
    tile_valid = jnp.clip(take(offs + counts, tile_expert) - tile_start, 0, TM_E)
    return d1, d2, P, tile_expert, tile_valid


def _combine(h_ref, y1_ref, y2_ref, rg_ref):
    rg = rg_ref[...]
    lo1, hi1 = _unpack_halves(y1_ref[...])
    lo2, hi2 = _unpack_halves(y2_ref[...])
    g1, g2 = rg[:, 0:1], rg[:, 1:2]
    return h_ref[...] + jnp.concatenate([g1 * lo1 + g2 * lo2, g1 * hi1 + g2 * hi2], axis=1)


ODD_WIDTHS = (1024, 1024, 512, 512)


def _combine_proj_odd_kernel(h_ref, y1_ref, y2_ref, rg_ref, g_ref, win_ref, wvt_ref, ind_ref, hn_ref, vt_ref,
                             vdt_ref, *out_refs):
    hn = _combine(h_ref, y1_ref, y2_ref, rg_ref)
    hn_ref[...] = hn
    u = _rms(hn, g_ref[...]).astype(BF16)
    vt_ref[...] = _values_t(_dot_nt(wvt_ref[0:512, :], u)).astype(BF16)
    vdt_ref[...] = _dot_nt(wvt_ref[512:1024, :], u).astype(BF16)
    qc_ref, kc_ref, qd_ref, kd_ref = out_refs
    qc = _spread_heads(_dot(u, win_ref[:, 0:512]), lambda h: 0)
    kc = _spread_heads(_dot(u, win_ref[:, 512:1024]), lambda h: 0)
    for hh in range(8):
        qc_ref[:, 128 * hh:128 * (hh + 1)] = qc[hh].astype(BF16)
        kc_ref[:, 128 * hh:128 * (hh + 1)] = (kc[hh] + ind_ref[...]).astype(BF16)
    qd_ref[...] = _dot(u, win_ref[:, 1024:1536]).astype(BF16)
    kd_ref[...] = _dot(u, win_ref[:, 1536:2048]).astype(BF16)


def _odd_weights(w_in):
    s = HD ** -0.5 * LOG2E
    scale = np.asarray([s] * 512 + [1.0] * 512 + [s] * 512 + [1.0] * 512, np.float32)
    win = jnp.concatenate([w_in[:, 0:1024], w_in[:, 1536:2560]], axis=1) * jnp.asarray(scale)
    wvt = jnp.concatenate([w_in[:, 1024:1536], w_in[:, 2560:3072]], axis=1).T
    return win.astype(BF16), wvt.astype(BF16)


def _block_indicator(T):
    assert T // MOBA_L <= 64
    ind = np.zeros((T, 128), np.float32)
    ind[np.arange(T), 64 + np.arange(T) // MOBA_L] = 1.0
    return jnp.asarray(ind)


def _y_specs(N):
    return [pl.BlockSpec((TM, HALF), lambda i: (i, 0)), pl.BlockSpec((TM, HALF), lambda i: (i + N // TM, 0))]


def _combine_proj_odd(h, y12, rg, g, win, wvt, ind, T):
    N = h.shape[0]
    nt = T // TM
    full = lambda a: pl.BlockSpec(a.shape, lambda i: (0,) * a.ndim)
    row = lambda w: pl.BlockSpec((TM, w), lambda i: (i, 0))
    return pl.pallas_call(
        _combine_proj_odd_kernel,
        grid=(N // TM,),
        in_specs=[row(D)] + _y_specs(N) + [row(128), full(g), full(win), full(wvt),
                  pl.BlockSpec((TM, 128), lambda i: (i % nt, 0))],
        out_specs=[row(D), pl.BlockSpec((None, 8 * VT_ROWS, TM), lambda i: (i // nt, 0, i % nt)),
                   pl.BlockSpec((None, 512, TM), lambda i: (i // nt, 0, i % nt))]
        + [row(w) for w in ODD_WIDTHS],
        out_shape=[jax.ShapeDtypeStruct((N, D), F32), jax.ShapeDtypeStruct((N // T, 8 * VT_ROWS, T), BF16),
                   jax.ShapeDtypeStruct((N // T, 512, T), BF16)]
        + [jax.ShapeDtypeStruct((N, w), BF16) for w in ODD_WIDTHS],
        compiler_params=_cparams(("arbitrary",)),
        name="combine_proj_odd",
    )(h, y12, y12, rg, g, win, wvt, ind)


def _combine_final_kernel(h_ref, y1_ref, y2_ref, rg_ref, g_ref, o_ref):
    o_ref[...] = _rms(_combine(h_ref, y1_ref, y2_ref, rg_ref), g_ref[...])


def _combine_final(h, y12, rg, g):
    N = h.shape[0]
    row = lambda w: pl.BlockSpec((TM, w), lambda i: (i, 0))
    return pl.pallas_call(
        _combine_final_kernel,
        grid=(N // TM,),
        in_specs=[row(D)] + _y_specs(N) + [row(128), pl.BlockSpec((1, D), lambda i: (0, 0))],
        out_specs=row(D),
        out_shape=jax.ShapeDtypeStruct((N, D), F32),
        compiler_params=_cparams(("arbitrary",)),
        name="combine_final",
    )(h, y12, y12, rg, g)


def _moba_kernel(q_ref, k_ref, vt_ref, tb_ref, o_ref, kmean_ref, qa_ref, s_ref, smax_ref, m_ref, acc_ref, *, nkb,
                 nq):
    ia = pl.program_id(2)
    L, tq = MOBA_L, MOBA_TQ

    @pl.when(ia == 0)
    def _():
        kmean_ref[...] = jnp.zeros_like(kmean_ref)
        for n in range(nkb):
            kmean_ref[64 + n:65 + n, :] = jnp.sum(k_ref[n * L:(n + 1) * L, :].astype(F32), axis=0,
                                                  keepdims=True) * (1.0 / L)

    nb = -(-nkb // 8) * 8
    lane = lax.broadcasted_iota(jnp.int32, (tq, 128), 1)
    blk = lax.broadcasted_iota(jnp.int32, (nb, tq), 0)
    half = jnp.where(lax.broadcasted_iota(jnp.int32, (nb, tq), 1) >= L, 1, 0)
    blk_f = blk.astype(F32)
    for t, tile in enumerate((ia, nq - 1 - ia)):
        own = 2 * tile + half
        for j in range(2):
            qj = q_ref[pl.ds(pl.multiple_of(tile * tq, tq), tq), 128 * j:128 * (j + 1)]
            gsc = lax.dot_general(kmean_ref[64:64 + nb, 128 * j:128 * (j + 1)], qj.astype(F32),
                                  (((1,), (1,)), ((), ())), preferred_element_type=F32,
                                  precision=lax.Precision.HIGHEST)
            g = jnp.where(blk < own, gsc, NEG)
            allowed = jnp.where(blk == own, 1.0, 0.0)
            for _ in range(MOBA_TOPK):
                mx = jnp.max(g, axis=0, keepdims=True)
                idx = jnp.min(jnp.where(g == mx, blk_f, 1e6), axis=0, keepdims=True)
                pick = blk_f == idx
                allowed = jnp.maximum(allowed, jnp.where(pick, jnp.where(mx > 0.5 * NEG, 1.0, 0.0), 0.0))
                g = jnp.where(pick, 2.0 * NEG, g)
            mask = jnp.concatenate([jnp.zeros((64, tq), F32), jnp.where(allowed > 0.5, 0.0, NEG),
                                    jnp.zeros((64 - nb, tq), F32)], axis=0)
            qa_ref[t, j] = jnp.where(lane < HD, qj, mask.T.astype(BF16))

    def bias_of(start, q0, j):
        parts = []
        for c in range(tq // L):
            off = jnp.minimum(q0 - start - c * L, BIAS_CONST_FROM) + BIAS_PAD
            parts.append(tb_ref[j, :, pl.ds(pl.multiple_of(off, L), tq)])
        return jnp.concatenate(parts, axis=0)

    _causal_attention_pair_t(ia, nq, lambda t, q0, j: qa_ref[t, j], k_ref, vt_ref, o_ref, s_ref, smax_ref, m_ref,
                             acc_ref, tq=tq, bias_of=bias_of)


def _moba_attention(q, k, vt, tb, B, T):
    tq = MOBA_TQ
    nq = T // tq
    return pl.pallas_call(
        functools.partial(_moba_kernel, nkb=T // MOBA_L, nq=nq),
        grid=(B, 4, nq // 2),
        in_specs=[pl.BlockSpec((T, 256), lambda b, p, i: (b, p)),
                  pl.BlockSpec((T, 256), lambda b, p, i: (b, p)),
                  pl.BlockSpec((None, 2 * VT_ROWS, T), lambda b, p, i: (b, p, 0)),
                  pl.BlockSpec((2, MOBA_L, BIAS_ROWS), lambda b, p, i: (p, 0, 0))],
        out_specs=pl.BlockSpec((T, 128), lambda b, p, i: (b, p)),
        out_shape=jax.ShapeDtypeStruct((B * T, 512), BF16),
        scratch_shapes=[pltpu.VMEM((128, 256), F32), pltpu.VMEM((2, 2, tq, 128), BF16)]
        + _attention_t_scratch(tq),
        compiler_params=_cparams(("arbitrary", "arbitrary", "arbitrary")),
        name="moba_attention",
    )(q, k, vt, tb)


def _sb_kernel(q_ref, k_ref, vt_ref, o_ref, *, tq, pairs):
    i = pl.program_id(2)
    lane = lax.broadcasted_iota(jnp.int32, (tq, 128), 1)
    qs = []
    for p in range(pairs):
        q_pair = q_ref[:, 128 * p:128 * (p + 1)]
        qs += [jnp.where(lane < HD, q_pair, 0).astype(BF16), jnp.where(lane >= HD, q_pair, 0).astype(BF16)]
    key = lax.broadcasted_iota(jnp.int32, (tq, tq), 0)
    qry = lax.broadcasted_iota(jnp.int32, (tq, tq), 1)
    past = key < qry
    suffix = jnp.where(qry >= key, -1.0, 0.0).astype(BF16)
    heads = range(2 * pairs)

    def step(kb, carry, boundary):
        start = pl.multiple_of(kb * tq, tq)
        zs = [_dot_nt(k_ref[pl.ds(start, tq), 128 * (h // 2):128 * (h // 2 + 1)], qs[h]) for h in heads]
        parts = []
        for z in zs:
            neg_abs = pltpu.bitcast(pltpu.bitcast(z, jnp.uint32) | jnp.uint32(0x80000000), F32)
            soft = jnp.maximum(z, 0.0) + jnp.log2(1.0 + jnp.exp2(neg_abs))
            if boundary:
                soft = jnp.where(past, soft, 0.0)
            hi = soft.astype(BF16)
            parts.append((hi, (soft - hi.astype(F32)).astype(BF16)))
        sums = [_dot(suffix, hi) + _dot(suffix, lo) for hi, lo in parts]
        ws = []
        for h in heads:
            w = jnp.exp2(zs[h] + (carry[h][0] + sums[h]))
            ws.append((jnp.where(past, w, 0.0) if boundary else w).astype(BF16))
        return tuple((carry[h][0] + sums[h][0:1], carry[h][1]
                      + _dot(vt_ref[128 * (h // 2):128 * (h // 2 + 1), pl.ds(start, tq)], ws[h])) for h in heads)

    def live(carry):
        top = carry[0][0]
        for c, _ in carry[1:]:
            top = jnp.maximum(top, c)
        return (jnp.max(top) > SB_DONE).astype(jnp.int32)

    def body(state):
        t, _, carry = state
        carry = step(i - 1 - t, carry, False)
        return t + 1, live(carry), carry

    init = tuple((jnp.zeros((1, tq), F32), jnp.zeros((128, tq), F32)) for _ in range(2 * pairs))
    carry = step(i, init, True)
    _, _, carry = lax.while_loop(lambda st: (st[0] < i) & (st[1] > 0), body, (jnp.int32(0), live(carry), carry))
    rows = lax.broadcasted_iota(jnp.int32, (128, tq), 0)
    for p in range(pairs):
        ot = jnp.where(rows < HD, carry[2 * p][1], carry[2 * p + 1][1])
        o_ref[:, 128 * p:128 * (p + 1)] = ot.T.astype(BF16)


def _sb_attention(q, k, vt, B, T, tq=256, pairs=4):
    nq = T // tq
    w = 128 * pairs
    return pl.pallas_call(
        functools.partial(_sb_kernel, tq=tq, pairs=pairs),
        grid=(B, 4 // pairs, nq),
        in_specs=[pl.BlockSpec((tq, w), lambda b, p, i: (b * nq + i, p)),
                  pl.BlockSpec((T, w), lambda b, p, i: (b, p)),
                  pl.BlockSpec((None, w, T), lambda b, p, i: (b, p, 0))],
        out_specs=pl.BlockSpec((tq, w), lambda b, p, i: (b * nq + i, p)),
        out_shape=jax.ShapeDtypeStruct((B * T, 512), BF16),
        compiler_params=_cparams(("arbitrary", "arbitrary", "arbitrary")),
        name="sb_attention",
    )(q, k, vt)


def _router_weights(w_group, b_group, w_router, b_router):
    pad = 128 - N_EXP - N_GROUPS
    wr = jnp.concatenate([w_router, w_group, jnp.zeros((D, pad), F32)], axis=1).T
    hi = wr.astype(BF16)
    lo = (wr - hi.astype(F32)).astype(BF16)
    br = jnp.concatenate([b_router, b_group, jnp.zeros((pad,), F32)])[:, None]
    return jnp.concatenate([hi, lo], axis=0), br


def _ffn(h, oa, ob, wo, g, w_group, b_group, w_router, b_router, w_gate, w_up, w_down, layer):
    wr, br = _router_weights(w_group, b_group, w_router, b_router)
    hn, u, ri, rg, cnt = _outproj_router(h, oa, ob, wo.astype(BF16), g[None, :], wr, br)
    d1, d2, P, tile_expert, tile_valid = _dispatch_plan(ri, cnt, h.shape[0])
    xs = _sc_scatter_rows(u, d1, d2, P)
    ys = _moe_experts(xs, tile_expert, tile_valid, w_gate, w_up, w_down, layer)
    return hn, _sc_gather_rows(ys, jnp.concatenate([d1, d2])), rg


def kernel(x, norm_mix, norm_ffn, norm_final, rel_bias_table, w_in_even, g_mla_q, g_mla_kv, w_mla_qb, w_mla_kvb, swa_sinks, w_out_even, w_in_odd, w_out_odd, moe_w_group, moe_b_group, moe_w_router, moe_b_router, moe_w_gate, moe_w_up, moe_w_down):
    B, T, _ = x.shape
    h = x.reshape(B * T, D)
    tb_swa, tb_moba = _bias_tables(rel_bias_table)

    win, wqa, wqb, wk, wv = _even_weights(w_in_even[0], w_mla_qb[0], w_mla_kvb[0])
    q, k, v, qs8, ks, vs, vsw = _proj_even(h, norm_mix[0][None, :], win, g_mla_q[0][None, :], g_mla_kv[0][None, :],
                                           wqa, wqb, wk, wv, _rope_tables(T), T)
    oa = _mla_attention(q, k, v, B, T)
    ob = _swa_attention(qs8, ks, vs, vsw, swa_sinks[0], tb_swa, B, T)
    h, y12, rg = _ffn(h, oa, ob, w_out_even[0], norm_ffn[0], moe_w_group[0], moe_b_group[0], moe_w_router[0],
                      moe_b_router[0], moe_w_gate, moe_w_up, moe_w_down, 0)

    h, vct, vdt, qc, kc, qd, kd = _combine_proj_odd(h, y12, rg, norm_mix[1][None, :], *_odd_weights(w_in_odd[0]),
                                                   _block_indicator(T), T)
    oc = _moba_attention(qc, kc, vct, tb_moba, B, T)
    od = _sb_attention(qd, kd, vdt, B, T)
    h, y12, rg = _ffn(h, oc, od, w_out_odd[0], norm_ffn[1], moe_w_group[1], moe_b_group[1], moe_w_router[1],
                      moe_b_router[1], moe_w_gate, moe_w_up, moe_w_down, 1)
    out = _combine_final(h, y12, rg, norm_final[None, :])
    return out.reshape(B, T, D)
```

```python
import functools
import math

import numpy as np
import jax
import jax.numpy as jnp
from jax import lax
from jax.experimental import pallas as pl
from jax.experimental.pallas import tpu as pltpu
from jax.experimental.pallas import tpu_sc as plsc

F32 = jnp.float32
BF16 = jnp.bfloat16
NEG = -1e30
EPS = 1e-6

D = 1024
HD = 64
LANES = 128
MLA_HEADS, MLA_QR, MLA_KVR, MLA_NOPE, MLA_ROPE, MLA_V = 8, 256, 128, 64, 32, 64
ROPE_THETA = 10000.0
SWA_HEADS, SWA_KV, SWA_WIN = 8, 2, 128
MOBA_L, MOBA_TOPK = 256, 3
REL_BUCKETS, REL_MAX = 32, 2048
N_GROUPS, EPG, N_EXP, D_EXP = 4, 8, 32, 256
BIAS_CONST_FROM = 1792
MOBA_TQ = 2 * MOBA_L
BIAS_PAD = MOBA_L
BIAS_ROWS = BIAS_PAD + BIAS_CONST_FROM + MOBA_TQ
SB_DONE = -160.0
TM = 512
TM_E = 512
VMEM_LIMIT = 56 * 1024 * 1024
SC_CORES, SC_SUBCORES = 2, 16
SC_CHUNK = 128
HALF = D // 2
VT_ROWS = HD + 16
LOG2E = math.log2(math.e)


def _pack_halves(x):
    lo = pltpu.bitcast(x[:, :HALF].astype(BF16).astype(F32), jnp.uint32) >> 16
    hi = pltpu.bitcast(x[:, HALF:].astype(BF16).astype(F32), jnp.uint32) & jnp.uint32(0xFFFF0000)
    return pltpu.bitcast(lo | hi, jnp.int32)


def _unpack_halves(w):
    w = pltpu.bitcast(w, jnp.uint32)
    return pltpu.bitcast(w << 16, F32), pltpu.bitcast(w & jnp.uint32(0xFFFF0000), F32)


def _cparams(sem):
    return pltpu.CompilerParams(dimension_semantics=sem, vmem_limit_bytes=VMEM_LIMIT)


def _dot(a, b):
    return jnp.dot(a, b, preferred_element_type=F32)


def _dot_nt(a, b):
    return lax.dot_general(a, b, (((1,), (1,)), ((), ())), preferred_element_type=F32)


def _rms(x, g):
    return x * lax.rsqrt(jnp.mean(x * x, axis=-1, keepdims=True) + EPS) * g


def _values_t(vt):
    ones = jnp.ones((VT_ROWS - HD, vt.shape[1]), vt.dtype)
    return jnp.concatenate([blk for h in range(8) for blk in (vt[HD * h:HD * (h + 1)], ones)], axis=0)


def _spread_heads(x, lane_of):
    lane = lax.broadcasted_iota(jnp.int32, (x.shape[0], 128), 1)
    out = []
    for h in range(8):
        pair = x[:, 128 * (h // 2):128 * (h // 2 + 1)]
        if HD * (h % 2) != lane_of(h):
            pair = pltpu.roll(pair, HD, 1)
        out.append(jnp.where((lane >= lane_of(h)) & (lane < lane_of(h) + HD), pair, 0.0))
    return out


def _flash_update_t(st, vt, m, acc):
    m_new = jnp.maximum(m, jnp.max(st, axis=0, keepdims=True))
    alpha = jnp.exp2(m - m_new)
    pt = jnp.exp2(st - m_new).astype(BF16)
    return m_new, alpha * acc + _dot(vt, pt)


def _flash_update(s, v, m, l, acc):
    m_new = jnp.maximum(m, jnp.max(s, axis=-1, keepdims=True))
    alpha = jnp.exp(m - m_new)
    p = jnp.exp(s - m_new)
    l = alpha * l + jnp.sum(p, axis=-1, keepdims=True)
    acc = alpha * acc + _dot(p.astype(BF16), v)
    return m_new, l, acc


def _t5_bucket_np(dist):
    n = np.maximum(dist, 0)
    max_exact = REL_BUCKETS // 2
    nf = np.maximum(n, 1).astype(np.float32)
    large = max_exact + (np.log(nf / np.float32(max_exact)) / np.float32(math.log(REL_MAX / max_exact))
                         * np.float32(REL_BUCKETS - max_exact)).astype(np.int32)
    large = np.minimum(large, REL_BUCKETS - 1)
    return np.where(n < max_exact, n, large).astype(np.int32)


def _bias_kernel(table_ref, bucket_ref, out_ref, *, scale, ranges):
    b = bucket_ref[...]
    for m, (lo, hi) in enumerate(ranges):
        @pl.when(pl.program_id(0) == m)
        def _():
            for h in range(8):
                acc = jnp.full(b.shape, table_ref[lo, h] * scale, F32)
                for k in range(lo + 1, hi + 1):
                    acc = jnp.where(b == k, table_ref[k, h] * scale, acc)
                out_ref[h] = acc


def _bias_lookup(rel_table, dist, scale, name):
    R, C = dist.shape
    buckets = _t5_bucket_np(dist)
    ranges = tuple((int(buckets[:, c:c + 256].min()), int(buckets[:, c:c + 256].max())) for c in range(0, C, 256))
    return pl.pallas_call(
        functools.partial(_bias_kernel, scale=scale, ranges=ranges),
        grid=(C // 256,),
        in_specs=[pl.BlockSpec(memory_space=pltpu.SMEM),
                  pl.BlockSpec((R, 256), lambda m: (0, m))],
        out_specs=pl.BlockSpec((8, R, 256), lambda m: (0, 0, m)),
        out_shape=jax.ShapeDtypeStruct((8, R, C), F32),
        compiler_params=_cparams(("arbitrary",)),
        name=name,
    )(rel_table, jnp.asarray(buckets))


def _bias_tables(rel_table):
    swa = _bias_lookup(rel_table, np.arange(256)[None, :] - np.arange(256)[:, None], LOG2E, "bias_swa")
    moba = _bias_lookup(rel_table, np.arange(BIAS_ROWS)[None, :] - BIAS_PAD - np.arange(MOBA_L)[:, None], LOG2E,
                        "bias_moba")
    return swa, moba


def _proj_even_kernel(h_ref, g_ref, win_ref, gq_ref, gkv_ref, wqa_ref, wqb_ref, wk_ref, wv_ref,
                      cq_ref, sq_ref, ck_ref, sk_ref,
                      q_ref, k_ref, v_ref, qs_ref, ks_ref, vst_ref):
    u = _rms(h_ref[...], g_ref[...]).astype(BF16)
    nq = _rms(_dot(u, win_ref[:, 0:256]), gq_ref[...]).astype(BF16)
    qa = _dot(nq, wqa_ref[...])
    qb = _dot(nq, wqb_ref[...])
    cq, sq = cq_ref[...], sq_ref[...]
    for hh in range(MLA_HEADS):
        sl = slice(128 * hh, 128 * (hh + 1))
        q_ref[:, sl] = (qa[:, sl] * cq + qb[:, sl] * sq).astype(BF16)
    nkv = _rms(_dot(u, win_ref[:, 256:384]), gkv_ref[...]).astype(BF16)
    kn = _dot(nkv, wk_ref[...])
    v_ref[...] = _values_t(_dot_nt(wv_ref[...], nkv)).astype(BF16)
    rest = _dot(u, win_ref[:, 896:1408])
    kr = rest[:, 128:256] * ck_ref[...] + rest[:, 256:384] * sk_ref[...]
    for hh in range(MLA_HEADS):
        sl = slice(128 * hh, 128 * (hh + 1))
        k_ref[:, sl] = (kn[:, sl] + kr).astype(BF16)
    qs = _spread_heads(_dot(u, win_ref[:, 384:896]), lambda hq: HD * (hq // (SWA_HEADS // SWA_KV)))
    for hq in range(SWA_HEADS):
        qs_ref[:, 128 * hq:128 * (hq + 1)] = qs[hq].astype(BF16)
    ks_ref[...] = rest[:, 0:128].astype(BF16)
    vst_ref[...] = rest[:, 384:512].T.astype(BF16)


def _cols(w, idx, scale=None):
    idx = np.asarray(idx)
    cuts = [0] + [n for n in range(1, len(idx)) if (idx[n] < 0) != (idx[n - 1] < 0)
                  or (idx[n] >= 0 and idx[n] != idx[n - 1] + 1)] + [len(idx)]
    parts = [jnp.zeros((w.shape[0], b - a), w.dtype) if idx[a] < 0 else w[:, idx[a]:idx[a] + b - a]
             for a, b in zip(cuts[:-1], cuts[1:])]
    out = jnp.concatenate(parts, axis=1)
    if scale is not None:
        out = out * scale
    return out


def _even_weights(w_in, w_qb, w_kvb):
    o_cq, o_ckv, o_kr = 0, MLA_QR, MLA_QR + MLA_KVR
    o_qs = o_kr + MLA_ROPE
    o_ks = o_qs + SWA_HEADS * HD
    o_vs = o_ks + SWA_KV * HD
    idx = list(range(o_cq, o_cq + MLA_QR)) + list(range(o_ckv, o_ckv + MLA_KVR))
    scale = [1.0] * len(idx)
    idx += list(range(o_qs, o_qs + SWA_HEADS * HD))
    scale += [HD ** -0.5 * LOG2E] * (SWA_HEADS * HD)
    idx += list(range(o_ks, o_ks + 128))
    scale += [1.0] * 128
    half = MLA_ROPE // 2
    idx += [-1] * 64 + list(range(o_kr, o_kr + MLA_ROPE)) + [-1] * 32
    idx += [-1] * 64 + list(range(o_kr + half, o_kr + MLA_ROPE)) + list(range(o_kr, o_kr + half)) + [-1] * 32
    idx += list(range(o_vs, o_vs + 128))
    scale += [1.0] * 384
    win = _cols(w_in, idx, jnp.asarray(np.asarray(scale, np.float32))).astype(BF16)
    qw = MLA_NOPE + MLA_ROPE
    ia, ib, ik, iv = [], [], [], []
    for h in range(MLA_HEADS):
        ia += list(range(qw * h, qw * h + qw)) + [-1] * 32
        ib += [-1] * 64 + list(range(qw * h + MLA_NOPE + half, qw * h + qw)) \
            + list(range(qw * h + MLA_NOPE, qw * h + MLA_NOPE + half)) + [-1] * 32
        ik += list(range(128 * h, 128 * h + MLA_NOPE)) + [-1] * 64
        iv += list(range(128 * h + MLA_NOPE, 128 * h + 128))
    return (win, _cols(w_qb, ia).astype(BF16), _cols(w_qb, ib).astype(BF16),
            _cols(w_kvb, ik).astype(BF16), _cols(w_kvb, iv).T.astype(BF16))


def _rope_tables(T):
    f32 = np.float32
    freqs = f32(ROPE_THETA) ** (-np.arange(0, MLA_ROPE, 2, dtype=f32) / f32(MLA_ROPE))
    ang = np.arange(T, dtype=f32)[:, None] * freqs[None, :]
    cos, sin = np.cos(ang), np.sin(ang)
    z = lambda n: np.zeros((T, n), f32)
    scale = f32((MLA_NOPE + MLA_ROPE) ** -0.5 * LOG2E)
    ck = np.concatenate([z(64), cos, cos, z(32)], axis=1)
    sk = np.concatenate([z(64), -sin, sin, z(32)], axis=1)
    cq = np.concatenate([np.ones((T, 64), f32), cos, cos, z(32)], axis=1) * scale
    sq = sk * scale
    return tuple(jnp.asarray(t, F32) for t in (cq, sq, ck, sk))


def _proj_even(h, g, win, gq, gkv, wqa, wqb, wk, wv, tables, T):
    N = h.shape[0]
    nt = T // TM
    full = lambda a: pl.BlockSpec(a.shape, lambda i: (0,) * a.ndim)
    tab = pl.BlockSpec((TM, 128), lambda i: (i % nt, 0))
    row = lambda w: pl.BlockSpec((TM, w), lambda i: (i, 0))
    outs = (1024, 1024, -8 * VT_ROWS, 1024, 128, -128)
    spec = lambda w: row(w) if w > 0 else pl.BlockSpec((None, -w, TM), lambda i: (i // nt, 0, i % nt))
    shape = lambda w: jax.ShapeDtypeStruct((N, w) if w > 0 else (N // T, -w, T), BF16)
    return pl.pallas_call(
        _proj_even_kernel,
        grid=(N // TM,),
        in_specs=[row(D), full(g), full(win), full(gq), full(gkv), full(wqa), full(wqb), full(wk), full(wv),
                  tab, tab, tab, tab],
        out_specs=[spec(w) for w in outs],
        out_shape=[shape(w) for w in outs],
        compiler_params=_cparams(("arbitrary",)),
        name="proj_even",
    )(h, g, win, gq, gkv, wqa, wqb, wk, wv, *tables)


def _finish_t(accs):
    ot = jnp.concatenate([acc[:HD] / acc[HD:HD + 1] for acc in accs], axis=0)
    return ot.T.astype(BF16)


def _causal_attention_pair_t(ia, nq, q_of, k_ref, vt_ref, o_ref, s_ref, smax_ref, m_ref, acc_ref, *, tq,
                             bias_of=None):
    nh = s_ref.shape[1]
    first_query = (ia * tq, (nq - 1 - ia) * tq)
    n_items = nq + 1
    assert n_items % 2 == 1
    key = lax.broadcasted_iota(jnp.int32, (tq, tq), 0)
    qry = lax.broadcasted_iota(jnp.int32, (tq, tq), 1)

    def item(p):
        if isinstance(p, int) and p < 2:
            return p, first_query[p], first_query[p]
        t = jnp.where(p - 2 >= ia, 1, 0)
        return t, jnp.where(t == 1, first_query[1], first_query[0]), (p - 2 - t * ia) * tq

    def scores_to(slot, p):
        t, q0, start = item(p)
        start = pl.multiple_of(start, tq)
        for j in range(nh):
            st = _dot_nt(k_ref[pl.ds(start, tq), 128 * j:128 * (j + 1)], q_of(t, q0, j))
            if bias_of is not None:
                st = st + bias_of(start, q0, j)
            if isinstance(p, int) and p < 2:
                st = jnp.where(key <= qry, st, NEG)
            s_ref[slot, j] = st
            smax_ref[slot, j] = jnp.max(st, axis=0, keepdims=True)

    def update_from(slot, p):
        t, _, start = item(p)
        start = pl.multiple_of(start, tq)
        m_new = [jnp.maximum(m_ref[t, j], smax_ref[slot, j]) for j in range(nh)]
        pts = [jnp.exp2(s_ref[slot, j] - m_new[j]).astype(BF16) for j in range(nh)]
        for j in range(nh):
            acc_ref[t, j] = (jnp.exp2(m_ref[t, j] - m_new[j]) * acc_ref[t, j]
                             + _dot(vt_ref[VT_ROWS * j:VT_ROWS * (j + 1), pl.ds(start, tq)], pts[j]))
            m_ref[t, j] = m_new[j]

    def pair(k):
        scores_to(1, k + 1)
        update_from(0, k)
        scores_to(0, k + 2)
        update_from(1, k + 1)

    def quad(c, _):
        pair(2 + 4 * c)
        pair(4 + 4 * c)
        return 0

    m_ref[...] = jnp.full(m_ref.shape, NEG, F32)
    acc_ref[...] = jnp.zeros(acc_ref.shape, F32)
    scores_to(0, 0)
    pair(0)
    later_pairs = (n_items - 1) // 2 - 1
    lax.fori_loop(0, later_pairs // 2, quad, 0)
    if later_pairs % 2:
        pair(n_items - 3)
    update_from(0, n_items - 1)
    for t in range(2):
        o_ref[pl.ds(pl.multiple_of(first_query[t], tq), tq), :] = _finish_t([acc_ref[t, j] for j in range(nh)])


def _attention_t_scratch(tq, nh=2):
    return [pltpu.VMEM((2, nh, tq, tq), F32), pltpu.VMEM((2, nh, 1, tq), F32), pltpu.VMEM((2, nh, 1, tq), F32),
            pltpu.VMEM((2, nh, VT_ROWS, tq), F32)]


def _mla_kernel(q_ref, k_ref, vt_ref, o_ref, s_ref, smax_ref, m_ref, acc_ref, *, tq, nq):
    def q_of(t, q0, j):
        return q_ref[pl.ds(pl.multiple_of(q0, tq), tq), 128 * j:128 * (j + 1)]

    _causal_attention_pair_t(pl.program_id(2), nq, q_of, k_ref, vt_ref, o_ref, s_ref, smax_ref, m_ref, acc_ref,
                             tq=tq)


def _mla_attention(q, k, vt, B, T, tq=512):
    nq = T // tq
    return pl.pallas_call(
        functools.partial(_mla_kernel, tq=tq, nq=nq),
        grid=(B, MLA_HEADS // 2, nq // 2),
        in_specs=[pl.BlockSpec((T, 256), lambda b, p, i: (b, p)),
                  pl.BlockSpec((T, 256), lambda b, p, i: (b, p)),
                  pl.BlockSpec((None, 2 * VT_ROWS, T), lambda b, p, i: (b, p, 0))],
        out_specs=pl.BlockSpec((T, 128), lambda b, p, i: (b, p)),
        out_shape=jax.ShapeDtypeStruct((B * T, MLA_HEADS * MLA_V), BF16),
        scratch_shapes=_attention_t_scratch(tq),
        compiler_params=_cparams(("arbitrary", "arbitrary", "arbitrary")),
        name="mla_attention",
    )(q, k, vt)


def _swa_kernel(sink_ref, q_ref, k_ref, vt_ref, tb_ref, o_ref, *, tq):
    i = pl.program_id(1)
    G = SWA_HEADS // SWA_KV
    W = 2 * SWA_WIN
    kj = lax.broadcasted_iota(jnp.int32, (W, G * 128), 0)
    lane = lax.broadcasted_iota(jnp.int32, (W, G * 128), 1)
    qi = lane & 127
    head = lax.broadcasted_iota(jnp.int32, (1, G * 128), 1) >> 7
    sinks = []
    for c in range(SWA_KV):
        sink = jnp.zeros((1, G * 128), F32)
        for g in range(G):
            sink = jnp.where(head == g, sink_ref[G * c + g], sink)
        sinks.append(sink)
    scores, values = [], []
    for r in range(tq // 128):
        qstart = i * tq + 128 * r
        kstart = pl.multiple_of(jnp.maximum(qstart - SWA_WIN, 0), 128)
        off = pl.multiple_of(qstart - kstart, 128)
        dist = off + qi - kj
        valid = (dist >= 0) & (dist < SWA_WIN)
        kw = k_ref[pl.ds(kstart, W), :]
        for c in range(SWA_KV):
            qg = jnp.concatenate([q_ref[128 * r:128 * (r + 1), 128 * (G * c + g):128 * (G * c + g + 1)]
                                  for g in range(G)], axis=0)
            bias = jnp.concatenate([tb_ref[G * c + g, :, pl.ds(off, 128)] for g in range(G)], axis=1)
            scores.append(jnp.where(valid, _dot_nt(kw, qg) + bias, NEG))
            values.append(vt_ref[HD * c:HD * (c + 1), pl.ds(kstart, W)])
    probs = []
    for n, s in enumerate(scores):
        sink = sinks[n % SWA_KV]
        m = jnp.maximum(jnp.max(s, axis=0, keepdims=True), sink)
        p = jnp.exp2(s - m)
        probs.append((p.astype(BF16), jnp.sum(p, axis=0, keepdims=True) + jnp.exp2(sink - m)))
    for r in range(tq // 128):
        outs = []
        for c in range(SWA_KV):
            p, denom = probs[SWA_KV * r + c]
            ot = _dot(values[SWA_KV * r + c], p) / denom
            outs += [ot[:, 128 * g:128 * (g + 1)] for g in range(G)]
        o_ref[128 * r:128 * (r + 1), :] = jnp.concatenate(outs, axis=0).T.astype(BF16)


def _swa_attention(q8, k, vt, sinks, tb, B, T, tq=512):
    nq = T // tq
    return pl.pallas_call(
        functools.partial(_swa_kernel, tq=tq),
        grid=(B, nq),
        in_specs=[pl.BlockSpec(memory_space=pltpu.SMEM),
                  pl.BlockSpec((tq, 1024), lambda b, i: (b * nq + i, 0)),
                  pl.BlockSpec((T, 128), lambda b, i: (b, 0)),
                  pl.BlockSpec((None, 128, T), lambda b, i: (b, 0, 0)),
                  pl.BlockSpec((8, 256, 256), lambda b, i: (0, 0, 0))],
        out_specs=pl.BlockSpec((tq, 512), lambda b, i: (b * nq + i, 0)),
        out_shape=jax.ShapeDtypeStruct((B * T, SWA_HEADS * HD), BF16),
        compiler_params=_cparams(("arbitrary", "arbitrary")),
        name="swa_attention",
    )(sinks * LOG2E, q8, k, vt, tb)


def _outproj_router_kernel(h_ref, oa_ref, ob_ref, wo_ref, g_ref, wr_ref, br_ref,
                           hn_ref, u_ref, ri_ref, rg_ref, cnt_ref, base_ref):
    i = pl.program_id(0)

    @pl.when(i == 0)
    def _():
        base_ref[...] = jnp.zeros_like(base_ref)

    hn = h_ref[...] + _dot(oa_ref[...], wo_ref[0:512, :]) + _dot(ob_ref[...], wo_ref[512:1024, :])
    hn_ref[...] = hn
    u = _rms(hn, g_ref[...])
    u_ref[...] = _pack_halves(u)
    u_hi = u.astype(BF16)
    u_lo = (u - u_hi.astype(F32)).astype(BF16)
    both = _dot_nt(wr_ref[...], u_hi)
    logits = both[:128] + both[128:] + _dot_nt(wr_ref[0:128, :], u_lo) + br_ref[...]
    tm = logits.shape[1]
    sub = lax.broadcasted_iota(jnp.int32, (8, tm), 0).astype(F32)
    big = 1e6
    isg = sub < N_GROUPS
    gl = jnp.where(isg, logits[N_EXP:N_EXP + 8], NEG)
    gmax = jnp.max(gl, axis=0, keepdims=True)
    gsel = jnp.min(jnp.where(gl == gmax, sub, big), axis=0, keepdims=True)
    g_gate = 1.0 / jnp.sum(jnp.where(isg, jnp.exp(gl - gmax), 0.0), axis=0, keepdims=True)
    el = jnp.zeros((EPG, tm), F32)
    for grp in range(N_GROUPS):
        el = jnp.where(gsel == grp, logits[EPG * grp:EPG * (grp + 1)], el)
    m1 = jnp.max(el, axis=0, keepdims=True)
    i1 = jnp.min(jnp.where(el == m1, sub, big), axis=0, keepdims=True)
    el2 = jnp.where(sub == i1, NEG, el)
    m2 = jnp.max(el2, axis=0, keepdims=True)
    i2 = jnp.min(jnp.where(el2 == m2, sub, big), axis=0, keepdims=True)
    e1 = gsel * EPG + i1
    e2 = gsel * EPG + i2
    r = jnp.exp(m2 - m1)
    gate1 = g_gate / (1.0 + r)
    gate2 = g_gate * r / (1.0 + r)
    rows = lax.broadcasted_iota(jnp.int32, (128, tm), 0).astype(F32)
    oh1 = (rows == e1).astype(F32)
    oh2 = (rows == e2).astype(F32)
    oh = oh1 + oh2
    t_row = lax.broadcasted_iota(jnp.int32, (tm, tm), 0)
    t_col = lax.broadcasted_iota(jnp.int32, (tm, tm), 1)
    earlier = jnp.where(t_row < t_col, 1.0, 0.0).astype(BF16)
    prefix = _dot(oh.astype(BF16), earlier) + base_ref[:, 0:1]
    rank1 = jnp.sum(oh1 * prefix, axis=0, keepdims=True)
    rank2 = jnp.sum(oh2 * prefix, axis=0, keepdims=True)
    base_ref[...] = base_ref[...] + jnp.sum(oh, axis=1, keepdims=True)
    cnt_ref[...] = base_ref[...]
    ri_ref[...] = jnp.where(sub == 0, e1, jnp.where(sub == 1, e2, jnp.where(sub == 2, rank1,
                            jnp.where(sub == 3, rank2, 0.0))))
    rg_ref[...] = jnp.where(rows == 0, gate1, jnp.where(rows == 1, gate2, 0.0)).T


def _outproj_router(h, oa, ob, wo, g, wr, br):
    N = h.shape[0]
    full = lambda a: pl.BlockSpec(a.shape, lambda i: (0,) * a.ndim)
    row = lambda w: pl.BlockSpec((TM, w), lambda i: (i, 0))
    return pl.pallas_call(
        _outproj_router_kernel,
        grid=(N // TM,),
        in_specs=[row(D), row(512), row(512), full(wo), full(g), full(wr), full(br)],
        out_specs=[row(D), row(HALF), pl.BlockSpec((8, TM), lambda i: (0, i)), row(128),
                   pl.BlockSpec((128, 128), lambda i: (0, 0))],
        out_shape=[jax.ShapeDtypeStruct((N, D), F32), jax.ShapeDtypeStruct((N, HALF), jnp.int32),
                   jax.ShapeDtypeStruct((8, N), F32), jax.ShapeDtypeStruct((N, 128), F32),
                   jax.ShapeDtypeStruct((128, 128), F32)],
        scratch_shapes=[pltpu.VMEM((128, 128), F32)],
        compiler_params=_cparams(("arbitrary",)),
        name="outproj_router",
    )(h, oa, ob, wo, g, wr, br)


def _moe_kernel(te_ref, tv_ref, x_ref, wg_ref, wu_ref, wd_ref, o_ref, wgu_s, wd_s):
    i = pl.program_id(0)
    valid = tv_ref[i]

    @pl.when((i == 0) | (te_ref[i] != te_ref[jnp.maximum(i - 1, 0)]))
    def _():
        wgu_s[:, :D_EXP] = wg_ref[...].astype(BF16)
        wgu_s[:, D_EXP:] = wu_ref[...].astype(BF16)
        wd_s[...] = wd_ref[...].astype(BF16)

    @pl.when(valid > 0)
    def _():
        rows = lax.broadcasted_iota(jnp.int32, x_ref.shape, 0)
        lo, hi = _unpack_halves(jnp.where(rows < valid, x_ref[...], 0))
        hgu = _dot(lo.astype(BF16), wgu_s[:HALF, :]) + _dot(hi.astype(BF16), wgu_s[HALF:, :])
        hg, hu = hgu[:, :D_EXP], hgu[:, D_EXP:]
        a = hg * (1.0 / (1.0 + jnp.exp(-hg))) * hu
        o_ref[...] = _pack_halves(_dot(a.astype(BF16), wd_s[...]))

    @pl.when(valid <= 0)
    def _():
        o_ref[...] = jnp.zeros_like(o_ref)


def _moe_experts(xs, tile_expert, tile_valid, w_gate, w_up, w_down, layer):
    P = xs.shape[0]
    grid_spec = pltpu.PrefetchScalarGridSpec(
        num_scalar_prefetch=2,
        grid=(P // TM_E,),
        in_specs=[pl.BlockSpec((TM_E, HALF), lambda i, te, tv: (i, 0)),
                  pl.BlockSpec((None, None, D, D_EXP), lambda i, te, tv: (layer, te[i], 0, 0)),
                  pl.BlockSpec((None, None, D, D_EXP), lambda i, te, tv: (layer, te[i], 0, 0)),
                  pl.BlockSpec((None, None, D_EXP, D), lambda i, te, tv: (layer, te[i], 0, 0))],
        out_specs=pl.BlockSpec((TM_E, HALF), lambda i, te, tv: (i, 0)),
        scratch_shapes=[pltpu.VMEM((D, 2 * D_EXP), BF16), pltpu.VMEM((D_EXP, D), BF16)],
    )
    return pl.pallas_call(
        _moe_kernel,
        grid_spec=grid_spec,
        out_shape=jax.ShapeDtypeStruct((P, HALF), jnp.int32),
        compiler_params=_cparams(("arbitrary",)),
        name="moe_experts",
    )(tile_expert, tile_valid, xs, w_gate, w_up, w_down)


def _sc_mesh():
    return plsc.VectorSubcoreMesh(core_axis_name="c", subcore_axis_name="s",
                                  num_cores=SC_CORES, num_subcores=SC_SUBCORES)


def _sc_scatter_rows(src, d1, d2, P):
    N, W = src.shape
    per_w = N // (SC_CORES * SC_SUBCORES)

    @functools.partial(pl.kernel, mesh=_sc_mesh(), out_type=jax.ShapeDtypeStruct((P, W), src.dtype),
                       scratch_types=[pltpu.VMEM((SC_CHUNK,), jnp.int32), pltpu.VMEM((SC_CHUNK, W), src.dtype)],
                       name="sc_scatter_rows")
    def k(src_hbm, d1_hbm, d2_hbm, out_hbm, idx_v, rows_v):
        wid = lax.axis_index("s") * SC_CORES + lax.axis_index("c")

        @pl.loop(0, per_w // SC_CHUNK)
        def _(c):
            off = wid * per_w + c * SC_CHUNK
            pltpu.sync_copy(src_hbm.at[pl.ds(off, SC_CHUNK)], rows_v)
            pltpu.sync_copy(d1_hbm.at[pl.ds(off, SC_CHUNK)], idx_v)
            pltpu.sync_copy(rows_v, out_hbm.at[idx_v])
            pltpu.sync_copy(d2_hbm.at[pl.ds(off, SC_CHUNK)], idx_v)
            pltpu.sync_copy(rows_v, out_hbm.at[idx_v])

    return k(src, d1, d2)


def _sc_gather_rows(table, idx):
    B, W = idx.shape[0], table.shape[1]
    per_w = B // (SC_CORES * SC_SUBCORES)

    @functools.partial(pl.kernel, mesh=_sc_mesh(), out_type=jax.ShapeDtypeStruct((B, W), table.dtype),
                       scratch_types=[pltpu.VMEM((SC_CHUNK,), jnp.int32), pltpu.VMEM((SC_CHUNK, W), table.dtype)],
                       name="sc_gather_rows")
    def k(table_hbm, idx_hbm, out_hbm, idx_v, rows_v):
        wid = lax.axis_index("s") * SC_CORES + lax.axis_index("c")

        @pl.loop(0, per_w // SC_CHUNK)
        def _(c):
            off = wid * per_w + c * SC_CHUNK
            pltpu.sync_copy(idx_hbm.at[pl.ds(off, SC_CHUNK)], idx_v)
            pltpu.sync_copy(table_hbm.at[idx_v], rows_v)
            pltpu.sync_copy(rows_v, out_hbm.at[pl.ds(off, SC_CHUNK)])

    return k(table, idx)


def _dest_kernel(offs_ref, ri_ref, d_ref):
    ri = ri_ref[...]
    base = jnp.zeros(ri.shape, F32)
    for e in range(N_EXP):
        base = jnp.where(ri == e, offs_ref[e].astype(F32), base)
    d_ref[...] = (base + pltpu.roll(ri, 6, 0)).astype(jnp.int32)


def _dispatch_plan(ri, cnt, N):
    counts = cnt[:N_EXP, 0].astype(jnp.int32)
    padded = ((counts + TM_E - 1) // TM_E) * TM_E
    ends = jnp.cumsum(padded)
    offs = ends - padded
    take = lambda t, i: t.at[i].get(mode="promise_in_bounds")
    dest = pl.pallas_call(
        _dest_kernel,
        in_specs=[pl.BlockSpec(memory_space=pltpu.SMEM), pl.BlockSpec((8, N), lambda: (0, 0))],
        out_specs=pl.BlockSpec((8, N), lambda: (0, 0)),
        out_shape=jax.ShapeDtypeStruct((8, N), jnp.int32),
        name="dispatch_rows",
    )(offs, ri)
    d1, d2 = dest[0], dest[1]
    P = 2 * N + N_EXP * TM_E
    tile_start = jnp.arange(P // TM_E, dtype=jnp.int32) * TM_E
    tile_expert = jnp.minimum(jnp.sum((ends[None, :] <= tile_start[:, None]).astype(jnp.int32), axis=1), N_EXP - 1)
    tile_valid = jnp.clip(take(offs + counts, tile_expert) - tile_start, 0, TM_E)
    return d1, d2, P, tile_expert, tile_valid


def _combine(h_ref, y1_ref, y2_ref, rg_ref):
    rg = rg_ref[...]
    lo1, hi1 = _unpack_halves(y1_ref[...])
    lo2, hi2 = _unpack_halves(y2_ref[...])
    g1, g2 = rg[:, 0:1], rg[:, 1:2]
    return h_ref[...] + jnp.concatenate([g1 * lo1 + g2 * lo2, g1 * hi1 + g2 * hi2], axis=1)


ODD_WIDTHS = (1024, 1024, 512, 512)


def _combine_proj_odd_kernel(h_ref, y1_ref, y2_ref, rg_ref, g_ref, win_ref, wvt_ref, ind_ref, hn_ref, vt_ref,
                             vdt_ref, *out_refs):
    hn = _combine(h_ref, y1_ref, y2_ref, rg_ref)
    hn_ref[...] = hn
    u = _rms(hn, g_ref[...]).astype(BF16)
    vt_ref[...] = _values_t(_dot_nt(wvt_ref[0:512, :], u)).astype(BF16)
    vdt_ref[...] = _dot_nt(wvt_ref[512:1024, :], u).astype(BF16)
    qc_ref, kc_ref, qd_ref, kd_ref = out_refs
    qc = _spread_heads(_dot(u, win_ref[:, 0:512]), lambda h: 0)
    kc = _spread_heads(_dot(u, win_ref[:, 512:1024]), lambda h: 0)
    for hh in range(8):
        qc_ref[:, 128 * hh:128 * (hh + 1)] = qc[hh].astype(BF16)
        kc_ref[:, 128 * hh:128 * (hh + 1)] = (kc[hh] + ind_ref[...]).astype(BF16)
    qd_ref[...] = _dot(u, win_ref[:, 1024:1536]).astype(BF16)
    kd_ref[...] = _dot(u, win_ref[:, 1536:2048]).astype(BF16)


def _odd_weights(w_in):
    s = HD ** -0.5 * LOG2E
    scale = np.asarray([s] * 512 + [1.0] * 512 + [s] * 512 + [1.0] * 512, np.float32)
    win = jnp.concatenate([w_in[:, 0:1024], w_in[:, 1536:2560]], axis=1) * jnp.asarray(scale)
    wvt = jnp.concatenate([w_in[:, 1024:1536], w_in[:, 2560:3072]], axis=1).T
    return win.astype(BF16), wvt.astype(BF16)


def _block_indicator(T):
    assert T // MOBA_L <= 64
    ind = np.zeros((T, 128), np.float32)
    ind[np.arange(T), 64 + np.arange(T) // MOBA_L] = 1.0
    return jnp.asarray(ind)


def _y_specs(N):
    return [pl.BlockSpec((TM, HALF), lambda i: (i, 0)), pl.BlockSpec((TM, HALF), lambda i: (i + N // TM, 0))]


def _combine_proj_odd(h, y12, rg, g, win, wvt, ind, T):
    N = h.shape[0]
    nt = T // TM
    full = lambda a: pl.BlockSpec(a.shape, lambda i: (0,) * a.ndim)
    row = lambda w: pl.BlockSpec((TM, w), lambda i: (i, 0))
    return pl.pallas_call(
        _combine_proj_odd_kernel,
        grid=(N // TM,),
        in_specs=[row(D)] + _y_specs(N) + [row(128), full(g), full(win), full(wvt),
                  pl.BlockSpec((TM, 128), lambda i: (i % nt, 0))],
        out_specs=[row(D), pl.BlockSpec((None, 8 * VT_ROWS, TM), lambda i: (i // nt, 0, i % nt)),
                   pl.BlockSpec((None, 512, TM), lambda i: (i // nt, 0, i % nt))]
        + [row(w) for w in ODD_WIDTHS],
        out_shape=[jax.ShapeDtypeStruct((N, D), F32), jax.ShapeDtypeStruct((N // T, 8 * VT_ROWS, T), BF16),
                   jax.ShapeDtypeStruct((N // T, 512, T), BF16)]
        + [jax.ShapeDtypeStruct((N, w), BF16) for w in ODD_WIDTHS],
        compiler_params=_cparams(("arbitrary",)),
        name="combine_proj_odd",
    )(h, y12, y12, rg, g, win, wvt, ind)


def _combine_final_kernel(h_ref, y1_ref, y2_ref, rg_ref, g_ref, o_ref):
    o_ref[...] = _rms(_combine(h_ref, y1_ref, y2_ref, rg_ref), g_ref[...])


def _combine_final(h, y12, rg, g):
    N = h.shape[0]
    row = lambda w: pl.BlockSpec((TM, w), lambda i: (i, 0))
    return pl.pallas_call(
        _combine_final_kernel,
        grid=(N // TM,),
        in_specs=[row(D)] + _y_specs(N) + [row(128), pl.BlockSpec((1, D), lambda i: (0, 0))],
        out_specs=row(D),
        out_shape=jax.ShapeDtypeStruct((N, D), F32),
        compiler_params=_cparams(("arbitrary",)),
        name="combine_final",
    )(h, y12, y12, rg, g)


def _moba_kernel(q_ref, k_ref, vt_ref, tb_ref, o_ref, kmean_ref, qa_ref, s_ref, smax_ref, m_ref, acc_ref, *, nkb,
                 nq):
    ia = pl.program_id(2)
    L, tq = MOBA_L, MOBA_TQ

    @pl.when(ia == 0)
    def _():
        kmean_ref[...] = jnp.zeros_like(kmean_ref)
        for n in range(nkb):
            kmean_ref[64 + n:65 + n, :] = jnp.sum(k_ref[n * L:(n + 1) * L, :].astype(F32), axis=0,
                                                  keepdims=True) * (1.0 / L)

    nb = -(-nkb // 8) * 8
    lane = lax.broadcasted_iota(jnp.int32, (tq, 128), 1)
    blk = lax.broadcasted_iota(jnp.int32, (nb, tq), 0)
    half = jnp.where(lax.broadcasted_iota(jnp.int32, (nb, tq), 1) >= L, 1, 0)
    blk_f = blk.astype(F32)
    for t, tile in enumerate((ia, nq - 1 - ia)):
        own = 2 * tile + half
        for j in range(2):
            qj = q_ref[pl.ds(pl.multiple_of(tile * tq, tq), tq), 128 * j:128 * (j + 1)]
            gsc = lax.dot_general(kmean_ref[64:64 + nb, 128 * j:128 * (j + 1)], qj.astype(F32),
                                  (((1,), (1,)), ((), ())), preferred_element_type=F32,
                                  precision=lax.Precision.HIGHEST)
            g = jnp.where(blk < own, gsc, NEG)
            allowed = jnp.where(blk == own, 1.0, 0.0)
            for _ in range(MOBA_TOPK):
                mx = jnp.max(g, axis=0, keepdims=True)
                idx = jnp.min(jnp.where(g == mx, blk_f, 1e6), axis=0, keepdims=True)
                pick = blk_f == idx
                allowed = jnp.maximum(allowed, jnp.where(pick, jnp.where(mx > 0.5 * NEG, 1.0, 0.0), 0.0))
                g = jnp.where(pick, 2.0 * NEG, g)
            mask = jnp.concatenate([jnp.zeros((64, tq), F32), jnp.where(allowed > 0.5, 0.0, NEG),
                                    jnp.zeros((64 - nb, tq), F32)], axis=0)
            qa_ref[t, j] = jnp.where(lane < HD, qj, mask.T.astype(BF16))

    def bias_of(start, q0, j):
        parts = []
        for c in range(tq // L):
            off = jnp.minimum(q0 - start - c * L, BIAS_CONST_FROM) + BIAS_PAD
            parts.append(tb_ref[j, :, pl.ds(pl.multiple_of(off, L), tq)])
        return jnp.concatenate(parts, axis=0)

    _causal_attention_pair_t(ia, nq, lambda t, q0, j: qa_ref[t, j], k_ref, vt_ref, o_ref, s_ref, smax_ref, m_ref,
                             acc_ref, tq=tq, bias_of=bias_of)


def _moba_attention(q, k, vt, tb, B, T):
    tq = MOBA_TQ
    nq = T // tq
    return pl.pallas_call(
        functools.partial(_moba_kernel, nkb=T // MOBA_L, nq=nq),
        grid=(B, 4, nq // 2),
        in_specs=[pl.BlockSpec((T, 256), lambda b, p, i: (b, p)),
                  pl.BlockSpec((T, 256), lambda b, p, i: (b, p)),
                  pl.BlockSpec((None, 2 * VT_ROWS, T), lambda b, p, i: (b, p, 0)),
                  pl.BlockSpec((2, MOBA_L, BIAS_ROWS), lambda b, p, i: (p, 0, 0))],
        out_specs=pl.BlockSpec((T, 128), lambda b, p, i: (b, p)),
        out_shape=jax.ShapeDtypeStruct((B * T, 512), BF16),
        scratch_shapes=[pltpu.VMEM((128, 256), F32), pltpu.VMEM((2, 2, tq, 128), BF16)]
        + _attention_t_scratch(tq),
        compiler_params=_cparams(("arbitrary", "arbitrary", "arbitrary")),
        name="moba_attention",
    )(q, k, vt, tb)


def _sb_kernel(q_ref, k_ref, vt_ref, o_ref, *, tq, pairs):
    i = pl.program_id(2)
    lane = lax.broadcasted_iota(jnp.int32, (tq, 128), 1)
    qs = []
    for p in range(pairs):
        q_pair = q_ref[:, 128 * p:128 * (p + 1)]
        qs += [jnp.where(lane < HD, q_pair, 0).astype(BF16), jnp.where(lane >= HD, q_pair, 0).astype(BF16)]
    key = lax.broadcasted_iota(jnp.int32, (tq, tq), 0)
    qry = lax.broadcasted_iota(jnp.int32, (tq, tq), 1)
    past = key < qry
    suffix = jnp.where(qry >= key, -1.0, 0.0).astype(BF16)
    heads = range(2 * pairs)

    def step(kb, carry, boundary):
        start = pl.multiple_of(kb * tq, tq)
        zs = [_dot_nt(k_ref[pl.ds(start, tq), 128 * (h // 2):128 * (h // 2 + 1)], qs[h]) for h in heads]
        parts = []
        for z in zs:
            neg_abs = pltpu.bitcast(pltpu.bitcast(z, jnp.uint32) | jnp.uint32(0x80000000), F32)
            soft = jnp.maximum(z, 0.0) + jnp.log2(1.0 + jnp.exp2(neg_abs))
            if boundary:
                soft = jnp.where(past, soft, 0.0)
            hi = soft.astype(BF16)
            parts.append((hi, (soft - hi.astype(F32)).astype(BF16)))
        sums = [_dot(suffix, hi) + _dot(suffix, lo) for hi, lo in parts]
        ws = []
        for h in heads:
            w = jnp.exp2(zs[h] + (carry[h][0] + sums[h]))
            ws.append((jnp.where(past, w, 0.0) if boundary else w).astype(BF16))
        return tuple((carry[h][0] + sums[h][0:1], carry[h][1]
                      + _dot(vt_ref[128 * (h // 2):128 * (h // 2 + 1), pl.ds(start, tq)], ws[h])) for h in heads)

    def live(carry):
        top = carry[0][0]
        for c, _ in carry[1:]:
            top = jnp.maximum(top, c)
        return (jnp.max(top) > SB_DONE).astype(jnp.int32)

    def body(state):
        t, _, carry = state
        carry = step(i - 1 - t, carry, False)
        return t + 1, live(carry), carry

    init = tuple((jnp.zeros((1, tq), F32), jnp.zeros((128, tq), F32)) for _ in range(2 * pairs))
    carry = step(i, init, True)
    _, _, carry = lax.while_loop(lambda st: (st[0] < i) & (st[1] > 0), body, (jnp.int32(0), live(carry), carry))
    rows = lax.broadcasted_iota(jnp.int32, (128, tq), 0)
    for p in range(pairs):
        ot = jnp.where(rows < HD, carry[2 * p][1], carry[2 * p + 1][1])
        o_ref[:, 128 * p:128 * (p + 1)] = ot.T.astype(BF16)


def _sb_attention(q, k, vt, B, T, tq=256, pairs=4):
    nq = T // tq
    w = 128 * pairs
    return pl.pallas_call(
        functools.partial(_sb_kernel, tq=tq, pairs=pairs),
        grid=(B, 4 // pairs, nq),
        in_specs=[pl.BlockSpec((tq, w), lambda b, p, i: (b * nq + i, p)),
                  pl.BlockSpec((T, w), lambda b, p, i: (b, p)),
                  pl.BlockSpec((None, w, T), lambda b, p, i: (b, p, 0))],
        out_specs=pl.BlockSpec((tq, w), lambda b, p, i: (b * nq + i, p)),
        out_shape=jax.ShapeDtypeStruct((B * T, 512), BF16),
        compiler_params=_cparams(("arbitrary", "arbitrary", "arbitrary")),
        name="sb_attention",
    )(q, k, vt)


def _router_weights(w_group, b_group, w_router, b_router):
    pad = 128 - N_EXP - N_GROUPS
    wr = jnp.concatenate([w_router, w_group, jnp.zeros((D, pad), F32)], axis=1).T
    hi = wr.astype(BF16)
    lo = (wr - hi.astype(F32)).astype(BF16)
    br = jnp.concatenate([b_router, b_group, jnp.zeros((pad,), F32)])[:, None]
    return jnp.concatenate([hi, lo], axis=0), br


def _ffn(h, oa, ob, wo, g, w_group, b_group, w_router, b_router, w_gate, w_up, w_down, layer):
    wr, br = _router_weights(w_group, b_group, w_router, b_router)
    hn, u, ri, rg, cnt = _outproj_router(h, oa, ob, wo.astype(BF16), g[None, :], wr, br)
    d1, d2, P, tile_expert, tile_valid = _dispatch_plan(ri, cnt, h.shape[0])
    xs = _sc_scatter_rows(u, d1, d2, P)
    ys = _moe_experts(xs, tile_expert, tile_valid, w_gate, w_up, w_down, layer)
    return hn, _sc_gather_rows(ys, jnp.concatenate([d1, d2])), rg


def kernel(x, norm_mix, norm_ffn, norm_final, rel_bias_table, w_in_even, g_mla_q, g_mla_kv, w_mla_qb, w_mla_kvb, swa_sinks, w_out_even, w_in_odd, w_out_odd, moe_w_group, moe_b_group, moe_w_router, moe_b_router, moe_w_gate, moe_w_up, moe_w_down):
    B, T, _ = x.shape
    h = x.reshape(B * T, D)
    tb_swa, tb_moba = _bias_tables(rel_bias_table)

    win, wqa, wqb, wk, wv = _even_weights(w_in_even[0], w_mla_qb[0], w_mla_kvb[0])
    q, k, v, qs8, ks, vst = _proj_even(h, norm_mix[0][None, :], win, g_mla_q[0][None, :], g_mla_kv[0][None, :],
                                       wqa, wqb, wk, wv, _rope_tables(T), T)
    oa = _mla_attention(q, k, v, B, T)
    ob = _swa_attention(qs8, ks, vst, swa_sinks[0], tb_swa, B, T)
    h, y12, rg = _ffn(h, oa, ob, w_out_even[0], norm_ffn[0], moe_w_group[0], moe_b_group[0], moe_w_router[0],
                      moe_b_router[0], moe_w_gate, moe_w_up, moe_w_down, 0)

    h, vct, vdt, qc, kc, qd, kd = _combine_proj_odd(h, y12, rg, norm_mix[1][None, :], *_odd_weights(w_in_odd[0]),
                                                   _block_indicator(T), T)
    oc = _moba_attention(qc, kc, vct, tb_moba, B, T)
    od = _sb_attention(qd, kd, vdt, B, T)
    h, y12, rg = _ffn(h, oc, od, w_out_odd[0], norm_ffn[1], moe_w_group[1], moe_b_group[1], moe_w_router[1],
                      moe_b_router[1], moe_w_gate, moe_w_up, moe_w_down, 1)
    out = _combine_final(h, y12, rg, norm_final[None, :])
    return out.reshape(B, T, D)
```

```python
import functools
import math

import numpy as np
import jax
import jax.numpy as jnp
from jax import lax
from jax.experimental import pallas as pl
from jax.experimental.pallas import tpu as pltpu
from jax.experimental.pallas import tpu_sc as plsc

F32 = jnp.float32
BF16 = jnp.bfloat16
NEG = -1e30
EPS = 1e-6

D = 1024
HD = 64
LANES = 128
MLA_HEADS, MLA_QR, MLA_KVR, MLA_NOPE, MLA_ROPE, MLA_V = 8, 256, 128, 64, 32, 64
ROPE_THETA = 10000.0
SWA_HEADS, SWA_KV, SWA_WIN = 8, 2, 128
MOBA_L, MOBA_TOPK = 256, 3
REL_BUCKETS, REL_MAX = 32, 2048
N_GROUPS, EPG, N_EXP, D_EXP = 4, 8, 32, 256
BIAS_CONST_FROM = 1792
MOBA_TQ = 2 * MOBA_L
BIAS_PAD = MOBA_L
BIAS_ROWS = BIAS_PAD + BIAS_CONST_FROM + MOBA_TQ
SB_DONE = -160.0
TM = 512
TM_E = 512
VMEM_LIMIT = 56 * 1024 * 1024
SC_CORES, SC_SUBCORES = 2, 16
SC_CHUNK = 128
HALF = D // 2
VT_ROWS = HD + 16
LOG2E = math.log2(math.e)


def _pack_halves(x):
    lo = pltpu.bitcast(x[:, :HALF].astype(BF16).astype(F32), jnp.uint32) >> 16
    hi = pltpu.bitcast(x[:, HALF:].astype(BF16).astype(F32), jnp.uint32) & jnp.uint32(0xFFFF0000)
    return pltpu.bitcast(lo | hi, jnp.int32)


def _unpack_halves(w):
    w = pltpu.bitcast(w, jnp.uint32)
    return pltpu.bitcast(w << 16, F32), pltpu.bitcast(w & jnp.uint32(0xFFFF0000), F32)


def _cparams(sem):
    return pltpu.CompilerParams(dimension_semantics=sem, vmem_limit_bytes=VMEM_LIMIT)


def _dot(a, b):
    return jnp.dot(a, b, preferred_element_type=F32)


def _dot_nt(a, b):
    return lax.dot_general(a, b, (((1,), (1,)), ((), ())), preferred_element_type=F32)


def _rms(x, g):
    return x * lax.rsqrt(jnp.mean(x * x, axis=-1, keepdims=True) + EPS) * g


def _values_t(vt):
    ones = jnp.ones((VT_ROWS - HD, vt.shape[1]), vt.dtype)
    return jnp.concatenate([blk for h in range(8) for blk in (vt[HD * h:HD * (h + 1)], ones)], axis=0)


def _spread_heads(x, lane_of):
    lane = lax.broadcasted_iota(jnp.int32, (x.shape[0], 128), 1)
    out = []
    for h in range(8):
        pair = x[:, 128 * (h // 2):128 * (h // 2 + 1)]
        if HD * (h % 2) != lane_of(h):
            pair = pltpu.roll(pair, HD, 1)
        out.append(jnp.where((lane >= lane_of(h)) & (lane < lane_of(h) + HD), pair, 0.0))
    return out


def _flash_update_t(st, vt, m, acc):
    m_new = jnp.maximum(m, jnp.max(st, axis=0, keepdims=True))
    alpha = jnp.exp2(m - m_new)
    pt = jnp.exp2(st - m_new).astype(BF16)
    return m_new, alpha * acc + _dot(vt, pt)


def _flash_update(s, v, m, l, acc):
    m_new = jnp.maximum(m, jnp.max(s, axis=-1, keepdims=True))
    alpha = jnp.exp(m - m_new)
    p = jnp.exp(s - m_new)
    l = alpha * l + jnp.sum(p, axis=-1, keepdims=True)
    acc = alpha * acc + _dot(p.astype(BF16), v)
    return m_new, l, acc


def _t5_bucket_np(dist):
    n = np.maximum(dist, 0)
    max_exact = REL_BUCKETS // 2
    nf = np.maximum(n, 1).astype(np.float32)
    large = max_exact + (np.log(nf / np.float32(max_exact)) / np.float32(math.log(REL_MAX / max_exact))
                         * np.float32(REL_BUCKETS - max_exact)).astype(np.int32)
    large = np.minimum(large, REL_BUCKETS - 1)
    return np.where(n < max_exact, n, large).astype(np.int32)


def _bias_kernel(table_ref, bucket_ref, out_ref, *, scale, ranges):
    b = bucket_ref[...]
    for m, (lo, hi) in enumerate(ranges):
        @pl.when(pl.program_id(0) == m)
        def _():
            for h in range(8):
                acc = jnp.full(b.shape, table_ref[lo, h] * scale, F32)
                for k in range(lo + 1, hi + 1):
                    acc = jnp.where(b == k, table_ref[k, h] * scale, acc)
                out_ref[h] = acc


def _bias_lookup(rel_table, dist, scale, name):
    R, C = dist.shape
    buckets = _t5_bucket_np(dist)
    ranges = tuple((int(buckets[:, c:c + 256].min()), int(buckets[:, c:c + 256].max())) for c in range(0, C, 256))
    return pl.pallas_call(
        functools.partial(_bias_kernel, scale=scale, ranges=ranges),
        grid=(C // 256,),
        in_specs=[pl.BlockSpec(memory_space=pltpu.SMEM),
                  pl.BlockSpec((R, 256), lambda m: (0, m))],
        out_specs=pl.BlockSpec((8, R, 256), lambda m: (0, 0, m)),
        out_shape=jax.ShapeDtypeStruct((8, R, C), F32),
        compiler_params=_cparams(("arbitrary",)),
        name=name,
    )(rel_table, jnp.asarray(buckets))


def _bias_tables(rel_table):
    swa = _bias_lookup(rel_table, np.arange(256)[None, :] - np.arange(256)[:, None], LOG2E, "bias_swa")
    moba = _bias_lookup(rel_table, np.arange(BIAS_ROWS)[None, :] - BIAS_PAD - np.arange(MOBA_L)[:, None], LOG2E,
                        "bias_moba")
    return swa, moba


def _proj_even_kernel(h_ref, g_ref, win_ref, gq_ref, gkv_ref, wqa_ref, wqb_ref, wk_ref, wv_ref,
                      cq_ref, sq_ref, ck_ref, sk_ref,
                      q_ref, k_ref, v_ref, qs_ref, ks_ref, vst_ref):
    u = _rms(h_ref[...], g_ref[...]).astype(BF16)
    nq = _rms(_dot(u, win_ref[:, 0:256]), gq_ref[...]).astype(BF16)
    qa = _dot(nq, wqa_ref[...])
    qb = _dot(nq, wqb_ref[...])
    cq, sq = cq_ref[...], sq_ref[...]
    for hh in range(MLA_HEADS):
        sl = slice(128 * hh, 128 * (hh + 1))
        q_ref[:, sl] = (qa[:, sl] * cq + qb[:, sl] * sq).astype(BF16)
    nkv = _rms(_dot(u, win_ref[:, 256:384]), gkv_ref[...]).astype(BF16)
    kn = _dot(nkv, wk_ref[...])
    v_ref[...] = _values_t(_dot_nt(wv_ref[...], nkv)).astype(BF16)
    rest = _dot(u, win_ref[:, 896:1408])
    kr = rest[:, 128:256] * ck_ref[...] + rest[:, 256:384] * sk_ref[...]
    for hh in range(MLA_HEADS):
        sl = slice(128 * hh, 128 * (hh + 1))
        k_ref[:, sl] = (kn[:, sl] + kr).astype(BF16)
    qs = _spread_heads(_dot(u, win_ref[:, 384:896]), lambda hq: HD * (hq // (SWA_HEADS // SWA_KV)))
    for hq in range(SWA_HEADS):
        qs_ref[:, 128 * hq:128 * (hq + 1)] = qs[hq].astype(BF16)
    ks_ref[...] = rest[:, 0:128].astype(BF16)
    vst_ref[...] = rest[:, 384:512].T.astype(BF16)


def _cols(w, idx, scale=None):
    idx = np.asarray(idx)
    cuts = [0] + [n for n in range(1, len(idx)) if (idx[n] < 0) != (idx[n - 1] < 0)
                  or (idx[n] >= 0 and idx[n] != idx[n - 1] + 1)] + [len(idx)]
    parts = [jnp.zeros((w.shape[0], b - a), w.dtype) if idx[a] < 0 else w[:, idx[a]:idx[a] + b - a]
             for a, b in zip(cuts[:-1], cuts[1:])]
    out = jnp.concatenate(parts, axis=1)
    if scale is not None:
        out = out * scale
    return out


def _even_weights(w_in, w_qb, w_kvb):
    o_cq, o_ckv, o_kr = 0, MLA_QR, MLA_QR + MLA_KVR
    o_qs = o_kr + MLA_ROPE
    o_ks = o_qs + SWA_HEADS * HD
    o_vs = o_ks + SWA_KV * HD
    idx = list(range(o_cq, o_cq + MLA_QR)) + list(range(o_ckv, o_ckv + MLA_KVR))
    scale = [1.0] * len(idx)
    idx += list(range(o_qs, o_qs + SWA_HEADS * HD))
    scale += [HD ** -0.5 * LOG2E] * (SWA_HEADS * HD)
    idx += list(range(o_ks, o_ks + 128))
    scale += [1.0] * 128
    half = MLA_ROPE // 2
    idx += [-1] * 64 + list(range(o_kr, o_kr + MLA_ROPE)) + [-1] * 32
    idx += [-1] * 64 + list(range(o_kr + half, o_kr + MLA_ROPE)) + list(range(o_kr, o_kr + half)) + [-1] * 32
    idx += list(range(o_vs, o_vs + 128))
    scale += [1.0] * 384
    win = _cols(w_in, idx, jnp.asarray(np.asarray(scale, np.float32))).astype(BF16)
    qw = MLA_NOPE + MLA_ROPE
    ia, ib, ik, iv = [], [], [], []
    for h in range(MLA_HEADS):
        ia += list(range(qw * h, qw * h + qw)) + [-1] * 32
        ib += [-1] * 64 + list(range(qw * h + MLA_NOPE + half, qw * h + qw)) \
            + list(range(qw * h + MLA_NOPE, qw * h + MLA_NOPE + half)) + [-1] * 32
        ik += list(range(128 * h, 128 * h + MLA_NOPE)) + [-1] * 64
        iv += list(range(128 * h + MLA_NOPE, 128 * h + 128))
    return (win, _cols(w_qb, ia).astype(BF16), _cols(w_qb, ib).astype(BF16),
            _cols(w_kvb, ik).astype(BF16), _cols(w_kvb, iv).T.astype(BF16))


def _rope_tables(T):
    f32 = np.float32
    freqs = f32(ROPE_THETA) ** (-np.arange(0, MLA_ROPE, 2, dtype=f32) / f32(MLA_ROPE))
    ang = np.arange(T, dtype=f32)[:, None] * freqs[None, :]
    cos, sin = np.cos(ang), np.sin(ang)
    z = lambda n: np.zeros((T, n), f32)
    scale = f32((MLA_NOPE + MLA_ROPE) ** -0.5 * LOG2E)
    ck = np.concatenate([z(64), cos, cos, z(32)], axis=1)
    sk = np.concatenate([z(64), -sin, sin, z(32)], axis=1)
    cq = np.concatenate([np.ones((T, 64), f32), cos, cos, z(32)], axis=1) * scale
    sq = sk * scale
    return tuple(jnp.asarray(t, F32) for t in (cq, sq, ck, sk))


def _proj_even(h, g, win, gq, gkv, wqa, wqb, wk, wv, tables, T):
    N = h.shape[0]
    nt = T // TM
    full = lambda a: pl.BlockSpec(a.shape, lambda i: (0,) * a.ndim)
    tab = pl.BlockSpec((TM, 128), lambda i: (i % nt, 0))
    row = lambda w: pl.BlockSpec((TM, w), lambda i: (i, 0))
    outs = (1024, 1024, -8 * VT_ROWS, 1024, 128, -128)
    spec = lambda w: row(w) if w > 0 else pl.BlockSpec((None, -w, TM), lambda i: (i // nt, 0, i % nt))
    shape = lambda w: jax.ShapeDtypeStruct((N, w) if w > 0 else (N // T, -w, T), BF16)
    return pl.pallas_call(
        _proj_even_kernel,
        grid=(N // TM,),
        in_specs=[row(D), full(g), full(win), full(gq), full(gkv), full(wqa), full(wqb), full(wk), full(wv),
                  tab, tab, tab, tab],
        out_specs=[spec(w) for w in outs],
        out_shape=[shape(w) for w in outs],
        compiler_params=_cparams(("arbitrary",)),
        name="proj_even",
    )(h, g, win, gq, gkv, wqa, wqb, wk, wv, *tables)


def _finish_t(accs):
    ot = jnp.concatenate([acc[:HD] / acc[HD:HD + 1] for acc in accs], axis=0)
    return ot.T.astype(BF16)


def _causal_attention_pair_t(ia, nq, q_of, k_ref, vt_ref, o_ref, s_ref, smax_ref, m_ref, acc_ref, *, tq,
                             bias_of=None):
    nh = s_ref.shape[1]
    first_query = (ia * tq, (nq - 1 - ia) * tq)
    n_items = nq + 1
    assert n_items % 2 == 1
    key = lax.broadcasted_iota(jnp.int32, (tq, tq), 0)
    qry = lax.broadcasted_iota(jnp.int32, (tq, tq), 1)

    def item(p):
        if isinstance(p, int) and p < 2:
            return p, first_query[p], first_query[p]
        t = jnp.where(p - 2 >= ia, 1, 0)
        return t, jnp.where(t == 1, first_query[1], first_query[0]), (p - 2 - t * ia) * tq

    def scores_to(slot, p):
        t, q0, start = item(p)
        start = pl.multiple_of(start, tq)
        for j in range(nh):
            st = _dot_nt(k_ref[pl.ds(start, tq), 128 * j:128 * (j + 1)], q_of(t, q0, j))
            if bias_of is not None:
                st = st + bias_of(start, q0, j)
            if isinstance(p, int) and p < 2:
                st = jnp.where(key <= qry, st, NEG)
            s_ref[slot, j] = st
            smax_ref[slot, j] = jnp.max(st, axis=0, keepdims=True)

    def update_from(slot, p):
        t, _, start = item(p)
        start = pl.multiple_of(start, tq)
        m_new = [jnp.maximum(m_ref[t, j], smax_ref[slot, j]) for j in range(nh)]
        pts = [jnp.exp2(s_ref[slot, j] - m_new[j]).astype(BF16) for j in range(nh)]
        for j in range(nh):
            acc_ref[t, j] = (jnp.exp2(m_ref[t, j] - m_new[j]) * acc_ref[t, j]
                             + _dot(vt_ref[VT_ROWS * j:VT_ROWS * (j + 1), pl.ds(start, tq)], pts[j]))
            m_ref[t, j] = m_new[j]

    def pair(k):
        scores_to(1, k + 1)
        update_from(0, k)
        scores_to(0, k + 2)
        update_from(1, k + 1)

    def quad(c, _):
        pair(2 + 4 * c)
        pair(4 + 4 * c)
        return 0

    m_ref[...] = jnp.full(m_ref.shape, NEG, F32)
    acc_ref[...] = jnp.zeros(acc_ref.shape, F32)
    scores_to(0, 0)
    pair(0)
    later_pairs = (n_items - 1) // 2 - 1
    lax.fori_loop(0, later_pairs // 2, quad, 0)
    if later_pairs % 2:
        pair(n_items - 3)
    update_from(0, n_items - 1)
    for t in range(2):
        o_ref[pl.ds(pl.multiple_of(first_query[t], tq), tq), :] = _finish_t([acc_ref[t, j] for j in range(nh)])


def _attention_t_scratch(tq, nh=2):
    return [pltpu.VMEM((2, nh, tq, tq), F32), pltpu.VMEM((2, nh, 1, tq), F32), pltpu.VMEM((2, nh, 1, tq), F32),
            pltpu.VMEM((2, nh, VT_ROWS, tq), F32)]


def _mla_kernel(q_ref, k_ref, vt_ref, o_ref, s_ref, smax_ref, m_ref, acc_ref, *, tq, nq):
    def q_of(t, q0, j):
        return q_ref[pl.ds(pl.multiple_of(q0, tq), tq), 128 * j:128 * (j + 1)]

    _causal_attention_pair_t(pl.program_id(2), nq, q_of, k_ref, vt_ref, o_ref, s_ref, smax_ref, m_ref, acc_ref,
                             tq=tq)


def _mla_attention(q, k, vt, B, T, tq=512):
    nq = T // tq
    return pl.pallas_call(
        functools.partial(_mla_kernel, tq=tq, nq=nq),
        grid=(B, MLA_HEADS // 2, nq // 2),
        in_specs=[pl.BlockSpec((T, 256), lambda b, p, i: (b, p)),
                  pl.BlockSpec((T, 256), lambda b, p, i: (b, p)),
                  pl.BlockSpec((None, 2 * VT_ROWS, T), lambda b, p, i: (b, p, 0))],
        out_specs=pl.BlockSpec((T, 128), lambda b, p, i: (b, p)),
        out_shape=jax.ShapeDtypeStruct((B * T, MLA_HEADS * MLA_V), BF16),
        scratch_shapes=_attention_t_scratch(tq),
        compiler_params=_cparams(("arbitrary", "arbitrary", "arbitrary")),
        name="mla_attention",
    )(q, k, vt)


def _swa_kernel(sink_ref, q_ref, k_ref, vt_ref, tb_ref, o_ref, *, tq):
    i = pl.program_id(1)
    G = SWA_HEADS // SWA_KV
    W = 2 * SWA_WIN
    kj = lax.broadcasted_iota(jnp.int32, (W, G * 128), 0)
    lane = lax.broadcasted_iota(jnp.int32, (W, G * 128), 1)
    qi = lane & 127
    head = lax.broadcasted_iota(jnp.int32, (1, G * 128), 1) >> 7
    sinks = []
    for c in range(SWA_KV):
        sink = jnp.zeros((1, G * 128), F32)
        for g in range(G):
            sink = jnp.where(head == g, sink_ref[G * c + g], sink)
        sinks.append(sink)
    scores, values = [], []
    for r in range(tq // 128):
        qstart = i * tq + 128 * r
        kstart = pl.multiple_of(jnp.maximum(qstart - SWA_WIN, 0), 128)
        off = pl.multiple_of(qstart - kstart, 128)
        dist = off + qi - kj
        valid = (dist >= 0) & (dist < SWA_WIN)
        kw = k_ref[pl.ds(kstart, W), :]
        for c in range(SWA_KV):
            qg = jnp.concatenate([q_ref[128 * r:128 * (r + 1), 128 * (G * c + g):128 * (G * c + g + 1)]
                                  for g in range(G)], axis=0)
            bias = jnp.concatenate([tb_ref[G * c + g, :, pl.ds(off, 128)] for g in range(G)], axis=1)
            scores.append(jnp.where(valid, _dot_nt(kw, qg) + bias, NEG))
            values.append(vt_ref[HD * c:HD * (c + 1), pl.ds(kstart, W)])
    probs = []
    for n, s in enumerate(scores):
        sink = sinks[n % SWA_KV]
        m = jnp.maximum(jnp.max(s, axis=0, keepdims=True), sink)
        p = jnp.exp2(s - m)
        probs.append((p.astype(BF16), jnp.sum(p, axis=0, keepdims=True) + jnp.exp2(sink - m)))
    for r in range(tq // 128):
        outs = []
        for c in range(SWA_KV):
            p, denom = probs[SWA_KV * r + c]
            ot = _dot(values[SWA_KV * r + c], p) / denom
            outs += [ot[:, 128 * g:128 * (g + 1)] for g in range(G)]
        o_ref[128 * r:128 * (r + 1), :] = jnp.concatenate(outs, axis=0).T.astype(BF16)


def _swa_attention(q8, k, vt, sinks, tb, B, T, tq=512):
    nq = T // tq
    return pl.pallas_call(
        functools.partial(_swa_kernel, tq=tq),
        grid=(B, nq),
        in_specs=[pl.BlockSpec(memory_space=pltpu.SMEM),
                  pl.BlockSpec((tq, 1024), lambda b, i: (b * nq + i, 0)),
                  pl.BlockSpec((T, 128), lambda b, i: (b, 0)),
                  pl.BlockSpec((None, 128, T), lambda b, i: (b, 0, 0)),
                  pl.BlockSpec((8, 256, 256), lambda b, i: (0, 0, 0))],
        out_specs=pl.BlockSpec((tq, 512), lambda b, i: (b * nq + i, 0)),
        out_shape=jax.ShapeDtypeStruct((B * T, SWA_HEADS * HD), BF16),
        compiler_params=_cparams(("arbitrary", "arbitrary")),
        name="swa_attention",
    )(sinks * LOG2E, q8, k, vt, tb)


def _outproj_router_kernel(h_ref, oa_ref, ob_ref, wo_ref, g_ref, wr_ref, br_ref,
                           hn_ref, u_ref, ri_ref, rg_ref, cnt_ref, base_ref):
    i = pl.program_id(0)

    @pl.when(i == 0)
    def _():
        base_ref[...] = jnp.zeros_like(base_ref)

    hn = h_ref[...] + _dot(oa_ref[...], wo_ref[0:512, :]) + _dot(ob_ref[...], wo_ref[512:1024, :])
    hn_ref[...] = hn
    u = _rms(hn, g_ref[...])
    u_ref[...] = _pack_halves(u)
    u_hi = u.astype(BF16)
    u_lo = (u - u_hi.astype(F32)).astype(BF16)
    both = _dot_nt(wr_ref[...], u_hi)
    logits = both[:128] + both[128:] + _dot_nt(wr_ref[0:128, :], u_lo) + br_ref[...]
    tm = logits.shape[1]
    sub = lax.broadcasted_iota(jnp.int32, (8, tm), 0).astype(F32)
    big = 1e6
    isg = sub < N_GROUPS
    gl = jnp.where(isg, logits[N_EXP:N_EXP + 8], NEG)
    gmax = jnp.max(gl, axis=0, keepdims=True)
    gsel = jnp.min(jnp.where(gl == gmax, sub, big), axis=0, keepdims=True)
    g_gate = 1.0 / jnp.sum(jnp.where(isg, jnp.exp(gl - gmax), 0.0), axis=0, keepdims=True)
    el = jnp.zeros((EPG, tm), F32)
    for grp in range(N_GROUPS):
        el = jnp.where(gsel == grp, logits[EPG * grp:EPG * (grp + 1)], el)
    m1 = jnp.max(el, axis=0, keepdims=True)
    i1 = jnp.min(jnp.where(el == m1, sub, big), axis=0, keepdims=True)
    el2 = jnp.where(sub == i1, NEG, el)
    m2 = jnp.max(el2, axis=0, keepdims=True)
    i2 = jnp.min(jnp.where(el2 == m2, sub, big), axis=0, keepdims=True)
    e1 = gsel * EPG + i1
    e2 = gsel * EPG + i2
    r = jnp.exp(m2 - m1)
    gate1 = g_gate / (1.0 + r)
    gate2 = g_gate * r / (1.0 + r)
    rows = lax.broadcasted_iota(jnp.int32, (128, tm), 0).astype(F32)
    oh1 = (rows == e1).astype(F32)
    oh2 = (rows == e2).astype(F32)
    oh = oh1 + oh2
    t_row = lax.broadcasted_iota(jnp.int32, (tm, tm), 0)
    t_col = lax.broadcasted_iota(jnp.int32, (tm, tm), 1)
    earlier = jnp.where(t_row < t_col, 1.0, 0.0).astype(BF16)
    prefix = _dot(oh.astype(BF16), earlier) + base_ref[:, 0:1]
    rank1 = jnp.sum(oh1 * prefix, axis=0, keepdims=True)
    rank2 = jnp.sum(oh2 * prefix, axis=0, keepdims=True)
    base_ref[...] = base_ref[...] + jnp.sum(oh, axis=1, keepdims=True)
    cnt_ref[...] = base_ref[...]
    ri_ref[...] = jnp.where(sub == 0, e1, jnp.where(sub == 1, e2, jnp.where(sub == 2, rank1,
                            jnp.where(sub == 3, rank2, 0.0))))
    rg_ref[...] = jnp.where(rows == 0, gate1, jnp.where(rows == 1, gate2, 0.0)).T


def _outproj_router(h, oa, ob, wo, g, wr, br):
    N = h.shape[0]
    full = lambda a: pl.BlockSpec(a.shape, lambda i: (0,) * a.ndim)
    row = lambda w: pl.BlockSpec((TM, w), lambda i: (i, 0))
    return pl.pallas_call(
        _outproj_router_kernel,
        grid=(N // TM,),
        in_specs=[row(D), row(512), row(512), full(wo), full(g), full(wr), full(br)],
        out_specs=[row(D), row(HALF), pl.BlockSpec((8, TM), lambda i: (0, i)), row(128),
                   pl.BlockSpec((128, 128), lambda i: (0, 0))],
        out_shape=[jax.ShapeDtypeStruct((N, D), F32), jax.ShapeDtypeStruct((N, HALF), jnp.int32),
                   jax.ShapeDtypeStruct((8, N), F32), jax.ShapeDtypeStruct((N, 128), F32),
                   jax.ShapeDtypeStruct((128, 128), F32)],
        scratch_shapes=[pltpu.VMEM((128, 128), F32)],
        compiler_params=_cparams(("arbitrary",)),
        name="outproj_router",
    )(h, oa, ob, wo, g, wr, br)


def _moe_kernel(te_ref, tv_ref, x_ref, wg_ref, wu_ref, wd_ref, o_ref, wgu_s, wd_s):
    i = pl.program_id(0)
    valid = tv_ref[i]

    @pl.when((i == 0) | (te_ref[i] != te_ref[jnp.maximum(i - 1, 0)]))
    def _():
        wgu_s[:, :D_EXP] = wg_ref[...].astype(BF16)
        wgu_s[:, D_EXP:] = wu_ref[...].astype(BF16)
        wd_s[...] = wd_ref[...].astype(BF16)

    @pl.when(valid > 0)
    def _():
        rows = lax.broadcasted_iota(jnp.int32, x_ref.shape, 0)
        lo, hi = _unpack_halves(jnp.where(rows < valid, x_ref[...], 0))
        hgu = _dot(lo.astype(BF16), wgu_s[:HALF, :]) + _dot(hi.astype(BF16), wgu_s[HALF:, :])
        hg, hu = hgu[:, :D_EXP], hgu[:, D_EXP:]
        a = hg * (1.0 / (1.0 + jnp.exp(-hg))) * hu
        o_ref[...] = _pack_halves(_dot(a.astype(BF16), wd_s[...]))

    @pl.when(valid <= 0)
    def _():
        o_ref[...] = jnp.zeros_like(o_ref)


def _moe_experts(xs, tile_expert, tile_valid, w_gate, w_up, w_down, layer):
    P = xs.shape[0]
    grid_spec = pltpu.PrefetchScalarGridSpec(
        num_scalar_prefetch=2,
        grid=(P // TM_E,),
        in_specs=[pl.BlockSpec((TM_E, HALF), lambda i, te, tv: (i, 0)),
                  pl.BlockSpec((None, None, D, D_EXP), lambda i, te, tv: (layer, te[i], 0, 0)),
                  pl.BlockSpec((None, None, D, D_EXP), lambda i, te, tv: (layer, te[i], 0, 0)),
                  pl.BlockSpec((None, None, D_EXP, D), lambda i, te, tv: (layer, te[i], 0, 0))],
        out_specs=pl.BlockSpec((TM_E, HALF), lambda i, te, tv: (i, 0)),
        scratch_shapes=[pltpu.VMEM((D, 2 * D_EXP), BF16), pltpu.VMEM((D_EXP, D), BF16)],
    )
    return pl.pallas_call(
        _moe_kernel,
        grid_spec=grid_spec,
        out_shape=jax.ShapeDtypeStruct((P, HALF), jnp.int32),
        compiler_params=_cparams(("arbitrary",)),
        name="moe_experts",
    )(tile_expert, tile_valid, xs, w_gate, w_up, w_down)


def _sc_mesh():
    return plsc.VectorSubcoreMesh(core_axis_name="c", subcore_axis_name="s",
                                  num_cores=SC_CORES, num_subcores=SC_SUBCORES)


def _sc_scatter_rows(src, d1, d2, P):
    N, W = src.shape
    per_w = N // (SC_CORES * SC_SUBCORES)

    @functools.partial(pl.kernel, mesh=_sc_mesh(), out_type=jax.ShapeDtypeStruct((P, W), src.dtype),
                       scratch_types=[pltpu.VMEM((SC_CHUNK,), jnp.int32), pltpu.VMEM((SC_CHUNK, W), src.dtype)],
                       name="sc_scatter_rows")
    def k(src_hbm, d1_hbm, d2_hbm, out_hbm, idx_v, rows_v):
        wid = lax.axis_index("s") * SC_CORES + lax.axis_index("c")

        @pl.loop(0, per_w // SC_CHUNK)
        def _(c):
            off = wid * per_w + c * SC_CHUNK
            pltpu.sync_copy(src_hbm.at[pl.ds(off, SC_CHUNK)], rows_v)
            pltpu.sync_copy(d1_hbm.at[pl.ds(off, SC_CHUNK)], idx_v)
            pltpu.sync_copy(rows_v, out_hbm.at[idx_v])
            pltpu.sync_copy(d2_hbm.at[pl.ds(off, SC_CHUNK)], idx_v)
            pltpu.sync_copy(rows_v, out_hbm.at[idx_v])

    return k(src, d1, d2)


def _sc_gather_rows(table, idx):
    B, W = idx.shape[0], table.shape[1]
    per_w = B // (SC_CORES * SC_SUBCORES)

    @functools.partial(pl.kernel, mesh=_sc_mesh(), out_type=jax.ShapeDtypeStruct((B, W), table.dtype),
                       scratch_types=[pltpu.VMEM((SC_CHUNK,), jnp.int32), pltpu.VMEM((SC_CHUNK, W), table.dtype)],
                       name="sc_gather_rows")
    def k(table_hbm, idx_hbm, out_hbm, idx_v, rows_v):
        wid = lax.axis_index("s") * SC_CORES + lax.axis_index("c")

        @pl.loop(0, per_w // SC_CHUNK)
        def _(c):
            off = wid * per_w + c * SC_CHUNK
            pltpu.sync_copy(idx_hbm.at[pl.ds(off, SC_CHUNK)], idx_v)
            pltpu.sync_copy(table_hbm.at[idx_v], rows_v)
            pltpu.sync_copy(rows_v, out_hbm.at[pl.ds(off, SC_CHUNK)])

    return k(table, idx)


def _dest_kernel(offs_ref, ri_ref, d_ref):
    ri = ri_ref[...]
    base = jnp.zeros(ri.shape, F32)
    for e in range(N_EXP):
        base = jnp.where(ri == e, offs_ref[e].astype(F32), base)
    d_ref[...] = (base + pltpu.roll(ri, 6, 0)).astype(jnp.int32)


def _dispatch_plan(ri, cnt, N):
    counts = cnt[:N_EXP, 0].astype(jnp.int32)
    padded = ((counts + TM_E - 1) // TM_E) * TM_E
    ends = jnp.cumsum(padded)
    offs = ends - padded
    take = lambda t, i: t.at[i].get(mode="promise_in_bounds")
    dest = pl.pallas_call(
        _dest_kernel,
        in_specs=[pl.BlockSpec(memory_space=pltpu.SMEM), pl.BlockSpec((8, N), lambda: (0, 0))],
        out_specs=pl.BlockSpec((8, N), lambda: (0, 0)),
        out_shape=jax.ShapeDtypeStruct((8, N), jnp.int32),
        name="dispatch_rows",
    )(offs, ri)
    d1, d2 = dest[0], dest[1]
    P = 2 * N + N_EXP * TM_E
    tile_start = jnp.arange(P // TM_E, dtype=jnp.int32) * TM_E
    tile_expert = jnp.minimum(jnp.sum((ends[None, :] <= tile_start[:, None]).astype(jnp.int32), axis=1), N_EXP - 1)
    tile_valid = jnp.clip(take(offs + counts, tile_expert) - tile_start, 0, TM_E)
    return d1, d2, P, tile_expert, tile_valid


def _combine(h_ref, y1_ref, y2_ref, rg_ref):
    rg = rg_ref[...]
    lo1, hi1 = _unpack_halves(y1_ref[...])
    lo2, hi2 = _unpack_halves(y2_ref[...])
    g1, g2 = rg[:, 0:1], rg[:, 1:2]
    return h_ref[...] + jnp.concatenate([g1 * lo1 + g2 * lo2, g1 * hi1 + g2 * hi2], axis=1)


ODD_WIDTHS = (1024, 1024, 512, 512)


def _combine_proj_odd_kernel(h_ref, y1_ref, y2_ref, rg_ref, g_ref, win_ref, wvt_ref, ind_ref, hn_ref, vt_ref,
                             vdt_ref, *out_refs):
    hn = _combine(h_ref, y1_ref, y2_ref, rg_ref)
    hn_ref[...] = hn
    u = _rms(hn, g_ref[...]).astype(BF16)
    vt_ref[...] = _values_t(_dot_nt(wvt_ref[0:512, :], u)).astype(BF16)
    vdt_ref[...] = _dot_nt(wvt_ref[512:1024, :], u).astype(BF16)
    qc_ref, kc_ref, qd_ref, kd_ref = out_refs
    qc = _spread_heads(_dot(u, win_ref[:, 0:512]), lambda h: 0)
    kc = _spread_heads(_dot(u, win_ref[:, 512:1024]), lambda h: 0)
    for hh in range(8):
        qc_ref[:, 128 * hh:128 * (hh + 1)] = qc[hh].astype(BF16)
        kc_ref[:, 128 * hh:128 * (hh + 1)] = (kc[hh] + ind_ref[...]).astype(BF16)
    qd_ref[...] = _dot(u, win_ref[:, 1024:1536]).astype(BF16)
    kd_ref[...] = _dot(u, win_ref[:, 1536:2048]).astype(BF16)


def _odd_weights(w_in):
    s = HD ** -0.5 * LOG2E
    scale = np.asarray([s] * 512 + [1.0] * 512 + [s] * 512 + [1.0] * 512, np.float32)
    win = jnp.concatenate([w_in[:, 0:1024], w_in[:, 1536:2560]], axis=1) * jnp.asarray(scale)
    wvt = jnp.concatenate([w_in[:, 1024:1536], w_in[:, 2560:3072]], axis=1).T
    return win.astype(BF16), wvt.astype(BF16)


def _block_indicator(T):
    assert T // MOBA_L <= 64
    ind = np.zeros((T, 128), np.float32)
    ind[np.arange(T), 64 + np.arange(T) // MOBA_L] = 1.0
    return jnp.asarray(ind)


def _y_specs(N):
    return [pl.BlockSpec((TM, HALF), lambda i: (i, 0)), pl.BlockSpec((TM, HALF), lambda i: (i + N // TM, 0))]


def _combine_proj_odd(h, y12, rg, g, win, wvt, ind, T):
    N = h.shape[0]
    nt = T // TM
    full = lambda a: pl.BlockSpec(a.shape, lambda i: (0,) * a.ndim)
    row = lambda w: pl.BlockSpec((TM, w), lambda i: (i, 0))
    return pl.pallas_call(
        _combine_proj_odd_kernel,
        grid=(N // TM,),
        in_specs=[row(D)] + _y_specs(N) + [row(128), full(g), full(win), full(wvt),
                  pl.BlockSpec((TM, 128), lambda i: (i % nt, 0))],
        out_specs=[row(D), pl.BlockSpec((None, 8 * VT_ROWS, TM), lambda i: (i // nt, 0, i % nt)),
                   pl.BlockSpec((None, 512, TM), lambda i: (i // nt, 0, i % nt))]
        + [row(w) for w in ODD_WIDTHS],
        out_shape=[jax.ShapeDtypeStruct((N, D), F32), jax.ShapeDtypeStruct((N // T, 8 * VT_ROWS, T), BF16),
                   jax.ShapeDtypeStruct((N // T, 512, T), BF16)]
        + [jax.ShapeDtypeStruct((N, w), BF16) for w in ODD_WIDTHS],
        compiler_params=_cparams(("arbitrary",)),
        name="combine_proj_odd",
    )(h, y12, y12, rg, g, win, wvt, ind)


def _combine_final_kernel(h_ref, y1_ref, y2_ref, rg_ref, g_ref, o_ref):
    o_ref[...] = _rms(_combine(h_ref, y1_ref, y2_ref, rg_ref), g_ref[...])


def _combine_final(h, y12, rg, g):
    N = h.shape[0]
    row = lambda w: pl.BlockSpec((TM, w), lambda i: (i, 0))
    return pl.pallas_call(
        _combine_final_kernel,
        grid=(N // TM,),
        in_specs=[row(D)] + _y_specs(N) + [row(128), pl.BlockSpec((1, D), lambda i: (0, 0))],
        out_specs=row(D),
        out_shape=jax.ShapeDtypeStruct((N, D), F32),
        compiler_params=_cparams(("arbitrary",)),
        name="combine_final",
    )(h, y12, y12, rg, g)


def _moba_kernel(q_ref, k_ref, vt_ref, tb_ref, o_ref, kmean_ref, qa_ref, s_ref, smax_ref, m_ref, acc_ref, *, nkb,
                 nq):
    ia = pl.program_id(2)
    L, tq = MOBA_L, MOBA_TQ

    @pl.when(ia == 0)
    def _():
        kmean_ref[...] = jnp.zeros_like(kmean_ref)
        for n in range(nkb):
            kmean_ref[64 + n:65 + n, :] = jnp.sum(k_ref[n * L:(n + 1) * L, :].astype(F32), axis=0,
                                                  keepdims=True) * (1.0 / L)

    nb = -(-nkb // 8) * 8
    lane = lax.broadcasted_iota(jnp.int32, (tq, 128), 1)
    blk = lax.broadcasted_iota(jnp.int32, (nb, tq), 0)
    half = jnp.where(lax.broadcasted_iota(jnp.int32, (nb, tq), 1) >= L, 1, 0)
    blk_f = blk.astype(F32)
    for t, tile in enumerate((ia, nq - 1 - ia)):
        own = 2 * tile + half
        for j in range(2):
            qj = q_ref[pl.ds(pl.multiple_of(tile * tq, tq), tq), 128 * j:128 * (j + 1)]
            gsc = lax.dot_general(kmean_ref[64:64 + nb, 128 * j:128 * (j + 1)], qj.astype(F32),
                                  (((1,), (1,)), ((), ())), preferred_element_type=F32,
                                  precision=lax.Precision.HIGHEST)
            g = jnp.where(blk < own, gsc, NEG)
            allowed = jnp.where(blk == own, 1.0, 0.0)
            for _ in range(MOBA_TOPK):
                mx = jnp.max(g, axis=0, keepdims=True)
                idx = jnp.min(jnp.where(g == mx, blk_f, 1e6), axis=0, keepdims=True)
                pick = blk_f == idx
                allowed = jnp.maximum(allowed, jnp.where(pick, jnp.where(mx > 0.5 * NEG, 1.0, 0.0), 0.0))
                g = jnp.where(pick, 2.0 * NEG, g)
            mask = jnp.concatenate([jnp.zeros((64, tq), F32), jnp.where(allowed > 0.5, 0.0, NEG),
                                    jnp.zeros((64 - nb, tq), F32)], axis=0)
            qa_ref[t, j] = jnp.where(lane < HD, qj, mask.T.astype(BF16))

    def bias_of(start, q0, j):
        parts = []
        for c in range(tq // L):
            off = jnp.minimum(q0 - start - c * L, BIAS_CONST_FROM) + BIAS_PAD
            parts.append(tb_ref[j, :, pl.ds(pl.multiple_of(off, L), tq)])
        return jnp.concatenate(parts, axis=0)

    _causal_attention_pair_t(ia, nq, lambda t, q0, j: qa_ref[t, j], k_ref, vt_ref, o_ref, s_ref, smax_ref, m_ref,
                             acc_ref, tq=tq, bias_of=bias_of)


def _moba_attention(q, k, vt, tb, B, T):
    tq = MOBA_TQ
    nq = T // tq
    return pl.pallas_call(
        functools.partial(_moba_kernel, nkb=T // MOBA_L, nq=nq),
        grid=(B, 4, nq // 2),
        in_specs=[pl.BlockSpec((T, 256), lambda b, p, i: (b, p)),
                  pl.BlockSpec((T, 256), lambda b, p, i: (b, p)),
                  pl.BlockSpec((None, 2 * VT_ROWS, T), lambda b, p, i: (b, p, 0)),
                  pl.BlockSpec((2, MOBA_L, BIAS_ROWS), lambda b, p, i: (p, 0, 0))],
        out_specs=pl.BlockSpec((T, 128), lambda b, p, i: (b, p)),
        out_shape=jax.ShapeDtypeStruct((B * T, 512), BF16),
        scratch_shapes=[pltpu.VMEM((128, 256), F32), pltpu.VMEM((2, 2, tq, 128), BF16)]
        + _attention_t_scratch(tq),
        compiler_params=_cparams(("arbitrary", "arbitrary", "arbitrary")),
        name="moba_attention",
    )(q, k, vt, tb)


def _sb_kernel(q_ref, k_ref, vt_ref, o_ref, *, tq, pairs):
    i = pl.program_id(2)
    lane = lax.broadcasted_iota(jnp.int32, (tq, 128), 1)
    qs = []
    for p in range(pairs):
        q_pair = q_ref[:, 128 * p:128 * (p + 1)]
        qs += [jnp.where(lane < HD, q_pair, 0).astype(BF16), jnp.where(lane >= HD, q_pair, 0).astype(BF16)]
    key = lax.broadcasted_iota(jnp.int32, (tq, tq), 0)
    qry = lax.broadcasted_iota(jnp.int32, (tq, tq), 1)
    past = key < qry
    suffix = jnp.where(qry >= key, -1.0, 0.0).astype(BF16)
    heads = range(2 * pairs)

    def step(kb, carry, boundary):
        start = pl.multiple_of(kb * tq, tq)
        zs = [_dot_nt(k_ref[pl.ds(start, tq), 128 * (h // 2):128 * (h // 2 + 1)], qs[h]) for h in heads]
        parts = []
        for z in zs:
            neg_abs = pltpu.bitcast(pltpu.bitcast(z, jnp.uint32) | jnp.uint32(0x80000000), F32)
            soft = jnp.maximum(z, 0.0) + jnp.log2(1.0 + jnp.exp2(neg_abs))
            if boundary:
                soft = jnp.where(past, soft, 0.0)
            parts.append(soft.astype(BF16))
        sums = [_dot(suffix, part) for part in parts]
        ws = []
        for h in heads:
            w = jnp.exp2(zs[h] + (carry[h][0] + sums[h]))
            ws.append((jnp.where(past, w, 0.0) if boundary else w).astype(BF16))
        return tuple((carry[h][0] + sums[h][0:1], carry[h][1]
                      + _dot(vt_ref[128 * (h // 2):128 * (h // 2 + 1), pl.ds(start, tq)], ws[h])) for h in heads)

    def live(carry):
        top = carry[0][0]
        for c, _ in carry[1:]:
            top = jnp.maximum(top, c)
        return (jnp.max(top) > SB_DONE).astype(jnp.int32)

    def body(state):
        t, _, carry = state
        carry = step(i - 1 - t, carry, False)
        return t + 1, live(carry), carry

    init = tuple((jnp.zeros((1, tq), F32), jnp.zeros((128, tq), F32)) for _ in range(2 * pairs))
    carry = step(i, init, True)
    _, _, carry = lax.while_loop(lambda st: (st[0] < i) & (st[1] > 0), body, (jnp.int32(0), live(carry), carry))
    rows = lax.broadcasted_iota(jnp.int32, (128, tq), 0)
    for p in range(pairs):
        ot = jnp.where(rows < HD, carry[2 * p][1], carry[2 * p + 1][1])
        o_ref[:, 128 * p:128 * (p + 1)] = ot.T.astype(BF16)


def _sb_attention(q, k, vt, B, T, tq=256, pairs=4):
    nq = T // tq
    w = 128 * pairs
    return pl.pallas_call(
        functools.partial(_sb_kernel, tq=tq, pairs=pairs),
        grid=(B, 4 // pairs, nq),
        in_specs=[pl.BlockSpec((tq, w), lambda b, p, i: (b * nq + i, p)),
                  pl.BlockSpec((T, w), lambda b, p, i: (b, p)),
                  pl.BlockSpec((None, w, T), lambda b, p, i: (b, p, 0))],
        out_specs=pl.BlockSpec((tq, w), lambda b, p, i: (b * nq + i, p)),
        out_shape=jax.ShapeDtypeStruct((B * T, 512), BF16),
        compiler_params=_cparams(("arbitrary", "arbitrary", "arbitrary")),
        name="sb_attention",
    )(q, k, vt)


def _router_weights(w_group, b_group, w_router, b_router):
    pad = 128 - N_EXP - N_GROUPS
    wr = jnp.concatenate([w_router, w_group, jnp.zeros((D, pad), F32)], axis=1).T
    hi = wr.astype(BF16)
    lo = (wr - hi.astype(F32)).astype(BF16)
    br = jnp.concatenate([b_router, b_group, jnp.zeros((pad,), F32)])[:, None]
    return jnp.concatenate([hi, lo], axis=0), br


def _ffn(h, oa, ob, wo, g, w_group, b_group, w_router, b_router, w_gate, w_up, w_down, layer):
    wr, br = _router_weights(w_group, b_group, w_router, b_router)
    hn, u, ri, rg, cnt = _outproj_router(h, oa, ob, wo.astype(BF16), g[None, :], wr, br)
    d1, d2, P, tile_expert, tile_valid = _dispatch_plan(ri, cnt, h.shape[0])
    xs = _sc_scatter_rows(u, d1, d2, P)
    ys = _moe_experts(xs, tile_expert, tile_valid, w_gate, w_up, w_down, layer)
    return hn, _sc_gather_rows(ys, jnp.concatenate([d1, d2])), rg


def kernel(x, norm_mix, norm_ffn, norm_final, rel_bias_table, w_in_even, g_mla_q, g_mla_kv, w_mla_qb, w_mla_kvb, swa_sinks, w_out_even, w_in_odd, w_out_odd, moe_w_group, moe_b_group, moe_w_router, moe_b_router, moe_w_gate, moe_w_up, moe_w_down):
    B, T, _ = x.shape
    h = x.reshape(B * T, D)
    tb_swa, tb_moba = _bias_tables(rel_bias_table)

    win, wqa, wqb, wk, wv = _even_weights(w_in_even[0], w_mla_qb[0], w_mla_kvb[0])
    q, k, v, qs8, ks, vst = _proj_even(h, norm_mix[0][None, :], win, g_mla_q[0][None, :], g_mla_kv[0][None, :],
                                       wqa, wqb, wk, wv, _rope_tables(T), T)
    oa = _mla_attention(q, k, v, B, T)
    ob = _swa_attention(qs8, ks, vst, swa_sinks[0], tb_swa, B, T)
    h, y12, rg = _ffn(h, oa, ob, w_out_even[0], norm_ffn[0], moe_w_group[0], moe_b_group[0], moe_w_router[0],
                      moe_b_router[0], moe_w_gate, moe_w_up, moe_w_down, 0)

    h, vct, vdt, qc, kc, qd, kd = _combine_proj_odd(h, y12, rg, norm_mix[1][None, :], *_odd_weights(w_in_odd[0]),
                                                   _block_indicator(T), T)
    oc = _moba_attention(qc, kc, vct, tb_moba, B, T)
    od = _sb_attention(qd, kd, vdt, B, T)
    h, y12, rg = _ffn(h, oc, od, w_out_odd[0], norm_ffn[1], moe_w_group[1], moe_b_group[1], moe_w_router[1],
                      moe_b_router[1], moe_w_gate, moe_w_up, moe_w_down, 1)
    out = _combine_final(h, y12, rg, norm_final[None, :])
    return out.reshape(B, T, D)
```

```python
import functools
import math

import numpy as np
import jax
import jax.numpy as jnp
from jax import lax
from jax.experimental import pallas as pl
from jax.experimental.pallas import tpu as pltpu
from jax.experimental.pallas import tpu_sc as plsc

F32 = jnp.float32
BF16 = jnp.bfloat16
NEG = -1e30
EPS = 1e-6

D = 1024
HD = 64
MLA_HEADS, MLA_QR, MLA_KVR, MLA_NOPE, MLA_ROPE, MLA_V = 8, 256, 128, 64, 32, 64
ROPE_THETA = 10000.0
SWA_HEADS, SWA_KV, SWA_WIN = 8, 2, 128
MOBA_L, MOBA_TOPK = 256, 3
REL_BUCKETS, REL_MAX = 32, 2048
N_GROUPS, EPG, N_EXP, D_EXP = 4, 8, 32, 256
BIAS_CONST_FROM = 1792
MOBA_TQ = 2 * MOBA_L
BIAS_PAD = MOBA_L
BIAS_ROWS = BIAS_PAD + BIAS_CONST_FROM + MOBA_TQ
SB_DONE = -160.0
TM = 1024
TM_E = 512
VMEM_LIMIT = 56 * 1024 * 1024
SC_CORES, SC_SUBCORES = 2, 16
SC_CHUNK = 128
HALF = D // 2
VT_ROWS = HD + 16
LOG2E = math.log2(math.e)


def _pack_halves(x):
    lo = pltpu.bitcast(x[:, :HALF].astype(BF16).astype(F32), jnp.uint32) >> 16
    hi = pltpu.bitcast(x[:, HALF:].astype(BF16).astype(F32), jnp.uint32) & jnp.uint32(0xFFFF0000)
    return pltpu.bitcast(lo | hi, jnp.int32)


def _unpack_halves(w):
    w = pltpu.bitcast(w, jnp.uint32)
    return pltpu.bitcast(w << 16, F32), pltpu.bitcast(w & jnp.uint32(0xFFFF0000), F32)


def _cparams(sem):
    return pltpu.CompilerParams(dimension_semantics=sem, vmem_limit_bytes=VMEM_LIMIT)


def _dot(a, b):
    return jnp.dot(a, b, preferred_element_type=F32)


def _dot_nt(a, b):
    return lax.dot_general(a, b, (((1,), (1,)), ((), ())), preferred_element_type=F32)


def _rms(x, g):
    return x * lax.rsqrt(jnp.mean(x * x, axis=-1, keepdims=True) + EPS) * g


def _values_t(vt):
    ones = jnp.ones((VT_ROWS - HD, vt.shape[1]), vt.dtype)
    return jnp.concatenate([blk for h in range(8) for blk in (vt[HD * h:HD * (h + 1)], ones)], axis=0)


def _spread_heads(x, lane_of):
    lane = lax.broadcasted_iota(jnp.int32, (x.shape[0], 128), 1)
    out = []
    for h in range(8):
        pair = x[:, 128 * (h // 2):128 * (h // 2 + 1)]
        if HD * (h % 2) != lane_of(h):
            pair = pltpu.roll(pair, HD, 1)
        out.append(jnp.where((lane >= lane_of(h)) & (lane < lane_of(h) + HD), pair, 0.0))
    return out


def _t5_bucket_np(dist):
    n = np.maximum(dist, 0)
    max_exact = REL_BUCKETS // 2
    nf = np.maximum(n, 1).astype(np.float32)
    large = max_exact + (np.log(nf / np.float32(max_exact)) / np.float32(math.log(REL_MAX / max_exact))
                         * np.float32(REL_BUCKETS - max_exact)).astype(np.int32)
    large = np.minimum(large, REL_BUCKETS - 1)
    return np.where(n < max_exact, n, large).astype(np.int32)


def _bias_kernel(table_ref, bucket_ref, out_ref, *, scale, ranges):
    b = bucket_ref[...]
    for m, (lo, hi) in enumerate(ranges):
        @pl.when(pl.program_id(0) == m)
        def _():
            for h in range(8):
                acc = jnp.full(b.shape, table_ref[lo, h] * scale, F32)
                for k in range(lo + 1, hi + 1):
                    acc = jnp.where(b == k, table_ref[k, h] * scale, acc)
                out_ref[h] = acc


def _bias_lookup(rel_table, dist, scale, name):
    R, C = dist.shape
    buckets = _t5_bucket_np(dist)
    ranges = tuple((int(buckets[:, c:c + 256].min()), int(buckets[:, c:c + 256].max())) for c in range(0, C, 256))
    return pl.pallas_call(
        functools.partial(_bias_kernel, scale=scale, ranges=ranges),
        grid=(C // 256,),
        in_specs=[pl.BlockSpec(memory_space=pltpu.SMEM),
                  pl.BlockSpec((R, 256), lambda m: (0, m))],
        out_specs=pl.BlockSpec((8, R, 256), lambda m: (0, 0, m)),
        out_shape=jax.ShapeDtypeStruct((8, R, C), F32),
        compiler_params=_cparams(("arbitrary",)),
        name=name,
    )(rel_table, jnp.asarray(buckets))


def _bias_tables(rel_table):
    swa = _bias_lookup(rel_table, np.arange(256)[None, :] - np.arange(256)[:, None], LOG2E, "bias_swa")
    moba = _bias_lookup(rel_table, np.arange(BIAS_ROWS)[None, :] - BIAS_PAD - np.arange(MOBA_L)[:, None], LOG2E,
                        "bias_moba")
    return swa, moba


def _proj_even_kernel(h_ref, g_ref, win_ref, gq_ref, gkv_ref, wqa_ref, wqb_ref, wk_ref, wv_ref,
                      cq_ref, sq_ref, ck_ref, sk_ref,
                      q_ref, k_ref, v_ref, qs_ref, ks_ref, vst_ref):
    u = _rms(h_ref[...], g_ref[...]).astype(BF16)
    nq = _rms(_dot(u, win_ref[:, 0:256]), gq_ref[...]).astype(BF16)
    qa = _dot(nq, wqa_ref[...])
    qb = _dot(nq, wqb_ref[...])
    cq, sq = cq_ref[...], sq_ref[...]
    for hh in range(MLA_HEADS):
        sl = slice(128 * hh, 128 * (hh + 1))
        q_ref[:, sl] = (qa[:, sl] * cq + qb[:, sl] * sq).astype(BF16)
    nkv = _rms(_dot(u, win_ref[:, 256:384]), gkv_ref[...]).astype(BF16)
    kn = _dot(nkv, wk_ref[...])
    v_ref[...] = _values_t(_dot_nt(wv_ref[...], nkv)).astype(BF16)
    rest = _dot(u, win_ref[:, 896:1408])
    kr = rest[:, 128:256] * ck_ref[...] + rest[:, 256:384] * sk_ref[...]
    for hh in range(MLA_HEADS):
        sl = slice(128 * hh, 128 * (hh + 1))
        k_ref[:, sl] = (kn[:, sl] + kr).astype(BF16)
    qs = _spread_heads(_dot(u, win_ref[:, 384:896]), lambda hq: HD * (hq // (SWA_HEADS // SWA_KV)))
    for hq in range(SWA_HEADS):
        qs_ref[:, 128 * hq:128 * (hq + 1)] = qs[hq].astype(BF16)
    ks_ref[...] = rest[:, 0:128].astype(BF16)
    vst_ref[...] = rest[:, 384:512].T.astype(BF16)


def _cols(w, idx, scale=None):
    idx = np.asarray(idx)
    cuts = [0] + [n for n in range(1, len(idx)) if (idx[n] < 0) != (idx[n - 1] < 0)
                  or (idx[n] >= 0 and idx[n] != idx[n - 1] + 1)] + [len(idx)]
    parts = [jnp.zeros((w.shape[0], b - a), w.dtype) if idx[a] < 0 else w[:, idx[a]:idx[a] + b - a]
             for a, b in zip(cuts[:-1], cuts[1:])]
    out = jnp.concatenate(parts, axis=1)
    if scale is not None:
        out = out * scale
    return out


def _even_weights(w_in, w_qb, w_kvb):
    o_cq, o_ckv, o_kr = 0, MLA_QR, MLA_QR + MLA_KVR
    o_qs = o_kr + MLA_ROPE
    o_ks = o_qs + SWA_HEADS * HD
    o_vs = o_ks + SWA_KV * HD
    idx = list(range(o_cq, o_cq + MLA_QR)) + list(range(o_ckv, o_ckv + MLA_KVR))
    scale = [1.0] * len(idx)
    idx += list(range(o_qs, o_qs + SWA_HEADS * HD))
    scale += [HD ** -0.5 * LOG2E] * (SWA_HEADS * HD)
    idx += list(range(o_ks, o_ks + 128))
    scale += [1.0] * 128
    half = MLA_ROPE // 2
    idx += [-1] * 64 + list(range(o_kr, o_kr + MLA_ROPE)) + [-1] * 32
    idx += [-1] * 64 + list(range(o_kr + half, o_kr + MLA_ROPE)) + list(range(o_kr, o_kr + half)) + [-1] * 32
    idx += list(range(o_vs, o_vs + 128))
    scale += [1.0] * 384
    win = _cols(w_in, idx, jnp.asarray(np.asarray(scale, np.float32))).astype(BF16)
    qw = MLA_NOPE + MLA_ROPE
    ia, ib, ik, iv = [], [], [], []
    for h in range(MLA_HEADS):
        ia += list(range(qw * h, qw * h + qw)) + [-1] * 32
        ib += [-1] * 64 + list(range(qw * h + MLA_NOPE + half, qw * h + qw)) \
            + list(range(qw * h + MLA_NOPE, qw * h + MLA_NOPE + half)) + [-1] * 32
        ik += list(range(128 * h, 128 * h + MLA_NOPE)) + [-1] * 64
        iv += list(range(128 * h + MLA_NOPE, 128 * h + 128))
    return (win, _cols(w_qb, ia).astype(BF16), _cols(w_qb, ib).astype(BF16),
            _cols(w_kvb, ik).astype(BF16), _cols(w_kvb, iv).T.astype(BF16))


def _rope_tables(T):
    f32 = np.float32
    freqs = f32(ROPE_THETA) ** (-np.arange(0, MLA_ROPE, 2, dtype=f32) / f32(MLA_ROPE))
    ang = np.arange(T, dtype=f32)[:, None] * freqs[None, :]
    cos, sin = np.cos(ang), np.sin(ang)
    z = lambda n: np.zeros((T, n), f32)
    scale = f32((MLA_NOPE + MLA_ROPE) ** -0.5 * LOG2E)
    ck = np.concatenate([z(64), cos, cos, z(32)], axis=1)
    sk = np.concatenate([z(64), -sin, sin, z(32)], axis=1)
    cq = np.concatenate([np.ones((T, 64), f32), cos, cos, z(32)], axis=1) * scale
    sq = sk * scale
    return tuple(jnp.asarray(t, F32) for t in (cq, sq, ck, sk))


def _proj_even(h, g, win, gq, gkv, wqa, wqb, wk, wv, tables, T):
    N = h.shape[0]
    nt = T // TM
    full = lambda a: pl.BlockSpec(a.shape, lambda i: (0,) * a.ndim)
    tab = pl.BlockSpec((TM, 128), lambda i: (i % nt, 0))
    row = lambda w: pl.BlockSpec((TM, w), lambda i: (i, 0))
    outs = (1024, 1024, -8 * VT_ROWS, 1024, 128, -128)
    spec = lambda w: row(w) if w > 0 else pl.BlockSpec((None, -w, TM), lambda i: (i // nt, 0, i % nt))
    shape = lambda w: jax.ShapeDtypeStruct((N, w) if w > 0 else (N // T, -w, T), BF16)
    return pl.pallas_call(
        _proj_even_kernel,
        grid=(N // TM,),
        in_specs=[row(D), full(g), full(win), full(gq), full(gkv), full(wqa), full(wqb), full(wk), full(wv),
                  tab, tab, tab, tab],
        out_specs=[spec(w) for w in outs],
        out_shape=[shape(w) for w in outs],
        compiler_params=_cparams(("arbitrary",)),
        name="proj_even",
    )(h, g, win, gq, gkv, wqa, wqb, wk, wv, *tables)


def _finish_t(accs):
    ot = jnp.concatenate([acc[:HD] / acc[HD:HD + 1] for acc in accs], axis=0)
    return ot.T.astype(BF16)


def _causal_attention_pair_t(ia, nq, q_of, k_ref, vt_ref, o_ref, s_ref, smax_ref, m_ref, acc_ref, *, tq,
                             bias_of=None):
    nh = s_ref.shape[1]
    first_query = (ia * tq, (nq - 1 - ia) * tq)
    n_items = nq + 1
    assert n_items % 2 == 1
    key = lax.broadcasted_iota(jnp.int32, (tq, tq), 0)
    qry = lax.broadcasted_iota(jnp.int32, (tq, tq), 1)

    def item(p):
        if isinstance(p, int) and p < 2:
            return p, first_query[p], first_query[p]
        t = jnp.where(p - 2 >= ia, 1, 0)
        return t, jnp.where(t == 1, first_query[1], first_query[0]), (p - 2 - t * ia) * tq

    def scores_to(slot, p):
        t, q0, start = item(p)
        start = pl.multiple_of(start, tq)
        for j in range(nh):
            st = _dot_nt(k_ref[pl.ds(start, tq), 128 * j:128 * (j + 1)], q_of(t, q0, j))
            if bias_of is not None:
                st = st + bias_of(start, q0, j)
            if isinstance(p, int) and p < 2:
                st = jnp.where(key <= qry, st, NEG)
            s_ref[slot, j] = st
            smax_ref[slot, j] = jnp.max(st, axis=0, keepdims=True)

    def update_from(slot, p):
        t, _, start = item(p)
        start = pl.multiple_of(start, tq)
        m_new = [jnp.maximum(m_ref[t, j], smax_ref[slot, j]) for j in range(nh)]
        pts = [jnp.exp2(s_ref[slot, j] - m_new[j]).astype(BF16) for j in range(nh)]
        for j in range(nh):
            acc_ref[t, j] = (jnp.exp2(m_ref[t, j] - m_new[j]) * acc_ref[t, j]
                             + _dot(vt_ref[VT_ROWS * j:VT_ROWS * (j + 1), pl.ds(start, tq)], pts[j]))
            m_ref[t, j] = m_new[j]

    def pair(k):
        scores_to(1, k + 1)
        update_from(0, k)
        scores_to(0, k + 2)
        update_from(1, k + 1)

    def quad(c, _):
        pair(2 + 4 * c)
        pair(4 + 4 * c)
        return 0

    m_ref[...] = jnp.full(m_ref.shape, NEG, F32)
    acc_ref[...] = jnp.zeros(acc_ref.shape, F32)
    scores_to(0, 0)
    pair(0)
    later_pairs = (n_items - 1) // 2 - 1
    lax.fori_loop(0, later_pairs // 2, quad, 0)
    if later_pairs % 2:
        pair(n_items - 3)
    update_from(0, n_items - 1)
    for t in range(2):
        o_ref[pl.ds(pl.multiple_of(first_query[t], tq), tq), :] = _finish_t([acc_ref[t, j] for j in range(nh)])


def _attention_t_scratch(tq, nh=2):
    return [pltpu.VMEM((2, nh, tq, tq), F32), pltpu.VMEM((2, nh, 1, tq), F32), pltpu.VMEM((2, nh, 1, tq), F32),
            pltpu.VMEM((2, nh, VT_ROWS, tq), F32)]


def _mla_kernel(q_ref, k_ref, vt_ref, o_ref, s_ref, smax_ref, m_ref, acc_ref, *, tq, nq):
    def q_of(t, q0, j):
        return q_ref[pl.ds(pl.multiple_of(q0, tq), tq), 128 * j:128 * (j + 1)]

    _causal_attention_pair_t(pl.program_id(2), nq, q_of, k_ref, vt_ref, o_ref, s_ref, smax_ref, m_ref, acc_ref,
                             tq=tq)


def _mla_attention(q, k, vt, B, T, tq=512):
    nq = T // tq
    return pl.pallas_call(
        functools.partial(_mla_kernel, tq=tq, nq=nq),
        grid=(B, MLA_HEADS // 2, nq // 2),
        in_specs=[pl.BlockSpec((T, 256), lambda b, p, i: (b, p)),
                  pl.BlockSpec((T, 256), lambda b, p, i: (b, p)),
                  pl.BlockSpec((None, 2 * VT_ROWS, T), lambda b, p, i: (b, p, 0))],
        out_specs=pl.BlockSpec((T, 128), lambda b, p, i: (b, p)),
        out_shape=jax.ShapeDtypeStruct((B * T, MLA_HEADS * MLA_V), BF16),
        scratch_shapes=_attention_t_scratch(tq),
        compiler_params=_cparams(("arbitrary", "arbitrary", "arbitrary")),
        name="mla_attention",
    )(q, k, vt)


def _swa_kernel(sink_ref, q_ref, k_ref, vt_ref, tb_ref, o_ref, *, tq):
    i = pl.program_id(1)
    G = SWA_HEADS // SWA_KV
    W = 2 * SWA_WIN
    kj = lax.broadcasted_iota(jnp.int32, (W, G * 128), 0)
    lane = lax.broadcasted_iota(jnp.int32, (W, G * 128), 1)
    qi = lane & 127
    head = lax.broadcasted_iota(jnp.int32, (1, G * 128), 1) >> 7
    sinks = []
    for c in range(SWA_KV):
        sink = jnp.zeros((1, G * 128), F32)
        for g in range(G):
            sink = jnp.where(head == g, sink_ref[G * c + g], sink)
        sinks.append(sink)
    scores, values = [], []
    for r in range(tq // 128):
        qstart = i * tq + 128 * r
        kstart = pl.multiple_of(jnp.maximum(qstart - SWA_WIN, 0), 128)
        off = pl.multiple_of(qstart - kstart, 128)
        dist = off + qi - kj
        valid = (dist >= 0) & (dist < SWA_WIN)
        kw = k_ref[pl.ds(kstart, W), :]
        for c in range(SWA_KV):
            qg = jnp.concatenate([q_ref[128 * r:128 * (r + 1), 128 * (G * c + g):128 * (G * c + g + 1)]
                                  for g in range(G)], axis=0)
            bias = jnp.concatenate([tb_ref[G * c + g, :, pl.ds(off, 128)] for g in range(G)], axis=1)
            scores.append(jnp.where(valid, _dot_nt(kw, qg) + bias, NEG))
            values.append(vt_ref[HD * c:HD * (c + 1), pl.ds(kstart, W)])
    probs = []
    for n, s in enumerate(scores):
        sink = sinks[n % SWA_KV]
        m = jnp.maximum(jnp.max(s, axis=0, keepdims=True), sink)
        p = jnp.exp2(s - m)
        probs.append((p.astype(BF16), jnp.sum(p, axis=0, keepdims=True) + jnp.exp2(sink - m)))
    for r in range(tq // 128):
        outs = []
        for c in range(SWA_KV):
            p, denom = probs[SWA_KV * r + c]
            ot = _dot(values[SWA_KV * r + c], p) / denom
            outs += [ot[:, 128 * g:128 * (g + 1)] for g in range(G)]
        o_ref[128 * r:128 * (r + 1), :] = jnp.concatenate(outs, axis=0).T.astype(BF16)


def _swa_attention(q8, k, vt, sinks, tb, B, T, tq=512):
    nq = T // tq
    return pl.pallas_call(
        functools.partial(_swa_kernel, tq=tq),
        grid=(B, nq),
        in_specs=[pl.BlockSpec(memory_space=pltpu.SMEM),
                  pl.BlockSpec((tq, 1024), lambda b, i: (b * nq + i, 0)),
                  pl.BlockSpec((T, 128), lambda b, i: (b, 0)),
                  pl.BlockSpec((None, 128, T), lambda b, i: (b, 0, 0)),
                  pl.BlockSpec((8, 256, 256), lambda b, i: (0, 0, 0))],
        out_specs=pl.BlockSpec((tq, 512), lambda b, i: (b * nq + i, 0)),
        out_shape=jax.ShapeDtypeStruct((B * T, SWA_HEADS * HD), BF16),
        compiler_params=_cparams(("arbitrary", "arbitrary")),
        name="swa_attention",
    )(sinks * LOG2E, q8, k, vt, tb)


def _outproj_router_kernel(h_ref, oa_ref, ob_ref, wo_ref, g_ref, wr_ref, br_ref,
                           hn_ref, u_ref, ri_ref, rg_ref, cnt_ref, base_ref):
    i = pl.program_id(0)

    @pl.when(i == 0)
    def _():
        base_ref[...] = jnp.zeros_like(base_ref)

    hn = h_ref[...] + _dot(oa_ref[...], wo_ref[0:512, :]) + _dot(ob_ref[...], wo_ref[512:1024, :])
    hn_ref[...] = hn
    u = _rms(hn, g_ref[...])
    u_ref[...] = _pack_halves(u)
    u_hi = u.astype(BF16)
    u_lo = (u - u_hi.astype(F32)).astype(BF16)
    both = _dot_nt(wr_ref[...], u_hi)
    logits = both[:128] + both[128:] + _dot_nt(wr_ref[0:128, :], u_lo) + br_ref[...]
    tm = logits.shape[1]
    sub = lax.broadcasted_iota(jnp.int32, (8, tm), 0).astype(F32)
    big = 1e6
    isg = sub < N_GROUPS
    gl = jnp.where(isg, logits[N_EXP:N_EXP + 8], NEG)
    gmax = jnp.max(gl, axis=0, keepdims=True)
    gsel = jnp.min(jnp.where(gl == gmax, sub, big), axis=0, keepdims=True)
    g_gate = 1.0 / jnp.sum(jnp.where(isg, jnp.exp(gl - gmax), 0.0), axis=0, keepdims=True)
    el = jnp.zeros((EPG, tm), F32)
    for grp in range(N_GROUPS):
        el = jnp.where(gsel == grp, logits[EPG * grp:EPG * (grp + 1)], el)
    m1 = jnp.max(el, axis=0, keepdims=True)
    i1 = jnp.min(jnp.where(el == m1, sub, big), axis=0, keepdims=True)
    el2 = jnp.where(sub == i1, NEG, el)
    m2 = jnp.max(el2, axis=0, keepdims=True)
    i2 = jnp.min(jnp.where(el2 == m2, sub, big), axis=0, keepdims=True)
    e1 = gsel * EPG + i1
    e2 = gsel * EPG + i2
    r = jnp.exp(m2 - m1)
    gate1 = g_gate / (1.0 + r)
    gate2 = g_gate * r / (1.0 + r)
    rows = lax.broadcasted_iota(jnp.int32, (128, tm), 0).astype(F32)
    oh1 = (rows == e1).astype(F32)
    oh2 = (rows == e2).astype(F32)
    oh = oh1 + oh2
    t_row = lax.broadcasted_iota(jnp.int32, (tm, tm), 0)
    t_col = lax.broadcasted_iota(jnp.int32, (tm, tm), 1)
    earlier = jnp.where(t_row < t_col, 1.0, 0.0).astype(BF16)
    prefix = _dot(oh.astype(BF16), earlier) + base_ref[:, 0:1]
    rank1 = jnp.sum(oh1 * prefix, axis=0, keepdims=True)
    rank2 = jnp.sum(oh2 * prefix, axis=0, keepdims=True)
    base_ref[...] = base_ref[...] + jnp.sum(oh, axis=1, keepdims=True)
    cnt_ref[...] = base_ref[...]
    ri_ref[...] = jnp.where(sub == 0, e1, jnp.where(sub == 1, e2, jnp.where(sub == 2, rank1,
                            jnp.where(sub == 3, rank2, 0.0))))
    rg_ref[...] = jnp.where(rows == 0, gate1, jnp.where(rows == 1, gate2, 0.0)).T


def _outproj_router(h, oa, ob, wo, g, wr, br):
    N = h.shape[0]
    full = lambda a: pl.BlockSpec(a.shape, lambda i: (0,) * a.ndim)
    row = lambda w: pl.BlockSpec((TM, w), lambda i: (i, 0))
    return pl.pallas_call(
        _outproj_router_kernel,
        grid=(N // TM,),
        in_specs=[row(D), row(512), row(512), full(wo), full(g), full(wr), full(br)],
        out_specs=[row(D), row(HALF), pl.BlockSpec((8, TM), lambda i: (0, i)), row(128),
                   pl.BlockSpec((128, 128), lambda i: (0, 0))],
        out_shape=[jax.ShapeDtypeStruct((N, D), F32), jax.ShapeDtypeStruct((N, HALF), jnp.int32),
                   jax.ShapeDtypeStruct((8, N), F32), jax.ShapeDtypeStruct((N, 128), F32),
                   jax.ShapeDtypeStruct((128, 128), F32)],
        scratch_shapes=[pltpu.VMEM((128, 128), F32)],
        compiler_params=_cparams(("arbitrary",)),
        name="outproj_router",
    )(h, oa, ob, wo, g, wr, br)


def _moe_kernel(te_ref, tv_ref, x_ref, wg_ref, wu_ref, wd_ref, o_ref, wgu_s, wd_s):
    i = pl.program_id(0)
    valid = tv_ref[i]

    @pl.when((i == 0) | (te_ref[i] != te_ref[jnp.maximum(i - 1, 0)]))
    def _():
        wgu_s[:, :D_EXP] = wg_ref[...].astype(BF16)
        wgu_s[:, D_EXP:] = wu_ref[...].astype(BF16)
        wd_s[...] = wd_ref[...].astype(BF16)

    @pl.when(valid > 0)
    def _():
        rows = lax.broadcasted_iota(jnp.int32, x_ref.shape, 0)
        lo, hi = _unpack_halves(jnp.where(rows < valid, x_ref[...], 0))
        hgu = _dot(lo.astype(BF16), wgu_s[:HALF, :]) + _dot(hi.astype(BF16), wgu_s[HALF:, :])
        hg, hu = hgu[:, :D_EXP], hgu[:, D_EXP:]
        a = hg * (1.0 / (1.0 + jnp.exp(-hg))) * hu
        o_ref[...] = _pack_halves(_dot(a.astype(BF16), wd_s[...]))

    @pl.when(valid <= 0)
    def _():
        o_ref[...] = jnp.zeros_like(o_ref)


def _moe_experts(xs, tile_expert, tile_valid, w_gate, w_up, w_down, layer):
    P = xs.shape[0]
    grid_spec = pltpu.PrefetchScalarGridSpec(
        num_scalar_prefetch=2,
        grid=(P // TM_E,),
        in_specs=[pl.BlockSpec((TM_E, HALF), lambda i, te, tv: (i, 0)),
                  pl.BlockSpec((None, None, D, D_EXP), lambda i, te, tv: (layer, te[i], 0, 0)),
                  pl.BlockSpec((None, None, D, D_EXP), lambda i, te, tv: (layer, te[i], 0, 0)),
                  pl.BlockSpec((None, None, D_EXP, D), lambda i, te, tv: (layer, te[i], 0, 0))],
        out_specs=pl.BlockSpec((TM_E, HALF), lambda i, te, tv: (i, 0)),
        scratch_shapes=[pltpu.VMEM((D, 2 * D_EXP), BF16), pltpu.VMEM((D_EXP, D), BF16)],
    )
    return pl.pallas_call(
        _moe_kernel,
        grid_spec=grid_spec,
        out_shape=jax.ShapeDtypeStruct((P, HALF), jnp.int32),
        compiler_params=_cparams(("arbitrary",)),
        name="moe_experts",
    )(tile_expert, tile_valid, xs, w_gate, w_up, w_down)


def _sc_mesh():
    return plsc.VectorSubcoreMesh(core_axis_name="c", subcore_axis_name="s",
                                  num_cores=SC_CORES, num_subcores=SC_SUBCORES)


def _sc_scatter_rows(src, d1, d2, P):
    N, W = src.shape
    per_w = N // (SC_CORES * SC_SUBCORES)

    @functools.partial(pl.kernel, mesh=_sc_mesh(), out_type=jax.ShapeDtypeStruct((P, W), src.dtype),
                       scratch_types=[pltpu.VMEM((SC_CHUNK,), jnp.int32), pltpu.VMEM((SC_CHUNK, W), src.dtype)],
                       name="sc_scatter_rows")
    def k(src_hbm, d1_hbm, d2_hbm, out_hbm, idx_v, rows_v):
        wid = lax.axis_index("s") * SC_CORES + lax.axis_index("c")

        @pl.loop(0, per_w // SC_CHUNK)
        def _(c):
            off = wid * per_w + c * SC_CHUNK
            pltpu.sync_copy(src_hbm.at[pl.ds(off, SC_CHUNK)], rows_v)
            pltpu.sync_copy(d1_hbm.at[pl.ds(off, SC_CHUNK)], idx_v)
            pltpu.sync_copy(rows_v, out_hbm.at[idx_v])
            pltpu.sync_copy(d2_hbm.at[pl.ds(off, SC_CHUNK)], idx_v)
            pltpu.sync_copy(rows_v, out_hbm.at[idx_v])

    return k(src, d1, d2)


def _sc_gather_rows(table, idx):
    B, W = idx.shape[0], table.shape[1]
    per_w = B // (SC_CORES * SC_SUBCORES)

    @functools.partial(pl.kernel, mesh=_sc_mesh(), out_type=jax.ShapeDtypeStruct((B, W), table.dtype),
                       scratch_types=[pltpu.VMEM((SC_CHUNK,), jnp.int32), pltpu.VMEM((SC_CHUNK, W), table.dtype)],
                       name="sc_gather_rows")
    def k(table_hbm, idx_hbm, out_hbm, idx_v, rows_v):
        wid = lax.axis_index("s") * SC_CORES + lax.axis_index("c")

        @pl.loop(0, per_w // SC_CHUNK)
        def _(c):
            off = wid * per_w + c * SC_CHUNK
            pltpu.sync_copy(idx_hbm.at[pl.ds(off, SC_CHUNK)], idx_v)
            pltpu.sync_copy(table_hbm.at[idx_v], rows_v)
            pltpu.sync_copy(rows_v, out_hbm.at[pl.ds(off, SC_CHUNK)])

    return k(table, idx)


def _dest_kernel(offs_ref, ri_ref, d_ref):
    ri = ri_ref[...]
    base = jnp.zeros(ri.shape, F32)
    for e in range(N_EXP):
        base = jnp.where(ri == e, offs_ref[e].astype(F32), base)
    d_ref[...] = (base + pltpu.roll(ri, 6, 0)).astype(jnp.int32)


def _dispatch_plan(ri, cnt, N):
    counts = cnt[:N_EXP, 0].astype(jnp.int32)
    padded = ((counts + TM_E - 1) // TM_E) * TM_E
    ends = jnp.cumsum(padded)
    offs = ends - padded
    take = lambda t, i: t.at[i].get(mode="promise_in_bounds")
    dest = pl.pallas_call(
        _dest_kernel,
        in_specs=[pl.BlockSpec(memory_space=pltpu.SMEM), pl.BlockSpec((8, N), lambda: (0, 0))],
        out_specs=pl.BlockSpec((8, N), lambda: (0, 0)),
        out_shape=jax.ShapeDtypeStruct((8, N), jnp.int32),
        name="dispatch_rows",
    )(offs, ri)
    d1, d2 = dest[0], dest[1]
    P = 2 * N + N_EXP * TM_E
    tile_start = jnp.arange(P // TM_E, dtype=jnp.int32) * TM_E
    tile_expert = jnp.minimum(jnp.sum((ends[None, :] <= tile_start[:, None]).astype(jnp.int32), axis=1), N_EXP - 1)
    tile_valid = jnp.clip(take(offs + counts, tile_expert) - tile_start, 0, TM_E)
    return d1, d2, P, tile_expert, tile_valid


def _combine(h_ref, y1_ref, y2_ref, rg_ref):
    rg = rg_ref[...]
    lo1, hi1 = _unpack_halves(y1_ref[...])
    lo2, hi2 = _unpack_halves(y2_ref[...])
    g1, g2 = rg[:, 0:1], rg[:, 1:2]
    return h_ref[...] + jnp.concatenate([g1 * lo1 + g2 * lo2, g1 * hi1 + g2 * hi2], axis=1)


ODD_WIDTHS = (1024, 1024, 512, 512)


def _combine_proj_odd_kernel(h_ref, y1_ref, y2_ref, rg_ref, g_ref, win_ref, wvt_ref, ind_ref, hn_ref, vt_ref,
                             vdt_ref, *out_refs):
    hn = _combine(h_ref, y1_ref, y2_ref, rg_ref)
    hn_ref[...] = hn
    u = _rms(hn, g_ref[...]).astype(BF16)
    vt_ref[...] = _values_t(_dot_nt(wvt_ref[0:512, :], u)).astype(BF16)
    vdt_ref[...] = _dot_nt(wvt_ref[512:1024, :], u).astype(BF16)
    qc_ref, kc_ref, qd_ref, kd_ref = out_refs
    qc = _spread_heads(_dot(u, win_ref[:, 0:512]), lambda h: 0)
    kc = _spread_heads(_dot(u, win_ref[:, 512:1024]), lambda h: 0)
    for hh in range(8):
        qc_ref[:, 128 * hh:128 * (hh + 1)] = qc[hh].astype(BF16)
        kc_ref[:, 128 * hh:128 * (hh + 1)] = (kc[hh] + ind_ref[...]).astype(BF16)
    qd_ref[...] = _dot(u, win_ref[:, 1024:1536]).astype(BF16)
    kd_ref[...] = _dot(u, win_ref[:, 1536:2048]).astype(BF16)


def _odd_weights(w_in):
    s = HD ** -0.5 * LOG2E
    scale = np.asarray([s] * 512 + [1.0] * 512 + [s] * 512 + [1.0] * 512, np.float32)
    win = jnp.concatenate([w_in[:, 0:1024], w_in[:, 1536:2560]], axis=1) * jnp.asarray(scale)
    wvt = jnp.concatenate([w_in[:, 1024:1536], w_in[:, 2560:3072]], axis=1).T
    return win.astype(BF16), wvt.astype(BF16)


def _block_indicator(T):
    assert T // MOBA_L <= 64
    ind = np.zeros((T, 128), np.float32)
    ind[np.arange(T), 64 + np.arange(T) // MOBA_L] = 1.0
    return jnp.asarray(ind)


def _y_specs(N):
    return [pl.BlockSpec((TM, HALF), lambda i: (i, 0)), pl.BlockSpec((TM, HALF), lambda i: (i + N // TM, 0))]


def _combine_proj_odd(h, y12, rg, g, win, wvt, ind, T):
    N = h.shape[0]
    nt = T // TM
    full = lambda a: pl.BlockSpec(a.shape, lambda i: (0,) * a.ndim)
    row = lambda w: pl.BlockSpec((TM, w), lambda i: (i, 0))
    return pl.pallas_call(
        _combine_proj_odd_kernel,
        grid=(N // TM,),
        in_specs=[row(D)] + _y_specs(N) + [row(128), full(g), full(win), full(wvt),
                  pl.BlockSpec((TM, 128), lambda i: (i % nt, 0))],
        out_specs=[row(D), pl.BlockSpec((None, 8 * VT_ROWS, TM), lambda i: (i // nt, 0, i % nt)),
                   pl.BlockSpec((None, 512, TM), lambda i: (i // nt, 0, i % nt))]
        + [row(w) for w in ODD_WIDTHS],
        out_shape=[jax.ShapeDtypeStruct((N, D), F32), jax.ShapeDtypeStruct((N // T, 8 * VT_ROWS, T), BF16),
                   jax.ShapeDtypeStruct((N // T, 512, T), BF16)]
        + [jax.ShapeDtypeStruct((N, w), BF16) for w in ODD_WIDTHS],
        compiler_params=_cparams(("arbitrary",)),
        name="combine_proj_odd",
    )(h, y12, y12, rg, g, win, wvt, ind)


def _combine_final_kernel(h_ref, y1_ref, y2_ref, rg_ref, g_ref, o_ref):
    o_ref[...] = _rms(_combine(h_ref, y1_ref, y2_ref, rg_ref), g_ref[...])


def _combine_final(h, y12, rg, g):
    N = h.shape[0]
    row = lambda w: pl.BlockSpec((TM, w), lambda i: (i, 0))
    return pl.pallas_call(
        _combine_final_kernel,
        grid=(N // TM,),
        in_specs=[row(D)] + _y_specs(N) + [row(128), pl.BlockSpec((1, D), lambda i: (0, 0))],
        out_specs=row(D),
        out_shape=jax.ShapeDtypeStruct((N, D), F32),
        compiler_params=_cparams(("arbitrary",)),
        name="combine_final",
    )(h, y12, y12, rg, g)


def _moba_kernel(q_ref, k_ref, vt_ref, tb_ref, o_ref, kmean_ref, qa_ref, s_ref, smax_ref, m_ref, acc_ref, *, nkb,
                 nq):
    ia = pl.program_id(2)
    L, tq = MOBA_L, MOBA_TQ

    @pl.when(ia == 0)
    def _():
        kmean_ref[...] = jnp.zeros_like(kmean_ref)
        for n in range(nkb):
            kmean_ref[64 + n:65 + n, :] = jnp.sum(k_ref[n * L:(n + 1) * L, :].astype(F32), axis=0,
                                                  keepdims=True) * (1.0 / L)

    nb = -(-nkb // 8) * 8
    lane = lax.broadcasted_iota(jnp.int32, (tq, 128), 1)
    blk = lax.broadcasted_iota(jnp.int32, (nb, tq), 0)
    half = jnp.where(lax.broadcasted_iota(jnp.int32, (nb, tq), 1) >= L, 1, 0)
    blk_f = blk.astype(F32)
    for t, tile in enumerate((ia, nq - 1 - ia)):
        own = 2 * tile + half
        for j in range(2):
            qj = q_ref[pl.ds(pl.multiple_of(tile * tq, tq), tq), 128 * j:128 * (j + 1)]
            gsc = lax.dot_general(kmean_ref[64:64 + nb, 128 * j:128 * (j + 1)], qj.astype(F32),
                                  (((1,), (1,)), ((), ())), preferred_element_type=F32,
                                  precision=lax.Precision.HIGHEST)
            g = jnp.where(blk < own, gsc, NEG)
            allowed = jnp.where(blk == own, 1.0, 0.0)
            for _ in range(MOBA_TOPK):
                mx = jnp.max(g, axis=0, keepdims=True)
                idx = jnp.min(jnp.where(g == mx, blk_f, 1e6), axis=0, keepdims=True)
                pick = blk_f == idx
                allowed = jnp.maximum(allowed, jnp.where(pick, jnp.where(mx > 0.5 * NEG, 1.0, 0.0), 0.0))
                g = jnp.where(pick, 2.0 * NEG, g)
            mask = jnp.concatenate([jnp.zeros((64, tq), F32), jnp.where(allowed > 0.5, 0.0, NEG),
                                    jnp.zeros((64 - nb, tq), F32)], axis=0)
            qa_ref[t, j] = jnp.where(lane < HD, qj, mask.T.astype(BF16))

    def bias_of(start, q0, j):
        parts = []
        for c in range(tq // L):
            off = jnp.minimum(q0 - start - c * L, BIAS_CONST_FROM) + BIAS_PAD
            parts.append(tb_ref[j, :, pl.ds(pl.multiple_of(off, L), tq)])
        return jnp.concatenate(parts, axis=0)

    _causal_attention_pair_t(ia, nq, lambda t, q0, j: qa_ref[t, j], k_ref, vt_ref, o_ref, s_ref, smax_ref, m_ref,
                             acc_ref, tq=tq, bias_of=bias_of)


def _moba_attention(q, k, vt, tb, B, T):
    tq = MOBA_TQ
    nq = T // tq
    return pl.pallas_call(
        functools.partial(_moba_kernel, nkb=T // MOBA_L, nq=nq),
        grid=(B, 4, nq // 2),
        in_specs=[pl.BlockSpec((T, 256), lambda b, p, i: (b, p)),
                  pl.BlockSpec((T, 256), lambda b, p, i: (b, p)),
                  pl.BlockSpec((None, 2 * VT_ROWS, T), lambda b, p, i: (b, p, 0)),
                  pl.BlockSpec((2, MOBA_L, BIAS_ROWS), lambda b, p, i: (p, 0, 0))],
        out_specs=pl.BlockSpec((T, 128), lambda b, p, i: (b, p)),
        out_shape=jax.ShapeDtypeStruct((B * T, 512), BF16),
        scratch_shapes=[pltpu.VMEM((128, 256), F32), pltpu.VMEM((2, 2, tq, 128), BF16)]
        + _attention_t_scratch(tq),
        compiler_params=_cparams(("arbitrary", "arbitrary", "arbitrary")),
        name="moba_attention",
    )(q, k, vt, tb)


def _sb_kernel(q_ref, k_ref, vt_ref, o_ref, *, tq, pairs):
    i = pl.program_id(2)
    lane = lax.broadcasted_iota(jnp.int32, (tq, 128), 1)
    qs = []
    for p in range(pairs):
        q_pair = q_ref[:, 128 * p:128 * (p + 1)]
        qs += [jnp.where(lane < HD, q_pair, 0).astype(BF16), jnp.where(lane >= HD, q_pair, 0).astype(BF16)]
    key = lax.broadcasted_iota(jnp.int32, (tq, tq), 0)
    qry = lax.broadcasted_iota(jnp.int32, (tq, tq), 1)
    past = key < qry
    suffix = jnp.where(qry >= key, -1.0, 0.0).astype(BF16)
    heads = range(2 * pairs)

    def step(kb, carry, boundary):
        start = pl.multiple_of(kb * tq, tq)
        zs = [_dot_nt(k_ref[pl.ds(start, tq), 128 * (h // 2):128 * (h // 2 + 1)], qs[h]) for h in heads]
        parts = []
        for z in zs:
            neg_abs = pltpu.bitcast(pltpu.bitcast(z, jnp.uint32) | jnp.uint32(0x80000000), F32)
            soft = jnp.maximum(z, 0.0) + jnp.log2(1.0 + jnp.exp2(neg_abs))
            if boundary:
                soft = jnp.where(past, soft, 0.0)
            parts.append(soft.astype(BF16))
        sums = [_dot(suffix, part) for part in parts]
        ws = []
        for h in heads:
            w = jnp.exp2(zs[h] + (carry[h][0] + sums[h]))
            ws.append((jnp.where(past, w, 0.0) if boundary else w).astype(BF16))
        return tuple((carry[h][0] + sums[h][0:1], carry[h][1]
                      + _dot(vt_ref[128 * (h // 2):128 * (h // 2 + 1), pl.ds(start, tq)], ws[h])) for h in heads)

    def live(carry):
        top = carry[0][0]
        for c, _ in carry[1:]:
            top = jnp.maximum(top, c)
        return (jnp.max(top) > SB_DONE).astype(jnp.int32)

    def body(state):
        t, _, carry = state
        carry = step(i - 1 - t, carry, False)
        return t + 1, live(carry), carry

    init = tuple((jnp.zeros((1, tq), F32), jnp.zeros((128, tq), F32)) for _ in range(2 * pairs))
    carry = step(i, init, True)
    _, _, carry = lax.while_loop(lambda st: (st[0] < i) & (st[1] > 0), body, (jnp.int32(0), live(carry), carry))
    rows = lax.broadcasted_iota(jnp.int32, (128, tq), 0)
    for p in range(pairs):
        ot = jnp.where(rows < HD, carry[2 * p][1], carry[2 * p + 1][1])
        o_ref[:, 128 * p:128 * (p + 1)] = ot.T.astype(BF16)


def _sb_attention(q, k, vt, B, T, tq=256, pairs=4):
    nq = T // tq
    w = 128 * pairs
    return pl.pallas_call(
        functools.partial(_sb_kernel, tq=tq, pairs=pairs),
        grid=(B, 4 // pairs, nq),
        in_specs=[pl.BlockSpec((tq, w), lambda b, p, i: (b * nq + i, p)),
                  pl.BlockSpec((T, w), lambda b, p, i: (b, p)),
                  pl.BlockSpec((None, w, T), lambda b, p, i: (b, p, 0))],
        out_specs=pl.BlockSpec((tq, w), lambda b, p, i: (b * nq + i, p)),
        out_shape=jax.ShapeDtypeStruct((B * T, 512), BF16),
        compiler_params=_cparams(("arbitrary", "arbitrary", "arbitrary")),
        name="sb_attention",
    )(q, k, vt)


def _router_weights(w_group, b_group, w_router, b_router):
    pad = 128 - N_EXP - N_GROUPS
    wr = jnp.concatenate([w_router, w_group, jnp.zeros((D, pad), F32)], axis=1).T
    hi = wr.astype(BF16)
    lo = (wr - hi.astype(F32)).astype(BF16)
    br = jnp.concatenate([b_router, b_group, jnp.zeros((pad,), F32)])[:, None]
    return jnp.concatenate([hi, lo], axis=0), br


def _ffn(h, oa, ob, wo, g, w_group, b_group, w_router, b_router, w_gate, w_up, w_down, layer):
    wr, br = _router_weights(w_group, b_group, w_router, b_router)
    hn, u, ri, rg, cnt = _outproj_router(h, oa, ob, wo.astype(BF16), g[None, :], wr, br)
    d1, d2, P, tile_expert, tile_valid = _dispatch_plan(ri, cnt, h.shape[0])
    xs = _sc_scatter_rows(u, d1, d2, P)
    ys = _moe_experts(xs, tile_expert, tile_valid, w_gate, w_up, w_down, layer)
    return hn, _sc_gather_rows(ys, jnp.concatenate([d1, d2])), rg


def kernel(x, norm_mix, norm_ffn, norm_final, rel_bias_table, w_in_even, g_mla_q, g_mla_kv, w_mla_qb, w_mla_kvb, swa_sinks, w_out_even, w_in_odd, w_out_odd, moe_w_group, moe_b_group, moe_w_router, moe_b_router, moe_w_gate, moe_w_up, moe_w_down):
    B, T, _ = x.shape
    h = x.reshape(B * T, D)
    tb_swa, tb_moba = _bias_tables(rel_bias_table)

    win, wqa, wqb, wk, wv = _even_weights(w_in_even[0], w_mla_qb[0], w_mla_kvb[0])
    q, k, v, qs8, ks, vst = _proj_even(h, norm_mix[0][None, :], win, g_mla_q[0][None, :], g_mla_kv[0][None, :],
                                       wqa, wqb, wk, wv, _rope_tables(T), T)
    oa = _mla_attention(q, k, v, B, T)
    ob = _swa_attention(qs8, ks, vst, swa_sinks[0], tb_swa, B, T)
    h, y12, rg = _ffn(h, oa, ob, w_out_even[0], norm_ffn[0], moe_w_group[0], moe_b_group[0], moe_w_router[0],
                      moe_b_router[0], moe_w_gate, moe_w_up, moe_w_down, 0)

    h, vct, vdt, qc, kc, qd, kd = _combine_proj_odd(h, y12, rg, norm_mix[1][None, :], *_odd_weights(w_in_odd[0]),
                                                   _block_indicator(T), T)
    oc = _moba_attention(qc, kc, vct, tb_moba, B, T)
    od = _sb_attention(qd, kd, vdt, B, T)
    h, y12, rg = _ffn(h, oc, od, w_out_odd[0], norm_ffn[1], moe_w_group[1], moe_b_group[1], moe_w_router[1],
                      moe_b_router[1], moe_w_gate, moe_w_up, moe_w_down, 1)
    out = _combine_final(h, y12, rg, norm_final[None, :])
    return out.reshape(B, T, D)
```

```python
import functools
import math

import numpy as np
import jax
import jax.numpy as jnp
from jax import lax
from jax.experimental import pallas as pl
from jax.experimental.pallas import tpu as pltpu
from jax.experimental.pallas import tpu_sc as plsc

F32 = jnp.float32
BF16 = jnp.bfloat16
NEG = -1e30
EPS = 1e-6

D = 1024
HD = 64
MLA_HEADS, MLA_QR, MLA_KVR, MLA_NOPE, MLA_ROPE, MLA_V = 8, 256, 128, 64, 32, 64
ROPE_THETA = 10000.0
SWA_HEADS, SWA_KV, SWA_WIN = 8, 2, 128
MOBA_L, MOBA_TOPK = 256, 3
REL_BUCKETS, REL_MAX = 32, 2048
N_GROUPS, EPG, N_EXP, D_EXP = 4, 8, 32, 256
BIAS_CONST_FROM = 1792
MOBA_TQ = 2 * MOBA_L
BIAS_PAD = MOBA_L
BIAS_ROWS = BIAS_PAD + BIAS_CONST_FROM + MOBA_TQ
SB_DONE = -160.0
TM = 1024
TM_E = 512
VMEM_LIMIT = 56 * 1024 * 1024
SC_CORES, SC_SUBCORES = 2, 16
SC_CHUNK = 128
HALF = D // 2
VT_ROWS = HD + 16
LOG2E = math.log2(math.e)


def _pack_halves(x):
    lo = pltpu.bitcast(x[:, :HALF].astype(BF16).astype(F32), jnp.uint32) >> 16
    hi = pltpu.bitcast(x[:, HALF:].astype(BF16).astype(F32), jnp.uint32) & jnp.uint32(0xFFFF0000)
    return pltpu.bitcast(lo | hi, jnp.int32)


def _unpack_halves(w):
    w = pltpu.bitcast(w, jnp.uint32)
    return pltpu.bitcast(w << 16, F32), pltpu.bitcast(w & jnp.uint32(0xFFFF0000), F32)


def _cparams(sem):
    return pltpu.CompilerParams(dimension_semantics=sem, vmem_limit_bytes=VMEM_LIMIT)


def _dot(a, b):
    return jnp.dot(a, b, preferred_element_type=F32)


def _dot_nt(a, b):
    return lax.dot_general(a, b, (((1,), (1,)), ((), ())), preferred_element_type=F32)


def _rms(x, g):
    return x * lax.rsqrt(jnp.mean(x * x, axis=-1, keepdims=True) + EPS) * g


def _values_t(vt):
    ones = jnp.ones((VT_ROWS - HD, vt.shape[1]), vt.dtype)
    return jnp.concatenate([blk for h in range(8) for blk in (vt[HD * h:HD * (h + 1)], ones)], axis=0)


def _spread_heads(x, lane_of):
    lane = lax.broadcasted_iota(jnp.int32, (x.shape[0], 128), 1)
    out = []
    for h in range(8):
        pair = x[:, 128 * (h // 2):128 * (h // 2 + 1)]
        if HD * (h % 2) != lane_of(h):
            pair = pltpu.roll(pair, HD, 1)
        out.append(jnp.where((lane >= lane_of(h)) & (lane < lane_of(h) + HD), pair, 0.0))
    return out


def _t5_bucket_np(dist):
    n = np.maximum(dist, 0)
    max_exact = REL_BUCKETS // 2
    nf = np.maximum(n, 1).astype(np.float32)
    large = max_exact + (np.log(nf / np.float32(max_exact)) / np.float32(math.log(REL_MAX / max_exact))
                         * np.float32(REL_BUCKETS - max_exact)).astype(np.int32)
    large = np.minimum(large, REL_BUCKETS - 1)
    return np.where(n < max_exact, n, large).astype(np.int32)


def _bias_kernel(table_ref, bucket_ref, out_ref, *, scale, ranges):
    b = bucket_ref[...]
    for m, (lo, hi) in enumerate(ranges):
        @pl.when(pl.program_id(0) == m)
        def _():
            for h in range(8):
                acc = jnp.full(b.shape, NEG if lo == REL_BUCKETS else table_ref[lo, h] * scale, F32)
                for k in range(lo + 1, min(hi, REL_BUCKETS - 1) + 1):
                    acc = jnp.where(b == k, table_ref[k, h] * scale, acc)
                if hi == REL_BUCKETS:
                    acc = jnp.where(b == REL_BUCKETS, NEG, acc)
                out_ref[h] = acc


def _bias_lookup(rel_table, dist, scale, name, masked=None):
    R, C = dist.shape
    buckets = _t5_bucket_np(dist)
    if masked is not None:
        buckets = np.where(masked, REL_BUCKETS, buckets).astype(np.int32)
    ranges = tuple((int(buckets[:, c:c + 256].min()), int(buckets[:, c:c + 256].max())) for c in range(0, C, 256))
    return pl.pallas_call(
        functools.partial(_bias_kernel, scale=scale, ranges=ranges),
        grid=(C // 256,),
        in_specs=[pl.BlockSpec(memory_space=pltpu.SMEM),
                  pl.BlockSpec((R, 256), lambda m: (0, m))],
        out_specs=pl.BlockSpec((8, R, 256), lambda m: (0, 0, m)),
        out_shape=jax.ShapeDtypeStruct((8, R, C), F32),
        compiler_params=_cparams(("arbitrary",)),
        name=name,
    )(rel_table, jnp.asarray(buckets))


def _bias_tables(rel_table):
    d = np.arange(256)[None, :] - np.arange(256)[:, None]
    swa = _bias_lookup(rel_table, d, LOG2E, "bias_swa", masked=(d < 0) | (d >= SWA_WIN))
    moba = _bias_lookup(rel_table, np.arange(BIAS_ROWS)[None, :] - BIAS_PAD - np.arange(MOBA_L)[:, None], LOG2E,
                        "bias_moba")
    return swa, moba


def _proj_even_kernel(h_ref, g_ref, win_ref, gq_ref, gkv_ref, wqa_ref, wqb_ref, wk_ref, wv_ref,
                      cq_ref, sq_ref, ck_ref, sk_ref,
                      q_ref, k_ref, v_ref, qs_ref, ks_ref, vst_ref):
    u = _rms(h_ref[...], g_ref[...]).astype(BF16)
    nq = _rms(_dot(u, win_ref[:, 0:256]), gq_ref[...]).astype(BF16)
    qa = _dot(nq, wqa_ref[...])
    qb = _dot(nq, wqb_ref[...])
    cq, sq = cq_ref[...], sq_ref[...]
    for hh in range(MLA_HEADS):
        sl = slice(128 * hh, 128 * (hh + 1))
        q_ref[:, sl] = (qa[:, sl] * cq + qb[:, sl] * sq).astype(BF16)
    nkv = _rms(_dot(u, win_ref[:, 256:384]), gkv_ref[...]).astype(BF16)
    kn = _dot(nkv, wk_ref[...])
    v_ref[...] = _values_t(_dot_nt(wv_ref[...], nkv)).astype(BF16)
    rest = _dot(u, win_ref[:, 896:1408])
    kr = rest[:, 128:256] * ck_ref[...] + rest[:, 256:384] * sk_ref[...]
    for hh in range(MLA_HEADS):
        sl = slice(128 * hh, 128 * (hh + 1))
        k_ref[:, sl] = (kn[:, sl] + kr).astype(BF16)
    qs = _spread_heads(_dot(u, win_ref[:, 384:896]), lambda hq: HD * (hq // (SWA_HEADS // SWA_KV)))
    for hq in range(SWA_HEADS):
        qs_ref[:, 128 * hq:128 * (hq + 1)] = qs[hq].astype(BF16)
    ks_ref[...] = rest[:, 0:128].astype(BF16)
    vst_ref[...] = rest[:, 384:512].T.astype(BF16)


def _cols(w, idx, scale=None):
    idx = np.asarray(idx)
    cuts = [0] + [n for n in range(1, len(idx)) if (idx[n] < 0) != (idx[n - 1] < 0)
                  or (idx[n] >= 0 and idx[n] != idx[n - 1] + 1)] + [len(idx)]
    parts = [jnp.zeros((w.shape[0], b - a), w.dtype) if idx[a] < 0 else w[:, idx[a]:idx[a] + b - a]
             for a, b in zip(cuts[:-1], cuts[1:])]
    out = jnp.concatenate(parts, axis=1)
    if scale is not None:
        out = out * scale
    return out


def _even_weights(w_in, w_qb, w_kvb):
    o_cq, o_ckv, o_kr = 0, MLA_QR, MLA_QR + MLA_KVR
    o_qs = o_kr + MLA_ROPE
    o_ks = o_qs + SWA_HEADS * HD
    o_vs = o_ks + SWA_KV * HD
    idx = list(range(o_cq, o_cq + MLA_QR)) + list(range(o_ckv, o_ckv + MLA_KVR))
    scale = [1.0] * len(idx)
    idx += list(range(o_qs, o_qs + SWA_HEADS * HD))
    scale += [HD ** -0.5 * LOG2E] * (SWA_HEADS * HD)
    idx += list(range(o_ks, o_ks + 128))
    scale += [1.0] * 128
    half = MLA_ROPE // 2
    idx += [-1] * 64 + list(range(o_kr, o_kr + MLA_ROPE)) + [-1] * 32
    idx += [-1] * 64 + list(range(o_kr + half, o_kr + MLA_ROPE)) + list(range(o_kr, o_kr + half)) + [-1] * 32
    idx += list(range(o_vs, o_vs + 128))
    scale += [1.0] * 384
    win = _cols(w_in, idx, jnp.asarray(np.asarray(scale, np.float32))).astype(BF16)
    qw = MLA_NOPE + MLA_ROPE
    ia, ib, ik, iv = [], [], [], []
    for h in range(MLA_HEADS):
        ia += list(range(qw * h, qw * h + qw)) + [-1] * 32
        ib += [-1] * 64 + list(range(qw * h + MLA_NOPE + half, qw * h + qw)) \
            + list(range(qw * h + MLA_NOPE, qw * h + MLA_NOPE + half)) + [-1] * 32
        ik += list(range(128 * h, 128 * h + MLA_NOPE)) + [-1] * 64
        iv += list(range(128 * h + MLA_NOPE, 128 * h + 128))
    return (win, _cols(w_qb, ia).astype(BF16), _cols(w_qb, ib).astype(BF16),
            _cols(w_kvb, ik).astype(BF16), _cols(w_kvb, iv).T.astype(BF16))


def _rope_tables(T):
    f32 = np.float32
    freqs = f32(ROPE_THETA) ** (-np.arange(0, MLA_ROPE, 2, dtype=f32) / f32(MLA_ROPE))
    ang = np.arange(T, dtype=f32)[:, None] * freqs[None, :]
    cos, sin = np.cos(ang), np.sin(ang)
    z = lambda n: np.zeros((T, n), f32)
    scale = f32((MLA_NOPE + MLA_ROPE) ** -0.5 * LOG2E)
    ck = np.concatenate([z(64), cos, cos, z(32)], axis=1)
    sk = np.concatenate([z(64), -sin, sin, z(32)], axis=1)
    cq = np.concatenate([np.ones((T, 64), f32), cos, cos, z(32)], axis=1) * scale
    sq = sk * scale
    return tuple(jnp.asarray(t, F32) for t in (cq, sq, ck, sk))


def _proj_even(h, g, win, gq, gkv, wqa, wqb, wk, wv, tables, T):
    N = h.shape[0]
    nt = T // TM
    full = lambda a: pl.BlockSpec(a.shape, lambda i: (0,) * a.ndim)
    tab = pl.BlockSpec((TM, 128), lambda i: (i % nt, 0))
    row = lambda w: pl.BlockSpec((TM, w), lambda i: (i, 0))
    outs = (1024, 1024, -8 * VT_ROWS, 1024, 128, -128)
    spec = lambda w: row(w) if w > 0 else pl.BlockSpec((None, -w, TM), lambda i: (i // nt, 0, i % nt))
    shape = lambda w: jax.ShapeDtypeStruct((N, w) if w > 0 else (N // T, -w, T), BF16)
    return pl.pallas_call(
        _proj_even_kernel,
        grid=(N // TM,),
        in_specs=[row(D), full(g), full(win), full(gq), full(gkv), full(wqa), full(wqb), full(wk), full(wv),
                  tab, tab, tab, tab],
        out_specs=[spec(w) for w in outs],
        out_shape=[shape(w) for w in outs],
        compiler_params=_cparams(("arbitrary",)),
        name="proj_even",
    )(h, g, win, gq, gkv, wqa, wqb, wk, wv, *tables)


def _finish_t(accs):
    ot = jnp.concatenate([acc[:HD] / acc[HD:HD + 1] for acc in accs], axis=0)
    return ot.T.astype(BF16)


def _causal_attention_pair_t(ia, nq, q_of, k_ref, vt_ref, o_ref, s_ref, smax_ref, m_ref, acc_ref, *, tq,
                             bias_of=None):
    nh = s_ref.shape[1]
    first_query = (ia * tq, (nq - 1 - ia) * tq)
    n_items = nq + 1
    assert n_items % 2 == 1
    key = lax.broadcasted_iota(jnp.int32, (tq, tq), 0)
    qry = lax.broadcasted_iota(jnp.int32, (tq, tq), 1)

    def item(p):
        if isinstance(p, int) and p < 2:
            return p, first_query[p], first_query[p]
        t = jnp.where(p - 2 >= ia, 1, 0)
        return t, jnp.where(t == 1, first_query[1], first_query[0]), (p - 2 - t * ia) * tq

    def scores_to(slot, p):
        t, q0, start = item(p)
        start = pl.multiple_of(start, tq)
        for j in range(nh):
            st = _dot_nt(k_ref[pl.ds(start, tq), 128 * j:128 * (j + 1)], q_of(t, q0, j))
            if bias_of is not None:
                st = st + bias_of(start, q0, j)
            if isinstance(p, int) and p < 2:
                st = jnp.where(key <= qry, st, NEG)
            s_ref[slot, j] = st
            smax_ref[slot, j] = jnp.max(st, axis=0, keepdims=True)

    def update_from(slot, p):
        t, _, start = item(p)
        start = pl.multiple_of(start, tq)
        m_new = [jnp.maximum(m_ref[t, j], smax_ref[slot, j]) for j in range(nh)]
        pts = [jnp.exp2(s_ref[slot, j] - m_new[j]).astype(BF16) for j in range(nh)]
        for j in range(nh):
            acc_ref[t, j] = (jnp.exp2(m_ref[t, j] - m_new[j]) * acc_ref[t, j]
                             + _dot(vt_ref[VT_ROWS * j:VT_ROWS * (j + 1), pl.ds(start, tq)], pts[j]))
            m_ref[t, j] = m_new[j]

    def pair(k):
        scores_to(1, k + 1)
        update_from(0, k)
        scores_to(0, k + 2)
        update_from(1, k + 1)

    def quad(c, _):
        pair(2 + 4 * c)
        pair(4 + 4 * c)
        return 0

    m_ref[...] = jnp.full(m_ref.shape, NEG, F32)
    acc_ref[...] = jnp.zeros(acc_ref.shape, F32)
    scores_to(0, 0)
    pair(0)
    later_pairs = (n_items - 1) // 2 - 1
    lax.fori_loop(0, later_pairs // 2, quad, 0)
    if later_pairs % 2:
        pair(n_items - 3)
    update_from(0, n_items - 1)
    for t in range(2):
        o_ref[pl.ds(pl.multiple_of(first_query[t], tq), tq), :] = _finish_t([acc_ref[t, j] for j in range(nh)])


def _attention_t_scratch(tq, nh=2):
    return [pltpu.VMEM((2, nh, tq, tq), F32), pltpu.VMEM((2, nh, 1, tq), F32), pltpu.VMEM((2, nh, 1, tq), F32),
            pltpu.VMEM((2, nh, VT_ROWS, tq), F32)]


def _mla_kernel(q_ref, k_ref, vt_ref, o_ref, s_ref, smax_ref, m_ref, acc_ref, *, tq, nq):
    def q_of(t, q0, j):
        return q_ref[pl.ds(pl.multiple_of(q0, tq), tq), 128 * j:128 * (j + 1)]

    _causal_attention_pair_t(pl.program_id(2), nq, q_of, k_ref, vt_ref, o_ref, s_ref, smax_ref, m_ref, acc_ref,
                             tq=tq)


def _mla_attention(q, k, vt, B, T, tq=512):
    nq = T // tq
    return pl.pallas_call(
        functools.partial(_mla_kernel, tq=tq, nq=nq),
        grid=(B, MLA_HEADS // 2, nq // 2),
        in_specs=[pl.BlockSpec((T, 256), lambda b, p, i: (b, p)),
                  pl.BlockSpec((T, 256), lambda b, p, i: (b, p)),
                  pl.BlockSpec((None, 2 * VT_ROWS, T), lambda b, p, i: (b, p, 0))],
        out_specs=pl.BlockSpec((T, 128), lambda b, p, i: (b, p)),
        out_shape=jax.ShapeDtypeStruct((B * T, MLA_HEADS * MLA_V), BF16),
        scratch_shapes=_attention_t_scratch(tq),
        compiler_params=_cparams(("arbitrary", "arbitrary", "arbitrary")),
        name="mla_attention",
    )(q, k, vt)


def _swa_kernel(sink_ref, q_ref, k_ref, vt_ref, tb_ref, o_ref, *, tq):
    i = pl.program_id(1)
    G = SWA_HEADS // SWA_KV
    W = 2 * SWA_WIN
    head = lax.broadcasted_iota(jnp.int32, (1, G * 128), 1) >> 7
    sinks = []
    for c in range(SWA_KV):
        sink = jnp.zeros((1, G * 128), F32)
        for g in range(G):
            sink = jnp.where(head == g, sink_ref[G * c + g], sink)
        sinks.append(sink)
    scores, values = [], []
    for r in range(tq // 128):
        qstart = i * tq + 128 * r
        kstart = pl.multiple_of(jnp.maximum(qstart - SWA_WIN, 0), 128)
        off = pl.multiple_of(qstart - kstart, 128)
        kw = k_ref[pl.ds(kstart, W), :]
        for c in range(SWA_KV):
            qg = jnp.concatenate([q_ref[128 * r:128 * (r + 1), 128 * (G * c + g):128 * (G * c + g + 1)]
                                  for g in range(G)], axis=0)
            bias = jnp.concatenate([tb_ref[G * c + g, :, pl.ds(off, 128)] for g in range(G)], axis=1)
            scores.append(_dot_nt(kw, qg) + bias)
            values.append(vt_ref[HD * c:HD * (c + 1), pl.ds(kstart, W)])
    probs = []
    for n, s in enumerate(scores):
        sink = sinks[n % SWA_KV]
        m = jnp.maximum(jnp.max(s, axis=0, keepdims=True), sink)
        p = jnp.exp2(s - m)
        probs.append((p.astype(BF16), jnp.sum(p, axis=0, keepdims=True) + jnp.exp2(sink - m)))
    for r in range(tq // 128):
        outs = []
        for c in range(SWA_KV):
            p, denom = probs[SWA_KV * r + c]
            ot = _dot(values[SWA_KV * r + c], p) / denom
            outs += [ot[:, 128 * g:128 * (g + 1)] for g in range(G)]
        o_ref[128 * r:128 * (r + 1), :] = jnp.concatenate(outs, axis=0).T.astype(BF16)


def _swa_attention(q8, k, vt, sinks, tb, B, T, tq=512):
    nq = T // tq
    return pl.pallas_call(
        functools.partial(_swa_kernel, tq=tq),
        grid=(B, nq),
        in_specs=[pl.BlockSpec(memory_space=pltpu.SMEM),
                  pl.BlockSpec((tq, 1024), lambda b, i: (b * nq + i, 0)),
                  pl.BlockSpec((T, 128), lambda b, i: (b, 0)),
                  pl.BlockSpec((None, 128, T), lambda b, i: (b, 0, 0)),
                  pl.BlockSpec((8, 256, 256), lambda b, i: (0, 0, 0))],
        out_specs=pl.BlockSpec((tq, 512), lambda b, i: (b * nq + i, 0)),
        out_shape=jax.ShapeDtypeStruct((B * T, SWA_HEADS * HD), BF16),
        compiler_params=_cparams(("arbitrary", "arbitrary")),
        name="swa_attention",
    )(sinks * LOG2E, q8, k, vt, tb)


def _outproj_router_kernel(h_ref, oa_ref, ob_ref, wo_ref, g_ref, wr_ref, br_ref,
                           hn_ref, u_ref, ri_ref, rg_ref, cnt_ref, base_ref):
    i = pl.program_id(0)

    @pl.when(i == 0)
    def _():
        base_ref[...] = jnp.zeros_like(base_ref)

    hn = h_ref[...] + _dot(oa_ref[...], wo_ref[0:512, :]) + _dot(ob_ref[...], wo_ref[512:1024, :])
    hn_ref[...] = hn
    u = _rms(hn, g_ref[...])
    u_ref[...] = _pack_halves(u)
    u_hi = u.astype(BF16)
    u_lo = (u - u_hi.astype(F32)).astype(BF16)
    both = _dot_nt(wr_ref[...], u_hi)
    logits = both[:128] + both[128:] + _dot_nt(wr_ref[0:128, :], u_lo) + br_ref[...]
    tm = logits.shape[1]
    sub = lax.broadcasted_iota(jnp.int32, (8, tm), 0).astype(F32)
    big = 1e6
    isg = sub < N_GROUPS
    gl = jnp.where(isg, logits[N_EXP:N_EXP + 8], NEG)
    gmax = jnp.max(gl, axis=0, keepdims=True)
    gsel = jnp.min(jnp.where(gl == gmax, sub, big), axis=0, keepdims=True)
    g_gate = 1.0 / jnp.sum(jnp.where(isg, jnp.exp(gl - gmax), 0.0), axis=0, keepdims=True)
    el = jnp.zeros((EPG, tm), F32)
    for grp in range(N_GROUPS):
        el = jnp.where(gsel == grp, logits[EPG * grp:EPG * (grp + 1)], el)
    m1 = jnp.max(el, axis=0, keepdims=True)
    i1 = jnp.min(jnp.where(el == m1, sub, big), axis=0, keepdims=True)
    el2 = jnp.where(sub == i1, NEG, el)
    m2 = jnp.max(el2, axis=0, keepdims=True)
    i2 = jnp.min(jnp.where(el2 == m2, sub, big), axis=0, keepdims=True)
    e1 = gsel * EPG + i1
    e2 = gsel * EPG + i2
    r = jnp.exp(m2 - m1)
    gate1 = g_gate / (1.0 + r)
    gate2 = g_gate * r / (1.0 + r)
    rows = lax.broadcasted_iota(jnp.int32, (128, tm), 0).astype(F32)
    oh1 = (rows == e1).astype(F32)
    oh2 = (rows == e2).astype(F32)
    oh = oh1 + oh2
    t_row = lax.broadcasted_iota(jnp.int32, (tm, tm), 0)
    t_col = lax.broadcasted_iota(jnp.int32, (tm, tm), 1)
    earlier = jnp.where(t_row < t_col, 1.0, 0.0).astype(BF16)
    prefix = _dot(oh.astype(BF16), earlier) + base_ref[:, 0:1]
    rank1 = jnp.sum(oh1 * prefix, axis=0, keepdims=True)
    rank2 = jnp.sum(oh2 * prefix, axis=0, keepdims=True)
    base_ref[...] = base_ref[...] + jnp.sum(oh, axis=1, keepdims=True)
    cnt_ref[...] = base_ref[...]
    ri_ref[...] = jnp.where(sub == 0, e1, jnp.where(sub == 1, e2, jnp.where(sub == 2, rank1,
                            jnp.where(sub == 3, rank2, 0.0))))
    rg_ref[...] = jnp.where(rows == 0, gate1, jnp.where(rows == 1, gate2, 0.0)).T


def _outproj_router(h, oa, ob, wo, g, wr, br):
    N = h.shape[0]
    full = lambda a: pl.BlockSpec(a.shape, lambda i: (0,) * a.ndim)
    row = lambda w: pl.BlockSpec((TM, w), lambda i: (i, 0))
    return pl.pallas_call(
        _outproj_router_kernel,
        grid=(N // TM,),
        in_specs=[row(D), row(512), row(512), full(wo), full(g), full(wr), full(br)],
        out_specs=[row(D), row(HALF), pl.BlockSpec((8, TM), lambda i: (0, i)), row(128),
                   pl.BlockSpec((128, 128), lambda i: (0, 0))],
        out_shape=[jax.ShapeDtypeStruct((N, D), F32), jax.ShapeDtypeStruct((N, HALF), jnp.int32),
                   jax.ShapeDtypeStruct((8, N), F32), jax.ShapeDtypeStruct((N, 128), F32),
                   jax.ShapeDtypeStruct((128, 128), F32)],
        scratch_shapes=[pltpu.VMEM((128, 128), F32)],
        compiler_params=_cparams(("arbitrary",)),
        name="outproj_router",
    )(h, oa, ob, wo, g, wr, br)


def _moe_kernel(te_ref, tv_ref, x_ref, wg_ref, wu_ref, wd_ref, o_ref, wgu_s, wd_s):
    i = pl.program_id(0)
    valid = tv_ref[i]

    @pl.when((i == 0) | (te_ref[i] != te_ref[jnp.maximum(i - 1, 0)]))
    def _():
        wgu_s[:, :D_EXP] = wg_ref[...].astype(BF16)
        wgu_s[:, D_EXP:] = wu_ref[...].astype(BF16)
        wd_s[...] = wd_ref[...].astype(BF16)

    @pl.when(valid > 0)
    def _():
        rows = lax.broadcasted_iota(jnp.int32, x_ref.shape, 0)
        lo, hi = _unpack_halves(jnp.where(rows < valid, x_ref[...], 0))
        hgu = _dot(lo.astype(BF16), wgu_s[:HALF, :]) + _dot(hi.astype(BF16), wgu_s[HALF:, :])
        hg, hu = hgu[:, :D_EXP], hgu[:, D_EXP:]
        a = hg * (1.0 / (1.0 + jnp.exp(-hg))) * hu
        o_ref[...] = _pack_halves(_dot(a.astype(BF16), wd_s[...]))

    @pl.when(valid <= 0)
    def _():
        o_ref[...] = jnp.zeros_like(o_ref)


def _moe_experts(xs, tile_expert, tile_valid, w_gate, w_up, w_down, layer):
    P = xs.shape[0]
    grid_spec = pltpu.PrefetchScalarGridSpec(
        num_scalar_prefetch=2,
        grid=(P // TM_E,),
        in_specs=[pl.BlockSpec((TM_E, HALF), lambda i, te, tv: (i, 0)),
                  pl.BlockSpec((None, None, D, D_EXP), lambda i, te, tv: (layer, te[i], 0, 0)),
                  pl.BlockSpec((None, None, D, D_EXP), lambda i, te, tv: (layer, te[i], 0, 0)),
                  pl.BlockSpec((None, None, D_EXP, D), lambda i, te, tv: (layer, te[i], 0, 0))],
        out_specs=pl.BlockSpec((TM_E, HALF), lambda i, te, tv: (i, 0)),
        scratch_shapes=[pltpu.VMEM((D, 2 * D_EXP), BF16), pltpu.VMEM((D_EXP, D), BF16)],
    )
    return pl.pallas_call(
        _moe_kernel,
        grid_spec=grid_spec,
        out_shape=jax.ShapeDtypeStruct((P, HALF), jnp.int32),
        compiler_params=_cparams(("arbitrary",)),
        name="moe_experts",
    )(tile_expert, tile_valid, xs, w_gate, w_up, w_down)


def _sc_mesh():
    return plsc.VectorSubcoreMesh(core_axis_name="c", subcore_axis_name="s",
                                  num_cores=SC_CORES, num_subcores=SC_SUBCORES)


def _sc_scatter_rows(src, d1, d2, P):
    N, W = src.shape
    per_w = N // (SC_CORES * SC_SUBCORES)

    @functools.partial(pl.kernel, mesh=_sc_mesh(), out_type=jax.ShapeDtypeStruct((P, W), src.dtype),
                       scratch_types=[pltpu.VMEM((SC_CHUNK,), jnp.int32), pltpu.VMEM((SC_CHUNK, W), src.dtype)],
                       name="sc_scatter_rows")
    def k(src_hbm, d1_hbm, d2_hbm, out_hbm, idx_v, rows_v):
        wid = lax.axis_index("s") * SC_CORES + lax.axis_index("c")

        @pl.loop(0, per_w // SC_CHUNK)
        def _(c):
            off = wid * per_w + c * SC_CHUNK
            pltpu.sync_copy(src_hbm.at[pl.ds(off, SC_CHUNK)], rows_v)
            pltpu.sync_copy(d1_hbm.at[pl.ds(off, SC_CHUNK)], idx_v)
            pltpu.sync_copy(rows_v, out_hbm.at[idx_v])
            pltpu.sync_copy(d2_hbm.at[pl.ds(off, SC_CHUNK)], idx_v)
            pltpu.sync_copy(rows_v, out_hbm.at[idx_v])

    return k(src, d1, d2)


def _sc_gather_rows(table, idx):
    B, W = idx.shape[0], table.shape[1]
    per_w = B // (SC_CORES * SC_SUBCORES)

    @functools.partial(pl.kernel, mesh=_sc_mesh(), out_type=jax.ShapeDtypeStruct((B, W), table.dtype),
                       scratch_types=[pltpu.VMEM((SC_CHUNK,), jnp.int32), pltpu.VMEM((SC_CHUNK, W), table.dtype)],
                       name="sc_gather_rows")
    def k(table_hbm, idx_hbm, out_hbm, idx_v, rows_v):
        wid = lax.axis_index("s") * SC_CORES + lax.axis_index("c")

        @pl.loop(0, per_w // SC_CHUNK)
        def _(c):
            off = wid * per_w + c * SC_CHUNK
            pltpu.sync_copy(idx_hbm.at[pl.ds(off, SC_CHUNK)], idx_v)
            pltpu.sync_copy(table_hbm.at[idx_v], rows_v)
            pltpu.sync_copy(rows_v, out_hbm.at[pl.ds(off, SC_CHUNK)])

    return k(table, idx)


def _dest_kernel(offs_ref, ri_ref, d_ref):
    ri = ri_ref[...]
    base = jnp.zeros(ri.shape, F32)
    for e in range(N_EXP):
        base = jnp.where(ri == e, offs_ref[e].astype(F32), base)
    d_ref[...] = (base + pltpu.roll(ri, 6, 0)).astype(jnp.int32)


def _dispatch_plan(ri, cnt, N):
    counts = cnt[:N_EXP, 0].astype(jnp.int32)
    padded = ((counts + TM_E - 1) // TM_E) * TM_E
    ends = jnp.cumsum(padded)
    offs = ends - padded
    take = lambda t, i: t.at[i].get(mode="promise_in_bounds")
    dest = pl.pallas_call(
        _dest_kernel,
        in_specs=[pl.BlockSpec(memory_space=pltpu.SMEM), pl.BlockSpec((8, N), lambda: (0, 0))],
        out_specs=pl.BlockSpec((8, N), lambda: (0, 0)),
        out_shape=jax.ShapeDtypeStruct((8, N), jnp.int32),
        name="dispatch_rows",
    )(offs, ri)
    d1, d2 = dest[0], dest[1]
    P = 2 * N + N_EXP * TM_E
    tile_start = jnp.arange(P // TM_E, dtype=jnp.int32) * TM_E
    tile_expert = jnp.minimum(jnp.sum((ends[None, :] <= tile_start[:, None]).astype(jnp.int32), axis=1), N_EXP - 1)
    tile_valid = jnp.clip(take(offs + counts, tile_expert) - tile_start, 0, TM_E)
    return d1, d2, P, tile_expert, tile_valid


def _combine(h_ref, y1_ref, y2_ref, rg_ref):
    rg = rg_ref[...]
    lo1, hi1 = _unpack_halves(y1_ref[...])
    lo2, hi2 = _unpack_halves(y2_ref[...])
    g1, g2 = rg[:, 0:1], rg[:, 1:2]
    return h_ref[...] + jnp.concatenate([g1 * lo1 + g2 * lo2, g1 * hi1 + g2 * hi2], axis=1)


ODD_WIDTHS = (1024, 1024, 512, 512)


def _combine_proj_odd_kernel(h_ref, y1_ref, y2_ref, rg_ref, g_ref, win_ref, wvt_ref, ind_ref, hn_ref, vt_ref,
                             vdt_ref, *out_refs):
    hn = _combine(h_ref, y1_ref, y2_ref, rg_ref)
    hn_ref[...] = hn
    u = _rms(hn, g_ref[...]).astype(BF16)
    vt_ref[...] = _values_t(_dot_nt(wvt_ref[0:512, :], u)).astype(BF16)
    vdt_ref[...] = _dot_nt(wvt_ref[512:1024, :], u).astype(BF16)
    qc_ref, kc_ref, qd_ref, kd_ref = out_refs
    qc = _spread_heads(_dot(u, win_ref[:, 0:512]), lambda h: 0)
    kc = _spread_heads(_dot(u, win_ref[:, 512:1024]), lambda h: 0)
    for hh in range(8):
        qc_ref[:, 128 * hh:128 * (hh + 1)] = qc[hh].astype(BF16)
        kc_ref[:, 128 * hh:128 * (hh + 1)] = (kc[hh] + ind_ref[...]).astype(BF16)
    qd_ref[...] = _dot(u, win_ref[:, 1024:1536]).astype(BF16)
    kd_ref[...] = _dot(u, win_ref[:, 1536:2048]).astype(BF16)


def _odd_weights(w_in):
    s = HD ** -0.5 * LOG2E
    scale = np.asarray([s] * 512 + [1.0] * 512 + [s] * 512 + [1.0] * 512, np.float32)
    win = jnp.concatenate([w_in[:, 0:1024], w_in[:, 1536:2560]], axis=1) * jnp.asarray(scale)
    wvt = jnp.concatenate([w_in[:, 1024:1536], w_in[:, 2560:3072]], axis=1).T
    return win.astype(BF16), wvt.astype(BF16)


def _block_indicator(T):
    assert T // MOBA_L <= 64
    ind = np.zeros((T, 128), np.float32)
    ind[np.arange(T), 64 + np.arange(T) // MOBA_L] = 1.0
    return jnp.asarray(ind)


def _y_specs(N):
    return [pl.BlockSpec((TM, HALF), lambda i: (i, 0)), pl.BlockSpec((TM, HALF), lambda i: (i + N // TM, 0))]


def _combine_proj_odd(h, y12, rg, g, win, wvt, ind, T):
    N = h.shape[0]
    nt = T // TM
    full = lambda a: pl.BlockSpec(a.shape, lambda i: (0,) * a.ndim)
    row = lambda w: pl.BlockSpec((TM, w), lambda i: (i, 0))
    return pl.pallas_call(
        _combine_proj_odd_kernel,
        grid=(N // TM,),
        in_specs=[row(D)] + _y_specs(N) + [row(128), full(g), full(win), full(wvt),
                  pl.BlockSpec((TM, 128), lambda i: (i % nt, 0))],
        out_specs=[row(D), pl.BlockSpec((None, 8 * VT_ROWS, TM), lambda i: (i // nt, 0, i % nt)),
                   pl.BlockSpec((None, 512, TM), lambda i: (i // nt, 0, i % nt))]
        + [row(w) for w in ODD_WIDTHS],
        out_shape=[jax.ShapeDtypeStruct((N, D), F32), jax.ShapeDtypeStruct((N // T, 8 * VT_ROWS, T), BF16),
                   jax.ShapeDtypeStruct((N // T, 512, T), BF16)]
        + [jax.ShapeDtypeStruct((N, w), BF16) for w in ODD_WIDTHS],
        compiler_params=_cparams(("arbitrary",)),
        name="combine_proj_odd",
    )(h, y12, y12, rg, g, win, wvt, ind)


def _combine_final_kernel(h_ref, y1_ref, y2_ref, rg_ref, g_ref, o_ref):
    o_ref[...] = _rms(_combine(h_ref, y1_ref, y2_ref, rg_ref), g_ref[...])


def _combine_final(h, y12, rg, g):
    N = h.shape[0]
    row = lambda w: pl.BlockSpec((TM, w), lambda i: (i, 0))
    return pl.pallas_call(
        _combine_final_kernel,
        grid=(N // TM,),
        in_specs=[row(D)] + _y_specs(N) + [row(128), pl.BlockSpec((1, D), lambda i: (0, 0))],
        out_specs=row(D),
        out_shape=jax.ShapeDtypeStruct((N, D), F32),
        compiler_params=_cparams(("arbitrary",)),
        name="combine_final",
    )(h, y12, y12, rg, g)


def _moba_kernel(q_ref, k_ref, vt_ref, tb_ref, o_ref, kmean_ref, qa_ref, s_ref, smax_ref, m_ref, acc_ref, *, nkb,
                 nq):
    ia = pl.program_id(2)
    L, tq = MOBA_L, MOBA_TQ

    @pl.when(ia == 0)
    def _():
        kmean_ref[...] = jnp.zeros_like(kmean_ref)
        for n in range(nkb):
            kmean_ref[64 + n:65 + n, :] = jnp.sum(k_ref[n * L:(n + 1) * L, :].astype(F32), axis=0,
                                                  keepdims=True) * (1.0 / L)

    nb = -(-nkb // 8) * 8
    lane = lax.broadcasted_iota(jnp.int32, (tq, 128), 1)
    blk = lax.broadcasted_iota(jnp.int32, (nb, tq), 0)
    half = jnp.where(lax.broadcasted_iota(jnp.int32, (nb, tq), 1) >= L, 1, 0)
    blk_f = blk.astype(F32)
    for t, tile in enumerate((ia, nq - 1 - ia)):
        own = 2 * tile + half
        for j in range(2):
            qj = q_ref[pl.ds(pl.multiple_of(tile * tq, tq), tq), 128 * j:128 * (j + 1)]
            gsc = lax.dot_general(kmean_ref[64:64 + nb, 128 * j:128 * (j + 1)], qj.astype(F32),
                                  (((1,), (1,)), ((), ())), preferred_element_type=F32,
                                  precision=lax.Precision.HIGHEST)
            g = jnp.where(blk < own, gsc, NEG)
            allowed = jnp.where(blk == own, 1.0, 0.0)
            for _ in range(MOBA_TOPK):
                mx = jnp.max(g, axis=0, keepdims=True)
                idx = jnp.min(jnp.where(g == mx, blk_f, 1e6), axis=0, keepdims=True)
                pick = blk_f == idx
                allowed = jnp.maximum(allowed, jnp.where(pick, jnp.where(mx > 0.5 * NEG, 1.0, 0.0), 0.0))
                g = jnp.where(pick, 2.0 * NEG, g)
            mask = jnp.concatenate([jnp.zeros((64, tq), F32), jnp.where(allowed > 0.5, 0.0, NEG),
                                    jnp.zeros((64 - nb, tq), F32)], axis=0)
            qa_ref[t, j] = jnp.where(lane < HD, qj, mask.T.astype(BF16))

    def bias_of(start, q0, j):
        parts = []
        for c in range(tq // L):
            off = jnp.minimum(q0 - start - c * L, BIAS_CONST_FROM) + BIAS_PAD
            parts.append(tb_ref[j, :, pl.ds(pl.multiple_of(off, L), tq)])
        return jnp.concatenate(parts, axis=0)

    _causal_attention_pair_t(ia, nq, lambda t, q0, j: qa_ref[t, j], k_ref, vt_ref, o_ref, s_ref, smax_ref, m_ref,
                             acc_ref, tq=tq, bias_of=bias_of)


def _moba_attention(q, k, vt, tb, B, T):
    tq = MOBA_TQ
    nq = T // tq
    return pl.pallas_call(
        functools.partial(_moba_kernel, nkb=T // MOBA_L, nq=nq),
        grid=(B, 4, nq // 2),
        in_specs=[pl.BlockSpec((T, 256), lambda b, p, i: (b, p)),
                  pl.BlockSpec((T, 256), lambda b, p, i: (b, p)),
                  pl.BlockSpec((None, 2 * VT_ROWS, T), lambda b, p, i: (b, p, 0)),
                  pl.BlockSpec((2, MOBA_L, BIAS_ROWS), lambda b, p, i: (p, 0, 0))],
        out_specs=pl.BlockSpec((T, 128), lambda b, p, i: (b, p)),
        out_shape=jax.ShapeDtypeStruct((B * T, 512), BF16),
        scratch_shapes=[pltpu.VMEM((128, 256), F32), pltpu.VMEM((2, 2, tq, 128), BF16)]
        + _attention_t_scratch(tq),
        compiler_params=_cparams(("arbitrary", "arbitrary", "arbitrary")),
        name="moba_attention",
    )(q, k, vt, tb)


def _sb_kernel(q_ref, k_ref, vt_ref, o_ref, *, tq, pairs):
    i = pl.program_id(2)
    lane = lax.broadcasted_iota(jnp.int32, (tq, 128), 1)
    qs = []
    for p in range(pairs):
        q_pair = q_ref[:, 128 * p:128 * (p + 1)]
        qs += [jnp.where(lane < HD, q_pair, 0).astype(BF16), jnp.where(lane >= HD, q_pair, 0).astype(BF16)]
    key = lax.broadcasted_iota(jnp.int32, (tq, tq), 0)
    qry = lax.broadcasted_iota(jnp.int32, (tq, tq), 1)
    past = key < qry
    suffix = jnp.where(qry >= key, -1.0, 0.0).astype(BF16)
    heads = range(2 * pairs)

    def step(kb, carry, boundary):
        start = pl.multiple_of(kb * tq, tq)
        zs = [_dot_nt(k_ref[pl.ds(start, tq), 128 * (h // 2):128 * (h // 2 + 1)], qs[h]) for h in heads]
        parts = []
        for z in zs:
            neg_abs = pltpu.bitcast(pltpu.bitcast(z, jnp.uint32) | jnp.uint32(0x80000000), F32)
            soft = jnp.maximum(z, 0.0) + jnp.log2(1.0 + jnp.exp2(neg_abs))
            if boundary:
                soft = jnp.where(past, soft, 0.0)
            parts.append(soft.astype(BF16))
        sums = [_dot(suffix, part) for part in parts]
        ws = []
        for h in heads:
            w = jnp.exp2(zs[h] + (carry[h][0] + sums[h]))
            ws.append((jnp.where(past, w, 0.0) if boundary else w).astype(BF16))
        return tuple((carry[h][0] + sums[h][0:1], carry[h][1]
                      + _dot(vt_ref[128 * (h // 2):128 * (h // 2 + 1), pl.ds(start, tq)], ws[h])) for h in heads)

    def live(carry):
        top = carry[0][0]
        for c, _ in carry[1:]:
            top = jnp.maximum(top, c)
        return (jnp.max(top) > SB_DONE).astype(jnp.int32)

    def body(state):
        t, _, carry = state
        carry = step(i - 1 - t, carry, False)
        return t + 1, live(carry), carry

    init = tuple((jnp.zeros((1, tq), F32), jnp.zeros((128, tq), F32)) for _ in range(2 * pairs))
    carry = step(i, init, True)
    _, _, carry = lax.while_loop(lambda st: (st[0] < i) & (st[1] > 0), body, (jnp.int32(0), live(carry), carry))
    rows = lax.broadcasted_iota(jnp.int32, (128, tq), 0)
    for p in range(pairs):
        ot = jnp.where(rows < HD, carry[2 * p][1], carry[2 * p + 1][1])
        o_ref[:, 128 * p:128 * (p + 1)] = ot.T.astype(BF16)


def _sb_attention(q, k, vt, B, T, tq=256, pairs=4):
    nq = T // tq
    w = 128 * pairs
    return pl.pallas_call(
        functools.partial(_sb_kernel, tq=tq, pairs=pairs),
        grid=(B, 4 // pairs, nq),
        in_specs=[pl.BlockSpec((tq, w), lambda b, p, i: (b * nq + i, p)),
                  pl.BlockSpec((T, w), lambda b, p, i: (b, p)),
                  pl.BlockSpec((None, w, T), lambda b, p, i: (b, p, 0))],
        out_specs=pl.BlockSpec((tq, w), lambda b, p, i: (b * nq + i, p)),
        out_shape=jax.ShapeDtypeStruct((B * T, 512), BF16),
        compiler_params=_cparams(("arbitrary", "arbitrary", "arbitrary")),
        name="sb_attention",
    )(q, k, vt)


def _router_weights(w_group, b_group, w_router, b_router):
    pad = 128 - N_EXP - N_GROUPS
    wr = jnp.concatenate([w_router, w_group, jnp.zeros((D, pad), F32)], axis=1).T
    hi = wr.astype(BF16)
    lo = (wr - hi.astype(F32)).astype(BF16)
    br = jnp.concatenate([b_router, b_group, jnp.zeros((pad,), F32)])[:, None]
    return jnp.concatenate([hi, lo], axis=0), br


def _ffn(h, oa, ob, wo, g, w_group, b_group, w_router, b_router, w_gate, w_up, w_down, layer):
    wr, br = _router_weights(w_group, b_group, w_router, b_router)
    hn, u, ri, rg, cnt = _outproj_router(h, oa, ob, wo.astype(BF16), g[None, :], wr, br)
    d1, d2, P, tile_expert, tile_valid = _dispatch_plan(ri, cnt, h.shape[0])
    xs = _sc_scatter_rows(u, d1, d2, P)
    ys = _moe_experts(xs, tile_expert, tile_valid, w_gate, w_up, w_down, layer)
    return hn, _sc_gather_rows(ys, jnp.concatenate([d1, d2])), rg


def kernel(x, norm_mix, norm_ffn, norm_final, rel_bias_table, w_in_even, g_mla_q, g_mla_kv, w_mla_qb, w_mla_kvb, swa_sinks, w_out_even, w_in_odd, w_out_odd, moe_w_group, moe_b_group, moe_w_router, moe_b_router, moe_w_gate, moe_w_up, moe_w_down):
    B, T, _ = x.shape
    h = x.reshape(B * T, D)
    tb_swa, tb_moba = _bias_tables(rel_bias_table)

    win, wqa, wqb, wk, wv = _even_weights(w_in_even[0], w_mla_qb[0], w_mla_kvb[0])
    q, k, v, qs8, ks, vst = _proj_even(h, norm_mix[0][None, :], win, g_mla_q[0][None, :], g_mla_kv[0][None, :],
                                       wqa, wqb, wk, wv, _rope_tables(T), T)
    oa = _mla_attention(q, k, v, B, T)
    ob = _swa_attention(qs8, ks, vst, swa_sinks[0], tb_swa, B, T)
    h, y12, rg = _ffn(h, oa, ob, w_out_even[0], norm_ffn[0], moe_w_group[0], moe_b_group[0], moe_w_router[0],
                      moe_b_router[0], moe_w_gate, moe_w_up, moe_w_down, 0)

    h, vct, vdt, qc, kc, qd, kd = _combine_proj_odd(h, y12, rg, norm_mix[1][None, :], *_odd_weights(w_in_odd[0]),
                                                   _block_indicator(T), T)
    oc = _moba_attention(qc, kc, vct, tb_moba, B, T)
    od = _sb_attention(qd, kd, vdt, B, T)
    h, y12, rg = _ffn(h, oc, od, w_out_odd[0], norm_ffn[1], moe_w_group[1], moe_b_group[1], moe_w_router[1],
                      moe_b_router[1], moe_w_gate, moe_w_up, moe_w_down, 1)
    out = _combine_final(h, y12, rg, norm_final[None, :])
    return out.reshape(B, T, D)
```

```python
import functools
import math

import numpy as np
import jax
import jax.numpy as jnp
from jax import lax
from jax.experimental import pallas as pl
from jax.experimental.pallas import tpu as pltpu
from jax.experimental.pallas import tpu_sc as plsc

F32 = jnp.float32
BF16 = jnp.bfloat16
NEG = -1e30
EPS = 1e-6

D = 1024
HD = 64
MLA_HEADS, MLA_QR, MLA_KVR, MLA_NOPE, MLA_ROPE, MLA_V = 8, 256, 128, 64, 32, 64
ROPE_THETA = 10000.0
SWA_HEADS, SWA_KV, SWA_WIN = 8, 2, 128
MOBA_L, MOBA_TOPK = 256, 3
REL_BUCKETS, REL_MAX = 32, 2048
N_GROUPS, EPG, N_EXP, D_EXP = 4, 8, 32, 256
BIAS_CONST_FROM = 1792
MOBA_TQ = 2 * MOBA_L
BIAS_PAD = MOBA_L
BIAS_ROWS = BIAS_PAD + BIAS_CONST_FROM + MOBA_TQ
SB_DONE = -160.0
TM = 1024
TM_E = 512
VMEM_LIMIT = 56 * 1024 * 1024
SC_CORES, SC_SUBCORES = 2, 16
SC_CHUNK = 128
HALF = D // 2
VT_ROWS = HD + 16
LOG2E = math.log2(math.e)


def _pack_halves(x):
    lo = pltpu.bitcast(x[:, :HALF].astype(BF16).astype(F32), jnp.uint32) >> 16
    hi = pltpu.bitcast(x[:, HALF:].astype(BF16).astype(F32), jnp.uint32) & jnp.uint32(0xFFFF0000)
    return pltpu.bitcast(lo | hi, jnp.int32)


def _unpack_halves(w):
    w = pltpu.bitcast(w, jnp.uint32)
    return pltpu.bitcast(w << 16, F32), pltpu.bitcast(w & jnp.uint32(0xFFFF0000), F32)


def _cparams(sem):
    return pltpu.CompilerParams(dimension_semantics=sem, vmem_limit_bytes=VMEM_LIMIT)


def _dot(a, b):
    return jnp.dot(a, b, preferred_element_type=F32)


def _dot_nt(a, b):
    return lax.dot_general(a, b, (((1,), (1,)), ((), ())), preferred_element_type=F32)


def _rms(x, g):
    return x * lax.rsqrt(jnp.mean(x * x, axis=-1, keepdims=True) + EPS) * g


def _values_t(vt):
    ones = jnp.ones((VT_ROWS - HD, vt.shape[1]), vt.dtype)
    return jnp.concatenate([blk for h in range(8) for blk in (vt[HD * h:HD * (h + 1)], ones)], axis=0)


def _spread_heads(x, lane_of):
    lane = lax.broadcasted_iota(jnp.int32, (x.shape[0], 128), 1)
    out = []
    for h in range(8):
        pair = x[:, 128 * (h // 2):128 * (h // 2 + 1)]
        if HD * (h % 2) != lane_of(h):
            pair = pltpu.roll(pair, HD, 1)
        out.append(jnp.where((lane >= lane_of(h)) & (lane < lane_of(h) + HD), pair, 0.0))
    return out


def _t5_bucket_np(dist):
    n = np.maximum(dist, 0)
    max_exact = REL_BUCKETS // 2
    nf = np.maximum(n, 1).astype(np.float32)
    large = max_exact + (np.log(nf / np.float32(max_exact)) / np.float32(math.log(REL_MAX / max_exact))
                         * np.float32(REL_BUCKETS - max_exact)).astype(np.int32)
    large = np.minimum(large, REL_BUCKETS - 1)
    return np.where(n < max_exact, n, large).astype(np.int32)


def _bias_kernel(table_ref, bucket_ref, out_ref, *, scale, ranges):
    b = bucket_ref[...]
    for m, (lo, hi) in enumerate(ranges):
        @pl.when(pl.program_id(0) == m)
        def _():
            for h in range(8):
                acc = jnp.full(b.shape, NEG if lo == REL_BUCKETS else table_ref[lo, h] * scale, F32)
                for k in range(lo + 1, min(hi, REL_BUCKETS - 1) + 1):
                    acc = jnp.where(b == k, table_ref[k, h] * scale, acc)
                if hi == REL_BUCKETS:
                    acc = jnp.where(b == REL_BUCKETS, NEG, acc)
                out_ref[h] = acc


def _bias_lookup(rel_table, dist, scale, name, masked=None):
    R, C = dist.shape
    buckets = _t5_bucket_np(dist)
    if masked is not None:
        buckets = np.where(masked, REL_BUCKETS, buckets).astype(np.int32)
    ranges = tuple((int(buckets[:, c:c + 256].min()), int(buckets[:, c:c + 256].max())) for c in range(0, C, 256))
    return pl.pallas_call(
        functools.partial(_bias_kernel, scale=scale, ranges=ranges),
        grid=(C // 256,),
        in_specs=[pl.BlockSpec(memory_space=pltpu.SMEM),
                  pl.BlockSpec((R, 256), lambda m: (0, m))],
        out_specs=pl.BlockSpec((8, R, 256), lambda m: (0, 0, m)),
        out_shape=jax.ShapeDtypeStruct((8, R, C), F32),
        compiler_params=_cparams(("arbitrary",)),
        name=name,
    )(rel_table, jnp.asarray(buckets))


def _bias_tables(rel_table):
    d = np.arange(256)[None, :] - np.arange(256)[:, None]
    swa = _bias_lookup(rel_table, d, LOG2E, "bias_swa", masked=(d < 0) | (d >= SWA_WIN))
    moba = _bias_lookup(rel_table, np.arange(BIAS_ROWS)[None, :] - BIAS_PAD - np.arange(MOBA_L)[:, None], LOG2E,
                        "bias_moba")
    return swa, moba


def _proj_even_kernel(h_ref, g_ref, win_ref, gq_ref, gkv_ref, wqa_ref, wqb_ref, wk_ref, wv_ref,
                      cq_ref, sq_ref, ck_ref, sk_ref,
                      q_ref, k_ref, v_ref, qs_ref, ks_ref, vst_ref):
    u = _rms(h_ref[...], g_ref[...]).astype(BF16)
    nq = _rms(_dot(u, win_ref[:, 0:256]), gq_ref[...]).astype(BF16)
    qa = _dot(nq, wqa_ref[...])
    qb = _dot(nq, wqb_ref[...])
    cq, sq = cq_ref[...], sq_ref[...]
    for hh in range(MLA_HEADS):
        sl = slice(128 * hh, 128 * (hh + 1))
        q_ref[:, sl] = (qa[:, sl] * cq + qb[:, sl] * sq).astype(BF16)
    nkv = _rms(_dot(u, win_ref[:, 256:384]), gkv_ref[...]).astype(BF16)
    kn = _dot(nkv, wk_ref[...])
    v_ref[...] = _values_t(_dot_nt(wv_ref[...], nkv)).astype(BF16)
    rest = _dot(u, win_ref[:, 896:1408])
    kr = rest[:, 128:256] * ck_ref[...] + rest[:, 256:384] * sk_ref[...]
    for hh in range(MLA_HEADS):
        sl = slice(128 * hh, 128 * (hh + 1))
        k_ref[:, sl] = (kn[:, sl] + kr).astype(BF16)
    qs = _spread_heads(_dot(u, win_ref[:, 384:896]), lambda hq: HD * (hq // (SWA_HEADS // SWA_KV)))
    for hq in range(SWA_HEADS):
        qs_ref[:, 128 * hq:128 * (hq + 1)] = qs[hq].astype(BF16)
    ks_ref[...] = rest[:, 0:128].astype(BF16)
    vst_ref[...] = rest[:, 384:512].T.astype(BF16)


def _cols(w, idx, scale=None):
    idx = np.asarray(idx)
    cuts = [0] + [n for n in range(1, len(idx)) if (idx[n] < 0) != (idx[n - 1] < 0)
                  or (idx[n] >= 0 and idx[n] != idx[n - 1] + 1)] + [len(idx)]
    parts = [jnp.zeros((w.shape[0], b - a), w.dtype) if idx[a] < 0 else w[:, idx[a]:idx[a] + b - a]
             for a, b in zip(cuts[:-1], cuts[1:])]
    out = jnp.concatenate(parts, axis=1)
    if scale is not None:
        out = out * scale
    return out


def _even_weights(w_in, w_qb, w_kvb):
    o_cq, o_ckv, o_kr = 0, MLA_QR, MLA_QR + MLA_KVR
    o_qs = o_kr + MLA_ROPE
    o_ks = o_qs + SWA_HEADS * HD
    o_vs = o_ks + SWA_KV * HD
    idx = list(range(o_cq, o_cq + MLA_QR)) + list(range(o_ckv, o_ckv + MLA_KVR))
    scale = [1.0] * len(idx)
    idx += list(range(o_qs, o_qs + SWA_HEADS * HD))
    scale += [HD ** -0.5 * LOG2E] * (SWA_HEADS * HD)
    idx += list(range(o_ks, o_ks + 128))
    scale += [1.0] * 128
    half = MLA_ROPE // 2
    idx += [-1] * 64 + list(range(o_kr, o_kr + MLA_ROPE)) + [-1] * 32
    idx += [-1] * 64 + list(range(o_kr + half, o_kr + MLA_ROPE)) + list(range(o_kr, o_kr + half)) + [-1] * 32
    idx += list(range(o_vs, o_vs + 128))
    scale += [1.0] * 384
    win = _cols(w_in, idx, jnp.asarray(np.asarray(scale, np.float32))).astype(BF16)
    qw = MLA_NOPE + MLA_ROPE
    ia, ib, ik, iv = [], [], [], []
    for h in range(MLA_HEADS):
        ia += list(range(qw * h, qw * h + qw)) + [-1] * 32
        ib += [-1] * 64 + list(range(qw * h + MLA_NOPE + half, qw * h + qw)) \
            + list(range(qw * h + MLA_NOPE, qw * h + MLA_NOPE + half)) + [-1] * 32
        ik += list(range(128 * h, 128 * h + MLA_NOPE)) + [-1] * 64
        iv += list(range(128 * h + MLA_NOPE, 128 * h + 128))
    return (win, _cols(w_qb, ia).astype(BF16), _cols(w_qb, ib).astype(BF16),
            _cols(w_kvb, ik).astype(BF16), _cols(w_kvb, iv).T.astype(BF16))


def _rope_tables(T):
    f32 = np.float32
    freqs = f32(ROPE_THETA) ** (-np.arange(0, MLA_ROPE, 2, dtype=f32) / f32(MLA_ROPE))
    ang = np.arange(T, dtype=f32)[:, None] * freqs[None, :]
    cos, sin = np.cos(ang), np.sin(ang)
    z = lambda n: np.zeros((T, n), f32)
    scale = f32((MLA_NOPE + MLA_ROPE) ** -0.5 * LOG2E)
    ck = np.concatenate([z(64), cos, cos, z(32)], axis=1)
    sk = np.concatenate([z(64), -sin, sin, z(32)], axis=1)
    cq = np.concatenate([np.ones((T, 64), f32), cos, cos, z(32)], axis=1) * scale
    sq = sk * scale
    return tuple(jnp.asarray(t, F32) for t in (cq, sq, ck, sk))


def _proj_even(h, g, win, gq, gkv, wqa, wqb, wk, wv, tables, T):
    N = h.shape[0]
    nt = T // TM
    full = lambda a: pl.BlockSpec(a.shape, lambda i: (0,) * a.ndim)
    tab = pl.BlockSpec((TM, 128), lambda i: (i % nt, 0))
    row = lambda w: pl.BlockSpec((TM, w), lambda i: (i, 0))
    outs = (1024, 1024, -8 * VT_ROWS, 1024, 128, -128)
    spec = lambda w: row(w) if w > 0 else pl.BlockSpec((None, -w, TM), lambda i: (i // nt, 0, i % nt))
    shape = lambda w: jax.ShapeDtypeStruct((N, w) if w > 0 else (N // T, -w, T), BF16)
    return pl.pallas_call(
        _proj_even_kernel,
        grid=(N // TM,),
        in_specs=[row(D), full(g), full(win), full(gq), full(gkv), full(wqa), full(wqb), full(wk), full(wv),
                  tab, tab, tab, tab],
        out_specs=[spec(w) for w in outs],
        out_shape=[shape(w) for w in outs],
        compiler_params=_cparams(("arbitrary",)),
        name="proj_even",
    )(h, g, win, gq, gkv, wqa, wqb, wk, wv, *tables)


def _finish_t(accs):
    ot = jnp.concatenate([acc[:HD] / acc[HD:HD + 1] for acc in accs], axis=0)
    return ot.T.astype(BF16)


def _causal_attention_pair_t(ia, nq, q_of, k_ref, vt_ref, o_ref, s_ref, smax_ref, m_ref, acc_ref, *, tq,
                             bias_of=None):
    nh = s_ref.shape[1]
    first_query = (ia * tq, (nq - 1 - ia) * tq)
    n_items = nq + 1
    assert n_items % 2 == 1
    key = lax.broadcasted_iota(jnp.int32, (tq, tq), 0)
    qry = lax.broadcasted_iota(jnp.int32, (tq, tq), 1)

    def item(p):
        if isinstance(p, int) and p < 2:
            return p, first_query[p], first_query[p]
        t = jnp.where(p - 2 >= ia, 1, 0)
        return t, jnp.where(t == 1, first_query[1], first_query[0]), (p - 2 - t * ia) * tq

    def scores_to(slot, p):
        t, q0, start = item(p)
        start = pl.multiple_of(start, tq)
        for j in range(nh):
            st = _dot_nt(k_ref[pl.ds(start, tq), 128 * j:128 * (j + 1)], q_of(t, q0, j))
            if bias_of is not None:
                st = st + bias_of(start, q0, j)
            if isinstance(p, int) and p < 2:
                st = jnp.where(key <= qry, st, NEG)
            s_ref[slot, j] = st
            smax_ref[slot, j] = jnp.max(st, axis=0, keepdims=True)

    def update_from(slot, p):
        t, _, start = item(p)
        start = pl.multiple_of(start, tq)
        m_new = [jnp.maximum(m_ref[t, j], smax_ref[slot, j]) for j in range(nh)]
        pts = [jnp.exp2(s_ref[slot, j] - m_new[j]).astype(BF16) for j in range(nh)]
        for j in range(nh):
            acc_ref[t, j] = (jnp.exp2(m_ref[t, j] - m_new[j]) * acc_ref[t, j]
                             + _dot(vt_ref[VT_ROWS * j:VT_ROWS * (j + 1), pl.ds(start, tq)], pts[j]))
            m_ref[t, j] = m_new[j]

    def pair(k):
        scores_to(1, k + 1)
        update_from(0, k)
        scores_to(0, k + 2)
        update_from(1, k + 1)

    def quad(c, _):
        pair(2 + 4 * c)
        pair(4 + 4 * c)
        return 0

    m_ref[...] = jnp.full(m_ref.shape, NEG, F32)
    acc_ref[...] = jnp.zeros(acc_ref.shape, F32)
    scores_to(0, 0)
    pair(0)
    later_pairs = (n_items - 1) // 2 - 1
    lax.fori_loop(0, later_pairs // 2, quad, 0)
    if later_pairs % 2:
        pair(n_items - 3)
    update_from(0, n_items - 1)
    for t in range(2):
        o_ref[pl.ds(pl.multiple_of(first_query[t], tq), tq), :] = _finish_t([acc_ref[t, j] for j in range(nh)])


def _attention_t_scratch(tq, nh=2):
    return [pltpu.VMEM((2, nh, tq, tq), F32), pltpu.VMEM((2, nh, 1, tq), F32), pltpu.VMEM((2, nh, 1, tq), F32),
            pltpu.VMEM((2, nh, VT_ROWS, tq), F32)]


def _mla_kernel(q_ref, k_ref, vt_ref, o_ref, s_ref, smax_ref, m_ref, acc_ref, *, tq, nq):
    def q_of(t, q0, j):
        return q_ref[pl.ds(pl.multiple_of(q0, tq), tq), 128 * j:128 * (j + 1)]

    _causal_attention_pair_t(pl.program_id(2), nq, q_of, k_ref, vt_ref, o_ref, s_ref, smax_ref, m_ref, acc_ref,
                             tq=tq)


def _mla_attention(q, k, vt, B, T, tq=512):
    nq = T // tq
    return pl.pallas_call(
        functools.partial(_mla_kernel, tq=tq, nq=nq),
        grid=(B, MLA_HEADS // 2, nq // 2),
        in_specs=[pl.BlockSpec((T, 256), lambda b, p, i: (b, p)),
                  pl.BlockSpec((T, 256), lambda b, p, i: (b, p)),
                  pl.BlockSpec((None, 2 * VT_ROWS, T), lambda b, p, i: (b, p, 0))],
        out_specs=pl.BlockSpec((T, 128), lambda b, p, i: (b, p)),
        out_shape=jax.ShapeDtypeStruct((B * T, MLA_HEADS * MLA_V), BF16),
        scratch_shapes=_attention_t_scratch(tq),
        compiler_params=_cparams(("arbitrary", "arbitrary", "arbitrary")),
        name="mla_attention",
    )(q, k, vt)


def _swa_kernel(sink_ref, q_ref, k_ref, vt_ref, tb_ref, o_ref, *, tq):
    i = pl.program_id(1)
    G = SWA_HEADS // SWA_KV
    W = 2 * SWA_WIN
    head = lax.broadcasted_iota(jnp.int32, (1, G * 128), 1) >> 7
    sinks = []
    for c in range(SWA_KV):
        sink = jnp.zeros((1, G * 128), F32)
        for g in range(G):
            sink = jnp.where(head == g, sink_ref[G * c + g], sink)
        sinks.append(sink)
    scores, values = [], []
    for r in range(tq // 128):
        qstart = i * tq + 128 * r
        kstart = pl.multiple_of(jnp.maximum(qstart - SWA_WIN, 0), 128)
        off = pl.multiple_of(qstart - kstart, 128)
        kw = k_ref[pl.ds(kstart, W), :]
        for c in range(SWA_KV):
            qg = jnp.concatenate([q_ref[128 * r:128 * (r + 1), 128 * (G * c + g):128 * (G * c + g + 1)]
                                  for g in range(G)], axis=0)
            bias = jnp.concatenate([tb_ref[G * c + g, :, pl.ds(off, 128)] for g in range(G)], axis=1)
            scores.append(_dot_nt(kw, qg) + bias)
            values.append(vt_ref[HD * c:HD * (c + 1), pl.ds(kstart, W)])
    probs = []
    for n, s in enumerate(scores):
        sink = sinks[n % SWA_KV]
        m = jnp.maximum(jnp.max(s, axis=0, keepdims=True), sink)
        p = jnp.exp2(s - m)
        probs.append((p.astype(BF16), jnp.sum(p, axis=0, keepdims=True) + jnp.exp2(sink - m)))
    for r in range(tq // 128):
        outs = []
        for c in range(SWA_KV):
            p, denom = probs[SWA_KV * r + c]
            ot = _dot(values[SWA_KV * r + c], p) / denom
            outs += [ot[:, 128 * g:128 * (g + 1)] for g in range(G)]
        o_ref[128 * r:128 * (r + 1), :] = jnp.concatenate(outs, axis=0).T.astype(BF16)


def _swa_attention(q8, k, vt, sinks, tb, B, T, tq=512):
    nq = T // tq
    return pl.pallas_call(
        functools.partial(_swa_kernel, tq=tq),
        grid=(B, nq),
        in_specs=[pl.BlockSpec(memory_space=pltpu.SMEM),
                  pl.BlockSpec((tq, 1024), lambda b, i: (b * nq + i, 0)),
                  pl.BlockSpec((T, 128), lambda b, i: (b, 0)),
                  pl.BlockSpec((None, 128, T), lambda b, i: (b, 0, 0)),
                  pl.BlockSpec((8, 256, 256), lambda b, i: (0, 0, 0))],
        out_specs=pl.BlockSpec((tq, 512), lambda b, i: (b * nq + i, 0)),
        out_shape=jax.ShapeDtypeStruct((B * T, SWA_HEADS * HD), BF16),
        compiler_params=_cparams(("arbitrary", "arbitrary")),
        name="swa_attention",
    )(sinks * LOG2E, q8, k, vt, tb)


def _outproj_router_kernel(h_ref, oa_ref, ob_ref, wo_ref, g_ref, wr_ref, br_ref,
                           hn_ref, u_ref, ri_ref, rg_ref, cnt_ref, base_ref):
    i = pl.program_id(0)

    @pl.when(i == 0)
    def _():
        base_ref[...] = jnp.zeros_like(base_ref)

    hn = h_ref[...] + _dot(oa_ref[...], wo_ref[0:512, :]) + _dot(ob_ref[...], wo_ref[512:1024, :])
    hn_ref[...] = hn
    u = _rms(hn, g_ref[...])
    u_ref[...] = _pack_halves(u)
    u_hi = u.astype(BF16)
    u_lo = (u - u_hi.astype(F32)).astype(BF16)
    both = _dot_nt(wr_ref[...], u_hi)
    logits = both[:128] + both[128:] + _dot_nt(wr_ref[0:128, :], u_lo) + br_ref[...]
    tm = logits.shape[1]
    sub = lax.broadcasted_iota(jnp.int32, (8, tm), 0).astype(F32)
    big = 1e6
    isg = sub < N_GROUPS
    gl = jnp.where(isg, logits[N_EXP:N_EXP + 8], NEG)
    gmax = jnp.max(gl, axis=0, keepdims=True)
    gsel = jnp.min(jnp.where(gl == gmax, sub, big), axis=0, keepdims=True)
    g_gate = 1.0 / jnp.sum(jnp.where(isg, jnp.exp(gl - gmax), 0.0), axis=0, keepdims=True)
    el = jnp.zeros((EPG, tm), F32)
    for grp in range(N_GROUPS):
        el = jnp.where(gsel == grp, logits[EPG * grp:EPG * (grp + 1)], el)
    m1 = jnp.max(el, axis=0, keepdims=True)
    i1 = jnp.min(jnp.where(el == m1, sub, big), axis=0, keepdims=True)
    el2 = jnp.where(sub == i1, NEG, el)
    m2 = jnp.max(el2, axis=0, keepdims=True)
    i2 = jnp.min(jnp.where(el2 == m2, sub, big), axis=0, keepdims=True)
    e1 = gsel * EPG + i1
    e2 = gsel * EPG + i2
    r = jnp.exp(m2 - m1)
    gate1 = g_gate / (1.0 + r)
    gate2 = g_gate * r / (1.0 + r)
    rows = lax.broadcasted_iota(jnp.int32, (128, tm), 0).astype(F32)
    oh1 = (rows == e1).astype(F32)
    oh2 = (rows == e2).astype(F32)
    oh = oh1 + oh2
    t_row = lax.broadcasted_iota(jnp.int32, (tm, tm), 0)
    t_col = lax.broadcasted_iota(jnp.int32, (tm, tm), 1)
    earlier = jnp.where(t_row < t_col, 1.0, 0.0).astype(BF16)
    prefix = _dot(oh.astype(BF16), earlier) + base_ref[:, 0:1]
    rank1 = jnp.sum(oh1 * prefix, axis=0, keepdims=True)
    rank2 = jnp.sum(oh2 * prefix, axis=0, keepdims=True)
    base_ref[...] = base_ref[...] + jnp.sum(oh, axis=1, keepdims=True)
    cnt_ref[...] = base_ref[...]
    ri_ref[...] = jnp.where(sub == 0, e1, jnp.where(sub == 1, e2, jnp.where(sub == 2, rank1,
                            jnp.where(sub == 3, rank2, 0.0))))
    rg_ref[...] = jnp.where(rows == 0, gate1, jnp.where(rows == 1, gate2, 0.0)).T


def _outproj_router(h, oa, ob, wo, g, wr, br):
    N = h.shape[0]
    full = lambda a: pl.BlockSpec(a.shape, lambda i: (0,) * a.ndim)
    row = lambda w: pl.BlockSpec((TM, w), lambda i: (i, 0))
    return pl.pallas_call(
        _outproj_router_kernel,
        grid=(N // TM,),
        in_specs=[row(D), row(512), row(512), full(wo), full(g), full(wr), full(br)],
        out_specs=[row(D), row(HALF), pl.BlockSpec((8, TM), lambda i: (0, i)), row(128),
                   pl.BlockSpec((128, 128), lambda i: (0, 0))],
        out_shape=[jax.ShapeDtypeStruct((N, D), F32), jax.ShapeDtypeStruct((N, HALF), jnp.int32),
                   jax.ShapeDtypeStruct((8, N), F32), jax.ShapeDtypeStruct((N, 128), F32),
                   jax.ShapeDtypeStruct((128, 128), F32)],
        scratch_shapes=[pltpu.VMEM((128, 128), F32)],
        compiler_params=_cparams(("arbitrary",)),
        name="outproj_router",
    )(h, oa, ob, wo, g, wr, br)


def _moe_kernel(te_ref, tv_ref, run_ref, nx_ref, x_ref, wg_hbm, wu_hbm, wd_hbm, o_ref,
                wg_buf, wu_buf, wd_buf, sems, wgu_s, wd_s, *, layer):
    i = pl.program_id(0)
    valid = tv_ref[i]

    def weight_copies(e, slot):
        return (pltpu.make_async_copy(wg_hbm.at[layer, e], wg_buf.at[slot], sems.at[slot, 0]),
                pltpu.make_async_copy(wu_hbm.at[layer, e], wu_buf.at[slot], sems.at[slot, 1]),
                pltpu.make_async_copy(wd_hbm.at[layer, e], wd_buf.at[slot], sems.at[slot, 2]))

    @pl.when(run_ref[i] > 0)
    def _():
        slot = run_ref[i] - 1

        @pl.when(i == 0)
        def _():
            for c in weight_copies(te_ref[i], slot):
                c.start()

        for c in weight_copies(te_ref[i], slot):
            c.wait()
        wgu_s[:, :D_EXP] = wg_buf[slot].astype(BF16)
        wgu_s[:, D_EXP:] = wu_buf[slot].astype(BF16)
        wd_s[...] = wd_buf[slot].astype(BF16)

        @pl.when(nx_ref[i] >= 0)
        def _():
            for c in weight_copies(nx_ref[i], 1 - slot):
                c.start()

    @pl.when(valid > 0)
    def _():
        rows = lax.broadcasted_iota(jnp.int32, x_ref.shape, 0)
        lo, hi = _unpack_halves(jnp.where(rows < valid, x_ref[...], 0))
        hgu = _dot(lo.astype(BF16), wgu_s[:HALF, :]) + _dot(hi.astype(BF16), wgu_s[HALF:, :])
        hg, hu = hgu[:, :D_EXP], hgu[:, D_EXP:]
        a = hg * (1.0 / (1.0 + jnp.exp(-hg))) * hu
        o_ref[...] = _pack_halves(_dot(a.astype(BF16), wd_s[...]))

    @pl.when(valid <= 0)
    def _():
        o_ref[...] = jnp.zeros_like(o_ref)


def _moe_experts(xs, plan, w_gate, w_up, w_down, layer):
    P = xs.shape[0]
    hbm = pl.BlockSpec(memory_space=pl.ANY)
    grid_spec = pltpu.PrefetchScalarGridSpec(
        num_scalar_prefetch=4,
        grid=(P // TM_E,),
        in_specs=[pl.BlockSpec((TM_E, HALF), lambda i, *_: (i, 0)), hbm, hbm, hbm],
        out_specs=pl.BlockSpec((TM_E, HALF), lambda i, *_: (i, 0)),
        scratch_shapes=[pltpu.VMEM((2, D, D_EXP), F32), pltpu.VMEM((2, D, D_EXP), F32),
                        pltpu.VMEM((2, D_EXP, D), F32), pltpu.SemaphoreType.DMA((2, 3)),
                        pltpu.VMEM((D, 2 * D_EXP), BF16), pltpu.VMEM((D_EXP, D), BF16)],
    )
    return pl.pallas_call(
        functools.partial(_moe_kernel, layer=layer),
        grid_spec=grid_spec,
        out_shape=jax.ShapeDtypeStruct((P, HALF), jnp.int32),
        compiler_params=_cparams(("arbitrary",)),
        name="moe_experts",
    )(*plan, xs, w_gate, w_up, w_down)


def _sc_mesh():
    return plsc.VectorSubcoreMesh(core_axis_name="c", subcore_axis_name="s",
                                  num_cores=SC_CORES, num_subcores=SC_SUBCORES)


def _sc_scatter_rows(src, d1, d2, P):
    N, W = src.shape
    per_w = N // (SC_CORES * SC_SUBCORES)

    @functools.partial(pl.kernel, mesh=_sc_mesh(), out_type=jax.ShapeDtypeStruct((P, W), src.dtype),
                       scratch_types=[pltpu.VMEM((SC_CHUNK,), jnp.int32), pltpu.VMEM((SC_CHUNK, W), src.dtype)],
                       name="sc_scatter_rows")
    def k(src_hbm, d1_hbm, d2_hbm, out_hbm, idx_v, rows_v):
        wid = lax.axis_index("s") * SC_CORES + lax.axis_index("c")

        @pl.loop(0, per_w // SC_CHUNK)
        def _(c):
            off = wid * per_w + c * SC_CHUNK
            pltpu.sync_copy(src_hbm.at[pl.ds(off, SC_CHUNK)], rows_v)
            pltpu.sync_copy(d1_hbm.at[pl.ds(off, SC_CHUNK)], idx_v)
            pltpu.sync_copy(rows_v, out_hbm.at[idx_v])
            pltpu.sync_copy(d2_hbm.at[pl.ds(off, SC_CHUNK)], idx_v)
            pltpu.sync_copy(rows_v, out_hbm.at[idx_v])

    return k(src, d1, d2)


def _sc_gather_rows(table, idx):
    B, W = idx.shape[0], table.shape[1]
    per_w = B // (SC_CORES * SC_SUBCORES)

    @functools.partial(pl.kernel, mesh=_sc_mesh(), out_type=jax.ShapeDtypeStruct((B, W), table.dtype),
                       scratch_types=[pltpu.VMEM((SC_CHUNK,), jnp.int32), pltpu.VMEM((SC_CHUNK, W), table.dtype)],
                       name="sc_gather_rows")
    def k(table_hbm, idx_hbm, out_hbm, idx_v, rows_v):
        wid = lax.axis_index("s") * SC_CORES + lax.axis_index("c")

        @pl.loop(0, per_w // SC_CHUNK)
        def _(c):
            off = wid * per_w + c * SC_CHUNK
            pltpu.sync_copy(idx_hbm.at[pl.ds(off, SC_CHUNK)], idx_v)
            pltpu.sync_copy(table_hbm.at[idx_v], rows_v)
            pltpu.sync_copy(rows_v, out_hbm.at[pl.ds(off, SC_CHUNK)])

    return k(table, idx)


def _dest_kernel(offs_ref, ri_ref, d_ref):
    ri = ri_ref[...]
    base = jnp.zeros(ri.shape, F32)
    for e in range(N_EXP):
        base = jnp.where(ri == e, offs_ref[e].astype(F32), base)
    d_ref[...] = (base + pltpu.roll(ri, 6, 0)).astype(jnp.int32)


def _dispatch_plan(ri, cnt, N):
    counts = cnt[:N_EXP, 0].astype(jnp.int32)
    padded = ((counts + TM_E - 1) // TM_E) * TM_E
    ends = jnp.cumsum(padded)
    offs = ends - padded
    take = lambda t, i: t.at[i].get(mode="promise_in_bounds")
    dest = pl.pallas_call(
        _dest_kernel,
        in_specs=[pl.BlockSpec(memory_space=pltpu.SMEM), pl.BlockSpec((8, N), lambda: (0, 0))],
        out_specs=pl.BlockSpec((8, N), lambda: (0, 0)),
        out_shape=jax.ShapeDtypeStruct((8, N), jnp.int32),
        name="dispatch_rows",
    )(offs, ri)
    d1, d2 = dest[0], dest[1]
    P = 2 * N + N_EXP * TM_E
    tile_start = jnp.arange(P // TM_E, dtype=jnp.int32) * TM_E
    tile_expert = jnp.minimum(jnp.sum((ends[None, :] <= tile_start[:, None]).astype(jnp.int32), axis=1), N_EXP - 1)
    tile_valid = jnp.clip(take(offs + counts, tile_expert) - tile_start, 0, TM_E)
    n_tiles = P // TM_E
    tile = jnp.arange(n_tiles, dtype=jnp.int32)
    starts = (tile_valid > 0) & ((tile == 0) | (tile_expert != jnp.roll(tile_expert, 1)))
    slot = (jnp.cumsum(starts.astype(jnp.int32)) - 1) % 2
    run_code = jnp.where(starts, slot + 1, 0).astype(jnp.int32)
    start_at = jnp.where(starts, tile, n_tiles)
    next_start = jnp.concatenate([lax.cummin(start_at, reverse=True)[1:], jnp.full((1,), n_tiles, jnp.int32)])
    next_expert = jnp.where(next_start < n_tiles, take(tile_expert, jnp.minimum(next_start, n_tiles - 1)), -1)
    return d1, d2, P, (tile_expert, tile_valid, run_code, next_expert.astype(jnp.int32))


def _combine(h_ref, y1_ref, y2_ref, rg_ref):
    rg = rg_ref[...]
    lo1, hi1 = _unpack_halves(y1_ref[...])
    lo2, hi2 = _unpack_halves(y2_ref[...])
    g1, g2 = rg[:, 0:1], rg[:, 1:2]
    return h_ref[...] + jnp.concatenate([g1 * lo1 + g2 * lo2, g1 * hi1 + g2 * hi2], axis=1)


ODD_WIDTHS = (1024, 1024, 512, 512)


def _combine_proj_odd_kernel(h_ref, y1_ref, y2_ref, rg_ref, g_ref, win_ref, wvt_ref, ind_ref, hn_ref, vt_ref,
                             vdt_ref, *out_refs):
    hn = _combine(h_ref, y1_ref, y2_ref, rg_ref)
    hn_ref[...] = hn
    u = _rms(hn, g_ref[...]).astype(BF16)
    vt_ref[...] = _values_t(_dot_nt(wvt_ref[0:512, :], u)).astype(BF16)
    vdt_ref[...] = _dot_nt(wvt_ref[512:1024, :], u).astype(BF16)
    qc_ref, kc_ref, qd_ref, kd_ref = out_refs
    qc = _spread_heads(_dot(u, win_ref[:, 0:512]), lambda h: 0)
    kc = _spread_heads(_dot(u, win_ref[:, 512:1024]), lambda h: 0)
    for hh in range(8):
        qc_ref[:, 128 * hh:128 * (hh + 1)] = qc[hh].astype(BF16)
        kc_ref[:, 128 * hh:128 * (hh + 1)] = (kc[hh] + ind_ref[...]).astype(BF16)
    qd_ref[...] = _dot(u, win_ref[:, 1024:1536]).astype(BF16)
    kd_ref[...] = _dot(u, win_ref[:, 1536:2048]).astype(BF16)


def _odd_weights(w_in):
    s = HD ** -0.5 * LOG2E
    scale = np.asarray([s] * 512 + [1.0] * 512 + [s] * 512 + [1.0] * 512, np.float32)
    win = jnp.concatenate([w_in[:, 0:1024], w_in[:, 1536:2560]], axis=1) * jnp.asarray(scale)
    wvt = jnp.concatenate([w_in[:, 1024:1536], w_in[:, 2560:3072]], axis=1).T
    return win.astype(BF16), wvt.astype(BF16)


def _block_indicator(T):
    assert T // MOBA_L <= 64
    ind = np.zeros((T, 128), np.float32)
    ind[np.arange(T), 64 + np.arange(T) // MOBA_L] = 1.0
    return jnp.asarray(ind)


def _y_specs(N):
    return [pl.BlockSpec((TM, HALF), lambda i: (i, 0)), pl.BlockSpec((TM, HALF), lambda i: (i + N // TM, 0))]


def _combine_proj_odd(h, y12, rg, g, win, wvt, ind, T):
    N = h.shape[0]
    nt = T // TM
    full = lambda a: pl.BlockSpec(a.shape, lambda i: (0,) * a.ndim)
    row = lambda w: pl.BlockSpec((TM, w), lambda i: (i, 0))
    return pl.pallas_call(
        _combine_proj_odd_kernel,
        grid=(N // TM,),
        in_specs=[row(D)] + _y_specs(N) + [row(128), full(g), full(win), full(wvt),
                  pl.BlockSpec((TM, 128), lambda i: (i % nt, 0))],
        out_specs=[row(D), pl.BlockSpec((None, 8 * VT_ROWS, TM), lambda i: (i // nt, 0, i % nt)),
                   pl.BlockSpec((None, 512, TM), lambda i: (i // nt, 0, i % nt))]
        + [row(w) for w in ODD_WIDTHS],
        out_shape=[jax.ShapeDtypeStruct((N, D), F32), jax.ShapeDtypeStruct((N // T, 8 * VT_ROWS, T), BF16),
                   jax.ShapeDtypeStruct((N // T, 512, T), BF16)]
        + [jax.ShapeDtypeStruct((N, w), BF16) for w in ODD_WIDTHS],
        compiler_params=_cparams(("arbitrary",)),
        name="combine_proj_odd",
    )(h, y12, y12, rg, g, win, wvt, ind)


def _combine_final_kernel(h_ref, y1_ref, y2_ref, rg_ref, g_ref, o_ref):
    o_ref[...] = _rms(_combine(h_ref, y1_ref, y2_ref, rg_ref), g_ref[...])


def _combine_final(h, y12, rg, g):
    N = h.shape[0]
    row = lambda w: pl.BlockSpec((TM, w), lambda i: (i, 0))
    return pl.pallas_call(
        _combine_final_kernel,
        grid=(N // TM,),
        in_specs=[row(D)] + _y_specs(N) + [row(128), pl.BlockSpec((1, D), lambda i: (0, 0))],
        out_specs=row(D),
        out_shape=jax.ShapeDtypeStruct((N, D), F32),
        compiler_params=_cparams(("arbitrary",)),
        name="combine_final",
    )(h, y12, y12, rg, g)


def _moba_kernel(q_ref, k_ref, vt_ref, tb_ref, o_ref, kmean_ref, qa_ref, s_ref, smax_ref, m_ref, acc_ref, *, nkb,
                 nq):
    ia = pl.program_id(2)
    L, tq = MOBA_L, MOBA_TQ

    @pl.when(ia == 0)
    def _():
        kmean_ref[...] = jnp.zeros_like(kmean_ref)
        for n in range(nkb):
            kmean_ref[64 + n:65 + n, :] = jnp.sum(k_ref[n * L:(n + 1) * L, :].astype(F32), axis=0,
                                                  keepdims=True) * (1.0 / L)

    nb = -(-nkb // 8) * 8
    lane = lax.broadcasted_iota(jnp.int32, (tq, 128), 1)
    blk = lax.broadcasted_iota(jnp.int32, (nb, tq), 0)
    half = jnp.where(lax.broadcasted_iota(jnp.int32, (nb, tq), 1) >= L, 1, 0)
    blk_f = blk.astype(F32)
    for t, tile in enumerate((ia, nq - 1 - ia)):
        own = 2 * tile + half
        for j in range(2):
            qj = q_ref[pl.ds(pl.multiple_of(tile * tq, tq), tq), 128 * j:128 * (j + 1)]
            gsc = lax.dot_general(kmean_ref[64:64 + nb, 128 * j:128 * (j + 1)], qj.astype(F32),
                                  (((1,), (1,)), ((), ())), preferred_element_type=F32,
                                  precision=lax.Precision.HIGHEST)
            g = jnp.where(blk < own, gsc, NEG)
            allowed = jnp.where(blk == own, 1.0, 0.0)
            for _ in range(MOBA_TOPK):
                mx = jnp.max(g, axis=0, keepdims=True)
                idx = jnp.min(jnp.where(g == mx, blk_f, 1e6), axis=0, keepdims=True)
                pick = blk_f == idx
                allowed = jnp.maximum(allowed, jnp.where(pick, jnp.where(mx > 0.5 * NEG, 1.0, 0.0), 0.0))
                g = jnp.where(pick, 2.0 * NEG, g)
            mask = jnp.concatenate([jnp.zeros((64, tq), F32), jnp.where(allowed > 0.5, 0.0, NEG),
                                    jnp.zeros((64 - nb, tq), F32)], axis=0)
            qa_ref[t, j] = jnp.where(lane < HD, qj, mask.T.astype(BF16))

    def bias_of(start, q0, j):
        parts = []
        for c in range(tq // L):
            off = jnp.minimum(q0 - start - c * L, BIAS_CONST_FROM) + BIAS_PAD
            parts.append(tb_ref[j, :, pl.ds(pl.multiple_of(off, L), tq)])
        return jnp.concatenate(parts, axis=0)

    _causal_attention_pair_t(ia, nq, lambda t, q0, j: qa_ref[t, j], k_ref, vt_ref, o_ref, s_ref, smax_ref, m_ref,
                             acc_ref, tq=tq, bias_of=bias_of)


def _moba_attention(q, k, vt, tb, B, T):
    tq = MOBA_TQ
    nq = T // tq
    return pl.pallas_call(
        functools.partial(_moba_kernel, nkb=T // MOBA_L, nq=nq),
        grid=(B, 4, nq // 2),
        in_specs=[pl.BlockSpec((T, 256), lambda b, p, i: (b, p)),
                  pl.BlockSpec((T, 256), lambda b, p, i: (b, p)),
                  pl.BlockSpec((None, 2 * VT_ROWS, T), lambda b, p, i: (b, p, 0)),
                  pl.BlockSpec((2, MOBA_L, BIAS_ROWS), lambda b, p, i: (p, 0, 0))],
        out_specs=pl.BlockSpec((T, 128), lambda b, p, i: (b, p)),
        out_shape=jax.ShapeDtypeStruct((B * T, 512), BF16),
        scratch_shapes=[pltpu.VMEM((128, 256), F32), pltpu.VMEM((2, 2, tq, 128), BF16)]
        + _attention_t_scratch(tq),
        compiler_params=_cparams(("arbitrary", "arbitrary", "arbitrary")),
        name="moba_attention",
    )(q, k, vt, tb)


def _sb_kernel(q_ref, k_ref, vt_ref, o_ref, *, tq, pairs):
    i = pl.program_id(2)
    lane = lax.broadcasted_iota(jnp.int32, (tq, 128), 1)
    qs = []
    for p in range(pairs):
        q_pair = q_ref[:, 128 * p:128 * (p + 1)]
        qs += [jnp.where(lane < HD, q_pair, 0).astype(BF16), jnp.where(lane >= HD, q_pair, 0).astype(BF16)]
    key = lax.broadcasted_iota(jnp.int32, (tq, tq), 0)
    qry = lax.broadcasted_iota(jnp.int32, (tq, tq), 1)
    past = key < qry
    suffix = jnp.where(qry >= key, -1.0, 0.0).astype(BF16)
    heads = range(2 * pairs)

    def step(kb, carry, boundary):
        start = pl.multiple_of(kb * tq, tq)
        zs = [_dot_nt(k_ref[pl.ds(start, tq), 128 * (h // 2):128 * (h // 2 + 1)], qs[h]) for h in heads]
        parts = []
        for z in zs:
            neg_abs = pltpu.bitcast(pltpu.bitcast(z, jnp.uint32) | jnp.uint32(0x80000000), F32)
            soft = jnp.maximum(z, 0.0) + jnp.log2(1.0 + jnp.exp2(neg_abs))
            if boundary:
                soft = jnp.where(past, soft, 0.0)
            parts.append(soft.astype(BF16))
        sums = [_dot(suffix, part) for part in parts]
        ws = []
        for h in heads:
            w = jnp.exp2(zs[h] + (carry[h][0] + sums[h]))
            ws.append((jnp.where(past, w, 0.0) if boundary else w).astype(BF16))
        return tuple((carry[h][0] + sums[h][0:1], carry[h][1]
                      + _dot(vt_ref[128 * (h // 2):128 * (h // 2 + 1), pl.ds(start, tq)], ws[h])) for h in heads)

    def live(carry):
        top = carry[0][0]
        for c, _ in carry[1:]:
            top = jnp.maximum(top, c)
        return (jnp.max(top) > SB_DONE).astype(jnp.int32)

    def body(state):
        t, _, carry = state
        carry = step(i - 1 - t, carry, False)
        return t + 1, live(carry), carry

    init = tuple((jnp.zeros((1, tq), F32), jnp.zeros((128, tq), F32)) for _ in range(2 * pairs))
    carry = step(i, init, True)
    _, _, carry = lax.while_loop(lambda st: (st[0] < i) & (st[1] > 0), body, (jnp.int32(0), live(carry), carry))
    rows = lax.broadcasted_iota(jnp.int32, (128, tq), 0)
    for p in range(pairs):
        ot = jnp.where(rows < HD, carry[2 * p][1], carry[2 * p + 1][1])
        o_ref[:, 128 * p:128 * (p + 1)] = ot.T.astype(BF16)


def _sb_attention(q, k, vt, B, T, tq=256, pairs=4):
    nq = T // tq
    w = 128 * pairs
    return pl.pallas_call(
        functools.partial(_sb_kernel, tq=tq, pairs=pairs),
        grid=(B, 4 // pairs, nq),
        in_specs=[pl.BlockSpec((tq, w), lambda b, p, i: (b * nq + i, p)),
                  pl.BlockSpec((T, w), lambda b, p, i: (b, p)),
                  pl.BlockSpec((None, w, T), lambda b, p, i: (b, p, 0))],
        out_specs=pl.BlockSpec((tq, w), lambda b, p, i: (b * nq + i, p)),
        out_shape=jax.ShapeDtypeStruct((B * T, 512), BF16),
        compiler_params=_cparams(("arbitrary", "arbitrary", "arbitrary")),
        name="sb_attention",
    )(q, k, vt)


def _router_weights(w_group, b_group, w_router, b_router):
    pad = 128 - N_EXP - N_GROUPS
    wr = jnp.concatenate([w_router, w_group, jnp.zeros((D, pad), F32)], axis=1).T
    hi = wr.astype(BF16)
    lo = (wr - hi.astype(F32)).astype(BF16)
    br = jnp.concatenate([b_router, b_group, jnp.zeros((pad,), F32)])[:, None]
    return jnp.concatenate([hi, lo], axis=0), br


def _ffn(h, oa, ob, wo, g, w_group, b_group, w_router, b_router, w_gate, w_up, w_down, layer):
    wr, br = _router_weights(w_group, b_group, w_router, b_router)
    hn, u, ri, rg, cnt = _outproj_router(h, oa, ob, wo.astype(BF16), g[None, :], wr, br)
    d1, d2, P, plan = _dispatch_plan(ri, cnt, h.shape[0])
    xs = _sc_scatter_rows(u, d1, d2, P)
    ys = _moe_experts(xs, plan, w_gate, w_up, w_down, layer)
    return hn, _sc_gather_rows(ys, jnp.concatenate([d1, d2])), rg


def kernel(x, norm_mix, norm_ffn, norm_final, rel_bias_table, w_in_even, g_mla_q, g_mla_kv, w_mla_qb, w_mla_kvb, swa_sinks, w_out_even, w_in_odd, w_out_odd, moe_w_group, moe_b_group, moe_w_router, moe_b_router, moe_w_gate, moe_w_up, moe_w_down):
    B, T, _ = x.shape
    h = x.reshape(B * T, D)
    tb_swa, tb_moba = _bias_tables(rel_bias_table)

    win, wqa, wqb, wk, wv = _even_weights(w_in_even[0], w_mla_qb[0], w_mla_kvb[0])
    q, k, v, qs8, ks, vst = _proj_even(h, norm_mix[0][None, :], win, g_mla_q[0][None, :], g_mla_kv[0][None, :],
                                       wqa, wqb, wk, wv, _rope_tables(T), T)
    oa = _mla_attention(q, k, v, B, T)
    ob = _swa_attention(qs8, ks, vst, swa_sinks[0], tb_swa, B, T)
    h, y12, rg = _ffn(h, oa, ob, w_out_even[0], norm_ffn[0], moe_w_group[0], moe_b_group[0], moe_w_router[0],
                      moe_b_router[0], moe_w_gate, moe_w_up, moe_w_down, 0)

    h, vct, vdt, qc, kc, qd, kd = _combine_proj_odd(h, y12, rg, norm_mix[1][None, :], *_odd_weights(w_in_odd[0]),
                                                   _block_indicator(T), T)
    oc = _moba_attention(qc, kc, vct, tb_moba, B, T)
    od = _sb_attention(qd, kd, vdt, B, T)
    h, y12, rg = _ffn(h, oc, od, w_out_odd[0], norm_ffn[1], moe_w_group[1], moe_b_group[1], moe_w_router[1],
                      moe_b_router[1], moe_w_gate, moe_w_up, moe_w_down, 1)
    out = _combine_final(h, y12, rg, norm_final[None, :])
    return out.reshape(B, T, D)
```

```python
import functools
import math

import numpy as np
import jax
import jax.numpy as jnp
from jax import lax
from jax.experimental import pallas as pl
from jax.experimental.pallas import tpu as pltpu
from jax.experimental.pallas import tpu_sc as plsc

F32 = jnp.float32
BF16 = jnp.bfloat16
NEG = -1e30
EPS = 1e-6

D = 1024
HD = 64
MLA_HEADS, MLA_QR, MLA_KVR, MLA_NOPE, MLA_ROPE, MLA_V = 8, 256, 128, 64, 32, 64
ROPE_THETA = 10000.0
SWA_HEADS, SWA_KV, SWA_WIN = 8, 2, 128
MOBA_L, MOBA_TOPK = 256, 3
REL_BUCKETS, REL_MAX = 32, 2048
N_GROUPS, EPG, N_EXP, D_EXP = 4, 8, 32, 256
BIAS_CONST_FROM = 1792
MOBA_TQ = 2 * MOBA_L
BIAS_PAD = MOBA_L
BIAS_ROWS = BIAS_PAD + BIAS_CONST_FROM + MOBA_TQ
SB_DONE = -160.0
TM = 1024
TM_E = 512
VMEM_LIMIT = 56 * 1024 * 1024
SC_CORES, SC_SUBCORES = 2, 16
SC_CHUNK = 128
HALF = D // 2
VT_ROWS = HD + 16
LOG2E = math.log2(math.e)


def _pack_halves(x):
    lo = pltpu.bitcast(x[:, :HALF].astype(BF16).astype(F32), jnp.uint32) >> 16
    hi = pltpu.bitcast(x[:, HALF:].astype(BF16).astype(F32), jnp.uint32) & jnp.uint32(0xFFFF0000)
    return pltpu.bitcast(lo | hi, jnp.int32)


def _unpack_halves(w):
    w = pltpu.bitcast(w, jnp.uint32)
    return pltpu.bitcast(w << 16, F32), pltpu.bitcast(w & jnp.uint32(0xFFFF0000), F32)


def _cparams(sem):
    return pltpu.CompilerParams(dimension_semantics=sem, vmem_limit_bytes=VMEM_LIMIT)


def _dot(a, b):
    return jnp.dot(a, b, preferred_element_type=F32)


def _dot_nt(a, b):
    return lax.dot_general(a, b, (((1,), (1,)), ((), ())), preferred_element_type=F32)


def _rms(x, g):
    return x * lax.rsqrt(jnp.mean(x * x, axis=-1, keepdims=True) + EPS) * g


def _values_t(vt):
    ones = jnp.ones((VT_ROWS - HD, vt.shape[1]), vt.dtype)
    return jnp.concatenate([blk for h in range(8) for blk in (vt[HD * h:HD * (h + 1)], ones)], axis=0)


def _spread_heads(x, lane_of):
    lane = lax.broadcasted_iota(jnp.int32, (x.shape[0], 128), 1)
    out = []
    for h in range(8):
        pair = x[:, 128 * (h // 2):128 * (h // 2 + 1)]
        if HD * (h % 2) != lane_of(h):
            pair = pltpu.roll(pair, HD, 1)
        out.append(jnp.where((lane >= lane_of(h)) & (lane < lane_of(h) + HD), pair, 0.0))
    return out


def _t5_bucket_np(dist):
    n = np.maximum(dist, 0)
    max_exact = REL_BUCKETS // 2
    nf = np.maximum(n, 1).astype(np.float32)
    large = max_exact + (np.log(nf / np.float32(max_exact)) / np.float32(math.log(REL_MAX / max_exact))
                         * np.float32(REL_BUCKETS - max_exact)).astype(np.int32)
    large = np.minimum(large, REL_BUCKETS - 1)
    return np.where(n < max_exact, n, large).astype(np.int32)


def _bias_kernel(table_ref, bucket_ref, out_ref, *, scale, ranges):
    b = bucket_ref[...]
    for m, (lo, hi) in enumerate(ranges):
        @pl.when(pl.program_id(0) == m)
        def _():
            for h in range(8):
                acc = jnp.full(b.shape, NEG if lo == REL_BUCKETS else table_ref[lo, h] * scale, F32)
                for k in range(lo + 1, min(hi, REL_BUCKETS - 1) + 1):
                    acc = jnp.where(b == k, table_ref[k, h] * scale, acc)
                if hi == REL_BUCKETS:
                    acc = jnp.where(b == REL_BUCKETS, NEG, acc)
                out_ref[h] = acc


def _bias_lookup(rel_table, dist, scale, name, masked=None):
    R, C = dist.shape
    buckets = _t5_bucket_np(dist)
    if masked is not None:
        buckets = np.where(masked, REL_BUCKETS, buckets).astype(np.int32)
    ranges = tuple((int(buckets[:, c:c + 256].min()), int(buckets[:, c:c + 256].max())) for c in range(0, C, 256))
    return pl.pallas_call(
        functools.partial(_bias_kernel, scale=scale, ranges=ranges),
        grid=(C // 256,),
        in_specs=[pl.BlockSpec(memory_space=pltpu.SMEM),
                  pl.BlockSpec((R, 256), lambda m: (0, m))],
        out_specs=pl.BlockSpec((8, R, 256), lambda m: (0, 0, m)),
        out_shape=jax.ShapeDtypeStruct((8, R, C), F32),
        compiler_params=_cparams(("arbitrary",)),
        name=name,
    )(rel_table, jnp.asarray(buckets))


def _bias_tables(rel_table):
    d = np.arange(256)[None, :] - np.arange(256)[:, None]
    swa = _bias_lookup(rel_table, d, LOG2E, "bias_swa", masked=(d < 0) | (d >= SWA_WIN))
    moba = _bias_lookup(rel_table, np.arange(BIAS_ROWS)[None, :] - BIAS_PAD - np.arange(MOBA_L)[:, None], LOG2E,
                        "bias_moba")
    return swa, moba


def _proj_even_kernel(h_ref, g_ref, win_ref, gq_ref, gkv_ref, wqa_ref, wqb_ref, wk_ref, wv_ref,
                      cq_ref, sq_ref, ck_ref, sk_ref,
                      q_ref, k_ref, v_ref, qs_ref, ks_ref, vst_ref):
    u = _rms(h_ref[...], g_ref[...]).astype(BF16)
    nq = _rms(_dot(u, win_ref[:, 0:256]), gq_ref[...]).astype(BF16)
    qa = _dot(nq, wqa_ref[...])
    qb = _dot(nq, wqb_ref[...])
    cq, sq = cq_ref[...], sq_ref[...]
    for hh in range(MLA_HEADS):
        sl = slice(128 * hh, 128 * (hh + 1))
        q_ref[:, sl] = (qa[:, sl] * cq + qb[:, sl] * sq).astype(BF16)
    nkv = _rms(_dot(u, win_ref[:, 256:384]), gkv_ref[...]).astype(BF16)
    kn = _dot(nkv, wk_ref[...])
    v_ref[...] = _values_t(_dot_nt(wv_ref[...], nkv)).astype(BF16)
    rest = _dot(u, win_ref[:, 896:1408])
    kr = rest[:, 128:256] * ck_ref[...] + rest[:, 256:384] * sk_ref[...]
    for hh in range(MLA_HEADS):
        sl = slice(128 * hh, 128 * (hh + 1))
        k_ref[:, sl] = (kn[:, sl] + kr).astype(BF16)
    qs = _spread_heads(_dot(u, win_ref[:, 384:896]), lambda hq: HD * (hq // (SWA_HEADS // SWA_KV)))
    for hq in range(SWA_HEADS):
        qs_ref[:, 128 * hq:128 * (hq + 1)] = qs[hq].astype(BF16)
    ks_ref[...] = rest[:, 0:128].astype(BF16)
    vst_ref[...] = rest[:, 384:512].T.astype(BF16)


def _cols(w, idx, scale=None):
    idx = np.asarray(idx)
    cuts = [0] + [n for n in range(1, len(idx)) if (idx[n] < 0) != (idx[n - 1] < 0)
                  or (idx[n] >= 0 and idx[n] != idx[n - 1] + 1)] + [len(idx)]
    parts = [jnp.zeros((w.shape[0], b - a), w.dtype) if idx[a] < 0 else w[:, idx[a]:idx[a] + b - a]
             for a, b in zip(cuts[:-1], cuts[1:])]
    out = jnp.concatenate(parts, axis=1)
    if scale is not None:
        out = out * scale
    return out


def _even_weights(w_in, w_qb, w_kvb):
    o_cq, o_ckv, o_kr = 0, MLA_QR, MLA_QR + MLA_KVR
    o_qs = o_kr + MLA_ROPE
    o_ks = o_qs + SWA_HEADS * HD
    o_vs = o_ks + SWA_KV * HD
    idx = list(range(o_cq, o_cq + MLA_QR)) + list(range(o_ckv, o_ckv + MLA_KVR))
    scale = [1.0] * len(idx)
    idx += list(range(o_qs, o_qs + SWA_HEADS * HD))
    scale += [HD ** -0.5 * LOG2E] * (SWA_HEADS * HD)
    idx += list(range(o_ks, o_ks + 128))
    scale += [1.0] * 128
    half = MLA_ROPE // 2
    idx += [-1] * 64 + list(range(o_kr, o_kr + MLA_ROPE)) + [-1] * 32
    idx += [-1] * 64 + list(range(o_kr + half, o_kr + MLA_ROPE)) + list(range(o_kr, o_kr + half)) + [-1] * 32
    idx += list(range(o_vs, o_vs + 128))
    scale += [1.0] * 384
    win = _cols(w_in, idx, jnp.asarray(np.asarray(scale, np.float32))).astype(BF16)
    qw = MLA_NOPE + MLA_ROPE
    ia, ib, ik, iv = [], [], [], []
    for h in range(MLA_HEADS):
        ia += list(range(qw * h, qw * h + qw)) + [-1] * 32
        ib += [-1] * 64 + list(range(qw * h + MLA_NOPE + half, qw * h + qw)) \
            + list(range(qw * h + MLA_NOPE, qw * h + MLA_NOPE + half)) + [-1] * 32
        ik += list(range(128 * h, 128 * h + MLA_NOPE)) + [-1] * 64
        iv += list(range(128 * h + MLA_NOPE, 128 * h + 128))
    return (win, _cols(w_qb, ia).astype(BF16), _cols(w_qb, ib).astype(BF16),
            _cols(w_kvb, ik).astype(BF16), _cols(w_kvb, iv).T.astype(BF16))


def _rope_tables(T):
    f32 = np.float32
    freqs = f32(ROPE_THETA) ** (-np.arange(0, MLA_ROPE, 2, dtype=f32) / f32(MLA_ROPE))
    ang = np.arange(T, dtype=f32)[:, None] * freqs[None, :]
    cos, sin = np.cos(ang), np.sin(ang)
    z = lambda n: np.zeros((T, n), f32)
    scale = f32((MLA_NOPE + MLA_ROPE) ** -0.5 * LOG2E)
    ck = np.concatenate([z(64), cos, cos, z(32)], axis=1)
    sk = np.concatenate([z(64), -sin, sin, z(32)], axis=1)
    cq = np.concatenate([np.ones((T, 64), f32), cos, cos, z(32)], axis=1) * scale
    sq = sk * scale
    return tuple(jnp.asarray(t, F32) for t in (cq, sq, ck, sk))


def _proj_even(h, g, win, gq, gkv, wqa, wqb, wk, wv, tables, T):
    N = h.shape[0]
    nt = T // TM
    full = lambda a: pl.BlockSpec(a.shape, lambda i: (0,) * a.ndim)
    tab = pl.BlockSpec((TM, 128), lambda i: (i % nt, 0))
    row = lambda w: pl.BlockSpec((TM, w), lambda i: (i, 0))
    outs = (1024, 1024, -8 * VT_ROWS, 1024, 128, -128)
    spec = lambda w: row(w) if w > 0 else pl.BlockSpec((None, -w, TM), lambda i: (i // nt, 0, i % nt))
    shape = lambda w: jax.ShapeDtypeStruct((N, w) if w > 0 else (N // T, -w, T), BF16)
    return pl.pallas_call(
        _proj_even_kernel,
        grid=(N // TM,),
        in_specs=[row(D), full(g), full(win), full(gq), full(gkv), full(wqa), full(wqb), full(wk), full(wv),
                  tab, tab, tab, tab],
        out_specs=[spec(w) for w in outs],
        out_shape=[shape(w) for w in outs],
        compiler_params=_cparams(("arbitrary",)),
        name="proj_even",
    )(h, g, win, gq, gkv, wqa, wqb, wk, wv, *tables)


def _finish_t(accs):
    ot = jnp.concatenate([acc[:HD] / acc[HD:HD + 1] for acc in accs], axis=0)
    return ot.T.astype(BF16)


def _causal_attention_pair_t(ia, nq, q_of, k_ref, vt_ref, o_ref, s_ref, smax_ref, m_ref, acc_ref, *, tq,
                             bias_of=None):
    nh = s_ref.shape[1]
    first_query = (ia * tq, (nq - 1 - ia) * tq)
    n_items = nq + 1
    assert n_items % 2 == 1
    key = lax.broadcasted_iota(jnp.int32, (tq, tq), 0)
    qry = lax.broadcasted_iota(jnp.int32, (tq, tq), 1)

    def item(p):
        if isinstance(p, int) and p < 2:
            return p, first_query[p], first_query[p]
        t = jnp.where(p - 2 >= ia, 1, 0)
        return t, jnp.where(t == 1, first_query[1], first_query[0]), (p - 2 - t * ia) * tq

    def scores_to(slot, p):
        t, q0, start = item(p)
        start = pl.multiple_of(start, tq)
        for j in range(nh):
            st = _dot_nt(k_ref[pl.ds(start, tq), 128 * j:128 * (j + 1)], q_of(t, q0, j))
            if bias_of is not None:
                st = st + bias_of(start, q0, j)
            if isinstance(p, int) and p < 2:
                st = jnp.where(key <= qry, st, NEG)
            s_ref[slot, j] = st
            smax_ref[slot, j] = jnp.max(st, axis=0, keepdims=True)

    def update_from(slot, p):
        t, _, start = item(p)
        start = pl.multiple_of(start, tq)
        m_new = [jnp.maximum(m_ref[t, j], smax_ref[slot, j]) for j in range(nh)]
        pts = [jnp.exp2(s_ref[slot, j] - m_new[j]).astype(BF16) for j in range(nh)]
        for j in range(nh):
            acc_ref[t, j] = (jnp.exp2(m_ref[t, j] - m_new[j]) * acc_ref[t, j]
                             + _dot(vt_ref[VT_ROWS * j:VT_ROWS * (j + 1), pl.ds(start, tq)], pts[j]))
            m_ref[t, j] = m_new[j]

    def pair(k):
        scores_to(1, k + 1)
        update_from(0, k)
        scores_to(0, k + 2)
        update_from(1, k + 1)

    def quad(c, _):
        pair(2 + 4 * c)
        pair(4 + 4 * c)
        return 0

    m_ref[...] = jnp.full(m_ref.shape, NEG, F32)
    acc_ref[...] = jnp.zeros(acc_ref.shape, F32)
    scores_to(0, 0)
    pair(0)
    later_pairs = (n_items - 1) // 2 - 1
    lax.fori_loop(0, later_pairs // 2, quad, 0)
    if later_pairs % 2:
        pair(n_items - 3)
    update_from(0, n_items - 1)
    for t in range(2):
        o_ref[pl.ds(pl.multiple_of(first_query[t], tq), tq), :] = _finish_t([acc_ref[t, j] for j in range(nh)])


def _attention_t_scratch(tq, nh=2):
    return [pltpu.VMEM((2, nh, tq, tq), F32), pltpu.VMEM((2, nh, 1, tq), F32), pltpu.VMEM((2, nh, 1, tq), F32),
            pltpu.VMEM((2, nh, VT_ROWS, tq), F32)]


def _mla_kernel(q_ref, k_ref, vt_ref, o_ref, s_ref, smax_ref, m_ref, acc_ref, *, tq, nq):
    def q_of(t, q0, j):
        return q_ref[pl.ds(pl.multiple_of(q0, tq), tq), 128 * j:128 * (j + 1)]

    _causal_attention_pair_t(pl.program_id(2), nq, q_of, k_ref, vt_ref, o_ref, s_ref, smax_ref, m_ref, acc_ref,
                             tq=tq)


def _mla_attention(q, k, vt, B, T, tq=512):
    nq = T // tq
    return pl.pallas_call(
        functools.partial(_mla_kernel, tq=tq, nq=nq),
        grid=(B, MLA_HEADS // 2, nq // 2),
        in_specs=[pl.BlockSpec((T, 256), lambda b, p, i: (b, p)),
                  pl.BlockSpec((T, 256), lambda b, p, i: (b, p)),
                  pl.BlockSpec((None, 2 * VT_ROWS, T), lambda b, p, i: (b, p, 0))],
        out_specs=pl.BlockSpec((T, 128), lambda b, p, i: (b, p)),
        out_shape=jax.ShapeDtypeStruct((B * T, MLA_HEADS * MLA_V), BF16),
        scratch_shapes=_attention_t_scratch(tq),
        compiler_params=_cparams(("arbitrary", "arbitrary", "arbitrary")),
        name="mla_attention",
    )(q, k, vt)


def _swa_kernel(sink_ref, q_ref, k_ref, vt_ref, tb_ref, o_ref, *, tq):
    i = pl.program_id(1)
    G = SWA_HEADS // SWA_KV
    W = 2 * SWA_WIN
    head = lax.broadcasted_iota(jnp.int32, (1, G * 128), 1) >> 7
    sinks = []
    for c in range(SWA_KV):
        sink = jnp.zeros((1, G * 128), F32)
        for g in range(G):
            sink = jnp.where(head == g, sink_ref[G * c + g], sink)
        sinks.append(sink)
    scores, values = [], []
    for r in range(tq // 128):
        qstart = i * tq + 128 * r
        kstart = pl.multiple_of(jnp.maximum(qstart - SWA_WIN, 0), 128)
        off = pl.multiple_of(qstart - kstart, 128)
        kw = k_ref[pl.ds(kstart, W), :]
        for c in range(SWA_KV):
            qg = jnp.concatenate([q_ref[128 * r:128 * (r + 1), 128 * (G * c + g):128 * (G * c + g + 1)]
                                  for g in range(G)], axis=0)
            bias = jnp.concatenate([tb_ref[G * c + g, :, pl.ds(off, 128)] for g in range(G)], axis=1)
            scores.append(_dot_nt(kw, qg) + bias)
            values.append(vt_ref[HD * c:HD * (c + 1), pl.ds(kstart, W)])
    probs = []
    for n, s in enumerate(scores):
        sink = sinks[n % SWA_KV]
        m = jnp.maximum(jnp.max(s, axis=0, keepdims=True), sink)
        p = jnp.exp2(s - m)
        probs.append((p.astype(BF16), jnp.sum(p, axis=0, keepdims=True) + jnp.exp2(sink - m)))
    for r in range(tq // 128):
        outs = []
        for c in range(SWA_KV):
            p, denom = probs[SWA_KV * r + c]
            ot = _dot(values[SWA_KV * r + c], p) / denom
            outs += [ot[:, 128 * g:128 * (g + 1)] for g in range(G)]
        o_ref[128 * r:128 * (r + 1), :] = jnp.concatenate(outs, axis=0).T.astype(BF16)


def _swa_attention(q8, k, vt, sinks, tb, B, T, tq=512):
    nq = T // tq
    return pl.pallas_call(
        functools.partial(_swa_kernel, tq=tq),
        grid=(B, nq),
        in_specs=[pl.BlockSpec(memory_space=pltpu.SMEM),
                  pl.BlockSpec((tq, 1024), lambda b, i: (b * nq + i, 0)),
                  pl.BlockSpec((T, 128), lambda b, i: (b, 0)),
                  pl.BlockSpec((None, 128, T), lambda b, i: (b, 0, 0)),
                  pl.BlockSpec((8, 256, 256), lambda b, i: (0, 0, 0))],
        out_specs=pl.BlockSpec((tq, 512), lambda b, i: (b * nq + i, 0)),
        out_shape=jax.ShapeDtypeStruct((B * T, SWA_HEADS * HD), BF16),
        compiler_params=_cparams(("arbitrary", "arbitrary")),
        name="swa_attention",
    )(sinks * LOG2E, q8, k, vt, tb)


def _outproj_router_kernel(h_ref, oa_ref, ob_ref, wo_ref, g_ref, wr_ref, br_ref,
                           hn_ref, u_ref, ri_ref, rg_ref, cnt_ref, base_ref):
    i = pl.program_id(0)

    @pl.when(i == 0)
    def _():
        base_ref[...] = jnp.zeros_like(base_ref)

    hn = h_ref[...] + _dot(oa_ref[...], wo_ref[0:512, :]) + _dot(ob_ref[...], wo_ref[512:1024, :])
    hn_ref[...] = hn
    u = _rms(hn, g_ref[...])
    u_ref[...] = _pack_halves(u)
    u_hi = u.astype(BF16)
    u_lo = (u - u_hi.astype(F32)).astype(BF16)
    both = _dot_nt(wr_ref[...], u_hi)
    logits = both[:128] + both[128:] + _dot_nt(wr_ref[0:128, :], u_lo) + br_ref[...]
    tm = logits.shape[1]
    sub = lax.broadcasted_iota(jnp.int32, (8, tm), 0).astype(F32)
    big = 1e6
    isg = sub < N_GROUPS
    gl = jnp.where(isg, logits[N_EXP:N_EXP + 8], NEG)
    gmax = jnp.max(gl, axis=0, keepdims=True)
    gsel = jnp.min(jnp.where(gl == gmax, sub, big), axis=0, keepdims=True)
    g_gate = 1.0 / jnp.sum(jnp.where(isg, jnp.exp(gl - gmax), 0.0), axis=0, keepdims=True)
    el = jnp.zeros((EPG, tm), F32)
    for grp in range(N_GROUPS):
        el = jnp.where(gsel == grp, logits[EPG * grp:EPG * (grp + 1)], el)
    m1 = jnp.max(el, axis=0, keepdims=True)
    i1 = jnp.min(jnp.where(el == m1, sub, big), axis=0, keepdims=True)
    el2 = jnp.where(sub == i1, NEG, el)
    m2 = jnp.max(el2, axis=0, keepdims=True)
    i2 = jnp.min(jnp.where(el2 == m2, sub, big), axis=0, keepdims=True)
    e1 = gsel * EPG + i1
    e2 = gsel * EPG + i2
    r = jnp.exp(m2 - m1)
    gate1 = g_gate / (1.0 + r)
    gate2 = g_gate * r / (1.0 + r)
    rows = lax.broadcasted_iota(jnp.int32, (128, tm), 0).astype(F32)
    oh1 = (rows == e1).astype(F32)
    oh2 = (rows == e2).astype(F32)
    oh = oh1 + oh2
    t_row = lax.broadcasted_iota(jnp.int32, (tm, tm), 0)
    t_col = lax.broadcasted_iota(jnp.int32, (tm, tm), 1)
    earlier = jnp.where(t_row < t_col, 1.0, 0.0).astype(BF16)
    prefix = _dot(oh.astype(BF16), earlier) + base_ref[:, 0:1]
    rank1 = jnp.sum(oh1 * prefix, axis=0, keepdims=True)
    rank2 = jnp.sum(oh2 * prefix, axis=0, keepdims=True)
    base_ref[...] = base_ref[...] + jnp.sum(oh, axis=1, keepdims=True)
    cnt_ref[...] = base_ref[...]
    ri_ref[...] = jnp.where(sub == 0, e1, jnp.where(sub == 1, e2, jnp.where(sub == 2, rank1,
                            jnp.where(sub == 3, rank2, 0.0))))
    rg_ref[...] = jnp.where(rows == 0, gate1, jnp.where(rows == 1, gate2, 0.0)).T


def _outproj_router(h, oa, ob, wo, g, wr, br):
    N = h.shape[0]
    full = lambda a: pl.BlockSpec(a.shape, lambda i: (0,) * a.ndim)
    row = lambda w: pl.BlockSpec((TM, w), lambda i: (i, 0))
    return pl.pallas_call(
        _outproj_router_kernel,
        grid=(N // TM,),
        in_specs=[row(D), row(512), row(512), full(wo), full(g), full(wr), full(br)],
        out_specs=[row(D), row(HALF), pl.BlockSpec((8, TM), lambda i: (0, i)), row(128),
                   pl.BlockSpec((128, 128), lambda i: (0, 0))],
        out_shape=[jax.ShapeDtypeStruct((N, D), F32), jax.ShapeDtypeStruct((N, HALF), jnp.int32),
                   jax.ShapeDtypeStruct((8, N), F32), jax.ShapeDtypeStruct((N, 128), F32),
                   jax.ShapeDtypeStruct((128, 128), F32)],
        scratch_shapes=[pltpu.VMEM((128, 128), F32)],
        compiler_params=_cparams(("arbitrary",)),
        name="outproj_router",
    )(h, oa, ob, wo, g, wr, br)


def _moe_kernel(te_ref, tv_ref, run_ref, nx_ref, x_ref, wg_hbm, wu_hbm, wd_hbm, o_ref,
                wg_buf, wu_buf, wd_buf, sems, wgu_s, wd_s, *, layer):
    i = pl.program_id(0)
    valid = tv_ref[i]

    def weight_copies(e, slot):
        return (pltpu.make_async_copy(wg_hbm.at[layer, e], wg_buf.at[slot], sems.at[slot, 0]),
                pltpu.make_async_copy(wu_hbm.at[layer, e], wu_buf.at[slot], sems.at[slot, 1]),
                pltpu.make_async_copy(wd_hbm.at[layer, e], wd_buf.at[slot], sems.at[slot, 2]))

    @pl.when(run_ref[i] > 0)
    def _():
        slot = run_ref[i] - 1

        @pl.when(i == 0)
        def _():
            for c in weight_copies(te_ref[i], slot):
                c.start()

        for c in weight_copies(te_ref[i], slot):
            c.wait()
        wgu_s[:, :D_EXP] = wg_buf[slot].astype(BF16)
        wgu_s[:, D_EXP:] = wu_buf[slot].astype(BF16)
        wd_s[...] = wd_buf[slot].astype(BF16)

        @pl.when(nx_ref[i] >= 0)
        def _():
            for c in weight_copies(nx_ref[i], 1 - slot):
                c.start()

    @pl.when(valid > 0)
    def _():
        rows = lax.broadcasted_iota(jnp.int32, x_ref.shape, 0)
        lo, hi = _unpack_halves(jnp.where(rows < valid, x_ref[...], 0))
        hgu = _dot(lo.astype(BF16), wgu_s[:HALF, :]) + _dot(hi.astype(BF16), wgu_s[HALF:, :])
        hg, hu = hgu[:, :D_EXP], hgu[:, D_EXP:]
        a = hg * (1.0 / (1.0 + jnp.exp(-hg))) * hu
        o_ref[...] = _pack_halves(_dot(a.astype(BF16), wd_s[...]))

    @pl.when(valid <= 0)
    def _():
        o_ref[...] = jnp.zeros_like(o_ref)


def _moe_experts(xs, plan, w_gate, w_up, w_down, layer):
    P = xs.shape[0]
    hbm = pl.BlockSpec(memory_space=pl.ANY)
    grid_spec = pltpu.PrefetchScalarGridSpec(
        num_scalar_prefetch=4,
        grid=(P // TM_E,),
        in_specs=[pl.BlockSpec((TM_E, HALF), lambda i, *_: (i, 0)), hbm, hbm, hbm],
        out_specs=pl.BlockSpec((TM_E, HALF), lambda i, *_: (i, 0)),
        scratch_shapes=[pltpu.VMEM((2, D, D_EXP), F32), pltpu.VMEM((2, D, D_EXP), F32),
                        pltpu.VMEM((2, D_EXP, D), F32), pltpu.SemaphoreType.DMA((2, 3)),
                        pltpu.VMEM((D, 2 * D_EXP), BF16), pltpu.VMEM((D_EXP, D), BF16)],
    )
    return pl.pallas_call(
        functools.partial(_moe_kernel, layer=layer),
        grid_spec=grid_spec,
        out_shape=jax.ShapeDtypeStruct((P, HALF), jnp.int32),
        compiler_params=_cparams(("arbitrary",)),
        name="moe_experts",
    )(*plan, xs, w_gate, w_up, w_down)


def _sc_mesh():
    return plsc.VectorSubcoreMesh(core_axis_name="c", subcore_axis_name="s",
                                  num_cores=SC_CORES, num_subcores=SC_SUBCORES)


def _sc_scatter_rows(src, d1, d2, P):
    N, W = src.shape
    per_w = N // (SC_CORES * SC_SUBCORES)

    @functools.partial(pl.kernel, mesh=_sc_mesh(), out_type=jax.ShapeDtypeStruct((P, W), src.dtype),
                       scratch_types=[pltpu.VMEM((SC_CHUNK,), jnp.int32), pltpu.VMEM((SC_CHUNK, W), src.dtype)],
                       name="sc_scatter_rows")
    def k(src_hbm, d1_hbm, d2_hbm, out_hbm, idx_v, rows_v):
        wid = lax.axis_index("s") * SC_CORES + lax.axis_index("c")

        @pl.loop(0, per_w // SC_CHUNK)
        def _(c):
            off = wid * per_w + c * SC_CHUNK
            pltpu.sync_copy(src_hbm.at[pl.ds(off, SC_CHUNK)], rows_v)
            pltpu.sync_copy(d1_hbm.at[pl.ds(off, SC_CHUNK)], idx_v)
            pltpu.sync_copy(rows_v, out_hbm.at[idx_v])
            pltpu.sync_copy(d2_hbm.at[pl.ds(off, SC_CHUNK)], idx_v)
            pltpu.sync_copy(rows_v, out_hbm.at[idx_v])

    return k(src, d1, d2)


def _sc_gather_rows(table, idx):
    B, W = idx.shape[0], table.shape[1]
    per_w = B // (SC_CORES * SC_SUBCORES)

    @functools.partial(pl.kernel, mesh=_sc_mesh(), out_type=jax.ShapeDtypeStruct((B, W), table.dtype),
                       scratch_types=[pltpu.VMEM((SC_CHUNK,), jnp.int32), pltpu.VMEM((SC_CHUNK, W), table.dtype)],
                       name="sc_gather_rows")
    def k(table_hbm, idx_hbm, out_hbm, idx_v, rows_v):
        wid = lax.axis_index("s") * SC_CORES + lax.axis_index("c")

        @pl.loop(0, per_w // SC_CHUNK)
        def _(c):
            off = wid * per_w + c * SC_CHUNK
            pltpu.sync_copy(idx_hbm.at[pl.ds(off, SC_CHUNK)], idx_v)
            pltpu.sync_copy(table_hbm.at[idx_v], rows_v)
            pltpu.sync_copy(rows_v, out_hbm.at[pl.ds(off, SC_CHUNK)])

    return k(table, idx)


def _plan_kernel(cnt_ref, ri_ref, d_ref, te_ref, tv_ref, run_ref, nx_ref, offs_ref, *, n_tiles):
    def segment(e, carry):
        off, t = carry
        offs_ref[e] = off
        c = cnt_ref[e]
        nt = (c + TM_E - 1) // TM_E

        def fill(j, _):
            te_ref[t + j] = e
            tv_ref[t + j] = jnp.minimum(c - j * TM_E, TM_E)
            return 0

        lax.fori_loop(0, nt, fill, 0)
        return off + nt * TM_E, t + nt

    _, used = lax.fori_loop(0, N_EXP, segment, (jnp.int32(0), jnp.int32(0)))

    def tail(j, _):
        te_ref[j] = N_EXP - 1
        tv_ref[j] = 0
        return 0

    lax.fori_loop(used, n_tiles, tail, 0)

    def mark_runs(j, runs):
        start = (tv_ref[j] > 0) & ((j == 0) | (te_ref[j] != te_ref[jnp.maximum(j - 1, 0)]))
        run_ref[j] = jnp.where(start, runs % 2 + 1, 0)
        return runs + start.astype(jnp.int32)

    lax.fori_loop(0, n_tiles, mark_runs, jnp.int32(0))

    def next_run(jj, nxt):
        j = n_tiles - 1 - jj
        nx_ref[j] = nxt
        return jnp.where(run_ref[j] > 0, te_ref[j], nxt)

    lax.fori_loop(0, n_tiles, next_run, jnp.int32(-1))

    ri = ri_ref[...]
    base = jnp.zeros(ri.shape, F32)
    for e in range(N_EXP):
        base = jnp.where(ri == e, offs_ref[e].astype(F32), base)
    d_ref[...] = (base + pltpu.roll(ri, 6, 0)).astype(jnp.int32)


def _dispatch_plan(ri, cnt, N):
    P = 2 * N + N_EXP * TM_E
    n_tiles = P // TM_E
    smem = pl.BlockSpec(memory_space=pltpu.SMEM)
    tiles = jax.ShapeDtypeStruct((n_tiles,), jnp.int32)
    dest, *plan = pl.pallas_call(
        functools.partial(_plan_kernel, n_tiles=n_tiles),
        in_specs=[smem, pl.BlockSpec((8, N), lambda: (0, 0))],
        out_specs=[pl.BlockSpec((8, N), lambda: (0, 0)), smem, smem, smem, smem],
        out_shape=[jax.ShapeDtypeStruct((8, N), jnp.int32), tiles, tiles, tiles, tiles],
        scratch_shapes=[pltpu.SMEM((N_EXP,), jnp.int32)],
        name="dispatch_plan",
    )(cnt[:N_EXP, 0].astype(jnp.int32), ri)
    return dest[0], dest[1], P, tuple(plan)


def _combine(h_ref, y1_ref, y2_ref, rg_ref):
    rg = rg_ref[...]
    lo1, hi1 = _unpack_halves(y1_ref[...])
    lo2, hi2 = _unpack_halves(y2_ref[...])
    g1, g2 = rg[:, 0:1], rg[:, 1:2]
    return h_ref[...] + jnp.concatenate([g1 * lo1 + g2 * lo2, g1 * hi1 + g2 * hi2], axis=1)


ODD_WIDTHS = (1024, 1024, 512, 512)


def _combine_proj_odd_kernel(h_ref, y1_ref, y2_ref, rg_ref, g_ref, win_ref, wvt_ref, ind_ref, hn_ref, vt_ref,
                             vdt_ref, *out_refs):
    hn = _combine(h_ref, y1_ref, y2_ref, rg_ref)
    hn_ref[...] = hn
    u = _rms(hn, g_ref[...]).astype(BF16)
    vt_ref[...] = _values_t(_dot_nt(wvt_ref[0:512, :], u)).astype(BF16)
    vdt_ref[...] = _dot_nt(wvt_ref[512:1024, :], u).astype(BF16)
    qc_ref, kc_ref, qd_ref, kd_ref = out_refs
    qc = _spread_heads(_dot(u, win_ref[:, 0:512]), lambda h: 0)
    kc = _spread_heads(_dot(u, win_ref[:, 512:1024]), lambda h: 0)
    for hh in range(8):
        qc_ref[:, 128 * hh:128 * (hh + 1)] = qc[hh].astype(BF16)
        kc_ref[:, 128 * hh:128 * (hh + 1)] = (kc[hh] + ind_ref[...]).astype(BF16)
    qd_ref[...] = _dot(u, win_ref[:, 1024:1536]).astype(BF16)
    kd_ref[...] = _dot(u, win_ref[:, 1536:2048]).astype(BF16)


def _odd_weights(w_in):
    s = HD ** -0.5 * LOG2E
    scale = np.asarray([s] * 512 + [1.0] * 512 + [s] * 512 + [1.0] * 512, np.float32)
    win = jnp.concatenate([w_in[:, 0:1024], w_in[:, 1536:2560]], axis=1) * jnp.asarray(scale)
    wvt = jnp.concatenate([w_in[:, 1024:1536], w_in[:, 2560:3072]], axis=1).T
    return win.astype(BF16), wvt.astype(BF16)


def _block_indicator(T):
    assert T // MOBA_L <= 64
    ind = np.zeros((T, 128), np.float32)
    ind[np.arange(T), 64 + np.arange(T) // MOBA_L] = 1.0
    return jnp.asarray(ind)


def _y_specs(N):
    return [pl.BlockSpec((TM, HALF), lambda i: (i, 0)), pl.BlockSpec((TM, HALF), lambda i: (i + N // TM, 0))]


def _combine_proj_odd(h, y12, rg, g, win, wvt, ind, T):
    N = h.shape[0]
    nt = T // TM
    full = lambda a: pl.BlockSpec(a.shape, lambda i: (0,) * a.ndim)
    row = lambda w: pl.BlockSpec((TM, w), lambda i: (i, 0))
    return pl.pallas_call(
        _combine_proj_odd_kernel,
        grid=(N // TM,),
        in_specs=[row(D)] + _y_specs(N) + [row(128), full(g), full(win), full(wvt),
                  pl.BlockSpec((TM, 128), lambda i: (i % nt, 0))],
        out_specs=[row(D), pl.BlockSpec((None, 8 * VT_ROWS, TM), lambda i: (i // nt, 0, i % nt)),
                   pl.BlockSpec((None, 512, TM), lambda i: (i // nt, 0, i % nt))]
        + [row(w) for w in ODD_WIDTHS],
        out_shape=[jax.ShapeDtypeStruct((N, D), F32), jax.ShapeDtypeStruct((N // T, 8 * VT_ROWS, T), BF16),
                   jax.ShapeDtypeStruct((N // T, 512, T), BF16)]
        + [jax.ShapeDtypeStruct((N, w), BF16) for w in ODD_WIDTHS],
        compiler_params=_cparams(("arbitrary",)),
        name="combine_proj_odd",
    )(h, y12, y12, rg, g, win, wvt, ind)


def _combine_final_kernel(h_ref, y1_ref, y2_ref, rg_ref, g_ref, o_ref):
    o_ref[...] = _rms(_combine(h_ref, y1_ref, y2_ref, rg_ref), g_ref[...])


def _combine_final(h, y12, rg, g):
    N = h.shape[0]
    row = lambda w: pl.BlockSpec((TM, w), lambda i: (i, 0))
    return pl.pallas_call(
        _combine_final_kernel,
        grid=(N // TM,),
        in_specs=[row(D)] + _y_specs(N) + [row(128), pl.BlockSpec((1, D), lambda i: (0, 0))],
        out_specs=row(D),
        out_shape=jax.ShapeDtypeStruct((N, D), F32),
        compiler_params=_cparams(("arbitrary",)),
        name="combine_final",
    )(h, y12, y12, rg, g)


def _moba_kernel(q_ref, k_ref, vt_ref, tb_ref, o_ref, kmean_ref, qa_ref, s_ref, smax_ref, m_ref, acc_ref, *, nkb,
                 nq):
    ia = pl.program_id(2)
    L, tq = MOBA_L, MOBA_TQ

    @pl.when(ia == 0)
    def _():
        kmean_ref[...] = jnp.zeros_like(kmean_ref)
        for n in range(nkb):
            kmean_ref[64 + n:65 + n, :] = jnp.sum(k_ref[n * L:(n + 1) * L, :].astype(F32), axis=0,
                                                  keepdims=True) * (1.0 / L)

    nb = -(-nkb // 8) * 8
    lane = lax.broadcasted_iota(jnp.int32, (tq, 128), 1)
    blk = lax.broadcasted_iota(jnp.int32, (nb, tq), 0)
    half = jnp.where(lax.broadcasted_iota(jnp.int32, (nb, tq), 1) >= L, 1, 0)
    blk_f = blk.astype(F32)
    for t, tile in enumerate((ia, nq - 1 - ia)):
        own = 2 * tile + half
        for j in range(2):
            qj = q_ref[pl.ds(pl.multiple_of(tile * tq, tq), tq), 128 * j:128 * (j + 1)]
            gsc = lax.dot_general(kmean_ref[64:64 + nb, 128 * j:128 * (j + 1)], qj.astype(F32),
                                  (((1,), (1,)), ((), ())), preferred_element_type=F32,
                                  precision=lax.Precision.HIGHEST)
            g = jnp.where(blk < own, gsc, NEG)
            allowed = jnp.where(blk == own, 1.0, 0.0)
            for _ in range(MOBA_TOPK):
                mx = jnp.max(g, axis=0, keepdims=True)
                idx = jnp.min(jnp.where(g == mx, blk_f, 1e6), axis=0, keepdims=True)
                pick = blk_f == idx
                allowed = jnp.maximum(allowed, jnp.where(pick, jnp.where(mx > 0.5 * NEG, 1.0, 0.0), 0.0))
                g = jnp.where(pick, 2.0 * NEG, g)
            mask = jnp.concatenate([jnp.zeros((64, tq), F32), jnp.where(allowed > 0.5, 0.0, NEG),
                                    jnp.zeros((64 - nb, tq), F32)], axis=0)
            qa_ref[t, j] = jnp.where(lane < HD, qj, mask.T.astype(BF16))

    def bias_of(start, q0, j):
        parts = []
        for c in range(tq // L):
            off = jnp.minimum(q0 - start - c * L, BIAS_CONST_FROM) + BIAS_PAD
            parts.append(tb_ref[j, :, pl.ds(pl.multiple_of(off, L), tq)])
        return jnp.concatenate(parts, axis=0)

    _causal_attention_pair_t(ia, nq, lambda t, q0, j: qa_ref[t, j], k_ref, vt_ref, o_ref, s_ref, smax_ref, m_ref,
                             acc_ref, tq=tq, bias_of=bias_of)


def _moba_attention(q, k, vt, tb, B, T):
    tq = MOBA_TQ
    nq = T // tq
    return pl.pallas_call(
        functools.partial(_moba_kernel, nkb=T // MOBA_L, nq=nq),
        grid=(B, 4, nq // 2),
        in_specs=[pl.BlockSpec((T, 256), lambda b, p, i: (b, p)),
                  pl.BlockSpec((T, 256), lambda b, p, i: (b, p)),
                  pl.BlockSpec((None, 2 * VT_ROWS, T), lambda b, p, i: (b, p, 0)),
                  pl.BlockSpec((2, MOBA_L, BIAS_ROWS), lambda b, p, i: (p, 0, 0))],
        out_specs=pl.BlockSpec((T, 128), lambda b, p, i: (b, p)),
        out_shape=jax.ShapeDtypeStruct((B * T, 512), BF16),
        scratch_shapes=[pltpu.VMEM((128, 256), F32), pltpu.VMEM((2, 2, tq, 128), BF16)]
        + _attention_t_scratch(tq),
        compiler_params=_cparams(("arbitrary", "arbitrary", "arbitrary")),
        name="moba_attention",
    )(q, k, vt, tb)


def _sb_kernel(q_ref, k_ref, vt_ref, o_ref, *, tq, pairs):
    i = pl.program_id(2)
    lane = lax.broadcasted_iota(jnp.int32, (tq, 128), 1)
    qs = []
    for p in range(pairs):
        q_pair = q_ref[:, 128 * p:128 * (p + 1)]
        qs += [jnp.where(lane < HD, q_pair, 0).astype(BF16), jnp.where(lane >= HD, q_pair, 0).astype(BF16)]
    key = lax.broadcasted_iota(jnp.int32, (tq, tq), 0)
    qry = lax.broadcasted_iota(jnp.int32, (tq, tq), 1)
    past = key < qry
    suffix = jnp.where(qry >= key, -1.0, 0.0).astype(BF16)
    heads = range(2 * pairs)

    def step(kb, carry, boundary):
        start = pl.multiple_of(kb * tq, tq)
        zs = [_dot_nt(k_ref[pl.ds(start, tq), 128 * (h // 2):128 * (h // 2 + 1)], qs[h]) for h in heads]
        parts = []
        for z in zs:
            neg_abs = pltpu.bitcast(pltpu.bitcast(z, jnp.uint32) | jnp.uint32(0x80000000), F32)
            soft = jnp.maximum(z, 0.0) + jnp.log2(1.0 + jnp.exp2(neg_abs))
            if boundary:
                soft = jnp.where(past, soft, 0.0)
            parts.append(soft.astype(BF16))
        sums = [_dot(suffix, part) for part in parts]
        ws = []
        for h in heads:
            w = jnp.exp2(zs[h] + (carry[h][0] + sums[h]))
            ws.append((jnp.where(past, w, 0.0) if boundary else w).astype(BF16))
        return tuple((carry[h][0] + sums[h][0:1], carry[h][1]
                      + _dot(vt_ref[128 * (h // 2):128 * (h // 2 + 1), pl.ds(start, tq)], ws[h])) for h in heads)

    def live(carry):
        top = carry[0][0]
        for c, _ in carry[1:]:
            top = jnp.maximum(top, c)
        return (jnp.max(top) > SB_DONE).astype(jnp.int32)

    def body(state):
        t, _, carry = state
        carry = step(i - 1 - t, carry, False)
        return t + 1, live(carry), carry

    init = tuple((jnp.zeros((1, tq), F32), jnp.zeros((128, tq), F32)) for _ in range(2 * pairs))
    carry = step(i, init, True)
    _, _, carry = lax.while_loop(lambda st: (st[0] < i) & (st[1] > 0), body, (jnp.int32(0), live(carry), carry))
    rows = lax.broadcasted_iota(jnp.int32, (128, tq), 0)
    for p in range(pairs):
        ot = jnp.where(rows < HD, carry[2 * p][1], carry[2 * p + 1][1])
        o_ref[:, 128 * p:128 * (p + 1)] = ot.T.astype(BF16)


def _sb_attention(q, k, vt, B, T, tq=256, pairs=4):
    nq = T // tq
    w = 128 * pairs
    return pl.pallas_call(
        functools.partial(_sb_kernel, tq=tq, pairs=pairs),
        grid=(B, 4 // pairs, nq),
        in_specs=[pl.BlockSpec((tq, w), lambda b, p, i: (b * nq + i, p)),
                  pl.BlockSpec((T, w), lambda b, p, i: (b, p)),
                  pl.BlockSpec((None, w, T), lambda b, p, i: (b, p, 0))],
        out_specs=pl.BlockSpec((tq, w), lambda b, p, i: (b * nq + i, p)),
        out_shape=jax.ShapeDtypeStruct((B * T, 512), BF16),
        compiler_params=_cparams(("arbitrary", "arbitrary", "arbitrary")),
        name="sb_attention",
    )(q, k, vt)


def _router_weights(w_group, b_group, w_router, b_router):
    pad = 128 - N_EXP - N_GROUPS
    wr = jnp.concatenate([w_router, w_group, jnp.zeros((D, pad), F32)], axis=1).T
    hi = wr.astype(BF16)
    lo = (wr - hi.astype(F32)).astype(BF16)
    br = jnp.concatenate([b_router, b_group, jnp.zeros((pad,), F32)])[:, None]
    return jnp.concatenate([hi, lo], axis=0), br


def _ffn(h, oa, ob, wo, g, w_group, b_group, w_router, b_router, w_gate, w_up, w_down, layer):
    wr, br = _router_weights(w_group, b_group, w_router, b_router)
    hn, u, ri, rg, cnt = _outproj_router(h, oa, ob, wo.astype(BF16), g[None, :], wr, br)
    d1, d2, P, plan = _dispatch_plan(ri, cnt, h.shape[0])
    xs = _sc_scatter_rows(u, d1, d2, P)
    ys = _moe_experts(xs, plan, w_gate, w_up, w_down, layer)
    return hn, _sc_gather_rows(ys, jnp.concatenate([d1, d2])), rg


def kernel(x, norm_mix, norm_ffn, norm_final, rel_bias_table, w_in_even, g_mla_q, g_mla_kv, w_mla_qb, w_mla_kvb, swa_sinks, w_out_even, w_in_odd, w_out_odd, moe_w_group, moe_b_group, moe_w_router, moe_b_router, moe_w_gate, moe_w_up, moe_w_down):
    B, T, _ = x.shape
    h = x.reshape(B * T, D)
    tb_swa, tb_moba = _bias_tables(rel_bias_table)

    win, wqa, wqb, wk, wv = _even_weights(w_in_even[0], w_mla_qb[0], w_mla_kvb[0])
    q, k, v, qs8, ks, vst = _proj_even(h, norm_mix[0][None, :], win, g_mla_q[0][None, :], g_mla_kv[0][None, :],
                                       wqa, wqb, wk, wv, _rope_tables(T), T)
    oa = _mla_attention(q, k, v, B, T)
    ob = _swa_attention(qs8, ks, vst, swa_sinks[0], tb_swa, B, T)
    h, y12, rg = _ffn(h, oa, ob, w_out_even[0], norm_ffn[0], moe_w_group[0], moe_b_group[0], moe_w_router[0],
                      moe_b_router[0], moe_w_gate, moe_w_up, moe_w_down, 0)

    h, vct, vdt, qc, kc, qd, kd = _combine_proj_odd(h, y12, rg, norm_mix[1][None, :], *_odd_weights(w_in_odd[0]),
                                                   _block_indicator(T), T)
    oc = _moba_attention(qc, kc, vct, tb_moba, B, T)
    od = _sb_attention(qd, kd, vdt, B, T)
    h, y12, rg = _ffn(h, oc, od, w_out_odd[0], norm_ffn[1], moe_w_group[1], moe_b_group[1], moe_w_router[1],
                      moe_b_router[1], moe_w_gate, moe_w_up, moe_w_down, 1)
    out = _combine_final(h, y12, rg, norm_final[None, :])
    return out.reshape(B, T, D)
```

```python
import functools
import math

import numpy as np
import jax
import jax.numpy as jnp
from jax import lax
from jax.experimental import pallas as pl
from jax.experimental.pallas import tpu as pltpu
from jax.experimental.pallas import tpu_sc as plsc

F32 = jnp.float32
BF16 = jnp.bfloat16
NEG = -1e30
EPS = 1e-6

D = 1024
HD = 64
MLA_HEADS, MLA_QR, MLA_KVR, MLA_NOPE, MLA_ROPE, MLA_V = 8, 256, 128, 64, 32, 64
ROPE_THETA = 10000.0
SWA_HEADS, SWA_KV, SWA_WIN = 8, 2, 128
MOBA_L, MOBA_TOPK = 256, 3
REL_BUCKETS, REL_MAX = 32, 2048
N_GROUPS, EPG, N_EXP, D_EXP = 4, 8, 32, 256
BIAS_CONST_FROM = 1792
MOBA_TQ = 2 * MOBA_L
BIAS_PAD = MOBA_L
BIAS_ROWS = BIAS_PAD + BIAS_CONST_FROM + MOBA_TQ
SB_DONE = -160.0
TM = 1024
TM_E = 512
VMEM_LIMIT = 56 * 1024 * 1024
SC_CORES, SC_SUBCORES = 2, 16
SC_CHUNK = 128
HALF = D // 2
VT_ROWS = HD + 16
LOG2E = math.log2(math.e)


def _pack_halves(x):
    lo = pltpu.bitcast(x[:, :HALF].astype(BF16).astype(F32), jnp.uint32) >> 16
    hi = pltpu.bitcast(x[:, HALF:].astype(BF16).astype(F32), jnp.uint32) & jnp.uint32(0xFFFF0000)
    return pltpu.bitcast(lo | hi, jnp.int32)


def _unpack_halves(w):
    w = pltpu.bitcast(w, jnp.uint32)
    return pltpu.bitcast(w << 16, F32), pltpu.bitcast(w & jnp.uint32(0xFFFF0000), F32)


def _cparams(sem):
    return pltpu.CompilerParams(dimension_semantics=sem, vmem_limit_bytes=VMEM_LIMIT)


def _dot(a, b):
    return jnp.dot(a, b, preferred_element_type=F32)


def _dot_nt(a, b):
    return lax.dot_general(a, b, (((1,), (1,)), ((), ())), preferred_element_type=F32)


def _rms(x, g):
    return x * lax.rsqrt(jnp.mean(x * x, axis=-1, keepdims=True) + EPS) * g


def _values_t(vt):
    ones = jnp.ones((VT_ROWS - HD, vt.shape[1]), vt.dtype)
    return jnp.concatenate([blk for h in range(8) for blk in (vt[HD * h:HD * (h + 1)], ones)], axis=0)


def _spread_heads(x, lane_of):
    lane = lax.broadcasted_iota(jnp.int32, (x.shape[0], 128), 1)
    out = []
    for h in range(8):
        pair = x[:, 128 * (h // 2):128 * (h // 2 + 1)]
        if HD * (h % 2) != lane_of(h):
            pair = pltpu.roll(pair, HD, 1)
        out.append(jnp.where((lane >= lane_of(h)) & (lane < lane_of(h) + HD), pair, 0.0))
    return out


def _t5_bucket_np(dist):
    n = np.maximum(dist, 0)
    max_exact = REL_BUCKETS // 2
    nf = np.maximum(n, 1).astype(np.float32)
    large = max_exact + (np.log(nf / np.float32(max_exact)) / np.float32(math.log(REL_MAX / max_exact))
                         * np.float32(REL_BUCKETS - max_exact)).astype(np.int32)
    large = np.minimum(large, REL_BUCKETS - 1)
    return np.where(n < max_exact, n, large).astype(np.int32)


def _bias_kernel(table_ref, bucket_ref, out_ref, *, scale, ranges):
    b = bucket_ref[...]
    for m, (lo, hi) in enumerate(ranges):
        @pl.when(pl.program_id(0) == m)
        def _():
            for h in range(8):
                acc = jnp.full(b.shape, NEG if lo == REL_BUCKETS else table_ref[lo, h] * scale, F32)
                for k in range(lo + 1, min(hi, REL_BUCKETS - 1) + 1):
                    acc = jnp.where(b == k, table_ref[k, h] * scale, acc)
                if hi == REL_BUCKETS:
                    acc = jnp.where(b == REL_BUCKETS, NEG, acc)
                out_ref[h] = acc


def _bias_lookup(rel_table, dist, scale, name, masked=None):
    R, C = dist.shape
    buckets = _t5_bucket_np(dist)
    if masked is not None:
        buckets = np.where(masked, REL_BUCKETS, buckets).astype(np.int32)
    ranges = tuple((int(buckets[:, c:c + 256].min()), int(buckets[:, c:c + 256].max())) for c in range(0, C, 256))
    return pl.pallas_call(
        functools.partial(_bias_kernel, scale=scale, ranges=ranges),
        grid=(C // 256,),
        in_specs=[pl.BlockSpec(memory_space=pltpu.SMEM),
                  pl.BlockSpec((R, 256), lambda m: (0, m))],
        out_specs=pl.BlockSpec((8, R, 256), lambda m: (0, 0, m)),
        out_shape=jax.ShapeDtypeStruct((8, R, C), F32),
        compiler_params=_cparams(("arbitrary",)),
        name=name,
    )(rel_table, jnp.asarray(buckets))


def _bias_tables(rel_table):
    d = np.arange(256)[None, :] - np.arange(256)[:, None]
    swa = _bias_lookup(rel_table, d, LOG2E, "bias_swa", masked=(d < 0) | (d >= SWA_WIN))
    moba = _bias_lookup(rel_table, np.arange(BIAS_ROWS)[None, :] - BIAS_PAD - np.arange(MOBA_L)[:, None], LOG2E,
                        "bias_moba")
    return swa, moba


def _proj_even_kernel(h_ref, g_ref, win_ref, gq_ref, gkv_ref, wqa_ref, wqb_ref, wk_ref, wv_ref,
                      cq_ref, sq_ref, ck_ref, sk_ref,
                      q_ref, k_ref, v_ref, qs_ref, ks_ref, vst_ref):
    u = _rms(h_ref[...], g_ref[...]).astype(BF16)
    nq = _rms(_dot(u, win_ref[:, 0:256]), gq_ref[...]).astype(BF16)
    qa = _dot(nq, wqa_ref[...])
    qb = _dot(nq, wqb_ref[...])
    cq, sq = cq_ref[...], sq_ref[...]
    for hh in range(MLA_HEADS):
        sl = slice(128 * hh, 128 * (hh + 1))
        q_ref[:, sl] = (qa[:, sl] * cq + qb[:, sl] * sq).astype(BF16)
    nkv = _rms(_dot(u, win_ref[:, 256:384]), gkv_ref[...]).astype(BF16)
    kn = _dot(nkv, wk_ref[...])
    v_ref[...] = _values_t(_dot_nt(wv_ref[...], nkv)).astype(BF16)
    rest = _dot(u, win_ref[:, 896:1408])
    kr = rest[:, 128:256] * ck_ref[...] + rest[:, 256:384] * sk_ref[...]
    for hh in range(MLA_HEADS):
        sl = slice(128 * hh, 128 * (hh + 1))
        k_ref[:, sl] = (kn[:, sl] + kr).astype(BF16)
    qs = _spread_heads(_dot(u, win_ref[:, 384:896]), lambda hq: HD * (hq // (SWA_HEADS // SWA_KV)))
    for hq in range(SWA_HEADS):
        qs_ref[:, 128 * hq:128 * (hq + 1)] = qs[hq].astype(BF16)
    ks_ref[...] = rest[:, 0:128].astype(BF16)
    vst_ref[...] = rest[:, 384:512].T.astype(BF16)


def _cols(w, idx, scale=None):
    idx = np.asarray(idx)
    cuts = [0] + [n for n in range(1, len(idx)) if (idx[n] < 0) != (idx[n - 1] < 0)
                  or (idx[n] >= 0 and idx[n] != idx[n - 1] + 1)] + [len(idx)]
    parts = [jnp.zeros((w.shape[0], b - a), w.dtype) if idx[a] < 0 else w[:, idx[a]:idx[a] + b - a]
             for a, b in zip(cuts[:-1], cuts[1:])]
    out = jnp.concatenate(parts, axis=1)
    if scale is not None:
        out = out * scale
    return out


def _even_weights(w_in, w_qb, w_kvb):
    o_cq, o_ckv, o_kr = 0, MLA_QR, MLA_QR + MLA_KVR
    o_qs = o_kr + MLA_ROPE
    o_ks = o_qs + SWA_HEADS * HD
    o_vs = o_ks + SWA_KV * HD
    idx = list(range(o_cq, o_cq + MLA_QR)) + list(range(o_ckv, o_ckv + MLA_KVR))
    scale = [1.0] * len(idx)
    idx += list(range(o_qs, o_qs + SWA_HEADS * HD))
    scale += [HD ** -0.5 * LOG2E] * (SWA_HEADS * HD)
    idx += list(range(o_ks, o_ks + 128))
    scale += [1.0] * 128
    half = MLA_ROPE // 2
    idx += [-1] * 64 + list(range(o_kr, o_kr + MLA_ROPE)) + [-1] * 32
    idx += [-1] * 64 + list(range(o_kr + half, o_kr + MLA_ROPE)) + list(range(o_kr, o_kr + half)) + [-1] * 32
    idx += list(range(o_vs, o_vs + 128))
    scale += [1.0] * 384
    win = _cols(w_in, idx, jnp.asarray(np.asarray(scale, np.float32))).astype(BF16)
    qw = MLA_NOPE + MLA_ROPE
    ia, ib, ik, iv = [], [], [], []
    for h in range(MLA_HEADS):
        ia += list(range(qw * h, qw * h + qw)) + [-1] * 32
        ib += [-1] * 64 + list(range(qw * h + MLA_NOPE + half, qw * h + qw)) \
            + list(range(qw * h + MLA_NOPE, qw * h + MLA_NOPE + half)) + [-1] * 32
        ik += list(range(128 * h, 128 * h + MLA_NOPE)) + [-1] * 64
        iv += list(range(128 * h + MLA_NOPE, 128 * h + 128))
    return (win, _cols(w_qb, ia).astype(BF16), _cols(w_qb, ib).astype(BF16),
            _cols(w_kvb, ik).astype(BF16), _cols(w_kvb, iv).T.astype(BF16))


def _rope_tables(T):
    f32 = np.float32
    freqs = f32(ROPE_THETA) ** (-np.arange(0, MLA_ROPE, 2, dtype=f32) / f32(MLA_ROPE))
    ang = np.arange(T, dtype=f32)[:, None] * freqs[None, :]
    cos, sin = np.cos(ang), np.sin(ang)
    z = lambda n: np.zeros((T, n), f32)
    scale = f32((MLA_NOPE + MLA_ROPE) ** -0.5 * LOG2E)
    ck = np.concatenate([z(64), cos, cos, z(32)], axis=1)
    sk = np.concatenate([z(64), -sin, sin, z(32)], axis=1)
    cq = np.concatenate([np.ones((T, 64), f32), cos, cos, z(32)], axis=1) * scale
    sq = sk * scale
    return tuple(jnp.asarray(t, F32) for t in (cq, sq, ck, sk))


def _proj_even(h, g, win, gq, gkv, wqa, wqb, wk, wv, tables, T):
    N = h.shape[0]
    nt = T // TM
    full = lambda a: pl.BlockSpec(a.shape, lambda i: (0,) * a.ndim)
    tab = pl.BlockSpec((TM, 128), lambda i: (i % nt, 0))
    row = lambda w: pl.BlockSpec((TM, w), lambda i: (i, 0))
    outs = (1024, 1024, -8 * VT_ROWS, 1024, 128, -128)
    spec = lambda w: row(w) if w > 0 else pl.BlockSpec((None, -w, TM), lambda i: (i // nt, 0, i % nt))
    shape = lambda w: jax.ShapeDtypeStruct((N, w) if w > 0 else (N // T, -w, T), BF16)
    return pl.pallas_call(
        _proj_even_kernel,
        grid=(N // TM,),
        in_specs=[row(D), full(g), full(win), full(gq), full(gkv), full(wqa), full(wqb), full(wk), full(wv),
                  tab, tab, tab, tab],
        out_specs=[spec(w) for w in outs],
        out_shape=[shape(w) for w in outs],
        compiler_params=_cparams(("arbitrary",)),
        name="proj_even",
    )(h, g, win, gq, gkv, wqa, wqb, wk, wv, *tables)


def _finish_t(accs):
    ot = jnp.concatenate([acc[:HD] / acc[HD:HD + 1] for acc in accs], axis=0)
    return ot.T.astype(BF16)


def _causal_attention_pair_t(ia, nq, q_of, k_ref, vt_ref, o_ref, s_ref, smax_ref, m_ref, acc_ref, *, tq,
                             bias_of=None):
    nh = s_ref.shape[1]
    first_query = (ia * tq, (nq - 1 - ia) * tq)
    n_items = nq + 1
    assert n_items % 2 == 1
    key = lax.broadcasted_iota(jnp.int32, (tq, tq), 0)
    qry = lax.broadcasted_iota(jnp.int32, (tq, tq), 1)

    def item(p):
        if isinstance(p, int) and p < 2:
            return p, first_query[p], first_query[p]
        t = jnp.where(p - 2 >= ia, 1, 0)
        return t, jnp.where(t == 1, first_query[1], first_query[0]), (p - 2 - t * ia) * tq

    def scores_to(slot, p):
        t, q0, start = item(p)
        start = pl.multiple_of(start, tq)
        for j in range(nh):
            st = _dot_nt(k_ref[pl.ds(start, tq), 128 * j:128 * (j + 1)], q_of(t, q0, j))
            if bias_of is not None:
                st = st + bias_of(start, q0, j)
            if isinstance(p, int) and p < 2:
                st = jnp.where(key <= qry, st, NEG)
            s_ref[slot, j] = st
            smax_ref[slot, j] = jnp.max(st, axis=0, keepdims=True)

    def update_from(slot, p):
        t, _, start = item(p)
        start = pl.multiple_of(start, tq)
        m_new = [jnp.maximum(m_ref[t, j], smax_ref[slot, j]) for j in range(nh)]
        pts = [jnp.exp2(s_ref[slot, j] - m_new[j]).astype(BF16) for j in range(nh)]
        for j in range(nh):
            acc_ref[t, j] = (jnp.exp2(m_ref[t, j] - m_new[j]) * acc_ref[t, j]
                             + _dot(vt_ref[VT_ROWS * j:VT_ROWS * (j + 1), pl.ds(start, tq)], pts[j]))
            m_ref[t, j] = m_new[j]

    def pair(k):
        scores_to(1, k + 1)
        update_from(0, k)
        scores_to(0, k + 2)
        update_from(1, k + 1)

    def quad(c, _):
        pair(2 + 4 * c)
        pair(4 + 4 * c)
        return 0

    m_ref[...] = jnp.full(m_ref.shape, NEG, F32)
    acc_ref[...] = jnp.zeros(acc_ref.shape, F32)
    scores_to(0, 0)
    pair(0)
    later_pairs = (n_items - 1) // 2 - 1
    lax.fori_loop(0, later_pairs // 2, quad, 0)
    if later_pairs % 2:
        pair(n_items - 3)
    update_from(0, n_items - 1)
    for t in range(2):
        o_ref[pl.ds(pl.multiple_of(first_query[t], tq), tq), :] = _finish_t([acc_ref[t, j] for j in range(nh)])


def _attention_t_scratch(tq):
    return [pltpu.VMEM((2, 2, tq, tq), F32), pltpu.VMEM((2, 2, 1, tq), F32), pltpu.VMEM((2, 2, 1, tq), F32),
            pltpu.VMEM((2, 2, VT_ROWS, tq), F32)]


def _mla_kernel(q_ref, k_ref, vt_ref, o_ref, s_ref, smax_ref, m_ref, acc_ref, *, tq, nq):
    def q_of(t, q0, j):
        return q_ref[pl.ds(pl.multiple_of(q0, tq), tq), 128 * j:128 * (j + 1)]

    _causal_attention_pair_t(pl.program_id(2), nq, q_of, k_ref, vt_ref, o_ref, s_ref, smax_ref, m_ref, acc_ref,
                             tq=tq)


def _mla_attention(q, k, vt, B, T, tq=512):
    nq = T // tq
    return pl.pallas_call(
        functools.partial(_mla_kernel, tq=tq, nq=nq),
        grid=(B, MLA_HEADS // 2, nq // 2),
        in_specs=[pl.BlockSpec((T, 256), lambda b, p, i: (b, p)),
                  pl.BlockSpec((T, 256), lambda b, p, i: (b, p)),
                  pl.BlockSpec((None, 2 * VT_ROWS, T), lambda b, p, i: (b, p, 0))],
        out_specs=pl.BlockSpec((T, 128), lambda b, p, i: (b, p)),
        out_shape=jax.ShapeDtypeStruct((B * T, MLA_HEADS * MLA_V), BF16),
        scratch_shapes=_attention_t_scratch(tq),
        compiler_params=_cparams(("arbitrary", "arbitrary", "arbitrary")),
        name="mla_attention",
    )(q, k, vt)


def _swa_kernel(sink_ref, q_ref, k_ref, vt_ref, tb_ref, o_ref, *, tq):
    i = pl.program_id(1)
    G = SWA_HEADS // SWA_KV
    W = 2 * SWA_WIN
    head = lax.broadcasted_iota(jnp.int32, (1, G * 128), 1) >> 7
    sinks = []
    for c in range(SWA_KV):
        sink = jnp.zeros((1, G * 128), F32)
        for g in range(G):
            sink = jnp.where(head == g, sink_ref[G * c + g], sink)
        sinks.append(sink)
    scores, values = [], []
    for r in range(tq // 128):
        qstart = i * tq + 128 * r
        kstart = pl.multiple_of(jnp.maximum(qstart - SWA_WIN, 0), 128)
        off = pl.multiple_of(qstart - kstart, 128)
        kw = k_ref[pl.ds(kstart, W), :]
        for c in range(SWA_KV):
            qg = jnp.concatenate([q_ref[128 * r:128 * (r + 1), 128 * (G * c + g):128 * (G * c + g + 1)]
                                  for g in range(G)], axis=0)
            bias = jnp.concatenate([tb_ref[G * c + g, :, pl.ds(off, 128)] for g in range(G)], axis=1)
            scores.append(_dot_nt(kw, qg) + bias)
            values.append(vt_ref[HD * c:HD * (c + 1), pl.ds(kstart, W)])
    probs = []
    for n, s in enumerate(scores):
        sink = sinks[n % SWA_KV]
        m = jnp.maximum(jnp.max(s, axis=0, keepdims=True), sink)
        p = jnp.exp2(s - m)
        probs.append((p.astype(BF16), jnp.sum(p, axis=0, keepdims=True) + jnp.exp2(sink - m)))
    for r in range(tq // 128):
        outs = []
        for c in range(SWA_KV):
            p, denom = probs[SWA_KV * r + c]
            ot = _dot(values[SWA_KV * r + c], p) / denom
            outs += [ot[:, 128 * g:128 * (g + 1)] for g in range(G)]
        o_ref[128 * r:128 * (r + 1), :] = jnp.concatenate(outs, axis=0).T.astype(BF16)


def _swa_attention(q8, k, vt, sinks, tb, B, T, tq=512):
    nq = T // tq
    return pl.pallas_call(
        functools.partial(_swa_kernel, tq=tq),
        grid=(B, nq),
        in_specs=[pl.BlockSpec(memory_space=pltpu.SMEM),
                  pl.BlockSpec((tq, 1024), lambda b, i: (b * nq + i, 0)),
                  pl.BlockSpec((T, 128), lambda b, i: (b, 0)),
                  pl.BlockSpec((None, 128, T), lambda b, i: (b, 0, 0)),
                  pl.BlockSpec((8, 256, 256), lambda b, i: (0, 0, 0))],
        out_specs=pl.BlockSpec((tq, 512), lambda b, i: (b * nq + i, 0)),
        out_shape=jax.ShapeDtypeStruct((B * T, SWA_HEADS * HD), BF16),
        compiler_params=_cparams(("arbitrary", "arbitrary")),
        name="swa_attention",
    )(sinks * LOG2E, q8, k, vt, tb)


def _outproj_router_kernel(h_ref, oa_ref, ob_ref, wo_ref, g_ref, wr_ref, br_ref,
                           hn_ref, u_ref, ri_ref, rg_ref, cnt_ref, base_ref):
    i = pl.program_id(0)

    @pl.when(i == 0)
    def _():
        base_ref[...] = jnp.zeros_like(base_ref)

    hn = h_ref[...] + _dot(oa_ref[...], wo_ref[0:512, :]) + _dot(ob_ref[...], wo_ref[512:1024, :])
    hn_ref[...] = hn
    u = _rms(hn, g_ref[...])
    u_ref[...] = _pack_halves(u)
    u_hi = u.astype(BF16)
    u_lo = (u - u_hi.astype(F32)).astype(BF16)
    both = _dot_nt(wr_ref[...], u_hi)
    logits = both[:128] + both[128:] + _dot_nt(wr_ref[0:128, :], u_lo) + br_ref[...]
    tm = logits.shape[1]
    sub = lax.broadcasted_iota(jnp.int32, (8, tm), 0).astype(F32)
    big = 1e6
    isg = sub < N_GROUPS
    gl = jnp.where(isg, logits[N_EXP:N_EXP + 8], NEG)
    gmax = jnp.max(gl, axis=0, keepdims=True)
    gsel = jnp.min(jnp.where(gl == gmax, sub, big), axis=0, keepdims=True)
    g_gate = 1.0 / jnp.sum(jnp.where(isg, jnp.exp(gl - gmax), 0.0), axis=0, keepdims=True)
    el = jnp.zeros((EPG, tm), F32)
    for grp in range(N_GROUPS):
        el = jnp.where(gsel == grp, logits[EPG * grp:EPG * (grp + 1)], el)
    m1 = jnp.max(el, axis=0, keepdims=True)
    i1 = jnp.min(jnp.where(el == m1, sub, big), axis=0, keepdims=True)
    el2 = jnp.where(sub == i1, NEG, el)
    m2 = jnp.max(el2, axis=0, keepdims=True)
    i2 = jnp.min(jnp.where(el2 == m2, sub, big), axis=0, keepdims=True)
    e1 = gsel * EPG + i1
    e2 = gsel * EPG + i2
    r = jnp.exp(m2 - m1)
    gate1 = g_gate / (1.0 + r)
    gate2 = g_gate * r / (1.0 + r)
    rows = lax.broadcasted_iota(jnp.int32, (128, tm), 0).astype(F32)
    oh1 = (rows == e1).astype(F32)
    oh2 = (rows == e2).astype(F32)
    oh = oh1 + oh2
    t_row = lax.broadcasted_iota(jnp.int32, (tm, tm), 0)
    t_col = lax.broadcasted_iota(jnp.int32, (tm, tm), 1)
    earlier = jnp.where(t_row < t_col, 1.0, 0.0).astype(BF16)
    prefix = _dot(oh.astype(BF16), earlier) + base_ref[:, 0:1]
    rank1 = jnp.sum(oh1 * prefix, axis=0, keepdims=True)
    rank2 = jnp.sum(oh2 * prefix, axis=0, keepdims=True)
    base_ref[...] = base_ref[...] + jnp.sum(oh, axis=1, keepdims=True)
    cnt_ref[...] = base_ref[...]
    ri_ref[...] = jnp.where(sub == 0, e1, jnp.where(sub == 1, e2, jnp.where(sub == 2, rank1,
                            jnp.where(sub == 3, rank2, 0.0))))
    rg_ref[...] = jnp.where(rows == 0, gate1, jnp.where(rows == 1, gate2, 0.0)).T


def _outproj_router(h, oa, ob, wo, g, wr, br):
    N = h.shape[0]
    full = lambda a: pl.BlockSpec(a.shape, lambda i: (0,) * a.ndim)
    row = lambda w: pl.BlockSpec((TM, w), lambda i: (i, 0))
    return pl.pallas_call(
        _outproj_router_kernel,
        grid=(N // TM,),
        in_specs=[row(D), row(512), row(512), full(wo), full(g), full(wr), full(br)],
        out_specs=[row(D), row(HALF), pl.BlockSpec((8, TM), lambda i: (0, i)), row(128),
                   pl.BlockSpec((128, 128), lambda i: (0, 0))],
        out_shape=[jax.ShapeDtypeStruct((N, D), F32), jax.ShapeDtypeStruct((N, HALF), jnp.int32),
                   jax.ShapeDtypeStruct((8, N), F32), jax.ShapeDtypeStruct((N, 128), F32),
                   jax.ShapeDtypeStruct((128, 128), F32)],
        scratch_shapes=[pltpu.VMEM((128, 128), F32)],
        compiler_params=_cparams(("arbitrary",)),
        name="outproj_router",
    )(h, oa, ob, wo, g, wr, br)


def _moe_kernel(te_ref, tv_ref, run_ref, nx_ref, x_ref, wg_hbm, wu_hbm, wd_hbm, o_ref,
                wg_buf, wu_buf, wd_buf, sems, wgu_s, wd_s, *, layer):
    i = pl.program_id(0)
    valid = tv_ref[i]

    def weight_copies(e, slot):
        return (pltpu.make_async_copy(wg_hbm.at[layer, e], wg_buf.at[slot], sems.at[slot, 0]),
                pltpu.make_async_copy(wu_hbm.at[layer, e], wu_buf.at[slot], sems.at[slot, 1]),
                pltpu.make_async_copy(wd_hbm.at[layer, e], wd_buf.at[slot], sems.at[slot, 2]))

    @pl.when(run_ref[i] > 0)
    def _():
        slot = run_ref[i] - 1

        @pl.when(i == 0)
        def _():
            for c in weight_copies(te_ref[i], slot):
                c.start()

        for c in weight_copies(te_ref[i], slot):
            c.wait()
        wgu_s[:, :D_EXP] = wg_buf[slot].astype(BF16)
        wgu_s[:, D_EXP:] = wu_buf[slot].astype(BF16)
        wd_s[...] = wd_buf[slot].astype(BF16)

        @pl.when(nx_ref[i] >= 0)
        def _():
            for c in weight_copies(nx_ref[i], 1 - slot):
                c.start()

    @pl.when(valid > 0)
    def _():
        rows = lax.broadcasted_iota(jnp.int32, x_ref.shape, 0)
        lo, hi = _unpack_halves(jnp.where(rows < valid, x_ref[...], 0))
        hgu = _dot(lo.astype(BF16), wgu_s[:HALF, :]) + _dot(hi.astype(BF16), wgu_s[HALF:, :])
        hg, hu = hgu[:, :D_EXP], hgu[:, D_EXP:]
        a = hg * (1.0 / (1.0 + jnp.exp(-hg))) * hu
        o_ref[...] = _pack_halves(_dot(a.astype(BF16), wd_s[...]))

    @pl.when(valid <= 0)
    def _():
        o_ref[...] = jnp.zeros_like(o_ref)


def _moe_experts(xs, plan, w_gate, w_up, w_down, layer):
    P = xs.shape[0]
    hbm = pl.BlockSpec(memory_space=pl.ANY)
    grid_spec = pltpu.PrefetchScalarGridSpec(
        num_scalar_prefetch=4,
        grid=(P // TM_E,),
        in_specs=[pl.BlockSpec((TM_E, HALF), lambda i, *_: (i, 0)), hbm, hbm, hbm],
        out_specs=pl.BlockSpec((TM_E, HALF), lambda i, *_: (i, 0)),
        scratch_shapes=[pltpu.VMEM((2, D, D_EXP), F32), pltpu.VMEM((2, D, D_EXP), F32),
                        pltpu.VMEM((2, D_EXP, D), F32), pltpu.SemaphoreType.DMA((2, 3)),
                        pltpu.VMEM((D, 2 * D_EXP), BF16), pltpu.VMEM((D_EXP, D), BF16)],
    )
    return pl.pallas_call(
        functools.partial(_moe_kernel, layer=layer),
        grid_spec=grid_spec,
        out_shape=jax.ShapeDtypeStruct((P, HALF), jnp.int32),
        compiler_params=_cparams(("arbitrary",)),
        name="moe_experts",
    )(*plan, xs, w_gate, w_up, w_down)


def _sc_mesh():
    return plsc.VectorSubcoreMesh(core_axis_name="c", subcore_axis_name="s",
                                  num_cores=SC_CORES, num_subcores=SC_SUBCORES)


def _sc_scatter_rows(src, d1, d2, P):
    N, W = src.shape
    per_w = N // (SC_CORES * SC_SUBCORES)

    @functools.partial(pl.kernel, mesh=_sc_mesh(), out_type=jax.ShapeDtypeStruct((P, W), src.dtype),
                       scratch_types=[pltpu.VMEM((SC_CHUNK,), jnp.int32), pltpu.VMEM((SC_CHUNK, W), src.dtype)],
                       name="sc_scatter_rows")
    def k(src_hbm, d1_hbm, d2_hbm, out_hbm, idx_v, rows_v):
        wid = lax.axis_index("s") * SC_CORES + lax.axis_index("c")

        @pl.loop(0, per_w // SC_CHUNK)
        def _(c):
            off = wid * per_w + c * SC_CHUNK
            pltpu.sync_copy(src_hbm.at[pl.ds(off, SC_CHUNK)], rows_v)
            pltpu.sync_copy(d1_hbm.at[pl.ds(off, SC_CHUNK)], idx_v)
            pltpu.sync_copy(rows_v, out_hbm.at[idx_v])
            pltpu.sync_copy(d2_hbm.at[pl.ds(off, SC_CHUNK)], idx_v)
            pltpu.sync_copy(rows_v, out_hbm.at[idx_v])

    return k(src, d1, d2)


def _sc_gather_rows(table, idx):
    B, W = idx.shape[0], table.shape[1]
    per_w = B // (SC_CORES * SC_SUBCORES)

    @functools.partial(pl.kernel, mesh=_sc_mesh(), out_type=jax.ShapeDtypeStruct((B, W), table.dtype),
                       scratch_types=[pltpu.VMEM((SC_CHUNK,), jnp.int32), pltpu.VMEM((SC_CHUNK, W), table.dtype)],
                       name="sc_gather_rows")
    def k(table_hbm, idx_hbm, out_hbm, idx_v, rows_v):
        wid = lax.axis_index("s") * SC_CORES + lax.axis_index("c")

        @pl.loop(0, per_w // SC_CHUNK)
        def _(c):
            off = wid * per_w + c * SC_CHUNK
            pltpu.sync_copy(idx_hbm.at[pl.ds(off, SC_CHUNK)], idx_v)
            pltpu.sync_copy(table_hbm.at[idx_v], rows_v)
            pltpu.sync_copy(rows_v, out_hbm.at[pl.ds(off, SC_CHUNK)])

    return k(table, idx)


def _dest_kernel(offs_ref, ri_ref, d_ref):
    ri = ri_ref[...]
    base = jnp.zeros(ri.shape, F32)
    for e in range(N_EXP):
        base = jnp.where(ri == e, offs_ref[e].astype(F32), base)
    d_ref[...] = (base + pltpu.roll(ri, 6, 0)).astype(jnp.int32)


def _dispatch_plan(ri, cnt, N):
    counts = cnt[:N_EXP, 0].astype(jnp.int32)
    padded = ((counts + TM_E - 1) // TM_E) * TM_E
    ends = jnp.cumsum(padded)
    offs = ends - padded
    take = lambda t, i: t.at[i].get(mode="promise_in_bounds")
    dest = pl.pallas_call(
        _dest_kernel,
        in_specs=[pl.BlockSpec(memory_space=pltpu.SMEM), pl.BlockSpec((8, N), lambda: (0, 0))],
        out_specs=pl.BlockSpec((8, N), lambda: (0, 0)),
        out_shape=jax.ShapeDtypeStruct((8, N), jnp.int32),
        name="dispatch_rows",
    )(offs, ri)
    d1, d2 = dest[0], dest[1]
    P = 2 * N + N_EXP * TM_E
    tile_start = jnp.arange(P // TM_E, dtype=jnp.int32) * TM_E
    tile_expert = jnp.minimum(jnp.sum((ends[None, :] <= tile_start[:, None]).astype(jnp.int32), axis=1), N_EXP - 1)
    tile_valid = jnp.clip(take(offs + counts, tile_expert) - tile_start, 0, TM_E)
    n_tiles = P // TM_E
    tile = jnp.arange(n_tiles, dtype=jnp.int32)
    starts = (tile_valid > 0) & ((tile == 0) | (tile_expert != jnp.roll(tile_expert, 1)))
    slot = (jnp.cumsum(starts.astype(jnp.int32)) - 1) % 2
    run_code = jnp.where(starts, slot + 1, 0).astype(jnp.int32)
    start_at = jnp.where(starts, tile, n_tiles)
    next_start = jnp.concatenate([lax.cummin(start_at, reverse=True)[1:], jnp.full((1,), n_tiles, jnp.int32)])
    next_expert = jnp.where(next_start < n_tiles, take(tile_expert, jnp.minimum(next_start, n_tiles - 1)), -1)
    return d1, d2, P, (tile_expert, tile_valid, run_code, next_expert.astype(jnp.int32))


def _combine(h_ref, y1_ref, y2_ref, rg_ref):
    rg = rg_ref[...]
    lo1, hi1 = _unpack_halves(y1_ref[...])
    lo2, hi2 = _unpack_halves(y2_ref[...])
    g1, g2 = rg[:, 0:1], rg[:, 1:2]
    return h_ref[...] + jnp.concatenate([g1 * lo1 + g2 * lo2, g1 * hi1 + g2 * hi2], axis=1)


ODD_WIDTHS = (1024, 1024, 512, 512)


def _combine_proj_odd_kernel(h_ref, y1_ref, y2_ref, rg_ref, g_ref, win_ref, wvt_ref, ind_ref, hn_ref, vt_ref,
                             vdt_ref, *out_refs):
    hn = _combine(h_ref, y1_ref, y2_ref, rg_ref)
    hn_ref[...] = hn
    u = _rms(hn, g_ref[...]).astype(BF16)
    vt_ref[...] = _values_t(_dot_nt(wvt_ref[0:512, :], u)).astype(BF16)
    vdt_ref[...] = _dot_nt(wvt_ref[512:1024, :], u).astype(BF16)
    qc_ref, kc_ref, qd_ref, kd_ref = out_refs
    qc = _spread_heads(_dot(u, win_ref[:, 0:512]), lambda h: 0)
    kc = _spread_heads(_dot(u, win_ref[:, 512:1024]), lambda h: 0)
    for hh in range(8):
        qc_ref[:, 128 * hh:128 * (hh + 1)] = qc[hh].astype(BF16)
        kc_ref[:, 128 * hh:128 * (hh + 1)] = (kc[hh] + ind_ref[...]).astype(BF16)
    qd_ref[...] = _dot(u, win_ref[:, 1024:1536]).astype(BF16)
    kd_ref[...] = _dot(u, win_ref[:, 1536:2048]).astype(BF16)


def _odd_weights(w_in):
    s = HD ** -0.5 * LOG2E
    scale = np.asarray([s] * 512 + [1.0] * 512 + [s] * 512 + [1.0] * 512, np.float32)
    win = jnp.concatenate([w_in[:, 0:1024], w_in[:, 1536:2560]], axis=1) * jnp.asarray(scale)
    wvt = jnp.concatenate([w_in[:, 1024:1536], w_in[:, 2560:3072]], axis=1).T
    return win.astype(BF16), wvt.astype(BF16)


def _block_indicator(T):
    assert T // MOBA_L <= 64
    ind = np.zeros((T, 128), np.float32)
    ind[np.arange(T), 64 + np.arange(T) // MOBA_L] = 1.0
    return jnp.asarray(ind)


def _y_specs(N):
    return [pl.BlockSpec((TM, HALF), lambda i: (i, 0)), pl.BlockSpec((TM, HALF), lambda i: (i + N // TM, 0))]


def _combine_proj_odd(h, y12, rg, g, win, wvt, ind, T):
    N = h.shape[0]
    nt = T // TM
    full = lambda a: pl.BlockSpec(a.shape, lambda i: (0,) * a.ndim)
    row = lambda w: pl.BlockSpec((TM, w), lambda i: (i, 0))
    return pl.pallas_call(
        _combine_proj_odd_kernel,
        grid=(N // TM,),
        in_specs=[row(D)] + _y_specs(N) + [row(128), full(g), full(win), full(wvt),
                  pl.BlockSpec((TM, 128), lambda i: (i % nt, 0))],
        out_specs=[row(D), pl.BlockSpec((None, 8 * VT_ROWS, TM), lambda i: (i // nt, 0, i % nt)),
                   pl.BlockSpec((None, 512, TM), lambda i: (i // nt, 0, i % nt))]
        + [row(w) for w in ODD_WIDTHS],
        out_shape=[jax.ShapeDtypeStruct((N, D), F32), jax.ShapeDtypeStruct((N // T, 8 * VT_ROWS, T), BF16),
                   jax.ShapeDtypeStruct((N // T, 512, T), BF16)]
        + [jax.ShapeDtypeStruct((N, w), BF16) for w in ODD_WIDTHS],
        compiler_params=_cparams(("arbitrary",)),
        name="combine_proj_odd",
    )(h, y12, y12, rg, g, win, wvt, ind)


def _combine_final_kernel(h_ref, y1_ref, y2_ref, rg_ref, g_ref, o_ref):
    o_ref[...] = _rms(_combine(h_ref, y1_ref, y2_ref, rg_ref), g_ref[...])


def _combine_final(h, y12, rg, g):
    N = h.shape[0]
    row = lambda w: pl.BlockSpec((TM, w), lambda i: (i, 0))
    return pl.pallas_call(
        _combine_final_kernel,
        grid=(N // TM,),
        in_specs=[row(D)] + _y_specs(N) + [row(128), pl.BlockSpec((1, D), lambda i: (0, 0))],
        out_specs=row(D),
        out_shape=jax.ShapeDtypeStruct((N, D), F32),
        compiler_params=_cparams(("arbitrary",)),
        name="combine_final",
    )(h, y12, y12, rg, g)


def _moba_kernel(q_ref, k_ref, vt_ref, tb_ref, o_ref, kmean_ref, qa_ref, s_ref, smax_ref, m_ref, acc_ref, *, nkb,
                 nq):
    ia = pl.program_id(2)
    L, tq = MOBA_L, MOBA_TQ

    @pl.when(ia == 0)
    def _():
        kmean_ref[...] = jnp.zeros_like(kmean_ref)
        for n in range(nkb):
            kmean_ref[64 + n:65 + n, :] = jnp.sum(k_ref[n * L:(n + 1) * L, :].astype(F32), axis=0,
                                                  keepdims=True) * (1.0 / L)

    nb = -(-nkb // 8) * 8
    lane = lax.broadcasted_iota(jnp.int32, (tq, 128), 1)
    blk = lax.broadcasted_iota(jnp.int32, (nb, tq), 0)
    half = jnp.where(lax.broadcasted_iota(jnp.int32, (nb, tq), 1) >= L, 1, 0)
    blk_f = blk.astype(F32)
    for t, tile in enumerate((ia, nq - 1 - ia)):
        own = 2 * tile + half
        for j in range(2):
            qj = q_ref[pl.ds(pl.multiple_of(tile * tq, tq), tq), 128 * j:128 * (j + 1)]
            gsc = lax.dot_general(kmean_ref[64:64 + nb, 128 * j:128 * (j + 1)], qj.astype(F32),
                                  (((1,), (1,)), ((), ())), preferred_element_type=F32,
                                  precision=lax.Precision.HIGHEST)
            g = jnp.where(blk < own, gsc, NEG)
            allowed = jnp.where(blk == own, 1.0, 0.0)
            for _ in range(MOBA_TOPK):
                mx = jnp.max(g, axis=0, keepdims=True)
                idx = jnp.min(jnp.where(g == mx, blk_f, 1e6), axis=0, keepdims=True)
                pick = blk_f == idx
                allowed = jnp.maximum(allowed, jnp.where(pick, jnp.where(mx > 0.5 * NEG, 1.0, 0.0), 0.0))
                g = jnp.where(pick, 2.0 * NEG, g)
            mask = jnp.concatenate([jnp.zeros((64, tq), F32), jnp.where(allowed > 0.5, 0.0, NEG),
                                    jnp.zeros((64 - nb, tq), F32)], axis=0)
            qa_ref[t, j] = jnp.where(lane < HD, qj, mask.T.astype(BF16))

    def bias_of(start, q0, j):
        parts = []
        for c in range(tq // L):
            off = jnp.minimum(q0 - start - c * L, BIAS_CONST_FROM) + BIAS_PAD
            parts.append(tb_ref[j, :, pl.ds(pl.multiple_of(off, L), tq)])
        return jnp.concatenate(parts, axis=0)

    _causal_attention_pair_t(ia, nq, lambda t, q0, j: qa_ref[t, j], k_ref, vt_ref, o_ref, s_ref, smax_ref, m_ref,
                             acc_ref, tq=tq, bias_of=bias_of)


def _moba_attention(q, k, vt, tb, B, T):
    tq = MOBA_TQ
    nq = T // tq
    return pl.pallas_call(
        functools.partial(_moba_kernel, nkb=T // MOBA_L, nq=nq),
        grid=(B, 4, nq // 2),
        in_specs=[pl.BlockSpec((T, 256), lambda b, p, i: (b, p)),
                  pl.BlockSpec((T, 256), lambda b, p, i: (b, p)),
                  pl.BlockSpec((None, 2 * VT_ROWS, T), lambda b, p, i: (b, p, 0)),
                  pl.BlockSpec((2, MOBA_L, BIAS_ROWS), lambda b, p, i: (p, 0, 0))],
        out_specs=pl.BlockSpec((T, 128), lambda b, p, i: (b, p)),
        out_shape=jax.ShapeDtypeStruct((B * T, 512), BF16),
        scratch_shapes=[pltpu.VMEM((128, 256), F32), pltpu.VMEM((2, 2, tq, 128), BF16)]
        + _attention_t_scratch(tq),
        compiler_params=_cparams(("arbitrary", "arbitrary", "arbitrary")),
        name="moba_attention",
    )(q, k, vt, tb)


def _sb_kernel(q_ref, k_ref, vt_ref, o_ref, *, tq, pairs):
    i = pl.program_id(2)
    lane = lax.broadcasted_iota(jnp.int32, (tq, 128), 1)
    qs = []
    for p in range(pairs):
        q_pair = q_ref[:, 128 * p:128 * (p + 1)]
        qs += [jnp.where(lane < HD, q_pair, 0).astype(BF16), jnp.where(lane >= HD, q_pair, 0).astype(BF16)]
    key = lax.broadcasted_iota(jnp.int32, (tq, tq), 0)
    qry = lax.broadcasted_iota(jnp.int32, (tq, tq), 1)
    past = key < qry
    suffix = jnp.where(qry >= key, -1.0, 0.0).astype(BF16)
    heads = range(2 * pairs)

    def step(kb, carry, boundary):
        start = pl.multiple_of(kb * tq, tq)
        zs = [_dot_nt(k_ref[pl.ds(start, tq), 128 * (h // 2):128 * (h // 2 + 1)], qs[h]) for h in heads]
        parts = []
        for z in zs:
            neg_abs = pltpu.bitcast(pltpu.bitcast(z, jnp.uint32) | jnp.uint32(0x80000000), F32)
            soft = jnp.maximum(z, 0.0) + jnp.log2(1.0 + jnp.exp2(neg_abs))
            if boundary:
                soft = jnp.where(past, soft, 0.0)
            parts.append(soft.astype(BF16))
        sums = [_dot(suffix, part) for part in parts]
        ws = []
        for h in heads:
            w = jnp.exp2(zs[h] + (carry[h][0] + sums[h]))
            ws.append((jnp.where(past, w, 0.0) if boundary else w).astype(BF16))
        return tuple((carry[h][0] + sums[h][0:1], carry[h][1]
                      + _dot(vt_ref[128 * (h // 2):128 * (h // 2 + 1), pl.ds(start, tq)], ws[h])) for h in heads)

    def live(carry):
        top = carry[0][0]
        for c, _ in carry[1:]:
            top = jnp.maximum(top, c)
        return (jnp.max(top) > SB_DONE).astype(jnp.int32)

    def body(state):
        t, _, carry = state
        carry = step(i - 1 - t, carry, False)
        return t + 1, live(carry), carry

    init = tuple((jnp.zeros((1, tq), F32), jnp.zeros((128, tq), F32)) for _ in range(2 * pairs))
    carry = step(i, init, True)
    _, _, carry = lax.while_loop(lambda st: (st[0] < i) & (st[1] > 0), body, (jnp.int32(0), live(carry), carry))
    rows = lax.broadcasted_iota(jnp.int32, (128, tq), 0)
    for p in range(pairs):
        ot = jnp.where(rows < HD, carry[2 * p][1], carry[2 * p + 1][1])
        o_ref[:, 128 * p:128 * (p + 1)] = ot.T.astype(BF16)


def _sb_attention(q, k, vt, B, T, tq=256, pairs=4):
    nq = T // tq
    w = 128 * pairs
    return pl.pallas_call(
        functools.partial(_sb_kernel, tq=tq, pairs=pairs),
        grid=(B, 4 // pairs, nq),
        in_specs=[pl.BlockSpec((tq, w), lambda b, p, i: (b * nq + i, p)),
                  pl.BlockSpec((T, w), lambda b, p, i: (b, p)),
                  pl.BlockSpec((None, w, T), lambda b, p, i: (b, p, 0))],
        out_specs=pl.BlockSpec((tq, w), lambda b, p, i: (b * nq + i, p)),
        out_shape=jax.ShapeDtypeStruct((B * T, 512), BF16),
        compiler_params=_cparams(("arbitrary", "arbitrary", "arbitrary")),
        name="sb_attention",
    )(q, k, vt)


def _router_weights(w_group, b_group, w_router, b_router):
    pad = 128 - N_EXP - N_GROUPS
    wr = jnp.concatenate([w_router, w_group, jnp.zeros((D, pad), F32)], axis=1).T
    hi = wr.astype(BF16)
    lo = (wr - hi.astype(F32)).astype(BF16)
    br = jnp.concatenate([b_router, b_group, jnp.zeros((pad,), F32)])[:, None]
    return jnp.concatenate([hi, lo], axis=0), br


def _ffn(h, oa, ob, wo, g, w_group, b_group, w_router, b_router, w_gate, w_up, w_down, layer):
    wr, br = _router_weights(w_group, b_group, w_router, b_router)
    hn, u, ri, rg, cnt = _outproj_router(h, oa, ob, wo.astype(BF16), g[None, :], wr, br)
    d1, d2, P, plan = _dispatch_plan(ri, cnt, h.shape[0])
    xs = _sc_scatter_rows(u, d1, d2, P)
    ys = _moe_experts(xs, plan, w_gate, w_up, w_down, layer)
    return hn, _sc_gather_rows(ys, jnp.concatenate([d1, d2])), rg


def kernel(x, norm_mix, norm_ffn, norm_final, rel_bias_table, w_in_even, g_mla_q, g_mla_kv, w_mla_qb, w_mla_kvb, swa_sinks, w_out_even, w_in_odd, w_out_odd, moe_w_group, moe_b_group, moe_w_router, moe_b_router, moe_w_gate, moe_w_up, moe_w_down):
    B, T, _ = x.shape
    h = x.reshape(B * T, D)
    tb_swa, tb_moba = _bias_tables(rel_bias_table)

    win, wqa, wqb, wk, wv = _even_weights(w_in_even[0], w_mla_qb[0], w_mla_kvb[0])
    q, k, v, qs8, ks, vst = _proj_even(h, norm_mix[0][None, :], win, g_mla_q[0][None, :], g_mla_kv[0][None, :],
                                       wqa, wqb, wk, wv, _rope_tables(T), T)
    oa = _mla_attention(q, k, v, B, T)
    ob = _swa_attention(qs8, ks, vst, swa_sinks[0], tb_swa, B, T)
    h, y12, rg = _ffn(h, oa, ob, w_out_even[0], norm_ffn[0], moe_w_group[0], moe_b_group[0], moe_w_router[0],
                      moe_b_router[0], moe_w_gate, moe_w_up, moe_w_down, 0)

    h, vct, vdt, qc, kc, qd, kd = _combine_proj_odd(h, y12, rg, norm_mix[1][None, :], *_odd_weights(w_in_odd[0]),
                                                   _block_indicator(T), T)
    oc = _moba_attention(qc, kc, vct, tb_moba, B, T)
    od = _sb_attention(qd, kd, vdt, B, T)
    h, y12, rg = _ffn(h, oc, od, w_out_odd[0], norm_ffn[1], moe_w_group[1], moe_b_group[1], moe_w_router[1],
                      moe_b_router[1], moe_w_gate, moe_w_up, moe_w_down, 1)
    out = _combine_final(h, y12, rg, norm_final[None, :])
    return out.reshape(B, T, D)
```

```python
import functools
import math

import numpy as np
import jax
import jax.numpy as jnp
from jax import lax
from jax.experimental import pallas as pl
from jax.experimental.pallas import tpu as pltpu
from jax.experimental.pallas import tpu_sc as plsc

F32 = jnp.float32
BF16 = jnp.bfloat16
NEG = -1e30
EPS = 1e-6

D = 1024
HD = 64
MLA_HEADS, MLA_QR, MLA_KVR, MLA_NOPE, MLA_ROPE, MLA_V = 8, 256, 128, 64, 32, 64
ROPE_THETA = 10000.0
SWA_HEADS, SWA_KV, SWA_WIN = 8, 2, 128
MOBA_L, MOBA_TOPK = 256, 3
REL_BUCKETS, REL_MAX = 32, 2048
N_GROUPS, EPG, N_EXP, D_EXP = 4, 8, 32, 256
BIAS_CONST_FROM = 1792
MOBA_TQ = 2 * MOBA_L
BIAS_PAD = MOBA_L
BIAS_ROWS = BIAS_PAD + BIAS_CONST_FROM + MOBA_TQ
SB_DONE = -160.0
TM = 1024
TM_E = 512
VMEM_LIMIT = 56 * 1024 * 1024
SC_CORES, SC_SUBCORES = 2, 16
SC_CHUNK = 128
HALF = D // 2
VT_ROWS = HD + 16
LOG2E = math.log2(math.e)


def _pack_halves(x):
    lo = pltpu.bitcast(x[:, :HALF].astype(BF16).astype(F32), jnp.uint32) >> 16
    hi = pltpu.bitcast(x[:, HALF:].astype(BF16).astype(F32), jnp.uint32) & jnp.uint32(0xFFFF0000)
    return pltpu.bitcast(lo | hi, jnp.int32)


def _unpack_halves(w):
    w = pltpu.bitcast(w, jnp.uint32)
    return pltpu.bitcast(w << 16, F32), pltpu.bitcast(w & jnp.uint32(0xFFFF0000), F32)


def _cparams(sem):
    return pltpu.CompilerParams(dimension_semantics=sem, vmem_limit_bytes=VMEM_LIMIT)


def _dot(a, b):
    return jnp.dot(a, b, preferred_element_type=F32)


def _dot_nt(a, b):
    return lax.dot_general(a, b, (((1,), (1,)), ((), ())), preferred_element_type=F32)


def _rms(x, g):
    return x * lax.rsqrt(jnp.mean(x * x, axis=-1, keepdims=True) + EPS) * g


def _values_t(vt):
    ones = jnp.ones((VT_ROWS - HD, vt.shape[1]), vt.dtype)
    return jnp.concatenate([blk for h in range(8) for blk in (vt[HD * h:HD * (h + 1)], ones)], axis=0)


def _spread_heads(x, lane_of):
    lane = lax.broadcasted_iota(jnp.int32, (x.shape[0], 128), 1)
    out = []
    for h in range(8):
        pair = x[:, 128 * (h // 2):128 * (h // 2 + 1)]
        if HD * (h % 2) != lane_of(h):
            pair = pltpu.roll(pair, HD, 1)
        out.append(jnp.where((lane >= lane_of(h)) & (lane < lane_of(h) + HD), pair, 0.0))
    return out


def _t5_bucket_np(dist):
    n = np.maximum(dist, 0)
    max_exact = REL_BUCKETS // 2
    nf = np.maximum(n, 1).astype(np.float32)
    large = max_exact + (np.log(nf / np.float32(max_exact)) / np.float32(math.log(REL_MAX / max_exact))
                         * np.float32(REL_BUCKETS - max_exact)).astype(np.int32)
    large = np.minimum(large, REL_BUCKETS - 1)
    return np.where(n < max_exact, n, large).astype(np.int32)


def _bias_kernel(table_ref, bucket_ref, out_ref, *, scale, ranges):
    b = bucket_ref[...]
    for m, (lo, hi) in enumerate(ranges):
        @pl.when(pl.program_id(0) == m)
        def _():
            for h in range(8):
                acc = jnp.full(b.shape, NEG if lo == REL_BUCKETS else table_ref[lo, h] * scale, F32)
                for k in range(lo + 1, min(hi, REL_BUCKETS - 1) + 1):
                    acc = jnp.where(b == k, table_ref[k, h] * scale, acc)
                if hi == REL_BUCKETS:
                    acc = jnp.where(b == REL_BUCKETS, NEG, acc)
                out_ref[h] = acc


def _bias_lookup(rel_table, dist, scale, name, masked=None):
    R, C = dist.shape
    buckets = _t5_bucket_np(dist)
    if masked is not None:
        buckets = np.where(masked, REL_BUCKETS, buckets).astype(np.int32)
    ranges = tuple((int(buckets[:, c:c + 256].min()), int(buckets[:, c:c + 256].max())) for c in range(0, C, 256))
    return pl.pallas_call(
        functools.partial(_bias_kernel, scale=scale, ranges=ranges),
        grid=(C // 256,),
        in_specs=[pl.BlockSpec(memory_space=pltpu.SMEM),
                  pl.BlockSpec((R, 256), lambda m: (0, m))],
        out_specs=pl.BlockSpec((8, R, 256), lambda m: (0, 0, m)),
        out_shape=jax.ShapeDtypeStruct((8, R, C), F32),
        compiler_params=_cparams(("arbitrary",)),
        name=name,
    )(rel_table, jnp.asarray(buckets))


def _bias_tables(rel_table):
    d = np.arange(256)[None, :] - np.arange(256)[:, None]
    swa = _bias_lookup(rel_table, d, LOG2E, "bias_swa", masked=(d < 0) | (d >= SWA_WIN))
    moba = _bias_lookup(rel_table, np.arange(BIAS_ROWS)[None, :] - BIAS_PAD - np.arange(MOBA_L)[:, None], LOG2E,
                        "bias_moba")
    return swa, moba


def _proj_even_kernel(h_ref, g_ref, win_ref, gq_ref, gkv_ref, wqa_ref, wqb_ref, wk_ref, wv_ref,
                      cq_ref, sq_ref, ck_ref, sk_ref,
                      q_ref, k_ref, v_ref, qs_ref, ks_ref, vst_ref):
    u = _rms(h_ref[...], g_ref[...]).astype(BF16)
    nq = _rms(_dot(u, win_ref[:, 0:256]), gq_ref[...]).astype(BF16)
    qa = _dot(nq, wqa_ref[...])
    qb = _dot(nq, wqb_ref[...])
    cq, sq = cq_ref[...], sq_ref[...]
    for hh in range(MLA_HEADS):
        sl = slice(128 * hh, 128 * (hh + 1))
        q_ref[:, sl] = (qa[:, sl] * cq + qb[:, sl] * sq).astype(BF16)
    nkv = _rms(_dot(u, win_ref[:, 256:384]), gkv_ref[...]).astype(BF16)
    kn = _dot(nkv, wk_ref[...])
    v_ref[...] = _values_t(_dot_nt(wv_ref[...], nkv)).astype(BF16)
    rest = _dot(u, win_ref[:, 896:1408])
    kr = rest[:, 128:256] * ck_ref[...] + rest[:, 256:384] * sk_ref[...]
    for hh in range(MLA_HEADS):
        sl = slice(128 * hh, 128 * (hh + 1))
        k_ref[:, sl] = (kn[:, sl] + kr).astype(BF16)
    qs = _spread_heads(_dot(u, win_ref[:, 384:896]), lambda hq: HD * (hq // (SWA_HEADS // SWA_KV)))
    for hq in range(SWA_HEADS):
        qs_ref[:, 128 * hq:128 * (hq + 1)] = qs[hq].astype(BF16)
    ks_ref[...] = rest[:, 0:128].astype(BF16)
    vst_ref[...] = rest[:, 384:512].T.astype(BF16)


def _cols(w, idx, scale=None):
    idx = np.asarray(idx)
    cuts = [0] + [n for n in range(1, len(idx)) if (idx[n] < 0) != (idx[n - 1] < 0)
                  or (idx[n] >= 0 and idx[n] != idx[n - 1] + 1)] + [len(idx)]
    parts = [jnp.zeros((w.shape[0], b - a), w.dtype) if idx[a] < 0 else w[:, idx[a]:idx[a] + b - a]
             for a, b in zip(cuts[:-1], cuts[1:])]
    out = jnp.concatenate(parts, axis=1)
    if scale is not None:
        out = out * scale
    return out


def _even_weights(w_in, w_qb, w_kvb):
    o_cq, o_ckv, o_kr = 0, MLA_QR, MLA_QR + MLA_KVR
    o_qs = o_kr + MLA_ROPE
    o_ks = o_qs + SWA_HEADS * HD
    o_vs = o_ks + SWA_KV * HD
    idx = list(range(o_cq, o_cq + MLA_QR)) + list(range(o_ckv, o_ckv + MLA_KVR))
    scale = [1.0] * len(idx)
    idx += list(range(o_qs, o_qs + SWA_HEADS * HD))
    scale += [HD ** -0.5 * LOG2E] * (SWA_HEADS * HD)
    idx += list(range(o_ks, o_ks + 128))
    scale += [1.0] * 128
    half = MLA_ROPE // 2
    idx += [-1] * 64 + list(range(o_kr, o_kr + MLA_ROPE)) + [-1] * 32
    idx += [-1] * 64 + list(range(o_kr + half, o_kr + MLA_ROPE)) + list(range(o_kr, o_kr + half)) + [-1] * 32
    idx += list(range(o_vs, o_vs + 128))
    scale += [1.0] * 384
    win = _cols(w_in, idx, jnp.asarray(np.asarray(scale, np.float32))).astype(BF16)
    qw = MLA_NOPE + MLA_ROPE
    ia, ib, ik, iv = [], [], [], []
    for h in range(MLA_HEADS):
        ia += list(range(qw * h, qw * h + qw)) + [-1] * 32
        ib += [-1] * 64 + list(range(qw * h + MLA_NOPE + half, qw * h + qw)) \
            + list(range(qw * h + MLA_NOPE, qw * h + MLA_NOPE + half)) + [-1] * 32
        ik += list(range(128 * h, 128 * h + MLA_NOPE)) + [-1] * 64
        iv += list(range(128 * h + MLA_NOPE, 128 * h + 128))
    return (win, _cols(w_qb, ia).astype(BF16), _cols(w_qb, ib).astype(BF16),
            _cols(w_kvb, ik).astype(BF16), _cols(w_kvb, iv).T.astype(BF16))


def _rope_tables(T):
    f32 = np.float32
    freqs = f32(ROPE_THETA) ** (-np.arange(0, MLA_ROPE, 2, dtype=f32) / f32(MLA_ROPE))
    ang = np.arange(T, dtype=f32)[:, None] * freqs[None, :]
    cos, sin = np.cos(ang), np.sin(ang)
    z = lambda n: np.zeros((T, n), f32)
    scale = f32((MLA_NOPE + MLA_ROPE) ** -0.5 * LOG2E)
    ck = np.concatenate([z(64), cos, cos, z(32)], axis=1)
    sk = np.concatenate([z(64), -sin, sin, z(32)], axis=1)
    cq = np.concatenate([np.ones((T, 64), f32), cos, cos, z(32)], axis=1) * scale
    sq = sk * scale
    return tuple(jnp.asarray(t, F32) for t in (cq, sq, ck, sk))


def _proj_even(h, g, win, gq, gkv, wqa, wqb, wk, wv, tables, T):
    N = h.shape[0]
    nt = T // TM
    full = lambda a: pl.BlockSpec(a.shape, lambda i: (0,) * a.ndim)
    tab = pl.BlockSpec((TM, 128), lambda i: (i % nt, 0))
    row = lambda w: pl.BlockSpec((TM, w), lambda i: (i, 0))
    outs = (1024, 1024, -8 * VT_ROWS, 1024, 128, -128)
    spec = lambda w: row(w) if w > 0 else pl.BlockSpec((None, -w, TM), lambda i: (i // nt, 0, i % nt))
    shape = lambda w: jax.ShapeDtypeStruct((N, w) if w > 0 else (N // T, -w, T), BF16)
    return pl.pallas_call(
        _proj_even_kernel,
        grid=(N // TM,),
        in_specs=[row(D), full(g), full(win), full(gq), full(gkv), full(wqa), full(wqb), full(wk), full(wv),
                  tab, tab, tab, tab],
        out_specs=[spec(w) for w in outs],
        out_shape=[shape(w) for w in outs],
        compiler_params=_cparams(("arbitrary",)),
        name="proj_even",
    )(h, g, win, gq, gkv, wqa, wqb, wk, wv, *tables)


def _finish_t(accs):
    ot = jnp.concatenate([acc[:HD] / acc[HD:HD + 1] for acc in accs], axis=0)
    return ot.T.astype(BF16)


def _causal_attention_pair_t(ia, nq, q_of, k_ref, vt_ref, o_ref, s_ref, smax_ref, m_ref, acc_ref, *, tq,
                             bias_of=None):
    nh = s_ref.shape[1]
    first_query = (ia * tq, (nq - 1 - ia) * tq)
    n_items = nq + 1
    assert n_items % 2 == 1
    key = lax.broadcasted_iota(jnp.int32, (tq, tq), 0)
    qry = lax.broadcasted_iota(jnp.int32, (tq, tq), 1)

    def item(p):
        if isinstance(p, int) and p < 2:
            return p, first_query[p], first_query[p]
        t = jnp.where(p - 2 >= ia, 1, 0)
        return t, jnp.where(t == 1, first_query[1], first_query[0]), (p - 2 - t * ia) * tq

    def scores_to(slot, p):
        t, q0, start = item(p)
        start = pl.multiple_of(start, tq)
        for j in range(nh):
            st = _dot_nt(k_ref[pl.ds(start, tq), 128 * j:128 * (j + 1)], q_of(t, q0, j))
            if bias_of is not None:
                st = st + bias_of(start, q0, j)
            if isinstance(p, int) and p < 2:
                st = jnp.where(key <= qry, st, NEG)
            s_ref[slot, j] = st
            smax_ref[slot, j] = jnp.max(st, axis=0, keepdims=True)

    def update_from(slot, p):
        t, _, start = item(p)
        start = pl.multiple_of(start, tq)
        m_new = [jnp.maximum(m_ref[t, j], smax_ref[slot, j]) for j in range(nh)]
        pts = [jnp.exp2(s_ref[slot, j] - m_new[j]).astype(BF16) for j in range(nh)]
        for j in range(nh):
            acc_ref[t, j] = (jnp.exp2(m_ref[t, j] - m_new[j]) * acc_ref[t, j]
                             + _dot(vt_ref[VT_ROWS * j:VT_ROWS * (j + 1), pl.ds(start, tq)], pts[j]))
            m_ref[t, j] = m_new[j]

    def pair(k):
        scores_to(1, k + 1)
        update_from(0, k)
        scores_to(0, k + 2)
        update_from(1, k + 1)

    def quad(c, _):
        pair(2 + 4 * c)
        pair(4 + 4 * c)
        return 0

    m_ref[...] = jnp.full(m_ref.shape, NEG, F32)
    acc_ref[...] = jnp.zeros(acc_ref.shape, F32)
    scores_to(0, 0)
    pair(0)
    later_pairs = (n_items - 1) // 2 - 1
    lax.fori_loop(0, later_pairs // 2, quad, 0)
    if later_pairs % 2:
        pair(n_items - 3)
    update_from(0, n_items - 1)
    for t in range(2):
        o_ref[pl.ds(pl.multiple_of(first_query[t], tq), tq), :] = _finish_t([acc_ref[t, j] for j in range(nh)])


def _attention_t_scratch(tq):
    return [pltpu.VMEM((2, 2, tq, tq), F32), pltpu.VMEM((2, 2, 1, tq), F32), pltpu.VMEM((2, 2, 1, tq), F32),
            pltpu.VMEM((2, 2, VT_ROWS, tq), F32)]


def _mla_kernel(q_ref, k_ref, vt_ref, o_ref, s_ref, smax_ref, m_ref, acc_ref, *, tq, nq):
    def q_of(t, q0, j):
        return q_ref[pl.ds(pl.multiple_of(q0, tq), tq), 128 * j:128 * (j + 1)]

    _causal_attention_pair_t(pl.program_id(2), nq, q_of, k_ref, vt_ref, o_ref, s_ref, smax_ref, m_ref, acc_ref,
                             tq=tq)


def _mla_attention(q, k, vt, B, T, tq=512):
    nq = T // tq
    return pl.pallas_call(
        functools.partial(_mla_kernel, tq=tq, nq=nq),
        grid=(B, MLA_HEADS // 2, nq // 2),
        in_specs=[pl.BlockSpec((T, 256), lambda b, p, i: (b, p)),
                  pl.BlockSpec((T, 256), lambda b, p, i: (b, p)),
                  pl.BlockSpec((None, 2 * VT_ROWS, T), lambda b, p, i: (b, p, 0))],
        out_specs=pl.BlockSpec((T, 128), lambda b, p, i: (b, p)),
        out_shape=jax.ShapeDtypeStruct((B * T, MLA_HEADS * MLA_V), BF16),
        scratch_shapes=_attention_t_scratch(tq),
        compiler_params=_cparams(("arbitrary", "arbitrary", "arbitrary")),
        name="mla_attention",
    )(q, k, vt)


def _swa_kernel(sink_ref, q_ref, k_ref, vt_ref, tb_ref, o_ref, *, tq):
    i = pl.program_id(1)
    G = SWA_HEADS // SWA_KV
    W = 2 * SWA_WIN
    head = lax.broadcasted_iota(jnp.int32, (1, G * 128), 1) >> 7
    sinks = []
    for c in range(SWA_KV):
        sink = jnp.zeros((1, G * 128), F32)
        for g in range(G):
            sink = jnp.where(head == g, sink_ref[G * c + g], sink)
        sinks.append(sink)
    scores, values = [], []
    for r in range(tq // 128):
        qstart = i * tq + 128 * r
        kstart = pl.multiple_of(jnp.maximum(qstart - SWA_WIN, 0), 128)
        off = pl.multiple_of(qstart - kstart, 128)
        kw = k_ref[pl.ds(kstart, W), :]
        for c in range(SWA_KV):
            qg = jnp.concatenate([q_ref[128 * r:128 * (r + 1), 128 * (G * c + g):128 * (G * c + g + 1)]
                                  for g in range(G)], axis=0)
            bias = jnp.concatenate([tb_ref[G * c + g, :, pl.ds(off, 128)] for g in range(G)], axis=1)
            scores.append(_dot_nt(kw, qg) + bias)
            values.append(vt_ref[HD * c:HD * (c + 1), pl.ds(kstart, W)])
    probs = []
    for n, s in enumerate(scores):
        sink = sinks[n % SWA_KV]
        m = jnp.maximum(jnp.max(s, axis=0, keepdims=True), sink)
        p = jnp.exp2(s - m)
        probs.append((p.astype(BF16), jnp.sum(p, axis=0, keepdims=True) + jnp.exp2(sink - m)))
    for r in range(tq // 128):
        outs = []
        for c in range(SWA_KV):
            p, denom = probs[SWA_KV * r + c]
            ot = _dot(values[SWA_KV * r + c], p) / denom
            outs += [ot[:, 128 * g:128 * (g + 1)] for g in range(G)]
        o_ref[128 * r:128 * (r + 1), :] = jnp.concatenate(outs, axis=0).T.astype(BF16)


def _swa_attention(q8, k, vt, sinks, tb, B, T, tq=512):
    nq = T // tq
    return pl.pallas_call(
        functools.partial(_swa_kernel, tq=tq),
        grid=(B, nq),
        in_specs=[pl.BlockSpec(memory_space=pltpu.SMEM),
                  pl.BlockSpec((tq, 1024), lambda b, i: (b * nq + i, 0)),
                  pl.BlockSpec((T, 128), lambda b, i: (b, 0)),
                  pl.BlockSpec((None, 128, T), lambda b, i: (b, 0, 0)),
                  pl.BlockSpec((8, 256, 256), lambda b, i: (0, 0, 0))],
        out_specs=pl.BlockSpec((tq, 512), lambda b, i: (b * nq + i, 0)),
        out_shape=jax.ShapeDtypeStruct((B * T, SWA_HEADS * HD), BF16),
        compiler_params=_cparams(("arbitrary", "arbitrary")),
        name="swa_attention",
    )(sinks * LOG2E, q8, k, vt, tb)


def _outproj_router_kernel(h_ref, oa_ref, ob_ref, wo_ref, g_ref, wr_ref, br_ref,
                           hn_ref, u_ref, ri_ref, rg_ref, cnt_ref, base_ref):
    i = pl.program_id(0)

    @pl.when(i == 0)
    def _():
        base_ref[...] = jnp.zeros_like(base_ref)

    hn = h_ref[...] + _dot(oa_ref[...], wo_ref[0:512, :]) + _dot(ob_ref[...], wo_ref[512:1024, :])
    hn_ref[...] = hn
    u = _rms(hn, g_ref[...])
    u_ref[...] = _pack_halves(u)
    u_hi = u.astype(BF16)
    u_lo = (u - u_hi.astype(F32)).astype(BF16)
    both = _dot_nt(wr_ref[...], u_hi)
    logits = both[:128] + both[128:] + _dot_nt(wr_ref[0:128, :], u_lo) + br_ref[...]
    tm = logits.shape[1]
    sub = lax.broadcasted_iota(jnp.int32, (8, tm), 0).astype(F32)
    big = 1e6
    isg = sub < N_GROUPS
    gl = jnp.where(isg, logits[N_EXP:N_EXP + 8], NEG)
    gmax = jnp.max(gl, axis=0, keepdims=True)
    gsel = jnp.min(jnp.where(gl == gmax, sub, big), axis=0, keepdims=True)
    g_gate = 1.0 / jnp.sum(jnp.where(isg, jnp.exp(gl - gmax), 0.0), axis=0, keepdims=True)
    el = jnp.zeros((EPG, tm), F32)
    for grp in range(N_GROUPS):
        el = jnp.where(gsel == grp, logits[EPG * grp:EPG * (grp + 1)], el)
    m1 = jnp.max(el, axis=0, keepdims=True)
    i1 = jnp.min(jnp.where(el == m1, sub, big), axis=0, keepdims=True)
    el2 = jnp.where(sub == i1, NEG, el)
    m2 = jnp.max(el2, axis=0, keepdims=True)
    i2 = jnp.min(jnp.where(el2 == m2, sub, big), axis=0, keepdims=True)
    e1 = gsel * EPG + i1
    e2 = gsel * EPG + i2
    r = jnp.exp(m2 - m1)
    gate1 = g_gate / (1.0 + r)
    gate2 = g_gate * r / (1.0 + r)
    rows = lax.broadcasted_iota(jnp.int32, (128, tm), 0).astype(F32)
    oh1 = (rows == e1).astype(F32)
    oh2 = (rows == e2).astype(F32)
    oh = oh1 + oh2
    t_row = lax.broadcasted_iota(jnp.int32, (tm, tm), 0)
    t_col = lax.broadcasted_iota(jnp.int32, (tm, tm), 1)
    earlier = jnp.where(t_row < t_col, 1.0, 0.0).astype(BF16)
    prefix = _dot(oh.astype(BF16), earlier) + base_ref[:, 0:1]
    rank1 = jnp.sum(oh1 * prefix, axis=0, keepdims=True)
    rank2 = jnp.sum(oh2 * prefix, axis=0, keepdims=True)
    base_ref[...] = base_ref[...] + jnp.sum(oh, axis=1, keepdims=True)
    cnt_ref[...] = base_ref[...]
    ri_ref[...] = jnp.where(sub == 0, e1, jnp.where(sub == 1, e2, jnp.where(sub == 2, rank1,
                            jnp.where(sub == 3, rank2, 0.0))))
    rg_ref[...] = jnp.where(rows == 0, gate1, jnp.where(rows == 1, gate2, 0.0)).T


def _outproj_router(h, oa, ob, wo, g, wr, br):
    N = h.shape[0]
    full = lambda a: pl.BlockSpec(a.shape, lambda i: (0,) * a.ndim)
    row = lambda w: pl.BlockSpec((TM, w), lambda i: (i, 0))
    return pl.pallas_call(
        _outproj_router_kernel,
        grid=(N // TM,),
        in_specs=[row(D), row(512), row(512), full(wo), full(g), full(wr), full(br)],
        out_specs=[row(D), row(HALF), pl.BlockSpec((8, TM), lambda i: (0, i)), row(128),
                   pl.BlockSpec((128, 128), lambda i: (0, 0))],
        out_shape=[jax.ShapeDtypeStruct((N, D), F32), jax.ShapeDtypeStruct((N, HALF), jnp.int32),
                   jax.ShapeDtypeStruct((8, N), F32), jax.ShapeDtypeStruct((N, 128), F32),
                   jax.ShapeDtypeStruct((128, 128), F32)],
        scratch_shapes=[pltpu.VMEM((128, 128), F32)],
        compiler_params=_cparams(("arbitrary",)),
        name="outproj_router",
    )(h, oa, ob, wo, g, wr, br)


def _moe_kernel(te_ref, tv_ref, run_ref, nx_ref, x_ref, wg_hbm, wu_hbm, wd_hbm, o_ref,
                wg_buf, wu_buf, wd_buf, sems, wgu_s, wd_s, *, layer):
    i = pl.program_id(0)
    valid = tv_ref[i]

    def weight_copies(e, slot):
        return (pltpu.make_async_copy(wg_hbm.at[layer, e], wg_buf.at[slot], sems.at[slot, 0]),
                pltpu.make_async_copy(wu_hbm.at[layer, e], wu_buf.at[slot], sems.at[slot, 1]),
                pltpu.make_async_copy(wd_hbm.at[layer, e], wd_buf.at[slot], sems.at[slot, 2]))

    @pl.when(run_ref[i] > 0)
    def _():
        slot = run_ref[i] - 1

        @pl.when(i == 0)
        def _():
            for c in weight_copies(te_ref[i], slot):
                c.start()

        for c in weight_copies(te_ref[i], slot):
            c.wait()
        wgu_s[:, :D_EXP] = wg_buf[slot].astype(BF16)
        wgu_s[:, D_EXP:] = wu_buf[slot].astype(BF16)
        wd_s[...] = wd_buf[slot].astype(BF16)

        @pl.when(nx_ref[i] >= 0)
        def _():
            for c in weight_copies(nx_ref[i], 1 - slot):
                c.start()

    @pl.when(valid > 0)
    def _():
        rows = lax.broadcasted_iota(jnp.int32, x_ref.shape, 0)
        lo, hi = _unpack_halves(jnp.where(rows < valid, x_ref[...], 0))
        hgu = _dot(lo.astype(BF16), wgu_s[:HALF, :]) + _dot(hi.astype(BF16), wgu_s[HALF:, :])
        hg, hu = hgu[:, :D_EXP], hgu[:, D_EXP:]
        a = hg * (1.0 / (1.0 + jnp.exp(-hg))) * hu
        o_ref[...] = _pack_halves(_dot(a.astype(BF16), wd_s[...]))

    @pl.when(valid <= 0)
    def _():
        o_ref[...] = jnp.zeros_like(o_ref)


def _moe_experts(xs, plan, w_gate, w_up, w_down, layer):
    P = xs.shape[0]
    hbm = pl.BlockSpec(memory_space=pl.ANY)
    grid_spec = pltpu.PrefetchScalarGridSpec(
        num_scalar_prefetch=4,
        grid=(P // TM_E,),
        in_specs=[pl.BlockSpec((TM_E, HALF), lambda i, *_: (i, 0)), hbm, hbm, hbm],
        out_specs=pl.BlockSpec((TM_E, HALF), lambda i, *_: (i, 0)),
        scratch_shapes=[pltpu.VMEM((2, D, D_EXP), F32), pltpu.VMEM((2, D, D_EXP), F32),
                        pltpu.VMEM((2, D_EXP, D), F32), pltpu.SemaphoreType.DMA((2, 3)),
                        pltpu.VMEM((D, 2 * D_EXP), BF16), pltpu.VMEM((D_EXP, D), BF16)],
    )
    return pl.pallas_call(
        functools.partial(_moe_kernel, layer=layer),
        grid_spec=grid_spec,
        out_shape=jax.ShapeDtypeStruct((P, HALF), jnp.int32),
        compiler_params=_cparams(("arbitrary",)),
        name="moe_experts",
    )(*plan, xs, w_gate, w_up, w_down)


def _sc_mesh():
    return plsc.VectorSubcoreMesh(core_axis_name="c", subcore_axis_name="s",
                                  num_cores=SC_CORES, num_subcores=SC_SUBCORES)


def _sc_scatter_rows(src, d1, d2, P):
    N, W = src.shape
    per_w = N // (SC_CORES * SC_SUBCORES)

    @functools.partial(pl.kernel, mesh=_sc_mesh(), out_type=jax.ShapeDtypeStruct((P, W), src.dtype),
                       scratch_types=[pltpu.VMEM((SC_CHUNK,), jnp.int32), pltpu.VMEM((SC_CHUNK, W), src.dtype)],
                       name="sc_scatter_rows")
    def k(src_hbm, d1_hbm, d2_hbm, out_hbm, idx_v, rows_v):
        wid = lax.axis_index("s") * SC_CORES + lax.axis_index("c")

        @pl.loop(0, per_w // SC_CHUNK)
        def _(c):
            off = wid * per_w + c * SC_CHUNK
            pltpu.sync_copy(src_hbm.at[pl.ds(off, SC_CHUNK)], rows_v)
            pltpu.sync_copy(d1_hbm.at[pl.ds(off, SC_CHUNK)], idx_v)
            pltpu.sync_copy(rows_v, out_hbm.at[idx_v])
            pltpu.sync_copy(d2_hbm.at[pl.ds(off, SC_CHUNK)], idx_v)
            pltpu.sync_copy(rows_v, out_hbm.at[idx_v])

    return k(src, d1, d2)


def _sc_gather_rows(table, idx):
    B, W = idx.shape[0], table.shape[1]
    per_w = B // (SC_CORES * SC_SUBCORES)

    @functools.partial(pl.kernel, mesh=_sc_mesh(), out_type=jax.ShapeDtypeStruct((B, W), table.dtype),
                       scratch_types=[pltpu.VMEM((SC_CHUNK,), jnp.int32), pltpu.VMEM((SC_CHUNK, W), table.dtype)],
                       name="sc_gather_rows")
    def k(table_hbm, idx_hbm, out_hbm, idx_v, rows_v):
        wid = lax.axis_index("s") * SC_CORES + lax.axis_index("c")

        @pl.loop(0, per_w // SC_CHUNK)
        def _(c):
            off = wid * per_w + c * SC_CHUNK
            pltpu.sync_copy(idx_hbm.at[pl.ds(off, SC_CHUNK)], idx_v)
            pltpu.sync_copy(table_hbm.at[idx_v], rows_v)
            pltpu.sync_copy(rows_v, out_hbm.at[pl.ds(off, SC_CHUNK)])

    return k(table, idx)


def _dest_kernel(offs_ref, ri_ref, d_ref):
    ri = ri_ref[...]
    base = jnp.zeros(ri.shape, F32)
    for e in range(N_EXP):
        base = jnp.where(ri == e, offs_ref[e].astype(F32), base)
    d_ref[...] = (base + pltpu.roll(ri, 6, 0)).astype(jnp.int32)


def _dispatch_plan(ri, cnt, N):
    counts = cnt[:N_EXP, 0].astype(jnp.int32)
    padded = ((counts + TM_E - 1) // TM_E) * TM_E
    ends = jnp.cumsum(padded)
    offs = ends - padded
    take = lambda t, i: t.at[i].get(mode="promise_in_bounds")
    dest = pl.pallas_call(
        _dest_kernel,
        in_specs=[pl.BlockSpec(memory_space=pltpu.SMEM), pl.BlockSpec((8, N), lambda: (0, 0))],
        out_specs=pl.BlockSpec((8, N), lambda: (0, 0)),
        out_shape=jax.ShapeDtypeStruct((8, N), jnp.int32),
        name="dispatch_rows",
    )(offs, ri)
    d1, d2 = dest[0], dest[1]
    P = 2 * N + N_EXP * TM_E
    tile_start = jnp.arange(P // TM_E, dtype=jnp.int32) * TM_E
    tile_expert = jnp.minimum(jnp.sum((ends[None, :] <= tile_start[:, None]).astype(jnp.int32), axis=1), N_EXP - 1)
    tile_valid = jnp.clip(take(offs + counts, tile_expert) - tile_start, 0, TM_E)
    n_tiles = P // TM_E
    tile = jnp.arange(n_tiles, dtype=jnp.int32)
    starts = (tile_valid > 0) & ((tile == 0) | (tile_expert != jnp.roll(tile_expert, 1)))
    slot = (jnp.cumsum(starts.astype(jnp.int32)) - 1) % 2
    run_code = jnp.where(starts, slot + 1, 0).astype(jnp.int32)
    start_at = jnp.where(starts, tile, n_tiles)
    next_start = jnp.concatenate([lax.cummin(start_at, reverse=True)[1:], jnp.full((1,), n_tiles, jnp.int32)])
    next_expert = jnp.where(next_start < n_tiles, take(tile_expert, jnp.minimum(next_start, n_tiles - 1)), -1)
    return d1, d2, P, (tile_expert, tile_valid, run_code, next_expert.astype(jnp.int32))


def _combine(h_ref, y1_ref, y2_ref, rg_ref):
    rg = rg_ref[...]
    lo1, hi1 = _unpack_halves(y1_ref[...])
    lo2, hi2 = _unpack_halves(y2_ref[...])
    g1, g2 = rg[:, 0:1], rg[:, 1:2]
    return h_ref[...] + jnp.concatenate([g1 * lo1 + g2 * lo2, g1 * hi1 + g2 * hi2], axis=1)


ODD_WIDTHS = (1024, 1024, 512, 512)


def _combine_proj_odd_kernel(h_ref, y1_ref, y2_ref, rg_ref, g_ref, win_ref, wvt_ref, ind_ref, hn_ref, vt_ref,
                             vdt_ref, *out_refs):
    hn = _combine(h_ref, y1_ref, y2_ref, rg_ref)
    hn_ref[...] = hn
    u = _rms(hn, g_ref[...]).astype(BF16)
    vt_ref[...] = _values_t(_dot_nt(wvt_ref[0:512, :], u)).astype(BF16)
    vdt_ref[...] = _dot_nt(wvt_ref[512:1024, :], u).astype(BF16)
    qc_ref, kc_ref, qd_ref, kd_ref = out_refs
    qc = _spread_heads(_dot(u, win_ref[:, 0:512]), lambda h: 0)
    kc = _spread_heads(_dot(u, win_ref[:, 512:1024]), lambda h: 0)
    for hh in range(8):
        qc_ref[:, 128 * hh:128 * (hh + 1)] = qc[hh].astype(BF16)
        kc_ref[:, 128 * hh:128 * (hh + 1)] = (kc[hh] + ind_ref[...]).astype(BF16)
    qd_ref[...] = _dot(u, win_ref[:, 1024:1536]).astype(BF16)
    kd_ref[...] = _dot(u, win_ref[:, 1536:2048]).astype(BF16)


def _odd_weights(w_in):
    s = HD ** -0.5 * LOG2E
    scale = np.asarray([s] * 512 + [1.0] * 512 + [s] * 512 + [1.0] * 512, np.float32)
    win = jnp.concatenate([w_in[:, 0:1024], w_in[:, 1536:2560]], axis=1) * jnp.asarray(scale)
    wvt = jnp.concatenate([w_in[:, 1024:1536], w_in[:, 2560:3072]], axis=1).T
    return win.astype(BF16), wvt.astype(BF16)


def _block_indicator(T):
    assert T // MOBA_L <= 64
    ind = np.zeros((T, 128), np.float32)
    ind[np.arange(T), 64 + np.arange(T) // MOBA_L] = 1.0
    return jnp.asarray(ind)


def _y_specs(N):
    return [pl.BlockSpec((TM, HALF), lambda i: (i, 0)), pl.BlockSpec((TM, HALF), lambda i: (i + N // TM, 0))]


def _combine_proj_odd(h, y12, rg, g, win, wvt, ind, T):
    N = h.shape[0]
    nt = T // TM
    full = lambda a: pl.BlockSpec(a.shape, lambda i: (0,) * a.ndim)
    row = lambda w: pl.BlockSpec((TM, w), lambda i: (i, 0))
    return pl.pallas_call(
        _combine_proj_odd_kernel,
        grid=(N // TM,),
        in_specs=[row(D)] + _y_specs(N) + [row(128), full(g), full(win), full(wvt),
                  pl.BlockSpec((TM, 128), lambda i: (i % nt, 0))],
        out_specs=[row(D), pl.BlockSpec((None, 8 * VT_ROWS, TM), lambda i: (i // nt, 0, i % nt)),
                   pl.BlockSpec((None, 512, TM), lambda i: (i // nt, 0, i % nt))]
        + [row(w) for w in ODD_WIDTHS],
        out_shape=[jax.ShapeDtypeStruct((N, D), F32), jax.ShapeDtypeStruct((N // T, 8 * VT_ROWS, T), BF16),
                   jax.ShapeDtypeStruct((N // T, 512, T), BF16)]
        + [jax.ShapeDtypeStruct((N, w), BF16) for w in ODD_WIDTHS],
        compiler_params=_cparams(("arbitrary",)),
        name="combine_proj_odd",
    )(h, y12, y12, rg, g, win, wvt, ind)


def _combine_final_kernel(h_ref, y1_ref, y2_ref, rg_ref, g_ref, o_ref):
    o_ref[...] = _rms(_combine(h_ref, y1_ref, y2_ref, rg_ref), g_ref[...])


def _combine_final(h, y12, rg, g):
    N = h.shape[0]
    row = lambda w: pl.BlockSpec((TM, w), lambda i: (i, 0))
    return pl.pallas_call(
        _combine_final_kernel,
        grid=(N // TM,),
        in_specs=[row(D)] + _y_specs(N) + [row(128), pl.BlockSpec((1, D), lambda i: (0, 0))],
        out_specs=row(D),
        out_shape=jax.ShapeDtypeStruct((N, D), F32),
        compiler_params=_cparams(("arbitrary",)),
        name="combine_final",
    )(h, y12, y12, rg, g)


def _moba_kernel(q_ref, k_ref, vt_ref, tb_ref, o_ref, kmean_ref, ksplit_ref, qa_ref, s_ref, smax_ref, m_ref, acc_ref,
                 *, nkb, nq):
    ia = pl.program_id(2)
    L, tq = MOBA_L, MOBA_TQ

    @pl.when(ia == 0)
    def _():
        kmean_ref[...] = jnp.zeros_like(kmean_ref)
        for n in range(nkb):
            kmean_ref[64 + n:65 + n, :] = jnp.sum(k_ref[n * L:(n + 1) * L, :].astype(F32), axis=0,
                                                  keepdims=True) * (1.0 / L)
        hi = kmean_ref[...].astype(BF16)
        ksplit_ref[0] = hi
        ksplit_ref[1] = (kmean_ref[...] - hi.astype(F32)).astype(BF16)

    nb = -(-nkb // 8) * 8
    lane = lax.broadcasted_iota(jnp.int32, (tq, 128), 1)
    blk = lax.broadcasted_iota(jnp.int32, (nb, tq), 0)
    half = jnp.where(lax.broadcasted_iota(jnp.int32, (nb, tq), 1) >= L, 1, 0)
    blk_f = blk.astype(F32)
    for t, tile in enumerate((ia, nq - 1 - ia)):
        own = 2 * tile + half
        for j in range(2):
            qj = q_ref[pl.ds(pl.multiple_of(tile * tq, tq), tq), 128 * j:128 * (j + 1)]
            gsc = (_dot_nt(ksplit_ref[0, 64:64 + nb, 128 * j:128 * (j + 1)], qj)
                   + _dot_nt(ksplit_ref[1, 64:64 + nb, 128 * j:128 * (j + 1)], qj))
            g = jnp.where(blk < own, gsc, NEG)
            allowed = jnp.where(blk == own, 1.0, 0.0)
            for _ in range(MOBA_TOPK):
                mx = jnp.max(g, axis=0, keepdims=True)
                idx = jnp.min(jnp.where(g == mx, blk_f, 1e6), axis=0, keepdims=True)
                pick = blk_f == idx
                allowed = jnp.maximum(allowed, jnp.where(pick, jnp.where(mx > 0.5 * NEG, 1.0, 0.0), 0.0))
                g = jnp.where(pick, 2.0 * NEG, g)
            mask = jnp.concatenate([jnp.zeros((64, tq), F32), jnp.where(allowed > 0.5, 0.0, NEG),
                                    jnp.zeros((64 - nb, tq), F32)], axis=0)
            qa_ref[t, j] = jnp.where(lane < HD, qj, mask.T.astype(BF16))

    def bias_of(start, q0, j):
        parts = []
        for c in range(tq // L):
            off = jnp.minimum(q0 - start - c * L, BIAS_CONST_FROM) + BIAS_PAD
            parts.append(tb_ref[j, :, pl.ds(pl.multiple_of(off, L), tq)])
        return jnp.concatenate(parts, axis=0)

    _causal_attention_pair_t(ia, nq, lambda t, q0, j: qa_ref[t, j], k_ref, vt_ref, o_ref, s_ref, smax_ref, m_ref,
                             acc_ref, tq=tq, bias_of=bias_of)


def _moba_attention(q, k, vt, tb, B, T):
    tq = MOBA_TQ
    nq = T // tq
    return pl.pallas_call(
        functools.partial(_moba_kernel, nkb=T // MOBA_L, nq=nq),
        grid=(B, 4, nq // 2),
        in_specs=[pl.BlockSpec((T, 256), lambda b, p, i: (b, p)),
                  pl.BlockSpec((T, 256), lambda b, p, i: (b, p)),
                  pl.BlockSpec((None, 2 * VT_ROWS, T), lambda b, p, i: (b, p, 0)),
                  pl.BlockSpec((2, MOBA_L, BIAS_ROWS), lambda b, p, i: (p, 0, 0))],
        out_specs=pl.BlockSpec((T, 128), lambda b, p, i: (b, p)),
        out_shape=jax.ShapeDtypeStruct((B * T, 512), BF16),
        scratch_shapes=[pltpu.VMEM((128, 256), F32), pltpu.VMEM((2, 128, 256), BF16),
                        pltpu.VMEM((2, 2, tq, 128), BF16)]
        + _attention_t_scratch(tq),
        compiler_params=_cparams(("arbitrary", "arbitrary", "arbitrary")),
        name="moba_attention",
    )(q, k, vt, tb)


def _sb_kernel(q_ref, k_ref, vt_ref, o_ref, *, tq, pairs):
    i = pl.program_id(2)
    lane = lax.broadcasted_iota(jnp.int32, (tq, 128), 1)
    qs = []
    for p in range(pairs):
        q_pair = q_ref[:, 128 * p:128 * (p + 1)]
        qs += [jnp.where(lane < HD, q_pair, 0).astype(BF16), jnp.where(lane >= HD, q_pair, 0).astype(BF16)]
    key = lax.broadcasted_iota(jnp.int32, (tq, tq), 0)
    qry = lax.broadcasted_iota(jnp.int32, (tq, tq), 1)
    past = key < qry
    suffix = jnp.where(qry >= key, -1.0, 0.0).astype(BF16)
    heads = range(2 * pairs)

    def step(kb, carry, boundary):
        start = pl.multiple_of(kb * tq, tq)
        zs = [_dot_nt(k_ref[pl.ds(start, tq), 128 * (h // 2):128 * (h // 2 + 1)], qs[h]) for h in heads]
        parts = []
        for z in zs:
            neg_abs = pltpu.bitcast(pltpu.bitcast(z, jnp.uint32) | jnp.uint32(0x80000000), F32)
            soft = jnp.maximum(z, 0.0) + jnp.log2(1.0 + jnp.exp2(neg_abs))
            if boundary:
                soft = jnp.where(past, soft, 0.0)
            parts.append(soft.astype(BF16))
        sums = [_dot(suffix, part) for part in parts]
        ws = []
        for h in heads:
            w = jnp.exp2(zs[h] + (carry[h][0] + sums[h]))
            ws.append((jnp.where(past, w, 0.0) if boundary else w).astype(BF16))
        return tuple((carry[h][0] + sums[h][0:1], carry[h][1]
                      + _dot(vt_ref[128 * (h // 2):128 * (h // 2 + 1), pl.ds(start, tq)], ws[h])) for h in heads)

    def live(carry):
        top = carry[0][0]
        for c, _ in carry[1:]:
            top = jnp.maximum(top, c)
        return (jnp.max(top) > SB_DONE).astype(jnp.int32)

    def body(state):
        t, _, carry = state
        carry = step(i - 1 - t, carry, False)
        return t + 1, live(carry), carry

    init = tuple((jnp.zeros((1, tq), F32), jnp.zeros((128, tq), F32)) for _ in range(2 * pairs))
    carry = step(i, init, True)
    _, _, carry = lax.while_loop(lambda st: (st[0] < i) & (st[1] > 0), body, (jnp.int32(0), live(carry), carry))
    rows = lax.broadcasted_iota(jnp.int32, (128, tq), 0)
    for p in range(pairs):
        ot = jnp.where(rows < HD, carry[2 * p][1], carry[2 * p + 1][1])
        o_ref[:, 128 * p:128 * (p + 1)] = ot.T.astype(BF16)


def _sb_attention(q, k, vt, B, T, tq=256, pairs=4):
    nq = T // tq
    w = 128 * pairs
    return pl.pallas_call(
        functools.partial(_sb_kernel, tq=tq, pairs=pairs),
        grid=(B, 4 // pairs, nq),
        in_specs=[pl.BlockSpec((tq, w), lambda b, p, i: (b * nq + i, p)),
                  pl.BlockSpec((T, w), lambda b, p, i: (b, p)),
                  pl.BlockSpec((None, w, T), lambda b, p, i: (b, p, 0))],
        out_specs=pl.BlockSpec((tq, w), lambda b, p, i: (b * nq + i, p)),
        out_shape=jax.ShapeDtypeStruct((B * T, 512), BF16),
        compiler_params=_cparams(("arbitrary", "arbitrary", "arbitrary")),
        name="sb_attention",
    )(q, k, vt)


def _router_weights(w_group, b_group, w_router, b_router):
    pad = 128 - N_EXP - N_GROUPS
    wr = jnp.concatenate([w_router, w_group, jnp.zeros((D, pad), F32)], axis=1).T
    hi = wr.astype(BF16)
    lo = (wr - hi.astype(F32)).astype(BF16)
    br = jnp.concatenate([b_router, b_group, jnp.zeros((pad,), F32)])[:, None]
    return jnp.concatenate([hi, lo], axis=0), br


def _ffn(h, oa, ob, wo, g, w_group, b_group, w_router, b_router, w_gate, w_up, w_down, layer):
    wr, br = _router_weights(w_group, b_group, w_router, b_router)
    hn, u, ri, rg, cnt = _outproj_router(h, oa, ob, wo.astype(BF16), g[None, :], wr, br)
    d1, d2, P, plan = _dispatch_plan(ri, cnt, h.shape[0])
    xs = _sc_scatter_rows(u, d1, d2, P)
    ys = _moe_experts(xs, plan, w_gate, w_up, w_down, layer)
    return hn, _sc_gather_rows(ys, jnp.concatenate([d1, d2])), rg


def kernel(x, norm_mix, norm_ffn, norm_final, rel_bias_table, w_in_even, g_mla_q, g_mla_kv, w_mla_qb, w_mla_kvb, swa_sinks, w_out_even, w_in_odd, w_out_odd, moe_w_group, moe_b_group, moe_w_router, moe_b_router, moe_w_gate, moe_w_up, moe_w_down):
    B, T, _ = x.shape
    h = x.reshape(B * T, D)
    tb_swa, tb_moba = _bias_tables(rel_bias_table)

    win, wqa, wqb, wk, wv = _even_weights(w_in_even[0], w_mla_qb[0], w_mla_kvb[0])
    q, k, v, qs8, ks, vst = _proj_even(h, norm_mix[0][None, :], win, g_mla_q[0][None, :], g_mla_kv[0][None, :],
                                       wqa, wqb, wk, wv, _rope_tables(T), T)
    oa = _mla_attention(q, k, v, B, T)
    ob = _swa_attention(qs8, ks, vst, swa_sinks[0], tb_swa, B, T)
    h, y12, rg = _ffn(h, oa, ob, w_out_even[0], norm_ffn[0], moe_w_group[0], moe_b_group[0], moe_w_router[0],
                      moe_b_router[0], moe_w_gate, moe_w_up, moe_w_down, 0)

    h, vct, vdt, qc, kc, qd, kd = _combine_proj_odd(h, y12, rg, norm_mix[1][None, :], *_odd_weights(w_in_odd[0]),
                                                   _block_indicator(T), T)
    oc = _moba_attention(qc, kc, vct, tb_moba, B, T)
    od = _sb_attention(qd, kd, vdt, B, T)
    h, y12, rg = _ffn(h, oc, od, w_out_odd[0], norm_ffn[1], moe_w_group[1], moe_b_group[1], moe_w_router[1],
                      moe_b_router[1], moe_w_gate, moe_w_up, moe_w_down, 1)
    out = _combine_final(h, y12, rg, norm_final[None, :])
    return out.reshape(B, T, D)
```

```python
import functools
import math

import numpy as np
import jax
import jax.numpy as jnp
from jax import lax
from jax.experimental import pallas as pl
from jax.experimental.pallas import tpu as pltpu
from jax.experimental.pallas import tpu_sc as plsc

F32 = jnp.float32
BF16 = jnp.bfloat16
NEG = -1e30
EPS = 1e-6

D = 1024
HD = 64
MLA_HEADS, MLA_QR, MLA_KVR, MLA_NOPE, MLA_ROPE, MLA_V = 8, 256, 128, 64, 32, 64
ROPE_THETA = 10000.0
SWA_HEADS, SWA_KV, SWA_WIN = 8, 2, 128
MOBA_L, MOBA_TOPK = 256, 3
REL_BUCKETS, REL_MAX = 32, 2048
N_GROUPS, EPG, N_EXP, D_EXP = 4, 8, 32, 256
BIAS_CONST_FROM = 1792
MOBA_TQ = 2 * MOBA_L
BIAS_PAD = MOBA_L
BIAS_ROWS = BIAS_PAD + BIAS_CONST_FROM + MOBA_TQ
SB_DONE = -160.0
TM = 1024
TM_E = 512
VMEM_LIMIT = 56 * 1024 * 1024
SC_CORES, SC_SUBCORES = 2, 16
SC_CHUNK = 128
HALF = D // 2
VT_ROWS = HD + 16
LOG2E = math.log2(math.e)


def _pack_halves(x):
    lo = pltpu.bitcast(x[:, :HALF].astype(BF16).astype(F32), jnp.uint32) >> 16
    hi = pltpu.bitcast(x[:, HALF:].astype(BF16).astype(F32), jnp.uint32) & jnp.uint32(0xFFFF0000)
    return pltpu.bitcast(lo | hi, jnp.int32)


def _unpack_halves(w):
    w = pltpu.bitcast(w, jnp.uint32)
    return pltpu.bitcast(w << 16, F32), pltpu.bitcast(w & jnp.uint32(0xFFFF0000), F32)


def _cparams(sem):
    return pltpu.CompilerParams(dimension_semantics=sem, vmem_limit_bytes=VMEM_LIMIT)


def _dot(a, b):
    return jnp.dot(a, b, preferred_element_type=F32)


def _dot_nt(a, b):
    return lax.dot_general(a, b, (((1,), (1,)), ((), ())), preferred_element_type=F32)


def _rms(x, g):
    return x * lax.rsqrt(jnp.mean(x * x, axis=-1, keepdims=True) + EPS) * g


def _values_t(vt):
    ones = jnp.ones((VT_ROWS - HD, vt.shape[1]), vt.dtype)
    return jnp.concatenate([blk for h in range(8) for blk in (vt[HD * h:HD * (h + 1)], ones)], axis=0)


def _spread_heads(x, lane_of):
    lane = lax.broadcasted_iota(jnp.int32, (x.shape[0], 128), 1)
    out = []
    for h in range(8):
        pair = x[:, 128 * (h // 2):128 * (h // 2 + 1)]
        if HD * (h % 2) != lane_of(h):
            pair = pltpu.roll(pair, HD, 1)
        out.append(jnp.where((lane >= lane_of(h)) & (lane < lane_of(h) + HD), pair, 0.0))
    return out


def _t5_bucket_np(dist):
    n = np.maximum(dist, 0)
    max_exact = REL_BUCKETS // 2
    nf = np.maximum(n, 1).astype(np.float32)
    large = max_exact + (np.log(nf / np.float32(max_exact)) / np.float32(math.log(REL_MAX / max_exact))
                         * np.float32(REL_BUCKETS - max_exact)).astype(np.int32)
    large = np.minimum(large, REL_BUCKETS - 1)
    return np.where(n < max_exact, n, large).astype(np.int32)


def _bias_kernel(table_ref, bucket_ref, out_ref, *, scale, ranges):
    b = bucket_ref[...]
    for m, (lo, hi) in enumerate(ranges):
        @pl.when(pl.program_id(0) == m)
        def _():
            for h in range(8):
                acc = jnp.full(b.shape, NEG if lo == REL_BUCKETS else table_ref[lo, h] * scale, F32)
                for k in range(lo + 1, min(hi, REL_BUCKETS - 1) + 1):
                    acc = jnp.where(b == k, table_ref[k, h] * scale, acc)
                if hi == REL_BUCKETS:
                    acc = jnp.where(b == REL_BUCKETS, NEG, acc)
                out_ref[h] = acc


def _bias_lookup(rel_table, dist, scale, name, masked=None):
    R, C = dist.shape
    buckets = _t5_bucket_np(dist)
    if masked is not None:
        buckets = np.where(masked, REL_BUCKETS, buckets).astype(np.int32)
    ranges = tuple((int(buckets[:, c:c + 256].min()), int(buckets[:, c:c + 256].max())) for c in range(0, C, 256))
    return pl.pallas_call(
        functools.partial(_bias_kernel, scale=scale, ranges=ranges),
        grid=(C // 256,),
        in_specs=[pl.BlockSpec(memory_space=pltpu.SMEM),
                  pl.BlockSpec((R, 256), lambda m: (0, m))],
        out_specs=pl.BlockSpec((8, R, 256), lambda m: (0, 0, m)),
        out_shape=jax.ShapeDtypeStruct((8, R, C), F32),
        compiler_params=_cparams(("arbitrary",)),
        name=name,
    )(rel_table, jnp.asarray(buckets))


def _bias_tables(rel_table):
    d = np.arange(256)[None, :] - np.arange(256)[:, None]
    swa = _bias_lookup(rel_table, d, LOG2E, "bias_swa", masked=(d < 0) | (d >= SWA_WIN))
    moba = _bias_lookup(rel_table, np.arange(BIAS_ROWS)[None, :] - BIAS_PAD - np.arange(MOBA_L)[:, None], LOG2E,
                        "bias_moba")
    return swa, moba


def _proj_even_kernel(h_ref, g_ref, win_ref, gq_ref, gkv_ref, wqa_ref, wqb_ref, wk_ref, wv_ref,
                      cq_ref, sq_ref, ck_ref, sk_ref,
                      q_ref, k_ref, v_ref, qs_ref, ks_ref, vst_ref):
    u = _rms(h_ref[...], g_ref[...]).astype(BF16)
    nq = _rms(_dot(u, win_ref[:, 0:256]), gq_ref[...]).astype(BF16)
    qa = _dot(nq, wqa_ref[...])
    qb = _dot(nq, wqb_ref[...])
    cq, sq = cq_ref[...], sq_ref[...]
    for hh in range(MLA_HEADS):
        sl = slice(128 * hh, 128 * (hh + 1))
        q_ref[:, sl] = (qa[:, sl] * cq + qb[:, sl] * sq).astype(BF16)
    nkv = _rms(_dot(u, win_ref[:, 256:384]), gkv_ref[...]).astype(BF16)
    kn = _dot(nkv, wk_ref[...])
    v_ref[...] = _values_t(_dot_nt(wv_ref[...], nkv)).astype(BF16)
    rest = _dot(u, win_ref[:, 896:1408])
    kr = rest[:, 128:256] * ck_ref[...] + rest[:, 256:384] * sk_ref[...]
    for hh in range(MLA_HEADS):
        sl = slice(128 * hh, 128 * (hh + 1))
        k_ref[:, sl] = (kn[:, sl] + kr).astype(BF16)
    qs = _spread_heads(_dot(u, win_ref[:, 384:896]), lambda hq: HD * (hq // (SWA_HEADS // SWA_KV)))
    for hq in range(SWA_HEADS):
        qs_ref[:, 128 * hq:128 * (hq + 1)] = qs[hq].astype(BF16)
    ks_ref[...] = rest[:, 0:128].astype(BF16)
    vst_ref[...] = rest[:, 384:512].T.astype(BF16)


def _cols(w, idx, scale=None):
    idx = np.asarray(idx)
    cuts = [0] + [n for n in range(1, len(idx)) if (idx[n] < 0) != (idx[n - 1] < 0)
                  or (idx[n] >= 0 and idx[n] != idx[n - 1] + 1)] + [len(idx)]
    parts = [jnp.zeros((w.shape[0], b - a), w.dtype) if idx[a] < 0 else w[:, idx[a]:idx[a] + b - a]
             for a, b in zip(cuts[:-1], cuts[1:])]
    out = jnp.concatenate(parts, axis=1)
    if scale is not None:
        out = out * scale
    return out


def _even_weights(w_in, w_qb, w_kvb):
    o_cq, o_ckv, o_kr = 0, MLA_QR, MLA_QR + MLA_KVR
    o_qs = o_kr + MLA_ROPE
    o_ks = o_qs + SWA_HEADS * HD
    o_vs = o_ks + SWA_KV * HD
    idx = list(range(o_cq, o_cq + MLA_QR)) + list(range(o_ckv, o_ckv + MLA_KVR))
    scale = [1.0] * len(idx)
    idx += list(range(o_qs, o_qs + SWA_HEADS * HD))
    scale += [HD ** -0.5 * LOG2E] * (SWA_HEADS * HD)
    idx += list(range(o_ks, o_ks + 128))
    scale += [1.0] * 128
    half = MLA_ROPE // 2
    idx += [-1] * 64 + list(range(o_kr, o_kr + MLA_ROPE)) + [-1] * 32
    idx += [-1] * 64 + list(range(o_kr + half, o_kr + MLA_ROPE)) + list(range(o_kr, o_kr + half)) + [-1] * 32
    idx += list(range(o_vs, o_vs + 128))
    scale += [1.0] * 384
    win = _cols(w_in, idx, jnp.asarray(np.asarray(scale, np.float32))).astype(BF16)
    pad = lambda a, lo, hi: jnp.pad(a, ((0, 0), (0, 0), (lo, hi))).reshape(a.shape[0], -1).astype(BF16)
    q3 = w_qb.reshape(MLA_QR, MLA_HEADS, MLA_NOPE + MLA_ROPE)
    swapped = jnp.concatenate([q3[:, :, MLA_NOPE + half:], q3[:, :, MLA_NOPE:MLA_NOPE + half]], axis=2)
    kv3 = w_kvb.reshape(MLA_KVR, MLA_HEADS, MLA_NOPE + MLA_V)
    wv = kv3[:, :, MLA_NOPE:].reshape(MLA_KVR, -1).T.astype(BF16)
    return win, pad(q3, 0, 32), pad(swapped, MLA_NOPE, 32), pad(kv3[:, :, :MLA_NOPE], 0, MLA_V), wv


def _rope_tables(T):
    f32 = np.float32
    freqs = f32(ROPE_THETA) ** (-np.arange(0, MLA_ROPE, 2, dtype=f32) / f32(MLA_ROPE))
    ang = np.arange(T, dtype=f32)[:, None] * freqs[None, :]
    cos, sin = np.cos(ang), np.sin(ang)
    z = lambda n: np.zeros((T, n), f32)
    scale = f32((MLA_NOPE + MLA_ROPE) ** -0.5 * LOG2E)
    ck = np.concatenate([z(64), cos, cos, z(32)], axis=1)
    sk = np.concatenate([z(64), -sin, sin, z(32)], axis=1)
    cq = np.concatenate([np.ones((T, 64), f32), cos, cos, z(32)], axis=1) * scale
    sq = sk * scale
    return tuple(jnp.asarray(t, F32) for t in (cq, sq, ck, sk))


def _proj_even(h, g, win, gq, gkv, wqa, wqb, wk, wv, tables, T):
    N = h.shape[0]
    nt = T // TM
    full = lambda a: pl.BlockSpec(a.shape, lambda i: (0,) * a.ndim)
    tab = pl.BlockSpec((TM, 128), lambda i: (i % nt, 0))
    row = lambda w: pl.BlockSpec((TM, w), lambda i: (i, 0))
    outs = (1024, 1024, -8 * VT_ROWS, 1024, 128, -128)
    spec = lambda w: row(w) if w > 0 else pl.BlockSpec((None, -w, TM), lambda i: (i // nt, 0, i % nt))
    shape = lambda w: jax.ShapeDtypeStruct((N, w) if w > 0 else (N // T, -w, T), BF16)
    return pl.pallas_call(
        _proj_even_kernel,
        grid=(N // TM,),
        in_specs=[row(D), full(g), full(win), full(gq), full(gkv), full(wqa), full(wqb), full(wk), full(wv),
                  tab, tab, tab, tab],
        out_specs=[spec(w) for w in outs],
        out_shape=[shape(w) for w in outs],
        compiler_params=_cparams(("arbitrary",)),
        name="proj_even",
    )(h, g, win, gq, gkv, wqa, wqb, wk, wv, *tables)


def _finish_t(accs):
    ot = jnp.concatenate([acc[:HD] / acc[HD:HD + 1] for acc in accs], axis=0)
    return ot.T.astype(BF16)


def _causal_attention_pair_t(ia, nq, q_of, k_ref, vt_ref, o_ref, s_ref, smax_ref, m_ref, acc_ref, *, tq,
                             bias_of=None):
    nh = s_ref.shape[1]
    first_query = (ia * tq, (nq - 1 - ia) * tq)
    n_items = nq + 1
    assert n_items % 2 == 1
    key = lax.broadcasted_iota(jnp.int32, (tq, tq), 0)
    qry = lax.broadcasted_iota(jnp.int32, (tq, tq), 1)

    def item(p):
        if isinstance(p, int) and p < 2:
            return p, first_query[p], first_query[p]
        t = jnp.where(p - 2 >= ia, 1, 0)
        return t, jnp.where(t == 1, first_query[1], first_query[0]), (p - 2 - t * ia) * tq

    def scores_to(slot, p):
        t, q0, start = item(p)
        start = pl.multiple_of(start, tq)
        for j in range(nh):
            st = _dot_nt(k_ref[pl.ds(start, tq), 128 * j:128 * (j + 1)], q_of(t, q0, j))
            if bias_of is not None:
                st = st + bias_of(start, q0, j)
            if isinstance(p, int) and p < 2:
                st = jnp.where(key <= qry, st, NEG)
            s_ref[slot, j] = st
            smax_ref[slot, j] = jnp.max(st, axis=0, keepdims=True)

    def update_from(slot, p):
        t, _, start = item(p)
        start = pl.multiple_of(start, tq)
        m_new = [jnp.maximum(m_ref[t, j], smax_ref[slot, j]) for j in range(nh)]
        pts = [jnp.exp2(s_ref[slot, j] - m_new[j]).astype(BF16) for j in range(nh)]
        for j in range(nh):
            acc_ref[t, j] = (jnp.exp2(m_ref[t, j] - m_new[j]) * acc_ref[t, j]
                             + _dot(vt_ref[VT_ROWS * j:VT_ROWS * (j + 1), pl.ds(start, tq)], pts[j]))
            m_ref[t, j] = m_new[j]

    def pair(k):
        scores_to(1, k + 1)
        update_from(0, k)
        scores_to(0, k + 2)
        update_from(1, k + 1)

    def quad(c, _):
        pair(2 + 4 * c)
        pair(4 + 4 * c)
        return 0

    m_ref[...] = jnp.full(m_ref.shape, NEG, F32)
    acc_ref[...] = jnp.zeros(acc_ref.shape, F32)
    scores_to(0, 0)
    pair(0)
    later_pairs = (n_items - 1) // 2 - 1
    lax.fori_loop(0, later_pairs // 2, quad, 0)
    if later_pairs % 2:
        pair(n_items - 3)
    update_from(0, n_items - 1)
    for t in range(2):
        o_ref[pl.ds(pl.multiple_of(first_query[t], tq), tq), :] = _finish_t([acc_ref[t, j] for j in range(nh)])


def _attention_t_scratch(tq):
    return [pltpu.VMEM((2, 2, tq, tq), F32), pltpu.VMEM((2, 2, 1, tq), F32), pltpu.VMEM((2, 2, 1, tq), F32),
            pltpu.VMEM((2, 2, VT_ROWS, tq), F32)]


def _mla_kernel(q_ref, k_ref, vt_ref, o_ref, s_ref, smax_ref, m_ref, acc_ref, *, tq, nq):
    def q_of(t, q0, j):
        return q_ref[pl.ds(pl.multiple_of(q0, tq), tq), 128 * j:128 * (j + 1)]

    _causal_attention_pair_t(pl.program_id(2), nq, q_of, k_ref, vt_ref, o_ref, s_ref, smax_ref, m_ref, acc_ref,
                             tq=tq)


def _mla_attention(q, k, vt, B, T, tq=512):
    nq = T // tq
    return pl.pallas_call(
        functools.partial(_mla_kernel, tq=tq, nq=nq),
        grid=(B, MLA_HEADS // 2, nq // 2),
        in_specs=[pl.BlockSpec((T, 256), lambda b, p, i: (b, p)),
                  pl.BlockSpec((T, 256), lambda b, p, i: (b, p)),
                  pl.BlockSpec((None, 2 * VT_ROWS, T), lambda b, p, i: (b, p, 0))],
        out_specs=pl.BlockSpec((T, 128), lambda b, p, i: (b, p)),
        out_shape=jax.ShapeDtypeStruct((B * T, MLA_HEADS * MLA_V), BF16),
        scratch_shapes=_attention_t_scratch(tq),
        compiler_params=_cparams(("arbitrary", "arbitrary", "arbitrary")),
        name="mla_attention",
    )(q, k, vt)


def _swa_kernel(sink_ref, q_ref, k_ref, vt_ref, tb_ref, o_ref, *, tq):
    i = pl.program_id(1)
    G = SWA_HEADS // SWA_KV
    W = 2 * SWA_WIN
    head = lax.broadcasted_iota(jnp.int32, (1, G * 128), 1) >> 7
    sinks = []
    for c in range(SWA_KV):
        sink = jnp.zeros((1, G * 128), F32)
        for g in range(G):
            sink = jnp.where(head == g, sink_ref[G * c + g], sink)
        sinks.append(sink)
    scores, values = [], []
    for r in range(tq // 128):
        qstart = i * tq + 128 * r
        kstart = pl.multiple_of(jnp.maximum(qstart - SWA_WIN, 0), 128)
        off = pl.multiple_of(qstart - kstart, 128)
        kw = k_ref[pl.ds(kstart, W), :]
        for c in range(SWA_KV):
            qg = jnp.concatenate([q_ref[128 * r:128 * (r + 1), 128 * (G * c + g):128 * (G * c + g + 1)]
                                  for g in range(G)], axis=0)
            bias = jnp.concatenate([tb_ref[G * c + g, :, pl.ds(off, 128)] for g in range(G)], axis=1)
            scores.append(_dot_nt(kw, qg) + bias)
            values.append(vt_ref[HD * c:HD * (c + 1), pl.ds(kstart, W)])
    probs = []
    for n, s in enumerate(scores):
        sink = sinks[n % SWA_KV]
        m = jnp.maximum(jnp.max(s, axis=0, keepdims=True), sink)
        p = jnp.exp2(s - m)
        probs.append((p.astype(BF16), jnp.sum(p, axis=0, keepdims=True) + jnp.exp2(sink - m)))
    for r in range(tq // 128):
        outs = []
        for c in range(SWA_KV):
            p, denom = probs[SWA_KV * r + c]
            ot = _dot(values[SWA_KV * r + c], p) / denom
            outs += [ot[:, 128 * g:128 * (g + 1)] for g in range(G)]
        o_ref[128 * r:128 * (r + 1), :] = jnp.concatenate(outs, axis=0).T.astype(BF16)


def _swa_attention(q8, k, vt, sinks, tb, B, T, tq=512):
    nq = T // tq
    return pl.pallas_call(
        functools.partial(_swa_kernel, tq=tq),
        grid=(B, nq),
        in_specs=[pl.BlockSpec(memory_space=pltpu.SMEM),
                  pl.BlockSpec((tq, 1024), lambda b, i: (b * nq + i, 0)),
                  pl.BlockSpec((T, 128), lambda b, i: (b, 0)),
                  pl.BlockSpec((None, 128, T), lambda b, i: (b, 0, 0)),
                  pl.BlockSpec((8, 256, 256), lambda b, i: (0, 0, 0))],
        out_specs=pl.BlockSpec((tq, 512), lambda b, i: (b * nq + i, 0)),
        out_shape=jax.ShapeDtypeStruct((B * T, SWA_HEADS * HD), BF16),
        compiler_params=_cparams(("arbitrary", "arbitrary")),
        name="swa_attention",
    )(sinks * LOG2E, q8, k, vt, tb)


def _outproj_router_kernel(h_ref, oa_ref, ob_ref, wo_ref, g_ref, wr_ref, br_ref,
                           hn_ref, u_ref, ri_ref, rg_ref, cnt_ref, base_ref):
    i = pl.program_id(0)

    @pl.when(i == 0)
    def _():
        base_ref[...] = jnp.zeros_like(base_ref)

    hn = h_ref[...] + _dot(oa_ref[...], wo_ref[0:512, :]) + _dot(ob_ref[...], wo_ref[512:1024, :])
    hn_ref[...] = hn
    u = _rms(hn, g_ref[...])
    u_ref[...] = _pack_halves(u)
    u_hi = u.astype(BF16)
    u_lo = (u - u_hi.astype(F32)).astype(BF16)
    both = _dot_nt(wr_ref[...], u_hi)
    logits = both[:128] + both[128:] + _dot_nt(wr_ref[0:128, :], u_lo) + br_ref[...]
    tm = logits.shape[1]
    sub = lax.broadcasted_iota(jnp.int32, (8, tm), 0).astype(F32)
    big = 1e6
    isg = sub < N_GROUPS
    gl = jnp.where(isg, logits[N_EXP:N_EXP + 8], NEG)
    gmax = jnp.max(gl, axis=0, keepdims=True)
    gsel = jnp.min(jnp.where(gl == gmax, sub, big), axis=0, keepdims=True)
    g_gate = 1.0 / jnp.sum(jnp.where(isg, jnp.exp(gl - gmax), 0.0), axis=0, keepdims=True)
    el = jnp.zeros((EPG, tm), F32)
    for grp in range(N_GROUPS):
        el = jnp.where(gsel == grp, logits[EPG * grp:EPG * (grp + 1)], el)
    m1 = jnp.max(el, axis=0, keepdims=True)
    i1 = jnp.min(jnp.where(el == m1, sub, big), axis=0, keepdims=True)
    el2 = jnp.where(sub == i1, NEG, el)
    m2 = jnp.max(el2, axis=0, keepdims=True)
    i2 = jnp.min(jnp.where(el2 == m2, sub, big), axis=0, keepdims=True)
    e1 = gsel * EPG + i1
    e2 = gsel * EPG + i2
    r = jnp.exp(m2 - m1)
    gate1 = g_gate / (1.0 + r)
    gate2 = g_gate * r / (1.0 + r)
    rows = lax.broadcasted_iota(jnp.int32, (128, tm), 0).astype(F32)
    oh1 = (rows == e1).astype(F32)
    oh2 = (rows == e2).astype(F32)
    oh = oh1 + oh2
    t_row = lax.broadcasted_iota(jnp.int32, (tm, tm), 0)
    t_col = lax.broadcasted_iota(jnp.int32, (tm, tm), 1)
    earlier = jnp.where(t_row < t_col, 1.0, 0.0).astype(BF16)
    prefix = _dot(oh.astype(BF16), earlier) + base_ref[:, 0:1]
    rank1 = jnp.sum(oh1 * prefix, axis=0, keepdims=True)
    rank2 = jnp.sum(oh2 * prefix, axis=0, keepdims=True)
    base_ref[...] = base_ref[...] + jnp.sum(oh, axis=1, keepdims=True)
    cnt_ref[...] = base_ref[...]
    ri_ref[...] = jnp.where(sub == 0, e1, jnp.where(sub == 1, e2, jnp.where(sub == 2, rank1,
                            jnp.where(sub == 3, rank2, 0.0))))
    rg_ref[...] = jnp.where(rows == 0, gate1, jnp.where(rows == 1, gate2, 0.0)).T


def _outproj_router(h, oa, ob, wo, g, wr, br):
    N = h.shape[0]
    full = lambda a: pl.BlockSpec(a.shape, lambda i: (0,) * a.ndim)
    row = lambda w: pl.BlockSpec((TM, w), lambda i: (i, 0))
    return pl.pallas_call(
        _outproj_router_kernel,
        grid=(N // TM,),
        in_specs=[row(D), row(512), row(512), full(wo), full(g), full(wr), full(br)],
        out_specs=[row(D), row(HALF), pl.BlockSpec((8, TM), lambda i: (0, i)), row(128),
                   pl.BlockSpec((128, 128), lambda i: (0, 0))],
        out_shape=[jax.ShapeDtypeStruct((N, D), F32), jax.ShapeDtypeStruct((N, HALF), jnp.int32),
                   jax.ShapeDtypeStruct((8, N), F32), jax.ShapeDtypeStruct((N, 128), F32),
                   jax.ShapeDtypeStruct((128, 128), F32)],
        scratch_shapes=[pltpu.VMEM((128, 128), F32)],
        compiler_params=_cparams(("arbitrary",)),
        name="outproj_router",
    )(h, oa, ob, wo, g, wr, br)


def _moe_kernel(te_ref, tv_ref, run_ref, nx_ref, x_ref, wg_hbm, wu_hbm, wd_hbm, o_ref,
                wg_buf, wu_buf, wd_buf, sems, wgu_s, wd_s, *, layer):
    i = pl.program_id(0)
    valid = tv_ref[i]

    def weight_copies(e, slot):
        return (pltpu.make_async_copy(wg_hbm.at[layer, e], wg_buf.at[slot], sems.at[slot, 0]),
                pltpu.make_async_copy(wu_hbm.at[layer, e], wu_buf.at[slot], sems.at[slot, 1]),
                pltpu.make_async_copy(wd_hbm.at[layer, e], wd_buf.at[slot], sems.at[slot, 2]))

    @pl.when(run_ref[i] > 0)
    def _():
        slot = run_ref[i] - 1

        @pl.when(i == 0)
        def _():
            for c in weight_copies(te_ref[i], slot):
                c.start()

        for c in weight_copies(te_ref[i], slot):
            c.wait()
        wgu_s[:, :D_EXP] = wg_buf[slot].astype(BF16)
        wgu_s[:, D_EXP:] = wu_buf[slot].astype(BF16)
        wd_s[...] = wd_buf[slot].astype(BF16)

        @pl.when(nx_ref[i] >= 0)
        def _():
            for c in weight_copies(nx_ref[i], 1 - slot):
                c.start()

    @pl.when(valid > 0)
    def _():
        rows = lax.broadcasted_iota(jnp.int32, x_ref.shape, 0)
        lo, hi = _unpack_halves(jnp.where(rows < valid, x_ref[...], 0))
        hgu = _dot(lo.astype(BF16), wgu_s[:HALF, :]) + _dot(hi.astype(BF16), wgu_s[HALF:, :])
        hg, hu = hgu[:, :D_EXP], hgu[:, D_EXP:]
        a = hg * (1.0 / (1.0 + jnp.exp(-hg))) * hu
        o_ref[...] = _pack_halves(_dot(a.astype(BF16), wd_s[...]))

    @pl.when(valid <= 0)
    def _():
        o_ref[...] = jnp.zeros_like(o_ref)


def _moe_experts(xs, plan, w_gate, w_up, w_down, layer):
    P = xs.shape[0]
    hbm = pl.BlockSpec(memory_space=pl.ANY)
    grid_spec = pltpu.PrefetchScalarGridSpec(
        num_scalar_prefetch=4,
        grid=(P // TM_E,),
        in_specs=[pl.BlockSpec((TM_E, HALF), lambda i, *_: (i, 0)), hbm, hbm, hbm],
        out_specs=pl.BlockSpec((TM_E, HALF), lambda i, *_: (i, 0)),
        scratch_shapes=[pltpu.VMEM((2, D, D_EXP), F32), pltpu.VMEM((2, D, D_EXP), F32),
                        pltpu.VMEM((2, D_EXP, D), F32), pltpu.SemaphoreType.DMA((2, 3)),
                        pltpu.VMEM((D, 2 * D_EXP), BF16), pltpu.VMEM((D_EXP, D), BF16)],
    )
    return pl.pallas_call(
        functools.partial(_moe_kernel, layer=layer),
        grid_spec=grid_spec,
        out_shape=jax.ShapeDtypeStruct((P, HALF), jnp.int32),
        compiler_params=_cparams(("arbitrary",)),
        name="moe_experts",
    )(*plan, xs, w_gate, w_up, w_down)


def _sc_mesh():
    return plsc.VectorSubcoreMesh(core_axis_name="c", subcore_axis_name="s",
                                  num_cores=SC_CORES, num_subcores=SC_SUBCORES)


def _sc_scatter_rows(src, d1, d2, P):
    N, W = src.shape
    per_w = N // (SC_CORES * SC_SUBCORES)

    @functools.partial(pl.kernel, mesh=_sc_mesh(), out_type=jax.ShapeDtypeStruct((P, W), src.dtype),
                       scratch_types=[pltpu.VMEM((SC_CHUNK,), jnp.int32), pltpu.VMEM((SC_CHUNK, W), src.dtype)],
                       name="sc_scatter_rows")
    def k(src_hbm, d1_hbm, d2_hbm, out_hbm, idx_v, rows_v):
        wid = lax.axis_index("s") * SC_CORES + lax.axis_index("c")

        @pl.loop(0, per_w // SC_CHUNK)
        def _(c):
            off = wid * per_w + c * SC_CHUNK
            pltpu.sync_copy(src_hbm.at[pl.ds(off, SC_CHUNK)], rows_v)
            pltpu.sync_copy(d1_hbm.at[pl.ds(off, SC_CHUNK)], idx_v)
            pltpu.sync_copy(rows_v, out_hbm.at[idx_v])
            pltpu.sync_copy(d2_hbm.at[pl.ds(off, SC_CHUNK)], idx_v)
            pltpu.sync_copy(rows_v, out_hbm.at[idx_v])

    return k(src, d1, d2)


def _sc_gather_rows(table, idx):
    B, W = idx.shape[0], table.shape[1]
    per_w = B // (SC_CORES * SC_SUBCORES)

    @functools.partial(pl.kernel, mesh=_sc_mesh(), out_type=jax.ShapeDtypeStruct((B, W), table.dtype),
                       scratch_types=[pltpu.VMEM((SC_CHUNK,), jnp.int32), pltpu.VMEM((SC_CHUNK, W), table.dtype)],
                       name="sc_gather_rows")
    def k(table_hbm, idx_hbm, out_hbm, idx_v, rows_v):
        wid = lax.axis_index("s") * SC_CORES + lax.axis_index("c")

        @pl.loop(0, per_w // SC_CHUNK)
        def _(c):
            off = wid * per_w + c * SC_CHUNK
            pltpu.sync_copy(idx_hbm.at[pl.ds(off, SC_CHUNK)], idx_v)
            pltpu.sync_copy(table_hbm.at[idx_v], rows_v)
            pltpu.sync_copy(rows_v, out_hbm.at[pl.ds(off, SC_CHUNK)])

    return k(table, idx)


def _dest_kernel(offs_ref, ri_ref, d_ref):
    ri = ri_ref[...]
    base = jnp.zeros(ri.shape, F32)
    for e in range(N_EXP):
        base = jnp.where(ri == e, offs_ref[e].astype(F32), base)
    d_ref[...] = (base + pltpu.roll(ri, 6, 0)).astype(jnp.int32)


def _dispatch_plan(ri, cnt, N):
    counts = cnt[:N_EXP, 0].astype(jnp.int32)
    padded = ((counts + TM_E - 1) // TM_E) * TM_E
    ends = jnp.cumsum(padded)
    offs = ends - padded
    take = lambda t, i: t.at[i].get(mode="promise_in_bounds")
    dest = pl.pallas_call(
        _dest_kernel,
        in_specs=[pl.BlockSpec(memory_space=pltpu.SMEM), pl.BlockSpec((8, N), lambda: (0, 0))],
        out_specs=pl.BlockSpec((8, N), lambda: (0, 0)),
        out_shape=jax.ShapeDtypeStruct((8, N), jnp.int32),
        name="dispatch_rows",
    )(offs, ri)
    d1, d2 = dest[0], dest[1]
    P = 2 * N + N_EXP * TM_E
    tile_start = jnp.arange(P // TM_E, dtype=jnp.int32) * TM_E
    tile_expert = jnp.minimum(jnp.sum((ends[None, :] <= tile_start[:, None]).astype(jnp.int32), axis=1), N_EXP - 1)
    tile_valid = jnp.clip(take(offs + counts, tile_expert) - tile_start, 0, TM_E)
    n_tiles = P // TM_E
    tile = jnp.arange(n_tiles, dtype=jnp.int32)
    starts = (tile_valid > 0) & ((tile == 0) | (tile_expert != jnp.roll(tile_expert, 1)))
    slot = (jnp.cumsum(starts.astype(jnp.int32)) - 1) % 2
    run_code = jnp.where(starts, slot + 1, 0).astype(jnp.int32)
    start_at = jnp.where(starts, tile, n_tiles)
    next_start = jnp.concatenate([lax.cummin(start_at, reverse=True)[1:], jnp.full((1,), n_tiles, jnp.int32)])
    next_expert = jnp.where(next_start < n_tiles, take(tile_expert, jnp.minimum(next_start, n_tiles - 1)), -1)
    return d1, d2, P, (tile_expert, tile_valid, run_code, next_expert.astype(jnp.int32))


def _combine(h_ref, y1_ref, y2_ref, rg_ref):
    rg = rg_ref[...]
    lo1, hi1 = _unpack_halves(y1_ref[...])
    lo2, hi2 = _unpack_halves(y2_ref[...])
    g1, g2 = rg[:, 0:1], rg[:, 1:2]
    return h_ref[...] + jnp.concatenate([g1 * lo1 + g2 * lo2, g1 * hi1 + g2 * hi2], axis=1)


ODD_WIDTHS = (1024, 1024, 512, 512)


def _combine_proj_odd_kernel(h_ref, y1_ref, y2_ref, rg_ref, g_ref, win_ref, wvt_ref, ind_ref, hn_ref, vt_ref,
                             vdt_ref, *out_refs):
    hn = _combine(h_ref, y1_ref, y2_ref, rg_ref)
    hn_ref[...] = hn
    u = _rms(hn, g_ref[...]).astype(BF16)
    vt_ref[...] = _values_t(_dot_nt(wvt_ref[0:512, :], u)).astype(BF16)
    vdt_ref[...] = _dot_nt(wvt_ref[512:1024, :], u).astype(BF16)
    qc_ref, kc_ref, qd_ref, kd_ref = out_refs
    qc = _spread_heads(_dot(u, win_ref[:, 0:512]), lambda h: 0)
    kc = _spread_heads(_dot(u, win_ref[:, 512:1024]), lambda h: 0)
    for hh in range(8):
        qc_ref[:, 128 * hh:128 * (hh + 1)] = qc[hh].astype(BF16)
        kc_ref[:, 128 * hh:128 * (hh + 1)] = (kc[hh] + ind_ref[...]).astype(BF16)
    qd_ref[...] = _dot(u, win_ref[:, 1024:1536]).astype(BF16)
    kd_ref[...] = _dot(u, win_ref[:, 1536:2048]).astype(BF16)


def _odd_weights(w_in):
    s = HD ** -0.5 * LOG2E
    scale = np.asarray([s] * 512 + [1.0] * 512 + [s] * 512 + [1.0] * 512, np.float32)
    win = jnp.concatenate([w_in[:, 0:1024], w_in[:, 1536:2560]], axis=1) * jnp.asarray(scale)
    wvt = jnp.concatenate([w_in[:, 1024:1536], w_in[:, 2560:3072]], axis=1).astype(BF16).T
    return win.astype(BF16), wvt


def _block_indicator(T):
    assert T // MOBA_L <= 64
    ind = np.zeros((T, 128), np.float32)
    ind[np.arange(T), 64 + np.arange(T) // MOBA_L] = 1.0
    return jnp.asarray(ind)


def _y_specs(N):
    return [pl.BlockSpec((TM, HALF), lambda i: (i, 0)), pl.BlockSpec((TM, HALF), lambda i: (i + N // TM, 0))]


def _combine_proj_odd(h, y12, rg, g, win, wvt, ind, T):
    N = h.shape[0]
    nt = T // TM
    full = lambda a: pl.BlockSpec(a.shape, lambda i: (0,) * a.ndim)
    row = lambda w: pl.BlockSpec((TM, w), lambda i: (i, 0))
    return pl.pallas_call(
        _combine_proj_odd_kernel,
        grid=(N // TM,),
        in_specs=[row(D)] + _y_specs(N) + [row(128), full(g), full(win), full(wvt),
                  pl.BlockSpec((TM, 128), lambda i: (i % nt, 0))],
        out_specs=[row(D), pl.BlockSpec((None, 8 * VT_ROWS, TM), lambda i: (i // nt, 0, i % nt)),
                   pl.BlockSpec((None, 512, TM), lambda i: (i // nt, 0, i % nt))]
        + [row(w) for w in ODD_WIDTHS],
        out_shape=[jax.ShapeDtypeStruct((N, D), F32), jax.ShapeDtypeStruct((N // T, 8 * VT_ROWS, T), BF16),
                   jax.ShapeDtypeStruct((N // T, 512, T), BF16)]
        + [jax.ShapeDtypeStruct((N, w), BF16) for w in ODD_WIDTHS],
        compiler_params=_cparams(("arbitrary",)),
        name="combine_proj_odd",
    )(h, y12, y12, rg, g, win, wvt, ind)


def _combine_final_kernel(h_ref, y1_ref, y2_ref, rg_ref, g_ref, o_ref):
    o_ref[...] = _rms(_combine(h_ref, y1_ref, y2_ref, rg_ref), g_ref[...])


def _combine_final(h, y12, rg, g):
    N = h.shape[0]
    row = lambda w: pl.BlockSpec((TM, w), lambda i: (i, 0))
    return pl.pallas_call(
        _combine_final_kernel,
        grid=(N // TM,),
        in_specs=[row(D)] + _y_specs(N) + [row(128), pl.BlockSpec((1, D), lambda i: (0, 0))],
        out_specs=row(D),
        out_shape=jax.ShapeDtypeStruct((N, D), F32),
        compiler_params=_cparams(("arbitrary",)),
        name="combine_final",
    )(h, y12, y12, rg, g)


def _moba_kernel(q_ref, k_ref, vt_ref, tb_ref, o_ref, kmean_ref, ksplit_ref, qa_ref, s_ref, smax_ref, m_ref, acc_ref,
                 *, nkb, nq):
    ia = pl.program_id(2)
    L, tq = MOBA_L, MOBA_TQ

    @pl.when(ia == 0)
    def _():
        kmean_ref[...] = jnp.zeros_like(kmean_ref)
        for n in range(nkb):
            kmean_ref[64 + n:65 + n, :] = jnp.sum(k_ref[n * L:(n + 1) * L, :].astype(F32), axis=0,
                                                  keepdims=True) * (1.0 / L)
        hi = kmean_ref[...].astype(BF16)
        ksplit_ref[0] = hi
        ksplit_ref[1] = (kmean_ref[...] - hi.astype(F32)).astype(BF16)

    nb = -(-nkb // 8) * 8
    lane = lax.broadcasted_iota(jnp.int32, (tq, 128), 1)
    blk = lax.broadcasted_iota(jnp.int32, (nb, tq), 0)
    half = jnp.where(lax.broadcasted_iota(jnp.int32, (nb, tq), 1) >= L, 1, 0)
    blk_f = blk.astype(F32)
    for t, tile in enumerate((ia, nq - 1 - ia)):
        own = 2 * tile + half
        for j in range(2):
            qj = q_ref[pl.ds(pl.multiple_of(tile * tq, tq), tq), 128 * j:128 * (j + 1)]
            gsc = (_dot_nt(ksplit_ref[0, 64:64 + nb, 128 * j:128 * (j + 1)], qj)
                   + _dot_nt(ksplit_ref[1, 64:64 + nb, 128 * j:128 * (j + 1)], qj))
            g = jnp.where(blk < own, gsc, NEG)
            allowed = jnp.where(blk == own, 1.0, 0.0)
            for _ in range(MOBA_TOPK):
                mx = jnp.max(g, axis=0, keepdims=True)
                idx = jnp.min(jnp.where(g == mx, blk_f, 1e6), axis=0, keepdims=True)
                pick = blk_f == idx
                allowed = jnp.maximum(allowed, jnp.where(pick, jnp.where(mx > 0.5 * NEG, 1.0, 0.0), 0.0))
                g = jnp.where(pick, 2.0 * NEG, g)
            mask = jnp.concatenate([jnp.zeros((64, tq), F32), jnp.where(allowed > 0.5, 0.0, NEG),
                                    jnp.zeros((64 - nb, tq), F32)], axis=0)
            qa_ref[t, j] = jnp.where(lane < HD, qj, mask.T.astype(BF16))

    def bias_of(start, q0, j):
        parts = []
        for c in range(tq // L):
            off = jnp.minimum(q0 - start - c * L, BIAS_CONST_FROM) + BIAS_PAD
            parts.append(tb_ref[j, :, pl.ds(pl.multiple_of(off, L), tq)])
        return jnp.concatenate(parts, axis=0)

    _causal_attention_pair_t(ia, nq, lambda t, q0, j: qa_ref[t, j], k_ref, vt_ref, o_ref, s_ref, smax_ref, m_ref,
                             acc_ref, tq=tq, bias_of=bias_of)


def _moba_attention(q, k, vt, tb, B, T):
    tq = MOBA_TQ
    nq = T // tq
    return pl.pallas_call(
        functools.partial(_moba_kernel, nkb=T // MOBA_L, nq=nq),
        grid=(B, 4, nq // 2),
        in_specs=[pl.BlockSpec((T, 256), lambda b, p, i: (b, p)),
                  pl.BlockSpec((T, 256), lambda b, p, i: (b, p)),
                  pl.BlockSpec((None, 2 * VT_ROWS, T), lambda b, p, i: (b, p, 0)),
                  pl.BlockSpec((2, MOBA_L, BIAS_ROWS), lambda b, p, i: (p, 0, 0))],
        out_specs=pl.BlockSpec((T, 128), lambda b, p, i: (b, p)),
        out_shape=jax.ShapeDtypeStruct((B * T, 512), BF16),
        scratch_shapes=[pltpu.VMEM((128, 256), F32), pltpu.VMEM((2, 128, 256), BF16),
                        pltpu.VMEM((2, 2, tq, 128), BF16)]
        + _attention_t_scratch(tq),
        compiler_params=_cparams(("arbitrary", "arbitrary", "arbitrary")),
        name="moba_attention",
    )(q, k, vt, tb)


def _sb_kernel(q_ref, k_ref, vt_ref, o_ref, *, tq, pairs):
    i = pl.program_id(2)
    lane = lax.broadcasted_iota(jnp.int32, (tq, 128), 1)
    qs = []
    for p in range(pairs):
        q_pair = q_ref[:, 128 * p:128 * (p + 1)]
        qs += [jnp.where(lane < HD, q_pair, 0).astype(BF16), jnp.where(lane >= HD, q_pair, 0).astype(BF16)]
    key = lax.broadcasted_iota(jnp.int32, (tq, tq), 0)
    qry = lax.broadcasted_iota(jnp.int32, (tq, tq), 1)
    past = key < qry
    suffix = jnp.where(qry >= key, -1.0, 0.0).astype(BF16)
    heads = range(2 * pairs)

    def step(kb, carry, boundary):
        start = pl.multiple_of(kb * tq, tq)
        zs = [_dot_nt(k_ref[pl.ds(start, tq), 128 * (h // 2):128 * (h // 2 + 1)], qs[h]) for h in heads]
        parts = []
        for z in zs:
            neg_abs = pltpu.bitcast(pltpu.bitcast(z, jnp.uint32) | jnp.uint32(0x80000000), F32)
            soft = jnp.maximum(z, 0.0) + jnp.log2(1.0 + jnp.exp2(neg_abs))
            if boundary:
                soft = jnp.where(past, soft, 0.0)
            parts.append(soft.astype(BF16))
        sums = [_dot(suffix, part) for part in parts]
        ws = []
        for h in heads:
            w = jnp.exp2(zs[h] + (carry[h][0] + sums[h]))
            ws.append((jnp.where(past, w, 0.0) if boundary else w).astype(BF16))
        return tuple((carry[h][0] + sums[h][0:1], carry[h][1]
                      + _dot(vt_ref[128 * (h // 2):128 * (h // 2 + 1), pl.ds(start, tq)], ws[h])) for h in heads)

    def live(carry):
        top = carry[0][0]
        for c, _ in carry[1:]:
            top = jnp.maximum(top, c)
        return (jnp.max(top) > SB_DONE).astype(jnp.int32)

    def body(state):
        t, _, carry = state
        carry = step(i - 1 - t, carry, False)
        return t + 1, live(carry), carry

    init = tuple((jnp.zeros((1, tq), F32), jnp.zeros((128, tq), F32)) for _ in range(2 * pairs))
    carry = step(i, init, True)
    _, _, carry = lax.while_loop(lambda st: (st[0] < i) & (st[1] > 0), body, (jnp.int32(0), live(carry), carry))
    rows = lax.broadcasted_iota(jnp.int32, (128, tq), 0)
    for p in range(pairs):
        ot = jnp.where(rows < HD, carry[2 * p][1], carry[2 * p + 1][1])
        o_ref[:, 128 * p:128 * (p + 1)] = ot.T.astype(BF16)


def _sb_attention(q, k, vt, B, T, tq=256, pairs=4):
    nq = T // tq
    w = 128 * pairs
    return pl.pallas_call(
        functools.partial(_sb_kernel, tq=tq, pairs=pairs),
        grid=(B, 4 // pairs, nq),
        in_specs=[pl.BlockSpec((tq, w), lambda b, p, i: (b * nq + i, p)),
                  pl.BlockSpec((T, w), lambda b, p, i: (b, p)),
                  pl.BlockSpec((None, w, T), lambda b, p, i: (b, p, 0))],
        out_specs=pl.BlockSpec((tq, w), lambda b, p, i: (b * nq + i, p)),
        out_shape=jax.ShapeDtypeStruct((B * T, 512), BF16),
        compiler_params=_cparams(("arbitrary", "arbitrary", "arbitrary")),
        name="sb_attention",
    )(q, k, vt)


def _router_weights(w_group, b_group, w_router, b_router):
    pad = 128 - N_EXP - N_GROUPS
    wr = jnp.concatenate([w_router, w_group, jnp.zeros((D, pad), F32)], axis=1).T
    hi = wr.astype(BF16)
    lo = (wr - hi.astype(F32)).astype(BF16)
    br = jnp.concatenate([b_router, b_group, jnp.zeros((pad,), F32)])[:, None]
    return jnp.concatenate([hi, lo], axis=0), br


def _ffn(h, oa, ob, wo, g, w_group, b_group, w_router, b_router, w_gate, w_up, w_down, layer):
    wr, br = _router_weights(w_group, b_group, w_router, b_router)
    hn, u, ri, rg, cnt = _outproj_router(h, oa, ob, wo.astype(BF16), g[None, :], wr, br)
    d1, d2, P, plan = _dispatch_plan(ri, cnt, h.shape[0])
    xs = _sc_scatter_rows(u, d1, d2, P)
    ys = _moe_experts(xs, plan, w_gate, w_up, w_down, layer)
    return hn, _sc_gather_rows(ys, jnp.concatenate([d1, d2])), rg


def kernel(x, norm_mix, norm_ffn, norm_final, rel_bias_table, w_in_even, g_mla_q, g_mla_kv, w_mla_qb, w_mla_kvb, swa_sinks, w_out_even, w_in_odd, w_out_odd, moe_w_group, moe_b_group, moe_w_router, moe_b_router, moe_w_gate, moe_w_up, moe_w_down):
    B, T, _ = x.shape
    h = x.reshape(B * T, D)
    tb_swa, tb_moba = _bias_tables(rel_bias_table)

    win, wqa, wqb, wk, wv = _even_weights(w_in_even[0], w_mla_qb[0], w_mla_kvb[0])
    q, k, v, qs8, ks, vst = _proj_even(h, norm_mix[0][None, :], win, g_mla_q[0][None, :], g_mla_kv[0][None, :],
                                       wqa, wqb, wk, wv, _rope_tables(T), T)
    oa = _mla_attention(q, k, v, B, T)
    ob = _swa_attention(qs8, ks, vst, swa_sinks[0], tb_swa, B, T)
    h, y12, rg = _ffn(h, oa, ob, w_out_even[0], norm_ffn[0], moe_w_group[0], moe_b_group[0], moe_w_router[0],
                      moe_b_router[0], moe_w_gate, moe_w_up, moe_w_down, 0)

    h, vct, vdt, qc, kc, qd, kd = _combine_proj_odd(h, y12, rg, norm_mix[1][None, :], *_odd_weights(w_in_odd[0]),
                                                   _block_indicator(T), T)
    oc = _moba_attention(qc, kc, vct, tb_moba, B, T)
    od = _sb_attention(qd, kd, vdt, B, T)
    h, y12, rg = _ffn(h, oc, od, w_out_odd[0], norm_ffn[1], moe_w_group[1], moe_b_group[1], moe_w_router[1],
                      moe_b_router[1], moe_w_gate, moe_w_up, moe_w_down, 1)
    out = _combine_final(h, y12, rg, norm_final[None, :])
    return out.reshape(B, T, D)
```

```python
import functools
import math

import numpy as np
import jax
import jax.numpy as jnp
from jax import lax
from jax.experimental import pallas as pl
from jax.experimental.pallas import tpu as pltpu
from jax.experimental.pallas import tpu_sc as plsc

F32 = jnp.float32
BF16 = jnp.bfloat16
NEG = -1e30
EPS = 1e-6

D = 1024
HD = 64
MLA_HEADS, MLA_QR, MLA_KVR, MLA_NOPE, MLA_ROPE, MLA_V = 8, 256, 128, 64, 32, 64
ROPE_THETA = 10000.0
SWA_HEADS, SWA_KV, SWA_WIN = 8, 2, 128
MOBA_L, MOBA_TOPK = 256, 3
REL_BUCKETS, REL_MAX = 32, 2048
N_GROUPS, EPG, N_EXP, D_EXP = 4, 8, 32, 256
BIAS_CONST_FROM = 1792
MOBA_TQ = 2 * MOBA_L
BIAS_PAD = MOBA_L
BIAS_ROWS = BIAS_PAD + BIAS_CONST_FROM + MOBA_TQ
SB_DONE = -160.0
TM = 1024
TM_E = 512
VMEM_LIMIT = 56 * 1024 * 1024
SC_CORES, SC_SUBCORES = 2, 16
SC_CHUNK = 128
HALF = D // 2
VT_ROWS = HD + 16
LOG2E = math.log2(math.e)


def _pack_halves(x):
    lo = pltpu.bitcast(x[:, :HALF].astype(BF16).astype(F32), jnp.uint32) >> 16
    hi = pltpu.bitcast(x[:, HALF:].astype(BF16).astype(F32), jnp.uint32) & jnp.uint32(0xFFFF0000)
    return pltpu.bitcast(lo | hi, jnp.int32)


def _unpack_halves(w):
    w = pltpu.bitcast(w, jnp.uint32)
    return pltpu.bitcast(w << 16, F32), pltpu.bitcast(w & jnp.uint32(0xFFFF0000), F32)


def _cparams(sem):
    return pltpu.CompilerParams(dimension_semantics=sem, vmem_limit_bytes=VMEM_LIMIT)


def _dot(a, b):
    return jnp.dot(a, b, preferred_element_type=F32)


def _dot_nt(a, b):
    return lax.dot_general(a, b, (((1,), (1,)), ((), ())), preferred_element_type=F32)


def _rms(x, g):
    return x * lax.rsqrt(jnp.mean(x * x, axis=-1, keepdims=True) + EPS) * g


def _values_t(vt):
    ones = jnp.ones((VT_ROWS - HD, vt.shape[1]), vt.dtype)
    return jnp.concatenate([blk for h in range(8) for blk in (vt[HD * h:HD * (h + 1)], ones)], axis=0)


def _spread_heads(x, lane_of):
    lane = lax.broadcasted_iota(jnp.int32, (x.shape[0], 128), 1)
    out = []
    for h in range(8):
        pair = x[:, 128 * (h // 2):128 * (h // 2 + 1)]
        if HD * (h % 2) != lane_of(h):
            pair = pltpu.roll(pair, HD, 1)
        out.append(jnp.where((lane >= lane_of(h)) & (lane < lane_of(h) + HD), pair, 0.0))
    return out


def _t5_bucket_np(dist):
    n = np.maximum(dist, 0)
    max_exact = REL_BUCKETS // 2
    nf = np.maximum(n, 1).astype(np.float32)
    large = max_exact + (np.log(nf / np.float32(max_exact)) / np.float32(math.log(REL_MAX / max_exact))
                         * np.float32(REL_BUCKETS - max_exact)).astype(np.int32)
    large = np.minimum(large, REL_BUCKETS - 1)
    return np.where(n < max_exact, n, large).astype(np.int32)


def _bias_kernel(table_ref, bucket_ref, out_ref, *, scale, ranges):
    b = bucket_ref[...]
    for m, (lo, hi) in enumerate(ranges):
        @pl.when(pl.program_id(0) == m)
        def _():
            for h in range(8):
                acc = jnp.full(b.shape, NEG if lo == REL_BUCKETS else table_ref[lo, h] * scale, F32)
                for k in range(lo + 1, min(hi, REL_BUCKETS - 1) + 1):
                    acc = jnp.where(b == k, table_ref[k, h] * scale, acc)
                if hi == REL_BUCKETS:
                    acc = jnp.where(b == REL_BUCKETS, NEG, acc)
                out_ref[h] = acc


def _bias_lookup(rel_table, dist, scale, name, masked=None):
    R, C = dist.shape
    buckets = _t5_bucket_np(dist)
    if masked is not None:
        buckets = np.where(masked, REL_BUCKETS, buckets).astype(np.int32)
    ranges = tuple((int(buckets[:, c:c + 256].min()), int(buckets[:, c:c + 256].max())) for c in range(0, C, 256))
    return pl.pallas_call(
        functools.partial(_bias_kernel, scale=scale, ranges=ranges),
        grid=(C // 256,),
        in_specs=[pl.BlockSpec(memory_space=pltpu.SMEM),
                  pl.BlockSpec((R, 256), lambda m: (0, m))],
        out_specs=pl.BlockSpec((8, R, 256), lambda m: (0, 0, m)),
        out_shape=jax.ShapeDtypeStruct((8, R, C), F32),
        compiler_params=_cparams(("arbitrary",)),
        name=name,
    )(rel_table, jnp.asarray(buckets))


def _bias_tables(rel_table):
    d = np.arange(256)[None, :] - np.arange(256)[:, None]
    swa = _bias_lookup(rel_table, d, LOG2E, "bias_swa", masked=(d < 0) | (d >= SWA_WIN))
    moba = _bias_lookup(rel_table, np.arange(BIAS_ROWS)[None, :] - BIAS_PAD - np.arange(MOBA_L)[:, None], LOG2E,
                        "bias_moba")
    return swa, moba


def _proj_even_kernel(h_ref, g_ref, win_ref, gq_ref, gkv_ref, wqa_ref, wqb_ref, wk_ref, wv_ref,
                      cq_ref, sq_ref, ck_ref, sk_ref,
                      q_ref, k_ref, v_ref, qs_ref, ks_ref, vst_ref):
    u = _rms(h_ref[...], g_ref[...]).astype(BF16)
    nq = _rms(_dot(u, win_ref[:, 0:256]), gq_ref[...]).astype(BF16)
    qa = _dot(nq, wqa_ref[...])
    qb = _dot(nq, wqb_ref[...])
    cq, sq = cq_ref[...], sq_ref[...]
    for hh in range(MLA_HEADS):
        sl = slice(128 * hh, 128 * (hh + 1))
        q_ref[:, sl] = (qa[:, sl] * cq + qb[:, sl] * sq).astype(BF16)
    nkv = _rms(_dot(u, win_ref[:, 256:384]), gkv_ref[...]).astype(BF16)
    kn = _dot(nkv, wk_ref[...])
    v_ref[...] = _values_t(_dot_nt(wv_ref[...], nkv)).astype(BF16)
    rest = _dot(u, win_ref[:, 896:1408])
    kr = rest[:, 128:256] * ck_ref[...] + rest[:, 256:384] * sk_ref[...]
    for hh in range(MLA_HEADS):
        sl = slice(128 * hh, 128 * (hh + 1))
        k_ref[:, sl] = (kn[:, sl] + kr).astype(BF16)
    qs = _spread_heads(_dot(u, win_ref[:, 384:896]), lambda hq: HD * (hq // (SWA_HEADS // SWA_KV)))
    for hq in range(SWA_HEADS):
        qs_ref[:, 128 * hq:128 * (hq + 1)] = qs[hq].astype(BF16)
    ks_ref[...] = rest[:, 0:128].astype(BF16)
    vst_ref[...] = rest[:, 384:512].T.astype(BF16)


def _cols(w, idx, scale=None):
    idx = np.asarray(idx)
    cuts = [0] + [n for n in range(1, len(idx)) if (idx[n] < 0) != (idx[n - 1] < 0)
                  or (idx[n] >= 0 and idx[n] != idx[n - 1] + 1)] + [len(idx)]
    parts = [jnp.zeros((w.shape[0], b - a), w.dtype) if idx[a] < 0 else w[:, idx[a]:idx[a] + b - a]
             for a, b in zip(cuts[:-1], cuts[1:])]
    out = jnp.concatenate(parts, axis=1)
    if scale is not None:
        out = out * scale
    return out


def _even_weights(w_in, w_qb, w_kvb):
    o_cq, o_ckv, o_kr = 0, MLA_QR, MLA_QR + MLA_KVR
    o_qs = o_kr + MLA_ROPE
    o_ks = o_qs + SWA_HEADS * HD
    o_vs = o_ks + SWA_KV * HD
    idx = list(range(o_cq, o_cq + MLA_QR)) + list(range(o_ckv, o_ckv + MLA_KVR))
    scale = [1.0] * len(idx)
    idx += list(range(o_qs, o_qs + SWA_HEADS * HD))
    scale += [HD ** -0.5 * LOG2E] * (SWA_HEADS * HD)
    idx += list(range(o_ks, o_ks + 128))
    scale += [1.0] * 128
    half = MLA_ROPE // 2
    idx += [-1] * 64 + list(range(o_kr, o_kr + MLA_ROPE)) + [-1] * 32
    idx += [-1] * 64 + list(range(o_kr + half, o_kr + MLA_ROPE)) + list(range(o_kr, o_kr + half)) + [-1] * 32
    idx += list(range(o_vs, o_vs + 128))
    scale += [1.0] * 384
    win = _cols(w_in, idx, jnp.asarray(np.asarray(scale, np.float32))).astype(BF16)
    pad = lambda a, lo, hi: jnp.pad(a, ((0, 0), (0, 0), (lo, hi))).reshape(a.shape[0], -1).astype(BF16)
    q3 = w_qb.reshape(MLA_QR, MLA_HEADS, MLA_NOPE + MLA_ROPE)
    swapped = jnp.concatenate([q3[:, :, MLA_NOPE + half:], q3[:, :, MLA_NOPE:MLA_NOPE + half]], axis=2)
    kv3 = w_kvb.reshape(MLA_KVR, MLA_HEADS, MLA_NOPE + MLA_V)
    wv = kv3[:, :, MLA_NOPE:].reshape(MLA_KVR, -1).T.astype(BF16)
    return win, pad(q3, 0, 32), pad(swapped, MLA_NOPE, 32), pad(kv3[:, :, :MLA_NOPE], 0, MLA_V), wv


def _rope_tables(T):
    f32 = np.float32
    freqs = f32(ROPE_THETA) ** (-np.arange(0, MLA_ROPE, 2, dtype=f32) / f32(MLA_ROPE))
    ang = np.arange(T, dtype=f32)[:, None] * freqs[None, :]
    cos, sin = np.cos(ang), np.sin(ang)
    z = lambda n: np.zeros((T, n), f32)
    scale = f32((MLA_NOPE + MLA_ROPE) ** -0.5 * LOG2E)
    ck = np.concatenate([z(64), cos, cos, z(32)], axis=1)
    sk = np.concatenate([z(64), -sin, sin, z(32)], axis=1)
    cq = np.concatenate([np.ones((T, 64), f32), cos, cos, z(32)], axis=1) * scale
    sq = sk * scale
    return tuple(jnp.asarray(t, F32) for t in (cq, sq, ck, sk))


def _proj_even(h, g, win, gq, gkv, wqa, wqb, wk, wv, tables, T):
    N = h.shape[0]
    nt = T // TM
    full = lambda a: pl.BlockSpec(a.shape, lambda i: (0,) * a.ndim)
    tab = pl.BlockSpec((TM, 128), lambda i: (i % nt, 0))
    row = lambda w: pl.BlockSpec((TM, w), lambda i: (i, 0))
    outs = (1024, 1024, -8 * VT_ROWS, 1024, 128, -128)
    spec = lambda w: row(w) if w > 0 else pl.BlockSpec((None, -w, TM), lambda i: (i // nt, 0, i % nt))
    shape = lambda w: jax.ShapeDtypeStruct((N, w) if w > 0 else (N // T, -w, T), BF16)
    return pl.pallas_call(
        _proj_even_kernel,
        grid=(N // TM,),
        in_specs=[row(D), full(g), full(win), full(gq), full(gkv), full(wqa), full(wqb), full(wk), full(wv),
                  tab, tab, tab, tab],
        out_specs=[spec(w) for w in outs],
        out_shape=[shape(w) for w in outs],
        compiler_params=_cparams(("arbitrary",)),
        name="proj_even",
    )(h, g, win, gq, gkv, wqa, wqb, wk, wv, *tables)


def _finish_t(accs):
    ot = jnp.concatenate([acc[:HD] / acc[HD:HD + 1] for acc in accs], axis=0)
    return ot.T.astype(BF16)


def _causal_attention_pair_t(ia, nq, q_of, k_ref, vt_ref, o_ref, s_ref, smax_ref, m_ref, acc_ref, *, tq,
                             bias_of=None):
    nh = s_ref.shape[1]
    first_query = (ia * tq, (nq - 1 - ia) * tq)
    n_items = nq + 1
    assert n_items % 2 == 1
    key = lax.broadcasted_iota(jnp.int32, (tq, tq), 0)
    qry = lax.broadcasted_iota(jnp.int32, (tq, tq), 1)

    def item(p):
        if isinstance(p, int) and p < 2:
            return p, first_query[p], first_query[p]
        t = jnp.where(p - 2 >= ia, 1, 0)
        return t, jnp.where(t == 1, first_query[1], first_query[0]), (p - 2 - t * ia) * tq

    def scores_to(slot, p):
        t, q0, start = item(p)
        start = pl.multiple_of(start, tq)
        for j in range(nh):
            st = _dot_nt(k_ref[pl.ds(start, tq), 128 * j:128 * (j + 1)], q_of(t, q0, j))
            if bias_of is not None:
                st = st + bias_of(start, q0, j)
            if isinstance(p, int) and p < 2:
                st = jnp.where(key <= qry, st, NEG)
            s_ref[slot, j] = st
            smax_ref[slot, j] = jnp.max(st, axis=0, keepdims=True)

    def update_from(slot, p):
        t, _, start = item(p)
        start = pl.multiple_of(start, tq)
        m_new = [jnp.maximum(m_ref[t, j], smax_ref[slot, j]) for j in range(nh)]
        pts = [jnp.exp2(s_ref[slot, j] - m_new[j]).astype(BF16) for j in range(nh)]
        for j in range(nh):
            acc_ref[t, j] = (jnp.exp2(m_ref[t, j] - m_new[j]) * acc_ref[t, j]
                             + _dot(vt_ref[VT_ROWS * j:VT_ROWS * (j + 1), pl.ds(start, tq)], pts[j]))
            m_ref[t, j] = m_new[j]

    def pair(k):
        scores_to(1, k + 1)
        update_from(0, k)
        scores_to(0, k + 2)
        update_from(1, k + 1)

    def quad(c, _):
        pair(2 + 4 * c)
        pair(4 + 4 * c)
        return 0

    m_ref[...] = jnp.full(m_ref.shape, NEG, F32)
    acc_ref[...] = jnp.zeros(acc_ref.shape, F32)
    scores_to(0, 0)
    pair(0)
    later_pairs = (n_items - 1) // 2 - 1
    lax.fori_loop(0, later_pairs // 2, quad, 0)
    if later_pairs % 2:
        pair(n_items - 3)
    update_from(0, n_items - 1)
    for t in range(2):
        o_ref[pl.ds(pl.multiple_of(first_query[t], tq), tq), :] = _finish_t([acc_ref[t, j] for j in range(nh)])


def _attention_t_scratch(tq):
    return [pltpu.VMEM((2, 2, tq, tq), F32), pltpu.VMEM((2, 2, 1, tq), F32), pltpu.VMEM((2, 2, 1, tq), F32),
            pltpu.VMEM((2, 2, VT_ROWS, tq), F32)]


def _mla_kernel(q_ref, k_ref, vt_ref, o_ref, s_ref, smax_ref, m_ref, acc_ref, *, tq, nq):
    def q_of(t, q0, j):
        return q_ref[pl.ds(pl.multiple_of(q0, tq), tq), 128 * j:128 * (j + 1)]

    _causal_attention_pair_t(pl.program_id(2), nq, q_of, k_ref, vt_ref, o_ref, s_ref, smax_ref, m_ref, acc_ref,
                             tq=tq)


def _mla_attention(q, k, vt, B, T, tq=512):
    nq = T // tq
    return pl.pallas_call(
        functools.partial(_mla_kernel, tq=tq, nq=nq),
        grid=(B, MLA_HEADS // 2, nq // 2),
        in_specs=[pl.BlockSpec((T, 256), lambda b, p, i: (b, p)),
                  pl.BlockSpec((T, 256), lambda b, p, i: (b, p)),
                  pl.BlockSpec((None, 2 * VT_ROWS, T), lambda b, p, i: (b, p, 0))],
        out_specs=pl.BlockSpec((T, 128), lambda b, p, i: (b, p)),
        out_shape=jax.ShapeDtypeStruct((B * T, MLA_HEADS * MLA_V), BF16),
        scratch_shapes=_attention_t_scratch(tq),
        compiler_params=_cparams(("arbitrary", "arbitrary", "arbitrary")),
        name="mla_attention",
    )(q, k, vt)


def _swa_kernel(sink_ref, q_ref, k_ref, vt_ref, tb_ref, o_ref, *, tq):
    i = pl.program_id(1)
    G = SWA_HEADS // SWA_KV
    W = 2 * SWA_WIN
    head = lax.broadcasted_iota(jnp.int32, (1, G * 128), 1) >> 7
    sinks = []
    for c in range(SWA_KV):
        sink = jnp.zeros((1, G * 128), F32)
        for g in range(G):
            sink = jnp.where(head == g, sink_ref[G * c + g], sink)
        sinks.append(sink)
    scores, values = [], []
    for r in range(tq // 128):
        qstart = i * tq + 128 * r
        kstart = pl.multiple_of(jnp.maximum(qstart - SWA_WIN, 0), 128)
        off = pl.multiple_of(qstart - kstart, 128)
        kw = k_ref[pl.ds(kstart, W), :]
        for c in range(SWA_KV):
            qg = jnp.concatenate([q_ref[128 * r:128 * (r + 1), 128 * (G * c + g):128 * (G * c + g + 1)]
                                  for g in range(G)], axis=0)
            bias = jnp.concatenate([tb_ref[G * c + g, :, pl.ds(off, 128)] for g in range(G)], axis=1)
            scores.append(_dot_nt(kw, qg) + bias)
            values.append(vt_ref[HD * c:HD * (c + 1), pl.ds(kstart, W)])
    probs = []
    for n, s in enumerate(scores):
        sink = sinks[n % SWA_KV]
        m = jnp.maximum(jnp.max(s, axis=0, keepdims=True), sink)
        p = jnp.exp2(s - m)
        probs.append((p.astype(BF16), jnp.sum(p, axis=0, keepdims=True) + jnp.exp2(sink - m)))
    for r in range(tq // 128):
        outs = []
        for c in range(SWA_KV):
            p, denom = probs[SWA_KV * r + c]
            ot = _dot(values[SWA_KV * r + c], p) / denom
            outs += [ot[:, 128 * g:128 * (g + 1)] for g in range(G)]
        o_ref[128 * r:128 * (r + 1), :] = jnp.concatenate(outs, axis=0).T.astype(BF16)


def _swa_attention(q8, k, vt, sinks, tb, B, T, tq=512):
    nq = T // tq
    return pl.pallas_call(
        functools.partial(_swa_kernel, tq=tq),
        grid=(B, nq),
        in_specs=[pl.BlockSpec(memory_space=pltpu.SMEM),
                  pl.BlockSpec((tq, 1024), lambda b, i: (b * nq + i, 0)),
                  pl.BlockSpec((T, 128), lambda b, i: (b, 0)),
                  pl.BlockSpec((None, 128, T), lambda b, i: (b, 0, 0)),
                  pl.BlockSpec((8, 256, 256), lambda b, i: (0, 0, 0))],
        out_specs=pl.BlockSpec((tq, 512), lambda b, i: (b * nq + i, 0)),
        out_shape=jax.ShapeDtypeStruct((B * T, SWA_HEADS * HD), BF16),
        compiler_params=_cparams(("arbitrary", "arbitrary")),
        name="swa_attention",
    )(sinks * LOG2E, q8, k, vt, tb)


def _outproj_router_kernel(h_ref, oa_ref, ob_ref, wo_ref, g_ref, wr_ref, br_ref,
                           hn_ref, u_ref, ri_ref, rg_ref, cnt_ref, base_ref):
    i = pl.program_id(0)

    @pl.when(i == 0)
    def _():
        base_ref[...] = jnp.zeros_like(base_ref)

    hn = h_ref[...] + _dot(oa_ref[...], wo_ref[0:512, :]) + _dot(ob_ref[...], wo_ref[512:1024, :])
    hn_ref[...] = hn
    u = _rms(hn, g_ref[...])
    u_ref[...] = _pack_halves(u)
    u_hi = u.astype(BF16)
    u_lo = (u - u_hi.astype(F32)).astype(BF16)
    both = _dot_nt(wr_ref[...], u_hi)
    logits = both[:128] + both[128:] + _dot_nt(wr_ref[0:128, :], u_lo) + br_ref[...]
    tm = logits.shape[1]
    sub = lax.broadcasted_iota(jnp.int32, (8, tm), 0).astype(F32)
    big = 1e6
    isg = sub < N_GROUPS
    gl = jnp.where(isg, logits[N_EXP:N_EXP + 8], NEG)
    gmax = jnp.max(gl, axis=0, keepdims=True)
    gsel = jnp.min(jnp.where(gl == gmax, sub, big), axis=0, keepdims=True)
    g_gate = 1.0 / jnp.sum(jnp.where(isg, jnp.exp(gl - gmax), 0.0), axis=0, keepdims=True)
    el = jnp.zeros((EPG, tm), F32)
    for grp in range(N_GROUPS):
        el = jnp.where(gsel == grp, logits[EPG * grp:EPG * (grp + 1)], el)
    m1 = jnp.max(el, axis=0, keepdims=True)
    i1 = jnp.min(jnp.where(el == m1, sub, big), axis=0, keepdims=True)
    el2 = jnp.where(sub == i1, NEG, el)
    m2 = jnp.max(el2, axis=0, keepdims=True)
    i2 = jnp.min(jnp.where(el2 == m2, sub, big), axis=0, keepdims=True)
    e1 = gsel * EPG + i1
    e2 = gsel * EPG + i2
    r = jnp.exp(m2 - m1)
    gate1 = g_gate / (1.0 + r)
    gate2 = g_gate * r / (1.0 + r)
    rows = lax.broadcasted_iota(jnp.int32, (128, tm), 0).astype(F32)
    oh1 = (rows == e1).astype(F32)
    oh2 = (rows == e2).astype(F32)
    oh = oh1 + oh2
    t_row = lax.broadcasted_iota(jnp.int32, (tm, tm), 0)
    t_col = lax.broadcasted_iota(jnp.int32, (tm, tm), 1)
    earlier = jnp.where(t_row < t_col, 1.0, 0.0).astype(BF16)
    prefix = _dot(oh.astype(BF16), earlier) + base_ref[:, 0:1]
    rank1 = jnp.sum(oh1 * prefix, axis=0, keepdims=True)
    rank2 = jnp.sum(oh2 * prefix, axis=0, keepdims=True)
    base_ref[...] = base_ref[...] + jnp.sum(oh, axis=1, keepdims=True)
    cnt_ref[...] = base_ref[...]
    ri_ref[...] = jnp.where(sub == 0, e1, jnp.where(sub == 1, e2, jnp.where(sub == 2, rank1,
                            jnp.where(sub == 3, rank2, 0.0))))
    rg_ref[...] = jnp.where(rows == 0, gate1, jnp.where(rows == 1, gate2, 0.0)).T


def _outproj_router(h, oa, ob, wo, g, wr, br):
    N = h.shape[0]
    full = lambda a: pl.BlockSpec(a.shape, lambda i: (0,) * a.ndim)
    row = lambda w: pl.BlockSpec((TM, w), lambda i: (i, 0))
    return pl.pallas_call(
        _outproj_router_kernel,
        grid=(N // TM,),
        in_specs=[row(D), row(512), row(512), full(wo), full(g), full(wr), full(br)],
        out_specs=[row(D), row(HALF), pl.BlockSpec((8, TM), lambda i: (0, i)), row(128),
                   pl.BlockSpec((128, 128), lambda i: (0, 0))],
        out_shape=[jax.ShapeDtypeStruct((N, D), F32), jax.ShapeDtypeStruct((N, HALF), jnp.int32),
                   jax.ShapeDtypeStruct((8, N), F32), jax.ShapeDtypeStruct((N, 128), F32),
                   jax.ShapeDtypeStruct((128, 128), F32)],
        scratch_shapes=[pltpu.VMEM((128, 128), F32)],
        compiler_params=_cparams(("arbitrary",)),
        name="outproj_router",
    )(h, oa, ob, wo, g, wr, br)


def _moe_kernel(te_ref, tv_ref, run_ref, nx_ref, x_ref, wg_hbm, wu_hbm, wd_hbm, o_ref,
                wg_buf, wu_buf, wd_buf, sems, wgu_s, wd_s, *, layer):
    i = pl.program_id(0)
    valid = tv_ref[i]

    def weight_copies(e, slot):
        return (pltpu.make_async_copy(wg_hbm.at[layer, e], wg_buf.at[slot], sems.at[slot, 0]),
                pltpu.make_async_copy(wu_hbm.at[layer, e], wu_buf.at[slot], sems.at[slot, 1]),
                pltpu.make_async_copy(wd_hbm.at[layer, e], wd_buf.at[slot], sems.at[slot, 2]))

    @pl.when(run_ref[i] > 0)
    def _():
        slot = run_ref[i] - 1

        @pl.when(i == 0)
        def _():
            for c in weight_copies(te_ref[i], slot):
                c.start()

        for c in weight_copies(te_ref[i], slot):
            c.wait()
        wgu_s[:, :D_EXP] = wg_buf[slot].astype(BF16)
        wgu_s[:, D_EXP:] = wu_buf[slot].astype(BF16)
        wd_s[...] = wd_buf[slot].astype(BF16)

        @pl.when(nx_ref[i] >= 0)
        def _():
            for c in weight_copies(nx_ref[i], 1 - slot):
                c.start()

    @pl.when(valid > 0)
    def _():
        rows = lax.broadcasted_iota(jnp.int32, x_ref.shape, 0)
        lo, hi = _unpack_halves(jnp.where(rows < valid, x_ref[...], 0))
        hgu = _dot(lo.astype(BF16), wgu_s[:HALF, :]) + _dot(hi.astype(BF16), wgu_s[HALF:, :])
        hg, hu = hgu[:, :D_EXP], hgu[:, D_EXP:]
        a = hg * (1.0 / (1.0 + jnp.exp(-hg))) * hu
        o_ref[...] = _pack_halves(_dot(a.astype(BF16), wd_s[...]))

    @pl.when(valid <= 0)
    def _():
        o_ref[...] = jnp.zeros_like(o_ref)


def _moe_experts(xs, plan, w_gate, w_up, w_down, layer):
    P = xs.shape[0]
    hbm = pl.BlockSpec(memory_space=pl.ANY)
    grid_spec = pltpu.PrefetchScalarGridSpec(
        num_scalar_prefetch=4,
        grid=(P // TM_E,),
        in_specs=[pl.BlockSpec((TM_E, HALF), lambda i, *_: (i, 0)), hbm, hbm, hbm],
        out_specs=pl.BlockSpec((TM_E, HALF), lambda i, *_: (i, 0)),
        scratch_shapes=[pltpu.VMEM((2, D, D_EXP), F32), pltpu.VMEM((2, D, D_EXP), F32),
                        pltpu.VMEM((2, D_EXP, D), F32), pltpu.SemaphoreType.DMA((2, 3)),
                        pltpu.VMEM((D, 2 * D_EXP), BF16), pltpu.VMEM((D_EXP, D), BF16)],
    )
    return pl.pallas_call(
        functools.partial(_moe_kernel, layer=layer),
        grid_spec=grid_spec,
        out_shape=jax.ShapeDtypeStruct((P, HALF), jnp.int32),
        compiler_params=_cparams(("arbitrary",)),
        name="moe_experts",
    )(*plan, xs, w_gate, w_up, w_down)


def _sc_mesh():
    return plsc.VectorSubcoreMesh(core_axis_name="c", subcore_axis_name="s",
                                  num_cores=SC_CORES, num_subcores=SC_SUBCORES)


def _sc_scatter_rows(src, d1, d2, P):
    N, W = src.shape
    per_w = N // (SC_CORES * SC_SUBCORES)

    @functools.partial(pl.kernel, mesh=_sc_mesh(), out_type=jax.ShapeDtypeStruct((P, W), src.dtype),
                       scratch_types=[pltpu.VMEM((SC_CHUNK,), jnp.int32), pltpu.VMEM((SC_CHUNK, W), src.dtype)],
                       name="sc_scatter_rows")
    def k(src_hbm, d1_hbm, d2_hbm, out_hbm, idx_v, rows_v):
        wid = lax.axis_index("s") * SC_CORES + lax.axis_index("c")

        @pl.loop(0, per_w // SC_CHUNK)
        def _(c):
            off = wid * per_w + c * SC_CHUNK
            pltpu.sync_copy(src_hbm.at[pl.ds(off, SC_CHUNK)], rows_v)
            pltpu.sync_copy(d1_hbm.at[pl.ds(off, SC_CHUNK)], idx_v)
            pltpu.sync_copy(rows_v, out_hbm.at[idx_v])
            pltpu.sync_copy(d2_hbm.at[pl.ds(off, SC_CHUNK)], idx_v)
            pltpu.sync_copy(rows_v, out_hbm.at[idx_v])

    return k(src, d1, d2)


def _sc_gather_rows(table, idx):
    B, W = idx.shape[0], table.shape[1]
    per_w = B // (SC_CORES * SC_SUBCORES)

    @functools.partial(pl.kernel, mesh=_sc_mesh(), out_type=jax.ShapeDtypeStruct((B, W), table.dtype),
                       scratch_types=[pltpu.VMEM((SC_CHUNK,), jnp.int32), pltpu.VMEM((SC_CHUNK, W), table.dtype)],
                       name="sc_gather_rows")
    def k(table_hbm, idx_hbm, out_hbm, idx_v, rows_v):
        wid = lax.axis_index("s") * SC_CORES + lax.axis_index("c")

        @pl.loop(0, per_w // SC_CHUNK)
        def _(c):
            off = wid * per_w + c * SC_CHUNK
            pltpu.sync_copy(idx_hbm.at[pl.ds(off, SC_CHUNK)], idx_v)
            pltpu.sync_copy(table_hbm.at[idx_v], rows_v)
            pltpu.sync_copy(rows_v, out_hbm.at[pl.ds(off, SC_CHUNK)])

    return k(table, idx)


def _dest_kernel(offs_ref, ri_ref, d_ref):
    ri = ri_ref[...]
    base = jnp.zeros(ri.shape, F32)
    for e in range(N_EXP):
        base = jnp.where(ri == e, offs_ref[e].astype(F32), base)
    d_ref[...] = (base + pltpu.roll(ri, 6, 0)).astype(jnp.int32)


def _dispatch_plan(ri, cnt, N):
    counts = cnt[:N_EXP, 0].astype(jnp.int32)
    padded = ((counts + TM_E - 1) // TM_E) * TM_E
    ends = jnp.cumsum(padded)
    offs = ends - padded
    take = lambda t, i: t.at[i].get(mode="promise_in_bounds")
    dest = pl.pallas_call(
        _dest_kernel,
        in_specs=[pl.BlockSpec(memory_space=pltpu.SMEM), pl.BlockSpec((8, N), lambda: (0, 0))],
        out_specs=pl.BlockSpec((8, N), lambda: (0, 0)),
        out_shape=jax.ShapeDtypeStruct((8, N), jnp.int32),
        name="dispatch_rows",
    )(offs, ri)
    d1, d2 = dest[0], dest[1]
    P = 2 * N + N_EXP * TM_E
    tile_start = jnp.arange(P // TM_E, dtype=jnp.int32) * TM_E
    tile_expert = jnp.minimum(jnp.sum((ends[None, :] <= tile_start[:, None]).astype(jnp.int32), axis=1), N_EXP - 1)
    tile_valid = jnp.clip(take(offs + counts, tile_expert) - tile_start, 0, TM_E)
    n_tiles = P // TM_E
    tile = jnp.arange(n_tiles, dtype=jnp.int32)
    starts = (tile_valid > 0) & ((tile == 0) | (tile_expert != jnp.roll(tile_expert, 1)))
    slot = (jnp.cumsum(starts.astype(jnp.int32)) - 1) % 2
    run_code = jnp.where(starts, slot + 1, 0).astype(jnp.int32)
    start_at = jnp.where(starts, tile, n_tiles)
    next_start = jnp.concatenate([lax.cummin(start_at, reverse=True)[1:], jnp.full((1,), n_tiles, jnp.int32)])
    next_expert = jnp.where(next_start < n_tiles, take(tile_expert, jnp.minimum(next_start, n_tiles - 1)), -1)
    return d1, d2, P, (tile_expert, tile_valid, run_code, next_expert.astype(jnp.int32))


def _combine(h_ref, y1_ref, y2_ref, rg_ref):
    rg = rg_ref[...]
    lo1, hi1 = _unpack_halves(y1_ref[...])
    lo2, hi2 = _unpack_halves(y2_ref[...])
    g1, g2 = rg[:, 0:1], rg[:, 1:2]
    return h_ref[...] + jnp.concatenate([g1 * lo1 + g2 * lo2, g1 * hi1 + g2 * hi2], axis=1)


ODD_WIDTHS = (1024, 1024, 512, 512)


def _combine_proj_odd_kernel(h_ref, y1_ref, y2_ref, rg_ref, g_ref, win_ref, wvt_ref, ind_ref, hn_ref, vt_ref,
                             vdt_ref, *out_refs):
    hn = _combine(h_ref, y1_ref, y2_ref, rg_ref)
    hn_ref[...] = hn
    u = _rms(hn, g_ref[...]).astype(BF16)
    vt_ref[...] = _values_t(_dot_nt(wvt_ref[0:512, :], u)).astype(BF16)
    vdt_ref[...] = _dot_nt(wvt_ref[512:1024, :], u).astype(BF16)
    qc_ref, kc_ref, qd_ref, kd_ref = out_refs
    qc = _spread_heads(_dot(u, win_ref[:, 0:512]), lambda h: 0)
    kc = _spread_heads(_dot(u, win_ref[:, 512:1024]), lambda h: 0)
    for hh in range(8):
        qc_ref[:, 128 * hh:128 * (hh + 1)] = qc[hh].astype(BF16)
        kc_ref[:, 128 * hh:128 * (hh + 1)] = (kc[hh] + ind_ref[...]).astype(BF16)
    qd_ref[...] = _dot(u, win_ref[:, 1024:1536]).astype(BF16)
    kd_ref[...] = _dot(u, win_ref[:, 1536:2048]).astype(BF16)


def _odd_weights(w_in):
    s = HD ** -0.5 * LOG2E
    scale = np.asarray([s] * 512 + [1.0] * 512 + [s] * 512 + [1.0] * 512, np.float32)
    win = jnp.concatenate([w_in[:, 0:1024], w_in[:, 1536:2560]], axis=1) * jnp.asarray(scale)
    wvt = jnp.concatenate([w_in[:, 1024:1536], w_in[:, 2560:3072]], axis=1).astype(BF16).T
    return win.astype(BF16), wvt


def _block_indicator(T):
    assert T // MOBA_L <= 64
    ind = np.zeros((T, 128), np.float32)
    ind[np.arange(T), 64 + np.arange(T) // MOBA_L] = 1.0
    return jnp.asarray(ind)


def _y_specs(N):
    return [pl.BlockSpec((TM, HALF), lambda i: (i, 0)), pl.BlockSpec((TM, HALF), lambda i: (i + N // TM, 0))]


def _combine_proj_odd(h, y12, rg, g, win, wvt, ind, T):
    N = h.shape[0]
    nt = T // TM
    full = lambda a: pl.BlockSpec(a.shape, lambda i: (0,) * a.ndim)
    row = lambda w: pl.BlockSpec((TM, w), lambda i: (i, 0))
    return pl.pallas_call(
        _combine_proj_odd_kernel,
        grid=(N // TM,),
        in_specs=[row(D)] + _y_specs(N) + [row(128), full(g), full(win), full(wvt),
                  pl.BlockSpec((TM, 128), lambda i: (i % nt, 0))],
        out_specs=[row(D), pl.BlockSpec((None, 8 * VT_ROWS, TM), lambda i: (i // nt, 0, i % nt)),
                   pl.BlockSpec((None, 512, TM), lambda i: (i // nt, 0, i % nt))]
        + [row(w) for w in ODD_WIDTHS],
        out_shape=[jax.ShapeDtypeStruct((N, D), F32), jax.ShapeDtypeStruct((N // T, 8 * VT_ROWS, T), BF16),
                   jax.ShapeDtypeStruct((N // T, 512, T), BF16)]
        + [jax.ShapeDtypeStruct((N, w), BF16) for w in ODD_WIDTHS],
        compiler_params=_cparams(("arbitrary",)),
        name="combine_proj_odd",
    )(h, y12, y12, rg, g, win, wvt, ind)


def _combine_final_kernel(h_ref, y1_ref, y2_ref, rg_ref, g_ref, *rest):
    rest[-1][...] = _rms(_combine(h_ref, y1_ref, y2_ref, rg_ref), g_ref[...])


def _combine_final(h, ys, d1, d2, rg, g):
    N = h.shape[0]
    steps = N // 2 // TM
    out = None
    for half in range(2):
        rows = slice(half * (N // 2), (half + 1) * (N // 2))
        y12 = _sc_gather_rows(ys, jnp.concatenate([d1[rows], d2[rows]]))
        row = lambda w, first=half * steps: pl.BlockSpec((TM, w), lambda i: (i + first, 0))
        in_specs = [row(D), pl.BlockSpec((TM, HALF), lambda i: (i, 0)), pl.BlockSpec((TM, HALF), lambda i: (i + steps, 0)),
                    row(128), pl.BlockSpec((1, D), lambda i: (0, 0))]
        args = [h, y12, y12, rg, g]
        if out is not None:
            in_specs.append(pl.BlockSpec(memory_space=pl.ANY))
            args.append(out)
        out = pl.pallas_call(
            _combine_final_kernel,
            grid=(steps,),
            in_specs=in_specs,
            out_specs=row(D),
            out_shape=jax.ShapeDtypeStruct((N, D), F32),
            input_output_aliases={} if half == 0 else {5: 0},
            compiler_params=_cparams(("arbitrary",)),
            name="combine_final",
        )(*args)
    return out


def _moba_kernel(q_ref, k_ref, vt_ref, tb_ref, o_ref, kmean_ref, ksplit_ref, qa_ref, s_ref, smax_ref, m_ref, acc_ref,
                 *, nkb, nq):
    ia = pl.program_id(2)
    L, tq = MOBA_L, MOBA_TQ

    @pl.when(ia == 0)
    def _():
        kmean_ref[...] = jnp.zeros_like(kmean_ref)
        for n in range(nkb):
            kmean_ref[64 + n:65 + n, :] = jnp.sum(k_ref[n * L:(n + 1) * L, :].astype(F32), axis=0,
                                                  keepdims=True) * (1.0 / L)
        hi = kmean_ref[...].astype(BF16)
        ksplit_ref[0] = hi
        ksplit_ref[1] = (kmean_ref[...] - hi.astype(F32)).astype(BF16)

    nb = -(-nkb // 8) * 8
    lane = lax.broadcasted_iota(jnp.int32, (tq, 128), 1)
    blk = lax.broadcasted_iota(jnp.int32, (nb, tq), 0)
    half = jnp.where(lax.broadcasted_iota(jnp.int32, (nb, tq), 1) >= L, 1, 0)
    blk_f = blk.astype(F32)
    for t, tile in enumerate((ia, nq - 1 - ia)):
        own = 2 * tile + half
        for j in range(2):
            qj = q_ref[pl.ds(pl.multiple_of(tile * tq, tq), tq), 128 * j:128 * (j + 1)]
            gsc = (_dot_nt(ksplit_ref[0, 64:64 + nb, 128 * j:128 * (j + 1)], qj)
                   + _dot_nt(ksplit_ref[1, 64:64 + nb, 128 * j:128 * (j + 1)], qj))
            g = jnp.where(blk < own, gsc, NEG)
            allowed = jnp.where(blk == own, 1.0, 0.0)
            for _ in range(MOBA_TOPK):
                mx = jnp.max(g, axis=0, keepdims=True)
                idx = jnp.min(jnp.where(g == mx, blk_f, 1e6), axis=0, keepdims=True)
                pick = blk_f == idx
                allowed = jnp.maximum(allowed, jnp.where(pick, jnp.where(mx > 0.5 * NEG, 1.0, 0.0), 0.0))
                g = jnp.where(pick, 2.0 * NEG, g)
            mask = jnp.concatenate([jnp.zeros((64, tq), F32), jnp.where(allowed > 0.5, 0.0, NEG),
                                    jnp.zeros((64 - nb, tq), F32)], axis=0)
            qa_ref[t, j] = jnp.where(lane < HD, qj, mask.T.astype(BF16))

    def bias_of(start, q0, j):
        parts = []
        for c in range(tq // L):
            off = jnp.minimum(q0 - start - c * L, BIAS_CONST_FROM) + BIAS_PAD
            parts.append(tb_ref[j, :, pl.ds(pl.multiple_of(off, L), tq)])
        return jnp.concatenate(parts, axis=0)

    _causal_attention_pair_t(ia, nq, lambda t, q0, j: qa_ref[t, j], k_ref, vt_ref, o_ref, s_ref, smax_ref, m_ref,
                             acc_ref, tq=tq, bias_of=bias_of)


def _moba_attention(q, k, vt, tb, B, T):
    tq = MOBA_TQ
    nq = T // tq
    return pl.pallas_call(
        functools.partial(_moba_kernel, nkb=T // MOBA_L, nq=nq),
        grid=(B, 4, nq // 2),
        in_specs=[pl.BlockSpec((T, 256), lambda b, p, i: (b, p)),
                  pl.BlockSpec((T, 256), lambda b, p, i: (b, p)),
                  pl.BlockSpec((None, 2 * VT_ROWS, T), lambda b, p, i: (b, p, 0)),
                  pl.BlockSpec((2, MOBA_L, BIAS_ROWS), lambda b, p, i: (p, 0, 0))],
        out_specs=pl.BlockSpec((T, 128), lambda b, p, i: (b, p)),
        out_shape=jax.ShapeDtypeStruct((B * T, 512), BF16),
        scratch_shapes=[pltpu.VMEM((128, 256), F32), pltpu.VMEM((2, 128, 256), BF16),
                        pltpu.VMEM((2, 2, tq, 128), BF16)]
        + _attention_t_scratch(tq),
        compiler_params=_cparams(("arbitrary", "arbitrary", "arbitrary")),
        name="moba_attention",
    )(q, k, vt, tb)


def _sb_kernel(q_ref, k_ref, vt_ref, o_ref, *, tq, pairs):
    i = pl.program_id(2)
    lane = lax.broadcasted_iota(jnp.int32, (tq, 128), 1)
    qs = []
    for p in range(pairs):
        q_pair = q_ref[:, 128 * p:128 * (p + 1)]
        qs += [jnp.where(lane < HD, q_pair, 0).astype(BF16), jnp.where(lane >= HD, q_pair, 0).astype(BF16)]
    key = lax.broadcasted_iota(jnp.int32, (tq, tq), 0)
    qry = lax.broadcasted_iota(jnp.int32, (tq, tq), 1)
    past = key < qry
    suffix = jnp.where(qry >= key, -1.0, 0.0).astype(BF16)
    heads = range(2 * pairs)

    def step(kb, carry, boundary):
        start = pl.multiple_of(kb * tq, tq)
        zs = [_dot_nt(k_ref[pl.ds(start, tq), 128 * (h // 2):128 * (h // 2 + 1)], qs[h]) for h in heads]
        parts = []
        for z in zs:
            neg_abs = pltpu.bitcast(pltpu.bitcast(z, jnp.uint32) | jnp.uint32(0x80000000), F32)
            soft = jnp.maximum(z, 0.0) + jnp.log2(1.0 + jnp.exp2(neg_abs))
            if boundary:
                soft = jnp.where(past, soft, 0.0)
            parts.append(soft.astype(BF16))
        sums = [_dot(suffix, part) for part in parts]
        ws = []
        for h in heads:
            w = jnp.exp2(zs[h] + (carry[h][0] + sums[h]))
            ws.append((jnp.where(past, w, 0.0) if boundary else w).astype(BF16))
        return tuple((carry[h][0] + sums[h][0:1], carry[h][1]
                      + _dot(vt_ref[128 * (h // 2):128 * (h // 2 + 1), pl.ds(start, tq)], ws[h])) for h in heads)

    def live(carry):
        top = carry[0][0]
        for c, _ in carry[1:]:
            top = jnp.maximum(top, c)
        return (jnp.max(top) > SB_DONE).astype(jnp.int32)

    def body(state):
        t, _, carry = state
        carry = step(i - 1 - t, carry, False)
        return t + 1, live(carry), carry

    init = tuple((jnp.zeros((1, tq), F32), jnp.zeros((128, tq), F32)) for _ in range(2 * pairs))
    carry = step(i, init, True)
    _, _, carry = lax.while_loop(lambda st: (st[0] < i) & (st[1] > 0), body, (jnp.int32(0), live(carry), carry))
    rows = lax.broadcasted_iota(jnp.int32, (128, tq), 0)
    for p in range(pairs):
        ot = jnp.where(rows < HD, carry[2 * p][1], carry[2 * p + 1][1])
        o_ref[:, 128 * p:128 * (p + 1)] = ot.T.astype(BF16)


def _sb_attention(q, k, vt, B, T, tq=256, pairs=4):
    nq = T // tq
    w = 128 * pairs
    return pl.pallas_call(
        functools.partial(_sb_kernel, tq=tq, pairs=pairs),
        grid=(B, 4 // pairs, nq),
        in_specs=[pl.BlockSpec((tq, w), lambda b, p, i: (b * nq + i, p)),
                  pl.BlockSpec((T, w), lambda b, p, i: (b, p)),
                  pl.BlockSpec((None, w, T), lambda b, p, i: (b, p, 0))],
        out_specs=pl.BlockSpec((tq, w), lambda b, p, i: (b * nq + i, p)),
        out_shape=jax.ShapeDtypeStruct((B * T, 512), BF16),
        compiler_params=_cparams(("arbitrary", "arbitrary", "arbitrary")),
        name="sb_attention",
    )(q, k, vt)


def _router_weights(w_group, b_group, w_router, b_router):
    pad = 128 - N_EXP - N_GROUPS
    wr = jnp.concatenate([w_router, w_group, jnp.zeros((D, pad), F32)], axis=1).T
    hi = wr.astype(BF16)
    lo = (wr - hi.astype(F32)).astype(BF16)
    br = jnp.concatenate([b_router, b_group, jnp.zeros((pad,), F32)])[:, None]
    return jnp.concatenate([hi, lo], axis=0), br


def _ffn(h, oa, ob, wo, g, w_group, b_group, w_router, b_router, w_gate, w_up, w_down, layer):
    wr, br = _router_weights(w_group, b_group, w_router, b_router)
    hn, u, ri, rg, cnt = _outproj_router(h, oa, ob, wo.astype(BF16), g[None, :], wr, br)
    d1, d2, P, plan = _dispatch_plan(ri, cnt, h.shape[0])
    xs = _sc_scatter_rows(u, d1, d2, P)
    return hn, _moe_experts(xs, plan, w_gate, w_up, w_down, layer), d1, d2, rg


def kernel(x, norm_mix, norm_ffn, norm_final, rel_bias_table, w_in_even, g_mla_q, g_mla_kv, w_mla_qb, w_mla_kvb, swa_sinks, w_out_even, w_in_odd, w_out_odd, moe_w_group, moe_b_group, moe_w_router, moe_b_router, moe_w_gate, moe_w_up, moe_w_down):
    B, T, _ = x.shape
    h = x.reshape(B * T, D)
    tb_swa, tb_moba = _bias_tables(rel_bias_table)

    win, wqa, wqb, wk, wv = _even_weights(w_in_even[0], w_mla_qb[0], w_mla_kvb[0])
    q, k, v, qs8, ks, vst = _proj_even(h, norm_mix[0][None, :], win, g_mla_q[0][None, :], g_mla_kv[0][None, :],
                                       wqa, wqb, wk, wv, _rope_tables(T), T)
    oa = _mla_attention(q, k, v, B, T)
    ob = _swa_attention(qs8, ks, vst, swa_sinks[0], tb_swa, B, T)
    h, ys, d1, d2, rg = _ffn(h, oa, ob, w_out_even[0], norm_ffn[0], moe_w_group[0], moe_b_group[0], moe_w_router[0],
                             moe_b_router[0], moe_w_gate, moe_w_up, moe_w_down, 0)
    y12 = _sc_gather_rows(ys, jnp.concatenate([d1, d2]))

    h, vct, vdt, qc, kc, qd, kd = _combine_proj_odd(h, y12, rg, norm_mix[1][None, :], *_odd_weights(w_in_odd[0]),
                                                   _block_indicator(T), T)
    oc = _moba_attention(qc, kc, vct, tb_moba, B, T)
    od = _sb_attention(qd, kd, vdt, B, T)
    h, ys, d1, d2, rg = _ffn(h, oc, od, w_out_odd[0], norm_ffn[1], moe_w_group[1], moe_b_group[1], moe_w_router[1],
                             moe_b_router[1], moe_w_gate, moe_w_up, moe_w_down, 1)
    out = _combine_final(h, ys, d1, d2, rg, norm_final[None, :])
    return out.reshape(B, T, D)
```

```python
import functools
import math

import numpy as np
import jax
import jax.numpy as jnp
from jax import lax
from jax.experimental import pallas as pl
from jax.experimental.pallas import tpu as pltpu
from jax.experimental.pallas import tpu_sc as plsc

F32 = jnp.float32
BF16 = jnp.bfloat16
NEG = -1e30
EPS = 1e-6

D = 1024
HD = 64
MLA_HEADS, MLA_QR, MLA_KVR, MLA_NOPE, MLA_ROPE, MLA_V = 8, 256, 128, 64, 32, 64
ROPE_THETA = 10000.0
SWA_HEADS, SWA_KV, SWA_WIN = 8, 2, 128
MOBA_L, MOBA_TOPK = 256, 3
REL_BUCKETS, REL_MAX = 32, 2048
N_GROUPS, EPG, N_EXP, D_EXP = 4, 8, 32, 256
BIAS_CONST_FROM = 1792
MOBA_TQ = 2 * MOBA_L
BIAS_PAD = MOBA_L
BIAS_ROWS = BIAS_PAD + BIAS_CONST_FROM + MOBA_TQ
SB_DONE = -160.0
TM = 1024
TM_E = 512
VMEM_LIMIT = 56 * 1024 * 1024
SC_CORES, SC_SUBCORES = 2, 16
SC_CHUNK = 128
HALF = D // 2
VT_ROWS = HD + 16
LOG2E = math.log2(math.e)


def _pack_halves(x):
    lo = pltpu.bitcast(x[:, :HALF].astype(BF16).astype(F32), jnp.uint32) >> 16
    hi = pltpu.bitcast(x[:, HALF:].astype(BF16).astype(F32), jnp.uint32) & jnp.uint32(0xFFFF0000)
    return pltpu.bitcast(lo | hi, jnp.int32)


def _unpack_halves(w):
    w = pltpu.bitcast(w, jnp.uint32)
    return pltpu.bitcast(w << 16, F32), pltpu.bitcast(w & jnp.uint32(0xFFFF0000), F32)


def _cparams(sem):
    return pltpu.CompilerParams(dimension_semantics=sem, vmem_limit_bytes=VMEM_LIMIT)


def _dot(a, b):
    return jnp.dot(a, b, preferred_element_type=F32)


def _dot_nt(a, b):
    return lax.dot_general(a, b, (((1,), (1,)), ((), ())), preferred_element_type=F32)


def _rms(x, g):
    return x * lax.rsqrt(jnp.mean(x * x, axis=-1, keepdims=True) + EPS) * g


def _values_t(vt):
    ones = jnp.ones((VT_ROWS - HD, vt.shape[1]), vt.dtype)
    return jnp.concatenate([blk for h in range(8) for blk in (vt[HD * h:HD * (h + 1)], ones)], axis=0)


def _spread_heads(x, lane_of):
    lane = lax.broadcasted_iota(jnp.int32, (x.shape[0], 128), 1)
    out = []
    for h in range(8):
        pair = x[:, 128 * (h // 2):128 * (h // 2 + 1)]
        if HD * (h % 2) != lane_of(h):
            pair = pltpu.roll(pair, HD, 1)
        out.append(jnp.where((lane >= lane_of(h)) & (lane < lane_of(h) + HD), pair, 0.0))
    return out


def _t5_bucket_np(dist):
    n = np.maximum(dist, 0)
    max_exact = REL_BUCKETS // 2
    nf = np.maximum(n, 1).astype(np.float32)
    large = max_exact + (np.log(nf / np.float32(max_exact)) / np.float32(math.log(REL_MAX / max_exact))
                         * np.float32(REL_BUCKETS - max_exact)).astype(np.int32)
    large = np.minimum(large, REL_BUCKETS - 1)
    return np.where(n < max_exact, n, large).astype(np.int32)


def _bias_kernel(table_ref, bucket_ref, out_ref, *, scale, ranges):
    b = bucket_ref[...]
    for m, (lo, hi) in enumerate(ranges):
        @pl.when(pl.program_id(0) == m)
        def _():
            for h in range(8):
                acc = jnp.full(b.shape, NEG if lo == REL_BUCKETS else table_ref[lo, h] * scale, F32)
                for k in range(lo + 1, min(hi, REL_BUCKETS - 1) + 1):
                    acc = jnp.where(b == k, table_ref[k, h] * scale, acc)
                if hi == REL_BUCKETS:
                    acc = jnp.where(b == REL_BUCKETS, NEG, acc)
                out_ref[h] = acc


def _bias_lookup(rel_table, dist, scale, name, masked=None):
    R, C = dist.shape
    buckets = _t5_bucket_np(dist)
    if masked is not None:
        buckets = np.where(masked, REL_BUCKETS, buckets).astype(np.int32)
    ranges = tuple((int(buckets[:, c:c + 256].min()), int(buckets[:, c:c + 256].max())) for c in range(0, C, 256))
    return pl.pallas_call(
        functools.partial(_bias_kernel, scale=scale, ranges=ranges),
        grid=(C // 256,),
        in_specs=[pl.BlockSpec(memory_space=pltpu.SMEM),
                  pl.BlockSpec((R, 256), lambda m: (0, m))],
        out_specs=pl.BlockSpec((8, R, 256), lambda m: (0, 0, m)),
        out_shape=jax.ShapeDtypeStruct((8, R, C), F32),
        compiler_params=_cparams(("arbitrary",)),
        name=name,
    )(rel_table, jnp.asarray(buckets))


def _bias_tables(rel_table):
    d = np.arange(256)[None, :] - np.arange(256)[:, None]
    swa = _bias_lookup(rel_table, d, LOG2E, "bias_swa", masked=(d < 0) | (d >= SWA_WIN))
    moba = _bias_lookup(rel_table, np.arange(BIAS_ROWS)[None, :] - BIAS_PAD - np.arange(MOBA_L)[:, None], LOG2E,
                        "bias_moba")
    return swa, moba


def _proj_even_kernel(h_ref, g_ref, win_ref, gq_ref, gkv_ref, wqa_ref, wqb_ref, wk_ref, wv_ref,
                      cq_ref, sq_ref, ck_ref, sk_ref,
                      q_ref, k_ref, v_ref, qs_ref, ks_ref, vst_ref):
    u = _rms(h_ref[...], g_ref[...]).astype(BF16)
    nq = _rms(_dot(u, win_ref[:, 0:256]), gq_ref[...]).astype(BF16)
    qa = _dot(nq, wqa_ref[...])
    qb = _dot(nq, wqb_ref[...])
    cq, sq = cq_ref[...], sq_ref[...]
    for hh in range(MLA_HEADS):
        sl = slice(128 * hh, 128 * (hh + 1))
        q_ref[:, sl] = (qa[:, sl] * cq + qb[:, sl] * sq).astype(BF16)
    nkv = _rms(_dot(u, win_ref[:, 256:384]), gkv_ref[...]).astype(BF16)
    kn = _dot(nkv, wk_ref[...])
    v_ref[...] = _values_t(_dot_nt(wv_ref[...], nkv)).astype(BF16)
    rest = _dot(u, win_ref[:, 896:1408])
    kr = rest[:, 128:256] * ck_ref[...] + rest[:, 256:384] * sk_ref[...]
    for hh in range(MLA_HEADS):
        sl = slice(128 * hh, 128 * (hh + 1))
        k_ref[:, sl] = (kn[:, sl] + kr).astype(BF16)
    qs = _spread_heads(_dot(u, win_ref[:, 384:896]), lambda hq: HD * (hq // (SWA_HEADS // SWA_KV)))
    for hq in range(SWA_HEADS):
        qs_ref[:, 128 * hq:128 * (hq + 1)] = qs[hq].astype(BF16)
    ks_ref[...] = rest[:, 0:128].astype(BF16)
    vst_ref[...] = rest[:, 384:512].T.astype(BF16)


def _even_weights(w_in, w_qb, w_kvb):
    o_kr = MLA_QR + MLA_KVR
    o_qs = o_kr + MLA_ROPE
    o_ks = o_qs + SWA_HEADS * HD
    o_vs = o_ks + SWA_KV * HD
    half = MLA_ROPE // 2
    col = lambda a, b, s=1.0: (w_in[:, a:b] * s).astype(BF16)
    zero = lambda n: jnp.zeros((D, n), BF16)
    win = jnp.concatenate([
        col(0, o_kr), col(o_qs, o_ks, HD ** -0.5 * LOG2E), col(o_ks, o_vs),
        zero(64), col(o_kr, o_qs), zero(32),
        zero(64), col(o_kr + half, o_qs), col(o_kr, o_kr + half), zero(32),
        col(o_vs, o_vs + SWA_KV * HD)], axis=1)
    pad = lambda a, lo, hi: jnp.pad(a, ((0, 0), (0, 0), (lo, hi))).reshape(a.shape[0], -1).astype(BF16)
    q3 = w_qb.reshape(MLA_QR, MLA_HEADS, MLA_NOPE + MLA_ROPE)
    swapped = jnp.concatenate([q3[:, :, MLA_NOPE + half:], q3[:, :, MLA_NOPE:MLA_NOPE + half]], axis=2)
    kv3 = w_kvb.reshape(MLA_KVR, MLA_HEADS, MLA_NOPE + MLA_V)
    wv = kv3[:, :, MLA_NOPE:].reshape(MLA_KVR, -1).T.astype(BF16)
    return win, pad(q3, 0, 32), pad(swapped, MLA_NOPE, 32), pad(kv3[:, :, :MLA_NOPE], 0, MLA_V), wv


def _rope_tables(T):
    f32 = np.float32
    freqs = f32(ROPE_THETA) ** (-np.arange(0, MLA_ROPE, 2, dtype=f32) / f32(MLA_ROPE))
    ang = np.arange(T, dtype=f32)[:, None] * freqs[None, :]
    cos, sin = np.cos(ang), np.sin(ang)
    z = lambda n: np.zeros((T, n), f32)
    scale = f32((MLA_NOPE + MLA_ROPE) ** -0.5 * LOG2E)
    ck = np.concatenate([z(64), cos, cos, z(32)], axis=1)
    sk = np.concatenate([z(64), -sin, sin, z(32)], axis=1)
    cq = np.concatenate([np.ones((T, 64), f32), cos, cos, z(32)], axis=1) * scale
    sq = sk * scale
    return tuple(jnp.asarray(t, F32) for t in (cq, sq, ck, sk))


def _proj_even(h, g, win, gq, gkv, wqa, wqb, wk, wv, tables, T):
    N = h.shape[0]
    nt = T // TM
    full = lambda a: pl.BlockSpec(a.shape, lambda i: (0,) * a.ndim)
    tab = pl.BlockSpec((TM, 128), lambda i: (i % nt, 0))
    row = lambda w: pl.BlockSpec((TM, w), lambda i: (i, 0))
    outs = (1024, 1024, -8 * VT_ROWS, 1024, 128, -128)
    spec = lambda w: row(w) if w > 0 else pl.BlockSpec((None, -w, TM), lambda i: (i // nt, 0, i % nt))
    shape = lambda w: jax.ShapeDtypeStruct((N, w) if w > 0 else (N // T, -w, T), BF16)
    return pl.pallas_call(
        _proj_even_kernel,
        grid=(N // TM,),
        in_specs=[row(D), full(g), full(win), full(gq), full(gkv), full(wqa), full(wqb), full(wk), full(wv),
                  tab, tab, tab, tab],
        out_specs=[spec(w) for w in outs],
        out_shape=[shape(w) for w in outs],
        compiler_params=_cparams(("arbitrary",)),
        name="proj_even",
    )(h, g, win, gq, gkv, wqa, wqb, wk, wv, *tables)


def _finish_t(accs):
    ot = jnp.concatenate([acc[:HD] / acc[HD:HD + 1] for acc in accs], axis=0)
    return ot.T.astype(BF16)


def _causal_attention_pair_t(ia, nq, q_of, k_ref, vt_ref, o_ref, s_ref, smax_ref, m_ref, acc_ref, *, tq,
                             bias_of=None):
    nh = s_ref.shape[1]
    first_query = (ia * tq, (nq - 1 - ia) * tq)
    n_items = nq + 1
    assert n_items % 2 == 1
    key = lax.broadcasted_iota(jnp.int32, (tq, tq), 0)
    qry = lax.broadcasted_iota(jnp.int32, (tq, tq), 1)

    def item(p):
        if isinstance(p, int) and p < 2:
            return p, first_query[p], first_query[p]
        t = jnp.where(p - 2 >= ia, 1, 0)
        return t, jnp.where(t == 1, first_query[1], first_query[0]), (p - 2 - t * ia) * tq

    def scores_to(slot, p):
        t, q0, start = item(p)
        start = pl.multiple_of(start, tq)
        for j in range(nh):
            st = _dot_nt(k_ref[pl.ds(start, tq), 128 * j:128 * (j + 1)], q_of(t, q0, j))
            if bias_of is not None:
                st = st + bias_of(start, q0, j)
            if isinstance(p, int) and p < 2:
                st = jnp.where(key <= qry, st, NEG)
            s_ref[slot, j] = st
            smax_ref[slot, j] = jnp.max(st, axis=0, keepdims=True)

    def update_from(slot, p):
        t, _, start = item(p)
        start = pl.multiple_of(start, tq)
        m_new = [jnp.maximum(m_ref[t, j], smax_ref[slot, j]) for j in range(nh)]
        pts = [jnp.exp2(s_ref[slot, j] - m_new[j]).astype(BF16) for j in range(nh)]
        for j in range(nh):
            acc_ref[t, j] = (jnp.exp2(m_ref[t, j] - m_new[j]) * acc_ref[t, j]
                             + _dot(vt_ref[VT_ROWS * j:VT_ROWS * (j + 1), pl.ds(start, tq)], pts[j]))
            m_ref[t, j] = m_new[j]

    def pair(k):
        scores_to(1, k + 1)
        update_from(0, k)
        scores_to(0, k + 2)
        update_from(1, k + 1)

    def quad(c, _):
        pair(2 + 4 * c)
        pair(4 + 4 * c)
        return 0

    m_ref[...] = jnp.full(m_ref.shape, NEG, F32)
    acc_ref[...] = jnp.zeros(acc_ref.shape, F32)
    scores_to(0, 0)
    pair(0)
    later_pairs = (n_items - 1) // 2 - 1
    lax.fori_loop(0, later_pairs // 2, quad, 0)
    if later_pairs % 2:
        pair(n_items - 3)
    update_from(0, n_items - 1)
    for t in range(2):
        o_ref[pl.ds(pl.multiple_of(first_query[t], tq), tq), :] = _finish_t([acc_ref[t, j] for j in range(nh)])


def _attention_t_scratch(tq):
    return [pltpu.VMEM((2, 2, tq, tq), F32), pltpu.VMEM((2, 2, 1, tq), F32), pltpu.VMEM((2, 2, 1, tq), F32),
            pltpu.VMEM((2, 2, VT_ROWS, tq), F32)]


def _mla_kernel(q_ref, k_ref, vt_ref, o_ref, s_ref, smax_ref, m_ref, acc_ref, *, tq, nq):
    def q_of(t, q0, j):
        return q_ref[pl.ds(pl.multiple_of(q0, tq), tq), 128 * j:128 * (j + 1)]

    _causal_attention_pair_t(pl.program_id(2), nq, q_of, k_ref, vt_ref, o_ref, s_ref, smax_ref, m_ref, acc_ref,
                             tq=tq)


def _mla_attention(q, k, vt, B, T, tq=512):
    nq = T // tq
    return pl.pallas_call(
        functools.partial(_mla_kernel, tq=tq, nq=nq),
        grid=(B, MLA_HEADS // 2, nq // 2),
        in_specs=[pl.BlockSpec((T, 256), lambda b, p, i: (b, p)),
                  pl.BlockSpec((T, 256), lambda b, p, i: (b, p)),
                  pl.BlockSpec((None, 2 * VT_ROWS, T), lambda b, p, i: (b, p, 0))],
        out_specs=pl.BlockSpec((T, 128), lambda b, p, i: (b, p)),
        out_shape=jax.ShapeDtypeStruct((B * T, MLA_HEADS * MLA_V), BF16),
        scratch_shapes=_attention_t_scratch(tq),
        compiler_params=_cparams(("arbitrary", "arbitrary", "arbitrary")),
        name="mla_attention",
    )(q, k, vt)


def _swa_kernel(sink_ref, q_ref, k_ref, vt_ref, tb_ref, o_ref, *, tq):
    i = pl.program_id(1)
    G = SWA_HEADS // SWA_KV
    W = 2 * SWA_WIN
    head = lax.broadcasted_iota(jnp.int32, (1, G * 128), 1) >> 7
    sinks = []
    for c in range(SWA_KV):
        sink = jnp.zeros((1, G * 128), F32)
        for g in range(G):
            sink = jnp.where(head == g, sink_ref[G * c + g], sink)
        sinks.append(sink)
    scores, values = [], []
    for r in range(tq // 128):
        qstart = i * tq + 128 * r
        kstart = pl.multiple_of(jnp.maximum(qstart - SWA_WIN, 0), 128)
        off = pl.multiple_of(qstart - kstart, 128)
        kw = k_ref[pl.ds(kstart, W), :]
        for c in range(SWA_KV):
            qg = jnp.concatenate([q_ref[128 * r:128 * (r + 1), 128 * (G * c + g):128 * (G * c + g + 1)]
                                  for g in range(G)], axis=0)
            bias = jnp.concatenate([tb_ref[G * c + g, :, pl.ds(off, 128)] for g in range(G)], axis=1)
            scores.append(_dot_nt(kw, qg) + bias)
            values.append(vt_ref[HD * c:HD * (c + 1), pl.ds(kstart, W)])
    probs = []
    for n, s in enumerate(scores):
        sink = sinks[n % SWA_KV]
        m = jnp.maximum(jnp.max(s, axis=0, keepdims=True), sink)
        p = jnp.exp2(s - m)
        probs.append((p.astype(BF16), jnp.sum(p, axis=0, keepdims=True) + jnp.exp2(sink - m)))
    for r in range(tq // 128):
        outs = []
        for c in range(SWA_KV):
            p, denom = probs[SWA_KV * r + c]
            ot = _dot(values[SWA_KV * r + c], p) / denom
            outs += [ot[:, 128 * g:128 * (g + 1)] for g in range(G)]
        o_ref[128 * r:128 * (r + 1), :] = jnp.concatenate(outs, axis=0).T.astype(BF16)


def _swa_attention(q8, k, vt, sinks, tb, B, T, tq=512):
    nq = T // tq
    return pl.pallas_call(
        functools.partial(_swa_kernel, tq=tq),
        grid=(B, nq),
        in_specs=[pl.BlockSpec(memory_space=pltpu.SMEM),
                  pl.BlockSpec((tq, 1024), lambda b, i: (b * nq + i, 0)),
                  pl.BlockSpec((T, 128), lambda b, i: (b, 0)),
                  pl.BlockSpec((None, 128, T), lambda b, i: (b, 0, 0)),
                  pl.BlockSpec((8, 256, 256), lambda b, i: (0, 0, 0))],
        out_specs=pl.BlockSpec((tq, 512), lambda b, i: (b * nq + i, 0)),
        out_shape=jax.ShapeDtypeStruct((B * T, SWA_HEADS * HD), BF16),
        compiler_params=_cparams(("arbitrary", "arbitrary")),
        name="swa_attention",
    )(sinks * LOG2E, q8, k, vt, tb)


def _outproj_router_kernel(h_ref, oa_ref, ob_ref, wo_ref, g_ref, wr_ref, br_ref,
                           hn_ref, u_ref, ri_ref, rg_ref, cnt_ref, base_ref):
    i = pl.program_id(0)

    @pl.when(i == 0)
    def _():
        base_ref[...] = jnp.zeros_like(base_ref)

    hn = h_ref[...] + _dot(oa_ref[...], wo_ref[0:512, :]) + _dot(ob_ref[...], wo_ref[512:1024, :])
    hn_ref[...] = hn
    u = _rms(hn, g_ref[...])
    u_ref[...] = _pack_halves(u)
    u_hi = u.astype(BF16)
    u_lo = (u - u_hi.astype(F32)).astype(BF16)
    both = _dot_nt(wr_ref[...], u_hi)
    logits = both[:128] + both[128:] + _dot_nt(wr_ref[0:128, :], u_lo) + br_ref[...]
    tm = logits.shape[1]
    sub = lax.broadcasted_iota(jnp.int32, (8, tm), 0).astype(F32)
    big = 1e6
    isg = sub < N_GROUPS
    gl = jnp.where(isg, logits[N_EXP:N_EXP + 8], NEG)
    gmax = jnp.max(gl, axis=0, keepdims=True)
    gsel = jnp.min(jnp.where(gl == gmax, sub, big), axis=0, keepdims=True)
    g_gate = 1.0 / jnp.sum(jnp.where(isg, jnp.exp(gl - gmax), 0.0), axis=0, keepdims=True)
    el = jnp.zeros((EPG, tm), F32)
    for grp in range(N_GROUPS):
        el = jnp.where(gsel == grp, logits[EPG * grp:EPG * (grp + 1)], el)
    m1 = jnp.max(el, axis=0, keepdims=True)
    i1 = jnp.min(jnp.where(el == m1, sub, big), axis=0, keepdims=True)
    el2 = jnp.where(sub == i1, NEG, el)
    m2 = jnp.max(el2, axis=0, keepdims=True)
    i2 = jnp.min(jnp.where(el2 == m2, sub, big), axis=0, keepdims=True)
    e1 = gsel * EPG + i1
    e2 = gsel * EPG + i2
    r = jnp.exp(m2 - m1)
    gate1 = g_gate / (1.0 + r)
    gate2 = g_gate * r / (1.0 + r)
    rows = lax.broadcasted_iota(jnp.int32, (128, tm), 0).astype(F32)
    oh1 = (rows == e1).astype(F32)
    oh2 = (rows == e2).astype(F32)
    oh = oh1 + oh2
    t_row = lax.broadcasted_iota(jnp.int32, (tm, tm), 0)
    t_col = lax.broadcasted_iota(jnp.int32, (tm, tm), 1)
    earlier = jnp.where(t_row < t_col, 1.0, 0.0).astype(BF16)
    prefix = _dot(oh.astype(BF16), earlier) + base_ref[:, 0:1]
    rank1 = jnp.sum(oh1 * prefix, axis=0, keepdims=True)
    rank2 = jnp.sum(oh2 * prefix, axis=0, keepdims=True)
    base_ref[...] = base_ref[...] + jnp.sum(oh, axis=1, keepdims=True)
    cnt_ref[...] = base_ref[...]
    ri_ref[...] = jnp.where(sub == 0, e1, jnp.where(sub == 1, e2, jnp.where(sub == 2, rank1,
                            jnp.where(sub == 3, rank2, 0.0))))
    rg_ref[...] = jnp.where(rows == 0, gate1, jnp.where(rows == 1, gate2, 0.0)).T


def _outproj_router(h, oa, ob, wo, g, wr, br):
    N = h.shape[0]
    full = lambda a: pl.BlockSpec(a.shape, lambda i: (0,) * a.ndim)
    row = lambda w: pl.BlockSpec((TM, w), lambda i: (i, 0))
    return pl.pallas_call(
        _outproj_router_kernel,
        grid=(N // TM,),
        in_specs=[row(D), row(512), row(512), full(wo), full(g), full(wr), full(br)],
        out_specs=[row(D), row(HALF), pl.BlockSpec((8, TM), lambda i: (0, i)), row(128),
                   pl.BlockSpec((128, 128), lambda i: (0, 0))],
        out_shape=[jax.ShapeDtypeStruct((N, D), F32), jax.ShapeDtypeStruct((N, HALF), jnp.int32),
                   jax.ShapeDtypeStruct((8, N), F32), jax.ShapeDtypeStruct((N, 128), F32),
                   jax.ShapeDtypeStruct((128, 128), F32)],
        scratch_shapes=[pltpu.VMEM((128, 128), F32)],
        compiler_params=_cparams(("arbitrary",)),
        name="outproj_router",
    )(h, oa, ob, wo, g, wr, br)


def _moe_kernel(te_ref, tv_ref, run_ref, nx_ref, x_ref, wg_hbm, wu_hbm, wd_hbm, o_ref,
                wg_buf, wu_buf, wd_buf, sems, wgu_s, wd_s, *, layer):
    i = pl.program_id(0)
    valid = tv_ref[i]

    def weight_copies(e, slot):
        return (pltpu.make_async_copy(wg_hbm.at[layer, e], wg_buf.at[slot], sems.at[slot, 0]),
                pltpu.make_async_copy(wu_hbm.at[layer, e], wu_buf.at[slot], sems.at[slot, 1]),
                pltpu.make_async_copy(wd_hbm.at[layer, e], wd_buf.at[slot], sems.at[slot, 2]))

    @pl.when(run_ref[i] > 0)
    def _():
        slot = run_ref[i] - 1

        @pl.when(i == 0)
        def _():
            for c in weight_copies(te_ref[i], slot):
                c.start()

        for c in weight_copies(te_ref[i], slot):
            c.wait()
        wgu_s[:, :D_EXP] = wg_buf[slot].astype(BF16)
        wgu_s[:, D_EXP:] = wu_buf[slot].astype(BF16)
        wd_s[...] = wd_buf[slot].astype(BF16)

        @pl.when(nx_ref[i] >= 0)
        def _():
            for c in weight_copies(nx_ref[i], 1 - slot):
                c.start()

    @pl.when(valid > 0)
    def _():
        rows = lax.broadcasted_iota(jnp.int32, x_ref.shape, 0)
        lo, hi = _unpack_halves(jnp.where(rows < valid, x_ref[...], 0))
        hgu = _dot(lo.astype(BF16), wgu_s[:HALF, :]) + _dot(hi.astype(BF16), wgu_s[HALF:, :])
        hg, hu = hgu[:, :D_EXP], hgu[:, D_EXP:]
        a = hg * (1.0 / (1.0 + jnp.exp(-hg))) * hu
        o_ref[...] = _pack_halves(_dot(a.astype(BF16), wd_s[...]))

    @pl.when(valid <= 0)
    def _():
        o_ref[...] = jnp.zeros_like(o_ref)


def _moe_experts(xs, plan, w_gate, w_up, w_down, layer):
    P = xs.shape[0]
    hbm = pl.BlockSpec(memory_space=pl.ANY)
    grid_spec = pltpu.PrefetchScalarGridSpec(
        num_scalar_prefetch=4,
        grid=(P // TM_E,),
        in_specs=[pl.BlockSpec((TM_E, HALF), lambda i, *_: (i, 0)), hbm, hbm, hbm],
        out_specs=pl.BlockSpec((TM_E, HALF), lambda i, *_: (i, 0)),
        scratch_shapes=[pltpu.VMEM((2, D, D_EXP), F32), pltpu.VMEM((2, D, D_EXP), F32),
                        pltpu.VMEM((2, D_EXP, D), F32), pltpu.SemaphoreType.DMA((2, 3)),
                        pltpu.VMEM((D, 2 * D_EXP), BF16), pltpu.VMEM((D_EXP, D), BF16)],
    )
    return pl.pallas_call(
        functools.partial(_moe_kernel, layer=layer),
        grid_spec=grid_spec,
        out_shape=jax.ShapeDtypeStruct((P, HALF), jnp.int32),
        compiler_params=_cparams(("arbitrary",)),
        name="moe_experts",
    )(*plan, xs, w_gate, w_up, w_down)


def _sc_mesh():
    return plsc.VectorSubcoreMesh(core_axis_name="c", subcore_axis_name="s",
                                  num_cores=SC_CORES, num_subcores=SC_SUBCORES)


def _sc_scatter_rows(src, d1, d2, P):
    N, W = src.shape
    per_w = N // (SC_CORES * SC_SUBCORES)

    @functools.partial(pl.kernel, mesh=_sc_mesh(), out_type=jax.ShapeDtypeStruct((P, W), src.dtype),
                       scratch_types=[pltpu.VMEM((SC_CHUNK,), jnp.int32), pltpu.VMEM((SC_CHUNK, W), src.dtype)],
                       name="sc_scatter_rows")
    def k(src_hbm, d1_hbm, d2_hbm, out_hbm, idx_v, rows_v):
        wid = lax.axis_index("s") * SC_CORES + lax.axis_index("c")

        @pl.loop(0, per_w // SC_CHUNK)
        def _(c):
            off = wid * per_w + c * SC_CHUNK
            pltpu.sync_copy(src_hbm.at[pl.ds(off, SC_CHUNK)], rows_v)
            pltpu.sync_copy(d1_hbm.at[pl.ds(off, SC_CHUNK)], idx_v)
            pltpu.sync_copy(rows_v, out_hbm.at[idx_v])
            pltpu.sync_copy(d2_hbm.at[pl.ds(off, SC_CHUNK)], idx_v)
            pltpu.sync_copy(rows_v, out_hbm.at[idx_v])

    return k(src, d1, d2)


def _sc_gather_rows(table, idx):
    B, W = idx.shape[0], table.shape[1]
    per_w = B // (SC_CORES * SC_SUBCORES)

    @functools.partial(pl.kernel, mesh=_sc_mesh(), out_type=jax.ShapeDtypeStruct((B, W), table.dtype),
                       scratch_types=[pltpu.VMEM((SC_CHUNK,), jnp.int32), pltpu.VMEM((SC_CHUNK, W), table.dtype)],
                       name="sc_gather_rows")
    def k(table_hbm, idx_hbm, out_hbm, idx_v, rows_v):
        wid = lax.axis_index("s") * SC_CORES + lax.axis_index("c")

        @pl.loop(0, per_w // SC_CHUNK)
        def _(c):
            off = wid * per_w + c * SC_CHUNK
            pltpu.sync_copy(idx_hbm.at[pl.ds(off, SC_CHUNK)], idx_v)
            pltpu.sync_copy(table_hbm.at[idx_v], rows_v)
            pltpu.sync_copy(rows_v, out_hbm.at[pl.ds(off, SC_CHUNK)])

    return k(table, idx)


def _dest_kernel(offs_ref, ri_ref, d_ref):
    ri = ri_ref[...]
    base = jnp.zeros(ri.shape, F32)
    for e in range(N_EXP):
        base = jnp.where(ri == e, offs_ref[e].astype(F32), base)
    d_ref[...] = (base + pltpu.roll(ri, 6, 0)).astype(jnp.int32)


def _dispatch_plan(ri, cnt, N):
    counts = cnt[:N_EXP, 0].astype(jnp.int32)
    padded = ((counts + TM_E - 1) // TM_E) * TM_E
    ends = jnp.cumsum(padded)
    offs = ends - padded
    take = lambda t, i: t.at[i].get(mode="promise_in_bounds")
    dest = pl.pallas_call(
        _dest_kernel,
        in_specs=[pl.BlockSpec(memory_space=pltpu.SMEM), pl.BlockSpec((8, N), lambda: (0, 0))],
        out_specs=pl.BlockSpec((8, N), lambda: (0, 0)),
        out_shape=jax.ShapeDtypeStruct((8, N), jnp.int32),
        name="dispatch_rows",
    )(offs, ri)
    d1, d2 = dest[0], dest[1]
    P = 2 * N + N_EXP * TM_E
    tile_start = jnp.arange(P // TM_E, dtype=jnp.int32) * TM_E
    tile_expert = jnp.minimum(jnp.sum((ends[None, :] <= tile_start[:, None]).astype(jnp.int32), axis=1), N_EXP - 1)
    tile_valid = jnp.clip(take(offs + counts, tile_expert) - tile_start, 0, TM_E)
    n_tiles = P // TM_E
    tile = jnp.arange(n_tiles, dtype=jnp.int32)
    starts = (tile_valid > 0) & ((tile == 0) | (tile_expert != jnp.roll(tile_expert, 1)))
    slot = (jnp.cumsum(starts.astype(jnp.int32)) - 1) % 2
    run_code = jnp.where(starts, slot + 1, 0).astype(jnp.int32)
    start_at = jnp.where(starts, tile, n_tiles)
    next_start = jnp.concatenate([lax.cummin(start_at, reverse=True)[1:], jnp.full((1,), n_tiles, jnp.int32)])
    next_expert = jnp.where(next_start < n_tiles, take(tile_expert, jnp.minimum(next_start, n_tiles - 1)), -1)
    return d1, d2, P, (tile_expert, tile_valid, run_code, next_expert.astype(jnp.int32))


def _combine(h_ref, y1_ref, y2_ref, rg_ref):
    rg = rg_ref[...]
    lo1, hi1 = _unpack_halves(y1_ref[...])
    lo2, hi2 = _unpack_halves(y2_ref[...])
    g1, g2 = rg[:, 0:1], rg[:, 1:2]
    return h_ref[...] + jnp.concatenate([g1 * lo1 + g2 * lo2, g1 * hi1 + g2 * hi2], axis=1)


ODD_WIDTHS = (1024, 1024, 512, 512)


def _combine_proj_odd_kernel(h_ref, y1_ref, y2_ref, rg_ref, g_ref, win_ref, wvt_ref, ind_ref, hn_ref, vt_ref,
                             vdt_ref, *out_refs):
    hn = _combine(h_ref, y1_ref, y2_ref, rg_ref)
    hn_ref[...] = hn
    u = _rms(hn, g_ref[...]).astype(BF16)
    vt_ref[...] = _values_t(_dot_nt(wvt_ref[0:512, :], u)).astype(BF16)
    vdt_ref[...] = _dot_nt(wvt_ref[512:1024, :], u).astype(BF16)
    qc_ref, kc_ref, qd_ref, kd_ref = out_refs
    qc = _spread_heads(_dot(u, win_ref[:, 0:512]), lambda h: 0)
    kc = _spread_heads(_dot(u, win_ref[:, 512:1024]), lambda h: 0)
    for hh in range(8):
        qc_ref[:, 128 * hh:128 * (hh + 1)] = qc[hh].astype(BF16)
        kc_ref[:, 128 * hh:128 * (hh + 1)] = (kc[hh] + ind_ref[...]).astype(BF16)
    qd_ref[...] = _dot(u, win_ref[:, 1024:1536]).astype(BF16)
    kd_ref[...] = _dot(u, win_ref[:, 1536:2048]).astype(BF16)


def _odd_weights(w_in):
    s = HD ** -0.5 * LOG2E
    col = lambda a, b, s=1.0: (w_in[:, a:b] * s).astype(BF16)
    win = jnp.concatenate([col(0, 512, s), col(512, 1024), col(1536, 2048, s), col(2048, 2560)], axis=1)
    wvt = jnp.concatenate([col(1024, 1536), col(2560, 3072)], axis=1).T
    return win, wvt


def _block_indicator(T):
    assert T // MOBA_L <= 64
    ind = np.zeros((T, 128), np.float32)
    ind[np.arange(T), 64 + np.arange(T) // MOBA_L] = 1.0
    return jnp.asarray(ind)


def _y_specs(N):
    return [pl.BlockSpec((TM, HALF), lambda i: (i, 0)), pl.BlockSpec((TM, HALF), lambda i: (i + N // TM, 0))]


def _combine_proj_odd(h, y12, rg, g, win, wvt, ind, T):
    N = h.shape[0]
    nt = T // TM
    full = lambda a: pl.BlockSpec(a.shape, lambda i: (0,) * a.ndim)
    row = lambda w: pl.BlockSpec((TM, w), lambda i: (i, 0))
    return pl.pallas_call(
        _combine_proj_odd_kernel,
        grid=(N // TM,),
        in_specs=[row(D)] + _y_specs(N) + [row(128), full(g), full(win), full(wvt),
                  pl.BlockSpec((TM, 128), lambda i: (i % nt, 0))],
        out_specs=[row(D), pl.BlockSpec((None, 8 * VT_ROWS, TM), lambda i: (i // nt, 0, i % nt)),
                   pl.BlockSpec((None, 512, TM), lambda i: (i // nt, 0, i % nt))]
        + [row(w) for w in ODD_WIDTHS],
        out_shape=[jax.ShapeDtypeStruct((N, D), F32), jax.ShapeDtypeStruct((N // T, 8 * VT_ROWS, T), BF16),
                   jax.ShapeDtypeStruct((N // T, 512, T), BF16)]
        + [jax.ShapeDtypeStruct((N, w), BF16) for w in ODD_WIDTHS],
        compiler_params=_cparams(("arbitrary",)),
        name="combine_proj_odd",
    )(h, y12, y12, rg, g, win, wvt, ind)


def _combine_final_kernel(h_ref, y1_ref, y2_ref, rg_ref, g_ref, o_ref):
    o_ref[...] = _rms(_combine(h_ref, y1_ref, y2_ref, rg_ref), g_ref[...])


def _combine_final(h, y12, rg, g):
    N = h.shape[0]
    row = lambda w: pl.BlockSpec((TM, w), lambda i: (i, 0))
    return pl.pallas_call(
        _combine_final_kernel,
        grid=(N // TM,),
        in_specs=[row(D)] + _y_specs(N) + [row(128), pl.BlockSpec((1, D), lambda i: (0, 0))],
        out_specs=row(D),
        out_shape=jax.ShapeDtypeStruct((N, D), F32),
        compiler_params=_cparams(("arbitrary",)),
        name="combine_final",
    )(h, y12, y12, rg, g)


def _moba_kernel(q_ref, k_ref, vt_ref, tb_ref, o_ref, kmean_ref, ksplit_ref, qa_ref, s_ref, smax_ref, m_ref, acc_ref,
                 *, nkb, nq):
    ia = pl.program_id(2)
    L, tq = MOBA_L, MOBA_TQ

    @pl.when(ia == 0)
    def _():
        kmean_ref[...] = jnp.zeros_like(kmean_ref)
        for n in range(nkb):
            kmean_ref[64 + n:65 + n, :] = jnp.sum(k_ref[n * L:(n + 1) * L, :].astype(F32), axis=0,
                                                  keepdims=True) * (1.0 / L)
        hi = kmean_ref[...].astype(BF16)
        ksplit_ref[0] = hi
        ksplit_ref[1] = (kmean_ref[...] - hi.astype(F32)).astype(BF16)

    nb = -(-nkb // 8) * 8
    lane = lax.broadcasted_iota(jnp.int32, (tq, 128), 1)
    blk = lax.broadcasted_iota(jnp.int32, (nb, tq), 0)
    half = jnp.where(lax.broadcasted_iota(jnp.int32, (nb, tq), 1) >= L, 1, 0)
    blk_f = blk.astype(F32)
    for t, tile in enumerate((ia, nq - 1 - ia)):
        own = 2 * tile + half
        for j in range(2):
            qj = q_ref[pl.ds(pl.multiple_of(tile * tq, tq), tq), 128 * j:128 * (j + 1)]
            gsc = (_dot_nt(ksplit_ref[0, 64:64 + nb, 128 * j:128 * (j + 1)], qj)
                   + _dot_nt(ksplit_ref[1, 64:64 + nb, 128 * j:128 * (j + 1)], qj))
            g = jnp.where(blk < own, gsc, NEG)
            allowed = jnp.where(blk == own, 1.0, 0.0)
            for _ in range(MOBA_TOPK):
                mx = jnp.max(g, axis=0, keepdims=True)
                idx = jnp.min(jnp.where(g == mx, blk_f, 1e6), axis=0, keepdims=True)
                pick = blk_f == idx
                allowed = jnp.maximum(allowed, jnp.where(pick, jnp.where(mx > 0.5 * NEG, 1.0, 0.0), 0.0))
                g = jnp.where(pick, 2.0 * NEG, g)
            mask = jnp.concatenate([jnp.zeros((64, tq), F32), jnp.where(allowed > 0.5, 0.0, NEG),
                                    jnp.zeros((64 - nb, tq), F32)], axis=0)
            qa_ref[t, j] = jnp.where(lane < HD, qj, mask.T.astype(BF16))

    def bias_of(start, q0, j):
        parts = []
        for c in range(tq // L):
            off = jnp.minimum(q0 - start - c * L, BIAS_CONST_FROM) + BIAS_PAD
            parts.append(tb_ref[j, :, pl.ds(pl.multiple_of(off, L), tq)])
        return jnp.concatenate(parts, axis=0)

    _causal_attention_pair_t(ia, nq, lambda t, q0, j: qa_ref[t, j], k_ref, vt_ref, o_ref, s_ref, smax_ref, m_ref,
                             acc_ref, tq=tq, bias_of=bias_of)


def _moba_attention(q, k, vt, tb, B, T):
    tq = MOBA_TQ
    nq = T // tq
    return pl.pallas_call(
        functools.partial(_moba_kernel, nkb=T // MOBA_L, nq=nq),
        grid=(B, 4, nq // 2),
        in_specs=[pl.BlockSpec((T, 256), lambda b, p, i: (b, p)),
                  pl.BlockSpec((T, 256), lambda b, p, i: (b, p)),
                  pl.BlockSpec((None, 2 * VT_ROWS, T), lambda b, p, i: (b, p, 0)),
                  pl.BlockSpec((2, MOBA_L, BIAS_ROWS), lambda b, p, i: (p, 0, 0))],
        out_specs=pl.BlockSpec((T, 128), lambda b, p, i: (b, p)),
        out_shape=jax.ShapeDtypeStruct((B * T, 512), BF16),
        scratch_shapes=[pltpu.VMEM((128, 256), F32), pltpu.VMEM((2, 128, 256), BF16),
                        pltpu.VMEM((2, 2, tq, 128), BF16)]
        + _attention_t_scratch(tq),
        compiler_params=_cparams(("arbitrary", "arbitrary", "arbitrary")),
        name="moba_attention",
    )(q, k, vt, tb)


def _sb_kernel(q_ref, k_ref, vt_ref, o_ref, *, tq, pairs):
    i = pl.program_id(2)
    lane = lax.broadcasted_iota(jnp.int32, (tq, 128), 1)
    qs = []
    for p in range(pairs):
        q_pair = q_ref[:, 128 * p:128 * (p + 1)]
        qs += [jnp.where(lane < HD, q_pair, 0).astype(BF16), jnp.where(lane >= HD, q_pair, 0).astype(BF16)]
    key = lax.broadcasted_iota(jnp.int32, (tq, tq), 0)
    qry = lax.broadcasted_iota(jnp.int32, (tq, tq), 1)
    past = key < qry
    suffix = jnp.where(qry >= key, -1.0, 0.0).astype(BF16)
    heads = range(2 * pairs)

    def step(kb, carry, boundary):
        start = pl.multiple_of(kb * tq, tq)
        zs = [_dot_nt(k_ref[pl.ds(start, tq), 128 * (h // 2):128 * (h // 2 + 1)], qs[h]) for h in heads]
        parts = []
        for z in zs:
            neg_abs = pltpu.bitcast(pltpu.bitcast(z, jnp.uint32) | jnp.uint32(0x80000000), F32)
            soft = jnp.maximum(z, 0.0) + jnp.log2(1.0 + jnp.exp2(neg_abs))
            if boundary:
                soft = jnp.where(past, soft, 0.0)
            parts.append(soft.astype(BF16))
        sums = [_dot(suffix, part) for part in parts]
        ws = []
        for h in heads:
            w = jnp.exp2(zs[h] + (carry[h][0] + sums[h]))
            ws.append((jnp.where(past, w, 0.0) if boundary else w).astype(BF16))
        return tuple((carry[h][0] + sums[h][0:1], carry[h][1]
                      + _dot(vt_ref[128 * (h // 2):128 * (h // 2 + 1), pl.ds(start, tq)], ws[h])) for h in heads)

    def live(carry):
        top = carry[0][0]
        for c, _ in carry[1:]:
            top = jnp.maximum(top, c)
        return (jnp.max(top) > SB_DONE).astype(jnp.int32)

    def body(state):
        t, _, carry = state
        carry = step(i - 1 - t, carry, False)
        return t + 1, live(carry), carry

    init = tuple((jnp.zeros((1, tq), F32), jnp.zeros((128, tq), F32)) for _ in range(2 * pairs))
    carry = step(i, init, True)
    _, _, carry = lax.while_loop(lambda st: (st[0] < i) & (st[1] > 0), body, (jnp.int32(0), live(carry), carry))
    rows = lax.broadcasted_iota(jnp.int32, (128, tq), 0)
    for p in range(pairs):
        ot = jnp.where(rows < HD, carry[2 * p][1], carry[2 * p + 1][1])
        o_ref[:, 128 * p:128 * (p + 1)] = ot.T.astype(BF16)


def _sb_attention(q, k, vt, B, T, tq=256, pairs=4):
    nq = T // tq
    w = 128 * pairs
    return pl.pallas_call(
        functools.partial(_sb_kernel, tq=tq, pairs=pairs),
        grid=(B, 4 // pairs, nq),
        in_specs=[pl.BlockSpec((tq, w), lambda b, p, i: (b * nq + i, p)),
                  pl.BlockSpec((T, w), lambda b, p, i: (b, p)),
                  pl.BlockSpec((None, w, T), lambda b, p, i: (b, p, 0))],
        out_specs=pl.BlockSpec((tq, w), lambda b, p, i: (b * nq + i, p)),
        out_shape=jax.ShapeDtypeStruct((B * T, 512), BF16),
        compiler_params=_cparams(("arbitrary", "arbitrary", "arbitrary")),
        name="sb_attention",
    )(q, k, vt)


def _router_weights(w_group, b_group, w_router, b_router):
    pad = 128 - N_EXP - N_GROUPS
    wr = jnp.concatenate([w_router, w_group, jnp.zeros((D, pad), F32)], axis=1).T
    hi = wr.astype(BF16)
    lo = (wr - hi.astype(F32)).astype(BF16)
    br = jnp.concatenate([b_router, b_group, jnp.zeros((pad,), F32)])[:, None]
    return jnp.concatenate([hi, lo], axis=0), br


def _ffn(h, oa, ob, wo, g, w_group, b_group, w_router, b_router, w_gate, w_up, w_down, layer):
    wr, br = _router_weights(w_group, b_group, w_router, b_router)
    hn, u, ri, rg, cnt = _outproj_router(h, oa, ob, wo.astype(BF16), g[None, :], wr, br)
    d1, d2, P, plan = _dispatch_plan(ri, cnt, h.shape[0])
    xs = _sc_scatter_rows(u, d1, d2, P)
    ys = _moe_experts(xs, plan, w_gate, w_up, w_down, layer)
    return hn, _sc_gather_rows(ys, jnp.concatenate([d1, d2])), rg


def kernel(x, norm_mix, norm_ffn, norm_final, rel_bias_table, w_in_even, g_mla_q, g_mla_kv, w_mla_qb, w_mla_kvb, swa_sinks, w_out_even, w_in_odd, w_out_odd, moe_w_group, moe_b_group, moe_w_router, moe_b_router, moe_w_gate, moe_w_up, moe_w_down):
    B, T, _ = x.shape
    h = x.reshape(B * T, D)
    tb_swa, tb_moba = _bias_tables(rel_bias_table)

    win, wqa, wqb, wk, wv = _even_weights(w_in_even[0], w_mla_qb[0], w_mla_kvb[0])
    q, k, v, qs8, ks, vst = _proj_even(h, norm_mix[0][None, :], win, g_mla_q[0][None, :], g_mla_kv[0][None, :],
                                       wqa, wqb, wk, wv, _rope_tables(T), T)
    oa = _mla_attention(q, k, v, B, T)
    ob = _swa_attention(qs8, ks, vst, swa_sinks[0], tb_swa, B, T)
    h, y12, rg = _ffn(h, oa, ob, w_out_even[0], norm_ffn[0], moe_w_group[0], moe_b_group[0], moe_w_router[0],
                      moe_b_router[0], moe_w_gate, moe_w_up, moe_w_down, 0)

    h, vct, vdt, qc, kc, qd, kd = _combine_proj_odd(h, y12, rg, norm_mix[1][None, :], *_odd_weights(w_in_odd[0]),
                                                   _block_indicator(T), T)
    oc = _moba_attention(qc, kc, vct, tb_moba, B, T)
    od = _sb_attention(qd, kd, vdt, B, T)
    h, y12, rg = _ffn(h, oc, od, w_out_odd[0], norm_ffn[1], moe_w_group[1], moe_b_group[1], moe_w_router[1],
                      moe_b_router[1], moe_w_gate, moe_w_up, moe_w_down, 1)
    out = _combine_final(h, y12, rg, norm_final[None, :])
    return out.reshape(B, T, D)
```

```python
import functools
import math

import numpy as np
import jax
import jax.numpy as jnp
from jax import lax
from jax.experimental import pallas as pl
from jax.experimental.pallas import tpu as pltpu
from jax.experimental.pallas import tpu_sc as plsc

F32 = jnp.float32
BF16 = jnp.bfloat16
NEG = -1e30
EPS = 1e-6

D = 1024
HD = 64
MLA_HEADS, MLA_QR, MLA_KVR, MLA_NOPE, MLA_ROPE, MLA_V = 8, 256, 128, 64, 32, 64
ROPE_THETA = 10000.0
SWA_HEADS, SWA_KV, SWA_WIN = 8, 2, 128
MOBA_L, MOBA_TOPK = 256, 3
REL_BUCKETS, REL_MAX = 32, 2048
N_GROUPS, EPG, N_EXP, D_EXP = 4, 8, 32, 256
BIAS_CONST_FROM = 1792
MOBA_TQ = 2 * MOBA_L
BIAS_PAD = MOBA_L
BIAS_ROWS = BIAS_PAD + BIAS_CONST_FROM + MOBA_TQ
SB_DONE = -160.0
TM = 1024
TM_E = 512
VMEM_LIMIT = 56 * 1024 * 1024
SC_CORES, SC_SUBCORES = 2, 16
SC_CHUNK = 128
HALF = D // 2
VT_ROWS = HD + 16
LOG2E = math.log2(math.e)


def _pack_halves(x):
    lo = pltpu.bitcast(x[:, :HALF].astype(BF16).astype(F32), jnp.uint32) >> 16
    hi = pltpu.bitcast(x[:, HALF:].astype(BF16).astype(F32), jnp.uint32) & jnp.uint32(0xFFFF0000)
    return pltpu.bitcast(lo | hi, jnp.int32)


def _unpack_halves(w):
    w = pltpu.bitcast(w, jnp.uint32)
    return pltpu.bitcast(w << 16, F32), pltpu.bitcast(w & jnp.uint32(0xFFFF0000), F32)


def _cparams(sem):
    return pltpu.CompilerParams(dimension_semantics=sem, vmem_limit_bytes=VMEM_LIMIT)


def _dot(a, b):
    return jnp.dot(a, b, preferred_element_type=F32)


def _dot_nt(a, b):
    return lax.dot_general(a, b, (((1,), (1,)), ((), ())), preferred_element_type=F32)


def _rms(x, g):
    return x * lax.rsqrt(jnp.mean(x * x, axis=-1, keepdims=True) + EPS) * g


def _values_t(vt):
    ones = jnp.ones((VT_ROWS - HD, vt.shape[1]), vt.dtype)
    return jnp.concatenate([blk for h in range(8) for blk in (vt[HD * h:HD * (h + 1)], ones)], axis=0)


def _spread_heads(x, lane_of):
    lane = lax.broadcasted_iota(jnp.int32, (x.shape[0], 128), 1)
    out = []
    for h in range(8):
        pair = x[:, 128 * (h // 2):128 * (h // 2 + 1)]
        if HD * (h % 2) != lane_of(h):
            pair = pltpu.roll(pair, HD, 1)
        out.append(jnp.where((lane >= lane_of(h)) & (lane < lane_of(h) + HD), pair, 0.0))
    return out


def _t5_bucket_np(dist):
    n = np.maximum(dist, 0)
    max_exact = REL_BUCKETS // 2
    nf = np.maximum(n, 1).astype(np.float32)
    large = max_exact + (np.log(nf / np.float32(max_exact)) / np.float32(math.log(REL_MAX / max_exact))
                         * np.float32(REL_BUCKETS - max_exact)).astype(np.int32)
    large = np.minimum(large, REL_BUCKETS - 1)
    return np.where(n < max_exact, n, large).astype(np.int32)


def _bias_kernel(table_ref, bucket_ref, out_ref, *, scale, ranges):
    b = bucket_ref[...]
    for m, (lo, hi) in enumerate(ranges):
        @pl.when(pl.program_id(0) == m)
        def _():
            for h in range(8):
                acc = jnp.full(b.shape, NEG if lo == REL_BUCKETS else table_ref[lo, h] * scale, F32)
                for k in range(lo + 1, min(hi, REL_BUCKETS - 1) + 1):
                    acc = jnp.where(b == k, table_ref[k, h] * scale, acc)
                if hi == REL_BUCKETS:
                    acc = jnp.where(b == REL_BUCKETS, NEG, acc)
                out_ref[h] = acc


def _bias_lookup(rel_table, dist, scale, name, masked=None):
    R, C = dist.shape
    buckets = _t5_bucket_np(dist)
    if masked is not None:
        buckets = np.where(masked, REL_BUCKETS, buckets).astype(np.int32)
    ranges = tuple((int(buckets[:, c:c + 256].min()), int(buckets[:, c:c + 256].max())) for c in range(0, C, 256))
    return pl.pallas_call(
        functools.partial(_bias_kernel, scale=scale, ranges=ranges),
        grid=(C // 256,),
        in_specs=[pl.BlockSpec(memory_space=pltpu.SMEM),
                  pl.BlockSpec((R, 256), lambda m: (0, m))],
        out_specs=pl.BlockSpec((8, R, 256), lambda m: (0, 0, m)),
        out_shape=jax.ShapeDtypeStruct((8, R, C), F32),
        compiler_params=_cparams(("arbitrary",)),
        name=name,
    )(rel_table, jnp.asarray(buckets))


def _bias_tables(rel_table):
    d = np.arange(256)[None, :] - np.arange(256)[:, None]
    swa = _bias_lookup(rel_table, d, LOG2E, "bias_swa", masked=(d < 0) | (d >= SWA_WIN))
    moba = _bias_lookup(rel_table, np.arange(BIAS_ROWS)[None, :] - BIAS_PAD - np.arange(MOBA_L)[:, None], LOG2E,
                        "bias_moba")
    return swa, moba


def _proj_even_kernel(h_ref, g_ref, win_ref, gq_ref, gkv_ref, wqa_ref, wqb_ref, wk_ref, wv_ref,
                      cq_ref, sq_ref, ck_ref, sk_ref,
                      q_ref, k_ref, v_ref, qs_ref, ks_ref, vst_ref):
    u = _rms(h_ref[...], g_ref[...]).astype(BF16)
    nq = _rms(_dot(u, win_ref[:, 0:256]), gq_ref[...]).astype(BF16)
    qa = _dot(nq, wqa_ref[...])
    qb = _dot(nq, wqb_ref[...])
    cq, sq = cq_ref[...], sq_ref[...]
    for hh in range(MLA_HEADS):
        sl = slice(128 * hh, 128 * (hh + 1))
        q_ref[:, sl] = (qa[:, sl] * cq + qb[:, sl] * sq).astype(BF16)
    nkv = _rms(_dot(u, win_ref[:, 256:384]), gkv_ref[...]).astype(BF16)
    kn = _dot(nkv, wk_ref[...])
    v_ref[...] = _values_t(_dot_nt(wv_ref[...], nkv)).astype(BF16)
    rest = _dot(u, win_ref[:, 896:1408])
    kr = rest[:, 128:256] * ck_ref[...] + rest[:, 256:384] * sk_ref[...]
    for hh in range(MLA_HEADS):
        sl = slice(128 * hh, 128 * (hh + 1))
        k_ref[:, sl] = (kn[:, sl] + kr).astype(BF16)
    qs = _spread_heads(_dot(u, win_ref[:, 384:896]), lambda hq: HD * (hq // (SWA_HEADS // SWA_KV)))
    for hq in range(SWA_HEADS):
        qs_ref[:, 128 * hq:128 * (hq + 1)] = qs[hq].astype(BF16)
    ks_ref[...] = rest[:, 0:128].astype(BF16)
    vst_ref[...] = rest[:, 384:512].T.astype(BF16)


def _even_weights(w_in, w_qb, w_kvb):
    o_kr = MLA_QR + MLA_KVR
    o_qs = o_kr + MLA_ROPE
    o_ks = o_qs + SWA_HEADS * HD
    o_vs = o_ks + SWA_KV * HD
    half = MLA_ROPE // 2
    col = lambda a, b, s=1.0: (w_in[:, a:b] * s).astype(BF16)
    zero = lambda n: jnp.zeros((D, n), BF16)
    win = jnp.concatenate([
        col(0, o_kr), col(o_qs, o_ks, HD ** -0.5 * LOG2E), col(o_ks, o_vs),
        zero(64), col(o_kr, o_qs), zero(32),
        zero(64), col(o_kr + half, o_qs), col(o_kr, o_kr + half), zero(32),
        col(o_vs, o_vs + SWA_KV * HD)], axis=1)
    pad = lambda a, lo, hi: jnp.pad(a, ((0, 0), (0, 0), (lo, hi))).reshape(a.shape[0], -1).astype(BF16)
    q3 = w_qb.reshape(MLA_QR, MLA_HEADS, MLA_NOPE + MLA_ROPE)
    swapped = jnp.concatenate([q3[:, :, MLA_NOPE + half:], q3[:, :, MLA_NOPE:MLA_NOPE + half]], axis=2)
    kv3 = w_kvb.reshape(MLA_KVR, MLA_HEADS, MLA_NOPE + MLA_V)
    wv = kv3[:, :, MLA_NOPE:].reshape(MLA_KVR, -1).T.astype(BF16)
    return win, pad(q3, 0, 32), pad(swapped, MLA_NOPE, 32), pad(kv3[:, :, :MLA_NOPE], 0, MLA_V), wv


def _rope_tables(T):
    f32 = np.float32
    freqs = f32(ROPE_THETA) ** (-np.arange(0, MLA_ROPE, 2, dtype=f32) / f32(MLA_ROPE))
    ang = np.arange(T, dtype=f32)[:, None] * freqs[None, :]
    cos, sin = np.cos(ang), np.sin(ang)
    z = lambda n: np.zeros((T, n), f32)
    scale = f32((MLA_NOPE + MLA_ROPE) ** -0.5 * LOG2E)
    ck = np.concatenate([z(64), cos, cos, z(32)], axis=1)
    sk = np.concatenate([z(64), -sin, sin, z(32)], axis=1)
    cq = np.concatenate([np.ones((T, 64), f32), cos, cos, z(32)], axis=1) * scale
    sq = sk * scale
    return tuple(jnp.asarray(t, F32) for t in (cq, sq, ck, sk))


def _proj_even(h, g, win, gq, gkv, wqa, wqb, wk, wv, tables, T):
    N = h.shape[0]
    nt = T // TM
    full = lambda a: pl.BlockSpec(a.shape, lambda i: (0,) * a.ndim)
    tab = pl.BlockSpec((TM, 128), lambda i: (i % nt, 0))
    row = lambda w: pl.BlockSpec((TM, w), lambda i: (i, 0))
    outs = (1024, 1024, -8 * VT_ROWS, 1024, 128, -128)
    spec = lambda w: row(w) if w > 0 else pl.BlockSpec((None, -w, TM), lambda i: (i // nt, 0, i % nt))
    shape = lambda w: jax.ShapeDtypeStruct((N, w) if w > 0 else (N // T, -w, T), BF16)
    return pl.pallas_call(
        _proj_even_kernel,
        grid=(N // TM,),
        in_specs=[row(D), full(g), full(win), full(gq), full(gkv), full(wqa), full(wqb), full(wk), full(wv),
                  tab, tab, tab, tab],
        out_specs=[spec(w) for w in outs],
        out_shape=[shape(w) for w in outs],
        compiler_params=_cparams(("arbitrary",)),
        name="proj_even",
    )(h, g, win, gq, gkv, wqa, wqb, wk, wv, *tables)


def _finish_t(accs):
    ot = jnp.concatenate([acc[:HD] / acc[HD:HD + 1] for acc in accs], axis=0)
    return ot.T.astype(BF16)


def _causal_attention_pair_t(ia, nq, q_of, k_ref, vt_ref, o_ref, s_ref, smax_ref, m_ref, acc_ref, *, tq,
                             bias_of=None):
    nh = s_ref.shape[1]
    first_query = (ia * tq, (nq - 1 - ia) * tq)
    n_items = nq + 1
    assert n_items % 2 == 1
    key = lax.broadcasted_iota(jnp.int32, (tq, tq), 0)
    qry = lax.broadcasted_iota(jnp.int32, (tq, tq), 1)

    def item(p):
        if isinstance(p, int) and p < 2:
            return p, first_query[p], first_query[p]
        t = jnp.where(p - 2 >= ia, 1, 0)
        return t, jnp.where(t == 1, first_query[1], first_query[0]), (p - 2 - t * ia) * tq

    def scores_to(slot, p):
        t, q0, start = item(p)
        start = pl.multiple_of(start, tq)
        for j in range(nh):
            st = _dot_nt(k_ref[pl.ds(start, tq), 128 * j:128 * (j + 1)], q_of(t, q0, j))
            if bias_of is not None:
                st = st + bias_of(start, q0, j)
            if isinstance(p, int) and p < 2:
                st = jnp.where(key <= qry, st, NEG)
            s_ref[slot, j] = st
            smax_ref[slot, j] = jnp.max(st, axis=0, keepdims=True)

    def update_from(slot, p):
        t, _, start = item(p)
        start = pl.multiple_of(start, tq)
        m_new = [jnp.maximum(m_ref[t, j], smax_ref[slot, j]) for j in range(nh)]
        pts = [jnp.exp2(s_ref[slot, j] - m_new[j]).astype(BF16) for j in range(nh)]
        for j in range(nh):
            acc_ref[t, j] = (jnp.exp2(m_ref[t, j] - m_new[j]) * acc_ref[t, j]
                             + _dot(vt_ref[VT_ROWS * j:VT_ROWS * (j + 1), pl.ds(start, tq)], pts[j]))
            m_ref[t, j] = m_new[j]

    def pair(k):
        scores_to(1, k + 1)
        update_from(0, k)
        scores_to(0, k + 2)
        update_from(1, k + 1)

    def quad(c, _):
        pair(2 + 4 * c)
        pair(4 + 4 * c)
        return 0

    m_ref[...] = jnp.full(m_ref.shape, NEG, F32)
    acc_ref[...] = jnp.zeros(acc_ref.shape, F32)
    scores_to(0, 0)
    pair(0)
    later_pairs = (n_items - 1) // 2 - 1
    lax.fori_loop(0, later_pairs // 2, quad, 0)
    if later_pairs % 2:
        pair(n_items - 3)
    update_from(0, n_items - 1)
    for t in range(2):
        o_ref[pl.ds(pl.multiple_of(first_query[t], tq), tq), :] = _finish_t([acc_ref[t, j] for j in range(nh)])


def _attention_t_scratch(tq):
    return [pltpu.VMEM((2, 2, tq, tq), F32), pltpu.VMEM((2, 2, 1, tq), F32), pltpu.VMEM((2, 2, 1, tq), F32),
            pltpu.VMEM((2, 2, VT_ROWS, tq), F32)]


def _mla_kernel(q_ref, k_ref, vt_ref, o_ref, s_ref, smax_ref, m_ref, acc_ref, *, tq, nq):
    def q_of(t, q0, j):
        return q_ref[pl.ds(pl.multiple_of(q0, tq), tq), 128 * j:128 * (j + 1)]

    _causal_attention_pair_t(pl.program_id(2), nq, q_of, k_ref, vt_ref, o_ref, s_ref, smax_ref, m_ref, acc_ref,
                             tq=tq)


def _mla_attention(q, k, vt, B, T, tq=512):
    nq = T // tq
    return pl.pallas_call(
        functools.partial(_mla_kernel, tq=tq, nq=nq),
        grid=(B, MLA_HEADS // 2, nq // 2),
        in_specs=[pl.BlockSpec((T, 256), lambda b, p, i: (b, p)),
                  pl.BlockSpec((T, 256), lambda b, p, i: (b, p)),
                  pl.BlockSpec((None, 2 * VT_ROWS, T), lambda b, p, i: (b, p, 0))],
        out_specs=pl.BlockSpec((T, 128), lambda b, p, i: (b, p)),
        out_shape=jax.ShapeDtypeStruct((B * T, MLA_HEADS * MLA_V), BF16),
        scratch_shapes=_attention_t_scratch(tq),
        compiler_params=_cparams(("arbitrary", "arbitrary", "arbitrary")),
        name="mla_attention",
    )(q, k, vt)


def _swa_kernel(sink_ref, q_ref, k_ref, vt_ref, tb_ref, o_ref, *, tq):
    i = pl.program_id(1)
    G = SWA_HEADS // SWA_KV
    W = 2 * SWA_WIN
    head = lax.broadcasted_iota(jnp.int32, (1, G * 128), 1) >> 7
    sinks = []
    for c in range(SWA_KV):
        sink = jnp.zeros((1, G * 128), F32)
        for g in range(G):
            sink = jnp.where(head == g, sink_ref[G * c + g], sink)
        sinks.append(sink)
    scores, values = [], []
    for r in range(tq // 128):
        qstart = i * tq + 128 * r
        kstart = pl.multiple_of(jnp.maximum(qstart - SWA_WIN, 0), 128)
        off = pl.multiple_of(qstart - kstart, 128)
        kw = k_ref[pl.ds(kstart, W), :]
        for c in range(SWA_KV):
            qg = jnp.concatenate([q_ref[128 * r:128 * (r + 1), 128 * (G * c + g):128 * (G * c + g + 1)]
                                  for g in range(G)], axis=0)
            bias = jnp.concatenate([tb_ref[G * c + g, :, pl.ds(off, 128)] for g in range(G)], axis=1)
            scores.append(_dot_nt(kw, qg) + bias)
            values.append(vt_ref[HD * c:HD * (c + 1), pl.ds(kstart, W)])
    probs = []
    for n, s in enumerate(scores):
        sink = sinks[n % SWA_KV]
        m = jnp.maximum(jnp.max(s, axis=0, keepdims=True), sink)
        p = jnp.exp2(s - m)
        probs.append((p.astype(BF16), jnp.sum(p, axis=0, keepdims=True) + jnp.exp2(sink - m)))
    for r in range(tq // 128):
        outs = []
        for c in range(SWA_KV):
            p, denom = probs[SWA_KV * r + c]
            ot = _dot(values[SWA_KV * r + c], p) / denom
            outs += [ot[:, 128 * g:128 * (g + 1)] for g in range(G)]
        o_ref[128 * r:128 * (r + 1), :] = jnp.concatenate(outs, axis=0).T.astype(BF16)


def _swa_attention(q8, k, vt, sinks, tb, B, T, tq=512):
    nq = T // tq
    return pl.pallas_call(
        functools.partial(_swa_kernel, tq=tq),
        grid=(B, nq),
        in_specs=[pl.BlockSpec(memory_space=pltpu.SMEM),
                  pl.BlockSpec((tq, 1024), lambda b, i: (b * nq + i, 0)),
                  pl.BlockSpec((T, 128), lambda b, i: (b, 0)),
                  pl.BlockSpec((None, 128, T), lambda b, i: (b, 0, 0)),
                  pl.BlockSpec((8, 256, 256), lambda b, i: (0, 0, 0))],
        out_specs=pl.BlockSpec((tq, 512), lambda b, i: (b * nq + i, 0)),
        out_shape=jax.ShapeDtypeStruct((B * T, SWA_HEADS * HD), BF16),
        compiler_params=_cparams(("arbitrary", "arbitrary")),
        name="swa_attention",
    )(sinks * LOG2E, q8, k, vt, tb)


def _outproj_router_kernel(h_ref, oa_ref, ob_ref, wo_ref, g_ref, wr_ref, br_ref,
                           hn_ref, u_ref, ri_ref, rg_ref, cnt_ref, base_ref):
    i = pl.program_id(0)

    @pl.when(i == 0)
    def _():
        base_ref[...] = jnp.zeros_like(base_ref)

    hn = h_ref[...] + _dot(oa_ref[...], wo_ref[0:512, :]) + _dot(ob_ref[...], wo_ref[512:1024, :])
    hn_ref[...] = hn
    u = _rms(hn, g_ref[...])
    u_ref[...] = _pack_halves(u)
    u_hi = u.astype(BF16)
    u_lo = (u - u_hi.astype(F32)).astype(BF16)
    both = _dot_nt(wr_ref[...], u_hi)
    logits = both[:128] + both[128:] + _dot_nt(wr_ref[0:128, :], u_lo) + br_ref[...]
    tm = logits.shape[1]
    sub = lax.broadcasted_iota(jnp.int32, (8, tm), 0).astype(F32)
    big = 1e6
    isg = sub < N_GROUPS
    gl = jnp.where(isg, logits[N_EXP:N_EXP + 8], NEG)
    gmax = jnp.max(gl, axis=0, keepdims=True)
    gsel = jnp.min(jnp.where(gl == gmax, sub, big), axis=0, keepdims=True)
    g_gate = 1.0 / jnp.sum(jnp.where(isg, jnp.exp(gl - gmax), 0.0), axis=0, keepdims=True)
    el = jnp.zeros((EPG, tm), F32)
    for grp in range(N_GROUPS):
        el = jnp.where(gsel == grp, logits[EPG * grp:EPG * (grp + 1)], el)
    m1 = jnp.max(el, axis=0, keepdims=True)
    i1 = jnp.min(jnp.where(el == m1, sub, big), axis=0, keepdims=True)
    el2 = jnp.where(sub == i1, NEG, el)
    m2 = jnp.max(el2, axis=0, keepdims=True)
    i2 = jnp.min(jnp.where(el2 == m2, sub, big), axis=0, keepdims=True)
    e1 = gsel * EPG + i1
    e2 = gsel * EPG + i2
    r = jnp.exp(m2 - m1)
    gate1 = g_gate / (1.0 + r)
    gate2 = g_gate * r / (1.0 + r)
    rows = lax.broadcasted_iota(jnp.int32, (128, tm), 0).astype(F32)
    oh1 = (rows == e1).astype(F32)
    oh2 = (rows == e2).astype(F32)
    oh = oh1 + oh2
    t_row = lax.broadcasted_iota(jnp.int32, (tm, tm), 0)
    t_col = lax.broadcasted_iota(jnp.int32, (tm, tm), 1)
    earlier = jnp.where(t_row < t_col, 1.0, 0.0).astype(BF16)
    prefix = _dot(oh.astype(BF16), earlier) + base_ref[:, 0:1]
    rank1 = jnp.sum(oh1 * prefix, axis=0, keepdims=True)
    rank2 = jnp.sum(oh2 * prefix, axis=0, keepdims=True)
    base_ref[...] = base_ref[...] + jnp.sum(oh, axis=1, keepdims=True)
    cnt_ref[...] = base_ref[...]
    ri_ref[...] = jnp.where(sub == 0, e1, jnp.where(sub == 1, e2, jnp.where(sub == 2, rank1,
                            jnp.where(sub == 3, rank2, 0.0))))
    rg_ref[...] = jnp.where(rows == 0, gate1, jnp.where(rows == 1, gate2, 0.0)).T


def _outproj_router(h, oa, ob, wo, g, wr, br):
    N = h.shape[0]
    full = lambda a: pl.BlockSpec(a.shape, lambda i: (0,) * a.ndim)
    row = lambda w: pl.BlockSpec((TM, w), lambda i: (i, 0))
    return pl.pallas_call(
        _outproj_router_kernel,
        grid=(N // TM,),
        in_specs=[row(D), row(512), row(512), full(wo), full(g), full(wr), full(br)],
        out_specs=[row(D), row(HALF), pl.BlockSpec((8, TM), lambda i: (0, i)), row(128),
                   pl.BlockSpec((128, 128), lambda i: (0, 0))],
        out_shape=[jax.ShapeDtypeStruct((N, D), F32), jax.ShapeDtypeStruct((N, HALF), jnp.int32),
                   jax.ShapeDtypeStruct((8, N), F32), jax.ShapeDtypeStruct((N, 128), F32),
                   jax.ShapeDtypeStruct((128, 128), F32)],
        scratch_shapes=[pltpu.VMEM((128, 128), F32)],
        compiler_params=_cparams(("arbitrary",)),
        name="outproj_router",
    )(h, oa, ob, wo, g, wr, br)


def _moe_kernel(te_ref, tv_ref, run_ref, nx_ref, x_ref, wg_hbm, wu_hbm, wd_hbm, o_ref,
                wg_buf, wu_buf, wd_buf, sems, wgu_s, wd_s, *, layer):
    i = pl.program_id(0)
    valid = tv_ref[i]

    def weight_copies(e, slot):
        return (pltpu.make_async_copy(wg_hbm.at[layer, e], wg_buf.at[slot], sems.at[slot, 0]),
                pltpu.make_async_copy(wu_hbm.at[layer, e], wu_buf.at[slot], sems.at[slot, 1]),
                pltpu.make_async_copy(wd_hbm.at[layer, e], wd_buf.at[slot], sems.at[slot, 2]))

    @pl.when(run_ref[i] > 0)
    def _():
        slot = run_ref[i] - 1

        @pl.when(i == 0)
        def _():
            for c in weight_copies(te_ref[i], slot):
                c.start()

        for c in weight_copies(te_ref[i], slot):
            c.wait()
        wgu_s[:, :D_EXP] = wg_buf[slot].astype(BF16)
        wgu_s[:, D_EXP:] = wu_buf[slot].astype(BF16)
        wd_s[...] = wd_buf[slot].astype(BF16)

        @pl.when(nx_ref[i] >= 0)
        def _():
            for c in weight_copies(nx_ref[i], 1 - slot):
                c.start()

    @pl.when(valid > 0)
    def _():
        rows = lax.broadcasted_iota(jnp.int32, x_ref.shape, 0)
        lo, hi = _unpack_halves(jnp.where(rows < valid, x_ref[...], 0))
        hgu = _dot(lo.astype(BF16), wgu_s[:HALF, :]) + _dot(hi.astype(BF16), wgu_s[HALF:, :])
        hg, hu = hgu[:, :D_EXP], hgu[:, D_EXP:]
        a = hg * (1.0 / (1.0 + jnp.exp(-hg))) * hu
        o_ref[...] = _pack_halves(_dot(a.astype(BF16), wd_s[...]))

    @pl.when(valid <= 0)
    def _():
        o_ref[...] = jnp.zeros_like(o_ref)


def _moe_experts(xs, plan, w_gate, w_up, w_down, layer):
    P = xs.shape[0]
    hbm = pl.BlockSpec(memory_space=pl.ANY)
    grid_spec = pltpu.PrefetchScalarGridSpec(
        num_scalar_prefetch=4,
        grid=(P // TM_E,),
        in_specs=[pl.BlockSpec((TM_E, HALF), lambda i, *_: (i, 0)), hbm, hbm, hbm],
        out_specs=pl.BlockSpec((TM_E, HALF), lambda i, *_: (i, 0)),
        scratch_shapes=[pltpu.VMEM((2, D, D_EXP), F32), pltpu.VMEM((2, D, D_EXP), F32),
                        pltpu.VMEM((2, D_EXP, D), F32), pltpu.SemaphoreType.DMA((2, 3)),
                        pltpu.VMEM((D, 2 * D_EXP), BF16), pltpu.VMEM((D_EXP, D), BF16)],
    )
    return pl.pallas_call(
        functools.partial(_moe_kernel, layer=layer),
        grid_spec=grid_spec,
        out_shape=jax.ShapeDtypeStruct((P, HALF), jnp.int32),
        compiler_params=_cparams(("arbitrary",)),
        name="moe_experts",
    )(*plan, xs, w_gate, w_up, w_down)


def _sc_mesh():
    return plsc.VectorSubcoreMesh(core_axis_name="c", subcore_axis_name="s",
                                  num_cores=SC_CORES, num_subcores=SC_SUBCORES)


def _sc_scatter_rows(src, d1, d2, P):
    N, W = src.shape
    per_w = N // (SC_CORES * SC_SUBCORES)

    @functools.partial(pl.kernel, mesh=_sc_mesh(), out_type=jax.ShapeDtypeStruct((P, W), src.dtype),
                       scratch_types=[pltpu.VMEM((SC_CHUNK,), jnp.int32), pltpu.VMEM((SC_CHUNK, W), src.dtype)],
                       name="sc_scatter_rows")
    def k(src_hbm, d1_hbm, d2_hbm, out_hbm, idx_v, rows_v):
        wid = lax.axis_index("s") * SC_CORES + lax.axis_index("c")

        @pl.loop(0, per_w // SC_CHUNK)
        def _(c):
            off = wid * per_w + c * SC_CHUNK
            pltpu.sync_copy(src_hbm.at[pl.ds(off, SC_CHUNK)], rows_v)
            pltpu.sync_copy(d1_hbm.at[pl.ds(off, SC_CHUNK)], idx_v)
            pltpu.sync_copy(rows_v, out_hbm.at[idx_v])
            pltpu.sync_copy(d2_hbm.at[pl.ds(off, SC_CHUNK)], idx_v)
            pltpu.sync_copy(rows_v, out_hbm.at[idx_v])

    return k(src, d1, d2)


def _sc_gather_rows(table, idx):
    B, W = idx.shape[0], table.shape[1]
    per_w = B // (SC_CORES * SC_SUBCORES)

    @functools.partial(pl.kernel, mesh=_sc_mesh(), out_type=jax.ShapeDtypeStruct((B, W), table.dtype),
                       scratch_types=[pltpu.VMEM((SC_CHUNK,), jnp.int32), pltpu.VMEM((SC_CHUNK, W), table.dtype)],
                       name="sc_gather_rows")
    def k(table_hbm, idx_hbm, out_hbm, idx_v, rows_v):
        wid = lax.axis_index("s") * SC_CORES + lax.axis_index("c")

        @pl.loop(0, per_w // SC_CHUNK)
        def _(c):
            off = wid * per_w + c * SC_CHUNK
            pltpu.sync_copy(idx_hbm.at[pl.ds(off, SC_CHUNK)], idx_v)
            pltpu.sync_copy(table_hbm.at[idx_v], rows_v)
            pltpu.sync_copy(rows_v, out_hbm.at[pl.ds(off, SC_CHUNK)])

    return k(table, idx)


def _dest_kernel(offs_ref, ri_ref, d_ref):
    ri = ri_ref[...]
    base = jnp.zeros(ri.shape, F32)
    for e in range(N_EXP):
        base = jnp.where(ri == e, offs_ref[e].astype(F32), base)
    d_ref[...] = (base + pltpu.roll(ri, 6, 0)).astype(jnp.int32)


def _dispatch_plan(ri, cnt, N):
    counts = cnt[:N_EXP, 0].astype(jnp.int32)
    padded = ((counts + TM_E - 1) // TM_E) * TM_E
    ends = jnp.cumsum(padded)
    offs = ends - padded
    take = lambda t, i: t.at[i].get(mode="promise_in_bounds")
    dest = pl.pallas_call(
        _dest_kernel,
        in_specs=[pl.BlockSpec(memory_space=pltpu.SMEM), pl.BlockSpec((8, N), lambda: (0, 0))],
        out_specs=pl.BlockSpec((8, N), lambda: (0, 0)),
        out_shape=jax.ShapeDtypeStruct((8, N), jnp.int32),
        name="dispatch_rows",
    )(offs, ri)
    d1, d2 = dest[0], dest[1]
    P = 2 * N + N_EXP * TM_E
    tile_start = jnp.arange(P // TM_E, dtype=jnp.int32) * TM_E
    tile_expert = jnp.minimum(jnp.sum((ends[None, :] <= tile_start[:, None]).astype(jnp.int32), axis=1), N_EXP - 1)
    tile_valid = jnp.clip(take(offs + counts, tile_expert) - tile_start, 0, TM_E)
    n_tiles = P // TM_E
    tile = jnp.arange(n_tiles, dtype=jnp.int32)
    starts = (tile_valid > 0) & ((tile == 0) | (tile_expert != jnp.roll(tile_expert, 1)))
    slot = (jnp.cumsum(starts.astype(jnp.int32)) - 1) % 2
    run_code = jnp.where(starts, slot + 1, 0).astype(jnp.int32)
    start_at = jnp.where(starts, tile, n_tiles)
    next_start = jnp.concatenate([lax.cummin(start_at, reverse=True)[1:], jnp.full((1,), n_tiles, jnp.int32)])
    next_expert = jnp.where(next_start < n_tiles, take(tile_expert, jnp.minimum(next_start, n_tiles - 1)), -1)
    return d1, d2, P, (tile_expert, tile_valid, run_code, next_expert.astype(jnp.int32))


def _combine(h_ref, y1_ref, y2_ref, rg_ref):
    rg = rg_ref[...]
    lo1, hi1 = _unpack_halves(y1_ref[...])
    lo2, hi2 = _unpack_halves(y2_ref[...])
    g1, g2 = rg[:, 0:1], rg[:, 1:2]
    return h_ref[...] + jnp.concatenate([g1 * lo1 + g2 * lo2, g1 * hi1 + g2 * hi2], axis=1)


ODD_WIDTHS = (1024, 1024, 512, 512)


def _combine_proj_odd_kernel(h_ref, y1_ref, y2_ref, rg_ref, g_ref, win_ref, wvt_ref, ind_ref, hn_ref, vt_ref,
                             vdt_ref, *out_refs):
    hn = _combine(h_ref, y1_ref, y2_ref, rg_ref)
    hn_ref[...] = hn
    u = _rms(hn, g_ref[...]).astype(BF16)
    vt_ref[...] = _values_t(_dot_nt(wvt_ref[0:512, :], u)).astype(BF16)
    vdt_ref[...] = _dot_nt(wvt_ref[512:1024, :], u).astype(BF16)
    qc_ref, kc_ref, qd_ref, kd_ref = out_refs
    qc = _spread_heads(_dot(u, win_ref[:, 0:512]), lambda h: 0)
    kc = _spread_heads(_dot(u, win_ref[:, 512:1024]), lambda h: 0)
    for hh in range(8):
        qc_ref[:, 128 * hh:128 * (hh + 1)] = qc[hh].astype(BF16)
        kc_ref[:, 128 * hh:128 * (hh + 1)] = (kc[hh] + ind_ref[...]).astype(BF16)
    qd_ref[...] = _dot(u, win_ref[:, 1024:1536]).astype(BF16)
    kd_ref[...] = _dot(u, win_ref[:, 1536:2048]).astype(BF16)


def _odd_weights_kernel(w_ref, win_ref, wvt_ref):
    win_ref[:, 0:512] = (w_ref[:, 0:512] * (HD ** -0.5 * LOG2E)).astype(BF16)
    win_ref[:, 512:1024] = w_ref[:, 512:1024].astype(BF16)
    wvt_ref[...] = w_ref[:, 1024:1536].T.astype(BF16)


def _odd_weights(w_in):
    return pl.pallas_call(
        _odd_weights_kernel,
        grid=(2,),
        in_specs=[pl.BlockSpec((D, 1536), lambda i: (0, i))],
        out_specs=[pl.BlockSpec((D, 1024), lambda i: (0, i)), pl.BlockSpec((512, D), lambda i: (i, 0))],
        out_shape=[jax.ShapeDtypeStruct((D, 2048), BF16), jax.ShapeDtypeStruct((1024, D), BF16)],
        compiler_params=_cparams(("arbitrary",)),
        name="odd_weights",
    )(w_in)


def _block_indicator(T):
    assert T // MOBA_L <= 64
    ind = np.zeros((T, 128), np.float32)
    ind[np.arange(T), 64 + np.arange(T) // MOBA_L] = 1.0
    return jnp.asarray(ind)


def _y_specs(N):
    return [pl.BlockSpec((TM, HALF), lambda i: (i, 0)), pl.BlockSpec((TM, HALF), lambda i: (i + N // TM, 0))]


def _combine_proj_odd(h, y12, rg, g, win, wvt, ind, T):
    N = h.shape[0]
    nt = T // TM
    full = lambda a: pl.BlockSpec(a.shape, lambda i: (0,) * a.ndim)
    row = lambda w: pl.BlockSpec((TM, w), lambda i: (i, 0))
    return pl.pallas_call(
        _combine_proj_odd_kernel,
        grid=(N // TM,),
        in_specs=[row(D)] + _y_specs(N) + [row(128), full(g), full(win), full(wvt),
                  pl.BlockSpec((TM, 128), lambda i: (i % nt, 0))],
        out_specs=[row(D), pl.BlockSpec((None, 8 * VT_ROWS, TM), lambda i: (i // nt, 0, i % nt)),
                   pl.BlockSpec((None, 512, TM), lambda i: (i // nt, 0, i % nt))]
        + [row(w) for w in ODD_WIDTHS],
        out_shape=[jax.ShapeDtypeStruct((N, D), F32), jax.ShapeDtypeStruct((N // T, 8 * VT_ROWS, T), BF16),
                   jax.ShapeDtypeStruct((N // T, 512, T), BF16)]
        + [jax.ShapeDtypeStruct((N, w), BF16) for w in ODD_WIDTHS],
        compiler_params=_cparams(("arbitrary",)),
        name="combine_proj_odd",
    )(h, y12, y12, rg, g, win, wvt, ind)


def _combine_final_kernel(h_ref, y1_ref, y2_ref, rg_ref, g_ref, o_ref):
    o_ref[...] = _rms(_combine(h_ref, y1_ref, y2_ref, rg_ref), g_ref[...])


def _combine_final(h, y12, rg, g):
    N = h.shape[0]
    row = lambda w: pl.BlockSpec((TM, w), lambda i: (i, 0))
    return pl.pallas_call(
        _combine_final_kernel,
        grid=(N // TM,),
        in_specs=[row(D)] + _y_specs(N) + [row(128), pl.BlockSpec((1, D), lambda i: (0, 0))],
        out_specs=row(D),
        out_shape=jax.ShapeDtypeStruct((N, D), F32),
        compiler_params=_cparams(("arbitrary",)),
        name="combine_final",
    )(h, y12, y12, rg, g)


def _moba_kernel(q_ref, k_ref, vt_ref, tb_ref, o_ref, kmean_ref, ksplit_ref, qa_ref, s_ref, smax_ref, m_ref, acc_ref,
                 *, nkb, nq):
    ia = pl.program_id(2)
    L, tq = MOBA_L, MOBA_TQ

    @pl.when(ia == 0)
    def _():
        kmean_ref[...] = jnp.zeros_like(kmean_ref)
        for n in range(nkb):
            kmean_ref[64 + n:65 + n, :] = jnp.sum(k_ref[n * L:(n + 1) * L, :].astype(F32), axis=0,
                                                  keepdims=True) * (1.0 / L)
        hi = kmean_ref[...].astype(BF16)
        ksplit_ref[0] = hi
        ksplit_ref[1] = (kmean_ref[...] - hi.astype(F32)).astype(BF16)

    nb = -(-nkb // 8) * 8
    lane = lax.broadcasted_iota(jnp.int32, (tq, 128), 1)
    blk = lax.broadcasted_iota(jnp.int32, (nb, tq), 0)
    half = jnp.where(lax.broadcasted_iota(jnp.int32, (nb, tq), 1) >= L, 1, 0)
    blk_f = blk.astype(F32)
    for t, tile in enumerate((ia, nq - 1 - ia)):
        own = 2 * tile + half
        for j in range(2):
            qj = q_ref[pl.ds(pl.multiple_of(tile * tq, tq), tq), 128 * j:128 * (j + 1)]
            gsc = (_dot_nt(ksplit_ref[0, 64:64 + nb, 128 * j:128 * (j + 1)], qj)
                   + _dot_nt(ksplit_ref[1, 64:64 + nb, 128 * j:128 * (j + 1)], qj))
            g = jnp.where(blk < own, gsc, NEG)
            allowed = jnp.where(blk == own, 1.0, 0.0)
            for _ in range(MOBA_TOPK):
                mx = jnp.max(g, axis=0, keepdims=True)
                idx = jnp.min(jnp.where(g == mx, blk_f, 1e6), axis=0, keepdims=True)
                pick = blk_f == idx
                allowed = jnp.maximum(allowed, jnp.where(pick, jnp.where(mx > 0.5 * NEG, 1.0, 0.0), 0.0))
                g = jnp.where(pick, 2.0 * NEG, g)
            mask = jnp.concatenate([jnp.zeros((64, tq), F32), jnp.where(allowed > 0.5, 0.0, NEG),
                                    jnp.zeros((64 - nb, tq), F32)], axis=0)
            qa_ref[t, j] = jnp.where(lane < HD, qj, mask.T.astype(BF16))

    def bias_of(start, q0, j):
        parts = []
        for c in range(tq // L):
            off = jnp.minimum(q0 - start - c * L, BIAS_CONST_FROM) + BIAS_PAD
            parts.append(tb_ref[j, :, pl.ds(pl.multiple_of(off, L), tq)])
        return jnp.concatenate(parts, axis=0)

    _causal_attention_pair_t(ia, nq, lambda t, q0, j: qa_ref[t, j], k_ref, vt_ref, o_ref, s_ref, smax_ref, m_ref,
                             acc_ref, tq=tq, bias_of=bias_of)


def _moba_attention(q, k, vt, tb, B, T):
    tq = MOBA_TQ
    nq = T // tq
    return pl.pallas_call(
        functools.partial(_moba_kernel, nkb=T // MOBA_L, nq=nq),
        grid=(B, 4, nq // 2),
        in_specs=[pl.BlockSpec((T, 256), lambda b, p, i: (b, p)),
                  pl.BlockSpec((T, 256), lambda b, p, i: (b, p)),
                  pl.BlockSpec((None, 2 * VT_ROWS, T), lambda b, p, i: (b, p, 0)),
                  pl.BlockSpec((2, MOBA_L, BIAS_ROWS), lambda b, p, i: (p, 0, 0))],
        out_specs=pl.BlockSpec((T, 128), lambda b, p, i: (b, p)),
        out_shape=jax.ShapeDtypeStruct((B * T, 512), BF16),
        scratch_shapes=[pltpu.VMEM((128, 256), F32), pltpu.VMEM((2, 128, 256), BF16),
                        pltpu.VMEM((2, 2, tq, 128), BF16)]
        + _attention_t_scratch(tq),
        compiler_params=_cparams(("arbitrary", "arbitrary", "arbitrary")),
        name="moba_attention",
    )(q, k, vt, tb)


def _sb_kernel(q_ref, k_ref, vt_ref, o_ref, *, tq, pairs):
    i = pl.program_id(2)
    lane = lax.broadcasted_iota(jnp.int32, (tq, 128), 1)
    qs = []
    for p in range(pairs):
        q_pair = q_ref[:, 128 * p:128 * (p + 1)]
        qs += [jnp.where(lane < HD, q_pair, 0).astype(BF16), jnp.where(lane >= HD, q_pair, 0).astype(BF16)]
    key = lax.broadcasted_iota(jnp.int32, (tq, tq), 0)
    qry = lax.broadcasted_iota(jnp.int32, (tq, tq), 1)
    past = key < qry
    suffix = jnp.where(qry >= key, -1.0, 0.0).astype(BF16)
    heads = range(2 * pairs)

    def step(kb, carry, boundary):
        start = pl.multiple_of(kb * tq, tq)
        zs = [_dot_nt(k_ref[pl.ds(start, tq), 128 * (h // 2):128 * (h // 2 + 1)], qs[h]) for h in heads]
        parts = []
        for z in zs:
            neg_abs = pltpu.bitcast(pltpu.bitcast(z, jnp.uint32) | jnp.uint32(0x80000000), F32)
            soft = jnp.maximum(z, 0.0) + jnp.log2(1.0 + jnp.exp2(neg_abs))
            if boundary:
                soft = jnp.where(past, soft, 0.0)
            parts.append(soft.astype(BF16))
        sums = [_dot(suffix, part) for part in parts]
        ws = []
        for h in heads:
            w = jnp.exp2(zs[h] + (carry[h][0] + sums[h]))
            ws.append((jnp.where(past, w, 0.0) if boundary else w).astype(BF16))
        return tuple((carry[h][0] + sums[h][0:1], carry[h][1]
                      + _dot(vt_ref[128 * (h // 2):128 * (h // 2 + 1), pl.ds(start, tq)], ws[h])) for h in heads)

    def live(carry):
        top = carry[0][0]
        for c, _ in carry[1:]:
            top = jnp.maximum(top, c)
        return (jnp.max(top) > SB_DONE).astype(jnp.int32)

    def body(state):
        t, _, carry = state
        carry = step(i - 1 - t, carry, False)
        return t + 1, live(carry), carry

    init = tuple((jnp.zeros((1, tq), F32), jnp.zeros((128, tq), F32)) for _ in range(2 * pairs))
    carry = step(i, init, True)
    _, _, carry = lax.while_loop(lambda st: (st[0] < i) & (st[1] > 0), body, (jnp.int32(0), live(carry), carry))
    rows = lax.broadcasted_iota(jnp.int32, (128, tq), 0)
    for p in range(pairs):
        ot = jnp.where(rows < HD, carry[2 * p][1], carry[2 * p + 1][1])
        o_ref[:, 128 * p:128 * (p + 1)] = ot.T.astype(BF16)


def _sb_attention(q, k, vt, B, T, tq=256, pairs=4):
    nq = T // tq
    w = 128 * pairs
    return pl.pallas_call(
        functools.partial(_sb_kernel, tq=tq, pairs=pairs),
        grid=(B, 4 // pairs, nq),
        in_specs=[pl.BlockSpec((tq, w), lambda b, p, i: (b * nq + i, p)),
                  pl.BlockSpec((T, w), lambda b, p, i: (b, p)),
                  pl.BlockSpec((None, w, T), lambda b, p, i: (b, p, 0))],
        out_specs=pl.BlockSpec((tq, w), lambda b, p, i: (b * nq + i, p)),
        out_shape=jax.ShapeDtypeStruct((B * T, 512), BF16),
        compiler_params=_cparams(("arbitrary", "arbitrary", "arbitrary")),
        name="sb_attention",
    )(q, k, vt)


def _router_weights(w_group, b_group, w_router, b_router):
    pad = 128 - N_EXP - N_GROUPS
    wr = jnp.concatenate([w_router, w_group, jnp.zeros((D, pad), F32)], axis=1).T
    hi = wr.astype(BF16)
    lo = (wr - hi.astype(F32)).astype(BF16)
    br = jnp.concatenate([b_router, b_group, jnp.zeros((pad,), F32)])[:, None]
    return jnp.concatenate([hi, lo], axis=0), br


def _ffn(h, oa, ob, wo, g, w_group, b_group, w_router, b_router, w_gate, w_up, w_down, layer):
    wr, br = _router_weights(w_group, b_group, w_router, b_router)
    hn, u, ri, rg, cnt = _outproj_router(h, oa, ob, wo.astype(BF16), g[None, :], wr, br)
    d1, d2, P, plan = _dispatch_plan(ri, cnt, h.shape[0])
    xs = _sc_scatter_rows(u, d1, d2, P)
    ys = _moe_experts(xs, plan, w_gate, w_up, w_down, layer)
    return hn, _sc_gather_rows(ys, jnp.concatenate([d1, d2])), rg


def kernel(x, norm_mix, norm_ffn, norm_final, rel_bias_table, w_in_even, g_mla_q, g_mla_kv, w_mla_qb, w_mla_kvb, swa_sinks, w_out_even, w_in_odd, w_out_odd, moe_w_group, moe_b_group, moe_w_router, moe_b_router, moe_w_gate, moe_w_up, moe_w_down):
    B, T, _ = x.shape
    h = x.reshape(B * T, D)
    tb_swa, tb_moba = _bias_tables(rel_bias_table)

    win, wqa, wqb, wk, wv = _even_weights(w_in_even[0], w_mla_qb[0], w_mla_kvb[0])
    q, k, v, qs8, ks, vst = _proj_even(h, norm_mix[0][None, :], win, g_mla_q[0][None, :], g_mla_kv[0][None, :],
                                       wqa, wqb, wk, wv, _rope_tables(T), T)
    oa = _mla_attention(q, k, v, B, T)
    ob = _swa_attention(qs8, ks, vst, swa_sinks[0], tb_swa, B, T)
    h, y12, rg = _ffn(h, oa, ob, w_out_even[0], norm_ffn[0], moe_w_group[0], moe_b_group[0], moe_w_router[0],
                      moe_b_router[0], moe_w_gate, moe_w_up, moe_w_down, 0)

    h, vct, vdt, qc, kc, qd, kd = _combine_proj_odd(h, y12, rg, norm_mix[1][None, :], *_odd_weights(w_in_odd[0]),
                                                   _block_indicator(T), T)
    oc = _moba_attention(qc, kc, vct, tb_moba, B, T)
    od = _sb_attention(qd, kd, vdt, B, T)
    h, y12, rg = _ffn(h, oc, od, w_out_odd[0], norm_ffn[1], moe_w_group[1], moe_b_group[1], moe_w_router[1],
                      moe_b_router[1], moe_w_gate, moe_w_up, moe_w_down, 1)
    out = _combine_final(h, y12, rg, norm_final[None, :])
    return out.reshape(B, T, D)
```

```python
import functools
import math

import numpy as np
import jax
import jax.numpy as jnp
from jax import lax
from jax.experimental import pallas as pl
from jax.experimental.pallas import tpu as pltpu
from jax.experimental.pallas import tpu_sc as plsc

F32 = jnp.float32
BF16 = jnp.bfloat16
NEG = -1e30
EPS = 1e-6

D = 1024
HD = 64
MLA_HEADS, MLA_QR, MLA_KVR, MLA_NOPE, MLA_ROPE, MLA_V = 8, 256, 128, 64, 32, 64
ROPE_THETA = 10000.0
SWA_HEADS, SWA_KV, SWA_WIN = 8, 2, 128
MOBA_L, MOBA_TOPK = 256, 3
REL_BUCKETS, REL_MAX = 32, 2048
N_GROUPS, EPG, N_EXP, D_EXP = 4, 8, 32, 256
BIAS_CONST_FROM = 1792
MOBA_TQ = 2 * MOBA_L
BIAS_PAD = MOBA_L
BIAS_ROWS = BIAS_PAD + BIAS_CONST_FROM + MOBA_TQ
SB_DONE = -160.0
TM = 1024
TM_E = 512
VMEM_LIMIT = 56 * 1024 * 1024
SC_CORES, SC_SUBCORES = 2, 16
SC_CHUNK = 128
HALF = D // 2
VT_ROWS = HD + 16
LOG2E = math.log2(math.e)


def _pack_halves(x):
    lo = pltpu.bitcast(x[:, :HALF].astype(BF16).astype(F32), jnp.uint32) >> 16
    hi = pltpu.bitcast(x[:, HALF:].astype(BF16).astype(F32), jnp.uint32) & jnp.uint32(0xFFFF0000)
    return pltpu.bitcast(lo | hi, jnp.int32)


def _unpack_halves(w):
    w = pltpu.bitcast(w, jnp.uint32)
    return pltpu.bitcast(w << 16, F32), pltpu.bitcast(w & jnp.uint32(0xFFFF0000), F32)


def _cparams(sem):
    return pltpu.CompilerParams(dimension_semantics=sem, vmem_limit_bytes=VMEM_LIMIT)


def _dot(a, b):
    return jnp.dot(a, b, preferred_element_type=F32)


def _dot_nt(a, b):
    return lax.dot_general(a, b, (((1,), (1,)), ((), ())), preferred_element_type=F32)


def _rms(x, g):
    return x * lax.rsqrt(jnp.mean(x * x, axis=-1, keepdims=True) + EPS) * g


def _values_t(vt):
    ones = jnp.ones((VT_ROWS - HD, vt.shape[1]), vt.dtype)
    return jnp.concatenate([blk for h in range(8) for blk in (vt[HD * h:HD * (h + 1)], ones)], axis=0)


def _spread_heads(x, lane_of):
    lane = lax.broadcasted_iota(jnp.int32, (x.shape[0], 128), 1)
    out = []
    for h in range(8):
        pair = x[:, 128 * (h // 2):128 * (h // 2 + 1)]
        if HD * (h % 2) != lane_of(h):
            pair = pltpu.roll(pair, HD, 1)
        out.append(jnp.where((lane >= lane_of(h)) & (lane < lane_of(h) + HD), pair, 0.0))
    return out


def _t5_bucket_np(dist):
    n = np.maximum(dist, 0)
    max_exact = REL_BUCKETS // 2
    nf = np.maximum(n, 1).astype(np.float32)
    large = max_exact + (np.log(nf / np.float32(max_exact)) / np.float32(math.log(REL_MAX / max_exact))
                         * np.float32(REL_BUCKETS - max_exact)).astype(np.int32)
    large = np.minimum(large, REL_BUCKETS - 1)
    return np.where(n < max_exact, n, large).astype(np.int32)


def _bias_kernel(table_ref, bucket_ref, out_ref, *, scale, ranges):
    b = bucket_ref[...]
    for m, (lo, hi) in enumerate(ranges):
        @pl.when(pl.program_id(0) == m)
        def _():
            for h in range(8):
                acc = jnp.full(b.shape, NEG if lo == REL_BUCKETS else table_ref[lo, h] * scale, F32)
                for k in range(lo + 1, min(hi, REL_BUCKETS - 1) + 1):
                    acc = jnp.where(b == k, table_ref[k, h] * scale, acc)
                if hi == REL_BUCKETS:
                    acc = jnp.where(b == REL_BUCKETS, NEG, acc)
                out_ref[h] = acc


def _bias_lookup(rel_table, dist, scale, name, masked=None):
    R, C = dist.shape
    buckets = _t5_bucket_np(dist)
    if masked is not None:
        buckets = np.where(masked, REL_BUCKETS, buckets).astype(np.int32)
    ranges = tuple((int(buckets[:, c:c + 256].min()), int(buckets[:, c:c + 256].max())) for c in range(0, C, 256))
    return pl.pallas_call(
        functools.partial(_bias_kernel, scale=scale, ranges=ranges),
        grid=(C // 256,),
        in_specs=[pl.BlockSpec(memory_space=pltpu.SMEM),
                  pl.BlockSpec((R, 256), lambda m: (0, m))],
        out_specs=pl.BlockSpec((8, R, 256), lambda m: (0, 0, m)),
        out_shape=jax.ShapeDtypeStruct((8, R, C), F32),
        compiler_params=_cparams(("arbitrary",)),
        name=name,
    )(rel_table, jnp.asarray(buckets))


def _bias_tables(rel_table):
    d = np.arange(256)[None, :] - np.arange(256)[:, None]
    swa = _bias_lookup(rel_table, d, LOG2E, "bias_swa", masked=(d < 0) | (d >= SWA_WIN))
    moba = _bias_lookup(rel_table, np.arange(BIAS_ROWS)[None, :] - BIAS_PAD - np.arange(MOBA_L)[:, None], LOG2E,
                        "bias_moba")
    return swa, moba


def _proj_even_kernel(h_ref, g_ref, win_ref, gq_ref, gkv_ref, wqa_ref, wqb_ref, wk_ref, wv_ref,
                      cq_ref, sq_ref, ck_ref, sk_ref,
                      q_ref, k_ref, v_ref, qs_ref, ks_ref, vst_ref):
    u = _rms(h_ref[...], g_ref[...]).astype(BF16)
    nq = _rms(_dot(u, win_ref[:, 0:256]), gq_ref[...]).astype(BF16)
    qa = _dot(nq, wqa_ref[...])
    qb = _dot(nq, wqb_ref[...])
    cq, sq = cq_ref[...], sq_ref[...]
    for hh in range(MLA_HEADS):
        sl = slice(128 * hh, 128 * (hh + 1))
        q_ref[:, sl] = (qa[:, sl] * cq + qb[:, sl] * sq).astype(BF16)
    nkv = _rms(_dot(u, win_ref[:, 256:384]), gkv_ref[...]).astype(BF16)
    kn = _dot(nkv, wk_ref[...])
    v_ref[...] = _values_t(_dot_nt(wv_ref[...], nkv)).astype(BF16)
    rest = _dot(u, win_ref[:, 896:1408])
    kr = rest[:, 128:256] * ck_ref[...] + rest[:, 256:384] * sk_ref[...]
    for hh in range(MLA_HEADS):
        sl = slice(128 * hh, 128 * (hh + 1))
        k_ref[:, sl] = (kn[:, sl] + kr).astype(BF16)
    qs = _spread_heads(_dot(u, win_ref[:, 384:896]), lambda hq: HD * (hq // (SWA_HEADS // SWA_KV)))
    for hq in range(SWA_HEADS):
        qs_ref[:, 128 * hq:128 * (hq + 1)] = qs[hq].astype(BF16)
    ks_ref[...] = rest[:, 0:128].astype(BF16)
    vst_ref[...] = rest[:, 384:512].T.astype(BF16)


def _even_weights_kernel(w_ref, wqb_in_ref, wkvb_ref, win_ref, wqa_ref, wqb_ref, wk_ref, wv_ref):
    o_kr = MLA_QR + MLA_KVR
    o_qs = o_kr + MLA_ROPE
    o_ks = o_qs + SWA_HEADS * HD
    o_vs = o_ks + SWA_KV * HD
    half = MLA_ROPE // 2
    w = w_ref[...]
    zero = lambda a, n: jnp.zeros((a.shape[0], n), F32)
    win_ref[...] = jnp.concatenate([
        w[:, 0:o_kr], w[:, o_qs:o_ks] * (HD ** -0.5 * LOG2E), w[:, o_ks:o_vs],
        zero(w, 64), w[:, o_kr:o_qs], zero(w, 32),
        zero(w, 64), w[:, o_kr + half:o_qs], w[:, o_kr:o_kr + half], zero(w, 32),
        w[:, o_vs:o_vs + SWA_KV * HD]], axis=1).astype(BF16)
    wq, wkv = wqb_in_ref[...], wkvb_ref[...]
    hq, hkv = MLA_NOPE + MLA_ROPE, MLA_NOPE + MLA_V
    qa, qb, kn, vt = [], [], [], []
    for h in range(MLA_HEADS):
        qa += [wq[:, hq * h:hq * (h + 1)], zero(wq, 32)]
        qb += [zero(wq, MLA_NOPE), wq[:, hq * h + MLA_NOPE + half:hq * (h + 1)],
               wq[:, hq * h + MLA_NOPE:hq * h + MLA_NOPE + half], zero(wq, 32)]
        kn += [wkv[:, hkv * h:hkv * h + MLA_NOPE], zero(wkv, MLA_V)]
        vt += [wkv[:, hkv * h + MLA_NOPE:hkv * (h + 1)]]
    wqa_ref[...] = jnp.concatenate(qa, axis=1).astype(BF16)
    wqb_ref[...] = jnp.concatenate(qb, axis=1).astype(BF16)
    wk_ref[...] = jnp.concatenate(kn, axis=1).astype(BF16)
    wv_ref[...] = jnp.concatenate(vt, axis=1).T.astype(BF16)


def _even_weights(w_in, w_qb, w_kvb):
    shapes = [(D, 1408), (MLA_QR, 128 * MLA_HEADS), (MLA_QR, 128 * MLA_HEADS), (MLA_KVR, 128 * MLA_HEADS),
              (MLA_V * MLA_HEADS, MLA_KVR)]
    return pl.pallas_call(
        _even_weights_kernel,
        out_shape=[jax.ShapeDtypeStruct(s, BF16) for s in shapes],
        compiler_params=_cparams(None),
        name="even_weights",
    )(w_in, w_qb, w_kvb)


def _rope_tables(T):
    f32 = np.float32
    freqs = f32(ROPE_THETA) ** (-np.arange(0, MLA_ROPE, 2, dtype=f32) / f32(MLA_ROPE))
    ang = np.arange(T, dtype=f32)[:, None] * freqs[None, :]
    cos, sin = np.cos(ang), np.sin(ang)
    z = lambda n: np.zeros((T, n), f32)
    scale = f32((MLA_NOPE + MLA_ROPE) ** -0.5 * LOG2E)
    ck = np.concatenate([z(64), cos, cos, z(32)], axis=1)
    sk = np.concatenate([z(64), -sin, sin, z(32)], axis=1)
    cq = np.concatenate([np.ones((T, 64), f32), cos, cos, z(32)], axis=1) * scale
    sq = sk * scale
    return tuple(jnp.asarray(t, F32) for t in (cq, sq, ck, sk))


def _proj_even(h, g, win, gq, gkv, wqa, wqb, wk, wv, tables, T):
    N = h.shape[0]
    nt = T // TM
    full = lambda a: pl.BlockSpec(a.shape, lambda i: (0,) * a.ndim)
    tab = pl.BlockSpec((TM, 128), lambda i: (i % nt, 0))
    row = lambda w: pl.BlockSpec((TM, w), lambda i: (i, 0))
    outs = (1024, 1024, -8 * VT_ROWS, 1024, 128, -128)
    spec = lambda w: row(w) if w > 0 else pl.BlockSpec((None, -w, TM), lambda i: (i // nt, 0, i % nt))
    shape = lambda w: jax.ShapeDtypeStruct((N, w) if w > 0 else (N // T, -w, T), BF16)
    return pl.pallas_call(
        _proj_even_kernel,
        grid=(N // TM,),
        in_specs=[row(D), full(g), full(win), full(gq), full(gkv), full(wqa), full(wqb), full(wk), full(wv),
                  tab, tab, tab, tab],
        out_specs=[spec(w) for w in outs],
        out_shape=[shape(w) for w in outs],
        compiler_params=_cparams(("arbitrary",)),
        name="proj_even",
    )(h, g, win, gq, gkv, wqa, wqb, wk, wv, *tables)


def _finish_t(accs):
    ot = jnp.concatenate([acc[:HD] / acc[HD:HD + 1] for acc in accs], axis=0)
    return ot.T.astype(BF16)


def _causal_attention_pair_t(ia, nq, q_of, k_ref, vt_ref, o_ref, s_ref, smax_ref, m_ref, acc_ref, *, tq,
                             bias_of=None):
    nh = s_ref.shape[1]
    first_query = (ia * tq, (nq - 1 - ia) * tq)
    n_items = nq + 1
    assert n_items % 2 == 1
    key = lax.broadcasted_iota(jnp.int32, (tq, tq), 0)
    qry = lax.broadcasted_iota(jnp.int32, (tq, tq), 1)

    def item(p):
        if isinstance(p, int) and p < 2:
            return p, first_query[p], first_query[p]
        t = jnp.where(p - 2 >= ia, 1, 0)
        return t, jnp.where(t == 1, first_query[1], first_query[0]), (p - 2 - t * ia) * tq

    def scores_to(slot, p):
        t, q0, start = item(p)
        start = pl.multiple_of(start, tq)
        for j in range(nh):
            st = _dot_nt(k_ref[pl.ds(start, tq), 128 * j:128 * (j + 1)], q_of(t, q0, j))
            if bias_of is not None:
                st = st + bias_of(start, q0, j)
            if isinstance(p, int) and p < 2:
                st = jnp.where(key <= qry, st, NEG)
            s_ref[slot, j] = st
            smax_ref[slot, j] = jnp.max(st, axis=0, keepdims=True)

    def update_from(slot, p):
        t, _, start = item(p)
        start = pl.multiple_of(start, tq)
        m_new = [jnp.maximum(m_ref[t, j], smax_ref[slot, j]) for j in range(nh)]
        pts = [jnp.exp2(s_ref[slot, j] - m_new[j]).astype(BF16) for j in range(nh)]
        for j in range(nh):
            acc_ref[t, j] = (jnp.exp2(m_ref[t, j] - m_new[j]) * acc_ref[t, j]
                             + _dot(vt_ref[VT_ROWS * j:VT_ROWS * (j + 1), pl.ds(start, tq)], pts[j]))
            m_ref[t, j] = m_new[j]

    def pair(k):
        scores_to(1, k + 1)
        update_from(0, k)
        scores_to(0, k + 2)
        update_from(1, k + 1)

    def quad(c, _):
        pair(2 + 4 * c)
        pair(4 + 4 * c)
        return 0

    m_ref[...] = jnp.full(m_ref.shape, NEG, F32)
    acc_ref[...] = jnp.zeros(acc_ref.shape, F32)
    scores_to(0, 0)
    pair(0)
    later_pairs = (n_items - 1) // 2 - 1
    lax.fori_loop(0, later_pairs // 2, quad, 0)
    if later_pairs % 2:
        pair(n_items - 3)
    update_from(0, n_items - 1)
    for t in range(2):
        o_ref[pl.ds(pl.multiple_of(first_query[t], tq), tq), :] = _finish_t([acc_ref[t, j] for j in range(nh)])


def _attention_t_scratch(tq):
    return [pltpu.VMEM((2, 2, tq, tq), F32), pltpu.VMEM((2, 2, 1, tq), F32), pltpu.VMEM((2, 2, 1, tq), F32),
            pltpu.VMEM((2, 2, VT_ROWS, tq), F32)]


def _mla_kernel(q_ref, k_ref, vt_ref, o_ref, s_ref, smax_ref, m_ref, acc_ref, *, tq, nq):
    def q_of(t, q0, j):
        return q_ref[pl.ds(pl.multiple_of(q0, tq), tq), 128 * j:128 * (j + 1)]

    _causal_attention_pair_t(pl.program_id(2), nq, q_of, k_ref, vt_ref, o_ref, s_ref, smax_ref, m_ref, acc_ref,
                             tq=tq)


def _mla_attention(q, k, vt, B, T, tq=512):
    nq = T // tq
    return pl.pallas_call(
        functools.partial(_mla_kernel, tq=tq, nq=nq),
        grid=(B, MLA_HEADS // 2, nq // 2),
        in_specs=[pl.BlockSpec((T, 256), lambda b, p, i: (b, p)),
                  pl.BlockSpec((T, 256), lambda b, p, i: (b, p)),
                  pl.BlockSpec((None, 2 * VT_ROWS, T), lambda b, p, i: (b, p, 0))],
        out_specs=pl.BlockSpec((T, 128), lambda b, p, i: (b, p)),
        out_shape=jax.ShapeDtypeStruct((B * T, MLA_HEADS * MLA_V), BF16),
        scratch_shapes=_attention_t_scratch(tq),
        compiler_params=_cparams(("arbitrary", "arbitrary", "arbitrary")),
        name="mla_attention",
    )(q, k, vt)


def _swa_kernel(sink_ref, q_ref, k_ref, vt_ref, tb_ref, o_ref, *, tq):
    i = pl.program_id(1)
    G = SWA_HEADS // SWA_KV
    W = 2 * SWA_WIN
    head = lax.broadcasted_iota(jnp.int32, (1, G * 128), 1) >> 7
    sinks = []
    for c in range(SWA_KV):
        sink = jnp.zeros((1, G * 128), F32)
        for g in range(G):
            sink = jnp.where(head == g, sink_ref[G * c + g], sink)
        sinks.append(sink)
    scores, values = [], []
    for r in range(tq // 128):
        qstart = i * tq + 128 * r
        kstart = pl.multiple_of(jnp.maximum(qstart - SWA_WIN, 0), 128)
        off = pl.multiple_of(qstart - kstart, 128)
        kw = k_ref[pl.ds(kstart, W), :]
        for c in range(SWA_KV):
            qg = jnp.concatenate([q_ref[128 * r:128 * (r + 1), 128 * (G * c + g):128 * (G * c + g + 1)]
                                  for g in range(G)], axis=0)
            bias = jnp.concatenate([tb_ref[G * c + g, :, pl.ds(off, 128)] for g in range(G)], axis=1)
            scores.append(_dot_nt(kw, qg) + bias)
            values.append(vt_ref[HD * c:HD * (c + 1), pl.ds(kstart, W)])
    probs = []
    for n, s in enumerate(scores):
        sink = sinks[n % SWA_KV]
        m = jnp.maximum(jnp.max(s, axis=0, keepdims=True), sink)
        p = jnp.exp2(s - m)
        probs.append((p.astype(BF16), jnp.sum(p, axis=0, keepdims=True) + jnp.exp2(sink - m)))
    for r in range(tq // 128):
        outs = []
        for c in range(SWA_KV):
            p, denom = probs[SWA_KV * r + c]
            ot = _dot(values[SWA_KV * r + c], p) / denom
            outs += [ot[:, 128 * g:128 * (g + 1)] for g in range(G)]
        o_ref[128 * r:128 * (r + 1), :] = jnp.concatenate(outs, axis=0).T.astype(BF16)


def _swa_attention(q8, k, vt, sinks, tb, B, T, tq=512):
    nq = T // tq
    return pl.pallas_call(
        functools.partial(_swa_kernel, tq=tq),
        grid=(B, nq),
        in_specs=[pl.BlockSpec(memory_space=pltpu.SMEM),
                  pl.BlockSpec((tq, 1024), lambda b, i: (b * nq + i, 0)),
                  pl.BlockSpec((T, 128), lambda b, i: (b, 0)),
                  pl.BlockSpec((None, 128, T), lambda b, i: (b, 0, 0)),
                  pl.BlockSpec((8, 256, 256), lambda b, i: (0, 0, 0))],
        out_specs=pl.BlockSpec((tq, 512), lambda b, i: (b * nq + i, 0)),
        out_shape=jax.ShapeDtypeStruct((B * T, SWA_HEADS * HD), BF16),
        compiler_params=_cparams(("arbitrary", "arbitrary")),
        name="swa_attention",
    )(sinks * LOG2E, q8, k, vt, tb)


def _outproj_router_kernel(h_ref, oa_ref, ob_ref, wo_ref, g_ref, wr_ref, br_ref,
                           hn_ref, u_ref, ri_ref, rg_ref, cnt_ref, base_ref):
    i = pl.program_id(0)

    @pl.when(i == 0)
    def _():
        base_ref[...] = jnp.zeros_like(base_ref)

    hn = h_ref[...] + _dot(oa_ref[...], wo_ref[0:512, :]) + _dot(ob_ref[...], wo_ref[512:1024, :])
    hn_ref[...] = hn
    u = _rms(hn, g_ref[...])
    u_ref[...] = _pack_halves(u)
    u_hi = u.astype(BF16)
    u_lo = (u - u_hi.astype(F32)).astype(BF16)
    both = _dot_nt(wr_ref[...], u_hi)
    logits = both[:128] + both[128:] + _dot_nt(wr_ref[0:128, :], u_lo) + br_ref[...]
    tm = logits.shape[1]
    sub = lax.broadcasted_iota(jnp.int32, (8, tm), 0).astype(F32)
    big = 1e6
    isg = sub < N_GROUPS
    gl = jnp.where(isg, logits[N_EXP:N_EXP + 8], NEG)
    gmax = jnp.max(gl, axis=0, keepdims=True)
    gsel = jnp.min(jnp.where(gl == gmax, sub, big), axis=0, keepdims=True)
    g_gate = 1.0 / jnp.sum(jnp.where(isg, jnp.exp(gl - gmax), 0.0), axis=0, keepdims=True)
    el = jnp.zeros((EPG, tm), F32)
    for grp in range(N_GROUPS):
        el = jnp.where(gsel == grp, logits[EPG * grp:EPG * (grp + 1)], el)
    m1 = jnp.max(el, axis=0, keepdims=True)
    i1 = jnp.min(jnp.where(el == m1, sub, big), axis=0, keepdims=True)
    el2 = jnp.where(sub == i1, NEG, el)
    m2 = jnp.max(el2, axis=0, keepdims=True)
    i2 = jnp.min(jnp.where(el2 == m2, sub, big), axis=0, keepdims=True)
    e1 = gsel * EPG + i1
    e2 = gsel * EPG + i2
    r = jnp.exp(m2 - m1)
    gate1 = g_gate / (1.0 + r)
    gate2 = g_gate * r / (1.0 + r)
    rows = lax.broadcasted_iota(jnp.int32, (128, tm), 0).astype(F32)
    oh1 = (rows == e1).astype(F32)
    oh2 = (rows == e2).astype(F32)
    oh = oh1 + oh2
    t_row = lax.broadcasted_iota(jnp.int32, (tm, tm), 0)
    t_col = lax.broadcasted_iota(jnp.int32, (tm, tm), 1)
    earlier = jnp.where(t_row < t_col, 1.0, 0.0).astype(BF16)
    prefix = _dot(oh.astype(BF16), earlier) + base_ref[:, 0:1]
    rank1 = jnp.sum(oh1 * prefix, axis=0, keepdims=True)
    rank2 = jnp.sum(oh2 * prefix, axis=0, keepdims=True)
    base_ref[...] = base_ref[...] + jnp.sum(oh, axis=1, keepdims=True)
    cnt_ref[...] = base_ref[...]
    ri_ref[...] = jnp.where(sub == 0, e1, jnp.where(sub == 1, e2, jnp.where(sub == 2, rank1,
                            jnp.where(sub == 3, rank2, 0.0))))
    rg_ref[...] = jnp.where(rows == 0, gate1, jnp.where(rows == 1, gate2, 0.0)).T


def _outproj_router(h, oa, ob, wo, g, wr, br):
    N = h.shape[0]
    full = lambda a: pl.BlockSpec(a.shape, lambda i: (0,) * a.ndim)
    row = lambda w: pl.BlockSpec((TM, w), lambda i: (i, 0))
    return pl.pallas_call(
        _outproj_router_kernel,
        grid=(N // TM,),
        in_specs=[row(D), row(512), row(512), full(wo), full(g), full(wr), full(br)],
        out_specs=[row(D), row(HALF), pl.BlockSpec((8, TM), lambda i: (0, i)), row(128),
                   pl.BlockSpec((128, 128), lambda i: (0, 0))],
        out_shape=[jax.ShapeDtypeStruct((N, D), F32), jax.ShapeDtypeStruct((N, HALF), jnp.int32),
                   jax.ShapeDtypeStruct((8, N), F32), jax.ShapeDtypeStruct((N, 128), F32),
                   jax.ShapeDtypeStruct((128, 128), F32)],
        scratch_shapes=[pltpu.VMEM((128, 128), F32)],
        compiler_params=_cparams(("arbitrary",)),
        name="outproj_router",
    )(h, oa, ob, wo, g, wr, br)


def _moe_kernel(te_ref, tv_ref, run_ref, nx_ref, x_ref, wg_hbm, wu_hbm, wd_hbm, o_ref,
                wg_buf, wu_buf, wd_buf, sems, wgu_s, wd_s, *, layer):
    i = pl.program_id(0)
    valid = tv_ref[i]

    def weight_copies(e, slot):
        return (pltpu.make_async_copy(wg_hbm.at[layer, e], wg_buf.at[slot], sems.at[slot, 0]),
                pltpu.make_async_copy(wu_hbm.at[layer, e], wu_buf.at[slot], sems.at[slot, 1]),
                pltpu.make_async_copy(wd_hbm.at[layer, e], wd_buf.at[slot], sems.at[slot, 2]))

    @pl.when(run_ref[i] > 0)
    def _():
        slot = run_ref[i] - 1

        @pl.when(i == 0)
        def _():
            for c in weight_copies(te_ref[i], slot):
                c.start()

        for c in weight_copies(te_ref[i], slot):
            c.wait()
        wgu_s[:, :D_EXP] = wg_buf[slot].astype(BF16)
        wgu_s[:, D_EXP:] = wu_buf[slot].astype(BF16)
        wd_s[...] = wd_buf[slot].astype(BF16)

        @pl.when(nx_ref[i] >= 0)
        def _():
            for c in weight_copies(nx_ref[i], 1 - slot):
                c.start()

    @pl.when(valid > 0)
    def _():
        rows = lax.broadcasted_iota(jnp.int32, x_ref.shape, 0)
        lo, hi = _unpack_halves(jnp.where(rows < valid, x_ref[...], 0))
        hgu = _dot(lo.astype(BF16), wgu_s[:HALF, :]) + _dot(hi.astype(BF16), wgu_s[HALF:, :])
        hg, hu = hgu[:, :D_EXP], hgu[:, D_EXP:]
        a = hg * (1.0 / (1.0 + jnp.exp(-hg))) * hu
        o_ref[...] = _pack_halves(_dot(a.astype(BF16), wd_s[...]))

    @pl.when(valid <= 0)
    def _():
        o_ref[...] = jnp.zeros_like(o_ref)


def _moe_experts(xs, plan, w_gate, w_up, w_down, layer):
    P = xs.shape[0]
    hbm = pl.BlockSpec(memory_space=pl.ANY)
    grid_spec = pltpu.PrefetchScalarGridSpec(
        num_scalar_prefetch=4,
        grid=(P // TM_E,),
        in_specs=[pl.BlockSpec((TM_E, HALF), lambda i, *_: (i, 0)), hbm, hbm, hbm],
        out_specs=pl.BlockSpec((TM_E, HALF), lambda i, *_: (i, 0)),
        scratch_shapes=[pltpu.VMEM((2, D, D_EXP), F32), pltpu.VMEM((2, D, D_EXP), F32),
                        pltpu.VMEM((2, D_EXP, D), F32), pltpu.SemaphoreType.DMA((2, 3)),
                        pltpu.VMEM((D, 2 * D_EXP), BF16), pltpu.VMEM((D_EXP, D), BF16)],
    )
    return pl.pallas_call(
        functools.partial(_moe_kernel, layer=layer),
        grid_spec=grid_spec,
        out_shape=jax.ShapeDtypeStruct((P, HALF), jnp.int32),
        compiler_params=_cparams(("arbitrary",)),
        name="moe_experts",
    )(*plan, xs, w_gate, w_up, w_down)


def _sc_mesh():
    return plsc.VectorSubcoreMesh(core_axis_name="c", subcore_axis_name="s",
                                  num_cores=SC_CORES, num_subcores=SC_SUBCORES)


def _sc_scatter_rows(src, d1, d2, P):
    N, W = src.shape
    per_w = N // (SC_CORES * SC_SUBCORES)

    @functools.partial(pl.kernel, mesh=_sc_mesh(), out_type=jax.ShapeDtypeStruct((P, W), src.dtype),
                       scratch_types=[pltpu.VMEM((SC_CHUNK,), jnp.int32), pltpu.VMEM((SC_CHUNK, W), src.dtype)],
                       name="sc_scatter_rows")
    def k(src_hbm, d1_hbm, d2_hbm, out_hbm, idx_v, rows_v):
        wid = lax.axis_index("s") * SC_CORES + lax.axis_index("c")

        @pl.loop(0, per_w // SC_CHUNK)
        def _(c):
            off = wid * per_w + c * SC_CHUNK
            pltpu.sync_copy(src_hbm.at[pl.ds(off, SC_CHUNK)], rows_v)
            pltpu.sync_copy(d1_hbm.at[pl.ds(off, SC_CHUNK)], idx_v)
            pltpu.sync_copy(rows_v, out_hbm.at[idx_v])
            pltpu.sync_copy(d2_hbm.at[pl.ds(off, SC_CHUNK)], idx_v)
            pltpu.sync_copy(rows_v, out_hbm.at[idx_v])

    return k(src, d1, d2)


def _sc_gather_rows(table, idx):
    B, W = idx.shape[0], table.shape[1]
    per_w = B // (SC_CORES * SC_SUBCORES)

    @functools.partial(pl.kernel, mesh=_sc_mesh(), out_type=jax.ShapeDtypeStruct((B, W), table.dtype),
                       scratch_types=[pltpu.VMEM((SC_CHUNK,), jnp.int32), pltpu.VMEM((SC_CHUNK, W), table.dtype)],
                       name="sc_gather_rows")
    def k(table_hbm, idx_hbm, out_hbm, idx_v, rows_v):
        wid = lax.axis_index("s") * SC_CORES + lax.axis_index("c")

        @pl.loop(0, per_w // SC_CHUNK)
        def _(c):
            off = wid * per_w + c * SC_CHUNK
            pltpu.sync_copy(idx_hbm.at[pl.ds(off, SC_CHUNK)], idx_v)
            pltpu.sync_copy(table_hbm.at[idx_v], rows_v)
            pltpu.sync_copy(rows_v, out_hbm.at[pl.ds(off, SC_CHUNK)])

    return k(table, idx)


def _dest_kernel(offs_ref, ri_ref, d_ref):
    ri = ri_ref[...]
    base = jnp.zeros(ri.shape, F32)
    for e in range(N_EXP):
        base = jnp.where(ri == e, offs_ref[e].astype(F32), base)
    d_ref[...] = (base + pltpu.roll(ri, 6, 0)).astype(jnp.int32)


def _dispatch_plan(ri, cnt, N):
    counts = cnt[:N_EXP, 0].astype(jnp.int32)
    padded = ((counts + TM_E - 1) // TM_E) * TM_E
    ends = jnp.cumsum(padded)
    offs = ends - padded
    take = lambda t, i: t.at[i].get(mode="promise_in_bounds")
    dest = pl.pallas_call(
        _dest_kernel,
        in_specs=[pl.BlockSpec(memory_space=pltpu.SMEM), pl.BlockSpec((8, N), lambda: (0, 0))],
        out_specs=pl.BlockSpec((8, N), lambda: (0, 0)),
        out_shape=jax.ShapeDtypeStruct((8, N), jnp.int32),
        name="dispatch_rows",
    )(offs, ri)
    d1, d2 = dest[0], dest[1]
    P = 2 * N + N_EXP * TM_E
    tile_start = jnp.arange(P // TM_E, dtype=jnp.int32) * TM_E
    tile_expert = jnp.minimum(jnp.sum((ends[None, :] <= tile_start[:, None]).astype(jnp.int32), axis=1), N_EXP - 1)
    tile_valid = jnp.clip(take(offs + counts, tile_expert) - tile_start, 0, TM_E)
    n_tiles = P // TM_E
    tile = jnp.arange(n_tiles, dtype=jnp.int32)
    starts = (tile_valid > 0) & ((tile == 0) | (tile_expert != jnp.roll(tile_expert, 1)))
    slot = (jnp.cumsum(starts.astype(jnp.int32)) - 1) % 2
    run_code = jnp.where(starts, slot + 1, 0).astype(jnp.int32)
    start_at = jnp.where(starts, tile, n_tiles)
    next_start = jnp.concatenate([lax.cummin(start_at, reverse=True)[1:], jnp.full((1,), n_tiles, jnp.int32)])
    next_expert = jnp.where(next_start < n_tiles, take(tile_expert, jnp.minimum(next_start, n_tiles - 1)), -1)
    return d1, d2, P, (tile_expert, tile_valid, run_code, next_expert.astype(jnp.int32))


def _combine(h_ref, y1_ref, y2_ref, rg_ref):
    rg = rg_ref[...]
    lo1, hi1 = _unpack_halves(y1_ref[...])
    lo2, hi2 = _unpack_halves(y2_ref[...])
    g1, g2 = rg[:, 0:1], rg[:, 1:2]
    return h_ref[...] + jnp.concatenate([g1 * lo1 + g2 * lo2, g1 * hi1 + g2 * hi2], axis=1)


ODD_WIDTHS = (1024, 1024, 512, 512)


def _combine_proj_odd_kernel(h_ref, y1_ref, y2_ref, rg_ref, g_ref, win_ref, wvt_ref, ind_ref, hn_ref, vt_ref,
                             vdt_ref, *out_refs):
    hn = _combine(h_ref, y1_ref, y2_ref, rg_ref)
    hn_ref[...] = hn
    u = _rms(hn, g_ref[...]).astype(BF16)
    vt_ref[...] = _values_t(_dot_nt(wvt_ref[0:512, :], u)).astype(BF16)
    vdt_ref[...] = _dot_nt(wvt_ref[512:1024, :], u).astype(BF16)
    qc_ref, kc_ref, qd_ref, kd_ref = out_refs
    qc = _spread_heads(_dot(u, win_ref[:, 0:512]), lambda h: 0)
    kc = _spread_heads(_dot(u, win_ref[:, 512:1024]), lambda h: 0)
    for hh in range(8):
        qc_ref[:, 128 * hh:128 * (hh + 1)] = qc[hh].astype(BF16)
        kc_ref[:, 128 * hh:128 * (hh + 1)] = (kc[hh] + ind_ref[...]).astype(BF16)
    qd_ref[...] = _dot(u, win_ref[:, 1024:1536]).astype(BF16)
    kd_ref[...] = _dot(u, win_ref[:, 1536:2048]).astype(BF16)


def _odd_weights_kernel(w_ref, win_ref, wvt_ref):
    win_ref[:, 0:512] = (w_ref[:, 0:512] * (HD ** -0.5 * LOG2E)).astype(BF16)
    win_ref[:, 512:1024] = w_ref[:, 512:1024].astype(BF16)
    wvt_ref[...] = w_ref[:, 1024:1536].T.astype(BF16)


def _odd_weights(w_in):
    return pl.pallas_call(
        _odd_weights_kernel,
        grid=(2,),
        in_specs=[pl.BlockSpec((D, 1536), lambda i: (0, i))],
        out_specs=[pl.BlockSpec((D, 1024), lambda i: (0, i)), pl.BlockSpec((512, D), lambda i: (i, 0))],
        out_shape=[jax.ShapeDtypeStruct((D, 2048), BF16), jax.ShapeDtypeStruct((1024, D), BF16)],
        compiler_params=_cparams(("arbitrary",)),
        name="odd_weights",
    )(w_in)


def _block_indicator(T):
    assert T // MOBA_L <= 64
    ind = np.zeros((T, 128), np.float32)
    ind[np.arange(T), 64 + np.arange(T) // MOBA_L] = 1.0
    return jnp.asarray(ind)


def _y_specs(N):
    return [pl.BlockSpec((TM, HALF), lambda i: (i, 0)), pl.BlockSpec((TM, HALF), lambda i: (i + N // TM, 0))]


def _combine_proj_odd(h, y12, rg, g, win, wvt, ind, T):
    N = h.shape[0]
    nt = T // TM
    full = lambda a: pl.BlockSpec(a.shape, lambda i: (0,) * a.ndim)
    row = lambda w: pl.BlockSpec((TM, w), lambda i: (i, 0))
    return pl.pallas_call(
        _combine_proj_odd_kernel,
        grid=(N // TM,),
        in_specs=[row(D)] + _y_specs(N) + [row(128), full(g), full(win), full(wvt),
                  pl.BlockSpec((TM, 128), lambda i: (i % nt, 0))],
        out_specs=[row(D), pl.BlockSpec((None, 8 * VT_ROWS, TM), lambda i: (i // nt, 0, i % nt)),
                   pl.BlockSpec((None, 512, TM), lambda i: (i // nt, 0, i % nt))]
        + [row(w) for w in ODD_WIDTHS],
        out_shape=[jax.ShapeDtypeStruct((N, D), F32), jax.ShapeDtypeStruct((N // T, 8 * VT_ROWS, T), BF16),
                   jax.ShapeDtypeStruct((N // T, 512, T), BF16)]
        + [jax.ShapeDtypeStruct((N, w), BF16) for w in ODD_WIDTHS],
        compiler_params=_cparams(("arbitrary",)),
        name="combine_proj_odd",
    )(h, y12, y12, rg, g, win, wvt, ind)


def _combine_final_kernel(h_ref, y1_ref, y2_ref, rg_ref, g_ref, o_ref):
    o_ref[...] = _rms(_combine(h_ref, y1_ref, y2_ref, rg_ref), g_ref[...])


def _combine_final(h, y12, rg, g):
    N = h.shape[0]
    row = lambda w: pl.BlockSpec((TM, w), lambda i: (i, 0))
    return pl.pallas_call(
        _combine_final_kernel,
        grid=(N // TM,),
        in_specs=[row(D)] + _y_specs(N) + [row(128), pl.BlockSpec((1, D), lambda i: (0, 0))],
        out_specs=row(D),
        out_shape=jax.ShapeDtypeStruct((N, D), F32),
        compiler_params=_cparams(("arbitrary",)),
        name="combine_final",
    )(h, y12, y12, rg, g)


def _moba_kernel(q_ref, k_ref, vt_ref, tb_ref, o_ref, kmean_ref, ksplit_ref, qa_ref, s_ref, smax_ref, m_ref, acc_ref,
                 *, nkb, nq):
    ia = pl.program_id(2)
    L, tq = MOBA_L, MOBA_TQ

    @pl.when(ia == 0)
    def _():
        kmean_ref[...] = jnp.zeros_like(kmean_ref)
        for n in range(nkb):
            kmean_ref[64 + n:65 + n, :] = jnp.sum(k_ref[n * L:(n + 1) * L, :].astype(F32), axis=0,
                                                  keepdims=True) * (1.0 / L)
        hi = kmean_ref[...].astype(BF16)
        ksplit_ref[0] = hi
        ksplit_ref[1] = (kmean_ref[...] - hi.astype(F32)).astype(BF16)

    nb = -(-nkb // 8) * 8
    lane = lax.broadcasted_iota(jnp.int32, (tq, 128), 1)
    blk = lax.broadcasted_iota(jnp.int32, (nb, tq), 0)
    half = jnp.where(lax.broadcasted_iota(jnp.int32, (nb, tq), 1) >= L, 1, 0)
    blk_f = blk.astype(F32)
    for t, tile in enumerate((ia, nq - 1 - ia)):
        own = 2 * tile + half
        for j in range(2):
            qj = q_ref[pl.ds(pl.multiple_of(tile * tq, tq), tq), 128 * j:128 * (j + 1)]
            gsc = (_dot_nt(ksplit_ref[0, 64:64 + nb, 128 * j:128 * (j + 1)], qj)
                   + _dot_nt(ksplit_ref[1, 64:64 + nb, 128 * j:128 * (j + 1)], qj))
            g = jnp.where(blk < own, gsc, NEG)
            allowed = jnp.where(blk == own, 1.0, 0.0)
            for _ in range(MOBA_TOPK):
                mx = jnp.max(g, axis=0, keepdims=True)
                idx = jnp.min(jnp.where(g == mx, blk_f, 1e6), axis=0, keepdims=True)
                pick = blk_f == idx
                allowed = jnp.maximum(allowed, jnp.where(pick, jnp.where(mx > 0.5 * NEG, 1.0, 0.0), 0.0))
                g = jnp.where(pick, 2.0 * NEG, g)
            mask = jnp.concatenate([jnp.zeros((64, tq), F32), jnp.where(allowed > 0.5, 0.0, NEG),
                                    jnp.zeros((64 - nb, tq), F32)], axis=0)
            qa_ref[t, j] = jnp.where(lane < HD, qj, mask.T.astype(BF16))

    def bias_of(start, q0, j):
        parts = []
        for c in range(tq // L):
            off = jnp.minimum(q0 - start - c * L, BIAS_CONST_FROM) + BIAS_PAD
            parts.append(tb_ref[j, :, pl.ds(pl.multiple_of(off, L), tq)])
        return jnp.concatenate(parts, axis=0)

    _causal_attention_pair_t(ia, nq, lambda t, q0, j: qa_ref[t, j], k_ref, vt_ref, o_ref, s_ref, smax_ref, m_ref,
                             acc_ref, tq=tq, bias_of=bias_of)


def _moba_attention(q, k, vt, tb, B, T):
    tq = MOBA_TQ
    nq = T // tq
    return pl.pallas_call(
        functools.partial(_moba_kernel, nkb=T // MOBA_L, nq=nq),
        grid=(B, 4, nq // 2),
        in_specs=[pl.BlockSpec((T, 256), lambda b, p, i: (b, p)),
                  pl.BlockSpec((T, 256), lambda b, p, i: (b, p)),
                  pl.BlockSpec((None, 2 * VT_ROWS, T), lambda b, p, i: (b, p, 0)),
                  pl.BlockSpec((2, MOBA_L, BIAS_ROWS), lambda b, p, i: (p, 0, 0))],
        out_specs=pl.BlockSpec((T, 128), lambda b, p, i: (b, p)),
        out_shape=jax.ShapeDtypeStruct((B * T, 512), BF16),
        scratch_shapes=[pltpu.VMEM((128, 256), F32), pltpu.VMEM((2, 128, 256), BF16),
                        pltpu.VMEM((2, 2, tq, 128), BF16)]
        + _attention_t_scratch(tq),
        compiler_params=_cparams(("arbitrary", "arbitrary", "arbitrary")),
        name="moba_attention",
    )(q, k, vt, tb)


def _sb_kernel(q_ref, k_ref, vt_ref, o_ref, *, tq, pairs):
    i = pl.program_id(2)
    lane = lax.broadcasted_iota(jnp.int32, (tq, 128), 1)
    qs = []
    for p in range(pairs):
        q_pair = q_ref[:, 128 * p:128 * (p + 1)]
        qs += [jnp.where(lane < HD, q_pair, 0).astype(BF16), jnp.where(lane >= HD, q_pair, 0).astype(BF16)]
    key = lax.broadcasted_iota(jnp.int32, (tq, tq), 0)
    qry = lax.broadcasted_iota(jnp.int32, (tq, tq), 1)
    past = key < qry
    suffix = jnp.where(qry >= key, -1.0, 0.0).astype(BF16)
    heads = range(2 * pairs)

    def step(kb, carry, boundary):
        start = pl.multiple_of(kb * tq, tq)
        zs = [_dot_nt(k_ref[pl.ds(start, tq), 128 * (h // 2):128 * (h // 2 + 1)], qs[h]) for h in heads]
        parts = []
        for z in zs:
            neg_abs = pltpu.bitcast(pltpu.bitcast(z, jnp.uint32) | jnp.uint32(0x80000000), F32)
            soft = jnp.maximum(z, 0.0) + jnp.log2(1.0 + jnp.exp2(neg_abs))
            if boundary:
                soft = jnp.where(past, soft, 0.0)
            parts.append(soft.astype(BF16))
        sums = [_dot(suffix, part) for part in parts]
        ws = []
        for h in heads:
            w = jnp.exp2(zs[h] + (carry[h][0] + sums[h]))
            ws.append((jnp.where(past, w, 0.0) if boundary else w).astype(BF16))
        return tuple((carry[h][0] + sums[h][0:1], carry[h][1]
                      + _dot(vt_ref[128 * (h // 2):128 * (h // 2 + 1), pl.ds(start, tq)], ws[h])) for h in heads)

    def live(carry):
        top = carry[0][0]
        for c, _ in carry[1:]:
            top = jnp.maximum(top, c)
        return (jnp.max(top) > SB_DONE).astype(jnp.int32)

    def body(state):
        t, _, carry = state
        carry = step(i - 1 - t, carry, False)
        return t + 1, live(carry), carry

    init = tuple((jnp.zeros((1, tq), F32), jnp.zeros((128, tq), F32)) for _ in range(2 * pairs))
    carry = step(i, init, True)
    _, _, carry = lax.while_loop(lambda st: (st[0] < i) & (st[1] > 0), body, (jnp.int32(0), live(carry), carry))
    rows = lax.broadcasted_iota(jnp.int32, (128, tq), 0)
    for p in range(pairs):
        ot = jnp.where(rows < HD, carry[2 * p][1], carry[2 * p + 1][1])
        o_ref[:, 128 * p:128 * (p + 1)] = ot.T.astype(BF16)


def _sb_attention(q, k, vt, B, T, tq=256, pairs=4):
    nq = T // tq
    w = 128 * pairs
    return pl.pallas_call(
        functools.partial(_sb_kernel, tq=tq, pairs=pairs),
        grid=(B, 4 // pairs, nq),
        in_specs=[pl.BlockSpec((tq, w), lambda b, p, i: (b * nq + i, p)),
                  pl.BlockSpec((T, w), lambda b, p, i: (b, p)),
                  pl.BlockSpec((None, w, T), lambda b, p, i: (b, p, 0))],
        out_specs=pl.BlockSpec((tq, w), lambda b, p, i: (b * nq + i, p)),
        out_shape=jax.ShapeDtypeStruct((B * T, 512), BF16),
        compiler_params=_cparams(("arbitrary", "arbitrary", "arbitrary")),
        name="sb_attention",
    )(q, k, vt)


def _router_weights(w_group, b_group, w_router, b_router):
    pad = 128 - N_EXP - N_GROUPS
    wr = jnp.concatenate([w_router, w_group, jnp.zeros((D, pad), F32)], axis=1).T
    hi = wr.astype(BF16)
    lo = (wr - hi.astype(F32)).astype(BF16)
    br = jnp.concatenate([b_router, b_group, jnp.zeros((pad,), F32)])[:, None]
    return jnp.concatenate([hi, lo], axis=0), br


def _ffn(h, oa, ob, wo, g, w_group, b_group, w_router, b_router, w_gate, w_up, w_down, layer):
    wr, br = _router_weights(w_group, b_group, w_router, b_router)
    hn, u, ri, rg, cnt = _outproj_router(h, oa, ob, wo.astype(BF16), g[None, :], wr, br)
    d1, d2, P, plan = _dispatch_plan(ri, cnt, h.shape[0])
    xs = _sc_scatter_rows(u, d1, d2, P)
    ys = _moe_experts(xs, plan, w_gate, w_up, w_down, layer)
    return hn, _sc_gather_rows(ys, jnp.concatenate([d1, d2])), rg


def kernel(x, norm_mix, norm_ffn, norm_final, rel_bias_table, w_in_even, g_mla_q, g_mla_kv, w_mla_qb, w_mla_kvb, swa_sinks, w_out_even, w_in_odd, w_out_odd, moe_w_group, moe_b_group, moe_w_router, moe_b_router, moe_w_gate, moe_w_up, moe_w_down):
    B, T, _ = x.shape
    h = x.reshape(B * T, D)
    tb_swa, tb_moba = _bias_tables(rel_bias_table)

    win, wqa, wqb, wk, wv = _even_weights(w_in_even[0], w_mla_qb[0], w_mla_kvb[0])
    q, k, v, qs8, ks, vst = _proj_even(h, norm_mix[0][None, :], win, g_mla_q[0][None, :], g_mla_kv[0][None, :],
                                       wqa, wqb, wk, wv, _rope_tables(T), T)
    oa = _mla_attention(q, k, v, B, T)
    ob = _swa_attention(qs8, ks, vst, swa_sinks[0], tb_swa, B, T)
    h, y12, rg = _ffn(h, oa, ob, w_out_even[0], norm_ffn[0], moe_w_group[0], moe_b_group[0], moe_w_router[0],
                      moe_b_router[0], moe_w_gate, moe_w_up, moe_w_down, 0)

    h, vct, vdt, qc, kc, qd, kd = _combine_proj_odd(h, y12, rg, norm_mix[1][None, :], *_odd_weights(w_in_odd[0]),
                                                   _block_indicator(T), T)
    oc = _moba_attention(qc, kc, vct, tb_moba, B, T)
    od = _sb_attention(qd, kd, vdt, B, T)
    h, y12, rg = _ffn(h, oc, od, w_out_odd[0], norm_ffn[1], moe_w_group[1], moe_b_group[1], moe_w_router[1],
                      moe_b_router[1], moe_w_gate, moe_w_up, moe_w_down, 1)
    out = _combine_final(h, y12, rg, norm_final[None, :])
    return out.reshape(B, T, D)
```

```python
import functools
import math

import numpy as np
import jax
import jax.numpy as jnp
from jax import lax
from jax.experimental import pallas as pl
from jax.experimental.pallas import tpu as pltpu
from jax.experimental.pallas import tpu_sc as plsc

F32 = jnp.float32
BF16 = jnp.bfloat16
NEG = -1e30
EPS = 1e-6

D = 1024
HD = 64
MLA_HEADS, MLA_QR, MLA_KVR, MLA_NOPE, MLA_ROPE, MLA_V = 8, 256, 128, 64, 32, 64
ROPE_THETA = 10000.0
SWA_HEADS, SWA_KV, SWA_WIN = 8, 2, 128
MOBA_L, MOBA_TOPK = 256, 3
REL_BUCKETS, REL_MAX = 32, 2048
N_GROUPS, EPG, N_EXP, D_EXP = 4, 8, 32, 256
BIAS_CONST_FROM = 1792
MOBA_TQ = 2 * MOBA_L
BIAS_PAD = MOBA_L
BIAS_ROWS = BIAS_PAD + BIAS_CONST_FROM + MOBA_TQ
SB_DONE = -160.0
TM = 1024
TM_E = 512
VMEM_LIMIT = 56 * 1024 * 1024
SC_CORES, SC_SUBCORES = 2, 16
SC_CHUNK = 128
HALF = D // 2
VT_ROWS = HD + 16
LOG2E = math.log2(math.e)


def _pack_halves(x):
    lo = pltpu.bitcast(x[:, :HALF].astype(BF16).astype(F32), jnp.uint32) >> 16
    hi = pltpu.bitcast(x[:, HALF:].astype(BF16).astype(F32), jnp.uint32) & jnp.uint32(0xFFFF0000)
    return pltpu.bitcast(lo | hi, jnp.int32)


def _unpack_halves(w):
    w = pltpu.bitcast(w, jnp.uint32)
    return pltpu.bitcast(w << 16, F32), pltpu.bitcast(w & jnp.uint32(0xFFFF0000), F32)


def _cparams(sem):
    return pltpu.CompilerParams(dimension_semantics=sem, vmem_limit_bytes=VMEM_LIMIT)


def _dot(a, b):
    return jnp.dot(a, b, preferred_element_type=F32)


def _dot_nt(a, b):
    return lax.dot_general(a, b, (((1,), (1,)), ((), ())), preferred_element_type=F32)


def _rms(x, g):
    return x * lax.rsqrt(jnp.mean(x * x, axis=-1, keepdims=True) + EPS) * g


def _values_t(vt):
    ones = jnp.ones((VT_ROWS - HD, vt.shape[1]), vt.dtype)
    return jnp.concatenate([blk for h in range(8) for blk in (vt[HD * h:HD * (h + 1)], ones)], axis=0)


def _spread_heads(x, lane_of):
    lane = lax.broadcasted_iota(jnp.int32, (x.shape[0], 128), 1)
    out = []
    for h in range(8):
        pair = x[:, 128 * (h // 2):128 * (h // 2 + 1)]
        if HD * (h % 2) != lane_of(h):
            pair = pltpu.roll(pair, HD, 1)
        out.append(jnp.where((lane >= lane_of(h)) & (lane < lane_of(h) + HD), pair, 0.0))
    return out


def _t5_bucket_np(dist):
    n = np.maximum(dist, 0)
    max_exact = REL_BUCKETS // 2
    nf = np.maximum(n, 1).astype(np.float32)
    large = max_exact + (np.log(nf / np.float32(max_exact)) / np.float32(math.log(REL_MAX / max_exact))
                         * np.float32(REL_BUCKETS - max_exact)).astype(np.int32)
    large = np.minimum(large, REL_BUCKETS - 1)
    return np.where(n < max_exact, n, large).astype(np.int32)


def _bias_kernel(table_ref, bucket_ref, out_ref, *, scale, ranges):
    b = bucket_ref[...]
    for m, (lo, hi) in enumerate(ranges):
        @pl.when(pl.program_id(0) == m)
        def _():
            for h in range(8):
                acc = jnp.full(b.shape, NEG if lo == REL_BUCKETS else table_ref[h, lo] * scale, F32)
                for k in range(lo + 1, min(hi, REL_BUCKETS - 1) + 1):
                    acc = jnp.where(b == k, table_ref[h, k] * scale, acc)
                if hi == REL_BUCKETS:
                    acc = jnp.where(b == REL_BUCKETS, NEG, acc)
                out_ref[h] = acc


def _bias_lookup(rel_table, dist, scale, name, masked=None):
    R, C = dist.shape
    buckets = _t5_bucket_np(dist)
    if masked is not None:
        buckets = np.where(masked, REL_BUCKETS, buckets).astype(np.int32)
    ranges = tuple((int(buckets[:, c:c + 256].min()), int(buckets[:, c:c + 256].max())) for c in range(0, C, 256))
    return pl.pallas_call(
        functools.partial(_bias_kernel, scale=scale, ranges=ranges),
        grid=(C // 256,),
        in_specs=[pl.BlockSpec(memory_space=pltpu.SMEM),
                  pl.BlockSpec((R, 256), lambda m: (0, m))],
        out_specs=pl.BlockSpec((8, R, 256), lambda m: (0, 0, m)),
        out_shape=jax.ShapeDtypeStruct((8, R, C), F32),
        compiler_params=_cparams(("arbitrary",)),
        name=name,
    )(rel_table.T, jnp.asarray(buckets))


def _bias_tables(rel_table):
    d = np.arange(256)[None, :] - np.arange(256)[:, None]
    swa = _bias_lookup(rel_table, d, LOG2E, "bias_swa", masked=(d < 0) | (d >= SWA_WIN))
    moba = _bias_lookup(rel_table, np.arange(BIAS_ROWS)[None, :] - BIAS_PAD - np.arange(MOBA_L)[:, None], LOG2E,
                        "bias_moba")
    return swa, moba


def _proj_even_kernel(h_ref, g_ref, win_ref, gq_ref, gkv_ref, wqa_ref, wqb_ref, wk_ref, wv_ref,
                      cq_ref, sq_ref, ck_ref, sk_ref,
                      q_ref, k_ref, v_ref, qs_ref, ks_ref, vst_ref):
    u = _rms(h_ref[...], g_ref[...]).astype(BF16)
    nq = _rms(_dot(u, win_ref[:, 0:256]), gq_ref[...]).astype(BF16)
    qa = _dot(nq, wqa_ref[...])
    qb = _dot(nq, wqb_ref[...])
    cq, sq = cq_ref[...], sq_ref[...]
    for hh in range(MLA_HEADS):
        sl = slice(128 * hh, 128 * (hh + 1))
        q_ref[:, sl] = (qa[:, sl] * cq + qb[:, sl] * sq).astype(BF16)
    nkv = _rms(_dot(u, win_ref[:, 256:384]), gkv_ref[...]).astype(BF16)
    kn = _dot(nkv, wk_ref[...])
    v_ref[...] = _values_t(_dot_nt(wv_ref[...], nkv)).astype(BF16)
    rest = _dot(u, win_ref[:, 896:1408])
    kr = rest[:, 128:256] * ck_ref[...] + rest[:, 256:384] * sk_ref[...]
    for hh in range(MLA_HEADS):
        sl = slice(128 * hh, 128 * (hh + 1))
        k_ref[:, sl] = (kn[:, sl] + kr).astype(BF16)
    qs = _spread_heads(_dot(u, win_ref[:, 384:896]), lambda hq: HD * (hq // (SWA_HEADS // SWA_KV)))
    for hq in range(SWA_HEADS):
        qs_ref[:, 128 * hq:128 * (hq + 1)] = qs[hq].astype(BF16)
    ks_ref[...] = rest[:, 0:128].astype(BF16)
    vst_ref[...] = rest[:, 384:512].T.astype(BF16)


def _even_weights_kernel(wt_ref, wqb_in_ref, wkvb_ref, win_ref, wqa_ref, wqb_ref, wk_ref, wv_ref):
    o_kr = MLA_QR + MLA_KVR
    o_qs = o_kr + MLA_ROPE
    o_ks = o_qs + SWA_HEADS * HD
    o_vs = o_ks + SWA_KV * HD
    half = MLA_ROPE // 2
    wt = wt_ref[...]
    zero_rows = lambda n: jnp.zeros((n, D), F32)
    zero = lambda a, n: jnp.zeros((a.shape[0], n), F32)
    win_ref[...] = jnp.concatenate([
        wt[0:o_kr], wt[o_qs:o_ks] * (HD ** -0.5 * LOG2E), wt[o_ks:o_vs],
        zero_rows(64), wt[o_kr:o_qs], zero_rows(32),
        zero_rows(64), wt[o_kr + half:o_qs], wt[o_kr:o_kr + half], zero_rows(32),
        wt[o_vs:o_vs + SWA_KV * HD]], axis=0).T.astype(BF16)
    wq, wkv = wqb_in_ref[...], wkvb_ref[...]
    hq, hkv = MLA_NOPE + MLA_ROPE, MLA_NOPE + MLA_V
    qa, qb, kn, vt = [], [], [], []
    for h in range(MLA_HEADS):
        qa += [wq[:, hq * h:hq * (h + 1)], zero(wq, 32)]
        qb += [zero(wq, MLA_NOPE), wq[:, hq * h + MLA_NOPE + half:hq * (h + 1)],
               wq[:, hq * h + MLA_NOPE:hq * h + MLA_NOPE + half], zero(wq, 32)]
        kn += [wkv[:, hkv * h:hkv * h + MLA_NOPE], zero(wkv, MLA_V)]
        vt += [wkv[:, hkv * h + MLA_NOPE:hkv * (h + 1)]]
    wqa_ref[...] = jnp.concatenate(qa, axis=1).astype(BF16)
    wqb_ref[...] = jnp.concatenate(qb, axis=1).astype(BF16)
    wk_ref[...] = jnp.concatenate(kn, axis=1).astype(BF16)
    wv_ref[...] = jnp.concatenate(vt, axis=1).T.astype(BF16)


def _even_weights(w_in, w_qb, w_kvb):
    shapes =[(D, 1408), (MLA_QR, 128 * MLA_HEADS), (MLA_QR, 128 * MLA_HEADS), (MLA_KVR, 128 * MLA_HEADS),
              (MLA_V * MLA_HEADS, MLA_KVR)]
    return pl.pallas_call(
        _even_weights_kernel,
        out_shape=[jax.ShapeDtypeStruct(s, BF16) for s in shapes],
        compiler_params=_cparams(None),
        name="even_weights",
    )(w_in.T, w_qb, w_kvb)


def _rope_tables(T):
    f32 = np.float32
    freqs = f32(ROPE_THETA) ** (-np.arange(0, MLA_ROPE, 2, dtype=f32) / f32(MLA_ROPE))
    ang = np.arange(T, dtype=f32)[:, None] * freqs[None, :]
    cos, sin = np.cos(ang), np.sin(ang)
    z = lambda n: np.zeros((T, n), f32)
    scale = f32((MLA_NOPE + MLA_ROPE) ** -0.5 * LOG2E)
    ck = np.concatenate([z(64), cos, cos, z(32)], axis=1)
    sk = np.concatenate([z(64), -sin, sin, z(32)], axis=1)
    cq = np.concatenate([np.ones((T, 64), f32), cos, cos, z(32)], axis=1) * scale
    sq = sk * scale
    return tuple(jnp.asarray(t, F32) for t in (cq, sq, ck, sk))


def _proj_even(h, g, win, gq, gkv, wqa, wqb, wk, wv, tables, T):
    N = h.shape[0]
    nt = T // TM
    full = lambda a: pl.BlockSpec(a.shape, lambda i: (0,) * a.ndim)
    tab = pl.BlockSpec((TM, 128), lambda i: (i % nt, 0))
    row = lambda w: pl.BlockSpec((TM, w), lambda i: (i, 0))
    outs = (1024, 1024, -8 * VT_ROWS, 1024, 128, -128)
    spec = lambda w: row(w) if w > 0 else pl.BlockSpec((None, -w, TM), lambda i: (i // nt, 0, i % nt))
    shape = lambda w: jax.ShapeDtypeStruct((N, w) if w > 0 else (N // T, -w, T), BF16)
    return pl.pallas_call(
        _proj_even_kernel,
        grid=(N // TM,),
        in_specs=[row(D), full(g), full(win), full(gq), full(gkv), full(wqa), full(wqb), full(wk), full(wv),
                  tab, tab, tab, tab],
        out_specs=[spec(w) for w in outs],
        out_shape=[shape(w) for w in outs],
        compiler_params=_cparams(("arbitrary",)),
        name="proj_even",
    )(h, g, win, gq, gkv, wqa, wqb, wk, wv, *tables)


def _finish_t(accs):
    ot = jnp.concatenate([acc[:HD] / acc[HD:HD + 1] for acc in accs], axis=0)
    return ot.T.astype(BF16)


def _causal_attention_pair_t(ia, nq, q_of, k_ref, vt_ref, o_ref, s_ref, smax_ref, m_ref, acc_ref, *, tq,
                             bias_of=None):
    nh = s_ref.shape[1]
    first_query = (ia * tq, (nq - 1 - ia) * tq)
    n_items = nq + 1
    assert n_items % 2 == 1
    key = lax.broadcasted_iota(jnp.int32, (tq, tq), 0)
    qry = lax.broadcasted_iota(jnp.int32, (tq, tq), 1)

    def item(p):
        if isinstance(p, int) and p < 2:
            return p, first_query[p], first_query[p]
        t = jnp.where(p - 2 >= ia, 1, 0)
        return t, jnp.where(t == 1, first_query[1], first_query[0]), (p - 2 - t * ia) * tq

    def scores_to(slot, p):
        t, q0, start = item(p)
        start = pl.multiple_of(start, tq)
        for j in range(nh):
            st = _dot_nt(k_ref[pl.ds(start, tq), 128 * j:128 * (j + 1)], q_of(t, q0, j))
            if bias_of is not None:
                st = st + bias_of(start, q0, j)
            if isinstance(p, int) and p < 2:
                st = jnp.where(key <= qry, st, NEG)
            s_ref[slot, j] = st
            smax_ref[slot, j] = jnp.max(st, axis=0, keepdims=True)

    def update_from(slot, p):
        t, _, start = item(p)
        start = pl.multiple_of(start, tq)
        m_new = [jnp.maximum(m_ref[t, j], smax_ref[slot, j]) for j in range(nh)]
        pts = [jnp.exp2(s_ref[slot, j] - m_new[j]).astype(BF16) for j in range(nh)]
        for j in range(nh):
            acc_ref[t, j] = (jnp.exp2(m_ref[t, j] - m_new[j]) * acc_ref[t, j]
                             + _dot(vt_ref[VT_ROWS * j:VT_ROWS * (j + 1), pl.ds(start, tq)], pts[j]))
            m_ref[t, j] = m_new[j]

    def pair(k):
        scores_to(1, k + 1)
        update_from(0, k)
        scores_to(0, k + 2)
        update_from(1, k + 1)

    def quad(c, _):
        pair(2 + 4 * c)
        pair(4 + 4 * c)
        return 0

    m_ref[...] = jnp.full(m_ref.shape, NEG, F32)
    acc_ref[...] = jnp.zeros(acc_ref.shape, F32)
    scores_to(0, 0)
    pair(0)
    later_pairs = (n_items - 1) // 2 - 1
    lax.fori_loop(0, later_pairs // 2, quad, 0)
    if later_pairs % 2:
        pair(n_items - 3)
    update_from(0, n_items - 1)
    for t in range(2):
        o_ref[pl.ds(pl.multiple_of(first_query[t], tq), tq), :] = _finish_t([acc_ref[t, j] for j in range(nh)])


def _attention_t_scratch(tq):
    return [pltpu.VMEM((2, 2, tq, tq), F32), pltpu.VMEM((2, 2, 1, tq), F32), pltpu.VMEM((2, 2, 1, tq), F32),
            pltpu.VMEM((2, 2, VT_ROWS, tq), F32)]


def _mla_kernel(q_ref, k_ref, vt_ref, o_ref, s_ref, smax_ref, m_ref, acc_ref, *, tq, nq):
    def q_of(t, q0, j):
        return q_ref[pl.ds(pl.multiple_of(q0, tq), tq), 128 * j:128 * (j + 1)]

    _causal_attention_pair_t(pl.program_id(2), nq, q_of, k_ref, vt_ref, o_ref, s_ref, smax_ref, m_ref, acc_ref,
                             tq=tq)


def _mla_attention(q, k, vt, B, T, tq=512):
    nq = T // tq
    return pl.pallas_call(
        functools.partial(_mla_kernel, tq=tq, nq=nq),
        grid=(B, MLA_HEADS // 2, nq // 2),
        in_specs=[pl.BlockSpec((T, 256), lambda b, p, i: (b, p)),
                  pl.BlockSpec((T, 256), lambda b, p, i: (b, p)),
                  pl.BlockSpec((None, 2 * VT_ROWS, T), lambda b, p, i: (b, p, 0))],
        out_specs=pl.BlockSpec((T, 128), lambda b, p, i: (b, p)),
        out_shape=jax.ShapeDtypeStruct((B * T, MLA_HEADS * MLA_V), BF16),
        scratch_shapes=_attention_t_scratch(tq),
        compiler_params=_cparams(("arbitrary", "arbitrary", "arbitrary")),
        name="mla_attention",
    )(q, k, vt)


def _swa_kernel(sink_ref, q_ref, k_ref, vt_ref, tb_ref, o_ref, *, tq):
    i = pl.program_id(1)
    G = SWA_HEADS // SWA_KV
    W = 2 * SWA_WIN
    head = lax.broadcasted_iota(jnp.int32, (1, G * 128), 1) >> 7
    sinks = []
    for c in range(SWA_KV):
        sink = jnp.zeros((1, G * 128), F32)
        for g in range(G):
            sink = jnp.where(head == g, sink_ref[G * c + g], sink)
        sinks.append(sink)
    scores, values = [], []
    for r in range(tq // 128):
        qstart = i * tq + 128 * r
        kstart = pl.multiple_of(jnp.maximum(qstart - SWA_WIN, 0), 128)
        off = pl.multiple_of(qstart - kstart, 128)
        kw = k_ref[pl.ds(kstart, W), :]
        for c in range(SWA_KV):
            qg = jnp.concatenate([q_ref[128 * r:128 * (r + 1), 128 * (G * c + g):128 * (G * c + g + 1)]
                                  for g in range(G)], axis=0)
            bias = jnp.concatenate([tb_ref[G * c + g, :, pl.ds(off, 128)] for g in range(G)], axis=1)
            scores.append(_dot_nt(kw, qg) + bias)
            values.append(vt_ref[HD * c:HD * (c + 1), pl.ds(kstart, W)])
    probs = []
    for n, s in enumerate(scores):
        sink = sinks[n % SWA_KV]
        m = jnp.maximum(jnp.max(s, axis=0, keepdims=True), sink)
        p = jnp.exp2(s - m)
        probs.append((p.astype(BF16), jnp.sum(p, axis=0, keepdims=True) + jnp.exp2(sink - m)))
    for r in range(tq // 128):
        outs = []
        for c in range(SWA_KV):
            p, denom = probs[SWA_KV * r + c]
            ot = _dot(values[SWA_KV * r + c], p) / denom
            outs += [ot[:, 128 * g:128 * (g + 1)] for g in range(G)]
        o_ref[128 * r:128 * (r + 1), :] = jnp.concatenate(outs, axis=0).T.astype(BF16)


def _swa_attention(q8, k, vt, sinks, tb, B, T, tq=512):
    nq = T // tq
    return pl.pallas_call(
        functools.partial(_swa_kernel, tq=tq),
        grid=(B, nq),
        in_specs=[pl.BlockSpec(memory_space=pltpu.SMEM),
                  pl.BlockSpec((tq, 1024), lambda b, i: (b * nq + i, 0)),
                  pl.BlockSpec((T, 128), lambda b, i: (b, 0)),
                  pl.BlockSpec((None, 128, T), lambda b, i: (b, 0, 0)),
                  pl.BlockSpec((8, 256, 256), lambda b, i: (0, 0, 0))],
        out_specs=pl.BlockSpec((tq, 512), lambda b, i: (b * nq + i, 0)),
        out_shape=jax.ShapeDtypeStruct((B * T, SWA_HEADS * HD), BF16),
        compiler_params=_cparams(("arbitrary", "arbitrary")),
        name="swa_attention",
    )(sinks * LOG2E, q8, k, vt, tb)


def _outproj_router_kernel(h_ref, oa_ref, ob_ref, wo_ref, g_ref, wr_ref, br_ref,
                           hn_ref, u_ref, ri_ref, rg_ref, cnt_ref, base_ref):
    i = pl.program_id(0)

    @pl.when(i == 0)
    def _():
        base_ref[...] = jnp.zeros_like(base_ref)

    hn = h_ref[...] + _dot(oa_ref[...], wo_ref[0:512, :]) + _dot(ob_ref[...], wo_ref[512:1024, :])
    hn_ref[...] = hn
    u = _rms(hn, g_ref[...])
    u_ref[...] = _pack_halves(u)
    u_hi = u.astype(BF16)
    u_lo = (u - u_hi.astype(F32)).astype(BF16)
    both = _dot_nt(wr_ref[...], u_hi)
    logits = both[:128] + both[128:] + _dot_nt(wr_ref[0:128, :], u_lo) + br_ref[...]
    tm = logits.shape[1]
    sub = lax.broadcasted_iota(jnp.int32, (8, tm), 0).astype(F32)
    big = 1e6
    isg = sub < N_GROUPS
    gl = jnp.where(isg, logits[N_EXP:N_EXP + 8], NEG)
    gmax = jnp.max(gl, axis=0, keepdims=True)
    gsel = jnp.min(jnp.where(gl == gmax, sub, big), axis=0, keepdims=True)
    g_gate = 1.0 / jnp.sum(jnp.where(isg, jnp.exp(gl - gmax), 0.0), axis=0, keepdims=True)
    el = jnp.zeros((EPG, tm), F32)
    for grp in range(N_GROUPS):
        el = jnp.where(gsel == grp, logits[EPG * grp:EPG * (grp + 1)], el)
    m1 = jnp.max(el, axis=0, keepdims=True)
    i1 = jnp.min(jnp.where(el == m1, sub, big), axis=0, keepdims=True)
    el2 = jnp.where(sub == i1, NEG, el)
    m2 = jnp.max(el2, axis=0, keepdims=True)
    i2 = jnp.min(jnp.where(el2 == m2, sub, big), axis=0, keepdims=True)
    e1 = gsel * EPG + i1
    e2 = gsel * EPG + i2
    r = jnp.exp(m2 - m1)
    gate1 = g_gate / (1.0 + r)
    gate2 = g_gate * r / (1.0 + r)
    rows = lax.broadcasted_iota(jnp.int32, (128, tm), 0).astype(F32)
    oh1 = (rows == e1).astype(F32)
    oh2 = (rows == e2).astype(F32)
    oh = oh1 + oh2
    t_row = lax.broadcasted_iota(jnp.int32, (tm, tm), 0)
    t_col = lax.broadcasted_iota(jnp.int32, (tm, tm), 1)
    earlier = jnp.where(t_row < t_col, 1.0, 0.0).astype(BF16)
    prefix = _dot(oh.astype(BF16), earlier) + base_ref[:, 0:1]
    rank1 = jnp.sum(oh1 * prefix, axis=0, keepdims=True)
    rank2 = jnp.sum(oh2 * prefix, axis=0, keepdims=True)
    base_ref[...] = base_ref[...] + jnp.sum(oh, axis=1, keepdims=True)
    cnt_ref[...] = base_ref[...]
    ri_ref[...] = jnp.where(sub == 0, e1, jnp.where(sub == 1, e2, jnp.where(sub == 2, rank1,
                            jnp.where(sub == 3, rank2, 0.0))))
    rg_ref[...] = jnp.where(rows == 0, gate1, jnp.where(rows == 1, gate2, 0.0)).T


def _outproj_router(h, oa, ob, wo, g, wr, br):
    N = h.shape[0]
    full = lambda a: pl.BlockSpec(a.shape, lambda i: (0,) * a.ndim)
    row = lambda w: pl.BlockSpec((TM, w), lambda i: (i, 0))
    return pl.pallas_call(
        _outproj_router_kernel,
        grid=(N // TM,),
        in_specs=[row(D), row(512), row(512), full(wo), full(g), full(wr), full(br)],
        out_specs=[row(D), row(HALF), pl.BlockSpec((8, TM), lambda i: (0, i)), row(128),
                   pl.BlockSpec((128, 128), lambda i: (0, 0))],
        out_shape=[jax.ShapeDtypeStruct((N, D), F32), jax.ShapeDtypeStruct((N, HALF), jnp.int32),
                   jax.ShapeDtypeStruct((8, N), F32), jax.ShapeDtypeStruct((N, 128), F32),
                   jax.ShapeDtypeStruct((128, 128), F32)],
        scratch_shapes=[pltpu.VMEM((128, 128), F32)],
        compiler_params=_cparams(("arbitrary",)),
        name="outproj_router",
    )(h, oa, ob, wo, g, wr, br)


def _moe_kernel(te_ref, tv_ref, run_ref, nx_ref, x_ref, wg_hbm, wu_hbm, wd_hbm, o_ref,
                wg_buf, wu_buf, wd_buf, sems, wgu_s, wd_s, *, layer):
    i = pl.program_id(0)
    valid = tv_ref[i]

    def weight_copies(e, slot):
        return (pltpu.make_async_copy(wg_hbm.at[layer, e], wg_buf.at[slot], sems.at[slot, 0]),
                pltpu.make_async_copy(wu_hbm.at[layer, e], wu_buf.at[slot], sems.at[slot, 1]),
                pltpu.make_async_copy(wd_hbm.at[layer, e], wd_buf.at[slot], sems.at[slot, 2]))

    @pl.when(run_ref[i] > 0)
    def _():
        slot = run_ref[i] - 1

        @pl.when(i == 0)
        def _():
            for c in weight_copies(te_ref[i], slot):
                c.start()

        for c in weight_copies(te_ref[i], slot):
            c.wait()
        wgu_s[:, :D_EXP] = wg_buf[slot].astype(BF16)
        wgu_s[:, D_EXP:] = wu_buf[slot].astype(BF16)
        wd_s[...] = wd_buf[slot].astype(BF16)

        @pl.when(nx_ref[i] >= 0)
        def _():
            for c in weight_copies(nx_ref[i], 1 - slot):
                c.start()

    @pl.when(valid > 0)
    def _():
        rows = lax.broadcasted_iota(jnp.int32, x_ref.shape, 0)
        lo, hi = _unpack_halves(jnp.where(rows < valid, x_ref[...], 0))
        hgu = _dot(lo.astype(BF16), wgu_s[:HALF, :]) + _dot(hi.astype(BF16), wgu_s[HALF:, :])
        hg, hu = hgu[:, :D_EXP], hgu[:, D_EXP:]
        a = hg * (1.0 / (1.0 + jnp.exp(-hg))) * hu
        o_ref[...] = _pack_halves(_dot(a.astype(BF16), wd_s[...]))

    @pl.when(valid <= 0)
    def _():
        o_ref[...] = jnp.zeros_like(o_ref)


def _moe_experts(xs, plan, w_gate, w_up, w_down, layer):
    P = xs.shape[0]
    hbm = pl.BlockSpec(memory_space=pl.ANY)
    grid_spec = pltpu.PrefetchScalarGridSpec(
        num_scalar_prefetch=4,
        grid=(P // TM_E,),
        in_specs=[pl.BlockSpec((TM_E, HALF), lambda i, *_: (i, 0)), hbm, hbm, hbm],
        out_specs=pl.BlockSpec((TM_E, HALF), lambda i, *_: (i, 0)),
        scratch_shapes=[pltpu.VMEM((2, D, D_EXP), F32), pltpu.VMEM((2, D, D_EXP), F32),
                        pltpu.VMEM((2, D_EXP, D), F32), pltpu.SemaphoreType.DMA((2, 3)),
                        pltpu.VMEM((D, 2 * D_EXP), BF16), pltpu.VMEM((D_EXP, D), BF16)],
    )
    return pl.pallas_call(
        functools.partial(_moe_kernel, layer=layer),
        grid_spec=grid_spec,
        out_shape=jax.ShapeDtypeStruct((P, HALF), jnp.int32),
        compiler_params=_cparams(("arbitrary",)),
        name="moe_experts",
    )(*plan, xs, w_gate, w_up, w_down)


def _sc_mesh():
    return plsc.VectorSubcoreMesh(core_axis_name="c", subcore_axis_name="s",
                                  num_cores=SC_CORES, num_subcores=SC_SUBCORES)


def _sc_scatter_rows(src, d1, d2, P):
    N, W = src.shape
    per_w = N // (SC_CORES * SC_SUBCORES)

    @functools.partial(pl.kernel, mesh=_sc_mesh(), out_type=jax.ShapeDtypeStruct((P, W), src.dtype),
                       scratch_types=[pltpu.VMEM((SC_CHUNK,), jnp.int32), pltpu.VMEM((SC_CHUNK, W), src.dtype)],
                       name="sc_scatter_rows")
    def k(src_hbm, d1_hbm, d2_hbm, out_hbm, idx_v, rows_v):
        wid = lax.axis_index("s") * SC_CORES + lax.axis_index("c")

        @pl.loop(0, per_w // SC_CHUNK)
        def _(c):
            off = wid * per_w + c * SC_CHUNK
            pltpu.sync_copy(src_hbm.at[pl.ds(off, SC_CHUNK)], rows_v)
            pltpu.sync_copy(d1_hbm.at[pl.ds(off, SC_CHUNK)], idx_v)
            pltpu.sync_copy(rows_v, out_hbm.at[idx_v])
            pltpu.sync_copy(d2_hbm.at[pl.ds(off, SC_CHUNK)], idx_v)
            pltpu.sync_copy(rows_v, out_hbm.at[idx_v])

    return k(src, d1, d2)


def _sc_gather_rows(table, idx):
    B, W = idx.shape[0], table.shape[1]
    per_w = B // (SC_CORES * SC_SUBCORES)

    @functools.partial(pl.kernel, mesh=_sc_mesh(), out_type=jax.ShapeDtypeStruct((B, W), table.dtype),
                       scratch_types=[pltpu.VMEM((SC_CHUNK,), jnp.int32), pltpu.VMEM((SC_CHUNK, W), table.dtype)],
                       name="sc_gather_rows")
    def k(table_hbm, idx_hbm, out_hbm, idx_v, rows_v):
        wid = lax.axis_index("s") * SC_CORES + lax.axis_index("c")

        @pl.loop(0, per_w // SC_CHUNK)
        def _(c):
            off = wid * per_w + c * SC_CHUNK
            pltpu.sync_copy(idx_hbm.at[pl.ds(off, SC_CHUNK)], idx_v)
            pltpu.sync_copy(table_hbm.at[idx_v], rows_v)
            pltpu.sync_copy(rows_v, out_hbm.at[pl.ds(off, SC_CHUNK)])

    return k(table, idx)


def _dest_kernel(offs_ref, ri_ref, d_ref):
    ri = ri_ref[...]
    base = jnp.zeros(ri.shape, F32)
    for e in range(N_EXP):
        base = jnp.where(ri == e, offs_ref[e].astype(F32), base)
    d_ref[...] = (base + pltpu.roll(ri, 6, 0)).astype(jnp.int32)


def _dispatch_plan(ri, cnt, N):
    counts = cnt[:N_EXP, 0].astype(jnp.int32)
    padded = ((counts + TM_E - 1) // TM_E) * TM_E
    ends = jnp.cumsum(padded)
    offs = ends - padded
    take = lambda t, i: t.at[i].get(mode="promise_in_bounds")
    dest = pl.pallas_call(
        _dest_kernel,
        in_specs=[pl.BlockSpec(memory_space=pltpu.SMEM), pl.BlockSpec((8, N), lambda: (0, 0))],
        out_specs=pl.BlockSpec((8, N), lambda: (0, 0)),
        out_shape=jax.ShapeDtypeStruct((8, N), jnp.int32),
        name="dispatch_rows",
    )(offs, ri)
    d1, d2 = dest[0], dest[1]
    P = 2 * N + N_EXP * TM_E
    tile_start = jnp.arange(P // TM_E, dtype=jnp.int32) * TM_E
    tile_expert = jnp.minimum(jnp.sum((ends[None, :] <= tile_start[:, None]).astype(jnp.int32), axis=1), N_EXP - 1)
    tile_valid = jnp.clip(take(offs + counts, tile_expert) - tile_start, 0, TM_E)
    n_tiles = P // TM_E
    tile = jnp.arange(n_tiles, dtype=jnp.int32)
    starts = (tile_valid > 0) & ((tile == 0) | (tile_expert != jnp.roll(tile_expert, 1)))
    slot = (jnp.cumsum(starts.astype(jnp.int32)) - 1) % 2
    run_code = jnp.where(starts, slot + 1, 0).astype(jnp.int32)
    start_at = jnp.where(starts, tile, n_tiles)
    next_start = jnp.concatenate([lax.cummin(start_at, reverse=True)[1:], jnp.full((1,), n_tiles, jnp.int32)])
    next_expert = jnp.where(next_start < n_tiles, take(tile_expert, jnp.minimum(next_start, n_tiles - 1)), -1)
    return d1, d2, P, (tile_expert, tile_valid, run_code, next_expert.astype(jnp.int32))


def _combine(h_ref, y1_ref, y2_ref, rg_ref):
    rg = rg_ref[...]
    lo1, hi1 = _unpack_halves(y1_ref[...])
    lo2, hi2 = _unpack_halves(y2_ref[...])
    g1, g2 = rg[:, 0:1], rg[:, 1:2]
    return h_ref[...] + jnp.concatenate([g1 * lo1 + g2 * lo2, g1 * hi1 + g2 * hi2], axis=1)


ODD_WIDTHS = (1024, 1024, 512, 512)


def _combine_proj_odd_kernel(h_ref, y1_ref, y2_ref, rg_ref, g_ref, win_ref, wvt_ref, ind_ref, hn_ref, vt_ref,
                             vdt_ref, *out_refs):
    hn = _combine(h_ref, y1_ref, y2_ref, rg_ref)
    hn_ref[...] = hn
    u = _rms(hn, g_ref[...]).astype(BF16)
    vt_ref[...] = _values_t(_dot_nt(wvt_ref[0:512, :], u)).astype(BF16)
    vdt_ref[...] = _dot_nt(wvt_ref[512:1024, :], u).astype(BF16)
    qc_ref, kc_ref, qd_ref, kd_ref = out_refs
    qc = _spread_heads(_dot(u, win_ref[:, 0:512]), lambda h: 0)
    kc = _spread_heads(_dot(u, win_ref[:, 512:1024]), lambda h: 0)
    for hh in range(8):
        qc_ref[:, 128 * hh:128 * (hh + 1)] = qc[hh].astype(BF16)
        kc_ref[:, 128 * hh:128 * (hh + 1)] = (kc[hh] + ind_ref[...]).astype(BF16)
    qd_ref[...] = _dot(u, win_ref[:, 1024:1536]).astype(BF16)
    kd_ref[...] = _dot(u, win_ref[:, 1536:2048]).astype(BF16)


def _odd_weights_kernel(w_ref, win_ref, wvt_ref):
    win_ref[:, 0:512] = (w_ref[:, 0:512] * (HD ** -0.5 * LOG2E)).astype(BF16)
    win_ref[:, 512:1024] = w_ref[:, 512:1024].astype(BF16)
    wvt_ref[...] = w_ref[:, 1024:1536].T.astype(BF16)


def _odd_weights(w_in):
    return pl.pallas_call(
        _odd_weights_kernel,
        grid=(2,),
        in_specs=[pl.BlockSpec((D, 1536), lambda i: (0, i))],
        out_specs=[pl.BlockSpec((D, 1024), lambda i: (0, i)), pl.BlockSpec((512, D), lambda i: (i, 0))],
        out_shape=[jax.ShapeDtypeStruct((D, 2048), BF16), jax.ShapeDtypeStruct((1024, D), BF16)],
        compiler_params=_cparams(("arbitrary",)),
        name="odd_weights",
    )(w_in)


def _block_indicator(T):
    assert T // MOBA_L <= 64
    ind = np.zeros((T, 128), np.float32)
    ind[np.arange(T), 64 + np.arange(T) // MOBA_L] = 1.0
    return jnp.asarray(ind)


def _y_specs(N):
    return [pl.BlockSpec((TM, HALF), lambda i: (i, 0)), pl.BlockSpec((TM, HALF), lambda i: (i + N // TM, 0))]


def _combine_proj_odd(h, y12, rg, g, win, wvt, ind, T):
    N = h.shape[0]
    nt = T // TM
    full = lambda a: pl.BlockSpec(a.shape, lambda i: (0,) * a.ndim)
    row = lambda w: pl.BlockSpec((TM, w), lambda i: (i, 0))
    return pl.pallas_call(
        _combine_proj_odd_kernel,
        grid=(N // TM,),
        in_specs=[row(D)] + _y_specs(N) + [row(128), full(g), full(win), full(wvt),
                  pl.BlockSpec((TM, 128), lambda i: (i % nt, 0))],
        out_specs=[row(D), pl.BlockSpec((None, 8 * VT_ROWS, TM), lambda i: (i // nt, 0, i % nt)),
                   pl.BlockSpec((None, 512, TM), lambda i: (i // nt, 0, i % nt))]
        + [row(w) for w in ODD_WIDTHS],
        out_shape=[jax.ShapeDtypeStruct((N, D), F32), jax.ShapeDtypeStruct((N // T, 8 * VT_ROWS, T), BF16),
                   jax.ShapeDtypeStruct((N // T, 512, T), BF16)]
        + [jax.ShapeDtypeStruct((N, w), BF16) for w in ODD_WIDTHS],
        compiler_params=_cparams(("arbitrary",)),
        name="combine_proj_odd",
    )(h, y12, y12, rg, g, win, wvt, ind)


def _combine_final_kernel(h_ref, y1_ref, y2_ref, rg_ref, g_ref, o_ref):
    o_ref[...] = _rms(_combine(h_ref, y1_ref, y2_ref, rg_ref), g_ref[...])


def _combine_final(h, y12, rg, g):
    N = h.shape[0]
    row = lambda w: pl.BlockSpec((TM, w), lambda i: (i, 0))
    return pl.pallas_call(
        _combine_final_kernel,
        grid=(N // TM,),
        in_specs=[row(D)] + _y_specs(N) + [row(128), pl.BlockSpec((1, D), lambda i: (0, 0))],
        out_specs=row(D),
        out_shape=jax.ShapeDtypeStruct((N, D), F32),
        compiler_params=_cparams(("arbitrary",)),
        name="combine_final",
    )(h, y12, y12, rg, g)


def _moba_kernel(q_ref, k_ref, vt_ref, tb_ref, o_ref, kmean_ref, ksplit_ref, qa_ref, s_ref, smax_ref, m_ref, acc_ref,
                 *, nkb, nq):
    ia = pl.program_id(2)
    L, tq = MOBA_L, MOBA_TQ

    @pl.when(ia == 0)
    def _():
        kmean_ref[...] = jnp.zeros_like(kmean_ref)
        for n in range(nkb):
            kmean_ref[64 + n:65 + n, :] = jnp.sum(k_ref[n * L:(n + 1) * L, :].astype(F32), axis=0,
                                                  keepdims=True) * (1.0 / L)
        hi = kmean_ref[...].astype(BF16)
        ksplit_ref[0] = hi
        ksplit_ref[1] = (kmean_ref[...] - hi.astype(F32)).astype(BF16)

    nb = -(-nkb // 8) * 8
    lane = lax.broadcasted_iota(jnp.int32, (tq, 128), 1)
    blk = lax.broadcasted_iota(jnp.int32, (nb, tq), 0)
    half = jnp.where(lax.broadcasted_iota(jnp.int32, (nb, tq), 1) >= L, 1, 0)
    blk_f = blk.astype(F32)
    for t, tile in enumerate((ia, nq - 1 - ia)):
        own = 2 * tile + half
        for j in range(2):
            qj = q_ref[pl.ds(pl.multiple_of(tile * tq, tq), tq), 128 * j:128 * (j + 1)]
            gsc = (_dot_nt(ksplit_ref[0, 64:64 + nb, 128 * j:128 * (j + 1)], qj)
                   + _dot_nt(ksplit_ref[1, 64:64 + nb, 128 * j:128 * (j + 1)], qj))
            g = jnp.where(blk < own, gsc, NEG)
            allowed = jnp.where(blk == own, 1.0, 0.0)
            for _ in range(MOBA_TOPK):
                mx = jnp.max(g, axis=0, keepdims=True)
                idx = jnp.min(jnp.where(g == mx, blk_f, 1e6), axis=0, keepdims=True)
                pick = blk_f == idx
                allowed = jnp.maximum(allowed, jnp.where(pick, jnp.where(mx > 0.5 * NEG, 1.0, 0.0), 0.0))
                g = jnp.where(pick, 2.0 * NEG, g)
            mask = jnp.concatenate([jnp.zeros((64, tq), F32), jnp.where(allowed > 0.5, 0.0, NEG),
                                    jnp.zeros((64 - nb, tq), F32)], axis=0)
            qa_ref[t, j] = jnp.where(lane < HD, qj, mask.T.astype(BF16))

    def bias_of(start, q0, j):
        parts = []
        for c in range(tq // L):
            off = jnp.minimum(q0 - start - c * L, BIAS_CONST_FROM) + BIAS_PAD
            parts.append(tb_ref[j, :, pl.ds(pl.multiple_of(off, L), tq)])
        return jnp.concatenate(parts, axis=0)

    _causal_attention_pair_t(ia, nq, lambda t, q0, j: qa_ref[t, j], k_ref, vt_ref, o_ref, s_ref, smax_ref, m_ref,
                             acc_ref, tq=tq, bias_of=bias_of)


def _moba_attention(q, k, vt, tb, B, T):
    tq = MOBA_TQ
    nq = T // tq
    return pl.pallas_call(
        functools.partial(_moba_kernel, nkb=T // MOBA_L, nq=nq),
        grid=(B, 4, nq // 2),
        in_specs=[pl.BlockSpec((T, 256), lambda b, p, i: (b, p)),
                  pl.BlockSpec((T, 256), lambda b, p, i: (b, p)),
                  pl.BlockSpec((None, 2 * VT_ROWS, T), lambda b, p, i: (b, p, 0)),
                  pl.BlockSpec((2, MOBA_L, BIAS_ROWS), lambda b, p, i: (p, 0, 0))],
        out_specs=pl.BlockSpec((T, 128), lambda b, p, i: (b, p)),
        out_shape=jax.ShapeDtypeStruct((B * T, 512), BF16),
        scratch_shapes=[pltpu.VMEM((128, 256), F32), pltpu.VMEM((2, 128, 256), BF16),
                        pltpu.VMEM((2, 2, tq, 128), BF16)]
        + _attention_t_scratch(tq),
        compiler_params=_cparams(("arbitrary", "arbitrary", "arbitrary")),
        name="moba_attention",
    )(q, k, vt, tb)


def _sb_kernel(q_ref, k_ref, vt_ref, o_ref, *, tq, pairs):
    i = pl.program_id(2)
    lane = lax.broadcasted_iota(jnp.int32, (tq, 128), 1)
    qs = []
    for p in range(pairs):
        q_pair = q_ref[:, 128 * p:128 * (p + 1)]
        qs += [jnp.where(lane < HD, q_pair, 0).astype(BF16), jnp.where(lane >= HD, q_pair, 0).astype(BF16)]
    key = lax.broadcasted_iota(jnp.int32, (tq, tq), 0)
    qry = lax.broadcasted_iota(jnp.int32, (tq, tq), 1)
    past = key < qry
    suffix = jnp.where(qry >= key, -1.0, 0.0).astype(BF16)
    heads = range(2 * pairs)

    def step(kb, carry, boundary):
        start = pl.multiple_of(kb * tq, tq)
        zs = [_dot_nt(k_ref[pl.ds(start, tq), 128 * (h // 2):128 * (h // 2 + 1)], qs[h]) for h in heads]
        parts = []
        for z in zs:
            neg_abs = pltpu.bitcast(pltpu.bitcast(z, jnp.uint32) | jnp.uint32(0x80000000), F32)
            soft = jnp.maximum(z, 0.0) + jnp.log2(1.0 + jnp.exp2(neg_abs))
            if boundary:
                soft = jnp.where(past, soft, 0.0)
            parts.append(soft.astype(BF16))
        sums = [_dot(suffix, part) for part in parts]
        ws = []
        for h in heads:
            w = jnp.exp2(zs[h] + (carry[h][0] + sums[h]))
            ws.append((jnp.where(past, w, 0.0) if boundary else w).astype(BF16))
        return tuple((carry[h][0] + sums[h][0:1], carry[h][1]
                      + _dot(vt_ref[128 * (h // 2):128 * (h // 2 + 1), pl.ds(start, tq)], ws[h])) for h in heads)

    def live(carry):
        top = carry[0][0]
        for c, _ in carry[1:]:
            top = jnp.maximum(top, c)
        return (jnp.max(top) > SB_DONE).astype(jnp.int32)

    def body(state):
        t, _, carry = state
        carry = step(i - 1 - t, carry, False)
        return t + 1, live(carry), carry

    init = tuple((jnp.zeros((1, tq), F32), jnp.zeros((128, tq), F32)) for _ in range(2 * pairs))
    carry = step(i, init, True)
    _, _, carry = lax.while_loop(lambda st: (st[0] < i) & (st[1] > 0), body, (jnp.int32(0), live(carry), carry))
    rows = lax.broadcasted_iota(jnp.int32, (128, tq), 0)
    for p in range(pairs):
        ot = jnp.where(rows < HD, carry[2 * p][1], carry[2 * p + 1][1])
        o_ref[:, 128 * p:128 * (p + 1)] = ot.T.astype(BF16)


def _sb_attention(q, k, vt, B, T, tq=256, pairs=4):
    nq = T // tq
    w = 128 * pairs
    return pl.pallas_call(
        functools.partial(_sb_kernel, tq=tq, pairs=pairs),
        grid=(B, 4 // pairs, nq),
        in_specs=[pl.BlockSpec((tq, w), lambda b, p, i: (b * nq + i, p)),
                  pl.BlockSpec((T, w), lambda b, p, i: (b, p)),
                  pl.BlockSpec((None, w, T), lambda b, p, i: (b, p, 0))],
        out_specs=pl.BlockSpec((tq, w), lambda b, p, i: (b * nq + i, p)),
        out_shape=jax.ShapeDtypeStruct((B * T, 512), BF16),
        compiler_params=_cparams(("arbitrary", "arbitrary", "arbitrary")),
        name="sb_attention",
    )(q, k, vt)


def _router_weights(w_group, b_group, w_router, b_router):
    pad = 128 - N_EXP - N_GROUPS
    wr = jnp.concatenate([w_router, w_group, jnp.zeros((D, pad), F32)], axis=1).T
    hi = wr.astype(BF16)
    lo = (wr - hi.astype(F32)).astype(BF16)
    br = jnp.concatenate([b_router, b_group, jnp.zeros((pad,), F32)])[:, None]
    return jnp.concatenate([hi, lo], axis=0), br


def _ffn(h, oa, ob, wo, g, w_group, b_group, w_router, b_router, w_gate, w_up, w_down, layer):
    wr, br = _router_weights(w_group, b_group, w_router, b_router)
    hn, u, ri, rg, cnt = _outproj_router(h, oa, ob, wo.astype(BF16), g[None, :], wr, br)
    d1, d2, P, plan = _dispatch_plan(ri, cnt, h.shape[0])
    xs = _sc_scatter_rows(u, d1, d2, P)
    ys = _moe_experts(xs, plan, w_gate, w_up, w_down, layer)
    return hn, _sc_gather_rows(ys, jnp.concatenate([d1, d2])), rg


def kernel(x, norm_mix, norm_ffn, norm_final, rel_bias_table, w_in_even, g_mla_q, g_mla_kv, w_mla_qb, w_mla_kvb, swa_sinks, w_out_even, w_in_odd, w_out_odd, moe_w_group, moe_b_group, moe_w_router, moe_b_router, moe_w_gate, moe_w_up, moe_w_down):
    B, T, _ = x.shape
    h = x.reshape(B * T, D)
    tb_swa, tb_moba = _bias_tables(rel_bias_table)

    win, wqa, wqb, wk, wv = _even_weights(w_in_even[0], w_mla_qb[0], w_mla_kvb[0])
    q, k, v, qs8, ks, vst = _proj_even(h, norm_mix[0][None, :], win, g_mla_q[0][None, :], g_mla_kv[0][None, :],
                                       wqa, wqb, wk, wv, _rope_tables(T), T)
    oa = _mla_attention(q, k, v, B, T)
    ob = _swa_attention(qs8, ks, vst, swa_sinks[0], tb_swa, B, T)
    h, y12, rg = _ffn(h, oa, ob, w_out_even[0], norm_ffn[0], moe_w_group[0], moe_b_group[0], moe_w_router[0],
                      moe_b_router[0], moe_w_gate, moe_w_up, moe_w_down, 0)

    h, vct, vdt, qc, kc, qd, kd = _combine_proj_odd(h, y12, rg, norm_mix[1][None, :], *_odd_weights(w_in_odd[0]),
                                                   _block_indicator(T), T)
    oc = _moba_attention(qc, kc, vct, tb_moba, B, T)
    od = _sb_attention(qd, kd, vdt, B, T)
    h, y12, rg = _ffn(h, oc, od, w_out_odd[0], norm_ffn[1], moe_w_group[1], moe_b_group[1], moe_w_router[1],
                      moe_b_router[1], moe_w_gate, moe_w_up, moe_w_down, 1)
    out = _combine_final(h, y12, rg, norm_final[None, :])
    return out.reshape(B, T, D)
```

```python
import functools
import math

import numpy as np
import jax
import jax.numpy as jnp
from jax import lax
from jax.experimental import pallas as pl
from jax.experimental.pallas import tpu as pltpu
from jax.experimental.pallas import tpu_sc as plsc

F32 = jnp.float32
BF16 = jnp.bfloat16
NEG = -1e30
EPS = 1e-6

D = 1024
HD = 64
MLA_HEADS, MLA_QR, MLA_KVR, MLA_NOPE, MLA_ROPE, MLA_V = 8, 256, 128, 64, 32, 64
ROPE_THETA = 10000.0
SWA_HEADS, SWA_KV, SWA_WIN = 8, 2, 128
MOBA_L, MOBA_TOPK = 256, 3
REL_BUCKETS, REL_MAX = 32, 2048
N_GROUPS, EPG, N_EXP, D_EXP = 4, 8, 32, 256
BIAS_CONST_FROM = 1792
MOBA_TQ = 2 * MOBA_L
BIAS_PAD = MOBA_L
BIAS_ROWS = BIAS_PAD + BIAS_CONST_FROM + MOBA_TQ
SB_DONE = -160.0
TM = 1024
TM_E = 512
VMEM_LIMIT = 56 * 1024 * 1024
SC_CORES, SC_SUBCORES = 2, 16
SC_CHUNK = 128
HALF = D // 2
VT_ROWS = HD + 16
LOG2E = math.log2(math.e)


def _pack_halves(x):
    lo = pltpu.bitcast(x[:, :HALF].astype(BF16).astype(F32), jnp.uint32) >> 16
    hi = pltpu.bitcast(x[:, HALF:].astype(BF16).astype(F32), jnp.uint32) & jnp.uint32(0xFFFF0000)
    return pltpu.bitcast(lo | hi, jnp.int32)


def _unpack_halves(w):
    w = pltpu.bitcast(w, jnp.uint32)
    return pltpu.bitcast(w << 16, F32), pltpu.bitcast(w & jnp.uint32(0xFFFF0000), F32)


def _cparams(sem):
    return pltpu.CompilerParams(dimension_semantics=sem, vmem_limit_bytes=VMEM_LIMIT)


def _dot(a, b):
    return jnp.dot(a, b, preferred_element_type=F32)


def _dot_nt(a, b):
    return lax.dot_general(a, b, (((1,), (1,)), ((), ())), preferred_element_type=F32)


def _rms(x, g):
    return x * lax.rsqrt(jnp.mean(x * x, axis=-1, keepdims=True) + EPS) * g


def _values_t(vt):
    ones = jnp.ones((VT_ROWS - HD, vt.shape[1]), vt.dtype)
    return jnp.concatenate([blk for h in range(8) for blk in (vt[HD * h:HD * (h + 1)], ones)], axis=0)


def _spread_heads(x, lane_of):
    lane = lax.broadcasted_iota(jnp.int32, (x.shape[0], 128), 1)
    out = []
    for h in range(8):
        pair = x[:, 128 * (h // 2):128 * (h // 2 + 1)]
        if HD * (h % 2) != lane_of(h):
            pair = pltpu.roll(pair, HD, 1)
        out.append(jnp.where((lane >= lane_of(h)) & (lane < lane_of(h) + HD), pair, 0.0))
    return out


def _t5_bucket_np(dist):
    n = np.maximum(dist, 0)
    max_exact = REL_BUCKETS // 2
    nf = np.maximum(n, 1).astype(np.float32)
    large = max_exact + (np.log(nf / np.float32(max_exact)) / np.float32(math.log(REL_MAX / max_exact))
                         * np.float32(REL_BUCKETS - max_exact)).astype(np.int32)
    large = np.minimum(large, REL_BUCKETS - 1)
    return np.where(n < max_exact, n, large).astype(np.int32)


def _bias_kernel(table_ref, bucket_ref, out_ref, *, scale, ranges):
    b = bucket_ref[...]
    for m, (lo, hi) in enumerate(ranges):
        @pl.when(pl.program_id(0) == m)
        def _():
            for h in range(8):
                acc = jnp.full(b.shape, NEG if lo == REL_BUCKETS else table_ref[h, lo] * scale, F32)
                for k in range(lo + 1, min(hi, REL_BUCKETS - 1) + 1):
                    acc = jnp.where(b == k, table_ref[h, k] * scale, acc)
                if hi == REL_BUCKETS:
                    acc = jnp.where(b == REL_BUCKETS, NEG, acc)
                out_ref[h] = acc


def _bias_lookup(rel_table, dist, scale, name, masked=None):
    R, C = dist.shape
    buckets = _t5_bucket_np(dist)
    if masked is not None:
        buckets = np.where(masked, REL_BUCKETS, buckets).astype(np.int32)
    ranges = tuple((int(buckets[:, c:c + 256].min()), int(buckets[:, c:c + 256].max())) for c in range(0, C, 256))
    return pl.pallas_call(
        functools.partial(_bias_kernel, scale=scale, ranges=ranges),
        grid=(C // 256,),
        in_specs=[pl.BlockSpec(memory_space=pltpu.SMEM),
                  pl.BlockSpec((R, 256), lambda m: (0, m))],
        out_specs=pl.BlockSpec((8, R, 256), lambda m: (0, 0, m)),
        out_shape=jax.ShapeDtypeStruct((8, R, C), F32),
        compiler_params=_cparams(("arbitrary",)),
        name=name,
    )(rel_table.T, jnp.asarray(buckets))


def _bias_tables(rel_table):
    d = np.arange(256)[None, :] - np.arange(256)[:, None]
    swa = _bias_lookup(rel_table, d, LOG2E, "bias_swa", masked=(d < 0) | (d >= SWA_WIN))
    moba = _bias_lookup(rel_table, np.arange(BIAS_ROWS)[None, :] - BIAS_PAD - np.arange(MOBA_L)[:, None], LOG2E,
                        "bias_moba")
    return swa, moba


def _proj_even_kernel(h_ref, g_ref, win_ref, gq_ref, gkv_ref, wqa_ref, wqb_ref, wk_ref, wv_ref,
                      cq_ref, sq_ref, ck_ref, sk_ref,
                      q_ref, k_ref, v_ref, qs_ref, ks_ref, vst_ref):
    u = _rms(h_ref[...], g_ref[...]).astype(BF16)
    nq = _rms(_dot(u, win_ref[:, 0:256]), gq_ref[...]).astype(BF16)
    qa = _dot(nq, wqa_ref[...])
    qb = _dot(nq, wqb_ref[...])
    cq, sq = cq_ref[...], sq_ref[...]
    for hh in range(MLA_HEADS):
        sl = slice(128 * hh, 128 * (hh + 1))
        q_ref[:, sl] = (qa[:, sl] * cq + qb[:, sl] * sq).astype(BF16)
    nkv = _rms(_dot(u, win_ref[:, 256:384]), gkv_ref[...]).astype(BF16)
    kn = _dot(nkv, wk_ref[...])
    v_ref[...] = _values_t(_dot_nt(wv_ref[...], nkv)).astype(BF16)
    rest = _dot(u, win_ref[:, 896:1408])
    kr = rest[:, 128:256] * ck_ref[...] + rest[:, 256:384] * sk_ref[...]
    for hh in range(MLA_HEADS):
        sl = slice(128 * hh, 128 * (hh + 1))
        k_ref[:, sl] = (kn[:, sl] + kr).astype(BF16)
    qs = _spread_heads(_dot(u, win_ref[:, 384:896]), lambda hq: HD * (hq // (SWA_HEADS // SWA_KV)))
    for hq in range(SWA_HEADS):
        qs_ref[:, 128 * hq:128 * (hq + 1)] = qs[hq].astype(BF16)
    ks_ref[...] = rest[:, 0:128].astype(BF16)
    vst_ref[...] = rest[:, 384:512].T.astype(BF16)


def _even_weights_kernel(wt_ref, wqb_in_ref, wkvb_ref, win_ref, wqa_ref, wqb_ref, wk_ref, wv_ref):
    o_kr = MLA_QR + MLA_KVR
    o_qs = o_kr + MLA_ROPE
    o_ks = o_qs + SWA_HEADS * HD
    o_vs = o_ks + SWA_KV * HD
    half = MLA_ROPE // 2
    wt = wt_ref[...]
    zero_rows = lambda n: jnp.zeros((n, D), F32)
    zero = lambda a, n: jnp.zeros((a.shape[0], n), F32)
    win_ref[...] = jnp.concatenate([
        wt[0:o_kr], wt[o_qs:o_ks] * (HD ** -0.5 * LOG2E), wt[o_ks:o_vs],
        zero_rows(64), wt[o_kr:o_qs], zero_rows(32),
        zero_rows(64), wt[o_kr + half:o_qs], wt[o_kr:o_kr + half], zero_rows(32),
        wt[o_vs:o_vs + SWA_KV * HD]], axis=0).T.astype(BF16)
    wq, wkv = wqb_in_ref[...], wkvb_ref[...]
    hq, hkv = MLA_NOPE + MLA_ROPE, MLA_NOPE + MLA_V
    qa, qb, kn, vt = [], [], [], []
    for h in range(MLA_HEADS):
        qa += [wq[:, hq * h:hq * (h + 1)], zero(wq, 32)]
        qb += [zero(wq, MLA_NOPE), wq[:, hq * h + MLA_NOPE + half:hq * (h + 1)],
               wq[:, hq * h + MLA_NOPE:hq * h + MLA_NOPE + half], zero(wq, 32)]
        kn += [wkv[:, hkv * h:hkv * h + MLA_NOPE], zero(wkv, MLA_V)]
        vt += [wkv[:, hkv * h + MLA_NOPE:hkv * (h + 1)]]
    wqa_ref[...] = jnp.concatenate(qa, axis=1).astype(BF16)
    wqb_ref[...] = jnp.concatenate(qb, axis=1).astype(BF16)
    wk_ref[...] = jnp.concatenate(kn, axis=1).astype(BF16)
    wv_ref[...] = jnp.concatenate(vt, axis=1).T.astype(BF16)


def _even_weights(w_in, w_qb, w_kvb):
    shapes =[(D, 1408), (MLA_QR, 128 * MLA_HEADS), (MLA_QR, 128 * MLA_HEADS), (MLA_KVR, 128 * MLA_HEADS),
              (MLA_V * MLA_HEADS, MLA_KVR)]
    return pl.pallas_call(
        _even_weights_kernel,
        out_shape=[jax.ShapeDtypeStruct(s, BF16) for s in shapes],
        compiler_params=_cparams(None),
        name="even_weights",
    )(w_in.T, w_qb, w_kvb)


def _rope_tables(T):
    f32 = np.float32
    freqs = f32(ROPE_THETA) ** (-np.arange(0, MLA_ROPE, 2, dtype=f32) / f32(MLA_ROPE))
    ang = np.arange(T, dtype=f32)[:, None] * freqs[None, :]
    cos, sin = np.cos(ang), np.sin(ang)
    z = lambda n: np.zeros((T, n), f32)
    scale = f32((MLA_NOPE + MLA_ROPE) ** -0.5 * LOG2E)
    ck = np.concatenate([z(64), cos, cos, z(32)], axis=1)
    sk = np.concatenate([z(64), -sin, sin, z(32)], axis=1)
    cq = np.concatenate([np.ones((T, 64), f32), cos, cos, z(32)], axis=1) * scale
    sq = sk * scale
    return tuple(jnp.asarray(t, F32) for t in (cq, sq, ck, sk))


def _proj_even(h, g, win, gq, gkv, wqa, wqb, wk, wv, tables, T):
    N = h.shape[0]
    nt = T // TM
    full = lambda a: pl.BlockSpec(a.shape, lambda i: (0,) * a.ndim)
    tab = pl.BlockSpec((TM, 128), lambda i: (i % nt, 0))
    row = lambda w: pl.BlockSpec((TM, w), lambda i: (i, 0))
    outs = (1024, 1024, -8 * VT_ROWS, 1024, 128, -128)
    spec = lambda w: row(w) if w > 0 else pl.BlockSpec((None, -w, TM), lambda i: (i // nt, 0, i % nt))
    shape = lambda w: jax.ShapeDtypeStruct((N, w) if w > 0 else (N // T, -w, T), BF16)
    return pl.pallas_call(
        _proj_even_kernel,
        grid=(N // TM,),
        in_specs=[row(D), full(g), full(win), full(gq), full(gkv), full(wqa), full(wqb), full(wk), full(wv),
                  tab, tab, tab, tab],
        out_specs=[spec(w) for w in outs],
        out_shape=[shape(w) for w in outs],
        compiler_params=_cparams(("arbitrary",)),
        name="proj_even",
    )(h, g, win, gq, gkv, wqa, wqb, wk, wv, *tables)


def _finish_t(accs):
    ot = jnp.concatenate([acc[:HD] / acc[HD:HD + 1] for acc in accs], axis=0)
    return ot.T.astype(BF16)


def _causal_attention_pair_t(ia, nq, q_of, k_ref, vt_ref, o_ref, s_ref, smax_ref, m_ref, acc_ref, *, tq,
                             bias_of=None):
    nh = s_ref.shape[1]
    first_query = (ia * tq, (nq - 1 - ia) * tq)
    n_items = nq + 1
    assert n_items % 2 == 1
    key = lax.broadcasted_iota(jnp.int32, (tq, tq), 0)
    qry = lax.broadcasted_iota(jnp.int32, (tq, tq), 1)

    def item(p):
        if isinstance(p, int) and p < 2:
            return p, first_query[p], first_query[p]
        t = jnp.where(p - 2 >= ia, 1, 0)
        return t, jnp.where(t == 1, first_query[1], first_query[0]), (p - 2 - t * ia) * tq

    def scores_to(slot, p):
        t, q0, start = item(p)
        start = pl.multiple_of(start, tq)
        for j in range(nh):
            st = _dot_nt(k_ref[pl.ds(start, tq), 128 * j:128 * (j + 1)], q_of(t, q0, j))
            if bias_of is not None:
                st = st + bias_of(start, q0, j)
            if isinstance(p, int) and p < 2:
                st = jnp.where(key <= qry, st, NEG)
            s_ref[slot, j] = st
            smax_ref[slot, j] = jnp.max(st, axis=0, keepdims=True)

    def update_from(slot, p):
        t, _, start = item(p)
        start = pl.multiple_of(start, tq)
        m_new = [jnp.maximum(m_ref[t, j], smax_ref[slot, j]) for j in range(nh)]
        pts = [jnp.exp2(s_ref[slot, j] - m_new[j]).astype(BF16) for j in range(nh)]
        for j in range(nh):
            acc_ref[t, j] = (jnp.exp2(m_ref[t, j] - m_new[j]) * acc_ref[t, j]
                             + _dot(vt_ref[VT_ROWS * j:VT_ROWS * (j + 1), pl.ds(start, tq)], pts[j]))
            m_ref[t, j] = m_new[j]

    def pair(k):
        scores_to(1, k + 1)
        update_from(0, k)
        scores_to(0, k + 2)
        update_from(1, k + 1)

    def quad(c, _):
        pair(2 + 4 * c)
        pair(4 + 4 * c)
        return 0

    m_ref[...] = jnp.full(m_ref.shape, NEG, F32)
    acc_ref[...] = jnp.zeros(acc_ref.shape, F32)
    scores_to(0, 0)
    pair(0)
    later_pairs = (n_items - 1) // 2 - 1
    lax.fori_loop(0, later_pairs // 2, quad, 0)
    if later_pairs % 2:
        pair(n_items - 3)
    update_from(0, n_items - 1)
    for t in range(2):
        o_ref[pl.ds(pl.multiple_of(first_query[t], tq), tq), :] = _finish_t([acc_ref[t, j] for j in range(nh)])


def _attention_t_scratch(tq):
    return [pltpu.VMEM((2, 2, tq, tq), F32), pltpu.VMEM((2, 2, 1, tq), F32), pltpu.VMEM((2, 2, 1, tq), F32),
            pltpu.VMEM((2, 2, VT_ROWS, tq), F32)]


def _mla_kernel(q_ref, k_ref, vt_ref, o_ref, s_ref, smax_ref, m_ref, acc_ref, *, tq, nq):
    def q_of(t, q0, j):
        return q_ref[pl.ds(pl.multiple_of(q0, tq), tq), 128 * j:128 * (j + 1)]

    _causal_attention_pair_t(pl.program_id(2), nq, q_of, k_ref, vt_ref, o_ref, s_ref, smax_ref, m_ref, acc_ref,
                             tq=tq)


def _mla_attention(q, k, vt, B, T, tq=512):
    nq = T // tq
    return pl.pallas_call(
        functools.partial(_mla_kernel, tq=tq, nq=nq),
        grid=(B, MLA_HEADS // 2, nq // 2),
        in_specs=[pl.BlockSpec((T, 256), lambda b, p, i: (b, p)),
                  pl.BlockSpec((T, 256), lambda b, p, i: (b, p)),
                  pl.BlockSpec((None, 2 * VT_ROWS, T), lambda b, p, i: (b, p, 0))],
        out_specs=pl.BlockSpec((T, 128), lambda b, p, i: (b, p)),
        out_shape=jax.ShapeDtypeStruct((B * T, MLA_HEADS * MLA_V), BF16),
        scratch_shapes=_attention_t_scratch(tq),
        compiler_params=_cparams(("arbitrary", "arbitrary", "arbitrary")),
        name="mla_attention",
    )(q, k, vt)


def _swa_kernel(sink_ref, q_ref, k_ref, vt_ref, tb_ref, o_ref, *, tq):
    i = pl.program_id(1)
    G = SWA_HEADS // SWA_KV
    W = 2 * SWA_WIN
    head = lax.broadcasted_iota(jnp.int32, (1, G * 128), 1) >> 7
    sinks = []
    for c in range(SWA_KV):
        sink = jnp.zeros((1, G * 128), F32)
        for g in range(G):
            sink = jnp.where(head == g, sink_ref[G * c + g], sink)
        sinks.append(sink)
    scores, values = [], []
    for r in range(tq // 128):
        qstart = i * tq + 128 * r
        kstart = pl.multiple_of(jnp.maximum(qstart - SWA_WIN, 0), 128)
        off = pl.multiple_of(qstart - kstart, 128)
        kw = k_ref[pl.ds(kstart, W), :]
        for c in range(SWA_KV):
            qg = jnp.concatenate([q_ref[128 * r:128 * (r + 1), 128 * (G * c + g):128 * (G * c + g + 1)]
                                  for g in range(G)], axis=0)
            bias = jnp.concatenate([tb_ref[G * c + g, :, pl.ds(off, 128)] for g in range(G)], axis=1)
            scores.append(_dot_nt(kw, qg) + bias)
            values.append(vt_ref[HD * c:HD * (c + 1), pl.ds(kstart, W)])
    probs = []
    for n, s in enumerate(scores):
        sink = sinks[n % SWA_KV]
        m = jnp.maximum(jnp.max(s, axis=0, keepdims=True), sink)
        p = jnp.exp2(s - m)
        probs.append((p.astype(BF16), jnp.sum(p, axis=0, keepdims=True) + jnp.exp2(sink - m)))
    for r in range(tq // 128):
        outs = []
        for c in range(SWA_KV):
            p, denom = probs[SWA_KV * r + c]
            ot = _dot(values[SWA_KV * r + c], p) / denom
            outs += [ot[:, 128 * g:128 * (g + 1)] for g in range(G)]
        o_ref[128 * r:128 * (r + 1), :] = jnp.concatenate(outs, axis=0).T.astype(BF16)


def _swa_attention(q8, k, vt, sinks, tb, B, T, tq=512):
    nq = T // tq
    return pl.pallas_call(
        functools.partial(_swa_kernel, tq=tq),
        grid=(B, nq),
        in_specs=[pl.BlockSpec(memory_space=pltpu.SMEM),
                  pl.BlockSpec((tq, 1024), lambda b, i: (b * nq + i, 0)),
                  pl.BlockSpec((T, 128), lambda b, i: (b, 0)),
                  pl.BlockSpec((None, 128, T), lambda b, i: (b, 0, 0)),
                  pl.BlockSpec((8, 256, 256), lambda b, i: (0, 0, 0))],
        out_specs=pl.BlockSpec((tq, 512), lambda b, i: (b * nq + i, 0)),
        out_shape=jax.ShapeDtypeStruct((B * T, SWA_HEADS * HD), BF16),
        compiler_params=_cparams(("arbitrary", "arbitrary")),
        name="swa_attention",
    )(sinks * LOG2E, q8, k, vt, tb)


def _outproj_router_kernel(h_ref, oa_ref, ob_ref, wo_ref, g_ref, wr_ref, br_ref,
                           hn_ref, u_ref, ri_ref, rg_ref, cnt_ref, base_ref):
    i = pl.program_id(0)

    @pl.when(i == 0)
    def _():
        base_ref[...] = jnp.zeros_like(base_ref)

    hn = h_ref[...] + _dot(oa_ref[...], wo_ref[0:512, :]) + _dot(ob_ref[...], wo_ref[512:1024, :])
    hn_ref[...] = hn
    u = _rms(hn, g_ref[...])
    u_ref[...] = _pack_halves(u)
    u_hi = u.astype(BF16)
    u_lo = (u - u_hi.astype(F32)).astype(BF16)
    both = _dot_nt(wr_ref[...], u_hi)
    logits = both[:128] + both[128:] + _dot_nt(wr_ref[0:128, :], u_lo) + br_ref[...]
    tm = logits.shape[1]
    sub = lax.broadcasted_iota(jnp.int32, (8, tm), 0).astype(F32)
    big = 1e6
    isg = sub < N_GROUPS
    gl = jnp.where(isg, logits[N_EXP:N_EXP + 8], NEG)
    gmax = jnp.max(gl, axis=0, keepdims=True)
    gsel = jnp.min(jnp.where(gl == gmax, sub, big), axis=0, keepdims=True)
    g_gate = 1.0 / jnp.sum(jnp.where(isg, jnp.exp(gl - gmax), 0.0), axis=0, keepdims=True)
    el = jnp.zeros((EPG, tm), F32)
    for grp in range(N_GROUPS):
        el = jnp.where(gsel == grp, logits[EPG * grp:EPG * (grp + 1)], el)
    m1 = jnp.max(el, axis=0, keepdims=True)
    i1 = jnp.min(jnp.where(el == m1, sub, big), axis=0, keepdims=True)
    el2 = jnp.where(sub == i1, NEG, el)
    m2 = jnp.max(el2, axis=0, keepdims=True)
    i2 = jnp.min(jnp.where(el2 == m2, sub, big), axis=0, keepdims=True)
    e1 = gsel * EPG + i1
    e2 = gsel * EPG + i2
    r = jnp.exp(m2 - m1)
    gate1 = g_gate / (1.0 + r)
    gate2 = g_gate * r / (1.0 + r)
    rows = lax.broadcasted_iota(jnp.int32, (128, tm), 0).astype(F32)
    oh1 = (rows == e1).astype(F32)
    oh2 = (rows == e2).astype(F32)
    oh = oh1 + oh2
    t_row = lax.broadcasted_iota(jnp.int32, (tm, tm), 0)
    t_col = lax.broadcasted_iota(jnp.int32, (tm, tm), 1)
    earlier = jnp.where(t_row < t_col, 1.0, 0.0).astype(BF16)
    prefix = _dot(oh.astype(BF16), earlier) + base_ref[:, 0:1]
    rank1 = jnp.sum(oh1 * prefix, axis=0, keepdims=True)
    rank2 = jnp.sum(oh2 * prefix, axis=0, keepdims=True)
    base_ref[...] = base_ref[...] + jnp.sum(oh, axis=1, keepdims=True)
    cnt_ref[...] = base_ref[...]
    ri_ref[...] = jnp.where(sub == 0, e1, jnp.where(sub == 1, e2, jnp.where(sub == 2, rank1,
                            jnp.where(sub == 3, rank2, 0.0))))
    rg_ref[...] = jnp.where(rows == 0, gate1, jnp.where(rows == 1, gate2, 0.0)).T


def _outproj_router(h, oa, ob, wo, g, wr, br):
    N = h.shape[0]
    full = lambda a: pl.BlockSpec(a.shape, lambda i: (0,) * a.ndim)
    row = lambda w: pl.BlockSpec((TM, w), lambda i: (i, 0))
    return pl.pallas_call(
        _outproj_router_kernel,
        grid=(N // TM,),
        in_specs=[row(D), row(512), row(512), full(wo), full(g), full(wr), full(br)],
        out_specs=[row(D), row(HALF), pl.BlockSpec((8, TM), lambda i: (0, i)), row(128),
                   pl.BlockSpec((128, 128), lambda i: (0, 0))],
        out_shape=[jax.ShapeDtypeStruct((N, D), F32), jax.ShapeDtypeStruct((N, HALF), jnp.int32),
                   jax.ShapeDtypeStruct((8, N), F32), jax.ShapeDtypeStruct((N, 128), F32),
                   jax.ShapeDtypeStruct((128, 128), F32)],
        scratch_shapes=[pltpu.VMEM((128, 128), F32)],
        compiler_params=_cparams(("arbitrary",)),
        name="outproj_router",
    )(h, oa, ob, wo, g, wr, br)


def _moe_kernel(te_ref, tv_ref, run_ref, nx_ref, x_ref, wg_hbm, wu_hbm, wd_hbm, o_ref,
                wg_buf, wu_buf, wd_buf, sems, wgu_s, wd_s, *, layer):
    i = pl.program_id(0)
    valid = tv_ref[i]

    def weight_copies(e, slot):
        return (pltpu.make_async_copy(wg_hbm.at[layer, e], wg_buf.at[slot], sems.at[slot, 0]),
                pltpu.make_async_copy(wu_hbm.at[layer, e], wu_buf.at[slot], sems.at[slot, 1]),
                pltpu.make_async_copy(wd_hbm.at[layer, e], wd_buf.at[slot], sems.at[slot, 2]))

    @pl.when(run_ref[i] > 0)
    def _():
        slot = run_ref[i] - 1

        @pl.when(i == 0)
        def _():
            for c in weight_copies(te_ref[i], slot):
                c.start()

        for c in weight_copies(te_ref[i], slot):
            c.wait()
        wgu_s[:, :D_EXP] = wg_buf[slot].astype(BF16)
        wgu_s[:, D_EXP:] = wu_buf[slot].astype(BF16)
        wd_s[...] = wd_buf[slot].astype(BF16)

        @pl.when(nx_ref[i] >= 0)
        def _():
            for c in weight_copies(nx_ref[i], 1 - slot):
                c.start()

    @pl.when(valid > 0)
    def _():
        rows = lax.broadcasted_iota(jnp.int32, x_ref.shape, 0)
        lo, hi = _unpack_halves(jnp.where(rows < valid, x_ref[...], 0))
        hgu = _dot(lo.astype(BF16), wgu_s[:HALF, :]) + _dot(hi.astype(BF16), wgu_s[HALF:, :])
        hg, hu = hgu[:, :D_EXP], hgu[:, D_EXP:]
        a = hg * (1.0 / (1.0 + jnp.exp(-hg))) * hu
        o_ref[...] = _pack_halves(_dot(a.astype(BF16), wd_s[...]))

    @pl.when(valid <= 0)
    def _():
        o_ref[...] = jnp.zeros_like(o_ref)


def _moe_experts(xs, plan, w_gate, w_up, w_down, layer):
    P = xs.shape[0]
    hbm = pl.BlockSpec(memory_space=pl.ANY)
    grid_spec = pltpu.PrefetchScalarGridSpec(
        num_scalar_prefetch=4,
        grid=(P // TM_E,),
        in_specs=[pl.BlockSpec((TM_E, HALF), lambda i, *_: (i, 0)), hbm, hbm, hbm],
        out_specs=pl.BlockSpec((TM_E, HALF), lambda i, *_: (i, 0)),
        scratch_shapes=[pltpu.VMEM((2, D, D_EXP), F32), pltpu.VMEM((2, D, D_EXP), F32),
                        pltpu.VMEM((2, D_EXP, D), F32), pltpu.SemaphoreType.DMA((2, 3)),
                        pltpu.VMEM((D, 2 * D_EXP), BF16), pltpu.VMEM((D_EXP, D), BF16)],
    )
    return pl.pallas_call(
        functools.partial(_moe_kernel, layer=layer),
        grid_spec=grid_spec,
        out_shape=jax.ShapeDtypeStruct((P, HALF), jnp.int32),
        compiler_params=_cparams(("arbitrary",)),
        name="moe_experts",
    )(*plan, xs, w_gate, w_up, w_down)


def _sc_mesh():
    return plsc.VectorSubcoreMesh(core_axis_name="c", subcore_axis_name="s",
                                  num_cores=SC_CORES, num_subcores=SC_SUBCORES)


def _sc_scatter_rows(src, d1, d2, P):
    N, W = src.shape
    per_w = N // (SC_CORES * SC_SUBCORES)

    @functools.partial(pl.kernel, mesh=_sc_mesh(), out_type=jax.ShapeDtypeStruct((P, W), src.dtype),
                       scratch_types=[pltpu.VMEM((SC_CHUNK,), jnp.int32), pltpu.VMEM((SC_CHUNK, W), src.dtype)],
                       name="sc_scatter_rows")
    def k(src_hbm, d1_hbm, d2_hbm, out_hbm, idx_v, rows_v):
        wid = lax.axis_index("s") * SC_CORES + lax.axis_index("c")

        @pl.loop(0, per_w // SC_CHUNK)
        def _(c):
            off = wid * per_w + c * SC_CHUNK
            pltpu.sync_copy(src_hbm.at[pl.ds(off, SC_CHUNK)], rows_v)
            pltpu.sync_copy(d1_hbm.at[pl.ds(off, SC_CHUNK)], idx_v)
            pltpu.sync_copy(rows_v, out_hbm.at[idx_v])
            pltpu.sync_copy(d2_hbm.at[pl.ds(off, SC_CHUNK)], idx_v)
            pltpu.sync_copy(rows_v, out_hbm.at[idx_v])

    return k(src, d1, d2)


def _sc_gather_rows(table, idx):
    B, W = idx.shape[0], table.shape[1]
    per_w = B // (SC_CORES * SC_SUBCORES)

    @functools.partial(pl.kernel, mesh=_sc_mesh(), out_type=jax.ShapeDtypeStruct((B, W), table.dtype),
                       scratch_types=[pltpu.VMEM((SC_CHUNK,), jnp.int32), pltpu.VMEM((SC_CHUNK, W), table.dtype)],
                       name="sc_gather_rows")
    def k(table_hbm, idx_hbm, out_hbm, idx_v, rows_v):
        wid = lax.axis_index("s") * SC_CORES + lax.axis_index("c")

        @pl.loop(0, per_w // SC_CHUNK)
        def _(c):
            off = wid * per_w + c * SC_CHUNK
            pltpu.sync_copy(idx_hbm.at[pl.ds(off, SC_CHUNK)], idx_v)
            pltpu.sync_copy(table_hbm.at[idx_v], rows_v)
            pltpu.sync_copy(rows_v, out_hbm.at[pl.ds(off, SC_CHUNK)])

    return k(table, idx)


def _dest_kernel(offs_ref, ri_ref, d_ref):
    ri = ri_ref[...]
    base = jnp.zeros(ri.shape, F32)
    for e in range(N_EXP):
        base = jnp.where(ri == e, offs_ref[e].astype(F32), base)
    d_ref[...] = (base + pltpu.roll(ri, 6, 0)).astype(jnp.int32)


def _dispatch_plan(ri, cnt, N):
    counts = cnt[:N_EXP, 0].astype(jnp.int32)
    padded = ((counts + TM_E - 1) // TM_E) * TM_E
    ends = jnp.cumsum(padded)
    offs = ends - padded
    take = lambda t, i: t.at[i].get(mode="promise_in_bounds")
    dest = pl.pallas_call(
        _dest_kernel,
        in_specs=[pl.BlockSpec(memory_space=pltpu.SMEM), pl.BlockSpec((8, N), lambda: (0, 0))],
        out_specs=pl.BlockSpec((8, N), lambda: (0, 0)),
        out_shape=jax.ShapeDtypeStruct((8, N), jnp.int32),
        name="dispatch_rows",
    )(offs, ri)
    d1, d2 = dest[0], dest[1]
    P = 2 * N + N_EXP * TM_E
    tile_start = jnp.arange(P // TM_E, dtype=jnp.int32) * TM_E
    tile_expert = jnp.minimum(jnp.sum((ends[None, :] <= tile_start[:, None]).astype(jnp.int32), axis=1), N_EXP - 1)
    tile_valid = jnp.clip(take(offs + counts, tile_expert) - tile_start, 0, TM_E)
    n_tiles = P // TM_E
    tile = jnp.arange(n_tiles, dtype=jnp.int32)
    starts = (tile_valid > 0) & ((tile == 0) | (tile_expert != jnp.roll(tile_expert, 1)))
    slot = (jnp.cumsum(starts.astype(jnp.int32)) - 1) % 2
    run_code = jnp.where(starts, slot + 1, 0).astype(jnp.int32)
    start_at = jnp.where(starts, tile, n_tiles)
    next_start = jnp.concatenate([lax.cummin(start_at, reverse=True)[1:], jnp.full((1,), n_tiles, jnp.int32)])
    next_expert = jnp.where(next_start < n_tiles, take(tile_expert, jnp.minimum(next_start, n_tiles - 1)), -1)
    return d1, d2, P, (tile_expert, tile_valid, run_code, next_expert.astype(jnp.int32))


def _combine(h_ref, y1_ref, y2_ref, rg_ref):
    rg = rg_ref[...]
    lo1, hi1 = _unpack_halves(y1_ref[...])
    lo2, hi2 = _unpack_halves(y2_ref[...])
    g1, g2 = rg[:, 0:1], rg[:, 1:2]
    return h_ref[...] + jnp.concatenate([g1 * lo1 + g2 * lo2, g1 * hi1 + g2 * hi2], axis=1)


ODD_WIDTHS = (1024, 1024, 512, 512)


def _combine_proj_odd_kernel(h_ref, y1_ref, y2_ref, rg_ref, g_ref, win_ref, wvt_ref, ind_ref, hn_ref, vt_ref,
                             vdt_ref, *out_refs):
    hn = _combine(h_ref, y1_ref, y2_ref, rg_ref)
    hn_ref[...] = hn
    u = _rms(hn, g_ref[...]).astype(BF16)
    vt_ref[...] = _values_t(_dot_nt(wvt_ref[0:512, :], u)).astype(BF16)
    vdt_ref[...] = _dot_nt(wvt_ref[512:1024, :], u).astype(BF16)
    qc_ref, kc_ref, qd_ref, kd_ref = out_refs
    qc = _spread_heads(_dot(u, win_ref[:, 0:512]), lambda h: 0)
    kc = _spread_heads(_dot(u, win_ref[:, 512:1024]), lambda h: 0)
    for hh in range(8):
        qc_ref[:, 128 * hh:128 * (hh + 1)] = qc[hh].astype(BF16)
        kc_ref[:, 128 * hh:128 * (hh + 1)] = (kc[hh] + ind_ref[...]).astype(BF16)
    qd_ref[...] = _dot(u, win_ref[:, 1024:1536]).astype(BF16)
    kd_ref[...] = _dot(u, win_ref[:, 1536:2048]).astype(BF16)


def _odd_weights_kernel(w_ref, win_ref, wvt_ref):
    win_ref[:, 0:512] = (w_ref[:, 0:512] * (HD ** -0.5 * LOG2E)).astype(BF16)
    win_ref[:, 512:1024] = w_ref[:, 512:1024].astype(BF16)
    wvt_ref[...] = w_ref[:, 1024:1536].T.astype(BF16)


def _odd_weights(w_in):
    return pl.pallas_call(
        _odd_weights_kernel,
        grid=(2,),
        in_specs=[pl.BlockSpec((D, 1536), lambda i: (0, i))],
        out_specs=[pl.BlockSpec((D, 1024), lambda i: (0, i)), pl.BlockSpec((512, D), lambda i: (i, 0))],
        out_shape=[jax.ShapeDtypeStruct((D, 2048), BF16), jax.ShapeDtypeStruct((1024, D), BF16)],
        compiler_params=_cparams(("arbitrary",)),
        name="odd_weights",
    )(w_in)


def _block_indicator(T):
    assert T // MOBA_L <= 64
    ind = np.zeros((T, 128), np.float32)
    ind[np.arange(T), 64 + np.arange(T) // MOBA_L] = 1.0
    return jnp.asarray(ind)


def _y_specs(N):
    return [pl.BlockSpec((TM, HALF), lambda i: (i, 0)), pl.BlockSpec((TM, HALF), lambda i: (i + N // TM, 0))]


def _combine_proj_odd(h, y12, rg, g, win, wvt, ind, T):
    N = h.shape[0]
    nt = T // TM
    full = lambda a: pl.BlockSpec(a.shape, lambda i: (0,) * a.ndim)
    row = lambda w: pl.BlockSpec((TM, w), lambda i: (i, 0))
    return pl.pallas_call(
        _combine_proj_odd_kernel,
        grid=(N // TM,),
        in_specs=[row(D)] + _y_specs(N) + [row(128), full(g), full(win), full(wvt),
                  pl.BlockSpec((TM, 128), lambda i: (i % nt, 0))],
        out_specs=[row(D), pl.BlockSpec((None, 8 * VT_ROWS, TM), lambda i: (i // nt, 0, i % nt)),
                   pl.BlockSpec((None, 512, TM), lambda i: (i // nt, 0, i % nt))]
        + [row(w) for w in ODD_WIDTHS],
        out_shape=[jax.ShapeDtypeStruct((N, D), F32), jax.ShapeDtypeStruct((N // T, 8 * VT_ROWS, T), BF16),
                   jax.ShapeDtypeStruct((N // T, 512, T), BF16)]
        + [jax.ShapeDtypeStruct((N, w), BF16) for w in ODD_WIDTHS],
        compiler_params=_cparams(("arbitrary",)),
        name="combine_proj_odd",
    )(h, y12, y12, rg, g, win, wvt, ind)


def _combine_final_kernel(h_ref, y1_ref, y2_ref, rg_ref, g_ref, o_ref):
    o_ref[...] = _rms(_combine(h_ref, y1_ref, y2_ref, rg_ref), g_ref[...])


def _combine_final(h, y12, rg, g):
    N = h.shape[0]
    row = lambda w: pl.BlockSpec((TM, w), lambda i: (i, 0))
    return pl.pallas_call(
        _combine_final_kernel,
        grid=(N // TM,),
        in_specs=[row(D)] + _y_specs(N) + [row(128), pl.BlockSpec((1, D), lambda i: (0, 0))],
        out_specs=row(D),
        out_shape=jax.ShapeDtypeStruct((N, D), F32),
        compiler_params=_cparams(("arbitrary",)),
        name="combine_final",
    )(h, y12, y12, rg, g)


def _moba_kernel(q_ref, k_ref, vt_ref, tb_ref, o_ref, kmean_ref, ksplit_ref, qa_ref, s_ref, smax_ref, m_ref, acc_ref,
                 *, nkb, nq):
    ia = pl.program_id(2)
    L, tq = MOBA_L, MOBA_TQ

    @pl.when(ia == 0)
    def _():
        kmean_ref[...] = jnp.zeros_like(kmean_ref)
        for n in range(nkb):
            kmean_ref[64 + n:65 + n, :] = jnp.sum(k_ref[n * L:(n + 1) * L, :].astype(F32), axis=0,
                                                  keepdims=True) * (1.0 / L)
        hi = kmean_ref[...].astype(BF16)
        ksplit_ref[0] = hi
        ksplit_ref[1] = (kmean_ref[...] - hi.astype(F32)).astype(BF16)

    nb = -(-nkb // 8) * 8
    lane = lax.broadcasted_iota(jnp.int32, (tq, 128), 1)
    blk = lax.broadcasted_iota(jnp.int32, (nb, tq), 0)
    half = jnp.where(lax.broadcasted_iota(jnp.int32, (nb, tq), 1) >= L, 1, 0)
    blk_f = blk.astype(F32)
    for t, tile in enumerate((ia, nq - 1 - ia)):
        own = 2 * tile + half
        for j in range(2):
            qj = q_ref[pl.ds(pl.multiple_of(tile * tq, tq), tq), 128 * j:128 * (j + 1)]
            gsc = (_dot_nt(ksplit_ref[0, 64:64 + nb, 128 * j:128 * (j + 1)], qj)
                   + _dot_nt(ksplit_ref[1, 64:64 + nb, 128 * j:128 * (j + 1)], qj))
            g = jnp.where(blk < own, gsc, NEG)
            allowed = jnp.where(blk == own, 1.0, 0.0)
            for _ in range(MOBA_TOPK):
                mx = jnp.max(g, axis=0, keepdims=True)
                idx = jnp.min(jnp.where(g == mx, blk_f, 1e6), axis=0, keepdims=True)
                pick = blk_f == idx
                allowed = jnp.maximum(allowed, jnp.where(pick, jnp.where(mx > 0.5 * NEG, 1.0, 0.0), 0.0))
                g = jnp.where(pick, 2.0 * NEG, g)
            mask = jnp.concatenate([jnp.zeros((64, tq), F32), jnp.where(allowed > 0.5, 0.0, NEG),
                                    jnp.zeros((64 - nb, tq), F32)], axis=0)
            qa_ref[t, j] = jnp.where(lane < HD, qj, mask.T.astype(BF16))

    def bias_of(start, q0, j):
        parts = []
        for c in range(tq // L):
            off = jnp.minimum(q0 - start - c * L, BIAS_CONST_FROM) + BIAS_PAD
            parts.append(tb_ref[j, :, pl.ds(pl.multiple_of(off, L), tq)])
        return jnp.concatenate(parts, axis=0)

    _causal_attention_pair_t(ia, nq, lambda t, q0, j: qa_ref[t, j], k_ref, vt_ref, o_ref, s_ref, smax_ref, m_ref,
                             acc_ref, tq=tq, bias_of=bias_of)


def _moba_attention(q, k, vt, tb, B, T):
    tq = MOBA_TQ
    nq = T // tq
    return pl.pallas_call(
        functools.partial(_moba_kernel, nkb=T // MOBA_L, nq=nq),
        grid=(B, 4, nq // 2),
        in_specs=[pl.BlockSpec((T, 256), lambda b, p, i: (b, p)),
                  pl.BlockSpec((T, 256), lambda b, p, i: (b, p)),
                  pl.BlockSpec((None, 2 * VT_ROWS, T), lambda b, p, i: (b, p, 0)),
                  pl.BlockSpec((2, MOBA_L, BIAS_ROWS), lambda b, p, i: (p, 0, 0))],
        out_specs=pl.BlockSpec((T, 128), lambda b, p, i: (b, p)),
        out_shape=jax.ShapeDtypeStruct((B * T, 512), BF16),
        scratch_shapes=[pltpu.VMEM((128, 256), F32), pltpu.VMEM((2, 128, 256), BF16),
                        pltpu.VMEM((2, 2, tq, 128), BF16)]
        + _attention_t_scratch(tq),
        compiler_params=_cparams(("arbitrary", "arbitrary", "arbitrary")),
        name="moba_attention",
    )(q, k, vt, tb)


def _sb_kernel(q_ref, k_ref, vt_ref, o_ref, *, tq, pairs):
    i = pl.program_id(2)
    lane = lax.broadcasted_iota(jnp.int32, (tq, 128), 1)
    qs = []
    for p in range(pairs):
        q_pair = q_ref[:, 128 * p:128 * (p + 1)]
        qs += [jnp.where(lane < HD, q_pair, 0).astype(BF16), jnp.where(lane >= HD, q_pair, 0).astype(BF16)]
    key = lax.broadcasted_iota(jnp.int32, (tq, tq), 0)
    qry = lax.broadcasted_iota(jnp.int32, (tq, tq), 1)
    past = key < qry
    suffix = jnp.where(qry >= key, -1.0, 0.0).astype(BF16)
    heads = range(2 * pairs)

    def step(kb, carry, boundary):
        start = pl.multiple_of(kb * tq, tq)
        zs = [_dot_nt(k_ref[pl.ds(start, tq), 128 * (h // 2):128 * (h // 2 + 1)], qs[h]) for h in heads]
        parts = []
        for z in zs:
            neg_abs = pltpu.bitcast(pltpu.bitcast(z, jnp.uint32) | jnp.uint32(0x80000000), F32)
            soft = jnp.maximum(z, 0.0) + jnp.log2(1.0 + jnp.exp2(neg_abs))
            if boundary:
                soft = jnp.where(past, soft, 0.0)
            parts.append(soft.astype(BF16))
        sums = [_dot(suffix, part) for part in parts]
        ws = []
        for h in heads:
            w = jnp.exp2(zs[h] + (carry[h][0] + sums[h]))
            ws.append((jnp.where(past, w, 0.0) if boundary else w).astype(BF16))
        return tuple((carry[h][0] + sums[h][0:1], carry[h][1]
                      + _dot(vt_ref[128 * (h // 2):128 * (h // 2 + 1), pl.ds(start, tq)], ws[h])) for h in heads)

    def live(carry):
        top = carry[0][0]
        for c, _ in carry[1:]:
            top = jnp.maximum(top, c)
        return (jnp.max(top) > SB_DONE).astype(jnp.int32)

    def body(state):
        t, _, carry = state
        carry = step(i - 1 - t, carry, False)
        return t + 1, live(carry), carry

    init = tuple((jnp.zeros((1, tq), F32), jnp.zeros((128, tq), F32)) for _ in range(2 * pairs))
    carry = step(i, init, True)
    _, _, carry = lax.while_loop(lambda st: (st[0] < i) & (st[1] > 0), body, (jnp.int32(0), live(carry), carry))
    rows = lax.broadcasted_iota(jnp.int32, (128, tq), 0)
    for p in range(pairs):
        ot = jnp.where(rows < HD, carry[2 * p][1], carry[2 * p + 1][1])
        o_ref[:, 128 * p:128 * (p + 1)] = ot.T.astype(BF16)


def _sb_attention(q, k, vt, B, T, tq=256, pairs=4):
    nq = T // tq
    w = 128 * pairs
    return pl.pallas_call(
        functools.partial(_sb_kernel, tq=tq, pairs=pairs),
        grid=(B, 4 // pairs, nq),
        in_specs=[pl.BlockSpec((tq, w), lambda b, p, i: (b * nq + i, p)),
                  pl.BlockSpec((T, w), lambda b, p, i: (b, p)),
                  pl.BlockSpec((None, w, T), lambda b, p, i: (b, p, 0))],
        out_specs=pl.BlockSpec((tq, w), lambda b, p, i: (b * nq + i, p)),
        out_shape=jax.ShapeDtypeStruct((B * T, 512), BF16),
        compiler_params=_cparams(("arbitrary", "arbitrary", "arbitrary")),
        name="sb_attention",
    )(q, k, vt)


def _router_weights(w_group, b_group, w_router, b_router):
    pad = 128 - N_EXP - N_GROUPS
    wr = jnp.concatenate([w_router, w_group, jnp.zeros((D, pad), F32)], axis=1).T
    hi = wr.astype(BF16)
    lo = (wr - hi.astype(F32)).astype(BF16)
    br = jnp.concatenate([b_router, b_group, jnp.zeros((pad,), F32)])[:, None]
    return jnp.concatenate([hi, lo], axis=0), br


def _ffn(h, oa, ob, wo, g, w_group, b_group, w_router, b_router, w_gate, w_up, w_down, layer, side=()):
    wr, br = _router_weights(w_group, b_group, w_router, b_router)
    hn, u, ri, rg, cnt = _outproj_router(h, oa, ob, wo.astype(BF16), g[None, :], wr, br)
    d1, d2, P, plan = _dispatch_plan(ri, cnt, h.shape[0])
    xs, side = lax.optimization_barrier((_sc_scatter_rows(u, d1, d2, P), side))
    ys = _moe_experts(xs, plan, w_gate, w_up, w_down, layer)
    return hn, _sc_gather_rows(ys, jnp.concatenate([d1, d2])), rg, side


def kernel(x, norm_mix, norm_ffn, norm_final, rel_bias_table, w_in_even, g_mla_q, g_mla_kv, w_mla_qb, w_mla_kvb, swa_sinks, w_out_even, w_in_odd, w_out_odd, moe_w_group, moe_b_group, moe_w_router, moe_b_router, moe_w_gate, moe_w_up, moe_w_down):
    B, T, _ = x.shape
    h = x.reshape(B * T, D)
    tb_swa, tb_moba = _bias_tables(rel_bias_table)

    win, wqa, wqb, wk, wv = _even_weights(w_in_even[0], w_mla_qb[0], w_mla_kvb[0])
    q, k, v, qs8, ks, vst = _proj_even(h, norm_mix[0][None, :], win, g_mla_q[0][None, :], g_mla_kv[0][None, :],
                                       wqa, wqb, wk, wv, _rope_tables(T), T)
    oa = _mla_attention(q, k, v, B, T)
    ob = _swa_attention(qs8, ks, vst, swa_sinks[0], tb_swa, B, T)
    h, y12, rg, (tb_moba, w_odd) = _ffn(
        h, oa, ob, w_out_even[0], norm_ffn[0], moe_w_group[0], moe_b_group[0], moe_w_router[0], moe_b_router[0],
        moe_w_gate, moe_w_up, moe_w_down, 0, side=(tb_moba, _odd_weights(w_in_odd[0])))

    h, vct, vdt, qc, kc, qd, kd = _combine_proj_odd(h, y12, rg, norm_mix[1][None, :], *w_odd, _block_indicator(T), T)
    oc = _moba_attention(qc, kc, vct, tb_moba, B, T)
    od = _sb_attention(qd, kd, vdt, B, T)
    h, y12, rg, _ = _ffn(h, oc, od, w_out_odd[0], norm_ffn[1], moe_w_group[1], moe_b_group[1], moe_w_router[1],
                         moe_b_router[1], moe_w_gate, moe_w_up, moe_w_down, 1)
    out = _combine_final(h, y12, rg, norm_final[None, :])
    return out.reshape(B, T, D)
```

```python
import functools
import math

import numpy as np
import jax
import jax.numpy as jnp
from jax import lax
from jax.experimental import pallas as pl
from jax.experimental.pallas import tpu as pltpu
from jax.experimental.pallas import tpu_sc as plsc

F32 = jnp.float32
BF16 = jnp.bfloat16
NEG = -1e30
EPS = 1e-6

D = 1024
HD = 64
MLA_HEADS, MLA_QR, MLA_KVR, MLA_NOPE, MLA_ROPE, MLA_V = 8, 256, 128, 64, 32, 64
ROPE_THETA = 10000.0
SWA_HEADS, SWA_KV, SWA_WIN = 8, 2, 128
MOBA_L, MOBA_TOPK = 256, 3
REL_BUCKETS, REL_MAX = 32, 2048
N_GROUPS, EPG, N_EXP, D_EXP = 4, 8, 32, 256
BIAS_CONST_FROM = 1792
MOBA_TQ = 2 * MOBA_L
BIAS_PAD = MOBA_L
BIAS_ROWS = BIAS_PAD + BIAS_CONST_FROM + MOBA_TQ
SB_DONE = -160.0
TM = 1024
TM_E = 512
VMEM_LIMIT = 56 * 1024 * 1024
SC_CORES, SC_SUBCORES = 2, 16
SC_CHUNK = 128
HALF = D // 2
VT_ROWS = HD + 16
LOG2E = math.log2(math.e)


def _pack_halves(x):
    lo = pltpu.bitcast(x[:, :HALF].astype(BF16).astype(F32), jnp.uint32) >> 16
    hi = pltpu.bitcast(x[:, HALF:].astype(BF16).astype(F32), jnp.uint32) & jnp.uint32(0xFFFF0000)
    return pltpu.bitcast(lo | hi, jnp.int32)


def _unpack_halves(w):
    w = pltpu.bitcast(w, jnp.uint32)
    return pltpu.bitcast(w << 16, F32), pltpu.bitcast(w & jnp.uint32(0xFFFF0000), F32)


def _cparams(sem):
    return pltpu.CompilerParams(dimension_semantics=sem, vmem_limit_bytes=VMEM_LIMIT)


def _dot(a, b):
    return jnp.dot(a, b, preferred_element_type=F32)


def _dot_nt(a, b):
    return lax.dot_general(a, b, (((1,), (1,)), ((), ())), preferred_element_type=F32)


def _rms(x, g):
    return x * lax.rsqrt(jnp.mean(x * x, axis=-1, keepdims=True) + EPS) * g


def _values_t(vt):
    ones = jnp.ones((VT_ROWS - HD, vt.shape[1]), vt.dtype)
    return jnp.concatenate([blk for h in range(8) for blk in (vt[HD * h:HD * (h + 1)], ones)], axis=0)


def _spread_heads(x, lane_of):
    lane = lax.broadcasted_iota(jnp.int32, (x.shape[0], 128), 1)
    out = []
    for h in range(8):
        pair = x[:, 128 * (h // 2):128 * (h // 2 + 1)]
        if HD * (h % 2) != lane_of(h):
            pair = pltpu.roll(pair, HD, 1)
        out.append(jnp.where((lane >= lane_of(h)) & (lane < lane_of(h) + HD), pair, 0.0))
    return out


def _t5_bucket_np(dist):
    n = np.maximum(dist, 0)
    max_exact = REL_BUCKETS // 2
    nf = np.maximum(n, 1).astype(np.float32)
    large = max_exact + (np.log(nf / np.float32(max_exact)) / np.float32(math.log(REL_MAX / max_exact))
                         * np.float32(REL_BUCKETS - max_exact)).astype(np.int32)
    large = np.minimum(large, REL_BUCKETS - 1)
    return np.where(n < max_exact, n, large).astype(np.int32)


def _bias_kernel(table_ref, bucket_ref, out_ref, *, scale, ranges):
    b = bucket_ref[...]
    for m, (lo, hi) in enumerate(ranges):
        @pl.when(pl.program_id(0) == m)
        def _():
            for h in range(8):
                acc = jnp.full(b.shape, NEG if lo == REL_BUCKETS else table_ref[h, lo] * scale, F32)
                for k in range(lo + 1, min(hi, REL_BUCKETS - 1) + 1):
                    acc = jnp.where(b == k, table_ref[h, k] * scale, acc)
                if hi == REL_BUCKETS:
                    acc = jnp.where(b == REL_BUCKETS, NEG, acc)
                out_ref[h] = acc


def _bias_lookup(rel_table, dist, scale, name, masked=None):
    R, C = dist.shape
    buckets = _t5_bucket_np(dist)
    if masked is not None:
        buckets = np.where(masked, REL_BUCKETS, buckets).astype(np.int32)
    ranges = tuple((int(buckets[:, c:c + 256].min()), int(buckets[:, c:c + 256].max())) for c in range(0, C, 256))
    return pl.pallas_call(
        functools.partial(_bias_kernel, scale=scale, ranges=ranges),
        grid=(C // 256,),
        in_specs=[pl.BlockSpec(memory_space=pltpu.SMEM),
                  pl.BlockSpec((R, 256), lambda m: (0, m))],
        out_specs=pl.BlockSpec((8, R, 256), lambda m: (0, 0, m)),
        out_shape=jax.ShapeDtypeStruct((8, R, C), F32),
        compiler_params=_cparams(("arbitrary",)),
        name=name,
    )(rel_table.T, jnp.asarray(buckets))


def _bias_tables(rel_table):
    d = np.arange(256)[None, :] - np.arange(256)[:, None]
    swa = _bias_lookup(rel_table, d, LOG2E, "bias_swa", masked=(d < 0) | (d >= SWA_WIN))
    moba = _bias_lookup(rel_table, np.arange(BIAS_ROWS)[None, :] - BIAS_PAD - np.arange(MOBA_L)[:, None], LOG2E,
                        "bias_moba")
    return swa, moba


def _proj_even_kernel(h_ref, g_ref, win_ref, gq_ref, gkv_ref, wqa_ref, wqb_ref, wk_ref, wv_ref,
                      cq_ref, sq_ref, ck_ref, sk_ref,
                      q_ref, k_ref, v_ref, qs_ref, ks_ref, vst_ref):
    u = _rms(h_ref[...], g_ref[...]).astype(BF16)
    nq = _rms(_dot(u, win_ref[:, 0:256]), gq_ref[...]).astype(BF16)
    qa = _dot(nq, wqa_ref[...])
    qb = _dot(nq, wqb_ref[...])
    cq, sq = cq_ref[...], sq_ref[...]
    for hh in range(MLA_HEADS):
        sl = slice(128 * hh, 128 * (hh + 1))
        q_ref[:, sl] = (qa[:, sl] * cq + qb[:, sl] * sq).astype(BF16)
    nkv = _rms(_dot(u, win_ref[:, 256:384]), gkv_ref[...]).astype(BF16)
    kn = _dot(nkv, wk_ref[...])
    v_ref[...] = _values_t(_dot_nt(wv_ref[...], nkv)).astype(BF16)
    rest = _dot(u, win_ref[:, 896:1408])
    kr = rest[:, 128:256] * ck_ref[...] + rest[:, 256:384] * sk_ref[...]
    for hh in range(MLA_HEADS):
        sl = slice(128 * hh, 128 * (hh + 1))
        k_ref[:, sl] = (kn[:, sl] + kr).astype(BF16)
    qs = _spread_heads(_dot(u, win_ref[:, 384:896]), lambda hq: HD * (hq // (SWA_HEADS // SWA_KV)))
    for hq in range(SWA_HEADS):
        qs_ref[:, 128 * hq:128 * (hq + 1)] = qs[hq].astype(BF16)
    ks_ref[...] = rest[:, 0:128].astype(BF16)
    vst_ref[...] = rest[:, 384:512].T.astype(BF16)


def _even_weights_kernel(wt_ref, wqb_in_ref, wkvb_ref, win_ref, wqa_ref, wqb_ref, wk_ref, wv_ref):
    o_kr = MLA_QR + MLA_KVR
    o_qs = o_kr + MLA_ROPE
    o_ks = o_qs + SWA_HEADS * HD
    o_vs = o_ks + SWA_KV * HD
    half = MLA_ROPE // 2
    wt = wt_ref[...]
    zero_rows = lambda n: jnp.zeros((n, D), F32)
    zero = lambda a, n: jnp.zeros((a.shape[0], n), F32)
    win_ref[...] = jnp.concatenate([
        wt[0:o_kr], wt[o_qs:o_ks] * (HD ** -0.5 * LOG2E), wt[o_ks:o_vs],
        zero_rows(64), wt[o_kr:o_qs], zero_rows(32),
        zero_rows(64), wt[o_kr + half:o_qs], wt[o_kr:o_kr + half], zero_rows(32),
        wt[o_vs:o_vs + SWA_KV * HD]], axis=0).T.astype(BF16)
    wq, wkv = wqb_in_ref[...], wkvb_ref[...]
    hq, hkv = MLA_NOPE + MLA_ROPE, MLA_NOPE + MLA_V
    qa, qb, kn, vt = [], [], [], []
    for h in range(MLA_HEADS):
        qa += [wq[:, hq * h:hq * (h + 1)], zero(wq, 32)]
        qb += [zero(wq, MLA_NOPE), wq[:, hq * h + MLA_NOPE + half:hq * (h + 1)],
               wq[:, hq * h + MLA_NOPE:hq * h + MLA_NOPE + half], zero(wq, 32)]
        kn += [wkv[:, hkv * h:hkv * h + MLA_NOPE], zero(wkv, MLA_V)]
        vt += [wkv[:, hkv * h + MLA_NOPE:hkv * (h + 1)]]
    wqa_ref[...] = jnp.concatenate(qa, axis=1).astype(BF16)
    wqb_ref[...] = jnp.concatenate(qb, axis=1).astype(BF16)
    wk_ref[...] = jnp.concatenate(kn, axis=1).astype(BF16)
    wv_ref[...] = jnp.concatenate(vt, axis=1).T.astype(BF16)


def _even_weights(w_in, w_qb, w_kvb):
    shapes =[(D, 1408), (MLA_QR, 128 * MLA_HEADS), (MLA_QR, 128 * MLA_HEADS), (MLA_KVR, 128 * MLA_HEADS),
              (MLA_V * MLA_HEADS, MLA_KVR)]
    return pl.pallas_call(
        _even_weights_kernel,
        out_shape=[jax.ShapeDtypeStruct(s, BF16) for s in shapes],
        compiler_params=_cparams(None),
        name="even_weights",
    )(w_in.T, w_qb, w_kvb)


def _rope_tables(T):
    f32 = np.float32
    freqs = f32(ROPE_THETA) ** (-np.arange(0, MLA_ROPE, 2, dtype=f32) / f32(MLA_ROPE))
    ang = np.arange(T, dtype=f32)[:, None] * freqs[None, :]
    cos, sin = np.cos(ang), np.sin(ang)
    z = lambda n: np.zeros((T, n), f32)
    scale = f32((MLA_NOPE + MLA_ROPE) ** -0.5 * LOG2E)
    ck = np.concatenate([z(64), cos, cos, z(32)], axis=1)
    sk = np.concatenate([z(64), -sin, sin, z(32)], axis=1)
    cq = np.concatenate([np.ones((T, 64), f32), cos, cos, z(32)], axis=1) * scale
    sq = sk * scale
    return tuple(jnp.asarray(t, F32) for t in (cq, sq, ck, sk))


def _proj_even(h, g, win, gq, gkv, wqa, wqb, wk, wv, tables, T):
    N = h.shape[0]
    nt = T // TM
    full = lambda a: pl.BlockSpec(a.shape, lambda i: (0,) * a.ndim)
    tab = pl.BlockSpec((TM, 128), lambda i: (i % nt, 0))
    row = lambda w: pl.BlockSpec((TM, w), lambda i: (i, 0))
    outs = (1024, 1024, -8 * VT_ROWS, 1024, 128, -128)
    spec = lambda w: row(w) if w > 0 else pl.BlockSpec((None, -w, TM), lambda i: (i // nt, 0, i % nt))
    shape = lambda w: jax.ShapeDtypeStruct((N, w) if w > 0 else (N // T, -w, T), BF16)
    return pl.pallas_call(
        _proj_even_kernel,
        grid=(N // TM,),
        in_specs=[row(D), full(g), full(win), full(gq), full(gkv), full(wqa), full(wqb), full(wk), full(wv),
                  tab, tab, tab, tab],
        out_specs=[spec(w) for w in outs],
        out_shape=[shape(w) for w in outs],
        compiler_params=_cparams(("arbitrary",)),
        name="proj_even",
    )(h, g, win, gq, gkv, wqa, wqb, wk, wv, *tables)


def _finish_t(accs):
    ot = jnp.concatenate([acc[:HD] / acc[HD:HD + 1] for acc in accs], axis=0)
    return ot.T.astype(BF16)


def _causal_attention_pair_t(ia, nq, q_of, k_ref, vt_ref, o_ref, s_ref, smax_ref, m_ref, acc_ref, *, tq,
                             bias_of=None):
    nh = s_ref.shape[1]
    first_query = (ia * tq, (nq - 1 - ia) * tq)
    n_items = nq + 1
    assert n_items % 2 == 1
    key = lax.broadcasted_iota(jnp.int32, (tq, tq), 0)
    qry = lax.broadcasted_iota(jnp.int32, (tq, tq), 1)

    def item(p):
        if isinstance(p, int) and p < 2:
            return p, first_query[p], first_query[p]
        t = jnp.where(p - 2 >= ia, 1, 0)
        return t, jnp.where(t == 1, first_query[1], first_query[0]), (p - 2 - t * ia) * tq

    def scores_to(slot, p):
        t, q0, start = item(p)
        start = pl.multiple_of(start, tq)
        for j in range(nh):
            st = _dot_nt(k_ref[pl.ds(start, tq), 128 * j:128 * (j + 1)], q_of(t, q0, j))
            if bias_of is not None:
                st = st + bias_of(start, q0, j)
            if isinstance(p, int) and p < 2:
                st = jnp.where(key <= qry, st, NEG)
            s_ref[slot, j] = st
            smax_ref[slot, j] = jnp.max(st, axis=0, keepdims=True)

    def update_from(slot, p):
        t, _, start = item(p)
        start = pl.multiple_of(start, tq)
        m_new = [jnp.maximum(m_ref[t, j], smax_ref[slot, j]) for j in range(nh)]
        pts = [jnp.exp2(s_ref[slot, j] - m_new[j]).astype(BF16) for j in range(nh)]
        for j in range(nh):
            acc_ref[t, j] = (jnp.exp2(m_ref[t, j] - m_new[j]) * acc_ref[t, j]
                             + _dot(vt_ref[VT_ROWS * j:VT_ROWS * (j + 1), pl.ds(start, tq)], pts[j]))
            m_ref[t, j] = m_new[j]

    def pair(k):
        scores_to(1, k + 1)
        update_from(0, k)
        scores_to(0, k + 2)
        update_from(1, k + 1)

    def quad(c, _):
        pair(2 + 4 * c)
        pair(4 + 4 * c)
        return 0

    m_ref[...] = jnp.full(m_ref.shape, NEG, F32)
    acc_ref[...] = jnp.zeros(acc_ref.shape, F32)
    scores_to(0, 0)
    pair(0)
    later_pairs = (n_items - 1) // 2 - 1
    lax.fori_loop(0, later_pairs // 2, quad, 0)
    if later_pairs % 2:
        pair(n_items - 3)
    update_from(0, n_items - 1)
    for t in range(2):
        o_ref[pl.ds(pl.multiple_of(first_query[t], tq), tq), :] = _finish_t([acc_ref[t, j] for j in range(nh)])


def _attention_t_scratch(tq):
    return [pltpu.VMEM((2, 2, tq, tq), F32), pltpu.VMEM((2, 2, 1, tq), F32), pltpu.VMEM((2, 2, 1, tq), F32),
            pltpu.VMEM((2, 2, VT_ROWS, tq), F32)]


def _mla_kernel(q_ref, k_ref, vt_ref, o_ref, s_ref, smax_ref, m_ref, acc_ref, *, tq, nq):
    def q_of(t, q0, j):
        return q_ref[pl.ds(pl.multiple_of(q0, tq), tq), 128 * j:128 * (j + 1)]

    _causal_attention_pair_t(pl.program_id(2), nq, q_of, k_ref, vt_ref, o_ref, s_ref, smax_ref, m_ref, acc_ref,
                             tq=tq)


def _mla_attention(q, k, vt, B, T, tq=512):
    nq = T // tq
    return pl.pallas_call(
        functools.partial(_mla_kernel, tq=tq, nq=nq),
        grid=(B, MLA_HEADS // 2, nq // 2),
        in_specs=[pl.BlockSpec((T, 256), lambda b, p, i: (b, p)),
                  pl.BlockSpec((T, 256), lambda b, p, i: (b, p)),
                  pl.BlockSpec((None, 2 * VT_ROWS, T), lambda b, p, i: (b, p, 0))],
        out_specs=pl.BlockSpec((T, 128), lambda b, p, i: (b, p)),
        out_shape=jax.ShapeDtypeStruct((B * T, MLA_HEADS * MLA_V), BF16),
        scratch_shapes=_attention_t_scratch(tq),
        compiler_params=_cparams(("arbitrary", "arbitrary", "arbitrary")),
        name="mla_attention",
    )(q, k, vt)


def _swa_kernel(sink_ref, q_ref, k_ref, vt_ref, tb_ref, o_ref, *, tq):
    i = pl.program_id(1)
    G = SWA_HEADS // SWA_KV
    W = 2 * SWA_WIN
    head = lax.broadcasted_iota(jnp.int32, (1, G * 128), 1) >> 7
    sinks = []
    for c in range(SWA_KV):
        sink = jnp.zeros((1, G * 128), F32)
        for g in range(G):
            sink = jnp.where(head == g, sink_ref[G * c + g], sink)
        sinks.append(sink)
    scores, values = [], []
    for r in range(tq // 128):
        qstart = i * tq + 128 * r
        kstart = pl.multiple_of(jnp.maximum(qstart - SWA_WIN, 0), 128)
        off = pl.multiple_of(qstart - kstart, 128)
        kw = k_ref[pl.ds(kstart, W), :]
        for c in range(SWA_KV):
            qg = jnp.concatenate([q_ref[128 * r:128 * (r + 1), 128 * (G * c + g):128 * (G * c + g + 1)]
                                  for g in range(G)], axis=0)
            bias = jnp.concatenate([tb_ref[G * c + g, :, pl.ds(off, 128)] for g in range(G)], axis=1)
            scores.append(_dot_nt(kw, qg) + bias)
            values.append(vt_ref[HD * c:HD * (c + 1), pl.ds(kstart, W)])
    probs = []
    for n, s in enumerate(scores):
        sink = sinks[n % SWA_KV]
        m = jnp.maximum(jnp.max(s, axis=0, keepdims=True), sink)
        p = jnp.exp2(s - m)
        probs.append((p.astype(BF16), jnp.sum(p, axis=0, keepdims=True) + jnp.exp2(sink - m)))
    for r in range(tq // 128):
        outs = []
        for c in range(SWA_KV):
            p, denom = probs[SWA_KV * r + c]
            ot = _dot(values[SWA_KV * r + c], p) / denom
            outs += [ot[:, 128 * g:128 * (g + 1)] for g in range(G)]
        o_ref[128 * r:128 * (r + 1), :] = jnp.concatenate(outs, axis=0).T.astype(BF16)


def _swa_attention(q8, k, vt, sinks, tb, B, T, tq=512):
    nq = T // tq
    return pl.pallas_call(
        functools.partial(_swa_kernel, tq=tq),
        grid=(B, nq),
        in_specs=[pl.BlockSpec(memory_space=pltpu.SMEM),
                  pl.BlockSpec((tq, 1024), lambda b, i: (b * nq + i, 0)),
                  pl.BlockSpec((T, 128), lambda b, i: (b, 0)),
                  pl.BlockSpec((None, 128, T), lambda b, i: (b, 0, 0)),
                  pl.BlockSpec((8, 256, 256), lambda b, i: (0, 0, 0))],
        out_specs=pl.BlockSpec((tq, 512), lambda b, i: (b * nq + i, 0)),
        out_shape=jax.ShapeDtypeStruct((B * T, SWA_HEADS * HD), BF16),
        compiler_params=_cparams(("arbitrary", "arbitrary")),
        name="swa_attention",
    )(sinks * LOG2E, q8, k, vt, tb)


def _outproj_router_kernel(h_ref, oa_ref, ob_ref, wo_ref, g_ref, wr_ref, br_ref,
                           hn_ref, u_ref, ri_ref, rg_ref, cnt_ref, base_ref):
    i = pl.program_id(0)

    @pl.when(i == 0)
    def _():
        base_ref[...] = jnp.zeros_like(base_ref)

    hn = h_ref[...] + _dot(oa_ref[...], wo_ref[0:512, :]) + _dot(ob_ref[...], wo_ref[512:1024, :])
    hn_ref[...] = hn
    u = _rms(hn, g_ref[...])
    u_ref[...] = _pack_halves(u)
    logits = _dot_nt(wr_ref[...], u.astype(BF16)) + br_ref[...]
    tm = logits.shape[1]
    sub = lax.broadcasted_iota(jnp.int32, (8, tm), 0).astype(F32)
    big = 1e6
    isg = sub < N_GROUPS
    gl = jnp.where(isg, logits[N_EXP:N_EXP + 8], NEG)
    gmax = jnp.max(gl, axis=0, keepdims=True)
    gsel = jnp.min(jnp.where(gl == gmax, sub, big), axis=0, keepdims=True)
    g_gate = 1.0 / jnp.sum(jnp.where(isg, jnp.exp(gl - gmax), 0.0), axis=0, keepdims=True)
    el = jnp.zeros((EPG, tm), F32)
    for grp in range(N_GROUPS):
        el = jnp.where(gsel == grp, logits[EPG * grp:EPG * (grp + 1)], el)
    m1 = jnp.max(el, axis=0, keepdims=True)
    i1 = jnp.min(jnp.where(el == m1, sub, big), axis=0, keepdims=True)
    el2 = jnp.where(sub == i1, NEG, el)
    m2 = jnp.max(el2, axis=0, keepdims=True)
    i2 = jnp.min(jnp.where(el2 == m2, sub, big), axis=0, keepdims=True)
    e1 = gsel * EPG + i1
    e2 = gsel * EPG + i2
    r = jnp.exp(m2 - m1)
    gate1 = g_gate / (1.0 + r)
    gate2 = g_gate * r / (1.0 + r)
    rows = lax.broadcasted_iota(jnp.int32, (128, tm), 0).astype(F32)
    oh1 = (rows == e1).astype(F32)
    oh2 = (rows == e2).astype(F32)
    oh = oh1 + oh2
    t_row = lax.broadcasted_iota(jnp.int32, (tm, tm), 0)
    t_col = lax.broadcasted_iota(jnp.int32, (tm, tm), 1)
    earlier = jnp.where(t_row < t_col, 1.0, 0.0).astype(BF16)
    prefix = _dot(oh.astype(BF16), earlier) + base_ref[:, 0:1]
    rank1 = jnp.sum(oh1 * prefix, axis=0, keepdims=True)
    rank2 = jnp.sum(oh2 * prefix, axis=0, keepdims=True)
    base_ref[...] = base_ref[...] + jnp.sum(oh, axis=1, keepdims=True)
    cnt_ref[...] = base_ref[...]
    ri_ref[...] = jnp.where(sub == 0, e1, jnp.where(sub == 1, e2, jnp.where(sub == 2, rank1,
                            jnp.where(sub == 3, rank2, 0.0))))
    rg_ref[...] = jnp.where(rows == 0, gate1, jnp.where(rows == 1, gate2, 0.0)).T


def _outproj_router(h, oa, ob, wo, g, wr, br):
    N = h.shape[0]
    full = lambda a: pl.BlockSpec(a.shape, lambda i: (0,) * a.ndim)
    row = lambda w: pl.BlockSpec((TM, w), lambda i: (i, 0))
    return pl.pallas_call(
        _outproj_router_kernel,
        grid=(N // TM,),
        in_specs=[row(D), row(512), row(512), full(wo), full(g), full(wr), full(br)],
        out_specs=[row(D), row(HALF), pl.BlockSpec((8, TM), lambda i: (0, i)), row(128),
                   pl.BlockSpec((128, 128), lambda i: (0, 0))],
        out_shape=[jax.ShapeDtypeStruct((N, D), F32), jax.ShapeDtypeStruct((N, HALF), jnp.int32),
                   jax.ShapeDtypeStruct((8, N), F32), jax.ShapeDtypeStruct((N, 128), F32),
                   jax.ShapeDtypeStruct((128, 128), F32)],
        scratch_shapes=[pltpu.VMEM((128, 128), F32)],
        compiler_params=_cparams(("arbitrary",)),
        name="outproj_router",
    )(h, oa, ob, wo, g, wr, br)


def _moe_kernel(te_ref, tv_ref, run_ref, nx_ref, x_ref, wg_hbm, wu_hbm, wd_hbm, o_ref,
                wg_buf, wu_buf, wd_buf, sems, wgu_s, wd_s, *, layer):
    i = pl.program_id(0)
    valid = tv_ref[i]

    def weight_copies(e, slot):
        return (pltpu.make_async_copy(wg_hbm.at[layer, e], wg_buf.at[slot], sems.at[slot, 0]),
                pltpu.make_async_copy(wu_hbm.at[layer, e], wu_buf.at[slot], sems.at[slot, 1]),
                pltpu.make_async_copy(wd_hbm.at[layer, e], wd_buf.at[slot], sems.at[slot, 2]))

    @pl.when(run_ref[i] > 0)
    def _():
        slot = run_ref[i] - 1

        @pl.when(i == 0)
        def _():
            for c in weight_copies(te_ref[i], slot):
                c.start()

        for c in weight_copies(te_ref[i], slot):
            c.wait()
        wgu_s[:, :D_EXP] = wg_buf[slot].astype(BF16)
        wgu_s[:, D_EXP:] = wu_buf[slot].astype(BF16)
        wd_s[...] = wd_buf[slot].astype(BF16)

        @pl.when(nx_ref[i] >= 0)
        def _():
            for c in weight_copies(nx_ref[i], 1 - slot):
                c.start()

    @pl.when(valid > 0)
    def _():
        rows = lax.broadcasted_iota(jnp.int32, x_ref.shape, 0)
        lo, hi = _unpack_halves(jnp.where(rows < valid, x_ref[...], 0))
        hgu = _dot(lo.astype(BF16), wgu_s[:HALF, :]) + _dot(hi.astype(BF16), wgu_s[HALF:, :])
        hg, hu = hgu[:, :D_EXP], hgu[:, D_EXP:]
        a = hg * (1.0 / (1.0 + jnp.exp(-hg))) * hu
        o_ref[...] = _pack_halves(_dot(a.astype(BF16), wd_s[...]))

    @pl.when(valid <= 0)
    def _():
        o_ref[...] = jnp.zeros_like(o_ref)


def _moe_experts(xs, plan, w_gate, w_up, w_down, layer):
    P = xs.shape[0]
    hbm = pl.BlockSpec(memory_space=pl.ANY)
    grid_spec = pltpu.PrefetchScalarGridSpec(
        num_scalar_prefetch=4,
        grid=(P // TM_E,),
        in_specs=[pl.BlockSpec((TM_E, HALF), lambda i, *_: (i, 0)), hbm, hbm, hbm],
        out_specs=pl.BlockSpec((TM_E, HALF), lambda i, *_: (i, 0)),
        scratch_shapes=[pltpu.VMEM((2, D, D_EXP), F32), pltpu.VMEM((2, D, D_EXP), F32),
                        pltpu.VMEM((2, D_EXP, D), F32), pltpu.SemaphoreType.DMA((2, 3)),
                        pltpu.VMEM((D, 2 * D_EXP), BF16), pltpu.VMEM((D_EXP, D), BF16)],
    )
    return pl.pallas_call(
        functools.partial(_moe_kernel, layer=layer),
        grid_spec=grid_spec,
        out_shape=jax.ShapeDtypeStruct((P, HALF), jnp.int32),
        compiler_params=_cparams(("arbitrary",)),
        name="moe_experts",
    )(*plan, xs, w_gate, w_up, w_down)


def _sc_mesh():
    return plsc.VectorSubcoreMesh(core_axis_name="c", subcore_axis_name="s",
                                  num_cores=SC_CORES, num_subcores=SC_SUBCORES)


def _sc_scatter_rows(src, d1, d2, P):
    N, W = src.shape
    per_w = N // (SC_CORES * SC_SUBCORES)

    @functools.partial(pl.kernel, mesh=_sc_mesh(), out_type=jax.ShapeDtypeStruct((P, W), src.dtype),
                       scratch_types=[pltpu.VMEM((SC_CHUNK,), jnp.int32), pltpu.VMEM((SC_CHUNK, W), src.dtype)],
                       name="sc_scatter_rows")
    def k(src_hbm, d1_hbm, d2_hbm, out_hbm, idx_v, rows_v):
        wid = lax.axis_index("s") * SC_CORES + lax.axis_index("c")

        @pl.loop(0, per_w // SC_CHUNK)
        def _(c):
            off = wid * per_w + c * SC_CHUNK
            pltpu.sync_copy(src_hbm.at[pl.ds(off, SC_CHUNK)], rows_v)
            pltpu.sync_copy(d1_hbm.at[pl.ds(off, SC_CHUNK)], idx_v)
            pltpu.sync_copy(rows_v, out_hbm.at[idx_v])
            pltpu.sync_copy(d2_hbm.at[pl.ds(off, SC_CHUNK)], idx_v)
            pltpu.sync_copy(rows_v, out_hbm.at[idx_v])

    return k(src, d1, d2)


def _sc_gather_rows(table, idx):
    B, W = idx.shape[0], table.shape[1]
    per_w = B // (SC_CORES * SC_SUBCORES)

    @functools.partial(pl.kernel, mesh=_sc_mesh(), out_type=jax.ShapeDtypeStruct((B, W), table.dtype),
                       scratch_types=[pltpu.VMEM((SC_CHUNK,), jnp.int32), pltpu.VMEM((SC_CHUNK, W), table.dtype)],
                       name="sc_gather_rows")
    def k(table_hbm, idx_hbm, out_hbm, idx_v, rows_v):
        wid = lax.axis_index("s") * SC_CORES + lax.axis_index("c")

        @pl.loop(0, per_w // SC_CHUNK)
        def _(c):
            off = wid * per_w + c * SC_CHUNK
            pltpu.sync_copy(idx_hbm.at[pl.ds(off, SC_CHUNK)], idx_v)
            pltpu.sync_copy(table_hbm.at[idx_v], rows_v)
            pltpu.sync_copy(rows_v, out_hbm.at[pl.ds(off, SC_CHUNK)])

    return k(table, idx)


def _dest_kernel(offs_ref, ri_ref, d_ref):
    ri = ri_ref[...]
    base = jnp.zeros(ri.shape, F32)
    for e in range(N_EXP):
        base = jnp.where(ri == e, offs_ref[e].astype(F32), base)
    d_ref[...] = (base + pltpu.roll(ri, 6, 0)).astype(jnp.int32)


def _dispatch_plan(ri, cnt, N):
    counts = cnt[:N_EXP, 0].astype(jnp.int32)
    padded = ((counts + TM_E - 1) // TM_E) * TM_E
    ends = jnp.cumsum(padded)
    offs = ends - padded
    take = lambda t, i: t.at[i].get(mode="promise_in_bounds")
    dest = pl.pallas_call(
        _dest_kernel,
        in_specs=[pl.BlockSpec(memory_space=pltpu.SMEM), pl.BlockSpec((8, N), lambda: (0, 0))],
        out_specs=pl.BlockSpec((8, N), lambda: (0, 0)),
        out_shape=jax.ShapeDtypeStruct((8, N), jnp.int32),
        name="dispatch_rows",
    )(offs, ri)
    d1, d2 = dest[0], dest[1]
    P = 2 * N + N_EXP * TM_E
    tile_start = jnp.arange(P // TM_E, dtype=jnp.int32) * TM_E
    tile_expert = jnp.minimum(jnp.sum((ends[None, :] <= tile_start[:, None]).astype(jnp.int32), axis=1), N_EXP - 1)
    tile_valid = jnp.clip(take(offs + counts, tile_expert) - tile_start, 0, TM_E)
    n_tiles = P // TM_E
    tile = jnp.arange(n_tiles, dtype=jnp.int32)
    starts = (tile_valid > 0) & ((tile == 0) | (tile_expert != jnp.roll(tile_expert, 1)))
    slot = (jnp.cumsum(starts.astype(jnp.int32)) - 1) % 2
    run_code = jnp.where(starts, slot + 1, 0).astype(jnp.int32)
    start_at = jnp.where(starts, tile, n_tiles)
    next_start = jnp.concatenate([lax.cummin(start_at, reverse=True)[1:], jnp.full((1,), n_tiles, jnp.int32)])
    next_expert = jnp.where(next_start < n_tiles, take(tile_expert, jnp.minimum(next_start, n_tiles - 1)), -1)
    return d1, d2, P, (tile_expert, tile_valid, run_code, next_expert.astype(jnp.int32))


def _combine(h_ref, y1_ref, y2_ref, rg_ref):
    rg = rg_ref[...]
    lo1, hi1 = _unpack_halves(y1_ref[...])
    lo2, hi2 = _unpack_halves(y2_ref[...])
    g1, g2 = rg[:, 0:1], rg[:, 1:2]
    return h_ref[...] + jnp.concatenate([g1 * lo1 + g2 * lo2, g1 * hi1 + g2 * hi2], axis=1)


ODD_WIDTHS = (1024, 1024, 512, 512)


def _combine_proj_odd_kernel(h_ref, y1_ref, y2_ref, rg_ref, g_ref, win_ref, wvt_ref, ind_ref, hn_ref, vt_ref,
                             vdt_ref, *out_refs):
    hn = _combine(h_ref, y1_ref, y2_ref, rg_ref)
    hn_ref[...] = hn
    u = _rms(hn, g_ref[...]).astype(BF16)
    vt_ref[...] = _values_t(_dot_nt(wvt_ref[0:512, :], u)).astype(BF16)
    vdt_ref[...] = _dot_nt(wvt_ref[512:1024, :], u).astype(BF16)
    qc_ref, kc_ref, qd_ref, kd_ref = out_refs
    qc = _spread_heads(_dot(u, win_ref[:, 0:512]), lambda h: 0)
    kc = _spread_heads(_dot(u, win_ref[:, 512:1024]), lambda h: 0)
    for hh in range(8):
        qc_ref[:, 128 * hh:128 * (hh + 1)] = qc[hh].astype(BF16)
        kc_ref[:, 128 * hh:128 * (hh + 1)] = (kc[hh] + ind_ref[...]).astype(BF16)
    qd_ref[...] = _dot(u, win_ref[:, 1024:1536]).astype(BF16)
    kd_ref[...] = _dot(u, win_ref[:, 1536:2048]).astype(BF16)


def _odd_weights_kernel(w_ref, win_ref, wvt_ref):
    win_ref[:, 0:512] = (w_ref[:, 0:512] * (HD ** -0.5 * LOG2E)).astype(BF16)
    win_ref[:, 512:1024] = w_ref[:, 512:1024].astype(BF16)
    wvt_ref[...] = w_ref[:, 1024:1536].T.astype(BF16)


def _odd_weights(w_in):
    return pl.pallas_call(
        _odd_weights_kernel,
        grid=(2,),
        in_specs=[pl.BlockSpec((D, 1536), lambda i: (0, i))],
        out_specs=[pl.BlockSpec((D, 1024), lambda i: (0, i)), pl.BlockSpec((512, D), lambda i: (i, 0))],
        out_shape=[jax.ShapeDtypeStruct((D, 2048), BF16), jax.ShapeDtypeStruct((1024, D), BF16)],
        compiler_params=_cparams(("arbitrary",)),
        name="odd_weights",
    )(w_in)


def _block_indicator(T):
    assert T // MOBA_L <= 64
    ind = np.zeros((T, 128), np.float32)
    ind[np.arange(T), 64 + np.arange(T) // MOBA_L] = 1.0
    return jnp.asarray(ind)


def _y_specs(N):
    return [pl.BlockSpec((TM, HALF), lambda i: (i, 0)), pl.BlockSpec((TM, HALF), lambda i: (i + N // TM, 0))]


def _combine_proj_odd(h, y12, rg, g, win, wvt, ind, T):
    N = h.shape[0]
    nt = T // TM
    full = lambda a: pl.BlockSpec(a.shape, lambda i: (0,) * a.ndim)
    row = lambda w: pl.BlockSpec((TM, w), lambda i: (i, 0))
    return pl.pallas_call(
        _combine_proj_odd_kernel,
        grid=(N // TM,),
        in_specs=[row(D)] + _y_specs(N) + [row(128), full(g), full(win), full(wvt),
                  pl.BlockSpec((TM, 128), lambda i: (i % nt, 0))],
        out_specs=[row(D), pl.BlockSpec((None, 8 * VT_ROWS, TM), lambda i: (i // nt, 0, i % nt)),
                   pl.BlockSpec((None, 512, TM), lambda i: (i // nt, 0, i % nt))]
        + [row(w) for w in ODD_WIDTHS],
        out_shape=[jax.ShapeDtypeStruct((N, D), F32), jax.ShapeDtypeStruct((N // T, 8 * VT_ROWS, T), BF16),
                   jax.ShapeDtypeStruct((N // T, 512, T), BF16)]
        + [jax.ShapeDtypeStruct((N, w), BF16) for w in ODD_WIDTHS],
        compiler_params=_cparams(("arbitrary",)),
        name="combine_proj_odd",
    )(h, y12, y12, rg, g, win, wvt, ind)


def _combine_final_kernel(h_ref, y1_ref, y2_ref, rg_ref, g_ref, o_ref):
    o_ref[...] = _rms(_combine(h_ref, y1_ref, y2_ref, rg_ref), g_ref[...])


def _combine_final(h, y12, rg, g):
    N = h.shape[0]
    row = lambda w: pl.BlockSpec((TM, w), lambda i: (i, 0))
    return pl.pallas_call(
        _combine_final_kernel,
        grid=(N // TM,),
        in_specs=[row(D)] + _y_specs(N) + [row(128), pl.BlockSpec((1, D), lambda i: (0, 0))],
        out_specs=row(D),
        out_shape=jax.ShapeDtypeStruct((N, D), F32),
        compiler_params=_cparams(("arbitrary",)),
        name="combine_final",
    )(h, y12, y12, rg, g)


def _moba_kernel(q_ref, k_ref, vt_ref, tb_ref, o_ref, kmean_ref, ksplit_ref, qa_ref, s_ref, smax_ref, m_ref, acc_ref,
                 *, nkb, nq):
    ia = pl.program_id(2)
    L, tq = MOBA_L, MOBA_TQ

    @pl.when(ia == 0)
    def _():
        kmean_ref[...] = jnp.zeros_like(kmean_ref)
        for n in range(nkb):
            kmean_ref[64 + n:65 + n, :] = jnp.sum(k_ref[n * L:(n + 1) * L, :].astype(F32), axis=0,
                                                  keepdims=True) * (1.0 / L)
        hi = kmean_ref[...].astype(BF16)
        ksplit_ref[0] = hi
        ksplit_ref[1] = (kmean_ref[...] - hi.astype(F32)).astype(BF16)

    nb = -(-nkb // 8) * 8
    lane = lax.broadcasted_iota(jnp.int32, (tq, 128), 1)
    blk = lax.broadcasted_iota(jnp.int32, (nb, tq), 0)
    half = jnp.where(lax.broadcasted_iota(jnp.int32, (nb, tq), 1) >= L, 1, 0)
    blk_f = blk.astype(F32)
    for t, tile in enumerate((ia, nq - 1 - ia)):
        own = 2 * tile + half
        for j in range(2):
            qj = q_ref[pl.ds(pl.multiple_of(tile * tq, tq), tq), 128 * j:128 * (j + 1)]
            gsc = (_dot_nt(ksplit_ref[0, 64:64 + nb, 128 * j:128 * (j + 1)], qj)
                   + _dot_nt(ksplit_ref[1, 64:64 + nb, 128 * j:128 * (j + 1)], qj))
            g = jnp.where(blk < own, gsc, NEG)
            allowed = jnp.where(blk == own, 1.0, 0.0)
            for _ in range(MOBA_TOPK):
                mx = jnp.max(g, axis=0, keepdims=True)
                idx = jnp.min(jnp.where(g == mx, blk_f, 1e6), axis=0, keepdims=True)
                pick = blk_f == idx
                allowed = jnp.maximum(allowed, jnp.where(pick, jnp.where(mx > 0.5 * NEG, 1.0, 0.0), 0.0))
                g = jnp.where(pick, 2.0 * NEG, g)
            mask = jnp.concatenate([jnp.zeros((64, tq), F32), jnp.where(allowed > 0.5, 0.0, NEG),
                                    jnp.zeros((64 - nb, tq), F32)], axis=0)
            qa_ref[t, j] = jnp.where(lane < HD, qj, mask.T.astype(BF16))

    def bias_of(start, q0, j):
        parts = []
        for c in range(tq // L):
            off = jnp.minimum(q0 - start - c * L, BIAS_CONST_FROM) + BIAS_PAD
            parts.append(tb_ref[j, :, pl.ds(pl.multiple_of(off, L), tq)])
        return jnp.concatenate(parts, axis=0)

    _causal_attention_pair_t(ia, nq, lambda t, q0, j: qa_ref[t, j], k_ref, vt_ref, o_ref, s_ref, smax_ref, m_ref,
                             acc_ref, tq=tq, bias_of=bias_of)


def _moba_attention(q, k, vt, tb, B, T):
    tq = MOBA_TQ
    nq = T // tq
    return pl.pallas_call(
        functools.partial(_moba_kernel, nkb=T // MOBA_L, nq=nq),
        grid=(B, 4, nq // 2),
        in_specs=[pl.BlockSpec((T, 256), lambda b, p, i: (b, p)),
                  pl.BlockSpec((T, 256), lambda b, p, i: (b, p)),
                  pl.BlockSpec((None, 2 * VT_ROWS, T), lambda b, p, i: (b, p, 0)),
                  pl.BlockSpec((2, MOBA_L, BIAS_ROWS), lambda b, p, i: (p, 0, 0))],
        out_specs=pl.BlockSpec((T, 128), lambda b, p, i: (b, p)),
        out_shape=jax.ShapeDtypeStruct((B * T, 512), BF16),
        scratch_shapes=[pltpu.VMEM((128, 256), F32), pltpu.VMEM((2, 128, 256), BF16),
                        pltpu.VMEM((2, 2, tq, 128), BF16)]
        + _attention_t_scratch(tq),
        compiler_params=_cparams(("arbitrary", "arbitrary", "arbitrary")),
        name="moba_attention",
    )(q, k, vt, tb)


def _sb_kernel(q_ref, k_ref, vt_ref, o_ref, *, tq, pairs):
    i = pl.program_id(2)
    lane = lax.broadcasted_iota(jnp.int32, (tq, 128), 1)
    qs = []
    for p in range(pairs):
        q_pair = q_ref[:, 128 * p:128 * (p + 1)]
        qs += [jnp.where(lane < HD, q_pair, 0).astype(BF16), jnp.where(lane >= HD, q_pair, 0).astype(BF16)]
    key = lax.broadcasted_iota(jnp.int32, (tq, tq), 0)
    qry = lax.broadcasted_iota(jnp.int32, (tq, tq), 1)
    past = key < qry
    suffix = jnp.where(qry >= key, -1.0, 0.0).astype(BF16)
    heads = range(2 * pairs)

    def step(kb, carry, boundary):
        start = pl.multiple_of(kb * tq, tq)
        zs = [_dot_nt(k_ref[pl.ds(start, tq), 128 * (h // 2):128 * (h // 2 + 1)], qs[h]) for h in heads]
        parts = []
        for z in zs:
            neg_abs = pltpu.bitcast(pltpu.bitcast(z, jnp.uint32) | jnp.uint32(0x80000000), F32)
            soft = jnp.maximum(z, 0.0) + jnp.log2(1.0 + jnp.exp2(neg_abs))
            if boundary:
                soft = jnp.where(past, soft, 0.0)
            parts.append(soft.astype(BF16))
        sums = [_dot(suffix, part) for part in parts]
        ws = []
        for h in heads:
            w = jnp.exp2(zs[h] + (carry[h][0] + sums[h]))
            ws.append((jnp.where(past, w, 0.0) if boundary else w).astype(BF16))
        return tuple((carry[h][0] + sums[h][0:1], carry[h][1]
                      + _dot(vt_ref[128 * (h // 2):128 * (h // 2 + 1), pl.ds(start, tq)], ws[h])) for h in heads)

    def live(carry):
        top = carry[0][0]
        for c, _ in carry[1:]:
            top = jnp.maximum(top, c)
        return (jnp.max(top) > SB_DONE).astype(jnp.int32)

    def body(state):
        t, _, carry = state
        carry = step(i - 1 - t, carry, False)
        return t + 1, live(carry), carry

    init = tuple((jnp.zeros((1, tq), F32), jnp.zeros((128, tq), F32)) for _ in range(2 * pairs))
    carry = step(i, init, True)
    _, _, carry = lax.while_loop(lambda st: (st[0] < i) & (st[1] > 0), body, (jnp.int32(0), live(carry), carry))
    rows = lax.broadcasted_iota(jnp.int32, (128, tq), 0)
    for p in range(pairs):
        ot = jnp.where(rows < HD, carry[2 * p][1], carry[2 * p + 1][1])
        o_ref[:, 128 * p:128 * (p + 1)] = ot.T.astype(BF16)


def _sb_attention(q, k, vt, B, T, tq=256, pairs=4):
    nq = T // tq
    w = 128 * pairs
    return pl.pallas_call(
        functools.partial(_sb_kernel, tq=tq, pairs=pairs),
        grid=(B, 4 // pairs, nq),
        in_specs=[pl.BlockSpec((tq, w), lambda b, p, i: (b * nq + i, p)),
                  pl.BlockSpec((T, w), lambda b, p, i: (b, p)),
                  pl.BlockSpec((None, w, T), lambda b, p, i: (b, p, 0))],
        out_specs=pl.BlockSpec((tq, w), lambda b, p, i: (b * nq + i, p)),
        out_shape=jax.ShapeDtypeStruct((B * T, 512), BF16),
        compiler_params=_cparams(("arbitrary", "arbitrary", "arbitrary")),
        name="sb_attention",
    )(q, k, vt)


def _router_weights(w_group, b_group, w_router, b_router):
    pad = 128 - N_EXP - N_GROUPS
    wr = jnp.concatenate([w_router, w_group, jnp.zeros((D, pad), F32)], axis=1).T
    br = jnp.concatenate([b_router, b_group, jnp.zeros((pad,), F32)])[:, None]
    return wr.astype(BF16), br


def _ffn(h, oa, ob, wo, g, w_group, b_group, w_router, b_router, w_gate, w_up, w_down, layer):
    wr, br = _router_weights(w_group, b_group, w_router, b_router)
    hn, u, ri, rg, cnt = _outproj_router(h, oa, ob, wo.astype(BF16), g[None, :], wr, br)
    d1, d2, P, plan = _dispatch_plan(ri, cnt, h.shape[0])
    xs = _sc_scatter_rows(u, d1, d2, P)
    ys = _moe_experts(xs, plan, w_gate, w_up, w_down, layer)
    return hn, _sc_gather_rows(ys, jnp.concatenate([d1, d2])), rg


def kernel(x, norm_mix, norm_ffn, norm_final, rel_bias_table, w_in_even, g_mla_q, g_mla_kv, w_mla_qb, w_mla_kvb, swa_sinks, w_out_even, w_in_odd, w_out_odd, moe_w_group, moe_b_group, moe_w_router, moe_b_router, moe_w_gate, moe_w_up, moe_w_down):
    B, T, _ = x.shape
    h = x.reshape(B * T, D)
    tb_swa, tb_moba = _bias_tables(rel_bias_table)

    win, wqa, wqb, wk, wv = _even_weights(w_in_even[0], w_mla_qb[0], w_mla_kvb[0])
    q, k, v, qs8, ks, vst = _proj_even(h, norm_mix[0][None, :], win, g_mla_q[0][None, :], g_mla_kv[0][None, :],
                                       wqa, wqb, wk, wv, _rope_tables(T), T)
    oa = _mla_attention(q, k, v, B, T)
    ob = _swa_attention(qs8, ks, vst, swa_sinks[0], tb_swa, B, T)
    h, y12, rg = _ffn(h, oa, ob, w_out_even[0], norm_ffn[0], moe_w_group[0], moe_b_group[0], moe_w_router[0],
                      moe_b_router[0], moe_w_gate, moe_w_up, moe_w_down, 0)

    h, vct, vdt, qc, kc, qd, kd = _combine_proj_odd(h, y12, rg, norm_mix[1][None, :], *_odd_weights(w_in_odd[0]),
                                                   _block_indicator(T), T)
    oc = _moba_attention(qc, kc, vct, tb_moba, B, T)
    od = _sb_attention(qd, kd, vdt, B, T)
    h, y12, rg = _ffn(h, oc, od, w_out_odd[0], norm_ffn[1], moe_w_group[1], moe_b_group[1], moe_w_router[1],
                      moe_b_router[1], moe_w_gate, moe_w_up, moe_w_down, 1)
    out = _combine_final(h, y12, rg, norm_final[None, :])
    return out.reshape(B, T, D)
```

```python
import functools
import math

import numpy as np
import jax
import jax.numpy as jnp
from jax import lax
from jax.experimental import pallas as pl
from jax.experimental.pallas import tpu as pltpu
from jax.experimental.pallas import tpu_sc as plsc

F32 = jnp.float32
BF16 = jnp.bfloat16
NEG = -1e30
EPS = 1e-6

D = 1024
HD = 64
MLA_HEADS, MLA_QR, MLA_KVR, MLA_NOPE, MLA_ROPE, MLA_V = 8, 256, 128, 64, 32, 64
ROPE_THETA = 10000.0
SWA_HEADS, SWA_KV, SWA_WIN = 8, 2, 128
MOBA_L, MOBA_TOPK = 256, 3
REL_BUCKETS, REL_MAX = 32, 2048
N_GROUPS, EPG, N_EXP, D_EXP = 4, 8, 32, 256
BIAS_CONST_FROM = 1792
MOBA_TQ = 2 * MOBA_L
BIAS_PAD = MOBA_L
BIAS_ROWS = BIAS_PAD + BIAS_CONST_FROM + MOBA_TQ
SB_DONE = -160.0
TM = 1024
TM_E = 512
VMEM_LIMIT = 56 * 1024 * 1024
SC_CORES, SC_SUBCORES = 2, 16
SC_CHUNK = 128
HALF = D // 2
VT_ROWS = HD + 16
LOG2E = math.log2(math.e)


def _pack_halves(x):
    lo = pltpu.bitcast(x[:, :HALF].astype(BF16).astype(F32), jnp.uint32) >> 16
    hi = pltpu.bitcast(x[:, HALF:].astype(BF16).astype(F32), jnp.uint32) & jnp.uint32(0xFFFF0000)
    return pltpu.bitcast(lo | hi, jnp.int32)


def _unpack_halves(w):
    w = pltpu.bitcast(w, jnp.uint32)
    return pltpu.bitcast(w << 16, F32), pltpu.bitcast(w & jnp.uint32(0xFFFF0000), F32)


def _cparams(sem):
    return pltpu.CompilerParams(dimension_semantics=sem, vmem_limit_bytes=VMEM_LIMIT)


def _dot(a, b):
    return jnp.dot(a, b, preferred_element_type=F32)


def _dot_nt(a, b):
    return lax.dot_general(a, b, (((1,), (1,)), ((), ())), preferred_element_type=F32)


def _rms(x, g):
    return x * lax.rsqrt(jnp.mean(x * x, axis=-1, keepdims=True) + EPS) * g


def _values_t(vt):
    ones = jnp.ones((VT_ROWS - HD, vt.shape[1]), vt.dtype)
    return jnp.concatenate([blk for h in range(8) for blk in (vt[HD * h:HD * (h + 1)], ones)], axis=0)


def _spread_heads(x, lane_of):
    lane = lax.broadcasted_iota(jnp.int32, (x.shape[0], 128), 1)
    out = []
    for h in range(8):
        pair = x[:, 128 * (h // 2):128 * (h // 2 + 1)]
        if HD * (h % 2) != lane_of(h):
            pair = pltpu.roll(pair, HD, 1)
        out.append(jnp.where((lane >= lane_of(h)) & (lane < lane_of(h) + HD), pair, 0.0))
    return out


def _t5_bucket_np(dist):
    n = np.maximum(dist, 0)
    max_exact = REL_BUCKETS // 2
    nf = np.maximum(n, 1).astype(np.float32)
    large = max_exact + (np.log(nf / np.float32(max_exact)) / np.float32(math.log(REL_MAX / max_exact))
                         * np.float32(REL_BUCKETS - max_exact)).astype(np.int32)
    large = np.minimum(large, REL_BUCKETS - 1)
    return np.where(n < max_exact, n, large).astype(np.int32)


def _bias_kernel(table_ref, bucket_ref, out_ref, *, scale, ranges):
    b = bucket_ref[...]
    for m, (lo, hi) in enumerate(ranges):
        @pl.when(pl.program_id(0) == m)
        def _():
            for h in range(8):
                acc = jnp.full(b.shape, NEG if lo == REL_BUCKETS else table_ref[h, lo] * scale, F32)
                for k in range(lo + 1, min(hi, REL_BUCKETS - 1) + 1):
                    acc = jnp.where(b == k, table_ref[h, k] * scale, acc)
                if hi == REL_BUCKETS:
                    acc = jnp.where(b == REL_BUCKETS, NEG, acc)
                out_ref[h] = acc


def _bias_lookup(rel_table, dist, scale, name, masked=None):
    R, C = dist.shape
    buckets = _t5_bucket_np(dist)
    if masked is not None:
        buckets = np.where(masked, REL_BUCKETS, buckets).astype(np.int32)
    ranges = tuple((int(buckets[:, c:c + 256].min()), int(buckets[:, c:c + 256].max())) for c in range(0, C, 256))
    return pl.pallas_call(
        functools.partial(_bias_kernel, scale=scale, ranges=ranges),
        grid=(C // 256,),
        in_specs=[pl.BlockSpec(memory_space=pltpu.SMEM),
                  pl.BlockSpec((R, 256), lambda m: (0, m))],
        out_specs=pl.BlockSpec((8, R, 256), lambda m: (0, 0, m)),
        out_shape=jax.ShapeDtypeStruct((8, R, C), F32),
        compiler_params=_cparams(("arbitrary",)),
        name=name,
    )(rel_table.T, jnp.asarray(buckets))


def _bias_tables(rel_table):
    d = np.arange(256)[None, :] - np.arange(256)[:, None]
    swa = _bias_lookup(rel_table, d, LOG2E, "bias_swa", masked=(d < 0) | (d >= SWA_WIN))
    moba = _bias_lookup(rel_table, np.arange(BIAS_ROWS)[None, :] - BIAS_PAD - np.arange(MOBA_L)[:, None], LOG2E,
                        "bias_moba")
    return swa, moba


def _proj_even_kernel(h_ref, g_ref, win_ref, gq_ref, gkv_ref, wqa_ref, wqb_ref, wk_ref, wv_ref,
                      cq_ref, sq_ref, ck_ref, sk_ref,
                      q_ref, k_ref, v_ref, qs_ref, ks_ref, vst_ref):
    u = _rms(h_ref[...], g_ref[...]).astype(BF16)
    nq = _rms(_dot(u, win_ref[:, 0:256]), gq_ref[...]).astype(BF16)
    qa = _dot(nq, wqa_ref[...])
    qb = _dot(nq, wqb_ref[...])
    cq, sq = cq_ref[...], sq_ref[...]
    for hh in range(MLA_HEADS):
        sl = slice(128 * hh, 128 * (hh + 1))
        q_ref[:, sl] = (qa[:, sl] * cq + qb[:, sl] * sq).astype(BF16)
    nkv = _rms(_dot(u, win_ref[:, 256:384]), gkv_ref[...]).astype(BF16)
    kn = _dot(nkv, wk_ref[...])
    v_ref[...] = _values_t(_dot_nt(wv_ref[...], nkv)).astype(BF16)
    rest = _dot(u, win_ref[:, 896:1408])
    kr = rest[:, 128:256] * ck_ref[...] + rest[:, 256:384] * sk_ref[...]
    for hh in range(MLA_HEADS):
        sl = slice(128 * hh, 128 * (hh + 1))
        k_ref[:, sl] = (kn[:, sl] + kr).astype(BF16)
    qs = _spread_heads(_dot(u, win_ref[:, 384:896]), lambda hq: HD * (hq // (SWA_HEADS // SWA_KV)))
    for hq in range(SWA_HEADS):
        qs_ref[:, 128 * hq:128 * (hq + 1)] = qs[hq].astype(BF16)
    ks_ref[...] = rest[:, 0:128].astype(BF16)
    vst_ref[...] = rest[:, 384:512].T.astype(BF16)


def _even_weights_kernel(wt_ref, wqb_in_ref, wkvb_ref, win_ref, wqa_ref, wqb_ref, wk_ref, wv_ref):
    o_kr = MLA_QR + MLA_KVR
    o_qs = o_kr + MLA_ROPE
    o_ks = o_qs + SWA_HEADS * HD
    o_vs = o_ks + SWA_KV * HD
    half = MLA_ROPE // 2
    wt = wt_ref[...]
    zero_rows = lambda n: jnp.zeros((n, D), F32)
    zero = lambda a, n: jnp.zeros((a.shape[0], n), F32)
    win_ref[...] = jnp.concatenate([
        wt[0:o_kr], wt[o_qs:o_ks] * (HD ** -0.5 * LOG2E), wt[o_ks:o_vs],
        zero_rows(64), wt[o_kr:o_qs], zero_rows(32),
        zero_rows(64), wt[o_kr + half:o_qs], wt[o_kr:o_kr + half], zero_rows(32),
        wt[o_vs:o_vs + SWA_KV * HD]], axis=0).T.astype(BF16)
    wq, wkv = wqb_in_ref[...], wkvb_ref[...]
    hq, hkv = MLA_NOPE + MLA_ROPE, MLA_NOPE + MLA_V
    qa, qb, kn, vt = [], [], [], []
    for h in range(MLA_HEADS):
        qa += [wq[:, hq * h:hq * (h + 1)], zero(wq, 32)]
        qb += [zero(wq, MLA_NOPE), wq[:, hq * h + MLA_NOPE + half:hq * (h + 1)],
               wq[:, hq * h + MLA_NOPE:hq * h + MLA_NOPE + half], zero(wq, 32)]
        kn += [wkv[:, hkv * h:hkv * h + MLA_NOPE], zero(wkv, MLA_V)]
        vt += [wkv[:, hkv * h + MLA_NOPE:hkv * (h + 1)]]
    wqa_ref[...] = jnp.concatenate(qa, axis=1).astype(BF16)
    wqb_ref[...] = jnp.concatenate(qb, axis=1).astype(BF16)
    wk_ref[...] = jnp.concatenate(kn, axis=1).astype(BF16)
    wv_ref[...] = jnp.concatenate(vt, axis=1).T.astype(BF16)


def _even_weights(w_in, w_qb, w_kvb):
    shapes =[(D, 1408), (MLA_QR, 128 * MLA_HEADS), (MLA_QR, 128 * MLA_HEADS), (MLA_KVR, 128 * MLA_HEADS),
              (MLA_V * MLA_HEADS, MLA_KVR)]
    return pl.pallas_call(
        _even_weights_kernel,
        out_shape=[jax.ShapeDtypeStruct(s, BF16) for s in shapes],
        compiler_params=_cparams(None),
        name="even_weights",
    )(w_in.T, w_qb, w_kvb)


def _rope_tables(T):
    f32 = np.float32
    freqs = f32(ROPE_THETA) ** (-np.arange(0, MLA_ROPE, 2, dtype=f32) / f32(MLA_ROPE))
    ang = np.arange(T, dtype=f32)[:, None] * freqs[None, :]
    cos, sin = np.cos(ang), np.sin(ang)
    z = lambda n: np.zeros((T, n), f32)
    scale = f32((MLA_NOPE + MLA_ROPE) ** -0.5 * LOG2E)
    ck = np.concatenate([z(64), cos, cos, z(32)], axis=1)
    sk = np.concatenate([z(64), -sin, sin, z(32)], axis=1)
    cq = np.concatenate([np.ones((T, 64), f32), cos, cos, z(32)], axis=1) * scale
    sq = sk * scale
    return tuple(jnp.asarray(t, F32) for t in (cq, sq, ck, sk))


def _proj_even(h, g, win, gq, gkv, wqa, wqb, wk, wv, tables, T):
    N = h.shape[0]
    nt = T // TM
    full = lambda a: pl.BlockSpec(a.shape, lambda i: (0,) * a.ndim)
    tab = pl.BlockSpec((TM, 128), lambda i: (i % nt, 0))
    row = lambda w: pl.BlockSpec((TM, w), lambda i: (i, 0))
    outs = (1024, 1024, -8 * VT_ROWS, 1024, 128, -128)
    spec = lambda w: row(w) if w > 0 else pl.BlockSpec((None, -w, TM), lambda i: (i // nt, 0, i % nt))
    shape = lambda w: jax.ShapeDtypeStruct((N, w) if w > 0 else (N // T, -w, T), BF16)
    return pl.pallas_call(
        _proj_even_kernel,
        grid=(N // TM,),
        in_specs=[row(D), full(g), full(win), full(gq), full(gkv), full(wqa), full(wqb), full(wk), full(wv),
                  tab, tab, tab, tab],
        out_specs=[spec(w) for w in outs],
        out_shape=[shape(w) for w in outs],
        compiler_params=_cparams(("arbitrary",)),
        name="proj_even",
    )(h, g, win, gq, gkv, wqa, wqb, wk, wv, *tables)


def _finish_t(accs):
    ot = jnp.concatenate([acc[:HD] / acc[HD:HD + 1] for acc in accs], axis=0)
    return ot.T.astype(BF16)


def _causal_attention_pair_t(ia, nq, q_of, k_ref, vt_ref, o_ref, s_ref, smax_ref, m_ref, acc_ref, *, tq,
                             bias_of=None):
    nh = s_ref.shape[1]
    first_query = (ia * tq, (nq - 1 - ia) * tq)
    n_items = nq + 1
    assert n_items % 2 == 1
    key = lax.broadcasted_iota(jnp.int32, (tq, tq), 0)
    qry = lax.broadcasted_iota(jnp.int32, (tq, tq), 1)

    def item(p):
        if isinstance(p, int) and p < 2:
            return p, first_query[p], first_query[p]
        t = jnp.where(p - 2 >= ia, 1, 0)
        return t, jnp.where(t == 1, first_query[1], first_query[0]), (p - 2 - t * ia) * tq

    def scores_to(slot, p):
        t, q0, start = item(p)
        start = pl.multiple_of(start, tq)
        for j in range(nh):
            st = _dot_nt(k_ref[pl.ds(start, tq), 128 * j:128 * (j + 1)], q_of(t, q0, j))
            if bias_of is not None:
                st = st + bias_of(start, q0, j)
            if isinstance(p, int) and p < 2:
                st = jnp.where(key <= qry, st, NEG)
            s_ref[slot, j] = st
            smax_ref[slot, j] = jnp.max(st, axis=0, keepdims=True)

    def update_from(slot, p):
        t, _, start = item(p)
        start = pl.multiple_of(start, tq)
        m_new = [jnp.maximum(m_ref[t, j], smax_ref[slot, j]) for j in range(nh)]
        pts = [jnp.exp2(s_ref[slot, j] - m_new[j]).astype(BF16) for j in range(nh)]
        for j in range(nh):
            acc_ref[t, j] = (jnp.exp2(m_ref[t, j] - m_new[j]) * acc_ref[t, j]
                             + _dot(vt_ref[VT_ROWS * j:VT_ROWS * (j + 1), pl.ds(start, tq)], pts[j]))
            m_ref[t, j] = m_new[j]

    def pair(k):
        scores_to(1, k + 1)
        update_from(0, k)
        scores_to(0, k + 2)
        update_from(1, k + 1)

    def quad(c, _):
        pair(2 + 4 * c)
        pair(4 + 4 * c)
        return 0

    m_ref[...] = jnp.full(m_ref.shape, NEG, F32)
    acc_ref[...] = jnp.zeros(acc_ref.shape, F32)
    scores_to(0, 0)
    pair(0)
    later_pairs = (n_items - 1) // 2 - 1
    lax.fori_loop(0, later_pairs // 2, quad, 0)
    if later_pairs % 2:
        pair(n_items - 3)
    update_from(0, n_items - 1)
    for t in range(2):
        o_ref[pl.ds(pl.multiple_of(first_query[t], tq), tq), :] = _finish_t([acc_ref[t, j] for j in range(nh)])


def _attention_t_scratch(tq):
    return [pltpu.VMEM((2, 2, tq, tq), F32), pltpu.VMEM((2, 2, 1, tq), F32), pltpu.VMEM((2, 2, 1, tq), F32),
            pltpu.VMEM((2, 2, VT_ROWS, tq), F32)]


def _mla_kernel(q_ref, k_ref, vt_ref, o_ref, s_ref, smax_ref, m_ref, acc_ref, *, tq, nq):
    def q_of(t, q0, j):
        return q_ref[pl.ds(pl.multiple_of(q0, tq), tq), 128 * j:128 * (j + 1)]

    _causal_attention_pair_t(pl.program_id(2), nq, q_of, k_ref, vt_ref, o_ref, s_ref, smax_ref, m_ref, acc_ref,
                             tq=tq)


def _mla_attention(q, k, vt, B, T, tq=512):
    nq = T // tq
    return pl.pallas_call(
        functools.partial(_mla_kernel, tq=tq, nq=nq),
        grid=(B, MLA_HEADS // 2, nq // 2),
        in_specs=[pl.BlockSpec((T, 256), lambda b, p, i: (b, p)),
                  pl.BlockSpec((T, 256), lambda b, p, i: (b, p)),
                  pl.BlockSpec((None, 2 * VT_ROWS, T), lambda b, p, i: (b, p, 0))],
        out_specs=pl.BlockSpec((T, 128), lambda b, p, i: (b, p)),
        out_shape=jax.ShapeDtypeStruct((B * T, MLA_HEADS * MLA_V), BF16),
        scratch_shapes=_attention_t_scratch(tq),
        compiler_params=_cparams(("arbitrary", "arbitrary", "arbitrary")),
        name="mla_attention",
    )(q, k, vt)


def _swa_kernel(sink_ref, q_ref, k_ref, vt_ref, tb_ref, o_ref, *, tq):
    i = pl.program_id(1)
    G = SWA_HEADS // SWA_KV
    W = 2 * SWA_WIN
    head = lax.broadcasted_iota(jnp.int32, (1, G * 128), 1) >> 7
    sinks = []
    for c in range(SWA_KV):
        sink = jnp.zeros((1, G * 128), F32)
        for g in range(G):
            sink = jnp.where(head == g, sink_ref[G * c + g], sink)
        sinks.append(sink)
    scores, values = [], []
    for r in range(tq // 128):
        qstart = i * tq + 128 * r
        kstart = pl.multiple_of(jnp.maximum(qstart - SWA_WIN, 0), 128)
        off = pl.multiple_of(qstart - kstart, 128)
        kw = k_ref[pl.ds(kstart, W), :]
        for c in range(SWA_KV):
            qg = jnp.concatenate([q_ref[128 * r:128 * (r + 1), 128 * (G * c + g):128 * (G * c + g + 1)]
                                  for g in range(G)], axis=0)
            bias = jnp.concatenate([tb_ref[G * c + g, :, pl.ds(off, 128)] for g in range(G)], axis=1)
            scores.append(_dot_nt(kw, qg) + bias)
            values.append(vt_ref[HD * c:HD * (c + 1), pl.ds(kstart, W)])
    probs = []
    for n, s in enumerate(scores):
        sink = sinks[n % SWA_KV]
        m = jnp.maximum(jnp.max(s, axis=0, keepdims=True), sink)
        p = jnp.exp2(s - m)
        probs.append((p.astype(BF16), jnp.sum(p, axis=0, keepdims=True) + jnp.exp2(sink - m)))
    for r in range(tq // 128):
        outs = []
        for c in range(SWA_KV):
            p, denom = probs[SWA_KV * r + c]
            ot = _dot(values[SWA_KV * r + c], p) / denom
            outs += [ot[:, 128 * g:128 * (g + 1)] for g in range(G)]
        o_ref[128 * r:128 * (r + 1), :] = jnp.concatenate(outs, axis=0).T.astype(BF16)


def _swa_attention(q8, k, vt, sinks, tb, B, T, tq=512):
    nq = T // tq
    return pl.pallas_call(
        functools.partial(_swa_kernel, tq=tq),
        grid=(B, nq),
        in_specs=[pl.BlockSpec(memory_space=pltpu.SMEM),
                  pl.BlockSpec((tq, 1024), lambda b, i: (b * nq + i, 0)),
                  pl.BlockSpec((T, 128), lambda b, i: (b, 0)),
                  pl.BlockSpec((None, 128, T), lambda b, i: (b, 0, 0)),
                  pl.BlockSpec((8, 256, 256), lambda b, i: (0, 0, 0))],
        out_specs=pl.BlockSpec((tq, 512), lambda b, i: (b * nq + i, 0)),
        out_shape=jax.ShapeDtypeStruct((B * T, SWA_HEADS * HD), BF16),
        compiler_params=_cparams(("arbitrary", "arbitrary")),
        name="swa_attention",
    )(sinks * LOG2E, q8, k, vt, tb)


def _outproj_router_kernel(h_ref, oa_ref, ob_ref, wo_ref, g_ref, wr_ref, br_ref,
                           hn_ref, u_ref, ri_ref, cnt_ref, base_ref):
    i = pl.program_id(0)

    @pl.when(i == 0)
    def _():
        base_ref[...] = jnp.zeros_like(base_ref)

    hn = h_ref[...] + _dot(oa_ref[...], wo_ref[0:512, :]) + _dot(ob_ref[...], wo_ref[512:1024, :])
    hn_ref[...] = hn
    u = _rms(hn, g_ref[...])
    u_ref[...] = _pack_halves(u)
    u_hi = u.astype(BF16)
    u_lo = (u - u_hi.astype(F32)).astype(BF16)
    both = _dot_nt(wr_ref[...], u_hi)
    logits = both[:128] + both[128:] + _dot_nt(wr_ref[0:128, :], u_lo) + br_ref[...]
    tm = logits.shape[1]
    sub = lax.broadcasted_iota(jnp.int32, (8, tm), 0).astype(F32)
    big = 1e6
    isg = sub < N_GROUPS
    gl = jnp.where(isg, logits[N_EXP:N_EXP + 8], NEG)
    gmax = jnp.max(gl, axis=0, keepdims=True)
    gsel = jnp.min(jnp.where(gl == gmax, sub, big), axis=0, keepdims=True)
    g_gate = 1.0 / jnp.sum(jnp.where(isg, jnp.exp(gl - gmax), 0.0), axis=0, keepdims=True)
    el = jnp.zeros((EPG, tm), F32)
    for grp in range(N_GROUPS):
        el = jnp.where(gsel == grp, logits[EPG * grp:EPG * (grp + 1)], el)
    m1 = jnp.max(el, axis=0, keepdims=True)
    i1 = jnp.min(jnp.where(el == m1, sub, big), axis=0, keepdims=True)
    el2 = jnp.where(sub == i1, NEG, el)
    m2 = jnp.max(el2, axis=0, keepdims=True)
    i2 = jnp.min(jnp.where(el2 == m2, sub, big), axis=0, keepdims=True)
    e1 = gsel * EPG + i1
    e2 = gsel * EPG + i2
    r = jnp.exp(m2 - m1)
    gate1 = g_gate / (1.0 + r)
    gate2 = g_gate * r / (1.0 + r)
    rows = lax.broadcasted_iota(jnp.int32, (128, tm), 0).astype(F32)
    oh1 = (rows == e1).astype(F32)
    oh2 = (rows == e2).astype(F32)
    oh = oh1 + oh2
    t_row = lax.broadcasted_iota(jnp.int32, (tm, tm), 0)
    t_col = lax.broadcasted_iota(jnp.int32, (tm, tm), 1)
    earlier = jnp.where(t_row < t_col, 1.0, 0.0).astype(BF16)
    prefix = _dot(oh.astype(BF16), earlier) + base_ref[:, 0:1]
    rank1 = jnp.sum(oh1 * prefix, axis=0, keepdims=True)
    rank2 = jnp.sum(oh2 * prefix, axis=0, keepdims=True)
    base_ref[...] = base_ref[...] + jnp.sum(oh, axis=1, keepdims=True)
    cnt_ref[...] = base_ref[...]
    ri_ref[...] = jnp.where(sub == 0, e1, jnp.where(sub == 1, e2, jnp.where(sub == 2, rank1,
                            jnp.where(sub == 3, rank2, jnp.where(sub == 4, gate1,
                                                                 jnp.where(sub == 5, gate2, 0.0))))))


def _outproj_router(h, oa, ob, wo, g, wr, br):
    N = h.shape[0]
    full = lambda a: pl.BlockSpec(a.shape, lambda i: (0,) * a.ndim)
    row = lambda w: pl.BlockSpec((TM, w), lambda i: (i, 0))
    return pl.pallas_call(
        _outproj_router_kernel,
        grid=(N // TM,),
        in_specs=[row(D), row(512), row(512), full(wo), full(g), full(wr), full(br)],
        out_specs=[row(D), row(HALF), pl.BlockSpec((8, TM), lambda i: (0, i)),
                   pl.BlockSpec((128, 128), lambda i: (0, 0))],
        out_shape=[jax.ShapeDtypeStruct((N, D), F32), jax.ShapeDtypeStruct((N, HALF), jnp.int32),
                   jax.ShapeDtypeStruct((8, N), F32), jax.ShapeDtypeStruct((128, 128), F32)],
        scratch_shapes=[pltpu.VMEM((128, 128), F32)],
        compiler_params=_cparams(("arbitrary",)),
        name="outproj_router",
    )(h, oa, ob, wo, g, wr, br)


def _moe_kernel(te_ref, tv_ref, run_ref, nx_ref, x_ref, wg_hbm, wu_hbm, wd_hbm, o_ref,
                wg_buf, wu_buf, wd_buf, sems, wgu_s, wd_s, *, layer):
    i = pl.program_id(0)
    valid = tv_ref[i]

    def weight_copies(e, slot):
        return (pltpu.make_async_copy(wg_hbm.at[layer, e], wg_buf.at[slot], sems.at[slot, 0]),
                pltpu.make_async_copy(wu_hbm.at[layer, e], wu_buf.at[slot], sems.at[slot, 1]),
                pltpu.make_async_copy(wd_hbm.at[layer, e], wd_buf.at[slot], sems.at[slot, 2]))

    @pl.when(run_ref[i] > 0)
    def _():
        slot = run_ref[i] - 1

        @pl.when(i == 0)
        def _():
            for c in weight_copies(te_ref[i], slot):
                c.start()

        for c in weight_copies(te_ref[i], slot):
            c.wait()
        wgu_s[:, :D_EXP] = wg_buf[slot].astype(BF16)
        wgu_s[:, D_EXP:] = wu_buf[slot].astype(BF16)
        wd_s[...] = wd_buf[slot].astype(BF16)

        @pl.when(nx_ref[i] >= 0)
        def _():
            for c in weight_copies(nx_ref[i], 1 - slot):
                c.start()

    @pl.when(valid > 0)
    def _():
        rows = lax.broadcasted_iota(jnp.int32, x_ref.shape, 0)
        lo, hi = _unpack_halves(jnp.where(rows < valid, x_ref[...], 0))
        hgu = _dot(lo.astype(BF16), wgu_s[:HALF, :]) + _dot(hi.astype(BF16), wgu_s[HALF:, :])
        hg, hu = hgu[:, :D_EXP], hgu[:, D_EXP:]
        a = hg * (1.0 / (1.0 + jnp.exp(-hg))) * hu
        o_ref[...] = _pack_halves(_dot(a.astype(BF16), wd_s[...]))

    @pl.when(valid <= 0)
    def _():
        o_ref[...] = jnp.zeros_like(o_ref)


def _moe_experts(xs, plan, w_gate, w_up, w_down, layer):
    P = xs.shape[0]
    hbm = pl.BlockSpec(memory_space=pl.ANY)
    grid_spec = pltpu.PrefetchScalarGridSpec(
        num_scalar_prefetch=4,
        grid=(P // TM_E,),
        in_specs=[pl.BlockSpec((TM_E, HALF), lambda i, *_: (i, 0)), hbm, hbm, hbm],
        out_specs=pl.BlockSpec((TM_E, HALF), lambda i, *_: (i, 0)),
        scratch_shapes=[pltpu.VMEM((2, D, D_EXP), F32), pltpu.VMEM((2, D, D_EXP), F32),
                        pltpu.VMEM((2, D_EXP, D), F32), pltpu.SemaphoreType.DMA((2, 3)),
                        pltpu.VMEM((D, 2 * D_EXP), BF16), pltpu.VMEM((D_EXP, D), BF16)],
    )
    return pl.pallas_call(
        functools.partial(_moe_kernel, layer=layer),
        grid_spec=grid_spec,
        out_shape=jax.ShapeDtypeStruct((P, HALF), jnp.int32),
        compiler_params=_cparams(("arbitrary",)),
        name="moe_experts",
    )(*plan, xs, w_gate, w_up, w_down)


def _sc_mesh():
    return plsc.VectorSubcoreMesh(core_axis_name="c", subcore_axis_name="s",
                                  num_cores=SC_CORES, num_subcores=SC_SUBCORES)


def _sc_scatter_rows(src, d1, d2, P):
    N, W = src.shape
    per_w = N // (SC_CORES * SC_SUBCORES)

    @functools.partial(pl.kernel, mesh=_sc_mesh(), out_type=jax.ShapeDtypeStruct((P, W), src.dtype),
                       scratch_types=[pltpu.VMEM((SC_CHUNK,), jnp.int32), pltpu.VMEM((SC_CHUNK, W), src.dtype)],
                       name="sc_scatter_rows")
    def k(src_hbm, d1_hbm, d2_hbm, out_hbm, idx_v, rows_v):
        wid = lax.axis_index("s") * SC_CORES + lax.axis_index("c")

        @pl.loop(0, per_w // SC_CHUNK)
        def _(c):
            off = wid * per_w + c * SC_CHUNK
            pltpu.sync_copy(src_hbm.at[pl.ds(off, SC_CHUNK)], rows_v)
            pltpu.sync_copy(d1_hbm.at[pl.ds(off, SC_CHUNK)], idx_v)
            pltpu.sync_copy(rows_v, out_hbm.at[idx_v])
            pltpu.sync_copy(d2_hbm.at[pl.ds(off, SC_CHUNK)], idx_v)
            pltpu.sync_copy(rows_v, out_hbm.at[idx_v])

    return k(src, d1, d2)


def _sc_gather_rows(table, idx):
    B, W = idx.shape[0], table.shape[1]
    per_w = B // (SC_CORES * SC_SUBCORES)

    @functools.partial(pl.kernel, mesh=_sc_mesh(), out_type=jax.ShapeDtypeStruct((B, W), table.dtype),
                       scratch_types=[pltpu.VMEM((SC_CHUNK,), jnp.int32), pltpu.VMEM((SC_CHUNK, W), table.dtype)],
                       name="sc_gather_rows")
    def k(table_hbm, idx_hbm, out_hbm, idx_v, rows_v):
        wid = lax.axis_index("s") * SC_CORES + lax.axis_index("c")

        @pl.loop(0, per_w // SC_CHUNK)
        def _(c):
            off = wid * per_w + c * SC_CHUNK
            pltpu.sync_copy(idx_hbm.at[pl.ds(off, SC_CHUNK)], idx_v)
            pltpu.sync_copy(table_hbm.at[idx_v], rows_v)
            pltpu.sync_copy(rows_v, out_hbm.at[pl.ds(off, SC_CHUNK)])

    return k(table, idx)


def _dest_kernel(offs_ref, ri_ref, d_ref):
    ri = ri_ref[...]
    base = jnp.zeros(ri.shape, F32)
    for e in range(N_EXP):
        base = jnp.where(ri == e, offs_ref[e].astype(F32), base)
    d_ref[...] = (base + pltpu.roll(ri, 6, 0)).astype(jnp.int32)


def _dispatch_plan(ri, cnt, N):
    counts = cnt[:N_EXP, 0].astype(jnp.int32)
    padded = ((counts + TM_E - 1) // TM_E) * TM_E
    ends = jnp.cumsum(padded)
    offs = ends - padded
    take = lambda t, i: t.at[i].get(mode="promise_in_bounds")
    dest = pl.pallas_call(
        _dest_kernel,
        in_specs=[pl.BlockSpec(memory_space=pltpu.SMEM), pl.BlockSpec((8, N), lambda: (0, 0))],
        out_specs=pl.BlockSpec((8, N), lambda: (0, 0)),
        out_shape=jax.ShapeDtypeStruct((8, N), jnp.int32),
        name="dispatch_rows",
    )(offs, ri)
    d1, d2 = dest[0], dest[1]
    P = 2 * N + N_EXP * TM_E
    tile_start = jnp.arange(P // TM_E, dtype=jnp.int32) * TM_E
    tile_expert = jnp.minimum(jnp.sum((ends[None, :] <= tile_start[:, None]).astype(jnp.int32), axis=1), N_EXP - 1)
    tile_valid = jnp.clip(take(offs + counts, tile_expert) - tile_start, 0, TM_E)
    n_tiles = P // TM_E
    tile = jnp.arange(n_tiles, dtype=jnp.int32)
    starts = (tile_valid > 0) & ((tile == 0) | (tile_expert != jnp.roll(tile_expert, 1)))
    slot = (jnp.cumsum(starts.astype(jnp.int32)) - 1) % 2
    run_code = jnp.where(starts, slot + 1, 0).astype(jnp.int32)
    start_at = jnp.where(starts, tile, n_tiles)
    next_start = jnp.concatenate([lax.cummin(start_at, reverse=True)[1:], jnp.full((1,), n_tiles, jnp.int32)])
    next_expert = jnp.where(next_start < n_tiles, take(tile_expert, jnp.minimum(next_start, n_tiles - 1)), -1)
    return d1, d2, P, (tile_expert, tile_valid, run_code, next_expert.astype(jnp.int32))


def _combine(h_ref, y1_ref, y2_ref, ri_ref):
    ri = ri_ref[...]
    rows = lax.broadcasted_iota(jnp.int32, (128, ri.shape[1]), 0)
    rg = jnp.where(rows == 0, ri[4:5], jnp.where(rows == 1, ri[5:6], 0.0)).T
    lo1, hi1 = _unpack_halves(y1_ref[...])
    lo2, hi2 = _unpack_halves(y2_ref[...])
    g1, g2 = rg[:, 0:1], rg[:, 1:2]
    return h_ref[...] + jnp.concatenate([g1 * lo1 + g2 * lo2, g1 * hi1 + g2 * hi2], axis=1)


ODD_WIDTHS = (1024, 1024, 512, 512)


def _combine_proj_odd_kernel(h_ref, y1_ref, y2_ref, ri_ref, g_ref, win_ref, wvt_ref, ind_ref, hn_ref, vt_ref,
                             vdt_ref, *out_refs):
    hn = _combine(h_ref, y1_ref, y2_ref, ri_ref)
    hn_ref[...] = hn
    u = _rms(hn, g_ref[...]).astype(BF16)
    vt_ref[...] = _values_t(_dot_nt(wvt_ref[0:512, :], u)).astype(BF16)
    vdt_ref[...] = _dot_nt(wvt_ref[512:1024, :], u).astype(BF16)
    qc_ref, kc_ref, qd_ref, kd_ref = out_refs
    qc = _spread_heads(_dot(u, win_ref[:, 0:512]), lambda h: 0)
    kc = _spread_heads(_dot(u, win_ref[:, 512:1024]), lambda h: 0)
    for hh in range(8):
        qc_ref[:, 128 * hh:128 * (hh + 1)] = qc[hh].astype(BF16)
        kc_ref[:, 128 * hh:128 * (hh + 1)] = (kc[hh] + ind_ref[...]).astype(BF16)
    qd_ref[...] = _dot(u, win_ref[:, 1024:1536]).astype(BF16)
    kd_ref[...] = _dot(u, win_ref[:, 1536:2048]).astype(BF16)


def _odd_weights_kernel(w_ref, win_ref, wvt_ref):
    win_ref[:, 0:512] = (w_ref[:, 0:512] * (HD ** -0.5 * LOG2E)).astype(BF16)
    win_ref[:, 512:1024] = w_ref[:, 512:1024].astype(BF16)
    wvt_ref[...] = w_ref[:, 1024:1536].T.astype(BF16)


def _odd_weights(w_in):
    return pl.pallas_call(
        _odd_weights_kernel,
        grid=(2,),
        in_specs=[pl.BlockSpec((D, 1536), lambda i: (0, i))],
        out_specs=[pl.BlockSpec((D, 1024), lambda i: (0, i)), pl.BlockSpec((512, D), lambda i: (i, 0))],
        out_shape=[jax.ShapeDtypeStruct((D, 2048), BF16), jax.ShapeDtypeStruct((1024, D), BF16)],
        compiler_params=_cparams(("arbitrary",)),
        name="odd_weights",
    )(w_in)


def _block_indicator(T):
    assert T // MOBA_L <= 64
    ind = np.zeros((T, 128), np.float32)
    ind[np.arange(T), 64 + np.arange(T) // MOBA_L] = 1.0
    return jnp.asarray(ind)


def _y_specs(N):
    return [pl.BlockSpec((TM, HALF), lambda i: (i, 0)), pl.BlockSpec((TM, HALF), lambda i: (i + N // TM, 0))]


def _combine_proj_odd(h, y12, ri, g, win, wvt, ind, T):
    N = h.shape[0]
    nt = T // TM
    full = lambda a: pl.BlockSpec(a.shape, lambda i: (0,) * a.ndim)
    row = lambda w: pl.BlockSpec((TM, w), lambda i: (i, 0))
    return pl.pallas_call(
        _combine_proj_odd_kernel,
        grid=(N // TM,),
        in_specs=[row(D)] + _y_specs(N) + [pl.BlockSpec((8, TM), lambda i: (0, i)), full(g), full(win), full(wvt),
                  pl.BlockSpec((TM, 128), lambda i: (i % nt, 0))],
        out_specs=[row(D), pl.BlockSpec((None, 8 * VT_ROWS, TM), lambda i: (i // nt, 0, i % nt)),
                   pl.BlockSpec((None, 512, TM), lambda i: (i // nt, 0, i % nt))]
        + [row(w) for w in ODD_WIDTHS],
        out_shape=[jax.ShapeDtypeStruct((N, D), F32), jax.ShapeDtypeStruct((N // T, 8 * VT_ROWS, T), BF16),
                   jax.ShapeDtypeStruct((N // T, 512, T), BF16)]
        + [jax.ShapeDtypeStruct((N, w), BF16) for w in ODD_WIDTHS],
        compiler_params=_cparams(("arbitrary",)),
        name="combine_proj_odd",
    )(h, y12, y12, ri, g, win, wvt, ind)


def _combine_final_kernel(h_ref, y1_ref, y2_ref, ri_ref, g_ref, o_ref):
    o_ref[...] = _rms(_combine(h_ref, y1_ref, y2_ref, ri_ref), g_ref[...])


def _combine_final(h, y12, ri, g):
    N = h.shape[0]
    row = lambda w: pl.BlockSpec((TM, w), lambda i: (i, 0))
    return pl.pallas_call(
        _combine_final_kernel,
        grid=(N // TM,),
        in_specs=[row(D)] + _y_specs(N) + [pl.BlockSpec((8, TM), lambda i: (0, i)),
                                           pl.BlockSpec((1, D), lambda i: (0, 0))],
        out_specs=row(D),
        out_shape=jax.ShapeDtypeStruct((N, D), F32),
        compiler_params=_cparams(("arbitrary",)),
        name="combine_final",
    )(h, y12, y12, ri, g)


def _moba_kernel(q_ref, k_ref, vt_ref, tb_ref, o_ref, kmean_ref, ksplit_ref, qa_ref, s_ref, smax_ref, m_ref, acc_ref,
                 *, nkb, nq):
    ia = pl.program_id(2)
    L, tq = MOBA_L, MOBA_TQ

    @pl.when(ia == 0)
    def _():
        kmean_ref[...] = jnp.zeros_like(kmean_ref)
        for n in range(nkb):
            kmean_ref[64 + n:65 + n, :] = jnp.sum(k_ref[n * L:(n + 1) * L, :].astype(F32), axis=0,
                                                  keepdims=True) * (1.0 / L)
        hi = kmean_ref[...].astype(BF16)
        ksplit_ref[0] = hi
        ksplit_ref[1] = (kmean_ref[...] - hi.astype(F32)).astype(BF16)

    nb = -(-nkb // 8) * 8
    lane = lax.broadcasted_iota(jnp.int32, (tq, 128), 1)
    blk = lax.broadcasted_iota(jnp.int32, (nb, tq), 0)
    half = jnp.where(lax.broadcasted_iota(jnp.int32, (nb, tq), 1) >= L, 1, 0)
    blk_f = blk.astype(F32)
    for t, tile in enumerate((ia, nq - 1 - ia)):
        own = 2 * tile + half
        for j in range(2):
            qj = q_ref[pl.ds(pl.multiple_of(tile * tq, tq), tq), 128 * j:128 * (j + 1)]
            gsc = (_dot_nt(ksplit_ref[0, 64:64 + nb, 128 * j:128 * (j + 1)], qj)
                   + _dot_nt(ksplit_ref[1, 64:64 + nb, 128 * j:128 * (j + 1)], qj))
            g = jnp.where(blk < own, gsc, NEG)
            allowed = jnp.where(blk == own, 1.0, 0.0)
            for _ in range(MOBA_TOPK):
                mx = jnp.max(g, axis=0, keepdims=True)
                idx = jnp.min(jnp.where(g == mx, blk_f, 1e6), axis=0, keepdims=True)
                pick = blk_f == idx
                allowed = jnp.maximum(allowed, jnp.where(pick, jnp.where(mx > 0.5 * NEG, 1.0, 0.0), 0.0))
                g = jnp.where(pick, 2.0 * NEG, g)
            mask = jnp.concatenate([jnp.zeros((64, tq), F32), jnp.where(allowed > 0.5, 0.0, NEG),
                                    jnp.zeros((64 - nb, tq), F32)], axis=0)
            qa_ref[t, j] = jnp.where(lane < HD, qj, mask.T.astype(BF16))

    def bias_of(start, q0, j):
        parts = []
        for c in range(tq // L):
            off = jnp.minimum(q0 - start - c * L, BIAS_CONST_FROM) + BIAS_PAD
            parts.append(tb_ref[j, :, pl.ds(pl.multiple_of(off, L), tq)])
        return jnp.concatenate(parts, axis=0)

    _causal_attention_pair_t(ia, nq, lambda t, q0, j: qa_ref[t, j], k_ref, vt_ref, o_ref, s_ref, smax_ref, m_ref,
                             acc_ref, tq=tq, bias_of=bias_of)


def _moba_attention(q, k, vt, tb, B, T):
    tq = MOBA_TQ
    nq = T // tq
    return pl.pallas_call(
        functools.partial(_moba_kernel, nkb=T // MOBA_L, nq=nq),
        grid=(B, 4, nq // 2),
        in_specs=[pl.BlockSpec((T, 256), lambda b, p, i: (b, p)),
                  pl.BlockSpec((T, 256), lambda b, p, i: (b, p)),
                  pl.BlockSpec((None, 2 * VT_ROWS, T), lambda b, p, i: (b, p, 0)),
                  pl.BlockSpec((2, MOBA_L, BIAS_ROWS), lambda b, p, i: (p, 0, 0))],
        out_specs=pl.BlockSpec((T, 128), lambda b, p, i: (b, p)),
        out_shape=jax.ShapeDtypeStruct((B * T, 512), BF16),
        scratch_shapes=[pltpu.VMEM((128, 256), F32), pltpu.VMEM((2, 128, 256), BF16),
                        pltpu.VMEM((2, 2, tq, 128), BF16)]
        + _attention_t_scratch(tq),
        compiler_params=_cparams(("arbitrary", "arbitrary", "arbitrary")),
        name="moba_attention",
    )(q, k, vt, tb)


def _sb_kernel(q_ref, k_ref, vt_ref, o_ref, *, tq, pairs):
    i = pl.program_id(2)
    lane = lax.broadcasted_iota(jnp.int32, (tq, 128), 1)
    qs = []
    for p in range(pairs):
        q_pair = q_ref[:, 128 * p:128 * (p + 1)]
        qs += [jnp.where(lane < HD, q_pair, 0).astype(BF16), jnp.where(lane >= HD, q_pair, 0).astype(BF16)]
    key = lax.broadcasted_iota(jnp.int32, (tq, tq), 0)
    qry = lax.broadcasted_iota(jnp.int32, (tq, tq), 1)
    past = key < qry
    suffix = jnp.where(qry >= key, -1.0, 0.0).astype(BF16)
    heads = range(2 * pairs)

    def step(kb, carry, boundary):
        start = pl.multiple_of(kb * tq, tq)
        zs = [_dot_nt(k_ref[pl.ds(start, tq), 128 * (h // 2):128 * (h // 2 + 1)], qs[h]) for h in heads]
        parts = []
        for z in zs:
            neg_abs = pltpu.bitcast(pltpu.bitcast(z, jnp.uint32) | jnp.uint32(0x80000000), F32)
            soft = jnp.maximum(z, 0.0) + jnp.log2(1.0 + jnp.exp2(neg_abs))
            if boundary:
                soft = jnp.where(past, soft, 0.0)
            parts.append(soft.astype(BF16))
        sums = [_dot(suffix, part) for part in parts]
        ws = []
        for h in heads:
            w = jnp.exp2(zs[h] + (carry[h][0] + sums[h]))
            ws.append((jnp.where(past, w, 0.0) if boundary else w).astype(BF16))
        return tuple((carry[h][0] + sums[h][0:1], carry[h][1]
                      + _dot(vt_ref[128 * (h // 2):128 * (h // 2 + 1), pl.ds(start, tq)], ws[h])) for h in heads)

    def live(carry):
        top = carry[0][0]
        for c, _ in carry[1:]:
            top = jnp.maximum(top, c)
        return (jnp.max(top) > SB_DONE).astype(jnp.int32)

    def body(state):
        t, _, carry = state
        carry = step(i - 1 - t, carry, False)
        return t + 1, live(carry), carry

    init = tuple((jnp.zeros((1, tq), F32), jnp.zeros((128, tq), F32)) for _ in range(2 * pairs))
    carry = step(i, init, True)
    _, _, carry = lax.while_loop(lambda st: (st[0] < i) & (st[1] > 0), body, (jnp.int32(0), live(carry), carry))
    rows = lax.broadcasted_iota(jnp.int32, (128, tq), 0)
    for p in range(pairs):
        ot = jnp.where(rows < HD, carry[2 * p][1], carry[2 * p + 1][1])
        o_ref[:, 128 * p:128 * (p + 1)] = ot.T.astype(BF16)


def _sb_attention(q, k, vt, B, T, tq=256, pairs=4):
    nq = T // tq
    w = 128 * pairs
    return pl.pallas_call(
        functools.partial(_sb_kernel, tq=tq, pairs=pairs),
        grid=(B, 4 // pairs, nq),
        in_specs=[pl.BlockSpec((tq, w), lambda b, p, i: (b * nq + i, p)),
                  pl.BlockSpec((T, w), lambda b, p, i: (b, p)),
                  pl.BlockSpec((None, w, T), lambda b, p, i: (b, p, 0))],
        out_specs=pl.BlockSpec((tq, w), lambda b, p, i: (b * nq + i, p)),
        out_shape=jax.ShapeDtypeStruct((B * T, 512), BF16),
        compiler_params=_cparams(("arbitrary", "arbitrary", "arbitrary")),
        name="sb_attention",
    )(q, k, vt)


def _router_weights(w_group, b_group, w_router, b_router):
    pad = 128 - N_EXP - N_GROUPS
    wr = jnp.concatenate([w_router, w_group, jnp.zeros((D, pad), F32)], axis=1).T
    hi = wr.astype(BF16)
    lo = (wr - hi.astype(F32)).astype(BF16)
    br = jnp.concatenate([b_router, b_group, jnp.zeros((pad,), F32)])[:, None]
    return jnp.concatenate([hi, lo], axis=0), br


def _ffn(h, oa, ob, wo, g, w_group, b_group, w_router, b_router, w_gate, w_up, w_down, layer):
    wr, br = _router_weights(w_group, b_group, w_router, b_router)
    hn, u, ri, cnt = _outproj_router(h, oa, ob, wo.astype(BF16), g[None, :], wr, br)
    d1, d2, P, plan = _dispatch_plan(ri, cnt, h.shape[0])
    xs = _sc_scatter_rows(u, d1, d2, P)
    ys = _moe_experts(xs, plan, w_gate, w_up, w_down, layer)
    return hn, _sc_gather_rows(ys, jnp.concatenate([d1, d2])), ri


def kernel(x, norm_mix, norm_ffn, norm_final, rel_bias_table, w_in_even, g_mla_q, g_mla_kv, w_mla_qb, w_mla_kvb, swa_sinks, w_out_even, w_in_odd, w_out_odd, moe_w_group, moe_b_group, moe_w_router, moe_b_router, moe_w_gate, moe_w_up, moe_w_down):
    B, T, _ = x.shape
    h = x.reshape(B * T, D)
    tb_swa, tb_moba = _bias_tables(rel_bias_table)

    win, wqa, wqb, wk, wv = _even_weights(w_in_even[0], w_mla_qb[0], w_mla_kvb[0])
    q, k, v, qs8, ks, vst = _proj_even(h, norm_mix[0][None, :], win, g_mla_q[0][None, :], g_mla_kv[0][None, :],
                                       wqa, wqb, wk, wv, _rope_tables(T), T)
    oa = _mla_attention(q, k, v, B, T)
    ob = _swa_attention(qs8, ks, vst, swa_sinks[0], tb_swa, B, T)
    h, y12, ri = _ffn(h, oa, ob, w_out_even[0], norm_ffn[0], moe_w_group[0], moe_b_group[0], moe_w_router[0],
                      moe_b_router[0], moe_w_gate, moe_w_up, moe_w_down, 0)

    h, vct, vdt, qc, kc, qd, kd = _combine_proj_odd(h, y12, ri, norm_mix[1][None, :], *_odd_weights(w_in_odd[0]),
                                                   _block_indicator(T), T)
    oc = _moba_attention(qc, kc, vct, tb_moba, B, T)
    od = _sb_attention(qd, kd, vdt, B, T)
    h, y12, ri = _ffn(h, oc, od, w_out_odd[0], norm_ffn[1], moe_w_group[1], moe_b_group[1], moe_w_router[1],
                      moe_b_router[1], moe_w_gate, moe_w_up, moe_w_down, 1)
    out = _combine_final(h, y12, ri, norm_final[None, :])
    return out.reshape(B, T, D)
```

```python
import functools
import math

import numpy as np
import jax
import jax.numpy as jnp
from jax import lax
from jax.experimental import pallas as pl
from jax.experimental.pallas import tpu as pltpu
from jax.experimental.pallas import tpu_sc as plsc

F32 = jnp.float32
BF16 = jnp.bfloat16
NEG = -1e30
EPS = 1e-6

D = 1024
HD = 64
MLA_HEADS, MLA_QR, MLA_KVR, MLA_NOPE, MLA_ROPE, MLA_V = 8, 256, 128, 64, 32, 64
ROPE_THETA = 10000.0
SWA_HEADS, SWA_KV, SWA_WIN = 8, 2, 128
MOBA_L, MOBA_TOPK = 256, 3
REL_BUCKETS, REL_MAX = 32, 2048
N_GROUPS, EPG, N_EXP, D_EXP = 4, 8, 32, 256
BIAS_CONST_FROM = 1792
MOBA_TQ = 2 * MOBA_L
BIAS_PAD = MOBA_L
BIAS_ROWS = BIAS_PAD + BIAS_CONST_FROM + MOBA_TQ
SB_DONE = -160.0
TM = 1024
TM_E = 512
VMEM_LIMIT = 56 * 1024 * 1024
SC_CORES, SC_SUBCORES = 2, 16
SC_CHUNK = 128
HALF = D // 2
VT_ROWS = HD + 16
LOG2E = math.log2(math.e)


def _pack_halves(x):
    lo = pltpu.bitcast(x[:, :HALF].astype(BF16).astype(F32), jnp.uint32) >> 16
    hi = pltpu.bitcast(x[:, HALF:].astype(BF16).astype(F32), jnp.uint32) & jnp.uint32(0xFFFF0000)
    return pltpu.bitcast(lo | hi, jnp.int32)


def _unpack_halves(w):
    w = pltpu.bitcast(w, jnp.uint32)
    return pltpu.bitcast(w << 16, F32), pltpu.bitcast(w & jnp.uint32(0xFFFF0000), F32)


def _cparams(sem):
    return pltpu.CompilerParams(dimension_semantics=sem, vmem_limit_bytes=VMEM_LIMIT)


def _dot(a, b):
    return jnp.dot(a, b, preferred_element_type=F32)


def _dot_nt(a, b):
    return lax.dot_general(a, b, (((1,), (1,)), ((), ())), preferred_element_type=F32)


def _rms(x, g):
    return x * lax.rsqrt(jnp.mean(x * x, axis=-1, keepdims=True) + EPS) * g


def _values_t(vt):
    ones = jnp.ones((VT_ROWS - HD, vt.shape[1]), vt.dtype)
    return jnp.concatenate([blk for h in range(8) for blk in (vt[HD * h:HD * (h + 1)], ones)], axis=0)


def _spread_heads(x, lane_of):
    lane = lax.broadcasted_iota(jnp.int32, (x.shape[0], 128), 1)
    out = []
    for h in range(8):
        pair = x[:, 128 * (h // 2):128 * (h // 2 + 1)]
        if HD * (h % 2) != lane_of(h):
            pair = pltpu.roll(pair, HD, 1)
        out.append(jnp.where((lane >= lane_of(h)) & (lane < lane_of(h) + HD), pair, 0.0))
    return out


def _t5_bucket_np(dist):
    n = np.maximum(dist, 0)
    max_exact = REL_BUCKETS // 2
    nf = np.maximum(n, 1).astype(np.float32)
    large = max_exact + (np.log(nf / np.float32(max_exact)) / np.float32(math.log(REL_MAX / max_exact))
                         * np.float32(REL_BUCKETS - max_exact)).astype(np.int32)
    large = np.minimum(large, REL_BUCKETS - 1)
    return np.where(n < max_exact, n, large).astype(np.int32)


def _bias_kernel(table_ref, bucket_ref, out_ref, *, scale, ranges):
    b = bucket_ref[...]
    for m, (lo, hi) in enumerate(ranges):
        @pl.when(pl.program_id(0) == m)
        def _():
            for h in range(8):
                acc = jnp.full(b.shape, NEG if lo == REL_BUCKETS else table_ref[h, lo] * scale, F32)
                for k in range(lo + 1, min(hi, REL_BUCKETS - 1) + 1):
                    acc = jnp.where(b == k, table_ref[h, k] * scale, acc)
                if hi == REL_BUCKETS:
                    acc = jnp.where(b == REL_BUCKETS, NEG, acc)
                out_ref[h] = acc


def _bias_lookup(rel_table, dist, scale, name, masked=None):
    R, C = dist.shape
    buckets = _t5_bucket_np(dist)
    if masked is not None:
        buckets = np.where(masked, REL_BUCKETS, buckets).astype(np.int32)
    ranges = tuple((int(buckets[:, c:c + 256].min()), int(buckets[:, c:c + 256].max())) for c in range(0, C, 256))
    return pl.pallas_call(
        functools.partial(_bias_kernel, scale=scale, ranges=ranges),
        grid=(C // 256,),
        in_specs=[pl.BlockSpec(memory_space=pltpu.SMEM),
                  pl.BlockSpec((R, 256), lambda m: (0, m))],
        out_specs=pl.BlockSpec((8, R, 256), lambda m: (0, 0, m)),
        out_shape=jax.ShapeDtypeStruct((8, R, C), F32),
        compiler_params=_cparams(("arbitrary",)),
        name=name,
    )(rel_table.T, jnp.asarray(buckets))


def _bias_tables(rel_table):
    d = np.arange(256)[None, :] - np.arange(256)[:, None]
    swa = _bias_lookup(rel_table, d, LOG2E, "bias_swa", masked=(d < 0) | (d >= SWA_WIN))
    moba = _bias_lookup(rel_table, np.arange(BIAS_ROWS)[None, :] - BIAS_PAD - np.arange(MOBA_L)[:, None], LOG2E,
                        "bias_moba")
    return swa, moba


def _proj_even_kernel(h_ref, g_ref, win_ref, gq_ref, gkv_ref, wqa_ref, wqb_ref, wk_ref, wv_ref,
                      cq_ref, sq_ref, ck_ref, sk_ref,
                      q_ref, k_ref, v_ref, qs_ref, ks_ref, vst_ref):
    u = _rms(h_ref[...], g_ref[...]).astype(BF16)
    nq = _rms(_dot(u, win_ref[:, 0:256]), gq_ref[...]).astype(BF16)
    qa = _dot(nq, wqa_ref[...])
    qb = _dot(nq, wqb_ref[...])
    cq, sq = cq_ref[...], sq_ref[...]
    for hh in range(MLA_HEADS):
        sl = slice(128 * hh, 128 * (hh + 1))
        q_ref[:, sl] = (qa[:, sl] * cq + qb[:, sl] * sq).astype(BF16)
    nkv = _rms(_dot(u, win_ref[:, 256:384]), gkv_ref[...]).astype(BF16)
    kn = _dot(nkv, wk_ref[...])
    v_ref[...] = _values_t(_dot_nt(wv_ref[...], nkv)).astype(BF16)
    rest = _dot(u, win_ref[:, 896:1408])
    kr = rest[:, 128:256] * ck_ref[...] + rest[:, 256:384] * sk_ref[...]
    for hh in range(MLA_HEADS):
        sl = slice(128 * hh, 128 * (hh + 1))
        k_ref[:, sl] = (kn[:, sl] + kr).astype(BF16)
    qs = _spread_heads(_dot(u, win_ref[:, 384:896]), lambda hq: HD * (hq // (SWA_HEADS // SWA_KV)))
    for hq in range(SWA_HEADS):
        qs_ref[:, 128 * hq:128 * (hq + 1)] = qs[hq].astype(BF16)
    ks_ref[...] = rest[:, 0:128].astype(BF16)
    vst_ref[...] = rest[:, 384:512].T.astype(BF16)


def _even_weights_kernel(wt_ref, wqb_in_ref, wkvb_ref, win_ref, wqa_ref, wqb_ref, wk_ref, wv_ref):
    o_kr = MLA_QR + MLA_KVR
    o_qs = o_kr + MLA_ROPE
    o_ks = o_qs + SWA_HEADS * HD
    o_vs = o_ks + SWA_KV * HD
    half = MLA_ROPE // 2
    wt = wt_ref[...]
    zero_rows = lambda n: jnp.zeros((n, D), F32)
    zero = lambda a, n: jnp.zeros((a.shape[0], n), F32)
    win_ref[...] = jnp.concatenate([
        wt[0:o_kr], wt[o_qs:o_ks] * (HD ** -0.5 * LOG2E), wt[o_ks:o_vs],
        zero_rows(64), wt[o_kr:o_qs], zero_rows(32),
        zero_rows(64), wt[o_kr + half:o_qs], wt[o_kr:o_kr + half], zero_rows(32),
        wt[o_vs:o_vs + SWA_KV * HD]], axis=0).T.astype(BF16)
    wq, wkv = wqb_in_ref[...], wkvb_ref[...]
    hq, hkv = MLA_NOPE + MLA_ROPE, MLA_NOPE + MLA_V
    qa, qb, kn, vt = [], [], [], []
    for h in range(MLA_HEADS):
        qa += [wq[:, hq * h:hq * (h + 1)], zero(wq, 32)]
        qb += [zero(wq, MLA_NOPE), wq[:, hq * h + MLA_NOPE + half:hq * (h + 1)],
               wq[:, hq * h + MLA_NOPE:hq * h + MLA_NOPE + half], zero(wq, 32)]
        kn += [wkv[:, hkv * h:hkv * h + MLA_NOPE], zero(wkv, MLA_V)]
        vt += [wkv[:, hkv * h + MLA_NOPE:hkv * (h + 1)]]
    wqa_ref[...] = jnp.concatenate(qa, axis=1).astype(BF16)
    wqb_ref[...] = jnp.concatenate(qb, axis=1).astype(BF16)
    wk_ref[...] = jnp.concatenate(kn, axis=1).astype(BF16)
    wv_ref[...] = jnp.concatenate(vt, axis=1).T.astype(BF16)


def _even_weights(w_in, w_qb, w_kvb):
    shapes =[(D, 1408), (MLA_QR, 128 * MLA_HEADS), (MLA_QR, 128 * MLA_HEADS), (MLA_KVR, 128 * MLA_HEADS),
              (MLA_V * MLA_HEADS, MLA_KVR)]
    return pl.pallas_call(
        _even_weights_kernel,
        out_shape=[jax.ShapeDtypeStruct(s, BF16) for s in shapes],
        compiler_params=_cparams(None),
        name="even_weights",
    )(w_in.T, w_qb, w_kvb)


def _rope_tables(T):
    f32 = np.float32
    freqs = f32(ROPE_THETA) ** (-np.arange(0, MLA_ROPE, 2, dtype=f32) / f32(MLA_ROPE))
    ang = np.arange(T, dtype=f32)[:, None] * freqs[None, :]
    cos, sin = np.cos(ang), np.sin(ang)
    z = lambda n: np.zeros((T, n), f32)
    scale = f32((MLA_NOPE + MLA_ROPE) ** -0.5 * LOG2E)
    ck = np.concatenate([z(64), cos, cos, z(32)], axis=1)
    sk = np.concatenate([z(64), -sin, sin, z(32)], axis=1)
    cq = np.concatenate([np.ones((T, 64), f32), cos, cos, z(32)], axis=1) * scale
    sq = sk * scale
    return tuple(jnp.asarray(t, F32) for t in (cq, sq, ck, sk))


def _proj_even(h, g, win, gq, gkv, wqa, wqb, wk, wv, tables, T):
    N = h.shape[0]
    nt = T // TM
    full = lambda a: pl.BlockSpec(a.shape, lambda i: (0,) * a.ndim)
    tab = pl.BlockSpec((TM, 128), lambda i: (i % nt, 0))
    row = lambda w: pl.BlockSpec((TM, w), lambda i: (i, 0))
    outs = (1024, 1024, -8 * VT_ROWS, 1024, 128, -128)
    spec = lambda w: row(w) if w > 0 else pl.BlockSpec((None, -w, TM), lambda i: (i // nt, 0, i % nt))
    shape = lambda w: jax.ShapeDtypeStruct((N, w) if w > 0 else (N // T, -w, T), BF16)
    return pl.pallas_call(
        _proj_even_kernel,
        grid=(N // TM,),
        in_specs=[row(D), full(g), full(win), full(gq), full(gkv), full(wqa), full(wqb), full(wk), full(wv),
                  tab, tab, tab, tab],
        out_specs=[spec(w) for w in outs],
        out_shape=[shape(w) for w in outs],
        compiler_params=_cparams(("arbitrary",)),
        name="proj_even",
    )(h, g, win, gq, gkv, wqa, wqb, wk, wv, *tables)


def _finish_t(accs):
    ot = jnp.concatenate([acc[:HD] / acc[HD:HD + 1] for acc in accs], axis=0)
    return ot.T.astype(BF16)


def _causal_attention_pair_t(ia, nq, q_of, k_ref, vt_ref, o_ref, s_ref, smax_ref, m_ref, acc_ref, *, tq,
                             bias_of=None):
    nh = s_ref.shape[1]
    first_query = (ia * tq, (nq - 1 - ia) * tq)
    n_items = nq + 1
    assert n_items % 2 == 1
    key = lax.broadcasted_iota(jnp.int32, (tq, tq), 0)
    qry = lax.broadcasted_iota(jnp.int32, (tq, tq), 1)

    def item(p):
        if isinstance(p, int) and p < 2:
            return p, first_query[p], first_query[p]
        t = jnp.where(p - 2 >= ia, 1, 0)
        return t, jnp.where(t == 1, first_query[1], first_query[0]), (p - 2 - t * ia) * tq

    def scores_to(slot, p):
        t, q0, start = item(p)
        start = pl.multiple_of(start, tq)
        for j in range(nh):
            st = _dot_nt(k_ref[pl.ds(start, tq), 128 * j:128 * (j + 1)], q_of(t, q0, j))
            if bias_of is not None:
                st = st + bias_of(start, q0, j)
            if isinstance(p, int) and p < 2:
                st = jnp.where(key <= qry, st, NEG)
            s_ref[slot, j] = st
            smax_ref[slot, j] = jnp.max(st, axis=0, keepdims=True)

    def update_from(slot, p):
        t, _, start = item(p)
        start = pl.multiple_of(start, tq)
        m_new = [jnp.maximum(m_ref[t, j], smax_ref[slot, j]) for j in range(nh)]
        pts = [jnp.exp2(s_ref[slot, j] - m_new[j]).astype(BF16) for j in range(nh)]
        for j in range(nh):
            acc_ref[t, j] = (jnp.exp2(m_ref[t, j] - m_new[j]) * acc_ref[t, j]
                             + _dot(vt_ref[VT_ROWS * j:VT_ROWS * (j + 1), pl.ds(start, tq)], pts[j]))
            m_ref[t, j] = m_new[j]

    def pair(k):
        scores_to(1, k + 1)
        update_from(0, k)
        scores_to(0, k + 2)
        update_from(1, k + 1)

    def quad(c, _):
        pair(2 + 4 * c)
        pair(4 + 4 * c)
        return 0

    m_ref[...] = jnp.full(m_ref.shape, NEG, F32)
    acc_ref[...] = jnp.zeros(acc_ref.shape, F32)
    scores_to(0, 0)
    pair(0)
    later_pairs = (n_items - 1) // 2 - 1
    lax.fori_loop(0, later_pairs // 2, quad, 0)
    if later_pairs % 2:
        pair(n_items - 3)
    update_from(0, n_items - 1)
    for t in range(2):
        o_ref[pl.ds(pl.multiple_of(first_query[t], tq), tq), :] = _finish_t([acc_ref[t, j] for j in range(nh)])


def _attention_t_scratch(tq):
    return [pltpu.VMEM((2, 2, tq, tq), F32), pltpu.VMEM((2, 2, 1, tq), F32), pltpu.VMEM((2, 2, 1, tq), F32),
            pltpu.VMEM((2, 2, VT_ROWS, tq), F32)]


def _mla_kernel(q_ref, k_ref, vt_ref, o_ref, s_ref, smax_ref, m_ref, acc_ref, *, tq, nq):
    def q_of(t, q0, j):
        return q_ref[pl.ds(pl.multiple_of(q0, tq), tq), 128 * j:128 * (j + 1)]

    _causal_attention_pair_t(pl.program_id(2), nq, q_of, k_ref, vt_ref, o_ref, s_ref, smax_ref, m_ref, acc_ref,
                             tq=tq)


def _mla_attention(q, k, vt, B, T, tq=512):
    nq = T // tq
    return pl.pallas_call(
        functools.partial(_mla_kernel, tq=tq, nq=nq),
        grid=(B, MLA_HEADS // 2, nq // 2),
        in_specs=[pl.BlockSpec((T, 256), lambda b, p, i: (b, p)),
                  pl.BlockSpec((T, 256), lambda b, p, i: (b, p)),
                  pl.BlockSpec((None, 2 * VT_ROWS, T), lambda b, p, i: (b, p, 0))],
        out_specs=pl.BlockSpec((T, 128), lambda b, p, i: (b, p)),
        out_shape=jax.ShapeDtypeStruct((B * T, MLA_HEADS * MLA_V), BF16),
        scratch_shapes=_attention_t_scratch(tq),
        compiler_params=_cparams(("arbitrary", "arbitrary", "arbitrary")),
        name="mla_attention",
    )(q, k, vt)


def _swa_kernel(sink_ref, q_ref, k_ref, vt_ref, tb_ref, o_ref, *, tq):
    i = pl.program_id(1)
    G = SWA_HEADS // SWA_KV
    W = 2 * SWA_WIN
    head = lax.broadcasted_iota(jnp.int32, (1, G * 128), 1) >> 7
    sinks = []
    for c in range(SWA_KV):
        sink = jnp.zeros((1, G * 128), F32)
        for g in range(G):
            sink = jnp.where(head == g, sink_ref[G * c + g], sink)
        sinks.append(sink)
    scores, values = [], []
    for r in range(tq // 128):
        qstart = i * tq + 128 * r
        kstart = pl.multiple_of(jnp.maximum(qstart - SWA_WIN, 0), 128)
        off = pl.multiple_of(qstart - kstart, 128)
        kw = k_ref[pl.ds(kstart, W), :]
        for c in range(SWA_KV):
            qg = jnp.concatenate([q_ref[128 * r:128 * (r + 1), 128 * (G * c + g):128 * (G * c + g + 1)]
                                  for g in range(G)], axis=0)
            bias = jnp.concatenate([tb_ref[G * c + g, :, pl.ds(off, 128)] for g in range(G)], axis=1)
            scores.append(_dot_nt(kw, qg) + bias)
            values.append(vt_ref[HD * c:HD * (c + 1), pl.ds(kstart, W)])
    probs = []
    for n, s in enumerate(scores):
        sink = sinks[n % SWA_KV]
        m = jnp.maximum(jnp.max(s, axis=0, keepdims=True), sink)
        p = jnp.exp2(s - m)
        probs.append((p.astype(BF16), jnp.sum(p, axis=0, keepdims=True) + jnp.exp2(sink - m)))
    for r in range(tq // 128):
        outs = []
        for c in range(SWA_KV):
            p, denom = probs[SWA_KV * r + c]
            ot = _dot(values[SWA_KV * r + c], p) / denom
            outs += [ot[:, 128 * g:128 * (g + 1)] for g in range(G)]
        o_ref[128 * r:128 * (r + 1), :] = jnp.concatenate(outs, axis=0).T.astype(BF16)


def _swa_attention(q8, k, vt, sinks, tb, B, T, tq=512):
    nq = T // tq
    return pl.pallas_call(
        functools.partial(_swa_kernel, tq=tq),
        grid=(B, nq),
        in_specs=[pl.BlockSpec(memory_space=pltpu.SMEM),
                  pl.BlockSpec((tq, 1024), lambda b, i: (b * nq + i, 0)),
                  pl.BlockSpec((T, 128), lambda b, i: (b, 0)),
                  pl.BlockSpec((None, 128, T), lambda b, i: (b, 0, 0)),
                  pl.BlockSpec((8, 256, 256), lambda b, i: (0, 0, 0))],
        out_specs=pl.BlockSpec((tq, 512), lambda b, i: (b * nq + i, 0)),
        out_shape=jax.ShapeDtypeStruct((B * T, SWA_HEADS * HD), BF16),
        compiler_params=_cparams(("arbitrary", "arbitrary")),
        name="swa_attention",
    )(sinks * LOG2E, q8, k, vt, tb)


def _outproj_router_kernel(h_ref, oa_ref, ob_ref, wo_ref, g_ref, wr_ref, br_ref,
                           hn_ref, u_ref, ri_ref, rg_ref, cnt_ref, base_ref):
    i = pl.program_id(0)

    @pl.when(i == 0)
    def _():
        base_ref[...] = jnp.zeros_like(base_ref)

    hn = h_ref[...] + _dot(oa_ref[...], wo_ref[0:512, :]) + _dot(ob_ref[...], wo_ref[512:1024, :])
    hn_ref[...] = hn
    u = _rms(hn, g_ref[...])
    u_ref[...] = _pack_halves(u)
    u_hi = u.astype(BF16)
    u_lo = (u - u_hi.astype(F32)).astype(BF16)
    both = _dot_nt(wr_ref[...], u_hi)
    logits = both[:128] + both[128:] + _dot_nt(wr_ref[0:128, :], u_lo) + br_ref[...]
    tm = logits.shape[1]
    sub = lax.broadcasted_iota(jnp.int32, (8, tm), 0).astype(F32)
    big = 1e6
    isg = sub < N_GROUPS
    gl = jnp.where(isg, logits[N_EXP:N_EXP + 8], NEG)
    gmax = jnp.max(gl, axis=0, keepdims=True)
    gsel = jnp.min(jnp.where(gl == gmax, sub, big), axis=0, keepdims=True)
    g_gate = 1.0 / jnp.sum(jnp.where(isg, jnp.exp(gl - gmax), 0.0), axis=0, keepdims=True)
    el = jnp.zeros((EPG, tm), F32)
    for grp in range(N_GROUPS):
        el = jnp.where(gsel == grp, logits[EPG * grp:EPG * (grp + 1)], el)
    m1 = jnp.max(el, axis=0, keepdims=True)
    i1 = jnp.min(jnp.where(el == m1, sub, big), axis=0, keepdims=True)
    el2 = jnp.where(sub == i1, NEG, el)
    m2 = jnp.max(el2, axis=0, keepdims=True)
    i2 = jnp.min(jnp.where(el2 == m2, sub, big), axis=0, keepdims=True)
    e1 = gsel * EPG + i1
    e2 = gsel * EPG + i2
    r = jnp.exp(m2 - m1)
    gate1 = g_gate / (1.0 + r)
    gate2 = g_gate * r / (1.0 + r)
    rows = lax.broadcasted_iota(jnp.int32, (128, tm), 0).astype(F32)
    oh1 = (rows == e1).astype(F32)
    oh2 = (rows == e2).astype(F32)
    oh = oh1 + oh2
    t_row = lax.broadcasted_iota(jnp.int32, (tm, tm), 0)
    t_col = lax.broadcasted_iota(jnp.int32, (tm, tm), 1)
    earlier = jnp.where(t_row < t_col, 1.0, 0.0).astype(BF16)
    prefix = _dot(oh.astype(BF16), earlier) + base_ref[:, 0:1]
    rank1 = jnp.sum(oh1 * prefix, axis=0, keepdims=True)
    rank2 = jnp.sum(oh2 * prefix, axis=0, keepdims=True)
    base_ref[...] = base_ref[...] + jnp.sum(oh, axis=1, keepdims=True)
    cnt_ref[...] = base_ref[...]
    ri_ref[...] = jnp.where(sub == 0, e1, jnp.where(sub == 1, e2, jnp.where(sub == 2, rank1,
                            jnp.where(sub == 3, rank2, 0.0))))
    rg_ref[...] = jnp.where(rows == 0, gate1, jnp.where(rows == 1, gate2, 0.0)).T


def _outproj_router(h, oa, ob, wo, g, wr, br):
    N = h.shape[0]
    full = lambda a: pl.BlockSpec(a.shape, lambda i: (0,) * a.ndim)
    row = lambda w: pl.BlockSpec((TM, w), lambda i: (i, 0))
    return pl.pallas_call(
        _outproj_router_kernel,
        grid=(N // TM,),
        in_specs=[row(D), row(512), row(512), full(wo), full(g), full(wr), full(br)],
        out_specs=[row(D), row(HALF), pl.BlockSpec((8, TM), lambda i: (0, i)), row(128),
                   pl.BlockSpec((128, 128), lambda i: (0, 0))],
        out_shape=[jax.ShapeDtypeStruct((N, D), F32), jax.ShapeDtypeStruct((N, HALF), jnp.int32),
                   jax.ShapeDtypeStruct((8, N), F32), jax.ShapeDtypeStruct((N, 128), F32),
                   jax.ShapeDtypeStruct((128, 128), F32)],
        scratch_shapes=[pltpu.VMEM((128, 128), F32)],
        compiler_params=_cparams(("arbitrary",)),
        name="outproj_router",
    )(h, oa, ob, wo, g, wr, br)


def _moe_kernel(te_ref, tv_ref, run_ref, nx_ref, tb_ref, x_ref, wg_hbm, wu_hbm, wd_hbm, o_ref,
                wg_buf, wu_buf, wd_buf, sems, wgu_s, wd_s, *, layer):
    i = pl.program_id(0)
    valid = tv_ref[i]

    def weight_copies(e, slot):
        return (pltpu.make_async_copy(wg_hbm.at[layer, e], wg_buf.at[slot], sems.at[slot, 0]),
                pltpu.make_async_copy(wu_hbm.at[layer, e], wu_buf.at[slot], sems.at[slot, 1]),
                pltpu.make_async_copy(wd_hbm.at[layer, e], wd_buf.at[slot], sems.at[slot, 2]))

    @pl.when(run_ref[i] > 0)
    def _():
        slot = run_ref[i] - 1

        @pl.when(i == 0)
        def _():
            for c in weight_copies(te_ref[i], slot):
                c.start()

        for c in weight_copies(te_ref[i], slot):
            c.wait()
        wgu_s[:, :D_EXP] = wg_buf[slot].astype(BF16)
        wgu_s[:, D_EXP:] = wu_buf[slot].astype(BF16)
        wd_s[...] = wd_buf[slot].astype(BF16)

        @pl.when(nx_ref[i] >= 0)
        def _():
            for c in weight_copies(nx_ref[i], 1 - slot):
                c.start()

    @pl.when(valid > 0)
    def _():
        rows = lax.broadcasted_iota(jnp.int32, x_ref.shape, 0)
        lo, hi = _unpack_halves(jnp.where(rows < valid, x_ref[...], 0))
        hgu = _dot(lo.astype(BF16), wgu_s[:HALF, :]) + _dot(hi.astype(BF16), wgu_s[HALF:, :])
        hg, hu = hgu[:, :D_EXP], hgu[:, D_EXP:]
        a = hg * (1.0 / (1.0 + jnp.exp(-hg))) * hu
        o_ref[...] = _pack_halves(_dot(a.astype(BF16), wd_s[...]))


def _moe_experts(xs, plan, w_gate, w_up, w_down, layer):
    P = xs.shape[0]
    hbm = pl.BlockSpec(memory_space=pl.ANY)
    grid_spec = pltpu.PrefetchScalarGridSpec(
        num_scalar_prefetch=5,
        grid=(P // TM_E,),
        in_specs=[pl.BlockSpec((TM_E, HALF), lambda i, *plan: (plan[4][i], 0)), hbm, hbm, hbm],
        out_specs=pl.BlockSpec((TM_E, HALF), lambda i, *plan: (plan[4][i], 0)),
        scratch_shapes=[pltpu.VMEM((2, D, D_EXP), F32), pltpu.VMEM((2, D, D_EXP), F32),
                        pltpu.VMEM((2, D_EXP, D), F32), pltpu.SemaphoreType.DMA((2, 3)),
                        pltpu.VMEM((D, 2 * D_EXP), BF16), pltpu.VMEM((D_EXP, D), BF16)],
    )
    return pl.pallas_call(
        functools.partial(_moe_kernel, layer=layer),
        grid_spec=grid_spec,
        out_shape=jax.ShapeDtypeStruct((P, HALF), jnp.int32),
        compiler_params=_cparams(("arbitrary",)),
        name="moe_experts",
    )(*plan, xs, w_gate, w_up, w_down)


def _sc_mesh():
    return plsc.VectorSubcoreMesh(core_axis_name="c", subcore_axis_name="s",
                                  num_cores=SC_CORES, num_subcores=SC_SUBCORES)


def _sc_scatter_rows(src, d1, d2, P):
    N, W = src.shape
    per_w = N // (SC_CORES * SC_SUBCORES)

    @functools.partial(pl.kernel, mesh=_sc_mesh(), out_type=jax.ShapeDtypeStruct((P, W), src.dtype),
                       scratch_types=[pltpu.VMEM((SC_CHUNK,), jnp.int32), pltpu.VMEM((SC_CHUNK, W), src.dtype)],
                       name="sc_scatter_rows")
    def k(src_hbm, d1_hbm, d2_hbm, out_hbm, idx_v, rows_v):
        wid = lax.axis_index("s") * SC_CORES + lax.axis_index("c")

        @pl.loop(0, per_w // SC_CHUNK)
        def _(c):
            off = wid * per_w + c * SC_CHUNK
            pltpu.sync_copy(src_hbm.at[pl.ds(off, SC_CHUNK)], rows_v)
            pltpu.sync_copy(d1_hbm.at[pl.ds(off, SC_CHUNK)], idx_v)
            pltpu.sync_copy(rows_v, out_hbm.at[idx_v])
            pltpu.sync_copy(d2_hbm.at[pl.ds(off, SC_CHUNK)], idx_v)
            pltpu.sync_copy(rows_v, out_hbm.at[idx_v])

    return k(src, d1, d2)


def _sc_gather_rows(table, idx):
    B, W = idx.shape[0], table.shape[1]
    per_w = B // (SC_CORES * SC_SUBCORES)

    @functools.partial(pl.kernel, mesh=_sc_mesh(), out_type=jax.ShapeDtypeStruct((B, W), table.dtype),
                       scratch_types=[pltpu.VMEM((SC_CHUNK,), jnp.int32), pltpu.VMEM((SC_CHUNK, W), table.dtype)],
                       name="sc_gather_rows")
    def k(table_hbm, idx_hbm, out_hbm, idx_v, rows_v):
        wid = lax.axis_index("s") * SC_CORES + lax.axis_index("c")

        @pl.loop(0, per_w // SC_CHUNK)
        def _(c):
            off = wid * per_w + c * SC_CHUNK
            pltpu.sync_copy(idx_hbm.at[pl.ds(off, SC_CHUNK)], idx_v)
            pltpu.sync_copy(table_hbm.at[idx_v], rows_v)
            pltpu.sync_copy(rows_v, out_hbm.at[pl.ds(off, SC_CHUNK)])

    return k(table, idx)


def _dest_kernel(offs_ref, ri_ref, d_ref):
    ri = ri_ref[...]
    base = jnp.zeros(ri.shape, F32)
    for e in range(N_EXP):
        base = jnp.where(ri == e, offs_ref[e].astype(F32), base)
    d_ref[...] = (base + pltpu.roll(ri, 6, 0)).astype(jnp.int32)


def _dispatch_plan(ri, cnt, N):
    counts = cnt[:N_EXP, 0].astype(jnp.int32)
    padded = ((counts + TM_E - 1) // TM_E) * TM_E
    ends = jnp.cumsum(padded)
    offs = ends - padded
    take = lambda t, i: t.at[i].get(mode="promise_in_bounds")
    dest = pl.pallas_call(
        _dest_kernel,
        in_specs=[pl.BlockSpec(memory_space=pltpu.SMEM), pl.BlockSpec((8, N), lambda: (0, 0))],
        out_specs=pl.BlockSpec((8, N), lambda: (0, 0)),
        out_shape=jax.ShapeDtypeStruct((8, N), jnp.int32),
        name="dispatch_rows",
    )(offs, ri)
    d1, d2 = dest[0], dest[1]
    P = 2 * N + N_EXP * TM_E
    tile_start = jnp.arange(P // TM_E, dtype=jnp.int32) * TM_E
    tile_expert = jnp.minimum(jnp.sum((ends[None, :] <= tile_start[:, None]).astype(jnp.int32), axis=1), N_EXP - 1)
    tile_valid = jnp.clip(take(offs + counts, tile_expert) - tile_start, 0, TM_E)
    n_tiles = P // TM_E
    tile = jnp.arange(n_tiles, dtype=jnp.int32)
    starts = (tile_valid > 0) & ((tile == 0) | (tile_expert != jnp.roll(tile_expert, 1)))
    slot = (jnp.cumsum(starts.astype(jnp.int32)) - 1) % 2
    run_code = jnp.where(starts, slot + 1, 0).astype(jnp.int32)
    start_at = jnp.where(starts, tile, n_tiles)
    next_start = jnp.concatenate([lax.cummin(start_at, reverse=True)[1:], jnp.full((1,), n_tiles, jnp.int32)])
    next_expert = jnp.where(next_start < n_tiles, take(tile_expert, jnp.minimum(next_start, n_tiles - 1)), -1)
    tile_block = jnp.minimum(tile, ends[N_EXP - 1] // TM_E - 1)
    return d1, d2, P, (tile_expert, tile_valid, run_code, next_expert.astype(jnp.int32), tile_block)


def _combine(h_ref, y1_ref, y2_ref, rg_ref):
    rg = rg_ref[...]
    lo1, hi1 = _unpack_halves(y1_ref[...])
    lo2, hi2 = _unpack_halves(y2_ref[...])
    g1, g2 = rg[:, 0:1], rg[:, 1:2]
    return h_ref[...] + jnp.concatenate([g1 * lo1 + g2 * lo2, g1 * hi1 + g2 * hi2], axis=1)


ODD_WIDTHS = (1024, 1024, 512, 512)


def _combine_proj_odd_kernel(h_ref, y1_ref, y2_ref, rg_ref, g_ref, win_ref, wvt_ref, ind_ref, hn_ref, vt_ref,
                             vdt_ref, *out_refs):
    hn = _combine(h_ref, y1_ref, y2_ref, rg_ref)
    hn_ref[...] = hn
    u = _rms(hn, g_ref[...]).astype(BF16)
    vt_ref[...] = _values_t(_dot_nt(wvt_ref[0:512, :], u)).astype(BF16)
    vdt_ref[...] = _dot_nt(wvt_ref[512:1024, :], u).astype(BF16)
    qc_ref, kc_ref, qd_ref, kd_ref = out_refs
    qc = _spread_heads(_dot(u, win_ref[:, 0:512]), lambda h: 0)
    kc = _spread_heads(_dot(u, win_ref[:, 512:1024]), lambda h: 0)
    for hh in range(8):
        qc_ref[:, 128 * hh:128 * (hh + 1)] = qc[hh].astype(BF16)
        kc_ref[:, 128 * hh:128 * (hh + 1)] = (kc[hh] + ind_ref[...]).astype(BF16)
    qd_ref[...] = _dot(u, win_ref[:, 1024:1536]).astype(BF16)
    kd_ref[...] = _dot(u, win_ref[:, 1536:2048]).astype(BF16)


def _odd_weights_kernel(w_ref, win_ref, wvt_ref):
    win_ref[:, 0:512] = (w_ref[:, 0:512] * (HD ** -0.5 * LOG2E)).astype(BF16)
    win_ref[:, 512:1024] = w_ref[:, 512:1024].astype(BF16)
    wvt_ref[...] = w_ref[:, 1024:1536].T.astype(BF16)


def _odd_weights(w_in):
    return pl.pallas_call(
        _odd_weights_kernel,
        grid=(2,),
        in_specs=[pl.BlockSpec((D, 1536), lambda i: (0, i))],
        out_specs=[pl.BlockSpec((D, 1024), lambda i: (0, i)), pl.BlockSpec((512, D), lambda i: (i, 0))],
        out_shape=[jax.ShapeDtypeStruct((D, 2048), BF16), jax.ShapeDtypeStruct((1024, D), BF16)],
        compiler_params=_cparams(("arbitrary",)),
        name="odd_weights",
    )(w_in)


def _block_indicator(T):
    assert T // MOBA_L <= 64
    ind = np.zeros((T, 128), np.float32)
    ind[np.arange(T), 64 + np.arange(T) // MOBA_L] = 1.0
    return jnp.asarray(ind)


def _y_specs(N):
    return [pl.BlockSpec((TM, HALF), lambda i: (i, 0)), pl.BlockSpec((TM, HALF), lambda i: (i + N // TM, 0))]


def _combine_proj_odd(h, y12, rg, g, win, wvt, ind, T):
    N = h.shape[0]
    nt = T // TM
    full = lambda a: pl.BlockSpec(a.shape, lambda i: (0,) * a.ndim)
    row = lambda w: pl.BlockSpec((TM, w), lambda i: (i, 0))
    return pl.pallas_call(
        _combine_proj_odd_kernel,
        grid=(N // TM,),
        in_specs=[row(D)] + _y_specs(N) + [row(128), full(g), full(win), full(wvt),
                  pl.BlockSpec((TM, 128), lambda i: (i % nt, 0))],
        out_specs=[row(D), pl.BlockSpec((None, 8 * VT_ROWS, TM), lambda i: (i // nt, 0, i % nt)),
                   pl.BlockSpec((None, 512, TM), lambda i: (i // nt, 0, i % nt))]
        + [row(w) for w in ODD_WIDTHS],
        out_shape=[jax.ShapeDtypeStruct((N, D), F32), jax.ShapeDtypeStruct((N // T, 8 * VT_ROWS, T), BF16),
                   jax.ShapeDtypeStruct((N // T, 512, T), BF16)]
        + [jax.ShapeDtypeStruct((N, w), BF16) for w in ODD_WIDTHS],
        compiler_params=_cparams(("arbitrary",)),
        name="combine_proj_odd",
    )(h, y12, y12, rg, g, win, wvt, ind)


def _combine_final_kernel(h_ref, y1_ref, y2_ref, rg_ref, g_ref, o_ref):
    o_ref[...] = _rms(_combine(h_ref, y1_ref, y2_ref, rg_ref), g_ref[...])


def _combine_final(h, y12, rg, g):
    N = h.shape[0]
    row = lambda w: pl.BlockSpec((TM, w), lambda i: (i, 0))
    return pl.pallas_call(
        _combine_final_kernel,
        grid=(N // TM,),
        in_specs=[row(D)] + _y_specs(N) + [row(128), pl.BlockSpec((1, D), lambda i: (0, 0))],
        out_specs=row(D),
        out_shape=jax.ShapeDtypeStruct((N, D), F32),
        compiler_params=_cparams(("arbitrary",)),
        name="combine_final",
    )(h, y12, y12, rg, g)


def _moba_kernel(q_ref, k_ref, vt_ref, tb_ref, o_ref, kmean_ref, ksplit_ref, qa_ref, s_ref, smax_ref, m_ref, acc_ref,
                 *, nkb, nq):
    ia = pl.program_id(2)
    L, tq = MOBA_L, MOBA_TQ

    @pl.when(ia == 0)
    def _():
        kmean_ref[...] = jnp.zeros_like(kmean_ref)
        for n in range(nkb):
            kmean_ref[64 + n:65 + n, :] = jnp.sum(k_ref[n * L:(n + 1) * L, :].astype(F32), axis=0,
                                                  keepdims=True) * (1.0 / L)
        hi = kmean_ref[...].astype(BF16)
        ksplit_ref[0] = hi
        ksplit_ref[1] = (kmean_ref[...] - hi.astype(F32)).astype(BF16)

    nb = -(-nkb // 8) * 8
    lane = lax.broadcasted_iota(jnp.int32, (tq, 128), 1)
    blk = lax.broadcasted_iota(jnp.int32, (nb, tq), 0)
    half = jnp.where(lax.broadcasted_iota(jnp.int32, (nb, tq), 1) >= L, 1, 0)
    blk_f = blk.astype(F32)
    for t, tile in enumerate((ia, nq - 1 - ia)):
        own = 2 * tile + half
        for j in range(2):
            qj = q_ref[pl.ds(pl.multiple_of(tile * tq, tq), tq), 128 * j:128 * (j + 1)]
            gsc = (_dot_nt(ksplit_ref[0, 64:64 + nb, 128 * j:128 * (j + 1)], qj)
                   + _dot_nt(ksplit_ref[1, 64:64 + nb, 128 * j:128 * (j + 1)], qj))
            g = jnp.where(blk < own, gsc, NEG)
            allowed = jnp.where(blk == own, 1.0, 0.0)
            for _ in range(MOBA_TOPK):
                mx = jnp.max(g, axis=0, keepdims=True)
                idx = jnp.min(jnp.where(g == mx, blk_f, 1e6), axis=0, keepdims=True)
                pick = blk_f == idx
                allowed = jnp.maximum(allowed, jnp.where(pick, jnp.where(mx > 0.5 * NEG, 1.0, 0.0), 0.0))
                g = jnp.where(pick, 2.0 * NEG, g)
            mask = jnp.concatenate([jnp.zeros((64, tq), F32), jnp.where(allowed > 0.5, 0.0, NEG),
                                    jnp.zeros((64 - nb, tq), F32)], axis=0)
            qa_ref[t, j] = jnp.where(lane < HD, qj, mask.T.astype(BF16))

    def bias_of(start, q0, j):
        parts = []
        for c in range(tq // L):
            off = jnp.minimum(q0 - start - c * L, BIAS_CONST_FROM) + BIAS_PAD
            parts.append(tb_ref[j, :, pl.ds(pl.multiple_of(off, L), tq)])
        return jnp.concatenate(parts, axis=0)

    _causal_attention_pair_t(ia, nq, lambda t, q0, j: qa_ref[t, j], k_ref, vt_ref, o_ref, s_ref, smax_ref, m_ref,
                             acc_ref, tq=tq, bias_of=bias_of)


def _moba_attention(q, k, vt, tb, B, T):
    tq = MOBA_TQ
    nq = T // tq
    return pl.pallas_call(
        functools.partial(_moba_kernel, nkb=T // MOBA_L, nq=nq),
        grid=(B, 4, nq // 2),
        in_specs=[pl.BlockSpec((T, 256), lambda b, p, i: (b, p)),
                  pl.BlockSpec((T, 256), lambda b, p, i: (b, p)),
                  pl.BlockSpec((None, 2 * VT_ROWS, T), lambda b, p, i: (b, p, 0)),
                  pl.BlockSpec((2, MOBA_L, BIAS_ROWS), lambda b, p, i: (p, 0, 0))],
        out_specs=pl.BlockSpec((T, 128), lambda b, p, i: (b, p)),
        out_shape=jax.ShapeDtypeStruct((B * T, 512), BF16),
        scratch_shapes=[pltpu.VMEM((128, 256), F32), pltpu.VMEM((2, 128, 256), BF16),
                        pltpu.VMEM((2, 2, tq, 128), BF16)]
        + _attention_t_scratch(tq),
        compiler_params=_cparams(("arbitrary", "arbitrary", "arbitrary")),
        name="moba_attention",
    )(q, k, vt, tb)


def _sb_kernel(q_ref, k_ref, vt_ref, o_ref, *, tq, pairs):
    i = pl.program_id(2)
    lane = lax.broadcasted_iota(jnp.int32, (tq, 128), 1)
    qs = []
    for p in range(pairs):
        q_pair = q_ref[:, 128 * p:128 * (p + 1)]
        qs += [jnp.where(lane < HD, q_pair, 0).astype(BF16), jnp.where(lane >= HD, q_pair, 0).astype(BF16)]
    key = lax.broadcasted_iota(jnp.int32, (tq, tq), 0)
    qry = lax.broadcasted_iota(jnp.int32, (tq, tq), 1)
    past = key < qry
    suffix = jnp.where(qry >= key, -1.0, 0.0).astype(BF16)
    heads = range(2 * pairs)

    def step(kb, carry, boundary):
        start = pl.multiple_of(kb * tq, tq)
        zs = [_dot_nt(k_ref[pl.ds(start, tq), 128 * (h // 2):128 * (h // 2 + 1)], qs[h]) for h in heads]
        parts = []
        for z in zs:
            neg_abs = pltpu.bitcast(pltpu.bitcast(z, jnp.uint32) | jnp.uint32(0x80000000), F32)
            soft = jnp.maximum(z, 0.0) + jnp.log2(1.0 + jnp.exp2(neg_abs))
            if boundary:
                soft = jnp.where(past, soft, 0.0)
            parts.append(soft.astype(BF16))
        sums = [_dot(suffix, part) for part in parts]
        ws = []
        for h in heads:
            w = jnp.exp2(zs[h] + (carry[h][0] + sums[h]))
            ws.append((jnp.where(past, w, 0.0) if boundary else w).astype(BF16))
        return tuple((carry[h][0] + sums[h][0:1], carry[h][1]
                      + _dot(vt_ref[128 * (h // 2):128 * (h // 2 + 1), pl.ds(start, tq)], ws[h])) for h in heads)

    def live(carry):
        top = carry[0][0]
        for c, _ in carry[1:]:
            top = jnp.maximum(top, c)
        return (jnp.max(top) > SB_DONE).astype(jnp.int32)

    def body(state):
        t, _, carry = state
        carry = step(i - 1 - t, carry, False)
        return t + 1, live(carry), carry

    init = tuple((jnp.zeros((1, tq), F32), jnp.zeros((128, tq), F32)) for _ in range(2 * pairs))
    carry = step(i, init, True)
    _, _, carry = lax.while_loop(lambda st: (st[0] < i) & (st[1] > 0), body, (jnp.int32(0), live(carry), carry))
    rows = lax.broadcasted_iota(jnp.int32, (128, tq), 0)
    for p in range(pairs):
        ot = jnp.where(rows < HD, carry[2 * p][1], carry[2 * p + 1][1])
        o_ref[:, 128 * p:128 * (p + 1)] = ot.T.astype(BF16)


def _sb_attention(q, k, vt, B, T, tq=256, pairs=4):
    nq = T // tq
    w = 128 * pairs
    return pl.pallas_call(
        functools.partial(_sb_kernel, tq=tq, pairs=pairs),
        grid=(B, 4 // pairs, nq),
        in_specs=[pl.BlockSpec((tq, w), lambda b, p, i: (b * nq + i, p)),
                  pl.BlockSpec((T, w), lambda b, p, i: (b, p)),
                  pl.BlockSpec((None, w, T), lambda b, p, i: (b, p, 0))],
        out_specs=pl.BlockSpec((tq, w), lambda b, p, i: (b * nq + i, p)),
        out_shape=jax.ShapeDtypeStruct((B * T, 512), BF16),
        compiler_params=_cparams(("arbitrary", "arbitrary", "arbitrary")),
        name="sb_attention",
    )(q, k, vt)


def _router_weights(w_group, b_group, w_router, b_router):
    pad = 128 - N_EXP - N_GROUPS
    wr = jnp.concatenate([w_router, w_group, jnp.zeros((D, pad), F32)], axis=1).T
    hi = wr.astype(BF16)
    lo = (wr - hi.astype(F32)).astype(BF16)
    br = jnp.concatenate([b_router, b_group, jnp.zeros((pad,), F32)])[:, None]
    return jnp.concatenate([hi, lo], axis=0), br


def _ffn(h, oa, ob, wo, g, w_group, b_group, w_router, b_router, w_gate, w_up, w_down, layer):
    wr, br = _router_weights(w_group, b_group, w_router, b_router)
    hn, u, ri, rg, cnt = _outproj_router(h, oa, ob, wo.astype(BF16), g[None, :], wr, br)
    d1, d2, P, plan = _dispatch_plan(ri, cnt, h.shape[0])
    xs = _sc_scatter_rows(u, d1, d2, P)
    ys = _moe_experts(xs, plan, w_gate, w_up, w_down, layer)
    return hn, _sc_gather_rows(ys, jnp.concatenate([d1, d2])), rg


def kernel(x, norm_mix, norm_ffn, norm_final, rel_bias_table, w_in_even, g_mla_q, g_mla_kv, w_mla_qb, w_mla_kvb, swa_sinks, w_out_even, w_in_odd, w_out_odd, moe_w_group, moe_b_group, moe_w_router, moe_b_router, moe_w_gate, moe_w_up, moe_w_down):
    B, T, _ = x.shape
    h = x.reshape(B * T, D)
    tb_swa, tb_moba = _bias_tables(rel_bias_table)

    win, wqa, wqb, wk, wv = _even_weights(w_in_even[0], w_mla_qb[0], w_mla_kvb[0])
    q, k, v, qs8, ks, vst = _proj_even(h, norm_mix[0][None, :], win, g_mla_q[0][None, :], g_mla_kv[0][None, :],
                                       wqa, wqb, wk, wv, _rope_tables(T), T)
    oa = _mla_attention(q, k, v, B, T)
    ob = _swa_attention(qs8, ks, vst, swa_sinks[0], tb_swa, B, T)
    h, y12, rg = _ffn(h, oa, ob, w_out_even[0], norm_ffn[0], moe_w_group[0], moe_b_group[0], moe_w_router[0],
                      moe_b_router[0], moe_w_gate, moe_w_up, moe_w_down, 0)

    h, vct, vdt, qc, kc, qd, kd = _combine_proj_odd(h, y12, rg, norm_mix[1][None, :], *_odd_weights(w_in_odd[0]),
                                                   _block_indicator(T), T)
    oc = _moba_attention(qc, kc, vct, tb_moba, B, T)
    od = _sb_attention(qd, kd, vdt, B, T)
    h, y12, rg = _ffn(h, oc, od, w_out_odd[0], norm_ffn[1], moe_w_group[1], moe_b_group[1], moe_w_router[1],
                      moe_b_router[1], moe_w_gate, moe_w_up, moe_w_down, 1)
    out = _combine_final(h, y12, rg, norm_final[None, :])
    return out.reshape(B, T, D)
```

```python
import functools
import math

import numpy as np
import jax
import jax.numpy as jnp
from jax import lax
from jax.experimental import pallas as pl
from jax.experimental.pallas import tpu as pltpu
from jax.experimental.pallas import tpu_sc as plsc

F32 = jnp.float32
BF16 = jnp.bfloat16
NEG = -1e30
EPS = 1e-6

D = 1024
HD = 64
MLA_HEADS, MLA_QR, MLA_KVR, MLA_NOPE, MLA_ROPE, MLA_V = 8, 256, 128, 64, 32, 64
ROPE_THETA = 10000.0
SWA_HEADS, SWA_KV, SWA_WIN = 8, 2, 128
MOBA_L, MOBA_TOPK = 256, 3
REL_BUCKETS, REL_MAX = 32, 2048
N_GROUPS, EPG, N_EXP, D_EXP = 4, 8, 32, 256
BIAS_CONST_FROM = 1792
MOBA_TQ = 2 * MOBA_L
BIAS_PAD = MOBA_L
BIAS_ROWS = BIAS_PAD + BIAS_CONST_FROM + MOBA_TQ
SB_DONE = -160.0
TM = 1024
TM_E = 512
VMEM_LIMIT = 56 * 1024 * 1024
SC_CORES, SC_SUBCORES = 2, 16
SC_CHUNK = 128
HALF = D // 2
VT_ROWS = HD + 16
LOG2E = math.log2(math.e)


def _pack_halves(x):
    lo = pltpu.bitcast(x[:, :HALF].astype(BF16).astype(F32), jnp.uint32) >> 16
    hi = pltpu.bitcast(x[:, HALF:].astype(BF16).astype(F32), jnp.uint32) & jnp.uint32(0xFFFF0000)
    return pltpu.bitcast(lo | hi, jnp.int32)


def _unpack_halves(w):
    w = pltpu.bitcast(w, jnp.uint32)
    return pltpu.bitcast(w << 16, F32), pltpu.bitcast(w & jnp.uint32(0xFFFF0000), F32)


def _cparams(sem):
    return pltpu.CompilerParams(dimension_semantics=sem, vmem_limit_bytes=VMEM_LIMIT)


def _dot(a, b):
    return jnp.dot(a, b, preferred_element_type=F32)


def _dot_nt(a, b):
    return lax.dot_general(a, b, (((1,), (1,)), ((), ())), preferred_element_type=F32)


def _rms(x, g):
    return x * lax.rsqrt(jnp.mean(x * x, axis=-1, keepdims=True) + EPS) * g


def _values_t(vt):
    ones = jnp.ones((VT_ROWS - HD, vt.shape[1]), vt.dtype)
    return jnp.concatenate([blk for h in range(8) for blk in (vt[HD * h:HD * (h + 1)], ones)], axis=0)


def _spread_heads(x, lane_of):
    lane = lax.broadcasted_iota(jnp.int32, (x.shape[0], 128), 1)
    out = []
    for h in range(8):
        pair = x[:, 128 * (h // 2):128 * (h // 2 + 1)]
        if HD * (h % 2) != lane_of(h):
            pair = pltpu.roll(pair, HD, 1)
        out.append(jnp.where((lane >= lane_of(h)) & (lane < lane_of(h) + HD), pair, 0.0))
    return out


def _t5_bucket_np(dist):
    n = np.maximum(dist, 0)
    max_exact = REL_BUCKETS // 2
    nf = np.maximum(n, 1).astype(np.float32)
    large = max_exact + (np.log(nf / np.float32(max_exact)) / np.float32(math.log(REL_MAX / max_exact))
                         * np.float32(REL_BUCKETS - max_exact)).astype(np.int32)
    large = np.minimum(large, REL_BUCKETS - 1)
    return np.where(n < max_exact, n, large).astype(np.int32)


def _bias_kernel(table_ref, bucket_ref, out_ref, *, scale, ranges):
    b = bucket_ref[...]
    for m, (lo, hi) in enumerate(ranges):
        @pl.when(pl.program_id(0) == m)
        def _():
            for h in range(8):
                acc = jnp.full(b.shape, NEG if lo == REL_BUCKETS else table_ref[h, lo] * scale, F32)
                for k in range(lo + 1, min(hi, REL_BUCKETS - 1) + 1):
                    acc = jnp.where(b == k, table_ref[h, k] * scale, acc)
                if hi == REL_BUCKETS:
                    acc = jnp.where(b == REL_BUCKETS, NEG, acc)
                out_ref[h] = acc


def _bias_lookup(rel_table, dist, scale, name, masked=None):
    R, C = dist.shape
    buckets = _t5_bucket_np(dist)
    if masked is not None:
        buckets = np.where(masked, REL_BUCKETS, buckets).astype(np.int32)
    ranges = tuple((int(buckets[:, c:c + 256].min()), int(buckets[:, c:c + 256].max())) for c in range(0, C, 256))
    return pl.pallas_call(
        functools.partial(_bias_kernel, scale=scale, ranges=ranges),
        grid=(C // 256,),
        in_specs=[pl.BlockSpec(memory_space=pltpu.SMEM),
                  pl.BlockSpec((R, 256), lambda m: (0, m))],
        out_specs=pl.BlockSpec((8, R, 256), lambda m: (0, 0, m)),
        out_shape=jax.ShapeDtypeStruct((8, R, C), F32),
        compiler_params=_cparams(("arbitrary",)),
        name=name,
    )(rel_table.T, jnp.asarray(buckets))


def _bias_tables(rel_table):
    d = np.arange(256)[None, :] - np.arange(256)[:, None]
    swa = _bias_lookup(rel_table, d, LOG2E, "bias_swa", masked=(d < 0) | (d >= SWA_WIN))
    moba = _bias_lookup(rel_table, np.arange(BIAS_ROWS)[None, :] - BIAS_PAD - np.arange(MOBA_L)[:, None], LOG2E,
                        "bias_moba")
    return swa, moba


def _proj_even_kernel(h_ref, g_ref, win_ref, gq_ref, gkv_ref, wqa_ref, wqb_ref, wk_ref, wv_ref,
                      cq_ref, sq_ref, ck_ref, sk_ref,
                      q_ref, k_ref, v_ref, qs_ref, ks_ref, vst_ref):
    u = _rms(h_ref[...], g_ref[...]).astype(BF16)
    nq = _rms(_dot(u, win_ref[:, 0:256]), gq_ref[...]).astype(BF16)
    qa = _dot(nq, wqa_ref[...])
    qb = _dot(nq, wqb_ref[...])
    cq, sq = cq_ref[...], sq_ref[...]
    for hh in range(MLA_HEADS):
        sl = slice(128 * hh, 128 * (hh + 1))
        q_ref[:, sl] = (qa[:, sl] * cq + qb[:, sl] * sq).astype(BF16)
    nkv = _rms(_dot(u, win_ref[:, 256:384]), gkv_ref[...]).astype(BF16)
    kn = _dot(nkv, wk_ref[...])
    v_ref[...] = _values_t(_dot_nt(wv_ref[...], nkv)).astype(BF16)
    rest = _dot(u, win_ref[:, 896:1408])
    kr = rest[:, 128:256] * ck_ref[...] + rest[:, 256:384] * sk_ref[...]
    for hh in range(MLA_HEADS):
        sl = slice(128 * hh, 128 * (hh + 1))
        k_ref[:, sl] = (kn[:, sl] + kr).astype(BF16)
    qs = _spread_heads(_dot(u, win_ref[:, 384:896]), lambda hq: HD * (hq // (SWA_HEADS // SWA_KV)))
    for hq in range(SWA_HEADS):
        qs_ref[:, 128 * hq:128 * (hq + 1)] = qs[hq].astype(BF16)
    ks_ref[...] = rest[:, 0:128].astype(BF16)
    vst_ref[...] = rest[:, 384:512].T.astype(BF16)


def _even_weights_kernel(wt_ref, wqb_in_ref, wkvb_ref, win_ref, wqa_ref, wqb_ref, wk_ref, wv_ref):
    o_kr = MLA_QR + MLA_KVR
    o_qs = o_kr + MLA_ROPE
    o_ks = o_qs + SWA_HEADS * HD
    o_vs = o_ks + SWA_KV * HD
    half = MLA_ROPE // 2
    wt = wt_ref[...]
    zero_rows = lambda n: jnp.zeros((n, D), F32)
    zero = lambda a, n: jnp.zeros((a.shape[0], n), F32)
    win_ref[...] = jnp.concatenate([
        wt[0:o_kr], wt[o_qs:o_ks] * (HD ** -0.5 * LOG2E), wt[o_ks:o_vs],
        zero_rows(64), wt[o_kr:o_qs], zero_rows(32),
        zero_rows(64), wt[o_kr + half:o_qs], wt[o_kr:o_kr + half], zero_rows(32),
        wt[o_vs:o_vs + SWA_KV * HD]], axis=0).T.astype(BF16)
    wq, wkv = wqb_in_ref[...], wkvb_ref[...]
    hq, hkv = MLA_NOPE + MLA_ROPE, MLA_NOPE + MLA_V
    qa, qb, kn, vt = [], [], [], []
    for h in range(MLA_HEADS):
        qa += [wq[:, hq * h:hq * (h + 1)], zero(wq, 32)]
        qb += [zero(wq, MLA_NOPE), wq[:, hq * h + MLA_NOPE + half:hq * (h + 1)],
               wq[:, hq * h + MLA_NOPE:hq * h + MLA_NOPE + half], zero(wq, 32)]
        kn += [wkv[:, hkv * h:hkv * h + MLA_NOPE], zero(wkv, MLA_V)]
        vt += [wkv[:, hkv * h + MLA_NOPE:hkv * (h + 1)]]
    wqa_ref[...] = jnp.concatenate(qa, axis=1).astype(BF16)
    wqb_ref[...] = jnp.concatenate(qb, axis=1).astype(BF16)
    wk_ref[...] = jnp.concatenate(kn, axis=1).astype(BF16)
    wv_ref[...] = jnp.concatenate(vt, axis=1).T.astype(BF16)


def _even_weights(w_in, w_qb, w_kvb):
    shapes =[(D, 1408), (MLA_QR, 128 * MLA_HEADS), (MLA_QR, 128 * MLA_HEADS), (MLA_KVR, 128 * MLA_HEADS),
              (MLA_V * MLA_HEADS, MLA_KVR)]
    return pl.pallas_call(
        _even_weights_kernel,
        out_shape=[jax.ShapeDtypeStruct(s, BF16) for s in shapes],
        compiler_params=_cparams(None),
        name="even_weights",
    )(w_in.T, w_qb, w_kvb)


def _rope_tables(T):
    f32 = np.float32
    freqs = f32(ROPE_THETA) ** (-np.arange(0, MLA_ROPE, 2, dtype=f32) / f32(MLA_ROPE))
    ang = np.arange(T, dtype=f32)[:, None] * freqs[None, :]
    cos, sin = np.cos(ang), np.sin(ang)
    z = lambda n: np.zeros((T, n), f32)
    scale = f32((MLA_NOPE + MLA_ROPE) ** -0.5 * LOG2E)
    ck = np.concatenate([z(64), cos, cos, z(32)], axis=1)
    sk = np.concatenate([z(64), -sin, sin, z(32)], axis=1)
    cq = np.concatenate([np.ones((T, 64), f32), cos, cos, z(32)], axis=1) * scale
    sq = sk * scale
    return tuple(jnp.asarray(t, F32) for t in (cq, sq, ck, sk))


def _proj_even(h, g, win, gq, gkv, wqa, wqb, wk, wv, tables, T):
    N = h.shape[0]
    nt = T // TM
    full = lambda a: pl.BlockSpec(a.shape, lambda i: (0,) * a.ndim)
    tab = pl.BlockSpec((TM, 128), lambda i: (i % nt, 0))
    row = lambda w: pl.BlockSpec((TM, w), lambda i: (i, 0))
    outs = (1024, 1024, -8 * VT_ROWS, 1024, 128, -128)
    spec = lambda w: row(w) if w > 0 else pl.BlockSpec((None, -w, TM), lambda i: (i // nt, 0, i % nt))
    shape = lambda w: jax.ShapeDtypeStruct((N, w) if w > 0 else (N // T, -w, T), BF16)
    return pl.pallas_call(
        _proj_even_kernel,
        grid=(N // TM,),
        in_specs=[row(D), full(g), full(win), full(gq), full(gkv), full(wqa), full(wqb), full(wk), full(wv),
                  tab, tab, tab, tab],
        out_specs=[spec(w) for w in outs],
        out_shape=[shape(w) for w in outs],
        compiler_params=_cparams(("arbitrary",)),
        name="proj_even",
    )(h, g, win, gq, gkv, wqa, wqb, wk, wv, *tables)


def _finish_t(accs):
    ot = jnp.concatenate([acc[:HD] / acc[HD:HD + 1] for acc in accs], axis=0)
    return ot.T.astype(BF16)


def _causal_attention_pair_t(ia, nq, q_of, k_ref, vt_ref, o_ref, s_ref, smax_ref, m_ref, acc_ref, *, tq,
                             bias_of=None):
    nh = s_ref.shape[1]
    first_query = (ia * tq, (nq - 1 - ia) * tq)
    n_items = nq + 1
    assert n_items % 2 == 1
    key = lax.broadcasted_iota(jnp.int32, (tq, tq), 0)
    qry = lax.broadcasted_iota(jnp.int32, (tq, tq), 1)

    def item(p):
        if isinstance(p, int) and p < 2:
            return p, first_query[p], first_query[p]
        t = jnp.where(p - 2 >= ia, 1, 0)
        return t, jnp.where(t == 1, first_query[1], first_query[0]), (p - 2 - t * ia) * tq

    def scores_to(slot, p):
        t, q0, start = item(p)
        start = pl.multiple_of(start, tq)
        for j in range(nh):
            st = _dot_nt(k_ref[pl.ds(start, tq), 128 * j:128 * (j + 1)], q_of(t, q0, j))
            if bias_of is not None:
                st = st + bias_of(start, q0, j)
            if isinstance(p, int) and p < 2:
                st = jnp.where(key <= qry, st, NEG)
            s_ref[slot, j] = st
            smax_ref[slot, j] = jnp.max(st, axis=0, keepdims=True)

    def update_from(slot, p):
        t, _, start = item(p)
        start = pl.multiple_of(start, tq)
        m_new = [jnp.maximum(m_ref[t, j], smax_ref[slot, j]) for j in range(nh)]
        pts = [jnp.exp2(s_ref[slot, j] - m_new[j]).astype(BF16) for j in range(nh)]
        for j in range(nh):
            acc_ref[t, j] = (jnp.exp2(m_ref[t, j] - m_new[j]) * acc_ref[t, j]
                             + _dot(vt_ref[VT_ROWS * j:VT_ROWS * (j + 1), pl.ds(start, tq)], pts[j]))
            m_ref[t, j] = m_new[j]

    def pair(k):
        scores_to(1, k + 1)
        update_from(0, k)
        scores_to(0, k + 2)
        update_from(1, k + 1)

    def quad(c, _):
        pair(2 + 4 * c)
        pair(4 + 4 * c)
        return 0

    m_ref[...] = jnp.full(m_ref.shape, NEG, F32)
    acc_ref[...] = jnp.zeros(acc_ref.shape, F32)
    scores_to(0, 0)
    pair(0)
    later_pairs = (n_items - 1) // 2 - 1
    lax.fori_loop(0, later_pairs // 2, quad, 0)
    if later_pairs % 2:
        pair(n_items - 3)
    update_from(0, n_items - 1)
    for t in range(2):
        o_ref[pl.ds(pl.multiple_of(first_query[t], tq), tq), :] = _finish_t([acc_ref[t, j] for j in range(nh)])


def _attention_t_scratch(tq):
    return [pltpu.VMEM((2, 2, tq, tq), F32), pltpu.VMEM((2, 2, 1, tq), F32), pltpu.VMEM((2, 2, 1, tq), F32),
            pltpu.VMEM((2, 2, VT_ROWS, tq), F32)]


def _mla_kernel(q_ref, k_ref, vt_ref, o_ref, s_ref, smax_ref, m_ref, acc_ref, *, tq, nq):
    def q_of(t, q0, j):
        return q_ref[pl.ds(pl.multiple_of(q0, tq), tq), 128 * j:128 * (j + 1)]

    _causal_attention_pair_t(pl.program_id(2), nq, q_of, k_ref, vt_ref, o_ref, s_ref, smax_ref, m_ref, acc_ref,
                             tq=tq)


def _mla_attention(q, k, vt, B, T, tq=512):
    nq = T // tq
    return pl.pallas_call(
        functools.partial(_mla_kernel, tq=tq, nq=nq),
        grid=(B, MLA_HEADS // 2, nq // 2),
        in_specs=[pl.BlockSpec((T, 256), lambda b, p, i: (b, p)),
                  pl.BlockSpec((T, 256), lambda b, p, i: (b, p)),
                  pl.BlockSpec((None, 2 * VT_ROWS, T), lambda b, p, i: (b, p, 0))],
        out_specs=pl.BlockSpec((T, 128), lambda b, p, i: (b, p)),
        out_shape=jax.ShapeDtypeStruct((B * T, MLA_HEADS * MLA_V), BF16),
        scratch_shapes=_attention_t_scratch(tq),
        compiler_params=_cparams(("arbitrary", "arbitrary", "arbitrary")),
        name="mla_attention",
    )(q, k, vt)


def _swa_kernel(sink_ref, q_ref, k_ref, vt_ref, tb_ref, o_ref, *, tq):
    i = pl.program_id(1)
    G = SWA_HEADS // SWA_KV
    W = 2 * SWA_WIN
    head = lax.broadcasted_iota(jnp.int32, (1, G * 128), 1) >> 7
    sinks = []
    for c in range(SWA_KV):
        sink = jnp.zeros((1, G * 128), F32)
        for g in range(G):
            sink = jnp.where(head == g, sink_ref[G * c + g], sink)
        sinks.append(sink)
    scores, values = [], []
    for r in range(tq // 128):
        qstart = i * tq + 128 * r
        kstart = pl.multiple_of(jnp.maximum(qstart - SWA_WIN, 0), 128)
        off = pl.multiple_of(qstart - kstart, 128)
        kw = k_ref[pl.ds(kstart, W), :]
        for c in range(SWA_KV):
            qg = jnp.concatenate([q_ref[128 * r:128 * (r + 1), 128 * (G * c + g):128 * (G * c + g + 1)]
                                  for g in range(G)], axis=0)
            bias = jnp.concatenate([tb_ref[G * c + g, :, pl.ds(off, 128)] for g in range(G)], axis=1)
            scores.append(_dot_nt(kw, qg) + bias)
            values.append(vt_ref[HD * c:HD * (c + 1), pl.ds(kstart, W)])
    probs = []
    for n, s in enumerate(scores):
        sink = sinks[n % SWA_KV]
        m = jnp.maximum(jnp.max(s, axis=0, keepdims=True), sink)
        p = jnp.exp2(s - m)
        probs.append((p.astype(BF16), jnp.sum(p, axis=0, keepdims=True) + jnp.exp2(sink - m)))
    for r in range(tq // 128):
        outs = []
        for c in range(SWA_KV):
            p, denom = probs[SWA_KV * r + c]
            ot = _dot(values[SWA_KV * r + c], p) / denom
            outs += [ot[:, 128 * g:128 * (g + 1)] for g in range(G)]
        o_ref[128 * r:128 * (r + 1), :] = jnp.concatenate(outs, axis=0).T.astype(BF16)


def _swa_attention(q8, k, vt, sinks, tb, B, T, tq=512):
    nq = T // tq
    return pl.pallas_call(
        functools.partial(_swa_kernel, tq=tq),
        grid=(B, nq),
        in_specs=[pl.BlockSpec(memory_space=pltpu.SMEM),
                  pl.BlockSpec((tq, 1024), lambda b, i: (b * nq + i, 0)),
                  pl.BlockSpec((T, 128), lambda b, i: (b, 0)),
                  pl.BlockSpec((None, 128, T), lambda b, i: (b, 0, 0)),
                  pl.BlockSpec((8, 256, 256), lambda b, i: (0, 0, 0))],
        out_specs=pl.BlockSpec((tq, 512), lambda b, i: (b * nq + i, 0)),
        out_shape=jax.ShapeDtypeStruct((B * T, SWA_HEADS * HD), BF16),
        compiler_params=_cparams(("arbitrary", "arbitrary")),
        name="swa_attention",
    )(sinks * LOG2E, q8, k, vt, tb)


def _outproj_router_kernel(h_ref, oa_ref, ob_ref, wo_ref, g_ref, wr_ref, br_ref,
                           hn_ref, u_ref, ri_ref, rg_ref, cnt_ref, base_ref):
    i = pl.program_id(0)

    @pl.when(i == 0)
    def _():
        base_ref[...] = jnp.zeros_like(base_ref)

    hn = h_ref[...] + _dot(oa_ref[...], wo_ref[0:512, :]) + _dot(ob_ref[...], wo_ref[512:1024, :])
    hn_ref[...] = hn
    u = _rms(hn, g_ref[...])
    u_ref[...] = _pack_halves(u)
    u_hi = u.astype(BF16)
    u_lo = (u - u_hi.astype(F32)).astype(BF16)
    both = _dot_nt(wr_ref[...], u_hi)
    logits = both[:128] + both[128:] + _dot_nt(wr_ref[0:128, :], u_lo) + br_ref[...]
    tm = logits.shape[1]
    sub = lax.broadcasted_iota(jnp.int32, (8, tm), 0).astype(F32)
    big = 1e6
    isg = sub < N_GROUPS
    gl = jnp.where(isg, logits[N_EXP:N_EXP + 8], NEG)
    gmax = jnp.max(gl, axis=0, keepdims=True)
    gsel = jnp.min(jnp.where(gl == gmax, sub, big), axis=0, keepdims=True)
    g_gate = 1.0 / jnp.sum(jnp.where(isg, jnp.exp(gl - gmax), 0.0), axis=0, keepdims=True)
    el = jnp.zeros((EPG, tm), F32)
    for grp in range(N_GROUPS):
        el = jnp.where(gsel == grp, logits[EPG * grp:EPG * (grp + 1)], el)
    m1 = jnp.max(el, axis=0, keepdims=True)
    i1 = jnp.min(jnp.where(el == m1, sub, big), axis=0, keepdims=True)
    el2 = jnp.where(sub == i1, NEG, el)
    m2 = jnp.max(el2, axis=0, keepdims=True)
    i2 = jnp.min(jnp.where(el2 == m2, sub, big), axis=0, keepdims=True)
    e1 = gsel * EPG + i1
    e2 = gsel * EPG + i2
    r = jnp.exp(m2 - m1)
    gate1 = g_gate / (1.0 + r)
    gate2 = g_gate * r / (1.0 + r)
    rows = lax.broadcasted_iota(jnp.int32, (128, tm), 0).astype(F32)
    oh1 = (rows == e1).astype(F32)
    oh2 = (rows == e2).astype(F32)
    oh = oh1 + oh2
    t_row = lax.broadcasted_iota(jnp.int32, (tm, tm), 0)
    t_col = lax.broadcasted_iota(jnp.int32, (tm, tm), 1)
    earlier = jnp.where(t_row < t_col, 1.0, 0.0).astype(BF16)
    prefix = _dot(oh.astype(BF16), earlier) + base_ref[:, 0:1]
    rank1 = jnp.sum(oh1 * prefix, axis=0, keepdims=True)
    rank2 = jnp.sum(oh2 * prefix, axis=0, keepdims=True)
    base_ref[...] = base_ref[...] + jnp.sum(oh, axis=1, keepdims=True)
    cnt_ref[...] = base_ref[...]
    ri_ref[...] = jnp.where(sub == 0, e1, jnp.where(sub == 1, e2, jnp.where(sub == 2, rank1,
                            jnp.where(sub == 3, rank2, 0.0))))
    rg_ref[...] = jnp.where(rows == 0, gate1, jnp.where(rows == 1, gate2, 0.0)).T


def _outproj_router(h, oa, ob, wo, g, wr, br):
    N = h.shape[0]
    full = lambda a: pl.BlockSpec(a.shape, lambda i: (0,) * a.ndim)
    row = lambda w: pl.BlockSpec((TM, w), lambda i: (i, 0))
    return pl.pallas_call(
        _outproj_router_kernel,
        grid=(N // TM,),
        in_specs=[row(D), row(512), row(512), full(wo), full(g), full(wr), full(br)],
        out_specs=[row(D), row(HALF), pl.BlockSpec((8, TM), lambda i: (0, i)), row(128),
                   pl.BlockSpec((128, 128), lambda i: (0, 0))],
        out_shape=[jax.ShapeDtypeStruct((N, D), F32), jax.ShapeDtypeStruct((N, HALF), jnp.int32),
                   jax.ShapeDtypeStruct((8, N), F32), jax.ShapeDtypeStruct((N, 128), F32),
                   jax.ShapeDtypeStruct((128, 128), F32)],
        scratch_shapes=[pltpu.VMEM((128, 128), F32)],
        compiler_params=_cparams(("arbitrary",)),
        name="outproj_router",
    )(h, oa, ob, wo, g, wr, br)


X_BUFS = 3


def _moe_kernel(te_ref, tv_ref, run_ref, nx_ref, tb_ref, x_hbm, wg_hbm, wu_hbm, wd_hbm, o_ref,
                wg_buf, wu_buf, wd_buf, sems, wgu_s, wd_s, x_buf, x_sems, *, layer):
    i = pl.program_id(0)
    valid = tv_ref[i]

    def weight_copies(e, slot):
        return (pltpu.make_async_copy(wg_hbm.at[layer, e], wg_buf.at[slot], sems.at[slot, 0]),
                pltpu.make_async_copy(wu_hbm.at[layer, e], wu_buf.at[slot], sems.at[slot, 1]),
                pltpu.make_async_copy(wd_hbm.at[layer, e], wd_buf.at[slot], sems.at[slot, 2]))

    @pl.when(run_ref[i] > 0)
    def _():
        slot = run_ref[i] - 1

        @pl.when(i == 0)
        def _():
            for c in weight_copies(te_ref[i], slot):
                c.start()

        for c in weight_copies(te_ref[i], slot):
            c.wait()
        wgu_s[:, :D_EXP] = wg_buf[slot].astype(BF16)
        wgu_s[:, D_EXP:] = wu_buf[slot].astype(BF16)
        wd_s[...] = wd_buf[slot].astype(BF16)

        @pl.when(nx_ref[i] >= 0)
        def _():
            for c in weight_copies(nx_ref[i], 1 - slot):
                c.start()

    n_tiles = pl.num_programs(0)

    def x_copy(j, slot):
        return pltpu.make_async_copy(x_hbm.at[pl.ds(pl.multiple_of(j * TM_E, TM_E), TM_E)], x_buf.at[slot],
                                     x_sems.at[slot])

    def start_tile(j):
        @pl.when((j < n_tiles) & (tv_ref[jnp.minimum(j, n_tiles - 1)] > 0))
        def _():
            x_copy(j, j % X_BUFS).start()

    @pl.when(i == 0)
    def _():
        for j in range(X_BUFS - 1):
            start_tile(i + j)

    start_tile(i + X_BUFS - 1)

    @pl.when(valid > 0)
    def _():
        x_copy(i, i % X_BUFS).wait()
        x_ref = x_buf.at[i % X_BUFS]
        rows = lax.broadcasted_iota(jnp.int32, x_ref.shape, 0)
        lo, hi = _unpack_halves(jnp.where(rows < valid, x_ref[...], 0))
        hgu = _dot(lo.astype(BF16), wgu_s[:HALF, :]) + _dot(hi.astype(BF16), wgu_s[HALF:, :])
        hg, hu = hgu[:, :D_EXP], hgu[:, D_EXP:]
        a = hg * (1.0 / (1.0 + jnp.exp(-hg))) * hu
        o_ref[...] = _pack_halves(_dot(a.astype(BF16), wd_s[...]))


def _moe_experts(xs, plan, w_gate, w_up, w_down, layer):
    P = xs.shape[0]
    hbm = pl.BlockSpec(memory_space=pl.ANY)
    grid_spec = pltpu.PrefetchScalarGridSpec(
        num_scalar_prefetch=5,
        grid=(P // TM_E,),
        in_specs=[hbm, hbm, hbm, hbm],
        out_specs=pl.BlockSpec((TM_E, HALF), lambda i, *plan: (plan[4][i], 0)),
        scratch_shapes=[pltpu.VMEM((2, D, D_EXP), F32), pltpu.VMEM((2, D, D_EXP), F32),
                        pltpu.VMEM((2, D_EXP, D), F32), pltpu.SemaphoreType.DMA((2, 3)),
                        pltpu.VMEM((D, 2 * D_EXP), BF16), pltpu.VMEM((D_EXP, D), BF16),
                        pltpu.VMEM((X_BUFS, TM_E, HALF), jnp.int32), pltpu.SemaphoreType.DMA((X_BUFS,))],
    )
    return pl.pallas_call(
        functools.partial(_moe_kernel, layer=layer),
        grid_spec=grid_spec,
        out_shape=jax.ShapeDtypeStruct((P, HALF), jnp.int32),
        compiler_params=_cparams(("arbitrary",)),
        name="moe_experts",
    )(*plan, xs, w_gate, w_up, w_down)


def _sc_mesh():
    return plsc.VectorSubcoreMesh(core_axis_name="c", subcore_axis_name="s",
                                  num_cores=SC_CORES, num_subcores=SC_SUBCORES)


def _sc_scatter_rows(src, d1, d2, P):
    N, W = src.shape
    per_w = N // (SC_CORES * SC_SUBCORES)

    @functools.partial(pl.kernel, mesh=_sc_mesh(), out_type=jax.ShapeDtypeStruct((P, W), src.dtype),
                       scratch_types=[pltpu.VMEM((SC_CHUNK,), jnp.int32), pltpu.VMEM((SC_CHUNK, W), src.dtype)],
                       name="sc_scatter_rows")
    def k(src_hbm, d1_hbm, d2_hbm, out_hbm, idx_v, rows_v):
        wid = lax.axis_index("s") * SC_CORES + lax.axis_index("c")

        @pl.loop(0, per_w // SC_CHUNK)
        def _(c):
            off = wid * per_w + c * SC_CHUNK
            pltpu.sync_copy(src_hbm.at[pl.ds(off, SC_CHUNK)], rows_v)
            pltpu.sync_copy(d1_hbm.at[pl.ds(off, SC_CHUNK)], idx_v)
            pltpu.sync_copy(rows_v, out_hbm.at[idx_v])
            pltpu.sync_copy(d2_hbm.at[pl.ds(off, SC_CHUNK)], idx_v)
            pltpu.sync_copy(rows_v, out_hbm.at[idx_v])

    return k(src, d1, d2)


def _sc_gather_rows(table, idx):
    B, W = idx.shape[0], table.shape[1]
    per_w = B // (SC_CORES * SC_SUBCORES)

    @functools.partial(pl.kernel, mesh=_sc_mesh(), out_type=jax.ShapeDtypeStruct((B, W), table.dtype),
                       scratch_types=[pltpu.VMEM((SC_CHUNK,), jnp.int32), pltpu.VMEM((SC_CHUNK, W), table.dtype)],
                       name="sc_gather_rows")
    def k(table_hbm, idx_hbm, out_hbm, idx_v, rows_v):
        wid = lax.axis_index("s") * SC_CORES + lax.axis_index("c")

        @pl.loop(0, per_w // SC_CHUNK)
        def _(c):
            off = wid * per_w + c * SC_CHUNK
            pltpu.sync_copy(idx_hbm.at[pl.ds(off, SC_CHUNK)], idx_v)
            pltpu.sync_copy(table_hbm.at[idx_v], rows_v)
            pltpu.sync_copy(rows_v, out_hbm.at[pl.ds(off, SC_CHUNK)])

    return k(table, idx)


def _dest_kernel(offs_ref, ri_ref, d_ref):
    ri = ri_ref[...]
    base = jnp.zeros(ri.shape, F32)
    for e in range(N_EXP):
        base = jnp.where(ri == e, offs_ref[e].astype(F32), base)
    d_ref[...] = (base + pltpu.roll(ri, 6, 0)).astype(jnp.int32)


def _dispatch_plan(ri, cnt, N):
    counts = cnt[:N_EXP, 0].astype(jnp.int32)
    padded = ((counts + TM_E - 1) // TM_E) * TM_E
    ends = jnp.cumsum(padded)
    offs = ends - padded
    take = lambda t, i: t.at[i].get(mode="promise_in_bounds")
    dest = pl.pallas_call(
        _dest_kernel,
        in_specs=[pl.BlockSpec(memory_space=pltpu.SMEM), pl.BlockSpec((8, N), lambda: (0, 0))],
        out_specs=pl.BlockSpec((8, N), lambda: (0, 0)),
        out_shape=jax.ShapeDtypeStruct((8, N), jnp.int32),
        name="dispatch_rows",
    )(offs, ri)
    d1, d2 = dest[0], dest[1]
    P = 2 * N + N_EXP * TM_E
    tile_start = jnp.arange(P // TM_E, dtype=jnp.int32) * TM_E
    tile_expert = jnp.minimum(jnp.sum((ends[None, :] <= tile_start[:, None]).astype(jnp.int32), axis=1), N_EXP - 1)
    tile_valid = jnp.clip(take(offs + counts, tile_expert) - tile_start, 0, TM_E)
    n_tiles = P // TM_E
    tile = jnp.arange(n_tiles, dtype=jnp.int32)
    starts = (tile_valid > 0) & ((tile == 0) | (tile_expert != jnp.roll(tile_expert, 1)))
    slot = (jnp.cumsum(starts.astype(jnp.int32)) - 1) % 2
    run_code = jnp.where(starts, slot + 1, 0).astype(jnp.int32)
    start_at = jnp.where(starts, tile, n_tiles)
    next_start = jnp.concatenate([lax.cummin(start_at, reverse=True)[1:], jnp.full((1,), n_tiles, jnp.int32)])
    next_expert = jnp.where(next_start < n_tiles, take(tile_expert, jnp.minimum(next_start, n_tiles - 1)), -1)
    tile_block = jnp.minimum(tile, ends[N_EXP - 1] // TM_E - 1)
    return d1, d2, P, (tile_expert, tile_valid, run_code, next_expert.astype(jnp.int32), tile_block)


def _combine(h_ref, y1_ref, y2_ref, rg_ref):
    rg = rg_ref[...]
    lo1, hi1 = _unpack_halves(y1_ref[...])
    lo2, hi2 = _unpack_halves(y2_ref[...])
    g1, g2 = rg[:, 0:1], rg[:, 1:2]
    return h_ref[...] + jnp.concatenate([g1 * lo1 + g2 * lo2, g1 * hi1 + g2 * hi2], axis=1)


ODD_WIDTHS = (1024, 1024, 512, 512)


def _combine_proj_odd_kernel(h_ref, y1_ref, y2_ref, rg_ref, g_ref, win_ref, wvt_ref, ind_ref, hn_ref, vt_ref,
                             vdt_ref, *out_refs):
    hn = _combine(h_ref, y1_ref, y2_ref, rg_ref)
    hn_ref[...] = hn
    u = _rms(hn, g_ref[...]).astype(BF16)
    vt_ref[...] = _values_t(_dot_nt(wvt_ref[0:512, :], u)).astype(BF16)
    vdt_ref[...] = _dot_nt(wvt_ref[512:1024, :], u).astype(BF16)
    qc_ref, kc_ref, qd_ref, kd_ref = out_refs
    qc = _spread_heads(_dot(u, win_ref[:, 0:512]), lambda h: 0)
    kc = _spread_heads(_dot(u, win_ref[:, 512:1024]), lambda h: 0)
    for hh in range(8):
        qc_ref[:, 128 * hh:128 * (hh + 1)] = qc[hh].astype(BF16)
        kc_ref[:, 128 * hh:128 * (hh + 1)] = (kc[hh] + ind_ref[...]).astype(BF16)
    qd_ref[...] = _dot(u, win_ref[:, 1024:1536]).astype(BF16)
    kd_ref[...] = _dot(u, win_ref[:, 1536:2048]).astype(BF16)


def _odd_weights_kernel(w_ref, win_ref, wvt_ref):
    win_ref[:, 0:512] = (w_ref[:, 0:512] * (HD ** -0.5 * LOG2E)).astype(BF16)
    win_ref[:, 512:1024] = w_ref[:, 512:1024].astype(BF16)
    wvt_ref[...] = w_ref[:, 1024:1536].T.astype(BF16)


def _odd_weights(w_in):
    return pl.pallas_call(
        _odd_weights_kernel,
        grid=(2,),
        in_specs=[pl.BlockSpec((D, 1536), lambda i: (0, i))],
        out_specs=[pl.BlockSpec((D, 1024), lambda i: (0, i)), pl.BlockSpec((512, D), lambda i: (i, 0))],
        out_shape=[jax.ShapeDtypeStruct((D, 2048), BF16), jax.ShapeDtypeStruct((1024, D), BF16)],
        compiler_params=_cparams(("arbitrary",)),
        name="odd_weights",
    )(w_in)


def _block_indicator(T):
    assert T // MOBA_L <= 64
    ind = np.zeros((T, 128), np.float32)
    ind[np.arange(T), 64 + np.arange(T) // MOBA_L] = 1.0
    return jnp.asarray(ind)


def _y_specs(N):
    return [pl.BlockSpec((TM, HALF), lambda i: (i, 0)), pl.BlockSpec((TM, HALF), lambda i: (i + N // TM, 0))]


def _combine_proj_odd(h, y12, rg, g, win, wvt, ind, T):
    N = h.shape[0]
    nt = T // TM
    full = lambda a: pl.BlockSpec(a.shape, lambda i: (0,) * a.ndim)
    row = lambda w: pl.BlockSpec((TM, w), lambda i: (i, 0))
    return pl.pallas_call(
        _combine_proj_odd_kernel,
        grid=(N // TM,),
        in_specs=[row(D)] + _y_specs(N) + [row(128), full(g), full(win), full(wvt),
                  pl.BlockSpec((TM, 128), lambda i: (i % nt, 0))],
        out_specs=[row(D), pl.BlockSpec((None, 8 * VT_ROWS, TM), lambda i: (i // nt, 0, i % nt)),
                   pl.BlockSpec((None, 512, TM), lambda i: (i // nt, 0, i % nt))]
        + [row(w) for w in ODD_WIDTHS],
        out_shape=[jax.ShapeDtypeStruct((N, D), F32), jax.ShapeDtypeStruct((N // T, 8 * VT_ROWS, T), BF16),
                   jax.ShapeDtypeStruct((N // T, 512, T), BF16)]
        + [jax.ShapeDtypeStruct((N, w), BF16) for w in ODD_WIDTHS],
        compiler_params=_cparams(("arbitrary",)),
        name="combine_proj_odd",
    )(h, y12, y12, rg, g, win, wvt, ind)


def _combine_final_kernel(h_ref, y1_ref, y2_ref, rg_ref, g_ref, o_ref):
    o_ref[...] = _rms(_combine(h_ref, y1_ref, y2_ref, rg_ref), g_ref[...])


def _combine_final(h, y12, rg, g):
    N = h.shape[0]
    row = lambda w: pl.BlockSpec((TM, w), lambda i: (i, 0))
    return pl.pallas_call(
        _combine_final_kernel,
        grid=(N // TM,),
        in_specs=[row(D)] + _y_specs(N) + [row(128), pl.BlockSpec((1, D), lambda i: (0, 0))],
        out_specs=row(D),
        out_shape=jax.ShapeDtypeStruct((N, D), F32),
        compiler_params=_cparams(("arbitrary",)),
        name="combine_final",
    )(h, y12, y12, rg, g)


def _moba_kernel(q_ref, k_ref, vt_ref, tb_ref, o_ref, kmean_ref, ksplit_ref, qa_ref, s_ref, smax_ref, m_ref, acc_ref,
                 *, nkb, nq):
    ia = pl.program_id(2)
    L, tq = MOBA_L, MOBA_TQ

    @pl.when(ia == 0)
    def _():
        kmean_ref[...] = jnp.zeros_like(kmean_ref)
        for n in range(nkb):
            kmean_ref[64 + n:65 + n, :] = jnp.sum(k_ref[n * L:(n + 1) * L, :].astype(F32), axis=0,
                                                  keepdims=True) * (1.0 / L)
        hi = kmean_ref[...].astype(BF16)
        ksplit_ref[0] = hi
        ksplit_ref[1] = (kmean_ref[...] - hi.astype(F32)).astype(BF16)

    nb = -(-nkb // 8) * 8
    lane = lax.broadcasted_iota(jnp.int32, (tq, 128), 1)
    blk = lax.broadcasted_iota(jnp.int32, (nb, tq), 0)
    half = jnp.where(lax.broadcasted_iota(jnp.int32, (nb, tq), 1) >= L, 1, 0)
    blk_f = blk.astype(F32)
    for t, tile in enumerate((ia, nq - 1 - ia)):
        own = 2 * tile + half
        for j in range(2):
            qj = q_ref[pl.ds(pl.multiple_of(tile * tq, tq), tq), 128 * j:128 * (j + 1)]
            gsc = (_dot_nt(ksplit_ref[0, 64:64 + nb, 128 * j:128 * (j + 1)], qj)
                   + _dot_nt(ksplit_ref[1, 64:64 + nb, 128 * j:128 * (j + 1)], qj))
            g = jnp.where(blk < own, gsc, NEG)
            allowed = jnp.where(blk == own, 1.0, 0.0)
            for _ in range(MOBA_TOPK):
                mx = jnp.max(g, axis=0, keepdims=True)
                idx = jnp.min(jnp.where(g == mx, blk_f, 1e6), axis=0, keepdims=True)
                pick = blk_f == idx
                allowed = jnp.maximum(allowed, jnp.where(pick, jnp.where(mx > 0.5 * NEG, 1.0, 0.0), 0.0))
                g = jnp.where(pick, 2.0 * NEG, g)
            mask = jnp.concatenate([jnp.zeros((64, tq), F32), jnp.where(allowed > 0.5, 0.0, NEG),
                                    jnp.zeros((64 - nb, tq), F32)], axis=0)
            qa_ref[t, j] = jnp.where(lane < HD, qj, mask.T.astype(BF16))

    def bias_of(start, q0, j):
        parts = []
        for c in range(tq // L):
            off = jnp.minimum(q0 - start - c * L, BIAS_CONST_FROM) + BIAS_PAD
            parts.append(tb_ref[j, :, pl.ds(pl.multiple_of(off, L), tq)])
        return jnp.concatenate(parts, axis=0)

    _causal_attention_pair_t(ia, nq, lambda t, q0, j: qa_ref[t, j], k_ref, vt_ref, o_ref, s_ref, smax_ref, m_ref,
                             acc_ref, tq=tq, bias_of=bias_of)


def _moba_attention(q, k, vt, tb, B, T):
    tq = MOBA_TQ
    nq = T // tq
    return pl.pallas_call(
        functools.partial(_moba_kernel, nkb=T // MOBA_L, nq=nq),
        grid=(B, 4, nq // 2),
        in_specs=[pl.BlockSpec((T, 256), lambda b, p, i: (b, p)),
                  pl.BlockSpec((T, 256), lambda b, p, i: (b, p)),
                  pl.BlockSpec((None, 2 * VT_ROWS, T), lambda b, p, i: (b, p, 0)),
                  pl.BlockSpec((2, MOBA_L, BIAS_ROWS), lambda b, p, i: (p, 0, 0))],
        out_specs=pl.BlockSpec((T, 128), lambda b, p, i: (b, p)),
        out_shape=jax.ShapeDtypeStruct((B * T, 512), BF16),
        scratch_shapes=[pltpu.VMEM((128, 256), F32), pltpu.VMEM((2, 128, 256), BF16),
                        pltpu.VMEM((2, 2, tq, 128), BF16)]
        + _attention_t_scratch(tq),
        compiler_params=_cparams(("arbitrary", "arbitrary", "arbitrary")),
        name="moba_attention",
    )(q, k, vt, tb)


def _sb_kernel(q_ref, k_ref, vt_ref, o_ref, *, tq, pairs):
    i = pl.program_id(2)
    lane = lax.broadcasted_iota(jnp.int32, (tq, 128), 1)
    qs = []
    for p in range(pairs):
        q_pair = q_ref[:, 128 * p:128 * (p + 1)]
        qs += [jnp.where(lane < HD, q_pair, 0).astype(BF16), jnp.where(lane >= HD, q_pair, 0).astype(BF16)]
    key = lax.broadcasted_iota(jnp.int32, (tq, tq), 0)
    qry = lax.broadcasted_iota(jnp.int32, (tq, tq), 1)
    past = key < qry
    suffix = jnp.where(qry >= key, -1.0, 0.0).astype(BF16)
    heads = range(2 * pairs)

    def step(kb, carry, boundary):
        start = pl.multiple_of(kb * tq, tq)
        zs = [_dot_nt(k_ref[pl.ds(start, tq), 128 * (h // 2):128 * (h // 2 + 1)], qs[h]) for h in heads]
        parts = []
        for z in zs:
            neg_abs = pltpu.bitcast(pltpu.bitcast(z, jnp.uint32) | jnp.uint32(0x80000000), F32)
            soft = jnp.maximum(z, 0.0) + jnp.log2(1.0 + jnp.exp2(neg_abs))
            if boundary:
                soft = jnp.where(past, soft, 0.0)
            parts.append(soft.astype(BF16))
        sums = [_dot(suffix, part) for part in parts]
        ws = []
        for h in heads:
            w = jnp.exp2(zs[h] + (carry[h][0] + sums[h]))
            ws.append((jnp.where(past, w, 0.0) if boundary else w).astype(BF16))
        return tuple((carry[h][0] + sums[h][0:1], carry[h][1]
                      + _dot(vt_ref[128 * (h // 2):128 * (h // 2 + 1), pl.ds(start, tq)], ws[h])) for h in heads)

    def live(carry):
        top = carry[0][0]
        for c, _ in carry[1:]:
            top = jnp.maximum(top, c)
        return (jnp.max(top) > SB_DONE).astype(jnp.int32)

    def body(state):
        t, _, carry = state
        carry = step(i - 1 - t, carry, False)
        return t + 1, live(carry), carry

    init = tuple((jnp.zeros((1, tq), F32), jnp.zeros((128, tq), F32)) for _ in range(2 * pairs))
    carry = step(i, init, True)
    _, _, carry = lax.while_loop(lambda st: (st[0] < i) & (st[1] > 0), body, (jnp.int32(0), live(carry), carry))
    rows = lax.broadcasted_iota(jnp.int32, (128, tq), 0)
    for p in range(pairs):
        ot = jnp.where(rows < HD, carry[2 * p][1], carry[2 * p + 1][1])
        o_ref[:, 128 * p:128 * (p + 1)] = ot.T.astype(BF16)


def _sb_attention(q, k, vt, B, T, tq=256, pairs=4):
    nq = T // tq
    w = 128 * pairs
    return pl.pallas_call(
        functools.partial(_sb_kernel, tq=tq, pairs=pairs),
        grid=(B, 4 // pairs, nq),
        in_specs=[pl.BlockSpec((tq, w), lambda b, p, i: (b * nq + i, p)),
                  pl.BlockSpec((T, w), lambda b, p, i: (b, p)),
                  pl.BlockSpec((None, w, T), lambda b, p, i: (b, p, 0))],
        out_specs=pl.BlockSpec((tq, w), lambda b, p, i: (b * nq + i, p)),
        out_shape=jax.ShapeDtypeStruct((B * T, 512), BF16),
        compiler_params=_cparams(("arbitrary", "arbitrary", "arbitrary")),
        name="sb_attention",
    )(q, k, vt)


def _router_weights(w_group, b_group, w_router, b_router):
    pad = 128 - N_EXP - N_GROUPS
    wr = jnp.concatenate([w_router, w_group, jnp.zeros((D, pad), F32)], axis=1).T
    hi = wr.astype(BF16)
    lo = (wr - hi.astype(F32)).astype(BF16)
    br = jnp.concatenate([b_router, b_group, jnp.zeros((pad,), F32)])[:, None]
    return jnp.concatenate([hi, lo], axis=0), br


def _ffn(h, oa, ob, wo, g, w_group, b_group, w_router, b_router, w_gate, w_up, w_down, layer):
    wr, br = _router_weights(w_group, b_group, w_router, b_router)
    hn, u, ri, rg, cnt = _outproj_router(h, oa, ob, wo.astype(BF16), g[None, :], wr, br)
    d1, d2, P, plan = _dispatch_plan(ri, cnt, h.shape[0])
    xs = _sc_scatter_rows(u, d1, d2, P)
    ys = _moe_experts(xs, plan, w_gate, w_up, w_down, layer)
    return hn, _sc_gather_rows(ys, jnp.concatenate([d1, d2])), rg


def kernel(x, norm_mix, norm_ffn, norm_final, rel_bias_table, w_in_even, g_mla_q, g_mla_kv, w_mla_qb, w_mla_kvb, swa_sinks, w_out_even, w_in_odd, w_out_odd, moe_w_group, moe_b_group, moe_w_router, moe_b_router, moe_w_gate, moe_w_up, moe_w_down):
    B, T, _ = x.shape
    h = x.reshape(B * T, D)
    tb_swa, tb_moba = _bias_tables(rel_bias_table)

    win, wqa, wqb, wk, wv = _even_weights(w_in_even[0], w_mla_qb[0], w_mla_kvb[0])
    q, k, v, qs8, ks, vst = _proj_even(h, norm_mix[0][None, :], win, g_mla_q[0][None, :], g_mla_kv[0][None, :],
                                       wqa, wqb, wk, wv, _rope_tables(T), T)
    oa = _mla_attention(q, k, v, B, T)
    ob = _swa_attention(qs8, ks, vst, swa_sinks[0], tb_swa, B, T)
    h, y12, rg = _ffn(h, oa, ob, w_out_even[0], norm_ffn[0], moe_w_group[0], moe_b_group[0], moe_w_router[0],
                      moe_b_router[0], moe_w_gate, moe_w_up, moe_w_down, 0)

    h, vct, vdt, qc, kc, qd, kd = _combine_proj_odd(h, y12, rg, norm_mix[1][None, :], *_odd_weights(w_in_odd[0]),
                                                   _block_indicator(T), T)
    oc = _moba_attention(qc, kc, vct, tb_moba, B, T)
    od = _sb_attention(qd, kd, vdt, B, T)
    h, y12, rg = _ffn(h, oc, od, w_out_odd[0], norm_ffn[1], moe_w_group[1], moe_b_group[1], moe_w_router[1],
                      moe_b_router[1], moe_w_gate, moe_w_up, moe_w_down, 1)
    out = _combine_final(h, y12, rg, norm_final[None, :])
    return out.reshape(B, T, D)
```

```python
import functools
import math

import numpy as np
import jax
import jax.numpy as jnp
from jax import lax
from jax.experimental import pallas as pl
from jax.experimental.pallas import tpu as pltpu
from jax.experimental.pallas import tpu_sc as plsc

F32 = jnp.float32
BF16 = jnp.bfloat16
NEG = -1e30
EPS = 1e-6

D = 1024
HD = 64
MLA_HEADS, MLA_QR, MLA_KVR, MLA_NOPE, MLA_ROPE, MLA_V = 8, 256, 128, 64, 32, 64
ROPE_THETA = 10000.0
SWA_HEADS, SWA_KV, SWA_WIN = 8, 2, 128
MOBA_L, MOBA_TOPK = 256, 3
REL_BUCKETS, REL_MAX = 32, 2048
N_GROUPS, EPG, N_EXP, D_EXP = 4, 8, 32, 256
BIAS_CONST_FROM = 1792
MOBA_TQ = 2 * MOBA_L
BIAS_PAD = MOBA_L
BIAS_ROWS = BIAS_PAD + BIAS_CONST_FROM + MOBA_TQ
SB_DONE = -160.0
TM = 1024
TM_E = 512
VMEM_LIMIT = 56 * 1024 * 1024
SC_CORES, SC_SUBCORES = 2, 16
SC_CHUNK = 128
HALF = D // 2
VT_ROWS = HD + 16
LOG2E = math.log2(math.e)


def _pack_halves(x):
    lo = pltpu.bitcast(x[:, :HALF].astype(BF16).astype(F32), jnp.uint32) >> 16
    hi = pltpu.bitcast(x[:, HALF:].astype(BF16).astype(F32), jnp.uint32) & jnp.uint32(0xFFFF0000)
    return pltpu.bitcast(lo | hi, jnp.int32)


def _unpack_halves(w):
    w = pltpu.bitcast(w, jnp.uint32)
    return pltpu.bitcast(w << 16, F32), pltpu.bitcast(w & jnp.uint32(0xFFFF0000), F32)


def _cparams(sem):
    return pltpu.CompilerParams(dimension_semantics=sem, vmem_limit_bytes=VMEM_LIMIT)


def _dot(a, b):
    return jnp.dot(a, b, preferred_element_type=F32)


def _dot_nt(a, b):
    return lax.dot_general(a, b, (((1,), (1,)), ((), ())), preferred_element_type=F32)


def _rms(x, g):
    return x * lax.rsqrt(jnp.mean(x * x, axis=-1, keepdims=True) + EPS) * g


def _values_t(vt):
    ones = jnp.ones((VT_ROWS - HD, vt.shape[1]), vt.dtype)
    return jnp.concatenate([blk for h in range(8) for blk in (vt[HD * h:HD * (h + 1)], ones)], axis=0)


def _spread_heads(x, lane_of):
    lane = lax.broadcasted_iota(jnp.int32, (x.shape[0], 128), 1)
    out = []
    for h in range(8):
        pair = x[:, 128 * (h // 2):128 * (h // 2 + 1)]
        if HD * (h % 2) != lane_of(h):
            pair = pltpu.roll(pair, HD, 1)
        out.append(jnp.where((lane >= lane_of(h)) & (lane < lane_of(h) + HD), pair, 0.0))
    return out


def _t5_bucket_np(dist):
    n = np.maximum(dist, 0)
    max_exact = REL_BUCKETS // 2
    nf = np.maximum(n, 1).astype(np.float32)
    large = max_exact + (np.log(nf / np.float32(max_exact)) / np.float32(math.log(REL_MAX / max_exact))
                         * np.float32(REL_BUCKETS - max_exact)).astype(np.int32)
    large = np.minimum(large, REL_BUCKETS - 1)
    return np.where(n < max_exact, n, large).astype(np.int32)


def _bias_kernel(table_ref, bucket_ref, out_ref, *, scale, ranges):
    b = bucket_ref[...]
    for m, (lo, hi) in enumerate(ranges):
        @pl.when(pl.program_id(0) == m)
        def _():
            for h in range(8):
                acc = jnp.full(b.shape, NEG if lo == REL_BUCKETS else table_ref[h, lo] * scale, F32)
                for k in range(lo + 1, min(hi, REL_BUCKETS - 1) + 1):
                    acc = jnp.where(b == k, table_ref[h, k] * scale, acc)
                if hi == REL_BUCKETS:
                    acc = jnp.where(b == REL_BUCKETS, NEG, acc)
                out_ref[h] = acc


def _bias_lookup(rel_table, dist, scale, name, masked=None):
    R, C = dist.shape
    buckets = _t5_bucket_np(dist)
    if masked is not None:
        buckets = np.where(masked, REL_BUCKETS, buckets).astype(np.int32)
    ranges = tuple((int(buckets[:, c:c + 256].min()), int(buckets[:, c:c + 256].max())) for c in range(0, C, 256))
    return pl.pallas_call(
        functools.partial(_bias_kernel, scale=scale, ranges=ranges),
        grid=(C // 256,),
        in_specs=[pl.BlockSpec(memory_space=pltpu.SMEM),
                  pl.BlockSpec((R, 256), lambda m: (0, m))],
        out_specs=pl.BlockSpec((8, R, 256), lambda m: (0, 0, m)),
        out_shape=jax.ShapeDtypeStruct((8, R, C), F32),
        compiler_params=_cparams(("arbitrary",)),
        name=name,
    )(rel_table.T, jnp.asarray(buckets))


def _bias_tables(rel_table):
    d = np.arange(256)[None, :] - np.arange(256)[:, None]
    swa = _bias_lookup(rel_table, d, LOG2E, "bias_swa", masked=(d < 0) | (d >= SWA_WIN))
    moba = _bias_lookup(rel_table, np.arange(BIAS_ROWS)[None, :] - BIAS_PAD - np.arange(MOBA_L)[:, None], LOG2E,
                        "bias_moba")
    return swa, moba


def _proj_even_kernel(h_ref, g_ref, win_ref, gq_ref, gkv_ref, wqa_ref, wqb_ref, wk_ref, wv_ref,
                      cq_ref, sq_ref, ck_ref, sk_ref,
                      q_ref, k_ref, v_ref, qs_ref, ks_ref, vst_ref):
    u = _rms(h_ref[...], g_ref[...]).astype(BF16)
    nq = _rms(_dot(u, win_ref[:, 0:256]), gq_ref[...]).astype(BF16)
    qa = _dot(nq, wqa_ref[...])
    qb = _dot(nq, wqb_ref[...])
    cq, sq = cq_ref[...], sq_ref[...]
    for hh in range(MLA_HEADS):
        sl = slice(128 * hh, 128 * (hh + 1))
        q_ref[:, sl] = (qa[:, sl] * cq + qb[:, sl] * sq).astype(BF16)
    nkv = _rms(_dot(u, win_ref[:, 256:384]), gkv_ref[...]).astype(BF16)
    kn = _dot(nkv, wk_ref[...])
    v_ref[...] = _values_t(_dot_nt(wv_ref[...], nkv)).astype(BF16)
    rest = _dot(u, win_ref[:, 896:1408])
    kr = rest[:, 128:256] * ck_ref[...] + rest[:, 256:384] * sk_ref[...]
    for hh in range(MLA_HEADS):
        sl = slice(128 * hh, 128 * (hh + 1))
        k_ref[:, sl] = (kn[:, sl] + kr).astype(BF16)
    qs = _spread_heads(_dot(u, win_ref[:, 384:896]), lambda hq: HD * (hq // (SWA_HEADS // SWA_KV)))
    for hq in range(SWA_HEADS):
        qs_ref[:, 128 * hq:128 * (hq + 1)] = qs[hq].astype(BF16)
    ks_ref[...] = rest[:, 0:128].astype(BF16)
    vst_ref[...] = rest[:, 384:512].T.astype(BF16)


def _even_weights_kernel(wt_ref, wqb_in_ref, wkvb_ref, win_ref, wqa_ref, wqb_ref, wk_ref, wv_ref):
    o_kr = MLA_QR + MLA_KVR
    o_qs = o_kr + MLA_ROPE
    o_ks = o_qs + SWA_HEADS * HD
    o_vs = o_ks + SWA_KV * HD
    half = MLA_ROPE // 2
    wt = wt_ref[...]
    zero_rows = lambda n: jnp.zeros((n, D), F32)
    zero = lambda a, n: jnp.zeros((a.shape[0], n), F32)
    win_ref[...] = jnp.concatenate([
        wt[0:o_kr], wt[o_qs:o_ks] * (HD ** -0.5 * LOG2E), wt[o_ks:o_vs],
        zero_rows(64), wt[o_kr:o_qs], zero_rows(32),
        zero_rows(64), wt[o_kr + half:o_qs], wt[o_kr:o_kr + half], zero_rows(32),
        wt[o_vs:o_vs + SWA_KV * HD]], axis=0).T.astype(BF16)
    wq, wkv = wqb_in_ref[...], wkvb_ref[...]
    hq, hkv = MLA_NOPE + MLA_ROPE, MLA_NOPE + MLA_V
    qa, qb, kn, vt = [], [], [], []
    for h in range(MLA_HEADS):
        qa += [wq[:, hq * h:hq * (h + 1)], zero(wq, 32)]
        qb += [zero(wq, MLA_NOPE), wq[:, hq * h + MLA_NOPE + half:hq * (h + 1)],
               wq[:, hq * h + MLA_NOPE:hq * h + MLA_NOPE + half], zero(wq, 32)]
        kn += [wkv[:, hkv * h:hkv * h + MLA_NOPE], zero(wkv, MLA_V)]
        vt += [wkv[:, hkv * h + MLA_NOPE:hkv * (h + 1)]]
    wqa_ref[...] = jnp.concatenate(qa, axis=1).astype(BF16)
    wqb_ref[...] = jnp.concatenate(qb, axis=1).astype(BF16)
    wk_ref[...] = jnp.concatenate(kn, axis=1).astype(BF16)
    wv_ref[...] = jnp.concatenate(vt, axis=1).T.astype(BF16)


def _even_weights(w_in, w_qb, w_kvb):
    shapes =[(D, 1408), (MLA_QR, 128 * MLA_HEADS), (MLA_QR, 128 * MLA_HEADS), (MLA_KVR, 128 * MLA_HEADS),
              (MLA_V * MLA_HEADS, MLA_KVR)]
    return pl.pallas_call(
        _even_weights_kernel,
        out_shape=[jax.ShapeDtypeStruct(s, BF16) for s in shapes],
        compiler_params=_cparams(None),
        name="even_weights",
    )(w_in.T, w_qb, w_kvb)


def _rope_tables(T):
    f32 = np.float32
    freqs = f32(ROPE_THETA) ** (-np.arange(0, MLA_ROPE, 2, dtype=f32) / f32(MLA_ROPE))
    ang = np.arange(T, dtype=f32)[:, None] * freqs[None, :]
    cos, sin = np.cos(ang), np.sin(ang)
    z = lambda n: np.zeros((T, n), f32)
    scale = f32((MLA_NOPE + MLA_ROPE) ** -0.5 * LOG2E)
    ck = np.concatenate([z(64), cos, cos, z(32)], axis=1)
    sk = np.concatenate([z(64), -sin, sin, z(32)], axis=1)
    cq = np.concatenate([np.ones((T, 64), f32), cos, cos, z(32)], axis=1) * scale
    sq = sk * scale
    return tuple(jnp.asarray(t, F32) for t in (cq, sq, ck, sk))


def _proj_even(h, g, win, gq, gkv, wqa, wqb, wk, wv, tables, T):
    N = h.shape[0]
    nt = T // TM
    full = lambda a: pl.BlockSpec(a.shape, lambda i: (0,) * a.ndim)
    tab = pl.BlockSpec((TM, 128), lambda i: (i % nt, 0))
    row = lambda w: pl.BlockSpec((TM, w), lambda i: (i, 0))
    outs = (1024, 1024, -8 * VT_ROWS, 1024, 128, -128)
    spec = lambda w: row(w) if w > 0 else pl.BlockSpec((None, -w, TM), lambda i: (i // nt, 0, i % nt))
    shape = lambda w: jax.ShapeDtypeStruct((N, w) if w > 0 else (N // T, -w, T), BF16)
    return pl.pallas_call(
        _proj_even_kernel,
        grid=(N // TM,),
        in_specs=[row(D), full(g), full(win), full(gq), full(gkv), full(wqa), full(wqb), full(wk), full(wv),
                  tab, tab, tab, tab],
        out_specs=[spec(w) for w in outs],
        out_shape=[shape(w) for w in outs],
        compiler_params=_cparams(("arbitrary",)),
        name="proj_even",
    )(h, g, win, gq, gkv, wqa, wqb, wk, wv, *tables)


def _finish_t(accs):
    ot = jnp.concatenate([acc[:HD] / acc[HD:HD + 1] for acc in accs], axis=0)
    return ot.T.astype(BF16)


def _causal_attention_pair_t(ia, nq, q_of, k_ref, vt_ref, o_ref, s_ref, smax_ref, m_ref, acc_ref, *, tq,
                             bias_of=None):
    nh = s_ref.shape[1]
    first_query = (ia * tq, (nq - 1 - ia) * tq)
    n_items = nq + 1
    assert n_items % 2 == 1
    key = lax.broadcasted_iota(jnp.int32, (tq, tq), 0)
    qry = lax.broadcasted_iota(jnp.int32, (tq, tq), 1)

    def item(p):
        if isinstance(p, int) and p < 2:
            return p, first_query[p], first_query[p]
        t = jnp.where(p - 2 >= ia, 1, 0)
        return t, jnp.where(t == 1, first_query[1], first_query[0]), (p - 2 - t * ia) * tq

    def scores_to(slot, p):
        t, q0, start = item(p)
        start = pl.multiple_of(start, tq)
        for j in range(nh):
            st = _dot_nt(k_ref[pl.ds(start, tq), 128 * j:128 * (j + 1)], q_of(t, q0, j))
            if bias_of is not None:
                st = st + bias_of(start, q0, j)
            if isinstance(p, int) and p < 2:
                st = jnp.where(key <= qry, st, NEG)
            s_ref[slot, j] = st
            smax_ref[slot, j] = jnp.max(st, axis=0, keepdims=True)

    def update_from(slot, p):
        t, _, start = item(p)
        start = pl.multiple_of(start, tq)
        m_new = [jnp.maximum(m_ref[t, j], smax_ref[slot, j]) for j in range(nh)]
        pts = [jnp.exp2(s_ref[slot, j] - m_new[j]).astype(BF16) for j in range(nh)]
        for j in range(nh):
            acc_ref[t, j] = (jnp.exp2(m_ref[t, j] - m_new[j]) * acc_ref[t, j]
                             + _dot(vt_ref[VT_ROWS * j:VT_ROWS * (j + 1), pl.ds(start, tq)], pts[j]))
            m_ref[t, j] = m_new[j]

    def pair(k):
        scores_to(1, k + 1)
        update_from(0, k)
        scores_to(0, k + 2)
        update_from(1, k + 1)

    def quad(c, _):
        pair(2 + 4 * c)
        pair(4 + 4 * c)
        return 0

    m_ref[...] = jnp.full(m_ref.shape, NEG, F32)
    acc_ref[...] = jnp.zeros(acc_ref.shape, F32)
    scores_to(0, 0)
    pair(0)
    later_pairs = (n_items - 1) // 2 - 1
    lax.fori_loop(0, later_pairs // 2, quad, 0)
    if later_pairs % 2:
        pair(n_items - 3)
    update_from(0, n_items - 1)
    for t in range(2):
        o_ref[pl.ds(pl.multiple_of(first_query[t], tq), tq), :] = _finish_t([acc_ref[t, j] for j in range(nh)])


def _attention_t_scratch(tq):
    return [pltpu.VMEM((2, 2, tq, tq), F32), pltpu.VMEM((2, 2, 1, tq), F32), pltpu.VMEM((2, 2, 1, tq), F32),
            pltpu.VMEM((2, 2, VT_ROWS, tq), F32)]


def _mla_kernel(q_ref, k_ref, vt_ref, o_ref, s_ref, smax_ref, m_ref, acc_ref, *, tq, nq):
    def q_of(t, q0, j):
        return q_ref[pl.ds(pl.multiple_of(q0, tq), tq), 128 * j:128 * (j + 1)]

    _causal_attention_pair_t(pl.program_id(2), nq, q_of, k_ref, vt_ref, o_ref, s_ref, smax_ref, m_ref, acc_ref,
                             tq=tq)


def _mla_attention(q, k, vt, B, T, tq=512):
    nq = T // tq
    return pl.pallas_call(
        functools.partial(_mla_kernel, tq=tq, nq=nq),
        grid=(B, MLA_HEADS // 2, nq // 2),
        in_specs=[pl.BlockSpec((T, 256), lambda b, p, i: (b, p)),
                  pl.BlockSpec((T, 256), lambda b, p, i: (b, p)),
                  pl.BlockSpec((None, 2 * VT_ROWS, T), lambda b, p, i: (b, p, 0))],
        out_specs=pl.BlockSpec((T, 128), lambda b, p, i: (b, p)),
        out_shape=jax.ShapeDtypeStruct((B * T, MLA_HEADS * MLA_V), BF16),
        scratch_shapes=_attention_t_scratch(tq),
        compiler_params=_cparams(("arbitrary", "arbitrary", "arbitrary")),
        name="mla_attention",
    )(q, k, vt)


def _swa_kernel(sink_ref, q_ref, k_ref, vt_ref, tb_ref, o_ref, *, tq):
    i = pl.program_id(1)
    G = SWA_HEADS // SWA_KV
    W = 2 * SWA_WIN
    head = lax.broadcasted_iota(jnp.int32, (1, G * 128), 1) >> 7
    sinks = []
    for c in range(SWA_KV):
        sink = jnp.zeros((1, G * 128), F32)
        for g in range(G):
            sink = jnp.where(head == g, sink_ref[G * c + g], sink)
        sinks.append(sink)
    scores, values = [], []
    for r in range(tq // 128):
        qstart = i * tq + 128 * r
        kstart = pl.multiple_of(jnp.maximum(qstart - SWA_WIN, 0), 128)
        off = pl.multiple_of(qstart - kstart, 128)
        kw = k_ref[pl.ds(kstart, W), :]
        for c in range(SWA_KV):
            qg = jnp.concatenate([q_ref[128 * r:128 * (r + 1), 128 * (G * c + g):128 * (G * c + g + 1)]
                                  for g in range(G)], axis=0)
            bias = jnp.concatenate([tb_ref[G * c + g, :, pl.ds(off, 128)] for g in range(G)], axis=1)
            scores.append(_dot_nt(kw, qg) + bias)
            values.append(vt_ref[HD * c:HD * (c + 1), pl.ds(kstart, W)])
    probs = []
    for n, s in enumerate(scores):
        sink = sinks[n % SWA_KV]
        m = jnp.maximum(jnp.max(s, axis=0, keepdims=True), sink)
        p = jnp.exp2(s - m)
        probs.append((p.astype(BF16), jnp.sum(p, axis=0, keepdims=True) + jnp.exp2(sink - m)))
    for r in range(tq // 128):
        outs = []
        for c in range(SWA_KV):
            p, denom = probs[SWA_KV * r + c]
            ot = _dot(values[SWA_KV * r + c], p) / denom
            outs += [ot[:, 128 * g:128 * (g + 1)] for g in range(G)]
        o_ref[128 * r:128 * (r + 1), :] = jnp.concatenate(outs, axis=0).T.astype(BF16)


def _swa_attention(q8, k, vt, sinks, tb, B, T, tq=512):
    nq = T // tq
    return pl.pallas_call(
        functools.partial(_swa_kernel, tq=tq),
        grid=(B, nq),
        in_specs=[pl.BlockSpec(memory_space=pltpu.SMEM),
                  pl.BlockSpec((tq, 1024), lambda b, i: (b * nq + i, 0)),
                  pl.BlockSpec((T, 128), lambda b, i: (b, 0)),
                  pl.BlockSpec((None, 128, T), lambda b, i: (b, 0, 0)),
                  pl.BlockSpec((8, 256, 256), lambda b, i: (0, 0, 0))],
        out_specs=pl.BlockSpec((tq, 512), lambda b, i: (b * nq + i, 0)),
        out_shape=jax.ShapeDtypeStruct((B * T, SWA_HEADS * HD), BF16),
        compiler_params=_cparams(("arbitrary", "arbitrary")),
        name="swa_attention",
    )(sinks * LOG2E, q8, k, vt, tb)


def _outproj_router_kernel(h_ref, oa_ref, ob_ref, wo_ref, g_ref, wr_ref, br_ref,
                           hn_ref, u_ref, ri_ref, rg_ref, cnt_ref, base_ref):
    i = pl.program_id(0)

    @pl.when(i == 0)
    def _():
        base_ref[...] = jnp.zeros_like(base_ref)

    hn = h_ref[...] + _dot(oa_ref[...], wo_ref[0:512, :]) + _dot(ob_ref[...], wo_ref[512:1024, :])
    hn_ref[...] = hn
    u = _rms(hn, g_ref[...])
    u_ref[...] = _pack_halves(u)
    u_hi = u.astype(BF16)
    u_lo = (u - u_hi.astype(F32)).astype(BF16)
    both = _dot_nt(wr_ref[...], u_hi)
    logits = both[:128] + both[128:] + _dot_nt(wr_ref[0:128, :], u_lo) + br_ref[...]
    tm = logits.shape[1]
    sub = lax.broadcasted_iota(jnp.int32, (8, tm), 0).astype(F32)
    big = 1e6
    isg = sub < N_GROUPS
    gl = jnp.where(isg, logits[N_EXP:N_EXP + 8], NEG)
    gmax = jnp.max(gl, axis=0, keepdims=True)
    gsel = jnp.min(jnp.where(gl == gmax, sub, big), axis=0, keepdims=True)
    g_gate = 1.0 / jnp.sum(jnp.where(isg, jnp.exp(gl - gmax), 0.0), axis=0, keepdims=True)
    el = jnp.zeros((EPG, tm), F32)
    for grp in range(N_GROUPS):
        el = jnp.where(gsel == grp, logits[EPG * grp:EPG * (grp + 1)], el)
    m1 = jnp.max(el, axis=0, keepdims=True)
    i1 = jnp.min(jnp.where(el == m1, sub, big), axis=0, keepdims=True)
    el2 = jnp.where(sub == i1, NEG, el)
    m2 = jnp.max(el2, axis=0, keepdims=True)
    i2 = jnp.min(jnp.where(el2 == m2, sub, big), axis=0, keepdims=True)
    e1 = gsel * EPG + i1
    e2 = gsel * EPG + i2
    r = jnp.exp(m2 - m1)
    gate1 = g_gate / (1.0 + r)
    gate2 = g_gate * r / (1.0 + r)
    rows = lax.broadcasted_iota(jnp.int32, (128, tm), 0).astype(F32)
    oh1 = (rows == e1).astype(F32)
    oh2 = (rows == e2).astype(F32)
    oh = oh1 + oh2
    t_row = lax.broadcasted_iota(jnp.int32, (tm, tm), 0)
    t_col = lax.broadcasted_iota(jnp.int32, (tm, tm), 1)
    earlier = jnp.where(t_row < t_col, 1.0, 0.0).astype(BF16)
    prefix = _dot(oh.astype(BF16), earlier) + base_ref[:, 0:1]
    rank1 = jnp.sum(oh1 * prefix, axis=0, keepdims=True)
    rank2 = jnp.sum(oh2 * prefix, axis=0, keepdims=True)
    base_ref[...] = base_ref[...] + jnp.sum(oh, axis=1, keepdims=True)
    cnt_ref[...] = base_ref[...]
    ri_ref[...] = jnp.where(sub == 0, e1, jnp.where(sub == 1, e2, jnp.where(sub == 2, rank1,
                            jnp.where(sub == 3, rank2, 0.0))))
    rg_ref[...] = jnp.where(rows == 0, gate1, jnp.where(rows == 1, gate2, 0.0)).T


def _outproj_router(h, oa, ob, wo, g, wr, br):
    N = h.shape[0]
    full = lambda a: pl.BlockSpec(a.shape, lambda i: (0,) * a.ndim)
    row = lambda w: pl.BlockSpec((TM, w), lambda i: (i, 0))
    return pl.pallas_call(
        _outproj_router_kernel,
        grid=(N // TM,),
        in_specs=[row(D), row(512), row(512), full(wo), full(g), full(wr), full(br)],
        out_specs=[row(D), row(HALF), pl.BlockSpec((8, TM), lambda i: (0, i)), row(128),
                   pl.BlockSpec((128, 128), lambda i: (0, 0))],
        out_shape=[jax.ShapeDtypeStruct((N, D), F32), jax.ShapeDtypeStruct((N, HALF), jnp.int32),
                   jax.ShapeDtypeStruct((8, N), F32), jax.ShapeDtypeStruct((N, 128), F32),
                   jax.ShapeDtypeStruct((128, 128), F32)],
        scratch_shapes=[pltpu.VMEM((128, 128), F32)],
        compiler_params=_cparams(("arbitrary",)),
        name="outproj_router",
    )(h, oa, ob, wo, g, wr, br)


X_BUFS = 4


def _moe_kernel(te_ref, tv_ref, run_ref, nx_ref, tb_ref, x_hbm, wg_hbm, wu_hbm, wd_hbm, o_ref,
                wg_buf, wu_buf, wd_buf, sems, wgu_s, wd_s, x_buf, x_sems, *, layer):
    i = pl.program_id(0)
    valid = tv_ref[i]

    def weight_copies(e, slot):
        return (pltpu.make_async_copy(wg_hbm.at[layer, e], wg_buf.at[slot], sems.at[slot, 0]),
                pltpu.make_async_copy(wu_hbm.at[layer, e], wu_buf.at[slot], sems.at[slot, 1]),
                pltpu.make_async_copy(wd_hbm.at[layer, e], wd_buf.at[slot], sems.at[slot, 2]))

    @pl.when(run_ref[i] > 0)
    def _():
        slot = run_ref[i] - 1

        @pl.when(i == 0)
        def _():
            for c in weight_copies(te_ref[i], slot):
                c.start()

        for c in weight_copies(te_ref[i], slot):
            c.wait()
        wgu_s[:, :D_EXP] = wg_buf[slot].astype(BF16)
        wgu_s[:, D_EXP:] = wu_buf[slot].astype(BF16)
        wd_s[...] = wd_buf[slot].astype(BF16)

        @pl.when(nx_ref[i] >= 0)
        def _():
            for c in weight_copies(nx_ref[i], 1 - slot):
                c.start()

    n_tiles = pl.num_programs(0)

    def x_copy(j, slot):
        return pltpu.make_async_copy(x_hbm.at[pl.ds(pl.multiple_of(j * TM_E, TM_E), TM_E)], x_buf.at[slot],
                                     x_sems.at[slot])

    def start_tile(j):
        @pl.when((j < n_tiles) & (tv_ref[jnp.minimum(j, n_tiles - 1)] > 0))
        def _():
            x_copy(j, j % X_BUFS).start()

    @pl.when(i == 0)
    def _():
        for j in range(X_BUFS - 1):
            start_tile(i + j)

    start_tile(i + X_BUFS - 1)

    @pl.when(valid > 0)
    def _():
        x_copy(i, i % X_BUFS).wait()
        x_ref = x_buf.at[i % X_BUFS]
        rows = lax.broadcasted_iota(jnp.int32, x_ref.shape, 0)
        lo, hi = _unpack_halves(jnp.where(rows < valid, x_ref[...], 0))
        hgu = _dot(lo.astype(BF16), wgu_s[:HALF, :]) + _dot(hi.astype(BF16), wgu_s[HALF:, :])
        hg, hu = hgu[:, :D_EXP], hgu[:, D_EXP:]
        a = hg * (1.0 / (1.0 + jnp.exp(-hg))) * hu
        o_ref[...] = _pack_halves(_dot(a.astype(BF16), wd_s[...]))


def _moe_experts(xs, plan, w_gate, w_up, w_down, layer):
    P = xs.shape[0]
    hbm = pl.BlockSpec(memory_space=pl.ANY)
    grid_spec = pltpu.PrefetchScalarGridSpec(
        num_scalar_prefetch=5,
        grid=(P // TM_E,),
        in_specs=[hbm, hbm, hbm, hbm],
        out_specs=pl.BlockSpec((TM_E, HALF), lambda i, *plan: (plan[4][i], 0)),
        scratch_shapes=[pltpu.VMEM((2, D, D_EXP), F32), pltpu.VMEM((2, D, D_EXP), F32),
                        pltpu.VMEM((2, D_EXP, D), F32), pltpu.SemaphoreType.DMA((2, 3)),
                        pltpu.VMEM((D, 2 * D_EXP), BF16), pltpu.VMEM((D_EXP, D), BF16),
                        pltpu.VMEM((X_BUFS, TM_E, HALF), jnp.int32), pltpu.SemaphoreType.DMA((X_BUFS,))],
    )
    return pl.pallas_call(
        functools.partial(_moe_kernel, layer=layer),
        grid_spec=grid_spec,
        out_shape=jax.ShapeDtypeStruct((P, HALF), jnp.int32),
        compiler_params=_cparams(("arbitrary",)),
        name="moe_experts",
    )(*plan, xs, w_gate, w_up, w_down)


def _sc_mesh():
    return plsc.VectorSubcoreMesh(core_axis_name="c", subcore_axis_name="s",
                                  num_cores=SC_CORES, num_subcores=SC_SUBCORES)


def _sc_scatter_rows(src, d1, d2, P):
    N, W = src.shape
    per_w = N // (SC_CORES * SC_SUBCORES)

    @functools.partial(pl.kernel, mesh=_sc_mesh(), out_type=jax.ShapeDtypeStruct((P, W), src.dtype),
                       scratch_types=[pltpu.VMEM((SC_CHUNK,), jnp.int32), pltpu.VMEM((SC_CHUNK, W), src.dtype)],
                       name="sc_scatter_rows")
    def k(src_hbm, d1_hbm, d2_hbm, out_hbm, idx_v, rows_v):
        wid = lax.axis_index("s") * SC_CORES + lax.axis_index("c")

        @pl.loop(0, per_w // SC_CHUNK)
        def _(c):
            off = wid * per_w + c * SC_CHUNK
            pltpu.sync_copy(src_hbm.at[pl.ds(off, SC_CHUNK)], rows_v)
            pltpu.sync_copy(d1_hbm.at[pl.ds(off, SC_CHUNK)], idx_v)
            pltpu.sync_copy(rows_v, out_hbm.at[idx_v])
            pltpu.sync_copy(d2_hbm.at[pl.ds(off, SC_CHUNK)], idx_v)
            pltpu.sync_copy(rows_v, out_hbm.at[idx_v])

    return k(src, d1, d2)


def _sc_gather_rows(table, idx):
    B, W = idx.shape[0], table.shape[1]
    per_w = B // (SC_CORES * SC_SUBCORES)

    @functools.partial(pl.kernel, mesh=_sc_mesh(), out_type=jax.ShapeDtypeStruct((B, W), table.dtype),
                       scratch_types=[pltpu.VMEM((SC_CHUNK,), jnp.int32), pltpu.VMEM((SC_CHUNK, W), table.dtype)],
                       name="sc_gather_rows")
    def k(table_hbm, idx_hbm, out_hbm, idx_v, rows_v):
        wid = lax.axis_index("s") * SC_CORES + lax.axis_index("c")

        @pl.loop(0, per_w // SC_CHUNK)
        def _(c):
            off = wid * per_w + c * SC_CHUNK
            pltpu.sync_copy(idx_hbm.at[pl.ds(off, SC_CHUNK)], idx_v)
            pltpu.sync_copy(table_hbm.at[idx_v], rows_v)
            pltpu.sync_copy(rows_v, out_hbm.at[pl.ds(off, SC_CHUNK)])

    return k(table, idx)


def _dest_kernel(offs_ref, ri_ref, d_ref):
    ri = ri_ref[...]
    base = jnp.zeros(ri.shape, F32)
    for e in range(N_EXP):
        base = jnp.where(ri == e, offs_ref[e].astype(F32), base)
    d_ref[...] = (base + pltpu.roll(ri, 6, 0)).astype(jnp.int32)


def _dispatch_plan(ri, cnt, N):
    counts = cnt[:N_EXP, 0].astype(jnp.int32)
    padded = ((counts + TM_E - 1) // TM_E) * TM_E
    ends = jnp.cumsum(padded)
    offs = ends - padded
    take = lambda t, i: t.at[i].get(mode="promise_in_bounds")
    dest = pl.pallas_call(
        _dest_kernel,
        in_specs=[pl.BlockSpec(memory_space=pltpu.SMEM), pl.BlockSpec((8, N), lambda: (0, 0))],
        out_specs=pl.BlockSpec((8, N), lambda: (0, 0)),
        out_shape=jax.ShapeDtypeStruct((8, N), jnp.int32),
        name="dispatch_rows",
    )(offs, ri)
    d1, d2 = dest[0], dest[1]
    P = 2 * N + N_EXP * TM_E
    tile_start = jnp.arange(P // TM_E, dtype=jnp.int32) * TM_E
    tile_expert = jnp.minimum(jnp.sum((ends[None, :] <= tile_start[:, None]).astype(jnp.int32), axis=1), N_EXP - 1)
    tile_valid = jnp.clip(take(offs + counts, tile_expert) - tile_start, 0, TM_E)
    n_tiles = P // TM_E
    tile = jnp.arange(n_tiles, dtype=jnp.int32)
    starts = (tile_valid > 0) & ((tile == 0) | (tile_expert != jnp.roll(tile_expert, 1)))
    slot = (jnp.cumsum(starts.astype(jnp.int32)) - 1) % 2
    run_code = jnp.where(starts, slot + 1, 0).astype(jnp.int32)
    start_at = jnp.where(starts, tile, n_tiles)
    next_start = jnp.concatenate([lax.cummin(start_at, reverse=True)[1:], jnp.full((1,), n_tiles, jnp.int32)])
    next_expert = jnp.where(next_start < n_tiles, take(tile_expert, jnp.minimum(next_start, n_tiles - 1)), -1)
    tile_block = jnp.minimum(tile, ends[N_EXP - 1] // TM_E - 1)
    return d1, d2, P, (tile_expert, tile_valid, run_code, next_expert.astype(jnp.int32), tile_block)


def _combine(h_ref, y1_ref, y2_ref, rg_ref):
    rg = rg_ref[...]
    lo1, hi1 = _unpack_halves(y1_ref[...])
    lo2, hi2 = _unpack_halves(y2_ref[...])
    g1, g2 = rg[:, 0:1], rg[:, 1:2]
    return h_ref[...] + jnp.concatenate([g1 * lo1 + g2 * lo2, g1 * hi1 + g2 * hi2], axis=1)


ODD_WIDTHS = (1024, 1024, 512, 512)


def _combine_proj_odd_kernel(h_ref, y1_ref, y2_ref, rg_ref, g_ref, win_ref, wvt_ref, ind_ref, hn_ref, vt_ref,
                             vdt_ref, *out_refs):
    hn = _combine(h_ref, y1_ref, y2_ref, rg_ref)
    hn_ref[...] = hn
    u = _rms(hn, g_ref[...]).astype(BF16)
    vt_ref[...] = _values_t(_dot_nt(wvt_ref[0:512, :], u)).astype(BF16)
    vdt_ref[...] = _dot_nt(wvt_ref[512:1024, :], u).astype(BF16)
    qc_ref, kc_ref, qd_ref, kd_ref = out_refs
    qc = _spread_heads(_dot(u, win_ref[:, 0:512]), lambda h: 0)
    kc = _spread_heads(_dot(u, win_ref[:, 512:1024]), lambda h: 0)
    for hh in range(8):
        qc_ref[:, 128 * hh:128 * (hh + 1)] = qc[hh].astype(BF16)
        kc_ref[:, 128 * hh:128 * (hh + 1)] = (kc[hh] + ind_ref[...]).astype(BF16)
    qd_ref[...] = _dot(u, win_ref[:, 1024:1536]).astype(BF16)
    kd_ref[...] = _dot(u, win_ref[:, 1536:2048]).astype(BF16)


def _odd_weights_kernel(w_ref, win_ref, wvt_ref):
    win_ref[:, 0:512] = (w_ref[:, 0:512] * (HD ** -0.5 * LOG2E)).astype(BF16)
    win_ref[:, 512:1024] = w_ref[:, 512:1024].astype(BF16)
    wvt_ref[...] = w_ref[:, 1024:1536].T.astype(BF16)


def _odd_weights(w_in):
    return pl.pallas_call(
        _odd_weights_kernel,
        grid=(2,),
        in_specs=[pl.BlockSpec((D, 1536), lambda i: (0, i))],
        out_specs=[pl.BlockSpec((D, 1024), lambda i: (0, i)), pl.BlockSpec((512, D), lambda i: (i, 0))],
        out_shape=[jax.ShapeDtypeStruct((D, 2048), BF16), jax.ShapeDtypeStruct((1024, D), BF16)],
        compiler_params=_cparams(("arbitrary",)),
        name="odd_weights",
    )(w_in)


def _block_indicator(T):
    assert T // MOBA_L <= 64
    ind = np.zeros((T, 128), np.float32)
    ind[np.arange(T), 64 + np.arange(T) // MOBA_L] = 1.0
    return jnp.asarray(ind)


def _y_specs(N):
    return [pl.BlockSpec((TM, HALF), lambda i: (i, 0)), pl.BlockSpec((TM, HALF), lambda i: (i + N // TM, 0))]


def _combine_proj_odd(h, y12, rg, g, win, wvt, ind, T):
    N = h.shape[0]
    nt = T // TM
    full = lambda a: pl.BlockSpec(a.shape, lambda i: (0,) * a.ndim)
    row = lambda w: pl.BlockSpec((TM, w), lambda i: (i, 0))
    return pl.pallas_call(
        _combine_proj_odd_kernel,
        grid=(N // TM,),
        in_specs=[row(D)] + _y_specs(N) + [row(128), full(g), full(win), full(wvt),
                  pl.BlockSpec((TM, 128), lambda i: (i % nt, 0))],
        out_specs=[row(D), pl.BlockSpec((None, 8 * VT_ROWS, TM), lambda i: (i // nt, 0, i % nt)),
                   pl.BlockSpec((None, 512, TM), lambda i: (i // nt, 0, i % nt))]
        + [row(w) for w in ODD_WIDTHS],
        out_shape=[jax.ShapeDtypeStruct((N, D), F32), jax.ShapeDtypeStruct((N // T, 8 * VT_ROWS, T), BF16),
                   jax.ShapeDtypeStruct((N // T, 512, T), BF16)]
        + [jax.ShapeDtypeStruct((N, w), BF16) for w in ODD_WIDTHS],
        compiler_params=_cparams(("arbitrary",)),
        name="combine_proj_odd",
    )(h, y12, y12, rg, g, win, wvt, ind)


def _combine_final_kernel(h_ref, y1_ref, y2_ref, rg_ref, g_ref, o_ref):
    o_ref[...] = _rms(_combine(h_ref, y1_ref, y2_ref, rg_ref), g_ref[...])


def _combine_final(h, y12, rg, g):
    N = h.shape[0]
    row = lambda w: pl.BlockSpec((TM, w), lambda i: (i, 0))
    return pl.pallas_call(
        _combine_final_kernel,
        grid=(N // TM,),
        in_specs=[row(D)] + _y_specs(N) + [row(128), pl.BlockSpec((1, D), lambda i: (0, 0))],
        out_specs=row(D),
        out_shape=jax.ShapeDtypeStruct((N, D), F32),
        compiler_params=_cparams(("arbitrary",)),
        name="combine_final",
    )(h, y12, y12, rg, g)


def _moba_kernel(q_ref, k_ref, vt_ref, tb_ref, o_ref, kmean_ref, ksplit_ref, qa_ref, s_ref, smax_ref, m_ref, acc_ref,
                 *, nkb, nq):
    ia = pl.program_id(2)
    L, tq = MOBA_L, MOBA_TQ

    @pl.when(ia == 0)
    def _():
        kmean_ref[...] = jnp.zeros_like(kmean_ref)
        for n in range(nkb):
            kmean_ref[64 + n:65 + n, :] = jnp.sum(k_ref[n * L:(n + 1) * L, :].astype(F32), axis=0,
                                                  keepdims=True) * (1.0 / L)
        hi = kmean_ref[...].astype(BF16)
        ksplit_ref[0] = hi
        ksplit_ref[1] = (kmean_ref[...] - hi.astype(F32)).astype(BF16)

    nb = -(-nkb // 8) * 8
    lane = lax.broadcasted_iota(jnp.int32, (tq, 128), 1)
    blk = lax.broadcasted_iota(jnp.int32, (nb, tq), 0)
    half = jnp.where(lax.broadcasted_iota(jnp.int32, (nb, tq), 1) >= L, 1, 0)
    blk_f = blk.astype(F32)
    for t, tile in enumerate((ia, nq - 1 - ia)):
        own = 2 * tile + half
        for j in range(2):
            qj = q_ref[pl.ds(pl.multiple_of(tile * tq, tq), tq), 128 * j:128 * (j + 1)]
            gsc = (_dot_nt(ksplit_ref[0, 64:64 + nb, 128 * j:128 * (j + 1)], qj)
                   + _dot_nt(ksplit_ref[1, 64:64 + nb, 128 * j:128 * (j + 1)], qj))
            g = jnp.where(blk < own, gsc, NEG)
            allowed = jnp.where(blk == own, 1.0, 0.0)
            for _ in range(MOBA_TOPK):
                mx = jnp.max(g, axis=0, keepdims=True)
                idx = jnp.min(jnp.where(g == mx, blk_f, 1e6), axis=0, keepdims=True)
                pick = blk_f == idx
                allowed = jnp.maximum(allowed, jnp.where(pick, jnp.where(mx > 0.5 * NEG, 1.0, 0.0), 0.0))
                g = jnp.where(pick, 2.0 * NEG, g)
            mask = jnp.concatenate([jnp.zeros((64, tq), F32), jnp.where(allowed > 0.5, 0.0, NEG),
                                    jnp.zeros((64 - nb, tq), F32)], axis=0)
            qa_ref[t, j] = jnp.where(lane < HD, qj, mask.T.astype(BF16))

    def bias_of(start, q0, j):
        parts = []
        for c in range(tq // L):
            off = jnp.minimum(q0 - start - c * L, BIAS_CONST_FROM) + BIAS_PAD
            parts.append(tb_ref[j, :, pl.ds(pl.multiple_of(off, L), tq)])
        return jnp.concatenate(parts, axis=0)

    _causal_attention_pair_t(ia, nq, lambda t, q0, j: qa_ref[t, j], k_ref, vt_ref, o_ref, s_ref, smax_ref, m_ref,
                             acc_ref, tq=tq, bias_of=bias_of)


def _moba_attention(q, k, vt, tb, B, T):
    tq = MOBA_TQ
    nq = T // tq
    return pl.pallas_call(
        functools.partial(_moba_kernel, nkb=T // MOBA_L, nq=nq),
        grid=(B, 4, nq // 2),
        in_specs=[pl.BlockSpec((T, 256), lambda b, p, i: (b, p)),
                  pl.BlockSpec((T, 256), lambda b, p, i: (b, p)),
                  pl.BlockSpec((None, 2 * VT_ROWS, T), lambda b, p, i: (b, p, 0)),
                  pl.BlockSpec((2, MOBA_L, BIAS_ROWS), lambda b, p, i: (p, 0, 0))],
        out_specs=pl.BlockSpec((T, 128), lambda b, p, i: (b, p)),
        out_shape=jax.ShapeDtypeStruct((B * T, 512), BF16),
        scratch_shapes=[pltpu.VMEM((128, 256), F32), pltpu.VMEM((2, 128, 256), BF16),
                        pltpu.VMEM((2, 2, tq, 128), BF16)]
        + _attention_t_scratch(tq),
        compiler_params=_cparams(("arbitrary", "arbitrary", "arbitrary")),
        name="moba_attention",
    )(q, k, vt, tb)


def _sb_kernel(q_ref, k_ref, vt_ref, o_ref, *, tq, pairs):
    i = pl.program_id(2)
    lane = lax.broadcasted_iota(jnp.int32, (tq, 128), 1)
    qs = []
    for p in range(pairs):
        q_pair = q_ref[:, 128 * p:128 * (p + 1)]
        qs += [jnp.where(lane < HD, q_pair, 0).astype(BF16), jnp.where(lane >= HD, q_pair, 0).astype(BF16)]
    key = lax.broadcasted_iota(jnp.int32, (tq, tq), 0)
    qry = lax.broadcasted_iota(jnp.int32, (tq, tq), 1)
    past = key < qry
    suffix = jnp.where(qry >= key, -1.0, 0.0).astype(BF16)
    heads = range(2 * pairs)

    def step(kb, carry, boundary):
        start = pl.multiple_of(kb * tq, tq)
        zs = [_dot_nt(k_ref[pl.ds(start, tq), 128 * (h // 2):128 * (h // 2 + 1)], qs[h]) for h in heads]
        parts = []
        for z in zs:
            neg_abs = pltpu.bitcast(pltpu.bitcast(z, jnp.uint32) | jnp.uint32(0x80000000), F32)
            soft = jnp.maximum(z, 0.0) + jnp.log2(1.0 + jnp.exp2(neg_abs))
            if boundary:
                soft = jnp.where(past, soft, 0.0)
            parts.append(soft.astype(BF16))
        sums = [_dot(suffix, part) for part in parts]
        ws = []
        for h in heads:
            w = jnp.exp2(zs[h] + (carry[h][0] + sums[h]))
            ws.append((jnp.where(past, w, 0.0) if boundary else w).astype(BF16))
        return tuple((carry[h][0] + sums[h][0:1], carry[h][1]
                      + _dot(vt_ref[128 * (h // 2):128 * (h // 2 + 1), pl.ds(start, tq)], ws[h])) for h in heads)

    def live(carry):
        top = carry[0][0]
        for c, _ in carry[1:]:
            top = jnp.maximum(top, c)
        return (jnp.max(top) > SB_DONE).astype(jnp.int32)

    def body(state):
        t, _, carry = state
        carry = step(i - 1 - t, carry, False)
        return t + 1, live(carry), carry

    init = tuple((jnp.zeros((1, tq), F32), jnp.zeros((128, tq), F32)) for _ in range(2 * pairs))
    carry = step(i, init, True)
    _, _, carry = lax.while_loop(lambda st: (st[0] < i) & (st[1] > 0), body, (jnp.int32(0), live(carry), carry))
    rows = lax.broadcasted_iota(jnp.int32, (128, tq), 0)
    for p in range(pairs):
        ot = jnp.where(rows < HD, carry[2 * p][1], carry[2 * p + 1][1])
        o_ref[:, 128 * p:128 * (p + 1)] = ot.T.astype(BF16)


def _sb_attention(q, k, vt, B, T, tq=256, pairs=4):
    nq = T // tq
    w = 128 * pairs
    return pl.pallas_call(
        functools.partial(_sb_kernel, tq=tq, pairs=pairs),
        grid=(B, 4 // pairs, nq),
        in_specs=[pl.BlockSpec((tq, w), lambda b, p, i: (b * nq + i, p)),
                  pl.BlockSpec((T, w), lambda b, p, i: (b, p)),
                  pl.BlockSpec((None, w, T), lambda b, p, i: (b, p, 0))],
        out_specs=pl.BlockSpec((tq, w), lambda b, p, i: (b * nq + i, p)),
        out_shape=jax.ShapeDtypeStruct((B * T, 512), BF16),
        compiler_params=_cparams(("arbitrary", "arbitrary", "arbitrary")),
        name="sb_attention",
    )(q, k, vt)


def _router_weights(w_group, b_group, w_router, b_router):
    pad = 128 - N_EXP - N_GROUPS
    wr = jnp.concatenate([w_router, w_group, jnp.zeros((D, pad), F32)], axis=1).T
    hi = wr.astype(BF16)
    lo = (wr - hi.astype(F32)).astype(BF16)
    br = jnp.concatenate([b_router, b_group, jnp.zeros((pad,), F32)])[:, None]
    return jnp.concatenate([hi, lo], axis=0), br


def _ffn(h, oa, ob, wo, g, w_group, b_group, w_router, b_router, w_gate, w_up, w_down, layer):
    wr, br = _router_weights(w_group, b_group, w_router, b_router)
    hn, u, ri, rg, cnt = _outproj_router(h, oa, ob, wo.astype(BF16), g[None, :], wr, br)
    d1, d2, P, plan = _dispatch_plan(ri, cnt, h.shape[0])
    xs = _sc_scatter_rows(u, d1, d2, P)
    ys = _moe_experts(xs, plan, w_gate, w_up, w_down, layer)
    return hn, _sc_gather_rows(ys, jnp.concatenate([d1, d2])), rg


def kernel(x, norm_mix, norm_ffn, norm_final, rel_bias_table, w_in_even, g_mla_q, g_mla_kv, w_mla_qb, w_mla_kvb, swa_sinks, w_out_even, w_in_odd, w_out_odd, moe_w_group, moe_b_group, moe_w_router, moe_b_router, moe_w_gate, moe_w_up, moe_w_down):
    B, T, _ = x.shape
    h = x.reshape(B * T, D)
    tb_swa, tb_moba = _bias_tables(rel_bias_table)

    win, wqa, wqb, wk, wv = _even_weights(w_in_even[0], w_mla_qb[0], w_mla_kvb[0])
    q, k, v, qs8, ks, vst = _proj_even(h, norm_mix[0][None, :], win, g_mla_q[0][None, :], g_mla_kv[0][None, :],
                                       wqa, wqb, wk, wv, _rope_tables(T), T)
    oa = _mla_attention(q, k, v, B, T)
    ob = _swa_attention(qs8, ks, vst, swa_sinks[0], tb_swa, B, T)
    h, y12, rg = _ffn(h, oa, ob, w_out_even[0], norm_ffn[0], moe_w_group[0], moe_b_group[0], moe_w_router[0],
                      moe_b_router[0], moe_w_gate, moe_w_up, moe_w_down, 0)

    h, vct, vdt, qc, kc, qd, kd = _combine_proj_odd(h, y12, rg, norm_mix[1][None, :], *_odd_weights(w_in_odd[0]),
                                                   _block_indicator(T), T)
    oc = _moba_attention(qc, kc, vct, tb_moba, B, T)
    od = _sb_attention(qd, kd, vdt, B, T)
    h, y12, rg = _ffn(h, oc, od, w_out_odd[0], norm_ffn[1], moe_w_group[1], moe_b_group[1], moe_w_router[1],
                      moe_b_router[1], moe_w_gate, moe_w_up, moe_w_down, 1)
    out = _combine_final(h, y12, rg, norm_final[None, :])
    return out.reshape(B, T, D)
```
